```python
import jax, jax.numpy as jnp
from jax import lax
import numpy as np

D_MODEL = 1024
BATCH = 8
SEQ = 8192
DEPTH = 2

N_MEM = 256
D_MIX = D_MODEL
POOL_WIDTH = D_MIX // 2
POOL_WINDOWS = (2, 4, 8, 16)
POOL_GROUPS = len(POOL_WINDOWS)
POOL_GROUP_DIM = POOL_WIDTH // POOL_GROUPS
SGU_WIDTH = D_MIX - POOL_WIDTH
SGU_HEADS = 4
SGU_HEAD_DIM = SGU_WIDTH // SGU_HEADS
CHUNK = 128
D_IN_PROJ = POOL_WIDTH + 2 * SGU_WIDTH
XATTN_HEADS = 4
XATTN_HEAD_DIM = D_MODEL // XATTN_HEADS
D_FF = 2816
CONV_WIDTH = 3
EPS = 1e-6

kernel_name = "hybrid_pool_sgu_memxattn_convffn"


def rmsnorm(x, g):
    xf = x.astype(jnp.float32)
    y = xf * lax.rsqrt(jnp.mean(xf * xf, axis=-1, keepdims=True) + EPS)
    return (y * g.astype(jnp.float32)).astype(x.dtype)


def layernorm_nobias(x, g):
    xf = x.astype(jnp.float32)
    mu = jnp.mean(xf, axis=-1, keepdims=True)
    xc = xf - mu
    y = xc * lax.rsqrt(jnp.mean(xc * xc, axis=-1, keepdims=True) + EPS)
    return (y * g.astype(jnp.float32)).astype(x.dtype)


def pool_mixer(p, pool_w, pool_scale):
    B, S, _ = p.shape
    pf = p.astype(jnp.float32)
    c = jnp.pad(jnp.cumsum(pf, axis=1), ((0, 0), (1, 0), (0, 0)))
    t = jnp.arange(S)
    diffs = []
    for gi, win in enumerate(POOL_WINDOWS):
        sl = slice(gi * POOL_GROUP_DIM, (gi + 1) * POOL_GROUP_DIM)
        cg = c[..., sl]
        prev = jnp.pad(cg, ((0, 0), (win - 1, 0), (0, 0)))[:, :S]
        count = jnp.minimum(t + 1, win).astype(jnp.float32)[None, :, None]
        diffs.append((cg[:, 1:] - prev) / count - pf[..., sl])
    d = jnp.stack(diffs, axis=2).astype(p.dtype)
    y = jnp.einsum('bsgc,gcd->bsgd', d, pool_w).reshape(B, S, POOL_WIDTH)
    return y * pool_scale


def sgu_mixer(u, v, sgu_g, sgu_w, sgu_b):
    B, S, _ = u.shape
    vn = layernorm_nobias(v, sgu_g)
    vc = vn.reshape(B, S // CHUNK, CHUNK, SGU_HEADS, SGU_HEAD_DIM)
    mask = jnp.tril(jnp.ones((CHUNK, CHUNK), dtype=bool))
    w_masked = jnp.where(mask[None], sgu_w, jnp.zeros_like(sgu_w))
    z = jnp.einsum('hts,bnshd->bnthd', w_masked, vc) + sgu_b.T[:, :, None]
    return u * z.reshape(B, S, SGU_WIDTH)


def mem_cross_attention(xn, mem, mem_g, wq, wk, wv, wo):
    B, S, _ = xn.shape
    memn = rmsnorm(mem, mem_g)
    q = (xn @ wq).reshape(B, S, XATTN_HEADS, XATTN_HEAD_DIM)
    k = (memn @ wk).reshape(B, N_MEM, XATTN_HEADS, XATTN_HEAD_DIM)
    v = (memn @ wv).reshape(B, N_MEM, XATTN_HEADS, XATTN_HEAD_DIM)
    s = jnp.einsum('bshd,bmhd->bhsm', q, k).astype(jnp.float32) * (XATTN_HEAD_DIM ** -0.5)
    pr = jax.nn.softmax(s, axis=-1).astype(v.dtype)
    o = jnp.einsum('bhsm,bmhd->bshd', pr, v).reshape(B, S, D_MODEL)
    return o @ wo


def conv_ffn(xn, w_up, conv_w, conv_b, w_down):
    S = xn.shape[1]
    h = xn @ w_up
    hp = jnp.pad(h, ((0, 0), (CONV_WIDTH - 1, 0), (0, 0)))
    hc = conv_b + sum(conv_w[k] * hp[:, k:k + S] for k in range(CONV_WIDTH))
    gate, val = jnp.split(hc, 2, axis=-1)
    return (jax.nn.silu(gate) * val) @ w_down


def _fwd_setup_inputs(seed: int = 0) -> dict:
    key = jax.random.key(seed)
    ks = jax.random.split(key, 24)
    f32 = jnp.float32
    n = lambda k, shape, s: (jax.random.normal(k, shape, f32) * s)
    gain = lambda k, shape: 1.0 + 0.05 * jax.random.normal(k, shape, f32)
    L = DEPTH
    return {
        "x": jax.random.normal(ks[0], (BATCH, SEQ, D_MODEL), f32),
        "mem": jax.random.normal(ks[1], (BATCH, N_MEM, D_MODEL), f32),
        "norm_mix_g": gain(ks[2], (L, D_MODEL)),
        "w_in": n(ks[3], (L, D_MODEL, D_IN_PROJ), D_MODEL ** -0.5),
        "pool_w": n(ks[4], (L, POOL_GROUPS, POOL_GROUP_DIM, POOL_GROUP_DIM), POOL_GROUP_DIM ** -0.5),
        "pool_scale": 1.0 + 0.1 * jax.random.normal(ks[5], (L, POOL_WIDTH), f32),
        "sgu_g": gain(ks[6], (L, SGU_WIDTH)),
        "sgu_w": n(ks[7], (L, SGU_HEADS, CHUNK, CHUNK), CHUNK ** -0.5),
        "sgu_b": 1.0 + 0.05 * jax.random.normal(ks[8], (L, SGU_HEADS, CHUNK), f32),
        "w_out": n(ks[9], (L, D_MIX, D_MODEL), D_MIX ** -0.5),
        "norm_xattn_g": gain(ks[10], (L, D_MODEL)),
        "mem_norm_g": gain(ks[11], (L, D_MODEL)),
        "wq": n(ks[12], (L, D_MODEL, D_MODEL), D_MODEL ** -0.5),
        "wk": n(ks[13], (L, D_MODEL, D_MODEL), D_MODEL ** -0.5),
        "wv": n(ks[14], (L, D_MODEL, D_MODEL), D_MODEL ** -0.5),
        "wo": n(ks[15], (L, D_MODEL, D_MODEL), D_MODEL ** -0.5),
        "norm_ffn_g": gain(ks[16], (L, D_MODEL)),
        "w_up": n(ks[17], (L, D_MODEL, 2 * D_FF), D_MODEL ** -0.5),
        "conv_w": n(ks[18], (L, CONV_WIDTH, 2 * D_FF), CONV_WIDTH ** -0.5),
        "conv_b": n(ks[19], (L, 2 * D_FF), 0.02),
        "w_down": n(ks[20], (L, D_FF, D_MODEL), D_FF ** -0.5),
        "final_norm_g": gain(ks[21], (D_MODEL,)),
    }


def _fwd_reference(x, mem, norm_mix_g, w_in, pool_w, pool_scale, sgu_g, sgu_w, sgu_b, w_out,
              norm_xattn_g, mem_norm_g, wq, wk, wv, wo,
              norm_ffn_g, w_up, conv_w, conv_b, w_down, final_norm_g):
    h = x
    for l in range(DEPTH):
        xn = rmsnorm(h, norm_mix_g[l])
        proj = xn @ w_in[l]
        p = proj[..., :POOL_WIDTH]
        uv = jax.nn.gelu(proj[..., POOL_WIDTH:], approximate=False)
        u, v = uv[..., :SGU_WIDTH], uv[..., SGU_WIDTH:]
        y_pool = pool_mixer(p, pool_w[l], pool_scale[l])
        y_sgu = sgu_mixer(u, v, sgu_g[l], sgu_w[l], sgu_b[l])
        h = h + jnp.concatenate([y_pool, y_sgu], axis=-1) @ w_out[l]
        xn = rmsnorm(h, norm_xattn_g[l])
        h = h + mem_cross_attention(xn, mem, mem_norm_g[l], wq[l], wk[l], wv[l], wo[l])
        xn = rmsnorm(h, norm_ffn_g[l])
        h = h + conv_ffn(xn, w_up[l], conv_w[l], conv_b[l], w_down[l])
    return rmsnorm(h, final_norm_g)


import jax as _jax
import jax.numpy as _jnp

TWIN_FORMAT = 'train_step'
FWD_PARAMS = ['x', 'mem', 'norm_mix_g', 'w_in', 'pool_w', 'pool_scale', 'sgu_g', 'sgu_w', 'sgu_b', 'w_out', 'norm_xattn_g', 'mem_norm_g', 'wq', 'wk', 'wv', 'wo', 'norm_ffn_g', 'w_up', 'conv_w', 'conv_b', 'w_down', 'final_norm_g']
TWIN_WEIGHTS = ['norm_mix_g', 'w_in', 'pool_w', 'pool_scale', 'sgu_g', 'sgu_w', 'sgu_b', 'w_out', 'norm_xattn_g', 'mem_norm_g', 'wq', 'wk', 'wv', 'wo', 'norm_ffn_g', 'w_up', 'conv_w', 'conv_b', 'w_down', 'final_norm_g']
TWIN_DIFF_INPUT = 'x'
TWIN_INPUTS = ['x', 'mem', 'norm_mix_g', 'w_in', 'pool_w', 'pool_scale', 'sgu_g', 'sgu_w', 'sgu_b', 'w_out', 'norm_xattn_g', 'mem_norm_g', 'wq', 'wk', 'wv', 'wo', 'norm_ffn_g', 'w_up', 'conv_w', 'conv_b', 'w_down', 'final_norm_g', 'loss_target', 'm_norm_mix_g', 'm_w_in', 'm_pool_w', 'm_pool_scale', 'm_sgu_g', 'm_sgu_w', 'm_sgu_b', 'm_w_out', 'm_norm_xattn_g', 'm_mem_norm_g', 'm_wq', 'm_wk', 'm_wv', 'm_wo', 'm_norm_ffn_g', 'm_w_up', 'm_conv_w', 'm_conv_b', 'm_w_down', 'm_final_norm_g', 'v_norm_mix_g', 'v_w_in', 'v_pool_w', 'v_pool_scale', 'v_sgu_g', 'v_sgu_w', 'v_sgu_b', 'v_w_out', 'v_norm_xattn_g', 'v_mem_norm_g', 'v_wq', 'v_wk', 'v_wv', 'v_wo', 'v_norm_ffn_g', 'v_w_up', 'v_conv_w', 'v_conv_b', 'v_w_down', 'v_final_norm_g']
TWIN_OUTPUTS = ['loss', 'grad_x', 'grad_norm_mix_g', 'grad_w_in', 'grad_pool_w', 'grad_pool_scale', 'grad_sgu_g', 'grad_sgu_w', 'grad_sgu_b', 'grad_w_out', 'grad_norm_xattn_g', 'grad_mem_norm_g', 'grad_wq', 'grad_wk', 'grad_wv', 'grad_wo', 'grad_norm_ffn_g', 'grad_w_up', 'grad_conv_w', 'grad_conv_b', 'grad_w_down', 'grad_final_norm_g', 'delta_norm_mix_g', 'delta_w_in', 'delta_pool_w', 'delta_pool_scale', 'delta_sgu_g', 'delta_sgu_w', 'delta_sgu_b', 'delta_w_out', 'delta_norm_xattn_g', 'delta_mem_norm_g', 'delta_wq', 'delta_wk', 'delta_wv', 'delta_wo', 'delta_norm_ffn_g', 'delta_w_up', 'delta_conv_w', 'delta_conv_b', 'delta_w_down', 'delta_final_norm_g', 'new_m_norm_mix_g', 'new_m_w_in', 'new_m_pool_w', 'new_m_pool_scale', 'new_m_sgu_g', 'new_m_sgu_w', 'new_m_sgu_b', 'new_m_w_out', 'new_m_norm_xattn_g', 'new_m_mem_norm_g', 'new_m_wq', 'new_m_wk', 'new_m_wv', 'new_m_wo', 'new_m_norm_ffn_g', 'new_m_w_up', 'new_m_conv_w', 'new_m_conv_b', 'new_m_w_down', 'new_m_final_norm_g', 'new_v_norm_mix_g', 'new_v_w_in', 'new_v_pool_w', 'new_v_pool_scale', 'new_v_sgu_g', 'new_v_sgu_w', 'new_v_sgu_b', 'new_v_w_out', 'new_v_norm_xattn_g', 'new_v_mem_norm_g', 'new_v_wq', 'new_v_wk', 'new_v_wv', 'new_v_wo', 'new_v_norm_ffn_g', 'new_v_w_up', 'new_v_conv_w', 'new_v_conv_b', 'new_v_w_down', 'new_v_final_norm_g']
TWIN_LEAF_KINDS = {'loss': 'loss', 'grad_x': 'grad_x', 'grad_norm_mix_g': 'grad_w', 'grad_w_in': 'grad_w', 'grad_pool_w': 'grad_w', 'grad_pool_scale': 'grad_w', 'grad_sgu_g': 'grad_w', 'grad_sgu_w': 'grad_w', 'grad_sgu_b': 'grad_w', 'grad_w_out': 'grad_w', 'grad_norm_xattn_g': 'grad_w', 'grad_mem_norm_g': 'grad_w', 'grad_wq': 'grad_w', 'grad_wk': 'grad_w', 'grad_wv': 'grad_w', 'grad_wo': 'grad_w', 'grad_norm_ffn_g': 'grad_w', 'grad_w_up': 'grad_w', 'grad_conv_w': 'grad_w', 'grad_conv_b': 'grad_w', 'grad_w_down': 'grad_w', 'grad_final_norm_g': 'grad_w', 'delta_norm_mix_g': 'delta_w', 'delta_w_in': 'delta_w', 'delta_pool_w': 'delta_w', 'delta_pool_scale': 'delta_w', 'delta_sgu_g': 'delta_w', 'delta_sgu_w': 'delta_w', 'delta_sgu_b': 'delta_w', 'delta_w_out': 'delta_w', 'delta_norm_xattn_g': 'delta_w', 'delta_mem_norm_g': 'delta_w', 'delta_wq': 'delta_w', 'delta_wk': 'delta_w', 'delta_wv': 'delta_w', 'delta_wo': 'delta_w', 'delta_norm_ffn_g': 'delta_w', 'delta_w_up': 'delta_w', 'delta_conv_w': 'delta_w', 'delta_conv_b': 'delta_w', 'delta_w_down': 'delta_w', 'delta_final_norm_g': 'delta_w', 'new_m_norm_mix_g': 'new_m', 'new_m_w_in': 'new_m', 'new_m_pool_w': 'new_m', 'new_m_pool_scale': 'new_m', 'new_m_sgu_g': 'new_m', 'new_m_sgu_w': 'new_m', 'new_m_sgu_b': 'new_m', 'new_m_w_out': 'new_m', 'new_m_norm_xattn_g': 'new_m', 'new_m_mem_norm_g': 'new_m', 'new_m_wq': 'new_m', 'new_m_wk': 'new_m', 'new_m_wv': 'new_m', 'new_m_wo': 'new_m', 'new_m_norm_ffn_g': 'new_m', 'new_m_w_up': 'new_m', 'new_m_conv_w': 'new_m', 'new_m_conv_b': 'new_m', 'new_m_w_down': 'new_m', 'new_m_final_norm_g': 'new_m', 'new_v_norm_mix_g': 'new_v', 'new_v_w_in': 'new_v', 'new_v_pool_w': 'new_v', 'new_v_pool_scale': 'new_v', 'new_v_sgu_g': 'new_v', 'new_v_sgu_w': 'new_v', 'new_v_sgu_b': 'new_v', 'new_v_w_out': 'new_v', 'new_v_norm_xattn_g': 'new_v', 'new_v_mem_norm_g': 'new_v', 'new_v_wq': 'new_v', 'new_v_wk': 'new_v', 'new_v_wv': 'new_v', 'new_v_wo': 'new_v', 'new_v_norm_ffn_g': 'new_v', 'new_v_w_up': 'new_v', 'new_v_conv_w': 'new_v', 'new_v_conv_b': 'new_v', 'new_v_w_down': 'new_v', 'new_v_final_norm_g': 'new_v'}


def _forward(args):
    return _fwd_reference(*[args[k] for k in FWD_PARAMS])


def _output_shape():
    out = _jax.eval_shape(lambda: _forward(_fwd_setup_inputs(0)))
    return out.shape, out.dtype

N_MICROBATCH = 1
ADAM_LR = 0.001
ADAM_B1 = 0.9
ADAM_B2 = 0.999
ADAM_EPS = 1e-08
ADAM_WD = 0.01
ADAM_STEP = 10
PER_EXAMPLE_BATCH_AXIS = {'x': 0, 'mem': 0, 'loss_target': 0}
SHARED_INPUTS = []
_WEIGHT_DTYPES = {'norm_mix_g': _jnp.float32, 'w_in': _jnp.float32, 'pool_w': _jnp.float32, 'pool_scale': _jnp.float32, 'sgu_g': _jnp.float32, 'sgu_w': _jnp.float32, 'sgu_b': _jnp.float32, 'w_out': _jnp.float32, 'norm_xattn_g': _jnp.float32, 'mem_norm_g': _jnp.float32, 'wq': _jnp.float32, 'wk': _jnp.float32, 'wv': _jnp.float32, 'wo': _jnp.float32, 'norm_ffn_g': _jnp.float32, 'w_up': _jnp.float32, 'conv_w': _jnp.float32, 'conv_b': _jnp.float32, 'w_down': _jnp.float32, 'final_norm_g': _jnp.float32}
MOMENT_SCALE = {'norm_mix_g': 2.036666e-01, 'w_in': 1.533935e-01, 'pool_w': 1.816773e-01, 'pool_scale': 1.837844e-01, 'sgu_g': 9.427709e-02, 'sgu_w': 9.087397e-02, 'sgu_b': 1.269970e-01, 'w_out': 1.788793e-01, 'norm_xattn_g': 2.068739e-02, 'mem_norm_g': 3.272175e-02, 'wq': 2.103559e-02, 'wk': 2.108768e-02, 'wv': 2.273615e-02, 'wo': 2.209776e-02, 'norm_ffn_g': 1.420869e-01, 'w_up': 6.286212e-02, 'conv_w': 6.390211e-02, 'conv_b': 6.854509e-02, 'w_down': 1.031312e-01, 'final_norm_g': 6.421224e+01}


def _to_microbatches(a, axis):
    t = _jnp.moveaxis(a, axis, 0)
    t = t.reshape((N_MICROBATCH, t.shape[0] // N_MICROBATCH) + t.shape[1:])
    return _jnp.moveaxis(t, 1, axis + 1)


def setup_inputs(seed: int = 0) -> dict:
    inp = _fwd_setup_inputs(seed)
    key = _jax.random.fold_in(_jax.random.key(seed), 7919)
    shape, _ = _output_shape()
    out = dict(inp)
    out["loss_target"] = _jax.random.normal(_jax.random.fold_in(key, 0), shape, _jnp.float32)
    for i, name in enumerate(TWIN_WEIGHTS):
        w = inp[name].astype(_jnp.float32)
        if MOMENT_SCALE is None:
            s = _jnp.sqrt(_jnp.mean(_jnp.square(w)) + 1e-30)
        else:
            s = MOMENT_SCALE[name]
        km, kv = _jax.random.split(_jax.random.fold_in(key, i + 1))
        out[name] = w
        out["m_" + name] = s * _jax.random.normal(km, w.shape, _jnp.float32)
        out["v_" + name] = (s * s) * _jax.random.uniform(kv, w.shape, _jnp.float32, 0.5, 1.5)
    if N_MICROBATCH > 1:
        for name, axis in PER_EXAMPLE_BATCH_AXIS.items():
            out[name] = _to_microbatches(out[name], axis)
    return {'x': out['x'], 'mem': out['mem'], 'norm_mix_g': out['norm_mix_g'], 'w_in': out['w_in'], 'pool_w': out['pool_w'], 'pool_scale': out['pool_scale'], 'sgu_g': out['sgu_g'], 'sgu_w': out['sgu_w'], 'sgu_b': out['sgu_b'], 'w_out': out['w_out'], 'norm_xattn_g': out['norm_xattn_g'], 'mem_norm_g': out['mem_norm_g'], 'wq': out['wq'], 'wk': out['wk'], 'wv': out['wv'], 'wo': out['wo'], 'norm_ffn_g': out['norm_ffn_g'], 'w_up': out['w_up'], 'conv_w': out['conv_w'], 'conv_b': out['conv_b'], 'w_down': out['w_down'], 'final_norm_g': out['final_norm_g'], 'loss_target': out['loss_target'], 'm_norm_mix_g': out['m_norm_mix_g'], 'm_w_in': out['m_w_in'], 'm_pool_w': out['m_pool_w'], 'm_pool_scale': out['m_pool_scale'], 'm_sgu_g': out['m_sgu_g'], 'm_sgu_w': out['m_sgu_w'], 'm_sgu_b': out['m_sgu_b'], 'm_w_out': out['m_w_out'], 'm_norm_xattn_g': out['m_norm_xattn_g'], 'm_mem_norm_g': out['m_mem_norm_g'], 'm_wq': out['m_wq'], 'm_wk': out['m_wk'], 'm_wv': out['m_wv'], 'm_wo': out['m_wo'], 'm_norm_ffn_g': out['m_norm_ffn_g'], 'm_w_up': out['m_w_up'], 'm_conv_w': out['m_conv_w'], 'm_conv_b': out['m_conv_b'], 'm_w_down': out['m_w_down'], 'm_final_norm_g': out['m_final_norm_g'], 'v_norm_mix_g': out['v_norm_mix_g'], 'v_w_in': out['v_w_in'], 'v_pool_w': out['v_pool_w'], 'v_pool_scale': out['v_pool_scale'], 'v_sgu_g': out['v_sgu_g'], 'v_sgu_w': out['v_sgu_w'], 'v_sgu_b': out['v_sgu_b'], 'v_w_out': out['v_w_out'], 'v_norm_xattn_g': out['v_norm_xattn_g'], 'v_mem_norm_g': out['v_mem_norm_g'], 'v_wq': out['v_wq'], 'v_wk': out['v_wk'], 'v_wv': out['v_wv'], 'v_wo': out['v_wo'], 'v_norm_ffn_g': out['v_norm_ffn_g'], 'v_w_up': out['v_w_up'], 'v_conv_w': out['v_conv_w'], 'v_conv_b': out['v_conv_b'], 'v_w_down': out['v_w_down'], 'v_final_norm_g': out['v_final_norm_g']}


def _loss(weights, diff, rest, loss_target):
    with _jax.named_scope("forward"):
        args = {**rest, TWIN_DIFF_INPUT: diff, **{k: w.astype(_WEIGHT_DTYPES[k]) for k, w in weights.items()}}
        y = _forward(args)
    with _jax.named_scope("loss_head"):
        err = _jnp.square(y.astype(_jnp.float32) - loss_target)
        return 0.5 * _jnp.sum(_jnp.mean(err, axis=-1)) if err.ndim else 0.5 * err


def _adamw(w, g, m, v):
    m = ADAM_B1 * m + (1.0 - ADAM_B1) * g
    v = ADAM_B2 * v + (1.0 - ADAM_B2) * _jnp.square(g)
    m_hat = m / (1.0 - ADAM_B1 ** ADAM_STEP)
    v_hat = v / (1.0 - ADAM_B2 ** ADAM_STEP)
    delta = -ADAM_LR * (m_hat / (_jnp.sqrt(v_hat) + ADAM_EPS) + ADAM_WD * w)
    return delta, m, v


def reference(x, mem, norm_mix_g, w_in, pool_w, pool_scale, sgu_g, sgu_w, sgu_b, w_out, norm_xattn_g, mem_norm_g, wq, wk, wv, wo, norm_ffn_g, w_up, conv_w, conv_b, w_down, final_norm_g, loss_target, m_norm_mix_g, m_w_in, m_pool_w, m_pool_scale, m_sgu_g, m_sgu_w, m_sgu_b, m_w_out, m_norm_xattn_g, m_mem_norm_g, m_wq, m_wk, m_wv, m_wo, m_norm_ffn_g, m_w_up, m_conv_w, m_conv_b, m_w_down, m_final_norm_g, v_norm_mix_g, v_w_in, v_pool_w, v_pool_scale, v_sgu_g, v_sgu_w, v_sgu_b, v_w_out, v_norm_xattn_g, v_mem_norm_g, v_wq, v_wk, v_wv, v_wo, v_norm_ffn_g, v_w_up, v_conv_w, v_conv_b, v_w_down, v_final_norm_g):
    given = dict(x=x, mem=mem, norm_mix_g=norm_mix_g, w_in=w_in, pool_w=pool_w, pool_scale=pool_scale, sgu_g=sgu_g, sgu_w=sgu_w, sgu_b=sgu_b, w_out=w_out, norm_xattn_g=norm_xattn_g, mem_norm_g=mem_norm_g, wq=wq, wk=wk, wv=wv, wo=wo, norm_ffn_g=norm_ffn_g, w_up=w_up, conv_w=conv_w, conv_b=conv_b, w_down=w_down, final_norm_g=final_norm_g, loss_target=loss_target, m_norm_mix_g=m_norm_mix_g, m_w_in=m_w_in, m_pool_w=m_pool_w, m_pool_scale=m_pool_scale, m_sgu_g=m_sgu_g, m_sgu_w=m_sgu_w, m_sgu_b=m_sgu_b, m_w_out=m_w_out, m_norm_xattn_g=m_norm_xattn_g, m_mem_norm_g=m_mem_norm_g, m_wq=m_wq, m_wk=m_wk, m_wv=m_wv, m_wo=m_wo, m_norm_ffn_g=m_norm_ffn_g, m_w_up=m_w_up, m_conv_w=m_conv_w, m_conv_b=m_conv_b, m_w_down=m_w_down, m_final_norm_g=m_final_norm_g, v_norm_mix_g=v_norm_mix_g, v_w_in=v_w_in, v_pool_w=v_pool_w, v_pool_scale=v_pool_scale, v_sgu_g=v_sgu_g, v_sgu_w=v_sgu_w, v_sgu_b=v_sgu_b, v_w_out=v_w_out, v_norm_xattn_g=v_norm_xattn_g, v_mem_norm_g=v_mem_norm_g, v_wq=v_wq, v_wk=v_wk, v_wv=v_wv, v_wo=v_wo, v_norm_ffn_g=v_norm_ffn_g, v_w_up=v_w_up, v_conv_w=v_conv_w, v_conv_b=v_conv_b, v_w_down=v_w_down, v_final_norm_g=v_final_norm_g)
    weights = {n: given[n] for n in TWIN_WEIGHTS}
    shared = {n: given[n] for n in SHARED_INPUTS}
    per_example = {n: given[n] for n in ['x', 'mem']}
    grad_fn = _jax.value_and_grad(_loss, argnums=(0, 1))

    def one_microbatch(ex, loss_target):
        ex = dict(ex)
        diff = ex.pop(TWIN_DIFF_INPUT)
        return grad_fn(weights, diff, {**shared, **ex}, loss_target)

    if N_MICROBATCH == 1:
        loss, (grad_w, grad_x) = one_microbatch(per_example, given["loss_target"])
    else:
        def body(carry, xs):
            loss_sum, grad_sum = carry
            l_k, (gw_k, gx_k) = one_microbatch(xs[0], xs[1])
            with _jax.named_scope("update"):
                return (loss_sum + l_k, _jax.tree.map(_jnp.add, grad_sum, gw_k)), gx_k

        init = (_jnp.zeros((), _jnp.float32), _jax.tree.map(_jnp.zeros_like, weights))
        (loss, grad_w), grad_x = _jax.lax.scan(body, init, (per_example, given["loss_target"]))
    with _jax.named_scope("update"):
        delta_w, new_m, new_v = {}, {}, {}
        for n in TWIN_WEIGHTS:
            delta_w[n], new_m[n], new_v[n] = _adamw(weights[n], grad_w[n], given["m_" + n], given["v_" + n])
    return (loss, grad_x, *[grad_w[n] for n in TWIN_WEIGHTS], *[delta_w[n] for n in TWIN_WEIGHTS],
            *[new_m[n] for n in TWIN_WEIGHTS], *[new_v[n] for n in TWIN_WEIGHTS])
```

```python
import functools
import math

import jax
import jax.numpy as jnp
from jax import lax
from jax.experimental import pallas as pl
from jax.experimental.pallas import tpu as pltpu

F32 = jnp.float32
_MXU = jnp.bfloat16
_PAY = jnp.bfloat16
EPS = 1e-6
WINDOWS = (2, 4, 8, 16)
GROUP = 128
N_XHEADS = 4
HALO = 16
FF_TILE = 256
VMEM_LIMIT = 56 * 1024 * 1024
MESH = pl.DeviceIdType.MESH

ADAM_LR, ADAM_B1, ADAM_B2, ADAM_EPS, ADAM_WD, ADAM_STEP = 0.001, 0.9, 0.999, 1e-08, 0.01, 10

VM = pl.BlockSpec(memory_space=pltpu.VMEM)
HB = pl.BlockSpec(memory_space=pltpu.HBM)


def _nn(a, b):
    return jnp.dot(a, b, preferred_element_type=F32)


def _nt(a, b):
    return lax.dot_general(a, b, (((1,), (1,)), ((), ())), preferred_element_type=F32)


def _tn(a, b):
    return lax.dot_general(a, b, (((0,), (0,)), ((), ())), preferred_element_type=F32)


def _rms(x):
    r = lax.rsqrt(jnp.mean(x * x, axis=-1, keepdims=True) + EPS)
    return x * r, r


def _rms_bwd(dxn, xhat, r, g):
    dxh = dxn * g
    dx = r * (dxh - xhat * jnp.mean(dxh * xhat, axis=-1, keepdims=True))
    return dx, jnp.sum(dxn * xhat, axis=0, keepdims=True)


def _gelu(x):
    return 0.5 * x * (1.0 + lax.erf(x * (2.0 ** -0.5)))


def _gelu_grad(x):
    return 0.5 * (1.0 + lax.erf(x * (2.0 ** -0.5))) + x * jnp.exp(-0.5 * x * x) * ((2.0 * math.pi) ** -0.5)


def _params(sem=None):
    return pltpu.CompilerParams(dimension_semantics=sem, vmem_limit_bytes=VMEM_LIMIT)


def _token_block(t, want):
    return want if t % want == 0 and t > want else GROUP


def _const_spec(shape):
    n = len(shape)
    return pl.BlockSpec(shape, lambda i: (0,) * n)


def _tril():
    return lax.broadcasted_iota(jnp.int32, (GROUP, GROUP), 0) >= lax.broadcasted_iota(jnp.int32, (GROUP, GROUP), 1)


def _pool_diff(pext, p, t0, tb, gi, win):
    sl = slice(gi * GROUP, (gi + 1) * GROUP)
    s = p[:, sl]
    for k in range(1, win):
        s = s + pext[HALO - k:HALO - k + tb, sl]
    tglob = t0 + lax.broadcasted_iota(jnp.int32, (tb, 1), 0)
    cnt = jnp.minimum(tglob + 1, win).astype(F32)
    return s / cnt - p[:, sl], cnt


def _layernorm(v):
    xc = v - jnp.mean(v, axis=-1, keepdims=True)
    rstd = lax.rsqrt(jnp.mean(xc * xc, axis=-1, keepdims=True) + EPS)
    return xc * rstd, rstd


def _mixer_fwd(h, g, w_in, pool_w, pool_scale, sgu_g, sgu_w, sgu_bt, w_out):
    t, d = h.shape
    pw = pool_w.shape[0] * GROUP
    sw = sgu_w.shape[0] * GROUP
    tb = _token_block(t, 512)

    def body(h_ref, g_ref, win_ref, pw_ref, ps_ref, sg_ref, sw_ref, sbt_ref, wout_ref, h1_ref, proj_ref, xn_ref, mix_ref, pext):
        i = pl.program_id(0)

        @pl.when(i == 0)
        def _():
            pext[0:HALO, :] = jnp.zeros((HALO, pw), F32)

        x = h_ref[...]
        xhat, _ = _rms(x)
        xn = (xhat * g_ref[...]).astype(_MXU)
        xn_ref[...] = xn
        proj = _nn(xn, win_ref[...])
        proj_ref[...] = proj
        p = proj[:, :pw]
        pext[HALO:HALO + tb, :] = p
        for gi, win in enumerate(WINDOWS):
            sl = slice(gi * GROUP, (gi + 1) * GROUP)
            dg, _ = _pool_diff(pext, p, i * tb, tb, gi, win)
            e = _nn(dg.astype(_MXU), pw_ref[gi].astype(_MXU))
            mix_ref[:, sl] = (e * ps_ref[:, sl]).astype(_MXU)
        pext[0:HALO, :] = p[tb - HALO:tb, :]
        uv = _gelu(proj[:, pw:])
        u = uv[:, :sw]
        vhat, _ = _layernorm(uv[:, sw:])
        vn = (vhat * sg_ref[...]).astype(_MXU)
        mask = _tril()
        for hh in range(sw // GROUP):
            wm = jnp.where(mask, sw_ref[hh], 0.0).astype(_MXU)
            for n in range(tb // GROUP):
                rows = slice(n * GROUP, (n + 1) * GROUP)
                cols = slice(hh * GROUP, (hh + 1) * GROUP)
                z = _nn(wm, vn[rows, cols]) + sbt_ref[hh]
                mix_ref[rows, pw + hh * GROUP:pw + (hh + 1) * GROUP] = (u[rows, cols] * z).astype(_MXU)
        h1_ref[...] = x + _nn(mix_ref[...], wout_ref[...])

    blk = lambda w: pl.BlockSpec((tb, w), lambda i: (i, 0))
    return pl.pallas_call(
        body, name="mixer_fwd", grid=(t // tb,),
        in_specs=[blk(d), VM, VM, VM, VM, VM, VM, VM, VM],
        out_specs=[blk(d), blk(w_in.shape[1]), blk(d), blk(d)],
        out_shape=[jax.ShapeDtypeStruct((t, d), F32), jax.ShapeDtypeStruct((t, w_in.shape[1]), F32),
                   jax.ShapeDtypeStruct((t, d), _MXU), jax.ShapeDtypeStruct((t, d), _MXU)],
        scratch_shapes=[pltpu.VMEM((HALO + tb, pw), F32)],
        compiler_params=_params(("arbitrary",)),
    )(h, g, w_in, pool_w, pool_scale, sgu_g, sgu_w, sgu_bt, w_out)


def _mixer_bwd(dh1, h, proj, g, w_in, pool_w, pool_scale, sgu_g, sgu_w, sgu_bt, w_out):
    t, d = h.shape
    ng, nh = pool_w.shape[0], sgu_w.shape[0]
    pw, sw = ng * GROUP, nh * GROUP
    tb = _token_block(t, 256)
    nb = t // tb

    def body(dh1_ref, h_ref, proj_ref, halo_ref, g_ref, win_ref, pw_ref, ps_ref, sg_ref, sw_ref, sbt_ref, wout_ref,
             dh_ref, dproj_ref, gg_ref, gpw_ref, gps_ref, gsg_ref, gsw_ref, gsbt_ref, pext, dext, duv):
        i = pl.program_id(0)
        blk = nb - 1 - i

        @pl.when(i == 0)
        def _():
            for r in (gg_ref, gpw_ref, gps_ref, gsg_ref, gsw_ref, gsbt_ref):
                r[...] = jnp.zeros(r.shape, F32)
            dext[tb:tb + HALO, :] = jnp.zeros((HALO, pw), F32)

        dh1v = dh1_ref[...]
        dmix = _nt(dh1v.astype(_MXU), wout_ref[...])
        proj_v = proj_ref[...]
        p = proj_v[:, :pw]
        pext[0:HALO, :] = jnp.where(blk == 0, 0.0, halo_ref[...])
        pext[HALO:HALO + tb, :] = p
        for gi, win in enumerate(WINDOWS):
            sl = slice(gi * GROUP, (gi + 1) * GROUP)
            dg, cnt = _pool_diff(pext, p, blk * tb, tb, gi, win)
            dgm = dg.astype(_MXU)
            pwm = pw_ref[gi].astype(_MXU)
            e = _nn(dgm, pwm)
            dy = dmix[:, sl]
            gps_ref[:, sl] += jnp.sum(dy * e, axis=0, keepdims=True)
            de = (dy * ps_ref[:, sl]).astype(_MXU)
            gpw_ref[gi] += _tn(dgm, de)
            dd = _nt(de, pwm)
            ddc = dd / cnt
            dext[0:tb, sl] = ddc
            acc = ddc
            for k in range(1, win):
                acc = acc + dext[k:k + tb, sl]
            dext[tb:tb + HALO, sl] = ddc[0:HALO, :]
            dproj_ref[:, sl] = (acc - dd).astype(_MXU)
        pre = proj_v[:, pw:]
        uv = _gelu(pre)
        u = uv[:, :sw]
        vhat, rstd = _layernorm(uv[:, sw:])
        vn = (vhat * sg_ref[...]).astype(_MXU)
        mask = _tril()
        for hh in range(nh):
            wm = jnp.where(mask, sw_ref[hh], 0.0).astype(_MXU)
            cols = slice(hh * GROUP, (hh + 1) * GROUP)
            gw = jnp.zeros((GROUP, GROUP), F32)
            gb = jnp.zeros((GROUP, GROUP), F32)
            for n in range(tb // GROUP):
                rows = slice(n * GROUP, (n + 1) * GROUP)
                vs = vn[rows, cols]
                z = _nn(wm, vs) + sbt_ref[hh]
                dy = dmix[rows, pw + hh * GROUP:pw + (hh + 1) * GROUP]
                dz = dy * u[rows, cols]
                gb = gb + dz
                dzm = dz.astype(_MXU)
                gw = gw + _nt(dzm, vs)
                duv[rows, cols] = dy * z
                duv[rows, sw + hh * GROUP:sw + (hh + 1) * GROUP] = _tn(wm, dzm)
            gsw_ref[hh] += jnp.where(mask, gw, 0.0)
            gsbt_ref[hh] += gb
        dvn = duv[:, sw:]
        gsg_ref[...] += jnp.sum(dvn * vhat, axis=0, keepdims=True)
        dxh = dvn * sg_ref[...]
        dv = rstd * (dxh - jnp.mean(dxh, axis=-1, keepdims=True) - vhat * jnp.mean(dxh * vhat, axis=-1, keepdims=True))
        gp = _gelu_grad(pre)
        dproj_ref[:, pw:pw + sw] = (duv[:, :sw] * gp[:, :sw]).astype(_MXU)
        dproj_ref[:, pw + sw:] = (dv * gp[:, sw:]).astype(_MXU)
        dxn = _nt(dproj_ref[...], win_ref[...])
        xhat, r = _rms(h_ref[...])
        dx, gg = _rms_bwd(dxn, xhat, r, g_ref[...])
        gg_ref[...] += gg
        dh_ref[...] = dh1v + dx

    rev = lambda w: pl.BlockSpec((tb, w), lambda i: (nb - 1 - i, 0))
    halo = pl.BlockSpec((HALO, pw), lambda i: (jnp.maximum((nb - 1 - i) * (tb // HALO) - 1, 0), 0))
    small = [(1, d), (ng, GROUP, GROUP), (1, pw), (1, sw), (nh, GROUP, GROUP), (nh, GROUP, GROUP)]
    return pl.pallas_call(
        body, name="mixer_bwd", grid=(nb,),
        in_specs=[rev(d), rev(d), rev(proj.shape[1]), halo, VM, VM, VM, VM, VM, VM, VM, VM],
        out_specs=[rev(d), rev(proj.shape[1])] + [_const_spec(s) for s in small],
        out_shape=[jax.ShapeDtypeStruct((t, d), F32), jax.ShapeDtypeStruct(proj.shape, _MXU)]
        + [jax.ShapeDtypeStruct(s, F32) for s in small],
        scratch_shapes=[pltpu.VMEM((HALO + tb, pw), F32), pltpu.VMEM((tb + HALO, pw), F32), pltpu.VMEM((tb, 2 * sw), F32)],
        compiler_params=_params(("arbitrary",)),
    )(dh1, h, proj, proj, g, w_in, pool_w, pool_scale, sgu_g, sgu_w, sgu_bt, w_out)


def _kv_fwd(mem, gm, wk, wv):
    n, d = mem.shape

    def body(mem_ref, gm_ref, wk_ref, wv_ref, k_ref, v_ref, memn_ref):
        xhat, _ = _rms(mem_ref[...])
        memn = (xhat * gm_ref[...]).astype(_MXU)
        memn_ref[...] = memn
        k_ref[...] = _nn(memn, wk_ref[...]).astype(_MXU)
        v_ref[...] = _nn(memn, wv_ref[...]).astype(_MXU)

    return pl.pallas_call(
        body, name="kv_fwd", in_specs=[VM] * 4, out_specs=[VM] * 3,
        out_shape=[jax.ShapeDtypeStruct((n, d), _MXU)] * 3, compiler_params=_params(),
    )(mem, gm, wk, wv)


def _kv_bwd(dk, dv, mem, wk, wv):
    n, d = mem.shape

    def body(dk_ref, dv_ref, mem_ref, wk_ref, wv_ref, ggm_ref):
        dmemn = _nt(dk_ref[...].astype(_MXU), wk_ref[...]) + _nt(dv_ref[...].astype(_MXU), wv_ref[...])
        xhat, _ = _rms(mem_ref[...])
        ggm_ref[...] = jnp.sum(dmemn * xhat, axis=0, keepdims=True)

    return pl.pallas_call(
        body, name="kv_bwd", in_specs=[VM] * 5, out_specs=VM,
        out_shape=jax.ShapeDtypeStruct((1, d), F32), compiler_params=_params(),
    )(dk, dv, mem, wk, wv)


def _softmax_rows(qm, k_ref, sl, scale):
    s = _nt(qm, k_ref[:, sl]) * scale
    e = jnp.exp(s - jnp.max(s, axis=-1, keepdims=True))
    return e / jnp.sum(e, axis=-1, keepdims=True)


def _xattn_fwd(h, g, wq, k, v, wo):
    t, d = h.shape
    hd = d // N_XHEADS
    scale = hd ** -0.5
    tb = _token_block(t, 512)

    def body(h_ref, g_ref, wq_ref, k_ref, v_ref, wo_ref, h2_ref, q_ref, o_ref, xn_ref):
        x = h_ref[...]
        xhat, _ = _rms(x)
        xn = (xhat * g_ref[...]).astype(_MXU)
        xn_ref[...] = xn
        qm = _nn(xn, wq_ref[...]).astype(_MXU)
        q_ref[...] = qm
        for a in range(N_XHEADS):
            sl = slice(a * hd, (a + 1) * hd)
            pr = _softmax_rows(qm[:, sl], k_ref, sl, scale)
            o_ref[:, sl] = _nn(pr.astype(_MXU), v_ref[:, sl]).astype(_MXU)
        h2_ref[...] = x + _nn(o_ref[...], wo_ref[...])

    blk = pl.BlockSpec((tb, d), lambda i: (i, 0))
    return pl.pallas_call(
        body, name="xattn_fwd", grid=(t // tb,),
        in_specs=[blk, VM, VM, VM, VM, VM], out_specs=[blk] * 4,
        out_shape=[jax.ShapeDtypeStruct((t, d), F32)] + [jax.ShapeDtypeStruct((t, d), _MXU)] * 3,
        compiler_params=_params(("arbitrary",)),
    )(h, g, wq, k, v, wo)


def _xattn_bwd(dh2, h, q, g, wq, k, v, wo):
    t, d = h.shape
    n = k.shape[0]
    hd = d // N_XHEADS
    scale = hd ** -0.5
    tb = _token_block(t, 512)

    def body(dh2_ref, h_ref, q_ref, g_ref, wq_ref, k_ref, v_ref, wo_ref, dh_ref, dq_ref, dk_ref, dv_ref, gg_ref):
        @pl.when(pl.program_id(0) == 0)
        def _():
            for r in (dk_ref, dv_ref, gg_ref):
                r[...] = jnp.zeros(r.shape, F32)

        dh2v = dh2_ref[...]
        dom = _nt(dh2v.astype(_MXU), wo_ref[...]).astype(_MXU)
        for a in range(N_XHEADS):
            sl = slice(a * hd, (a + 1) * hd)
            qh = q_ref[:, sl]
            pr = _softmax_rows(qh, k_ref, sl, scale)
            dv_ref[:, sl] += _tn(pr.astype(_MXU), dom[:, sl])
            dpr = _nt(dom[:, sl], v_ref[:, sl])
            ds = (pr * (dpr - jnp.sum(dpr * pr, axis=-1, keepdims=True)) * scale).astype(_MXU)
            dq_ref[:, sl] = _nn(ds, k_ref[:, sl]).astype(_MXU)
            dk_ref[:, sl] += _tn(ds, qh)
        dxn = _nt(dq_ref[...], wq_ref[...])
        xhat, r = _rms(h_ref[...])
        dx, gg = _rms_bwd(dxn, xhat, r, g_ref[...])
        gg_ref[...] += gg
        dh_ref[...] = dh2v + dx

    blk = pl.BlockSpec((tb, d), lambda i: (i, 0))
    return pl.pallas_call(
        body, name="xattn_bwd", grid=(t // tb,),
        in_specs=[blk, blk, blk, VM, VM, VM, VM, VM],
        out_specs=[blk, blk, _const_spec((n, d)), _const_spec((n, d)), _const_spec((1, d))],
        out_shape=[jax.ShapeDtypeStruct((t, d), F32), jax.ShapeDtypeStruct((t, d), _MXU),
                   jax.ShapeDtypeStruct((n, d), F32), jax.ShapeDtypeStruct((n, d), F32), jax.ShapeDtypeStruct((1, d), F32)],
        compiler_params=_params(("arbitrary",)),
    )(dh2, h, q, g, wq, k, v, wo)


def _ffn_fwd(h, g, w_up, conv_w, conv_b, w_down):
    t, d = h.shape
    f = w_down.shape[0]
    ft = FF_TILE
    tb = _token_block(t, 512)

    def body(h_ref, g_ref, wup_ref, cw_ref, cb_ref, wdown_ref, h3_ref, hh_ref, xn_ref, ext, carry):
        @pl.when(pl.program_id(0) == 0)
        def _():
            carry[...] = jnp.zeros(carry.shape, F32)

        x = h_ref[...]
        xhat, _ = _rms(x)
        xn = (xhat * g_ref[...]).astype(_MXU)
        xn_ref[...] = xn
        acc = jnp.zeros((tb, d), F32)
        for j in range(f // ft):
            hc = []
            for part, off in enumerate((j * ft, f + j * ft)):
                cols = slice(off, off + ft)
                cur = _nn(xn, wup_ref[:, cols])
                hh_ref[:, cols] = cur.astype(_MXU)
                ext[part, 0:8, :] = carry[:, cols]
                ext[part, 8:8 + tb, :] = cur
                carry[:, cols] = cur[tb - 8:tb, :]
                hc.append(cb_ref[:, cols] + cw_ref[0:1, cols] * ext[part, 6:6 + tb, :]
                          + cw_ref[1:2, cols] * ext[part, 7:7 + tb, :] + cw_ref[2:3, cols] * cur)
            act = (hc[0] * jax.nn.sigmoid(hc[0]) * hc[1]).astype(_MXU)
            acc = acc + _nn(act, wdown_ref[j * ft:(j + 1) * ft, :])
        h3_ref[...] = x + acc

    blk = lambda w: pl.BlockSpec((tb, w), lambda i: (i, 0))
    return pl.pallas_call(
        body, name="ffn_fwd", grid=(t // tb,),
        in_specs=[blk(d), VM, VM, VM, VM, VM], out_specs=[blk(d), blk(2 * f), blk(d)],
        out_shape=[jax.ShapeDtypeStruct((t, d), F32), jax.ShapeDtypeStruct((t, 2 * f), _MXU), jax.ShapeDtypeStruct((t, d), _MXU)],
        scratch_shapes=[pltpu.VMEM((2, 8 + tb, ft), F32), pltpu.VMEM((8, 2 * f), F32)],
        compiler_params=_params(("arbitrary",)),
    )(h, g, w_up, conv_w, conv_b, w_down)


def _ffn_bwd(dh3, h, hh, g, w_up, conv_w, conv_b, w_down):
    t, d = h.shape
    f = w_down.shape[0]
    ft = FF_TILE
    tb = _token_block(t, 256)
    nb = t // tb

    def body(dh3_ref, h_ref, hh_ref, halo_ref, g_ref, wup_ref, cw_ref, cb_ref, wdown_ref,
             dh_ref, dhh_ref, act_ref, gcw_ref, gcb_ref, gg_ref, hext, dext, dcarry):
        i = pl.program_id(0)
        blk = nb - 1 - i

        @pl.when(i == 0)
        def _():
            for r in (gcw_ref, gcb_ref, gg_ref, dcarry):
                r[...] = jnp.zeros(r.shape, F32)

        dh3v = dh3_ref[...]
        dhm = dh3v.astype(_MXU)
        dxn = jnp.zeros((tb, d), F32)
        for j in range(f // ft):
            cur, back1, back2, hc = [], [], [], []
            for part, off in enumerate((j * ft, f + j * ft)):
                cols = slice(off, off + ft)
                c0 = hh_ref[:, cols].astype(F32)
                hext[part, 0:HALO, :] = jnp.where(blk == 0, 0.0, halo_ref[:, cols].astype(F32))
                hext[part, HALO:HALO + tb, :] = c0
                b1 = hext[part, HALO - 1:HALO - 1 + tb, :]
                b2 = hext[part, HALO - 2:HALO - 2 + tb, :]
                cur.append(c0)
                back1.append(b1)
                back2.append(b2)
                hc.append(cb_ref[:, cols] + cw_ref[0:1, cols] * b2 + cw_ref[1:2, cols] * b1 + cw_ref[2:3, cols] * c0)
            sg = jax.nn.sigmoid(hc[0])
            silu = hc[0] * sg
            act_ref[:, j * ft:(j + 1) * ft] = (silu * hc[1]).astype(_MXU)
            dact = _nt(dhm, wdown_ref[j * ft:(j + 1) * ft, :])
            dhc = (dact * hc[1] * sg * (1.0 + hc[0] * (1.0 - sg)), dact * silu)
            for part, off in enumerate((j * ft, f + j * ft)):
                cols = slice(off, off + ft)
                dc = dhc[part]
                gcb_ref[:, cols] += jnp.sum(dc, axis=0, keepdims=True)
                gcw_ref[0:1, cols] += jnp.sum(dc * back2[part], axis=0, keepdims=True)
                gcw_ref[1:2, cols] += jnp.sum(dc * back1[part], axis=0, keepdims=True)
                gcw_ref[2:3, cols] += jnp.sum(dc * cur[part], axis=0, keepdims=True)
                dext[part, 0:tb, :] = dc
                dext[part, tb:tb + 8, :] = dcarry[:, cols]
                dhh = (cw_ref[2:3, cols] * dc + cw_ref[1:2, cols] * dext[part, 1:1 + tb, :]
                       + cw_ref[0:1, cols] * dext[part, 2:2 + tb, :]).astype(_MXU)
                dcarry[:, cols] = dc[0:8, :]
                dhh_ref[:, cols] = dhh
                dxn = dxn + _nt(dhh, wup_ref[:, cols])
        xhat, r = _rms(h_ref[...])
        dx, gg = _rms_bwd(dxn, xhat, r, g_ref[...])
        gg_ref[...] += gg
        dh_ref[...] = dh3v + dx

    rev = lambda w: pl.BlockSpec((tb, w), lambda i: (nb - 1 - i, 0))
    halo = pl.BlockSpec((HALO, 2 * f), lambda i: (jnp.maximum((nb - 1 - i) * (tb // HALO) - 1, 0), 0))
    return pl.pallas_call(
        body, name="ffn_bwd", grid=(nb,),
        in_specs=[rev(d), rev(d), rev(2 * f), halo, VM, VM, VM, VM, VM],
        out_specs=[rev(d), rev(2 * f), rev(f), _const_spec((3, 2 * f)), _const_spec((1, 2 * f)), _const_spec((1, d))],
        out_shape=[jax.ShapeDtypeStruct((t, d), F32), jax.ShapeDtypeStruct((t, 2 * f), _MXU), jax.ShapeDtypeStruct((t, f), _MXU),
                   jax.ShapeDtypeStruct((3, 2 * f), F32), jax.ShapeDtypeStruct((1, 2 * f), F32), jax.ShapeDtypeStruct((1, d), F32)],
        scratch_shapes=[pltpu.VMEM((2, HALO + tb, ft), F32), pltpu.VMEM((2, tb + 8, ft), F32), pltpu.VMEM((8, 2 * f), F32)],
        compiler_params=_params(("arbitrary",)),
    )(dh3, h, hh, hh, g, w_up, conv_w, conv_b, w_down)


def _loss_head(h, g, target):
    t, d = h.shape
    tb = _token_block(t, 512)

    def body(h_ref, g_ref, tgt_ref, dh_ref, loss_ref, gg_ref):
        @pl.when(pl.program_id(0) == 0)
        def _():
            loss_ref[...] = jnp.zeros(loss_ref.shape, F32)
            gg_ref[...] = jnp.zeros(gg_ref.shape, F32)

        xhat, r = _rms(h_ref[...])
        err = xhat * g_ref[...] - tgt_ref[...]
        loss_ref[...] += 0.5 * jnp.sum(jnp.sum(err * err, axis=-1, keepdims=True), axis=0, keepdims=True) / d
        dx, gg = _rms_bwd(err / d, xhat, r, g_ref[...])
        gg_ref[...] += gg
        dh_ref[...] = dx

    blk = pl.BlockSpec((tb, d), lambda i: (i, 0))
    return pl.pallas_call(
        body, name="loss_head", grid=(t // tb,),
        in_specs=[blk, VM, blk], out_specs=[blk, _const_spec((1, 1)), _const_spec((1, d))],
        out_shape=[jax.ShapeDtypeStruct((t, d), F32), jax.ShapeDtypeStruct((1, 1), F32), jax.ShapeDtypeStruct((1, d), F32)],
        compiler_params=_params(("arbitrary",)),
    )(h, g, target)


def _largest_tile(n, cap, mult=128):
    best = None
    for c in range(mult, min(n, cap) + 1, mult):
        if n % c == 0:
            best = c
    return best if best is not None else n


def _grad_matmul(a, b, name):
    t, m = a.shape
    n = b.shape[1]
    tm, tn, tk = _largest_tile(m, 1408), _largest_tile(n, 1024), _largest_tile(t, 1024)
    nk = t // tk

    def body(a_ref, b_ref, o_ref):
        @pl.when(pl.program_id(2) == 0)
        def _():
            o_ref[...] = jnp.zeros(o_ref.shape, F32)

        o_ref[...] += _tn(a_ref[...].astype(_MXU), b_ref[...].astype(_MXU))

    return pl.pallas_call(
        body, name=name, grid=(m // tm, n // tn, nk),
        in_specs=[pl.BlockSpec((tk, tm), lambda i, j, k: (k, i)), pl.BlockSpec((tk, tn), lambda i, j, k: (k, j))],
        out_specs=pl.BlockSpec((tm, tn), lambda i, j, k: (i, j)),
        out_shape=jax.ShapeDtypeStruct((m, n), F32),
        compiler_params=_params(("parallel", "parallel", "arbitrary")),
    )(a, b)


def _adamw_math(w, g, m, v):
    m = ADAM_B1 * m + (1.0 - ADAM_B1) * g
    v = ADAM_B2 * v + (1.0 - ADAM_B2) * (g * g)
    m_hat = m / (1.0 - ADAM_B1 ** ADAM_STEP)
    v_hat = v / (1.0 - ADAM_B2 ** ADAM_STEP)
    return -ADAM_LR * (m_hat / (jnp.sqrt(v_hat) + ADAM_EPS) + ADAM_WD * w), m, v


def _row_block(rows, cols, max_bytes=1 << 20, mult=16):
    best = None
    for r in range(mult, rows + 1, mult):
        if rows % r == 0 and r * cols * 4 <= max_bytes:
            best = r
    return best if best is not None else rows


def _adamw_big(w, g, m, v, name):
    shape = w.shape
    cols = shape[-1]
    flat = lambda a: a.reshape(-1, cols)
    rows = flat(w).shape[0]
    rb = _row_block(rows, cols)

    def body(w_ref, g_ref, m_ref, v_ref, d_ref, nm_ref, nv_ref):
        d_ref[...], nm_ref[...], nv_ref[...] = _adamw_math(w_ref[...], g_ref[...], m_ref[...], v_ref[...])

    blk = pl.BlockSpec((rb, cols), lambda i: (i, 0))
    outs = pl.pallas_call(
        body, name=name, grid=(rows // rb,), in_specs=[blk] * 4, out_specs=[blk] * 3,
        out_shape=[jax.ShapeDtypeStruct((rows, cols), F32)] * 3, compiler_params=_params(("parallel",)),
    )(flat(w), flat(g), flat(m), flat(v))
    return [o.reshape(shape) for o in outs]


def _adamw_small(ws, gs, ms, vs):
    n = len(ws)

    def body(*refs):
        for a in range(n):
            w_ref, g_ref, m_ref, v_ref = (refs[s * n + a] for s in range(4))
            d_ref, nm_ref, nv_ref = (refs[(4 + s) * n + a] for s in range(3))
            d_ref[...], nm_ref[...], nv_ref[...] = _adamw_math(w_ref[...], g_ref[...], m_ref[...], v_ref[...])

    outs = pl.pallas_call(
        body, name="adamw_small", in_specs=[VM] * (4 * n), out_specs=[VM] * (3 * n),
        out_shape=[jax.ShapeDtypeStruct(w.shape, F32) for w in ws] * 3, compiler_params=_params(),
    )(*ws, *gs, *ms, *vs)
    return outs[:n], outs[n:2 * n], outs[2 * n:]


def _place():
    x, y, c = lax.axis_index("x"), lax.axis_index("y"), lax.axis_index("c")
    chips = [(1 - x, y), (x, 1 - y), (1 - x, 1 - y)]
    return x, y, c, chips


def _rows(start, size, mult=16):
    return pl.ds(pl.multiple_of(start, mult), size)


def _full_window(ref, axis, chip, half=None):
    r, c = ref.shape
    if axis == 0:
        rs = r // 4
        if half is None:
            return ref.at[_rows(chip * rs, rs), :]
        return ref.at[_rows(chip * rs + half * (rs // 2), rs // 2), :]
    cs = c // 4
    if half is None:
        return ref.at[:, _rows(chip * cs, cs, 128)]
    return ref.at[_rows(half * (r // 2), r // 2), _rows(chip * cs, cs, 128)]


def _remote(src, dst, send_sem, recv_sem, to):
    return pltpu.make_async_remote_copy(src_ref=src, dst_ref=dst, send_sem=send_sem, recv_sem=recv_sem,
                                        device_id=to, device_id_type=MESH)


def _allgather_weights(shards, axes, conv_shard):
    na = len(shards)
    nl = shards[0].shape[0]
    full_shapes = []
    for s, ax in zip(shards, axes):
        _, rs, cs = s.shape
        full_shapes.append((rs * 4, cs) if ax == 0 else (rs, cs * 4))
    ncopy = na * nl * 3

    def body(*refs):
        ins, conv_in = refs[:na], refs[na]
        outs = refs[na + 1:na + 1 + na * nl]
        conv_outs = refs[na + 1 + na * nl:na + 1 + na * nl + nl]
        send, recv, fsend, frecv, csend, crecv, local = refs[na + 1 + na * nl + nl:]
        x, y, c, chips = _place()
        me = 2 * x + y
        sibling = (x, y, 1 - c)
        started = []
        n_local = 0
        for a in range(na):
            for l in range(nl):
                cp = pltpu.make_async_copy(ins[a].at[l], _full_window(outs[a * nl + l], axes[a], me), local.at[n_local])
                cp.start()
                started.append(cp)
                n_local += 1
        for l in range(nl):
            cs = conv_in.shape[2]
            cp = pltpu.make_async_copy(conv_in.at[l], conv_outs[l].at[:, _rows(me * cs, cs, 128)], local.at[n_local])
            cp.start()
            started.append(cp)
            n_local += 1
        sends = []
        for a in range(na):
            rs = ins[a].shape[1]
            for l in range(nl):
                for k, chip in enumerate(chips):
                    idx = (a * nl + l) * 3 + k
                    cp = _remote(ins[a].at[l, _rows(c * (rs // 2), rs // 2), :], _full_window(outs[a * nl + l], axes[a], me, c),
                                 send.at[idx], recv.at[idx], (*chip, c))
                    cp.start()
                    sends.append(cp)
        for l in range(nl):
            cs = conv_in.shape[2]
            for k, chip in enumerate(chips):
                cp = _remote(conv_in.at[l], conv_outs[l].at[:, _rows(me * cs, cs, 128)], csend.at[l * 3 + k], crecv.at[l * 3 + k], (*chip, c))
                cp.start()
                sends.append(cp)
        for a in range(na):
            for l in range(nl):
                for k, chip in enumerate(chips):
                    idx = (a * nl + l) * 3 + k
                    got = _full_window(outs[a * nl + l], axes[a], 2 * chip[0] + chip[1], c)
                    _remote(got, got, send.at[idx], recv.at[idx], sibling).wait_recv()
                    cp = _remote(got, got, fsend.at[idx], frecv.at[idx], sibling)
                    cp.start()
                    sends.append(cp)
        for a in range(na):
            for l in range(nl):
                for k, chip in enumerate(chips):
                    idx = (a * nl + l) * 3 + k
                    got = _full_window(outs[a * nl + l], axes[a], 2 * chip[0] + chip[1], 1 - c)
                    _remote(got, got, fsend.at[idx], frecv.at[idx], sibling).wait_recv()
        for l in range(nl):
            cs = conv_in.shape[2]
            for k, chip in enumerate(chips):
                got = conv_outs[l].at[:, _rows((2 * chip[0] + chip[1]) * cs, cs, 128)]
                _remote(got, got, csend.at[l * 3 + k], crecv.at[l * 3 + k], sibling).wait_recv()
        for cp in sends:
            cp.wait_send()
        for cp in started:
            cp.wait()

    out_shape = [jax.ShapeDtypeStruct(full_shapes[a], shards[a].dtype) for a in range(na) for _ in range(nl)]
    out_shape += [jax.ShapeDtypeStruct((conv_shard.shape[1], conv_shard.shape[2] * 4), conv_shard.dtype)] * nl
    outs = pl.pallas_call(
        body, name="allgather_weights", in_specs=[HB] * (na + 1), out_specs=[HB] * len(out_shape), out_shape=out_shape,
        scratch_shapes=[pltpu.SemaphoreType.DMA((ncopy,))] * 4 + [pltpu.SemaphoreType.DMA((nl * 3,))] * 2
        + [pltpu.SemaphoreType.DMA((na * nl + nl,))],
        compiler_params=pltpu.CompilerParams(has_side_effects=True),
    )(*shards, conv_shard)
    return [[outs[a * nl + l] for a in range(na)] for l in range(nl)], outs[na * nl:]


def _half_shape(full, axis):
    r, c = full
    return (4, r // 8, c) if axis == 0 else (r // 2, c)


def _half_region(ref, axis, half):
    if axis == 0:
        return ref.at[:, half]
    r = ref.shape[0]
    return ref.at[_rows(half * (r // 2), r // 2), :]


def _sibling_exchange(grads, axes):
    nl, na = len(grads), len(grads[0])
    views = []
    for l in range(nl):
        for a in range(na):
            g = grads[l][a]
            views.append(g.reshape(4, 2, g.shape[0] // 8, g.shape[1]) if axes[a] == 0 else g)

    def body(*refs):
        ins = refs[:nl * na]
        own, land = refs[nl * na:nl * na + na], refs[nl * na + na:nl * na + 2 * na]
        send, recv, local = refs[nl * na + 2 * na:]
        x, y, c, _ = _place()
        sibling = (x, y, 1 - c)
        copies = []
        for l in range(nl):
            for a in range(na):
                idx = l * na + a
                cp = pltpu.make_async_copy(_half_region(ins[idx], axes[a], c), own[a].at[l], local.at[idx])
                cp.start()
                copies.append(cp)
                cp = _remote(_half_region(ins[idx], axes[a], 1 - c), land[a].at[l], send.at[idx], recv.at[idx], sibling)
                cp.start()
                copies.append(cp)
        for cp in copies:
            cp.wait()

    halves = [jax.ShapeDtypeStruct((nl,) + _half_shape(grads[0][a].shape, axes[a]), F32) for a in range(na)]
    outs = pl.pallas_call(
        body, name="grad_sibling_exchange", in_specs=[HB] * (nl * na), out_specs=[HB] * (2 * na), out_shape=halves * 2,
        scratch_shapes=[pltpu.SemaphoreType.DMA((nl * na,))] * 3,
        compiler_params=pltpu.CompilerParams(has_side_effects=True),
    )(*views)
    return outs[:na], outs[na:]


def _add_cast(a, b, name):
    shape = a.shape
    cols = shape[-1]
    rows = a.size // cols
    rb = _row_block(rows, cols)

    def body(a_ref, b_ref, o_ref):
        o_ref[...] = (a_ref[...] + b_ref[...]).astype(_PAY)

    blk = pl.BlockSpec((rb, cols), lambda i: (i, 0))
    out = pl.pallas_call(
        body, name=name, grid=(rows // rb,), in_specs=[blk, blk], out_specs=blk,
        out_shape=jax.ShapeDtypeStruct((rows, cols), _PAY), compiler_params=_params(("parallel",)),
    )(a.reshape(rows, cols), b.reshape(rows, cols))
    return out.reshape(shape)


def _piece(ref, axis, chip):
    if axis == 0:
        return ref.at[:, chip]
    cs = ref.shape[2] // 4
    return ref.at[:, :, _rows(chip * cs, cs, 128)]


def _chip_scatter(sums, axes):
    na = len(sums)

    def piece_shape(a):
        nl = sums[a].shape[0]
        if axes[a] == 0:
            return (nl, sums[a].shape[2], sums[a].shape[3])
        return (nl, sums[a].shape[1], sums[a].shape[2] // 4)

    def body(*refs):
        ins, slots = refs[:na], refs[na:2 * na]
        send, recv, local = refs[2 * na:]
        x, y, c, chips = _place()
        copies = []
        for a in range(na):
            cp = pltpu.make_async_copy(_piece(ins[a], axes[a], 2 * x + y), slots[a].at[3], local.at[a])
            cp.start()
            copies.append(cp)
            for k, chip in enumerate(chips):
                cp = _remote(_piece(ins[a], axes[a], 2 * chip[0] + chip[1]), slots[a].at[k], send.at[a * 3 + k], recv.at[a * 3 + k], (*chip, c))
                cp.start()
                copies.append(cp)
        for cp in copies:
            cp.wait()

    outs = pl.pallas_call(
        body, name="grad_chip_scatter", in_specs=[HB] * na, out_specs=[HB] * na,
        out_shape=[jax.ShapeDtypeStruct((4,) + piece_shape(a), sums[a].dtype) for a in range(na)],
        scratch_shapes=[pltpu.SemaphoreType.DMA((na * 3,))] * 2 + [pltpu.SemaphoreType.DMA((na,))],
        compiler_params=pltpu.CompilerParams(has_side_effects=True),
    )(*sums)
    return outs


def _sum_slots(slots, name):
    shape = slots.shape[1:]
    cols = shape[-1]
    rows = slots.size // (4 * cols)
    rb = _row_block(rows, cols)

    def body(s_ref, o_ref):
        o_ref[...] = ((s_ref[3].astype(F32) + s_ref[0].astype(F32)) + s_ref[1].astype(F32)) + s_ref[2].astype(F32)

    out = pl.pallas_call(
        body, name=name, grid=(rows // rb,), in_specs=[pl.BlockSpec((4, rb, cols), lambda i: (0, i, 0))],
        out_specs=pl.BlockSpec((rb, cols), lambda i: (i, 0)),
        out_shape=jax.ShapeDtypeStruct((rows, cols), F32), compiler_params=_params(("parallel",)),
    )(slots.reshape(4, rows, cols))
    return out.reshape(shape)


def _sibling_assemble(pieces):
    na = len(pieces)

    def body(*refs):
        ins, outs = refs[:na], refs[na:2 * na]
        send, recv, local = refs[2 * na:]
        x, y, c, _ = _place()
        copies = []
        for a in range(na):
            hr = ins[a].shape[1]
            mine = outs[a].at[:, _rows(c * hr, hr), :]
            cp = pltpu.make_async_copy(ins[a], mine, local.at[a])
            cp.start()
            copies.append(cp)
            cp = _remote(ins[a], mine, send.at[a], recv.at[a], (x, y, 1 - c))
            cp.start()
            copies.append(cp)
        for cp in copies:
            cp.wait()

    outs = pl.pallas_call(
        body, name="grad_sibling_assemble", in_specs=[HB] * na, out_specs=[HB] * na,
        out_shape=[jax.ShapeDtypeStruct((p.shape[0], p.shape[1] * 2, p.shape[2]), F32) for p in pieces],
        scratch_shapes=[pltpu.SemaphoreType.DMA((na,))] * 3,
        compiler_params=pltpu.CompilerParams(has_side_effects=True),
    )(*pieces)
    return outs


def _allreduce_small(buf):
    rows, w = buf.shape
    half = rows // 2

    def body(buf_ref, out_ref, land, slots, red, sems_send, sems_recv):
        x, y, c, chips = _place()
        me = 2 * x + y
        sibling = (x, y, 1 - c)
        first = _remote(buf_ref, land, sems_send.at[0], sems_recv.at[0], sibling)
        first.start()
        first.wait()
        mine = pl.ds(pl.multiple_of(c * half, 8), half)
        slots[me] = buf_ref[mine, :] + land[mine, :]
        sends = []
        for k, chip in enumerate(chips):
            cp = _remote(slots.at[me], slots.at[me], sems_send.at[1 + k], sems_recv.at[1 + k], (*chip, c))
            cp.start()
            sends.append(cp)
        for k, chip in enumerate(chips):
            got = slots.at[2 * chip[0] + chip[1]]
            _remote(got, got, sems_send.at[1 + k], sems_recv.at[1 + k], sibling).wait_recv()
        red[...] = ((slots[0] + slots[1]) + slots[2]) + slots[3]
        out_ref[mine, :] = red[...]
        last = _remote(red, out_ref.at[mine, :], sems_send.at[4], sems_recv.at[4], sibling)
        last.start()
        theirs = out_ref.at[pl.ds(pl.multiple_of((1 - c) * half, 8), half), :]
        _remote(red, theirs, sems_send.at[4], sems_recv.at[4], sibling).wait_recv()
        for cp in sends:
            cp.wait_send()
        last.wait_send()

    return pl.pallas_call(
        body, name="allreduce_small", in_specs=[VM], out_specs=VM, out_shape=jax.ShapeDtypeStruct((rows, w), F32),
        scratch_shapes=[pltpu.VMEM((rows, w), F32), pltpu.VMEM((4, half, w), F32), pltpu.VMEM((half, w), F32),
                        pltpu.SemaphoreType.DMA((5,)), pltpu.SemaphoreType.DMA((5,))],
        compiler_params=pltpu.CompilerParams(has_side_effects=True, vmem_limit_bytes=VMEM_LIMIT),
    )(buf)


BIG = ("w_in", "w_out", "wq", "wk", "wv", "wo", "w_up", "w_down")
BIG_AXIS = {"w_in": 1, "w_out": 0, "wq": 0, "wk": 0, "wv": 0, "wo": 0, "w_up": 1, "w_down": 0}
SMALL = ("norm_mix_g", "pool_w", "pool_scale", "sgu_g", "sgu_w", "sgu_b", "norm_xattn_g", "mem_norm_g", "norm_ffn_g",
         "conv_w", "conv_b", "final_norm_g")
ORDER = ("norm_mix_g", "w_in", "pool_w", "pool_scale", "sgu_g", "sgu_w", "sgu_b", "w_out", "norm_xattn_g", "mem_norm_g",
         "wq", "wk", "wv", "wo", "norm_ffn_g", "w_up", "conv_w", "conv_b", "w_down", "final_norm_g")
PACK_WIDTH = 512


def kernel(x, mem, norm_mix_g, w_in, pool_w, pool_scale, sgu_g, sgu_w, sgu_b, w_out, norm_xattn_g, mem_norm_g, wq, wk, wv, wo, norm_ffn_g, w_up, conv_w, conv_b, w_down, final_norm_g, loss_target, m_norm_mix_g, m_w_in, m_pool_w, m_pool_scale, m_sgu_g, m_sgu_w, m_sgu_b, m_w_out, m_norm_xattn_g, m_mem_norm_g, m_wq, m_wk, m_wv, m_wo, m_norm_ffn_g, m_w_up, m_conv_w, m_conv_b, m_w_down, m_final_norm_g, v_norm_mix_g, v_w_in, v_pool_w, v_pool_scale, v_sgu_g, v_sgu_w, v_sgu_b, v_w_out, v_norm_xattn_g, v_mem_norm_g, v_wq, v_wk, v_wv, v_wo, v_norm_ffn_g, v_w_up, v_conv_w, v_conv_b, v_w_down, v_final_norm_g):
    given = dict(locals())
    w = {n: given[n] for n in ORDER}
    mom = {n: given["m_" + n] for n in ORDER}
    var = {n: given["v_" + n] for n in ORDER}
    nl = w_in.shape[0]
    xs, mems, tgt = x[0], mem[0], loss_target[0]
    chip = 2 * lax.axis_index("x") + lax.axis_index("y")

    axes = [BIG_AXIS[n] for n in BIG]
    full, conv_full = _allgather_weights([w[n].astype(_PAY) for n in BIG], axes, conv_w)
    full = [dict(zip(BIG, (a.astype(_MXU) for a in layer))) for layer in full]

    row = lambda a, l: a[l][None, :]
    saved = []
    h = xs
    for l in range(nl):
        fw = full[l]
        sbt = jnp.broadcast_to(sgu_b[l][:, :, None], sgu_w[l].shape)
        h1, proj, xn1, mix = _mixer_fwd(h, row(norm_mix_g, l), fw["w_in"], pool_w[l], row(pool_scale, l), row(sgu_g, l), sgu_w[l], sbt, fw["w_out"])
        k, v, memn = _kv_fwd(mems, row(mem_norm_g, l), fw["wk"], fw["wv"])
        h2, q, o, xn2 = _xattn_fwd(h1, row(norm_xattn_g, l), fw["wq"], k, v, fw["wo"])
        h3, hh, xn3 = _ffn_fwd(h2, row(norm_ffn_g, l), fw["w_up"], conv_full[l], row(conv_b, l), fw["w_down"])
        saved.append(dict(h=h, h1=h1, h2=h2, proj=proj, xn1=xn1, mix=mix, k=k, v=v, memn=memn, q=q, o=o, xn2=xn2, hh=hh, xn3=xn3, sbt=sbt))
        h = h3

    dh, loss_part, g_final = _loss_head(h, final_norm_g[None, :], tgt)

    big_grads = [None] * nl
    small_grads = [None] * nl
    for l in reversed(range(nl)):
        fw, s = full[l], saved[l]
        dh3 = dh
        dh2, dhh, act, g_cw, g_cb, g_nf = _ffn_bwd(dh3, s["h2"], s["hh"], row(norm_ffn_g, l), fw["w_up"], conv_full[l], row(conv_b, l), fw["w_down"])
        g_up = _grad_matmul(s["xn3"], dhh, "grad_w_up")
        g_down = _grad_matmul(act, dh3, "grad_w_down")
        dh1, dq, dk, dv, g_nx = _xattn_bwd(dh2, s["h1"], s["q"], row(norm_xattn_g, l), fw["wq"], s["k"], s["v"], fw["wo"])
        g_q = _grad_matmul(s["xn2"], dq, "grad_wq")
        g_o = _grad_matmul(s["o"], dh2, "grad_wo")
        g_k = _grad_matmul(s["memn"], dk, "grad_wk")
        g_v = _grad_matmul(s["memn"], dv, "grad_wv")
        g_mn = _kv_bwd(dk, dv, mems, fw["wk"], fw["wv"])
        dh0, dproj, g_nm, g_pw, g_ps, g_sg, g_sw, g_sbt = _mixer_bwd(dh1, s["h"], s["proj"], row(norm_mix_g, l), fw["w_in"], pool_w[l], row(pool_scale, l), row(sgu_g, l), sgu_w[l], s["sbt"], fw["w_out"])
        g_in = _grad_matmul(s["xn1"], dproj, "grad_w_in")
        g_out = _grad_matmul(s["mix"], dh1, "grad_w_out")
        big_grads[l] = dict(w_in=g_in, w_out=g_out, wq=g_q, wk=g_k, wv=g_v, wo=g_o, w_up=g_up, w_down=g_down)
        small_grads[l] = dict(norm_mix_g=g_nm, pool_w=g_pw, pool_scale=g_ps, sgu_g=g_sg, sgu_w=g_sw, sgu_b=jnp.sum(g_sbt, axis=-1),
                              norm_xattn_g=g_nx, mem_norm_g=g_mn, norm_ffn_g=g_nf, conv_w=g_cw, conv_b=g_cb)
        dh = dh0
    grad_x = dh[None]

    own, land = _sibling_exchange([[big_grads[l][n] for n in BIG] for l in range(nl)], axes)
    sums = [_add_cast(a, b, "grad_chip_sum_" + n) for a, b, n in zip(own, land, BIG)]
    slots = _chip_scatter(sums, axes)
    pieces = [_sum_slots(s, "grad_sum_" + n) for s, n in zip(slots, BIG)]
    shard_grads = dict(zip(BIG, _sibling_assemble(pieces)))

    layered = [n for n in SMALL if n != "final_norm_g"]
    parts = [small_grads[l][n].reshape(-1, PACK_WIDTH) for n in layered for l in range(nl)]
    parts.append(g_final.reshape(-1, PACK_WIDTH))
    parts.append(jnp.pad(loss_part, ((0, 0), (0, PACK_WIDTH - 1))))
    used = sum(p.shape[0] for p in parts)
    total = -(-used // 16) * 16
    packed = _allreduce_small(jnp.concatenate(parts + [jnp.zeros((total - used, PACK_WIDTH), F32)], axis=0))
    grads = dict(shard_grads)
    at = 0
    for n in layered:
        per_layer = []
        for l in range(nl):
            shape = small_grads[l][n].shape
            nrow = small_grads[l][n].size // PACK_WIDTH
            per_layer.append(packed[at:at + nrow].reshape(shape))
            at += nrow
        g = jnp.stack(per_layer)
        if n == "conv_w":
            cs = conv_w.shape[2]
            g = lax.dynamic_slice_in_dim(g, chip * cs, cs, axis=2)
        grads[n] = g.reshape(w[n].shape)
    grads["final_norm_g"] = packed[at:at + g_final.size // PACK_WIDTH].reshape(final_norm_g.shape)
    at += g_final.size // PACK_WIDTH
    loss = packed[at, 0]

    delta, new_m, new_v = {}, {}, {}
    for n in BIG:
        delta[n], new_m[n], new_v[n] = _adamw_big(w[n], grads[n], mom[n], var[n], "adamw_" + n)
    two_d = lambda a: a.reshape(-1, a.shape[-1])
    ds, nms, nvs = _adamw_small([two_d(w[n]) for n in SMALL], [two_d(grads[n]) for n in SMALL],
                                [two_d(mom[n]) for n in SMALL], [two_d(var[n]) for n in SMALL])
    for n, d_, m_, v_ in zip(SMALL, ds, nms, nvs):
        delta[n], new_m[n], new_v[n] = d_.reshape(w[n].shape), m_.reshape(w[n].shape), v_.reshape(w[n].shape)

    return (loss, grad_x, *[grads[n] for n in ORDER], *[delta[n] for n in ORDER], *[new_m[n] for n in ORDER], *[new_v[n] for n in ORDER])
```

```python
import functools
import math

import jax
import jax.numpy as jnp
from jax import lax
from jax.experimental import pallas as pl
from jax.experimental.pallas import tpu as pltpu

F32 = jnp.float32
_MXU = jnp.bfloat16
_PAY = jnp.bfloat16
EPS = 1e-6
WINDOWS = (2, 4, 8, 16)
GROUP = 128
N_XHEADS = 4
HALO = 16
FF_TILE = 256
VMEM_LIMIT = 56 * 1024 * 1024
MESH = pl.DeviceIdType.MESH

ADAM_LR, ADAM_B1, ADAM_B2, ADAM_EPS, ADAM_WD, ADAM_STEP = 0.001, 0.9, 0.999, 1e-08, 0.01, 10

VM = pl.BlockSpec(memory_space=pltpu.VMEM)
HB = pl.BlockSpec(memory_space=pltpu.HBM)


def _nn(a, b):
    return jnp.dot(a, b, preferred_element_type=F32)


def _nt(a, b):
    return lax.dot_general(a, b, (((1,), (1,)), ((), ())), preferred_element_type=F32)


def _tn(a, b):
    return lax.dot_general(a, b, (((0,), (0,)), ((), ())), preferred_element_type=F32)


def _rms(x):
    r = lax.rsqrt(jnp.mean(x * x, axis=-1, keepdims=True) + EPS)
    return x * r, r


def _rms_bwd(dxn, xhat, r, g):
    dxh = dxn * g
    dx = r * (dxh - xhat * jnp.mean(dxh * xhat, axis=-1, keepdims=True))
    return dx, jnp.sum(dxn * xhat, axis=0, keepdims=True)


def _gelu(x):
    return 0.5 * x * (1.0 + lax.erf(x * (2.0 ** -0.5)))


def _gelu_grad(x):
    return 0.5 * (1.0 + lax.erf(x * (2.0 ** -0.5))) + x * jnp.exp(-0.5 * x * x) * ((2.0 * math.pi) ** -0.5)


def _params(sem=None):
    return pltpu.CompilerParams(dimension_semantics=sem, vmem_limit_bytes=VMEM_LIMIT)


def _token_block(t, want):
    return want if t % want == 0 and t > want else GROUP


def _const_spec(shape):
    n = len(shape)
    return pl.BlockSpec(shape, lambda i: (0,) * n)


def _tril():
    return lax.broadcasted_iota(jnp.int32, (GROUP, GROUP), 0) >= lax.broadcasted_iota(jnp.int32, (GROUP, GROUP), 1)


def _pool_diff(pext, p, t0, tb, gi, win):
    sl = slice(gi * GROUP, (gi + 1) * GROUP)
    s = p[:, sl]
    for k in range(1, win):
        s = s + pext[HALO - k:HALO - k + tb, sl]
    tglob = t0 + lax.broadcasted_iota(jnp.int32, (tb, 1), 0)
    cnt = jnp.minimum(tglob + 1, win).astype(F32)
    return s / cnt - p[:, sl], cnt


def _layernorm(v):
    xc = v - jnp.mean(v, axis=-1, keepdims=True)
    rstd = lax.rsqrt(jnp.mean(xc * xc, axis=-1, keepdims=True) + EPS)
    return xc * rstd, rstd


def _mixer_fwd(h, g, w_in, pool_w, pool_scale, sgu_g, sgu_w, sgu_bt, w_out):
    t, d = h.shape
    pw = pool_w.shape[0] * GROUP
    sw = sgu_w.shape[0] * GROUP
    tb = _token_block(t, 512)

    def body(h_ref, g_ref, win_ref, pw_ref, ps_ref, sg_ref, sw_ref, sbt_ref, wout_ref, h1_ref, proj_ref, xn_ref, mix_ref, pext):
        i = pl.program_id(0)

        @pl.when(i == 0)
        def _():
            pext[0:HALO, :] = jnp.zeros((HALO, pw), F32)

        x = h_ref[...]
        xhat, _ = _rms(x)
        xn = (xhat * g_ref[...]).astype(_MXU)
        xn_ref[...] = xn
        proj = _nn(xn, win_ref[...])
        proj_ref[...] = proj
        p = proj[:, :pw]
        pext[HALO:HALO + tb, :] = p
        for gi, win in enumerate(WINDOWS):
            sl = slice(gi * GROUP, (gi + 1) * GROUP)
            dg, _ = _pool_diff(pext, p, i * tb, tb, gi, win)
            e = _nn(dg.astype(_MXU), pw_ref[gi].astype(_MXU))
            mix_ref[:, sl] = (e * ps_ref[:, sl]).astype(_MXU)
        pext[0:HALO, :] = p[tb - HALO:tb, :]
        uv = _gelu(proj[:, pw:])
        u = uv[:, :sw]
        vhat, _ = _layernorm(uv[:, sw:])
        vn = (vhat * sg_ref[...]).astype(_MXU)
        mask = _tril()
        for hh in range(sw // GROUP):
            wm = jnp.where(mask, sw_ref[hh], 0.0).astype(_MXU)
            for n in range(tb // GROUP):
                rows = slice(n * GROUP, (n + 1) * GROUP)
                cols = slice(hh * GROUP, (hh + 1) * GROUP)
                z = _nn(wm, vn[rows, cols]) + sbt_ref[hh]
                mix_ref[rows, pw + hh * GROUP:pw + (hh + 1) * GROUP] = (u[rows, cols] * z).astype(_MXU)
        h1_ref[...] = x + _nn(mix_ref[...], wout_ref[...])

    blk = lambda w: pl.BlockSpec((tb, w), lambda i: (i, 0))
    return pl.pallas_call(
        body, name="mixer_fwd", grid=(t // tb,),
        in_specs=[blk(d), VM, VM, VM, VM, VM, VM, VM, VM],
        out_specs=[blk(d), blk(w_in.shape[1]), blk(d), blk(d)],
        out_shape=[jax.ShapeDtypeStruct((t, d), F32), jax.ShapeDtypeStruct((t, w_in.shape[1]), F32),
                   jax.ShapeDtypeStruct((t, d), _MXU), jax.ShapeDtypeStruct((t, d), _MXU)],
        scratch_shapes=[pltpu.VMEM((HALO + tb, pw), F32)],
        compiler_params=_params(("arbitrary",)),
    )(h, g, w_in, pool_w, pool_scale, sgu_g, sgu_w, sgu_bt, w_out)


def _mixer_bwd(dh1, h, proj, g, w_in, pool_w, pool_scale, sgu_g, sgu_w, sgu_bt, w_out):
    t, d = h.shape
    ng, nh = pool_w.shape[0], sgu_w.shape[0]
    pw, sw = ng * GROUP, nh * GROUP
    tb = _token_block(t, 256)
    nb = t // tb

    def body(dh1_ref, h_ref, proj_ref, halo_ref, g_ref, win_ref, pw_ref, ps_ref, sg_ref, sw_ref, sbt_ref, wout_ref,
             dh_ref, dproj_ref, gg_ref, gpw_ref, gps_ref, gsg_ref, gsw_ref, gsbt_ref, pext, dext, duv):
        i = pl.program_id(0)
        blk = nb - 1 - i

        @pl.when(i == 0)
        def _():
            for r in (gg_ref, gpw_ref, gps_ref, gsg_ref, gsw_ref, gsbt_ref):
                r[...] = jnp.zeros(r.shape, F32)
            dext[tb:tb + HALO, :] = jnp.zeros((HALO, pw), F32)

        dh1v = dh1_ref[...]
        dmix = _nt(dh1v.astype(_MXU), wout_ref[...])
        proj_v = proj_ref[...]
        p = proj_v[:, :pw]
        pext[0:HALO, :] = jnp.where(blk == 0, 0.0, halo_ref[...])
        pext[HALO:HALO + tb, :] = p
        for gi, win in enumerate(WINDOWS):
            sl = slice(gi * GROUP, (gi + 1) * GROUP)
            dg, cnt = _pool_diff(pext, p, blk * tb, tb, gi, win)
            dgm = dg.astype(_MXU)
            pwm = pw_ref[gi].astype(_MXU)
            e = _nn(dgm, pwm)
            dy = dmix[:, sl]
            gps_ref[:, sl] += jnp.sum(dy * e, axis=0, keepdims=True)
            de = (dy * ps_ref[:, sl]).astype(_MXU)
            gpw_ref[gi] += _tn(dgm, de)
            dd = _nt(de, pwm)
            ddc = dd / cnt
            dext[0:tb, sl] = ddc
            acc = ddc
            for k in range(1, win):
                acc = acc + dext[k:k + tb, sl]
            dext[tb:tb + HALO, sl] = ddc[0:HALO, :]
            dproj_ref[:, sl] = (acc - dd).astype(_MXU)
        pre = proj_v[:, pw:]
        uv = _gelu(pre)
        u = uv[:, :sw]
        vhat, rstd = _layernorm(uv[:, sw:])
        vn = (vhat * sg_ref[...]).astype(_MXU)
        mask = _tril()
        for hh in range(nh):
            wm = jnp.where(mask, sw_ref[hh], 0.0).astype(_MXU)
            cols = slice(hh * GROUP, (hh + 1) * GROUP)
            gw = jnp.zeros((GROUP, GROUP), F32)
            gb = jnp.zeros((GROUP, GROUP), F32)
            for n in range(tb // GROUP):
                rows = slice(n * GROUP, (n + 1) * GROUP)
                vs = vn[rows, cols]
                z = _nn(wm, vs) + sbt_ref[hh]
                dy = dmix[rows, pw + hh * GROUP:pw + (hh + 1) * GROUP]
                dz = dy * u[rows, cols]
                gb = gb + dz
                dzm = dz.astype(_MXU)
                gw = gw + _nt(dzm, vs)
                duv[rows, cols] = dy * z
                duv[rows, sw + hh * GROUP:sw + (hh + 1) * GROUP] = _tn(wm, dzm)
            gsw_ref[hh] += jnp.where(mask, gw, 0.0)
            gsbt_ref[hh] += gb
        dvn = duv[:, sw:]
        gsg_ref[...] += jnp.sum(dvn * vhat, axis=0, keepdims=True)
        dxh = dvn * sg_ref[...]
        dv = rstd * (dxh - jnp.mean(dxh, axis=-1, keepdims=True) - vhat * jnp.mean(dxh * vhat, axis=-1, keepdims=True))
        gp = _gelu_grad(pre)
        dproj_ref[:, pw:pw + sw] = (duv[:, :sw] * gp[:, :sw]).astype(_MXU)
        dproj_ref[:, pw + sw:] = (dv * gp[:, sw:]).astype(_MXU)
        dxn = _nt(dproj_ref[...], win_ref[...])
        xhat, r = _rms(h_ref[...])
        dx, gg = _rms_bwd(dxn, xhat, r, g_ref[...])
        gg_ref[...] += gg
        dh_ref[...] = dh1v + dx

    rev = lambda w: pl.BlockSpec((tb, w), lambda i: (nb - 1 - i, 0))
    halo = pl.BlockSpec((HALO, pw), lambda i: (jnp.maximum((nb - 1 - i) * (tb // HALO) - 1, 0), 0))
    small = [(1, d), (ng, GROUP, GROUP), (1, pw), (1, sw), (nh, GROUP, GROUP), (nh, GROUP, GROUP)]
    return pl.pallas_call(
        body, name="mixer_bwd", grid=(nb,),
        in_specs=[rev(d), rev(d), rev(proj.shape[1]), halo, VM, VM, VM, VM, VM, VM, VM, VM],
        out_specs=[rev(d), rev(proj.shape[1])] + [_const_spec(s) for s in small],
        out_shape=[jax.ShapeDtypeStruct((t, d), F32), jax.ShapeDtypeStruct(proj.shape, _MXU)]
        + [jax.ShapeDtypeStruct(s, F32) for s in small],
        scratch_shapes=[pltpu.VMEM((HALO + tb, pw), F32), pltpu.VMEM((tb + HALO, pw), F32), pltpu.VMEM((tb, 2 * sw), F32)],
        compiler_params=_params(("arbitrary",)),
    )(dh1, h, proj, proj, g, w_in, pool_w, pool_scale, sgu_g, sgu_w, sgu_bt, w_out)


def _kv_fwd(mem, gm, wk, wv):
    n, d = mem.shape

    def body(mem_ref, gm_ref, wk_ref, wv_ref, k_ref, v_ref, memn_ref):
        xhat, _ = _rms(mem_ref[...])
        memn = (xhat * gm_ref[...]).astype(_MXU)
        memn_ref[...] = memn
        k_ref[...] = _nn(memn, wk_ref[...]).astype(_MXU)
        v_ref[...] = _nn(memn, wv_ref[...]).astype(_MXU)

    return pl.pallas_call(
        body, name="kv_fwd", in_specs=[VM] * 4, out_specs=[VM] * 3,
        out_shape=[jax.ShapeDtypeStruct((n, d), _MXU)] * 3, compiler_params=_params(),
    )(mem, gm, wk, wv)


def _kv_bwd(dk, dv, mem, wk, wv):
    n, d = mem.shape

    def body(dk_ref, dv_ref, mem_ref, wk_ref, wv_ref, ggm_ref):
        dmemn = _nt(dk_ref[...].astype(_MXU), wk_ref[...]) + _nt(dv_ref[...].astype(_MXU), wv_ref[...])
        xhat, _ = _rms(mem_ref[...])
        ggm_ref[...] = jnp.sum(dmemn * xhat, axis=0, keepdims=True)

    return pl.pallas_call(
        body, name="kv_bwd", in_specs=[VM] * 5, out_specs=VM,
        out_shape=jax.ShapeDtypeStruct((1, d), F32), compiler_params=_params(),
    )(dk, dv, mem, wk, wv)


def _softmax_rows(qm, k_ref, sl, scale):
    s = _nt(qm, k_ref[:, sl]) * scale
    e = jnp.exp(s - jnp.max(s, axis=-1, keepdims=True))
    return e / jnp.sum(e, axis=-1, keepdims=True)


def _xattn_fwd(h, g, wq, k, v, wo):
    t, d = h.shape
    hd = d // N_XHEADS
    scale = hd ** -0.5
    tb = _token_block(t, 512)

    def body(h_ref, g_ref, wq_ref, k_ref, v_ref, wo_ref, h2_ref, q_ref, o_ref, xn_ref):
        x = h_ref[...]
        xhat, _ = _rms(x)
        xn = (xhat * g_ref[...]).astype(_MXU)
        xn_ref[...] = xn
        qm = _nn(xn, wq_ref[...]).astype(_MXU)
        q_ref[...] = qm
        for a in range(N_XHEADS):
            sl = slice(a * hd, (a + 1) * hd)
            pr = _softmax_rows(qm[:, sl], k_ref, sl, scale)
            o_ref[:, sl] = _nn(pr.astype(_MXU), v_ref[:, sl]).astype(_MXU)
        h2_ref[...] = x + _nn(o_ref[...], wo_ref[...])

    blk = pl.BlockSpec((tb, d), lambda i: (i, 0))
    return pl.pallas_call(
        body, name="xattn_fwd", grid=(t // tb,),
        in_specs=[blk, VM, VM, VM, VM, VM], out_specs=[blk] * 4,
        out_shape=[jax.ShapeDtypeStruct((t, d), F32)] + [jax.ShapeDtypeStruct((t, d), _MXU)] * 3,
        compiler_params=_params(("arbitrary",)),
    )(h, g, wq, k, v, wo)


def _xattn_bwd(dh2, h, q, g, wq, k, v, wo):
    t, d = h.shape
    n = k.shape[0]
    hd = d // N_XHEADS
    scale = hd ** -0.5
    tb = _token_block(t, 512)

    def body(dh2_ref, h_ref, q_ref, g_ref, wq_ref, k_ref, v_ref, wo_ref, dh_ref, dq_ref, dk_ref, dv_ref, gg_ref):
        @pl.when(pl.program_id(0) == 0)
        def _():
            for r in (dk_ref, dv_ref, gg_ref):
                r[...] = jnp.zeros(r.shape, F32)

        dh2v = dh2_ref[...]
        dom = _nt(dh2v.astype(_MXU), wo_ref[...]).astype(_MXU)
        for a in range(N_XHEADS):
            sl = slice(a * hd, (a + 1) * hd)
            qh = q_ref[:, sl]
            pr = _softmax_rows(qh, k_ref, sl, scale)
            dv_ref[:, sl] += _tn(pr.astype(_MXU), dom[:, sl])
            dpr = _nt(dom[:, sl], v_ref[:, sl])
            ds = (pr * (dpr - jnp.sum(dpr * pr, axis=-1, keepdims=True)) * scale).astype(_MXU)
            dq_ref[:, sl] = _nn(ds, k_ref[:, sl]).astype(_MXU)
            dk_ref[:, sl] += _tn(ds, qh)
        dxn = _nt(dq_ref[...], wq_ref[...])
        xhat, r = _rms(h_ref[...])
        dx, gg = _rms_bwd(dxn, xhat, r, g_ref[...])
        gg_ref[...] += gg
        dh_ref[...] = dh2v + dx

    blk = pl.BlockSpec((tb, d), lambda i: (i, 0))
    return pl.pallas_call(
        body, name="xattn_bwd", grid=(t // tb,),
        in_specs=[blk, blk, blk, VM, VM, VM, VM, VM],
        out_specs=[blk, blk, _const_spec((n, d)), _const_spec((n, d)), _const_spec((1, d))],
        out_shape=[jax.ShapeDtypeStruct((t, d), F32), jax.ShapeDtypeStruct((t, d), _MXU),
                   jax.ShapeDtypeStruct((n, d), F32), jax.ShapeDtypeStruct((n, d), F32), jax.ShapeDtypeStruct((1, d), F32)],
        compiler_params=_params(("arbitrary",)),
    )(dh2, h, q, g, wq, k, v, wo)


def _ffn_fwd(h, g, w_up, conv_w, conv_b, w_down):
    t, d = h.shape
    f = w_down.shape[0]
    ft = FF_TILE
    tb = _token_block(t, 512)

    def body(h_ref, g_ref, wup_ref, cw_ref, cb_ref, wdown_ref, h3_ref, hh_ref, xn_ref, ext, carry):
        @pl.when(pl.program_id(0) == 0)
        def _():
            carry[...] = jnp.zeros(carry.shape, F32)

        x = h_ref[...]
        xhat, _ = _rms(x)
        xn = (xhat * g_ref[...]).astype(_MXU)
        xn_ref[...] = xn
        acc = jnp.zeros((tb, d), F32)
        for j in range(f // ft):
            hc = []
            for part, off in enumerate((j * ft, f + j * ft)):
                cols = slice(off, off + ft)
                cur = _nn(xn, wup_ref[:, cols])
                hh_ref[:, cols] = cur.astype(_MXU)
                ext[part, 0:8, :] = carry[:, cols]
                ext[part, 8:8 + tb, :] = cur
                carry[:, cols] = cur[tb - 8:tb, :]
                hc.append(cb_ref[:, cols] + cw_ref[0:1, cols] * ext[part, 6:6 + tb, :]
                          + cw_ref[1:2, cols] * ext[part, 7:7 + tb, :] + cw_ref[2:3, cols] * cur)
            act = (hc[0] * jax.nn.sigmoid(hc[0]) * hc[1]).astype(_MXU)
            acc = acc + _nn(act, wdown_ref[j * ft:(j + 1) * ft, :])
        h3_ref[...] = x + acc

    blk = lambda w: pl.BlockSpec((tb, w), lambda i: (i, 0))
    return pl.pallas_call(
        body, name="ffn_fwd", grid=(t // tb,),
        in_specs=[blk(d), VM, VM, VM, VM, VM], out_specs=[blk(d), blk(2 * f), blk(d)],
        out_shape=[jax.ShapeDtypeStruct((t, d), F32), jax.ShapeDtypeStruct((t, 2 * f), _MXU), jax.ShapeDtypeStruct((t, d), _MXU)],
        scratch_shapes=[pltpu.VMEM((2, 8 + tb, ft), F32), pltpu.VMEM((8, 2 * f), F32)],
        compiler_params=_params(("arbitrary",)),
    )(h, g, w_up, conv_w, conv_b, w_down)


def _ffn_bwd(dh3, h, hh, g, w_up, conv_w, conv_b, w_down):
    t, d = h.shape
    f = w_down.shape[0]
    ft = FF_TILE
    tb = _token_block(t, 256)
    nb = t // tb

    def body(dh3_ref, h_ref, hh_ref, halo_ref, g_ref, wup_ref, cw_ref, cb_ref, wdown_ref,
             dh_ref, dhh_ref, act_ref, gcw_ref, gcb_ref, gg_ref, hext, dext, dcarry):
        i = pl.program_id(0)
        blk = nb - 1 - i

        @pl.when(i == 0)
        def _():
            for r in (gcw_ref, gcb_ref, gg_ref, dcarry):
                r[...] = jnp.zeros(r.shape, F32)

        dh3v = dh3_ref[...]
        dhm = dh3v.astype(_MXU)
        dxn = jnp.zeros((tb, d), F32)
        for j in range(f // ft):
            cur, back1, back2, hc = [], [], [], []
            for part, off in enumerate((j * ft, f + j * ft)):
                cols = slice(off, off + ft)
                c0 = hh_ref[:, cols].astype(F32)
                hext[part, 0:HALO, :] = jnp.where(blk == 0, 0.0, halo_ref[:, cols].astype(F32))
                hext[part, HALO:HALO + tb, :] = c0
                b1 = hext[part, HALO - 1:HALO - 1 + tb, :]
                b2 = hext[part, HALO - 2:HALO - 2 + tb, :]
                cur.append(c0)
                back1.append(b1)
                back2.append(b2)
                hc.append(cb_ref[:, cols] + cw_ref[0:1, cols] * b2 + cw_ref[1:2, cols] * b1 + cw_ref[2:3, cols] * c0)
            sg = jax.nn.sigmoid(hc[0])
            silu = hc[0] * sg
            act_ref[:, j * ft:(j + 1) * ft] = (silu * hc[1]).astype(_MXU)
            dact = _nt(dhm, wdown_ref[j * ft:(j + 1) * ft, :])
            dhc = (dact * hc[1] * sg * (1.0 + hc[0] * (1.0 - sg)), dact * silu)
            for part, off in enumerate((j * ft, f + j * ft)):
                cols = slice(off, off + ft)
                dc = dhc[part]
                gcb_ref[:, cols] += jnp.sum(dc, axis=0, keepdims=True)
                gcw_ref[0:1, cols] += jnp.sum(dc * back2[part], axis=0, keepdims=True)
                gcw_ref[1:2, cols] += jnp.sum(dc * back1[part], axis=0, keepdims=True)
                gcw_ref[2:3, cols] += jnp.sum(dc * cur[part], axis=0, keepdims=True)
                dext[part, 0:tb, :] = dc
                dext[part, tb:tb + 8, :] = dcarry[:, cols]
                dhh = (cw_ref[2:3, cols] * dc + cw_ref[1:2, cols] * dext[part, 1:1 + tb, :]
                       + cw_ref[0:1, cols] * dext[part, 2:2 + tb, :]).astype(_MXU)
                dcarry[:, cols] = dc[0:8, :]
                dhh_ref[:, cols] = dhh
                dxn = dxn + _nt(dhh, wup_ref[:, cols])
        xhat, r = _rms(h_ref[...])
        dx, gg = _rms_bwd(dxn, xhat, r, g_ref[...])
        gg_ref[...] += gg
        dh_ref[...] = dh3v + dx

    rev = lambda w: pl.BlockSpec((tb, w), lambda i: (nb - 1 - i, 0))
    halo = pl.BlockSpec((HALO, 2 * f), lambda i: (jnp.maximum((nb - 1 - i) * (tb // HALO) - 1, 0), 0))
    return pl.pallas_call(
        body, name="ffn_bwd", grid=(nb,),
        in_specs=[rev(d), rev(d), rev(2 * f), halo, VM, VM, VM, VM, VM],
        out_specs=[rev(d), rev(2 * f), rev(f), _const_spec((3, 2 * f)), _const_spec((1, 2 * f)), _const_spec((1, d))],
        out_shape=[jax.ShapeDtypeStruct((t, d), F32), jax.ShapeDtypeStruct((t, 2 * f), _MXU), jax.ShapeDtypeStruct((t, f), _MXU),
                   jax.ShapeDtypeStruct((3, 2 * f), F32), jax.ShapeDtypeStruct((1, 2 * f), F32), jax.ShapeDtypeStruct((1, d), F32)],
        scratch_shapes=[pltpu.VMEM((2, HALO + tb, ft), F32), pltpu.VMEM((2, tb + 8, ft), F32), pltpu.VMEM((8, 2 * f), F32)],
        compiler_params=_params(("arbitrary",)),
    )(dh3, h, hh, hh, g, w_up, conv_w, conv_b, w_down)


def _loss_head(h, g, target):
    t, d = h.shape
    tb = _token_block(t, 512)

    def body(h_ref, g_ref, tgt_ref, dh_ref, loss_ref, gg_ref):
        @pl.when(pl.program_id(0) == 0)
        def _():
            loss_ref[...] = jnp.zeros(loss_ref.shape, F32)
            gg_ref[...] = jnp.zeros(gg_ref.shape, F32)

        xhat, r = _rms(h_ref[...])
        err = xhat * g_ref[...] - tgt_ref[...]
        loss_ref[...] += 0.5 * jnp.sum(jnp.sum(err * err, axis=-1, keepdims=True), axis=0, keepdims=True) / d
        dx, gg = _rms_bwd(err / d, xhat, r, g_ref[...])
        gg_ref[...] += gg
        dh_ref[...] = dx

    blk = pl.BlockSpec((tb, d), lambda i: (i, 0))
    return pl.pallas_call(
        body, name="loss_head", grid=(t // tb,),
        in_specs=[blk, VM, blk], out_specs=[blk, _const_spec((1, 1)), _const_spec((1, d))],
        out_shape=[jax.ShapeDtypeStruct((t, d), F32), jax.ShapeDtypeStruct((1, 1), F32), jax.ShapeDtypeStruct((1, d), F32)],
        compiler_params=_params(("arbitrary",)),
    )(h, g, target)


def _largest_tile(n, cap, mult=128):
    best = None
    for c in range(mult, min(n, cap) + 1, mult):
        if n % c == 0:
            best = c
    return best if best is not None else n


def _grad_matmul(a, b, name, layer, n_layers, into=None):
    t, m = a.shape
    n = b.shape[1]
    tm, tn, tk = _largest_tile(m, 1408), _largest_tile(n, 1024), _largest_tile(t, 1024)
    nk = t // tk

    def body(a_ref, b_ref, *rest):
        o_ref = rest[-1]

        @pl.when(pl.program_id(2) == 0)
        def _():
            o_ref[...] = jnp.zeros(o_ref.shape, F32)

        o_ref[...] += _tn(a_ref[...].astype(_MXU), b_ref[...].astype(_MXU))

    in_specs = [pl.BlockSpec((tk, tm), lambda i, j, k: (k, i)), pl.BlockSpec((tk, tn), lambda i, j, k: (k, j))]
    operands = (a, b)
    aliases = {}
    if into is not None:
        in_specs.append(pl.BlockSpec(memory_space=pl.ANY))
        operands = (a, b, into)
        aliases = {2: 0}
    return pl.pallas_call(
        body, name=name, grid=(m // tm, n // tn, nk), in_specs=in_specs,
        out_specs=pl.BlockSpec((None, tm, tn), lambda i, j, k: (layer, i, j)),
        out_shape=jax.ShapeDtypeStruct((n_layers, m, n), F32), input_output_aliases=aliases,
        compiler_params=_params(("parallel", "parallel", "arbitrary")),
    )(*operands)


def _adamw_math(w, g, m, v):
    m = ADAM_B1 * m + (1.0 - ADAM_B1) * g
    v = ADAM_B2 * v + (1.0 - ADAM_B2) * (g * g)
    m_hat = m / (1.0 - ADAM_B1 ** ADAM_STEP)
    v_hat = v / (1.0 - ADAM_B2 ** ADAM_STEP)
    return -ADAM_LR * (m_hat / (jnp.sqrt(v_hat) + ADAM_EPS) + ADAM_WD * w), m, v


def _row_block(rows, cols, max_bytes=1 << 20, mult=16):
    best = None
    for r in range(mult, rows + 1, mult):
        if rows % r == 0 and r * cols * 4 <= max_bytes:
            best = r
    return best if best is not None else rows


def _adamw_big(w, g, m, v, name):
    shape = w.shape
    cols = shape[-1]
    flat = lambda a: a.reshape(-1, cols)
    rows = flat(w).shape[0]
    rb = _row_block(rows, cols)

    def body(w_ref, g_ref, m_ref, v_ref, d_ref, nm_ref, nv_ref):
        d_ref[...], nm_ref[...], nv_ref[...] = _adamw_math(w_ref[...], g_ref[...], m_ref[...], v_ref[...])

    blk = pl.BlockSpec((rb, cols), lambda i: (i, 0))
    outs = pl.pallas_call(
        body, name=name, grid=(rows // rb,), in_specs=[blk] * 4, out_specs=[blk] * 3,
        out_shape=[jax.ShapeDtypeStruct((rows, cols), F32)] * 3, compiler_params=_params(("parallel",)),
    )(flat(w), flat(g), flat(m), flat(v))
    return [o.reshape(shape) for o in outs]


def _adamw_small(ws, gs, ms, vs):
    n = len(ws)

    def body(*refs):
        for a in range(n):
            w_ref, g_ref, m_ref, v_ref = (refs[s * n + a] for s in range(4))
            d_ref, nm_ref, nv_ref = (refs[(4 + s) * n + a] for s in range(3))
            d_ref[...], nm_ref[...], nv_ref[...] = _adamw_math(w_ref[...], g_ref[...], m_ref[...], v_ref[...])

    outs = pl.pallas_call(
        body, name="adamw_small", in_specs=[VM] * (4 * n), out_specs=[VM] * (3 * n),
        out_shape=[jax.ShapeDtypeStruct(w.shape, F32) for w in ws] * 3, compiler_params=_params(),
    )(*ws, *gs, *ms, *vs)
    return outs[:n], outs[n:2 * n], outs[2 * n:]


def _place():
    x, y, c = lax.axis_index("x"), lax.axis_index("y"), lax.axis_index("c")
    chips = [(1 - x, y), (x, 1 - y), (1 - x, 1 - y)]
    return x, y, c, chips


def _rows(start, size, mult=16):
    return pl.ds(pl.multiple_of(start, mult), size)


def _full_window(ref, axis, chip, half=None):
    r, c = ref.shape
    if axis == 0:
        rs = r // 4
        if half is None:
            return ref.at[_rows(chip * rs, rs), :]
        return ref.at[_rows(chip * rs + half * (rs // 2), rs // 2), :]
    cs = c // 4
    if half is None:
        return ref.at[:, _rows(chip * cs, cs, 128)]
    return ref.at[_rows(half * (r // 2), r // 2), _rows(chip * cs, cs, 128)]


def _remote(src, dst, send_sem, recv_sem, to):
    return pltpu.make_async_remote_copy(src_ref=src, dst_ref=dst, send_sem=send_sem, recv_sem=recv_sem,
                                        device_id=to, device_id_type=MESH)


def _scalars(*vals):
    return jnp.stack([jnp.asarray(v, jnp.int32) for v in vals])


def _cast_place(shard, layer, axis, chip, name):
    _, rs, cs = shard.shape
    full = (rs * 4, cs) if axis == 0 else (rs, cs * 4)
    rb = _row_block(rs, cs)
    nrb = rs // rb

    def body(chip_ref, s_ref, o_ref):
        o_ref[...] = s_ref[...].astype(_PAY)

    if axis == 0:
        out_map = lambda i, chip_ref: (chip_ref[0] * nrb + i, 0)
    else:
        out_map = lambda i, chip_ref: (i, chip_ref[0])
    return pl.pallas_call(
        body, name=name,
        grid_spec=pltpu.PrefetchScalarGridSpec(
            num_scalar_prefetch=1, grid=(nrb,),
            in_specs=[pl.BlockSpec((None, rb, cs), lambda i, chip_ref: (layer, i, 0))],
            out_specs=pl.BlockSpec((rb, cs), out_map)),
        out_shape=jax.ShapeDtypeStruct(full, _PAY), compiler_params=_params(("parallel",)),
    )(_scalars(chip), shard)


def _allgather_weights(placed, axes, conv_shard):
    n = len(placed)
    nl = conv_shard.shape[0]

    def body(*refs):
        conv_in = refs[n]
        outs = refs[n + 1:2 * n + 1]
        conv_outs = refs[2 * n + 1:2 * n + 1 + nl]
        send, recv, fsend, frecv, csend, crecv, local = refs[2 * n + 1 + nl:]
        x, y, c, chips = _place()
        me = 2 * x + y
        sibling = (x, y, 1 - c)
        cs = conv_in.shape[2]
        started = []
        for l in range(nl):
            cp = pltpu.make_async_copy(conv_in.at[l], conv_outs[l].at[:, _rows(me * cs, cs, 128)], local.at[l])
            cp.start()
            started.append(cp)
        sends = []
        for i in range(n):
            mine = _full_window(outs[i], axes[i], me, c)
            for k, chip in enumerate(chips):
                cp = _remote(mine, mine, send.at[i * 3 + k], recv.at[i * 3 + k], (*chip, c))
                cp.start()
                sends.append(cp)
        for l in range(nl):
            for k, chip in enumerate(chips):
                cp = _remote(conv_in.at[l], conv_outs[l].at[:, _rows(me * cs, cs, 128)], csend.at[l * 3 + k], crecv.at[l * 3 + k], (*chip, c))
                cp.start()
                sends.append(cp)
        for i in range(n):
            for k, chip in enumerate(chips):
                got = _full_window(outs[i], axes[i], 2 * chip[0] + chip[1], c)
                _remote(got, got, send.at[i * 3 + k], recv.at[i * 3 + k], sibling).wait_recv()
                cp = _remote(got, got, fsend.at[i * 3 + k], frecv.at[i * 3 + k], sibling)
                cp.start()
                sends.append(cp)
        for i in range(n):
            for k, chip in enumerate(chips):
                got = _full_window(outs[i], axes[i], 2 * chip[0] + chip[1], 1 - c)
                _remote(got, got, fsend.at[i * 3 + k], frecv.at[i * 3 + k], sibling).wait_recv()
        for l in range(nl):
            for k, chip in enumerate(chips):
                got = conv_outs[l].at[:, _rows((2 * chip[0] + chip[1]) * cs, cs, 128)]
                _remote(got, got, csend.at[l * 3 + k], crecv.at[l * 3 + k], sibling).wait_recv()
        for cp in sends:
            cp.wait_send()
        for cp in started:
            cp.wait()

    out_shape = [jax.ShapeDtypeStruct(p.shape, p.dtype) for p in placed]
    out_shape += [jax.ShapeDtypeStruct((conv_shard.shape[1], conv_shard.shape[2] * 4), conv_shard.dtype)] * nl
    outs = pl.pallas_call(
        body, name="allgather_weights", in_specs=[HB] * (n + 1), out_specs=[HB] * len(out_shape), out_shape=out_shape,
        input_output_aliases={i: i for i in range(n)},
        scratch_shapes=[pltpu.SemaphoreType.DMA((n * 3,))] * 4 + [pltpu.SemaphoreType.DMA((nl * 3,))] * 2
        + [pltpu.SemaphoreType.DMA((nl,))],
        compiler_params=pltpu.CompilerParams(has_side_effects=True),
    )(*placed, conv_shard)
    return outs[:n], outs[n:]


def _sibling_exchange(grads, axes):
    na = len(grads)
    views = [g.reshape(g.shape[0], 4, 2, g.shape[1] // 8, g.shape[2]) if ax == 0 else g for g, ax in zip(grads, axes)]

    def region(ref, axis, half):
        if axis == 0:
            return ref.at[:, :, half]
        r = ref.shape[1]
        return ref.at[:, _rows(half * (r // 2), r // 2), :]

    def body(*refs):
        ins, land = refs[:na], refs[na:2 * na]
        send, recv = refs[2 * na:]
        x, y, c, _ = _place()
        copies = []
        for a in range(na):
            cp = _remote(region(ins[a], axes[a], 1 - c), land[a], send.at[a], recv.at[a], (x, y, 1 - c))
            cp.start()
            copies.append(cp)
        for cp in copies:
            cp.wait()

    shapes = [(g.shape[0], 4, g.shape[1] // 8, g.shape[2]) if ax == 0 else (g.shape[0], g.shape[1] // 2, g.shape[2])
              for g, ax in zip(grads, axes)]
    return pl.pallas_call(
        body, name="grad_sibling_exchange", in_specs=[HB] * na, out_specs=[HB] * na,
        out_shape=[jax.ShapeDtypeStruct(s, F32) for s in shapes],
        scratch_shapes=[pltpu.SemaphoreType.DMA((na,))] * 2,
        compiler_params=pltpu.CompilerParams(has_side_effects=True),
    )(*views)


def _add_cast(mine, theirs, core, name):
    na, _, nb, cols = mine.shape
    rb = _row_block(nb, cols)

    def body(core_ref, a_ref, b_ref, o_ref):
        o_ref[...] = (a_ref[...] + b_ref[...]).astype(_PAY)

    blk = pl.BlockSpec((None, rb, cols), lambda i, k, core_ref: (i, k, 0))
    return pl.pallas_call(
        body, name=name,
        grid_spec=pltpu.PrefetchScalarGridSpec(
            num_scalar_prefetch=1, grid=(na, nb // rb),
            in_specs=[pl.BlockSpec((None, None, rb, cols), lambda i, k, core_ref: (i, core_ref[0], k, 0)), blk], out_specs=blk),
        out_shape=jax.ShapeDtypeStruct((na, nb, cols), _PAY), compiler_params=_params(("parallel", "parallel")),
    )(_scalars(core), mine, theirs)


def _piece(ref, axis, chip):
    if axis == 0:
        return ref.at[:, chip]
    cs = ref.shape[2] // 4
    return ref.at[:, :, _rows(chip * cs, cs, 128)]


def _chip_scatter(sums, axes):
    na = len(sums)

    def piece_shape(a):
        nl = sums[a].shape[0]
        if axes[a] == 0:
            return (nl, sums[a].shape[2], sums[a].shape[3])
        return (nl, sums[a].shape[1], sums[a].shape[2] // 4)

    def body(*refs):
        ins, slots = refs[:na], refs[na:2 * na]
        send, recv = refs[2 * na:]
        _, _, c, chips = _place()
        copies = []
        for a in range(na):
            for k, chip in enumerate(chips):
                cp = _remote(_piece(ins[a], axes[a], 2 * chip[0] + chip[1]), slots[a].at[k], send.at[a * 3 + k], recv.at[a * 3 + k], (*chip, c))
                cp.start()
                copies.append(cp)
        for cp in copies:
            cp.wait()

    return pl.pallas_call(
        body, name="grad_chip_scatter", in_specs=[HB] * na, out_specs=[HB] * na,
        out_shape=[jax.ShapeDtypeStruct((3,) + piece_shape(a), sums[a].dtype) for a in range(na)],
        scratch_shapes=[pltpu.SemaphoreType.DMA((na * 3,))] * 2,
        compiler_params=pltpu.CompilerParams(has_side_effects=True),
    )(*sums)


def _sum_slots(sums, slots, axis, chip, core, name):
    _, nl, hr, cs = slots.shape
    rb = _row_block(hr, cs)

    def body(at_ref, own_ref, s_ref, o_ref):
        o_ref[...] = ((own_ref[...].astype(F32) + s_ref[0].astype(F32)) + s_ref[1].astype(F32)) + s_ref[2].astype(F32)

    if axis == 0:
        own = pl.BlockSpec((None, None, rb, cs), lambda l, k, at_ref: (l, at_ref[0], k, 0))
    else:
        own = pl.BlockSpec((None, rb, cs), lambda l, k, at_ref: (l, k, at_ref[0]))
    out = pl.pallas_call(
        body, name=name,
        grid_spec=pltpu.PrefetchScalarGridSpec(
            num_scalar_prefetch=1, grid=(nl, hr // rb),
            in_specs=[own, pl.BlockSpec((3, None, rb, cs), lambda l, k, at_ref: (0, l, k, 0))],
            out_specs=pl.BlockSpec((None, None, rb, cs), lambda l, k, at_ref: (l, at_ref[1], k, 0))),
        out_shape=jax.ShapeDtypeStruct((nl, 2, hr, cs), F32), compiler_params=_params(("parallel", "parallel")),
    )(_scalars(chip, core), sums, slots)
    return out.reshape(nl, 2 * hr, cs)


def _sibling_assemble(shards):
    na = len(shards)

    def body(*refs):
        outs = refs[na:2 * na]
        send, recv = refs[2 * na:]
        x, y, c, _ = _place()
        copies = []
        for a in range(na):
            hr = outs[a].shape[1] // 2
            mine = outs[a].at[:, _rows(c * hr, hr), :]
            cp = _remote(mine, mine, send.at[a], recv.at[a], (x, y, 1 - c))
            cp.start()
            copies.append(cp)
        for cp in copies:
            cp.wait()

    return pl.pallas_call(
        body, name="grad_sibling_assemble", in_specs=[HB] * na, out_specs=[HB] * na,
        out_shape=[jax.ShapeDtypeStruct(s.shape, F32) for s in shards], input_output_aliases={a: a for a in range(na)},
        scratch_shapes=[pltpu.SemaphoreType.DMA((na,))] * 2,
        compiler_params=pltpu.CompilerParams(has_side_effects=True),
    )(*shards)


def _allreduce_small(buf):
    rows, w = buf.shape
    half = rows // 2

    def body(buf_ref, out_ref, land, slots, red, sems_send, sems_recv):
        x, y, c, chips = _place()
        me = 2 * x + y
        sibling = (x, y, 1 - c)
        first = _remote(buf_ref, land, sems_send.at[0], sems_recv.at[0], sibling)
        first.start()
        first.wait()
        mine = pl.ds(pl.multiple_of(c * half, 8), half)
        slots[me] = buf_ref[mine, :] + land[mine, :]
        sends = []
        for k, chip in enumerate(chips):
            cp = _remote(slots.at[me], slots.at[me], sems_send.at[1 + k], sems_recv.at[1 + k], (*chip, c))
            cp.start()
            sends.append(cp)
        for k, chip in enumerate(chips):
            got = slots.at[2 * chip[0] + chip[1]]
            _remote(got, got, sems_send.at[1 + k], sems_recv.at[1 + k], sibling).wait_recv()
        red[...] = ((slots[0] + slots[1]) + slots[2]) + slots[3]
        out_ref[mine, :] = red[...]
        last = _remote(red, out_ref.at[mine, :], sems_send.at[4], sems_recv.at[4], sibling)
        last.start()
        theirs = out_ref.at[pl.ds(pl.multiple_of((1 - c) * half, 8), half), :]
        _remote(red, theirs, sems_send.at[4], sems_recv.at[4], sibling).wait_recv()
        for cp in sends:
            cp.wait_send()
        last.wait_send()

    return pl.pallas_call(
        body, name="allreduce_small", in_specs=[VM], out_specs=VM, out_shape=jax.ShapeDtypeStruct((rows, w), F32),
        scratch_shapes=[pltpu.VMEM((rows, w), F32), pltpu.VMEM((4, half, w), F32), pltpu.VMEM((half, w), F32),
                        pltpu.SemaphoreType.DMA((5,)), pltpu.SemaphoreType.DMA((5,))],
        compiler_params=pltpu.CompilerParams(has_side_effects=True, vmem_limit_bytes=VMEM_LIMIT),
    )(buf)


BIG = ("w_in", "w_out", "wq", "wk", "wv", "wo", "w_up", "w_down")
BIG_AXIS = {"w_in": 1, "w_out": 0, "wq": 0, "wk": 0, "wv": 0, "wo": 0, "w_up": 1, "w_down": 0}
SMALL = ("norm_mix_g", "pool_w", "pool_scale", "sgu_g", "sgu_w", "sgu_b", "norm_xattn_g", "mem_norm_g", "norm_ffn_g",
         "conv_w", "conv_b", "final_norm_g")
ORDER = ("norm_mix_g", "w_in", "pool_w", "pool_scale", "sgu_g", "sgu_w", "sgu_b", "w_out", "norm_xattn_g", "mem_norm_g",
         "wq", "wk", "wv", "wo", "norm_ffn_g", "w_up", "conv_w", "conv_b", "w_down", "final_norm_g")
PACK_WIDTH = 512


def kernel(x, mem, norm_mix_g, w_in, pool_w, pool_scale, sgu_g, sgu_w, sgu_b, w_out, norm_xattn_g, mem_norm_g, wq, wk, wv, wo, norm_ffn_g, w_up, conv_w, conv_b, w_down, final_norm_g, loss_target, m_norm_mix_g, m_w_in, m_pool_w, m_pool_scale, m_sgu_g, m_sgu_w, m_sgu_b, m_w_out, m_norm_xattn_g, m_mem_norm_g, m_wq, m_wk, m_wv, m_wo, m_norm_ffn_g, m_w_up, m_conv_w, m_conv_b, m_w_down, m_final_norm_g, v_norm_mix_g, v_w_in, v_pool_w, v_pool_scale, v_sgu_g, v_sgu_w, v_sgu_b, v_w_out, v_norm_xattn_g, v_mem_norm_g, v_wq, v_wk, v_wv, v_wo, v_norm_ffn_g, v_w_up, v_conv_w, v_conv_b, v_w_down, v_final_norm_g):
    given = dict(locals())
    w = {n: given[n] for n in ORDER}
    mom = {n: given["m_" + n] for n in ORDER}
    var = {n: given["v_" + n] for n in ORDER}
    nl = w_in.shape[0]
    xs, mems, tgt = x[0], mem[0], loss_target[0]
    chip = 2 * lax.axis_index("x") + lax.axis_index("y")
    core = lax.axis_index("c")

    axes = [BIG_AXIS[n] for n in BIG]
    placed = [_cast_place(w[n], l, BIG_AXIS[n], chip, f"place_{n}_{l}") for n in BIG for l in range(nl)]
    gathered, conv_full = _allgather_weights(placed, [BIG_AXIS[n] for n in BIG for _ in range(nl)], conv_w)
    full = [{n: gathered[a * nl + l].astype(_MXU) for a, n in enumerate(BIG)} for l in range(nl)]

    row = lambda a, l: a[l][None, :]
    saved = []
    h = xs
    for l in range(nl):
        fw = full[l]
        sbt = jnp.broadcast_to(sgu_b[l][:, :, None], sgu_w[l].shape)
        h1, proj, xn1, mix = _mixer_fwd(h, row(norm_mix_g, l), fw["w_in"], pool_w[l], row(pool_scale, l), row(sgu_g, l), sgu_w[l], sbt, fw["w_out"])
        k, v, memn = _kv_fwd(mems, row(mem_norm_g, l), fw["wk"], fw["wv"])
        h2, q, o, xn2 = _xattn_fwd(h1, row(norm_xattn_g, l), fw["wq"], k, v, fw["wo"])
        h3, hh, xn3 = _ffn_fwd(h2, row(norm_ffn_g, l), fw["w_up"], conv_full[l], row(conv_b, l), fw["w_down"])
        saved.append(dict(h=h, h1=h1, h2=h2, proj=proj, xn1=xn1, mix=mix, k=k, v=v, memn=memn, q=q, o=o, xn2=xn2, hh=hh, xn3=xn3, sbt=sbt))
        h = h3

    dh, loss_part, g_final = _loss_head(h, final_norm_g[None, :], tgt)

    big_grads = {}
    small_grads = [None] * nl

    def weight_grad(n, a, b, l):
        big_grads[n] = _grad_matmul(a, b, "grad_" + n, l, nl, big_grads.get(n))

    for l in reversed(range(nl)):
        fw, s = full[l], saved[l]
        dh3 = dh
        dh2, dhh, act, g_cw, g_cb, g_nf = _ffn_bwd(dh3, s["h2"], s["hh"], row(norm_ffn_g, l), fw["w_up"], conv_full[l], row(conv_b, l), fw["w_down"])
        weight_grad("w_up", s["xn3"], dhh, l)
        weight_grad("w_down", act, dh3, l)
        dh1, dq, dk, dv, g_nx = _xattn_bwd(dh2, s["h1"], s["q"], row(norm_xattn_g, l), fw["wq"], s["k"], s["v"], fw["wo"])
        weight_grad("wq", s["xn2"], dq, l)
        weight_grad("wo", s["o"], dh2, l)
        weight_grad("wk", s["memn"], dk, l)
        weight_grad("wv", s["memn"], dv, l)
        g_mn = _kv_bwd(dk, dv, mems, fw["wk"], fw["wv"])
        dh0, dproj, g_nm, g_pw, g_ps, g_sg, g_sw, g_sbt = _mixer_bwd(dh1, s["h"], s["proj"], row(norm_mix_g, l), fw["w_in"], pool_w[l], row(pool_scale, l), row(sgu_g, l), sgu_w[l], s["sbt"], fw["w_out"])
        weight_grad("w_in", s["xn1"], dproj, l)
        weight_grad("w_out", s["mix"], dh1, l)
        small_grads[l] = dict(norm_mix_g=g_nm, pool_w=g_pw, pool_scale=g_ps, sgu_g=g_sg, sgu_w=g_sw, sgu_b=jnp.sum(g_sbt, axis=-1),
                              norm_xattn_g=g_nx, mem_norm_g=g_mn, norm_ffn_g=g_nf, conv_w=g_cw, conv_b=g_cb)
        dh = dh0
    grad_x = dh[None]

    whole = [big_grads[n] for n in BIG]
    theirs = _sibling_exchange(whole, axes)
    sums = []
    for g, t, ax, n in zip(whole, theirs, axes, BIG):
        gl, gr, gc = g.shape
        if ax == 0:
            s = _add_cast(g.reshape(gl * 4, 2, gr // 8, gc), t.reshape(gl * 4, gr // 8, gc), core, "grad_chip_sum_" + n)
            sums.append(s.reshape(gl, 4, gr // 8, gc))
        else:
            sums.append(_add_cast(g.reshape(gl, 2, gr // 2, gc), t, core, "grad_chip_sum_" + n))
    slots = _chip_scatter(sums, axes)
    halves = [_sum_slots(s, sl, ax, chip, core, "grad_sum_" + n) for s, sl, ax, n in zip(sums, slots, axes, BIG)]
    shard_grads = dict(zip(BIG, _sibling_assemble(halves)))

    layered = [n for n in SMALL if n != "final_norm_g"]
    parts = [small_grads[l][n].reshape(-1, PACK_WIDTH) for n in layered for l in range(nl)]
    parts.append(g_final.reshape(-1, PACK_WIDTH))
    parts.append(jnp.pad(loss_part, ((0, 0), (0, PACK_WIDTH - 1))))
    used = sum(p.shape[0] for p in parts)
    total = -(-used // 16) * 16
    packed = _allreduce_small(jnp.concatenate(parts + [jnp.zeros((total - used, PACK_WIDTH), F32)], axis=0))
    grads = dict(shard_grads)
    at = 0
    for n in layered:
        per_layer = []
        for l in range(nl):
            shape = small_grads[l][n].shape
            nrow = small_grads[l][n].size // PACK_WIDTH
            per_layer.append(packed[at:at + nrow].reshape(shape))
            at += nrow
        g = jnp.stack(per_layer)
        if n == "conv_w":
            cs = conv_w.shape[2]
            g = lax.dynamic_slice_in_dim(g, chip * cs, cs, axis=2)
        grads[n] = g.reshape(w[n].shape)
    grads["final_norm_g"] = packed[at:at + g_final.size // PACK_WIDTH].reshape(final_norm_g.shape)
    at += g_final.size // PACK_WIDTH
    loss = packed[at, 0]

    delta, new_m, new_v = {}, {}, {}
    for n in BIG:
        delta[n], new_m[n], new_v[n] = _adamw_big(w[n], grads[n], mom[n], var[n], "adamw_" + n)
    two_d = lambda a: a.reshape(-1, a.shape[-1])
    ds, nms, nvs = _adamw_small([two_d(w[n]) for n in SMALL], [two_d(grads[n]) for n in SMALL],
                                [two_d(mom[n]) for n in SMALL], [two_d(var[n]) for n in SMALL])
    for n, d_, m_, v_ in zip(SMALL, ds, nms, nvs):
        delta[n], new_m[n], new_v[n] = d_.reshape(w[n].shape), m_.reshape(w[n].shape), v_.reshape(w[n].shape)

    return (loss, grad_x, *[grads[n] for n in ORDER], *[delta[n] for n in ORDER], *[new_m[n] for n in ORDER], *[new_v[n] for n in ORDER])
```

```python
import math
from typing import NamedTuple

import jax
import jax.numpy as jnp
from jax import lax
from jax.experimental import pallas as pl
from jax.experimental.pallas import tpu as pltpu

F32 = jnp.float32
_MXU = jnp.bfloat16
_PAY = jnp.bfloat16
EPS = 1e-6
WINDOWS = (2, 4, 8, 16)
GROUP = 128
N_XHEADS = 4
HALO = 16
FF_TILE = 256
VMEM_LIMIT = 56 * 1024 * 1024
MESH = pl.DeviceIdType.MESH

ADAM_LR, ADAM_B1, ADAM_B2, ADAM_EPS, ADAM_WD, ADAM_STEP = 0.001, 0.9, 0.999, 1e-08, 0.01, 10

VM = pl.BlockSpec(memory_space=pltpu.VMEM)
HB = pl.BlockSpec(memory_space=pltpu.HBM)


def _nn(a, b):
    return jnp.dot(a, b, preferred_element_type=F32)


def _nt(a, b):
    return lax.dot_general(a, b, (((1,), (1,)), ((), ())), preferred_element_type=F32)


def _tn(a, b):
    return lax.dot_general(a, b, (((0,), (0,)), ((), ())), preferred_element_type=F32)


def _rms(x):
    r = lax.rsqrt(jnp.mean(x * x, axis=-1, keepdims=True) + EPS)
    return x * r, r


def _rms_bwd(dxn, xhat, r, g):
    dxh = dxn * g
    dx = r * (dxh - xhat * jnp.mean(dxh * xhat, axis=-1, keepdims=True))
    return dx, jnp.sum(dxn * xhat, axis=0, keepdims=True)


def _gelu(x):
    return 0.5 * x * (1.0 + lax.erf(x * (2.0 ** -0.5)))


def _gelu_grad(x):
    return 0.5 * (1.0 + lax.erf(x * (2.0 ** -0.5))) + x * jnp.exp(-0.5 * x * x) * ((2.0 * math.pi) ** -0.5)


def _params(sem=None):
    return pltpu.CompilerParams(dimension_semantics=sem, vmem_limit_bytes=VMEM_LIMIT)


def _token_block(t, want):
    return want if t % want == 0 and t > want else GROUP


def _const_spec(shape):
    n = len(shape)
    return pl.BlockSpec(shape, lambda i: (0,) * n)


def _tril():
    return lax.broadcasted_iota(jnp.int32, (GROUP, GROUP), 0) >= lax.broadcasted_iota(jnp.int32, (GROUP, GROUP), 1)


class _Hosted(NamedTuple):
    operands: tuple
    aliased: bool
    out_shapes: tuple
    sems: tuple
    stages: tuple


def _hosted_results(hosted):
    if hosted.aliased:
        return [jax.ShapeDtypeStruct(o.shape, o.dtype) for o in hosted.operands]
    return list(hosted.out_shapes)


def _call_hosting(main_body, hosted, *, name, steps, in_specs, out_specs, out_shape, scratch_shapes, operands):
    if hosted is None:
        outs = pl.pallas_call(main_body, name=name, grid=(steps,), in_specs=in_specs, out_specs=out_specs, out_shape=out_shape,
                              scratch_shapes=scratch_shapes, compiler_params=_params(("arbitrary",)))(*operands)
        return outs, ()
    n_in, n_out, n_sc, nh = len(in_specs), len(out_specs), len(scratch_shapes), len(hosted.operands)
    h_shapes = _hosted_results(hosted)
    start, middle, finish = hosted.stages

    def body(*refs):
        at = [0]

        def take(n):
            at[0] += n
            return refs[at[0] - n:at[0]]

        ins, h_in, outs, h_out, scratch, h_sems = take(n_in), take(nh), take(n_out), take(len(h_shapes)), take(n_sc), take(len(hosted.sems))
        step = pl.program_id(0)

        @pl.when(step == 0)
        def _():
            start(h_in, h_out, h_sems)

        if middle is not None:
            @pl.when(step == (3 * steps) // 4)
            def _():
                middle(h_in, h_out, h_sems)

        main_body(*ins, *outs, *scratch)

        @pl.when(step == steps - 1)
        def _():
            finish(h_in, h_out, h_sems)

    outs = pl.pallas_call(
        body, name=name, grid=(steps,), in_specs=list(in_specs) + [HB] * nh, out_specs=list(out_specs) + [HB] * len(h_shapes),
        out_shape=list(out_shape) + h_shapes, scratch_shapes=list(scratch_shapes) + list(hosted.sems),
        input_output_aliases={n_in + i: n_out + i for i in range(nh)} if hosted.aliased else {},
        compiler_params=_params(("arbitrary",)),
    )(*operands, *hosted.operands)
    return outs[:n_out], outs[n_out:]


def _run_hosted(hosted, name):
    nh = len(hosted.operands)
    h_shapes = _hosted_results(hosted)

    def body(*refs):
        h_in, h_out, h_sems = refs[:nh], refs[nh:nh + len(h_shapes)], refs[nh + len(h_shapes):]
        for stage in hosted.stages:
            if stage is not None:
                stage(h_in, h_out, h_sems)

    return pl.pallas_call(
        body, name=name, in_specs=[HB] * nh, out_specs=[HB] * len(h_shapes), out_shape=h_shapes, scratch_shapes=list(hosted.sems),
        input_output_aliases={i: i for i in range(nh)} if hosted.aliased else {},
        compiler_params=pltpu.CompilerParams(has_side_effects=True),
    )(*hosted.operands)


def _pool_diff(pext, p, t0, tb, gi, win):
    sl = slice(gi * GROUP, (gi + 1) * GROUP)
    s = p[:, sl]
    for k in range(1, win):
        s = s + pext[HALO - k:HALO - k + tb, sl]
    tglob = t0 + lax.broadcasted_iota(jnp.int32, (tb, 1), 0)
    cnt = jnp.minimum(tglob + 1, win).astype(F32)
    return s / cnt - p[:, sl], cnt


def _layernorm(v):
    xc = v - jnp.mean(v, axis=-1, keepdims=True)
    rstd = lax.rsqrt(jnp.mean(xc * xc, axis=-1, keepdims=True) + EPS)
    return xc * rstd, rstd


def _mixer_fwd(h, g, w_in, pool_w, pool_scale, sgu_g, sgu_w, sgu_bt, w_out):
    t, d = h.shape
    pw = pool_w.shape[0] * GROUP
    sw = sgu_w.shape[0] * GROUP
    tb = _token_block(t, 512)

    def body(h_ref, g_ref, win_ref, pw_ref, ps_ref, sg_ref, sw_ref, sbt_ref, wout_ref, h1_ref, proj_ref, xn_ref, mix_ref, pext):
        i = pl.program_id(0)

        @pl.when(i == 0)
        def _():
            pext[0:HALO, :] = jnp.zeros((HALO, pw), F32)

        x = h_ref[...]
        xhat, _ = _rms(x)
        xn = (xhat * g_ref[...]).astype(_MXU)
        xn_ref[...] = xn
        proj = _nn(xn, win_ref[...])
        proj_ref[...] = proj
        p = proj[:, :pw]
        pext[HALO:HALO + tb, :] = p
        for gi, win in enumerate(WINDOWS):
            sl = slice(gi * GROUP, (gi + 1) * GROUP)
            dg, _ = _pool_diff(pext, p, i * tb, tb, gi, win)
            e = _nn(dg.astype(_MXU), pw_ref[gi].astype(_MXU))
            mix_ref[:, sl] = (e * ps_ref[:, sl]).astype(_MXU)
        pext[0:HALO, :] = p[tb - HALO:tb, :]
        uv = _gelu(proj[:, pw:])
        u = uv[:, :sw]
        vhat, _ = _layernorm(uv[:, sw:])
        vn = (vhat * sg_ref[...]).astype(_MXU)
        mask = _tril()
        for hh in range(sw // GROUP):
            wm = jnp.where(mask, sw_ref[hh], 0.0).astype(_MXU)
            for n in range(tb // GROUP):
                rows = slice(n * GROUP, (n + 1) * GROUP)
                cols = slice(hh * GROUP, (hh + 1) * GROUP)
                z = _nn(wm, vn[rows, cols]) + sbt_ref[hh]
                mix_ref[rows, pw + hh * GROUP:pw + (hh + 1) * GROUP] = (u[rows, cols] * z).astype(_MXU)
        h1_ref[...] = x + _nn(mix_ref[...], wout_ref[...])

    blk = lambda w: pl.BlockSpec((tb, w), lambda i: (i, 0))
    return pl.pallas_call(
        body, name="mixer_fwd", grid=(t // tb,),
        in_specs=[blk(d), VM, VM, VM, VM, VM, VM, VM, VM],
        out_specs=[blk(d), blk(w_in.shape[1]), blk(d), blk(d)],
        out_shape=[jax.ShapeDtypeStruct((t, d), F32), jax.ShapeDtypeStruct((t, w_in.shape[1]), F32),
                   jax.ShapeDtypeStruct((t, d), _MXU), jax.ShapeDtypeStruct((t, d), _MXU)],
        scratch_shapes=[pltpu.VMEM((HALO + tb, pw), F32)],
        compiler_params=_params(("arbitrary",)),
    )(h, g, w_in, pool_w, pool_scale, sgu_g, sgu_w, sgu_bt, w_out)


def _mixer_bwd(dh1, h, proj, g, w_in, pool_w, pool_scale, sgu_g, sgu_w, sgu_bt, w_out):
    t, d = h.shape
    ng, nh = pool_w.shape[0], sgu_w.shape[0]
    pw, sw = ng * GROUP, nh * GROUP
    tb = _token_block(t, 256)
    nb = t // tb

    def body(dh1_ref, h_ref, proj_ref, halo_ref, g_ref, win_ref, pw_ref, ps_ref, sg_ref, sw_ref, sbt_ref, wout_ref,
             dh_ref, dproj_ref, gg_ref, gpw_ref, gps_ref, gsg_ref, gsw_ref, gsbt_ref, pext, dext, duv):
        i = pl.program_id(0)
        blk = nb - 1 - i

        @pl.when(i == 0)
        def _():
            for r in (gg_ref, gpw_ref, gps_ref, gsg_ref, gsw_ref, gsbt_ref):
                r[...] = jnp.zeros(r.shape, F32)
            dext[tb:tb + HALO, :] = jnp.zeros((HALO, pw), F32)

        dh1v = dh1_ref[...]
        dmix = _nt(dh1v.astype(_MXU), wout_ref[...])
        proj_v = proj_ref[...]
        p = proj_v[:, :pw]
        pext[0:HALO, :] = jnp.where(blk == 0, 0.0, halo_ref[...])
        pext[HALO:HALO + tb, :] = p
        for gi, win in enumerate(WINDOWS):
            sl = slice(gi * GROUP, (gi + 1) * GROUP)
            dg, cnt = _pool_diff(pext, p, blk * tb, tb, gi, win)
            dgm = dg.astype(_MXU)
            pwm = pw_ref[gi].astype(_MXU)
            e = _nn(dgm, pwm)
            dy = dmix[:, sl]
            gps_ref[:, sl] += jnp.sum(dy * e, axis=0, keepdims=True)
            de = (dy * ps_ref[:, sl]).astype(_MXU)
            gpw_ref[gi] += _tn(dgm, de)
            dd = _nt(de, pwm)
            ddc = dd / cnt
            dext[0:tb, sl] = ddc
            acc = ddc
            for k in range(1, win):
                acc = acc + dext[k:k + tb, sl]
            dext[tb:tb + HALO, sl] = ddc[0:HALO, :]
            dproj_ref[:, sl] = (acc - dd).astype(_MXU)
        pre = proj_v[:, pw:]
        uv = _gelu(pre)
        u = uv[:, :sw]
        vhat, rstd = _layernorm(uv[:, sw:])
        vn = (vhat * sg_ref[...]).astype(_MXU)
        mask = _tril()
        for hh in range(nh):
            wm = jnp.where(mask, sw_ref[hh], 0.0).astype(_MXU)
            cols = slice(hh * GROUP, (hh + 1) * GROUP)
            gw = jnp.zeros((GROUP, GROUP), F32)
            gb = jnp.zeros((GROUP, GROUP), F32)
            for n in range(tb // GROUP):
                rows = slice(n * GROUP, (n + 1) * GROUP)
                vs = vn[rows, cols]
                z = _nn(wm, vs) + sbt_ref[hh]
                dy = dmix[rows, pw + hh * GROUP:pw + (hh + 1) * GROUP]
                dz = dy * u[rows, cols]
                gb = gb + dz
                dzm = dz.astype(_MXU)
                gw = gw + _nt(dzm, vs)
                duv[rows, cols] = dy * z
                duv[rows, sw + hh * GROUP:sw + (hh + 1) * GROUP] = _tn(wm, dzm)
            gsw_ref[hh] += jnp.where(mask, gw, 0.0)
            gsbt_ref[hh] += gb
        dvn = duv[:, sw:]
        gsg_ref[...] += jnp.sum(dvn * vhat, axis=0, keepdims=True)
        dxh = dvn * sg_ref[...]
        dv = rstd * (dxh - jnp.mean(dxh, axis=-1, keepdims=True) - vhat * jnp.mean(dxh * vhat, axis=-1, keepdims=True))
        gp = _gelu_grad(pre)
        dproj_ref[:, pw:pw + sw] = (duv[:, :sw] * gp[:, :sw]).astype(_MXU)
        dproj_ref[:, pw + sw:] = (dv * gp[:, sw:]).astype(_MXU)
        dxn = _nt(dproj_ref[...], win_ref[...])
        xhat, r = _rms(h_ref[...])
        dx, gg = _rms_bwd(dxn, xhat, r, g_ref[...])
        gg_ref[...] += gg
        dh_ref[...] = dh1v + dx

    rev = lambda w: pl.BlockSpec((tb, w), lambda i: (nb - 1 - i, 0))
    halo = pl.BlockSpec((HALO, pw), lambda i: (jnp.maximum((nb - 1 - i) * (tb // HALO) - 1, 0), 0))
    small = [(1, d), (ng, GROUP, GROUP), (1, pw), (1, sw), (nh, GROUP, GROUP), (nh, GROUP, GROUP)]
    return pl.pallas_call(
        body, name="mixer_bwd", grid=(nb,),
        in_specs=[rev(d), rev(d), rev(proj.shape[1]), halo, VM, VM, VM, VM, VM, VM, VM, VM],
        out_specs=[rev(d), rev(proj.shape[1])] + [_const_spec(s) for s in small],
        out_shape=[jax.ShapeDtypeStruct((t, d), F32), jax.ShapeDtypeStruct(proj.shape, _MXU)]
        + [jax.ShapeDtypeStruct(s, F32) for s in small],
        scratch_shapes=[pltpu.VMEM((HALO + tb, pw), F32), pltpu.VMEM((tb + HALO, pw), F32), pltpu.VMEM((tb, 2 * sw), F32)],
        compiler_params=_params(("arbitrary",)),
    )(dh1, h, proj, proj, g, w_in, pool_w, pool_scale, sgu_g, sgu_w, sgu_bt, w_out)


def _kv_fwd(mem, gm, wk, wv):
    n, d = mem.shape

    def body(mem_ref, gm_ref, wk_ref, wv_ref, k_ref, v_ref, memn_ref):
        xhat, _ = _rms(mem_ref[...])
        memn = (xhat * gm_ref[...]).astype(_MXU)
        memn_ref[...] = memn
        k_ref[...] = _nn(memn, wk_ref[...]).astype(_MXU)
        v_ref[...] = _nn(memn, wv_ref[...]).astype(_MXU)

    return pl.pallas_call(
        body, name="kv_fwd", in_specs=[VM] * 4, out_specs=[VM] * 3,
        out_shape=[jax.ShapeDtypeStruct((n, d), _MXU)] * 3, compiler_params=_params(),
    )(mem, gm, wk, wv)


def _kv_bwd(dk, dv, mem, wk, wv):
    n, d = mem.shape

    def body(dk_ref, dv_ref, mem_ref, wk_ref, wv_ref, ggm_ref):
        dmemn = _nt(dk_ref[...].astype(_MXU), wk_ref[...]) + _nt(dv_ref[...].astype(_MXU), wv_ref[...])
        xhat, _ = _rms(mem_ref[...])
        ggm_ref[...] = jnp.sum(dmemn * xhat, axis=0, keepdims=True)

    return pl.pallas_call(
        body, name="kv_bwd", in_specs=[VM] * 5, out_specs=VM,
        out_shape=jax.ShapeDtypeStruct((1, d), F32), compiler_params=_params(),
    )(dk, dv, mem, wk, wv)


def _softmax_rows(qm, k_ref, sl, scale):
    s = _nt(qm, k_ref[:, sl]) * scale
    e = jnp.exp(s - jnp.max(s, axis=-1, keepdims=True))
    return e / jnp.sum(e, axis=-1, keepdims=True)


def _xattn_fwd(h, g, wq, k, v, wo):
    t, d = h.shape
    hd = d // N_XHEADS
    scale = hd ** -0.5
    tb = _token_block(t, 512)

    def body(h_ref, g_ref, wq_ref, k_ref, v_ref, wo_ref, h2_ref, q_ref, o_ref, xn_ref):
        x = h_ref[...]
        xhat, _ = _rms(x)
        xn = (xhat * g_ref[...]).astype(_MXU)
        xn_ref[...] = xn
        qm = _nn(xn, wq_ref[...]).astype(_MXU)
        q_ref[...] = qm
        for a in range(N_XHEADS):
            sl = slice(a * hd, (a + 1) * hd)
            pr = _softmax_rows(qm[:, sl], k_ref, sl, scale)
            o_ref[:, sl] = _nn(pr.astype(_MXU), v_ref[:, sl]).astype(_MXU)
        h2_ref[...] = x + _nn(o_ref[...], wo_ref[...])

    blk = pl.BlockSpec((tb, d), lambda i: (i, 0))
    return pl.pallas_call(
        body, name="xattn_fwd", grid=(t // tb,),
        in_specs=[blk, VM, VM, VM, VM, VM], out_specs=[blk] * 4,
        out_shape=[jax.ShapeDtypeStruct((t, d), F32)] + [jax.ShapeDtypeStruct((t, d), _MXU)] * 3,
        compiler_params=_params(("arbitrary",)),
    )(h, g, wq, k, v, wo)


def _xattn_bwd(dh2, h, q, g, wq, k, v, wo):
    t, d = h.shape
    n = k.shape[0]
    hd = d // N_XHEADS
    scale = hd ** -0.5
    tb = _token_block(t, 512)

    def body(dh2_ref, h_ref, q_ref, g_ref, wq_ref, k_ref, v_ref, wo_ref, dh_ref, dq_ref, dk_ref, dv_ref, gg_ref):
        @pl.when(pl.program_id(0) == 0)
        def _():
            for r in (dk_ref, dv_ref, gg_ref):
                r[...] = jnp.zeros(r.shape, F32)

        dh2v = dh2_ref[...]
        dom = _nt(dh2v.astype(_MXU), wo_ref[...]).astype(_MXU)
        for a in range(N_XHEADS):
            sl = slice(a * hd, (a + 1) * hd)
            qh = q_ref[:, sl]
            pr = _softmax_rows(qh, k_ref, sl, scale)
            dv_ref[:, sl] += _tn(pr.astype(_MXU), dom[:, sl])
            dpr = _nt(dom[:, sl], v_ref[:, sl])
            ds = (pr * (dpr - jnp.sum(dpr * pr, axis=-1, keepdims=True)) * scale).astype(_MXU)
            dq_ref[:, sl] = _nn(ds, k_ref[:, sl]).astype(_MXU)
            dk_ref[:, sl] += _tn(ds, qh)
        dxn = _nt(dq_ref[...], wq_ref[...])
        xhat, r = _rms(h_ref[...])
        dx, gg = _rms_bwd(dxn, xhat, r, g_ref[...])
        gg_ref[...] += gg
        dh_ref[...] = dh2v + dx

    blk = pl.BlockSpec((tb, d), lambda i: (i, 0))
    return pl.pallas_call(
        body, name="xattn_bwd", grid=(t // tb,),
        in_specs=[blk, blk, blk, VM, VM, VM, VM, VM],
        out_specs=[blk, blk, _const_spec((n, d)), _const_spec((n, d)), _const_spec((1, d))],
        out_shape=[jax.ShapeDtypeStruct((t, d), F32), jax.ShapeDtypeStruct((t, d), _MXU),
                   jax.ShapeDtypeStruct((n, d), F32), jax.ShapeDtypeStruct((n, d), F32), jax.ShapeDtypeStruct((1, d), F32)],
        compiler_params=_params(("arbitrary",)),
    )(dh2, h, q, g, wq, k, v, wo)


def _ffn_fwd(h, g, w_up, conv_w, conv_b, w_down, hosted=None):
    t, d = h.shape
    f = w_down.shape[0]
    ft = FF_TILE
    tb = _token_block(t, 512)

    def body(h_ref, g_ref, wup_ref, cw_ref, cb_ref, wdown_ref, h3_ref, hh_ref, xn_ref, ext, carry):
        @pl.when(pl.program_id(0) == 0)
        def _():
            carry[...] = jnp.zeros(carry.shape, F32)

        x = h_ref[...]
        xhat, _ = _rms(x)
        xn = (xhat * g_ref[...]).astype(_MXU)
        xn_ref[...] = xn
        acc = jnp.zeros((tb, d), F32)
        for j in range(f // ft):
            hc = []
            for part, off in enumerate((j * ft, f + j * ft)):
                cols = slice(off, off + ft)
                cur = _nn(xn, wup_ref[:, cols])
                hh_ref[:, cols] = cur.astype(_MXU)
                ext[part, 0:8, :] = carry[:, cols]
                ext[part, 8:8 + tb, :] = cur
                carry[:, cols] = cur[tb - 8:tb, :]
                hc.append(cb_ref[:, cols] + cw_ref[0:1, cols] * ext[part, 6:6 + tb, :]
                          + cw_ref[1:2, cols] * ext[part, 7:7 + tb, :] + cw_ref[2:3, cols] * cur)
            act = (hc[0] * jax.nn.sigmoid(hc[0]) * hc[1]).astype(_MXU)
            acc = acc + _nn(act, wdown_ref[j * ft:(j + 1) * ft, :])
        h3_ref[...] = x + acc

    blk = lambda w: pl.BlockSpec((tb, w), lambda i: (i, 0))
    return _call_hosting(
        body, hosted, name="ffn_fwd", steps=t // tb,
        in_specs=[blk(d), VM, VM, VM, VM, VM], out_specs=[blk(d), blk(2 * f), blk(d)],
        out_shape=[jax.ShapeDtypeStruct((t, d), F32), jax.ShapeDtypeStruct((t, 2 * f), _MXU), jax.ShapeDtypeStruct((t, d), _MXU)],
        scratch_shapes=[pltpu.VMEM((2, 8 + tb, ft), F32), pltpu.VMEM((8, 2 * f), F32)],
        operands=(h, g, w_up, conv_w, conv_b, w_down))


def _ffn_bwd(dh3, h, hh, g, w_up, conv_w, conv_b, w_down, hosted=None):
    t, d = h.shape
    f = w_down.shape[0]
    ft = FF_TILE
    tb = _token_block(t, 256)
    nb = t // tb

    def body(dh3_ref, h_ref, hh_ref, halo_ref, g_ref, wup_ref, cw_ref, cb_ref, wdown_ref,
             dh_ref, dhh_ref, act_ref, gcw_ref, gcb_ref, gg_ref, hext, dext, dcarry):
        i = pl.program_id(0)
        blk = nb - 1 - i

        @pl.when(i == 0)
        def _():
            for r in (gcw_ref, gcb_ref, gg_ref, dcarry):
                r[...] = jnp.zeros(r.shape, F32)

        dh3v = dh3_ref[...]
        dhm = dh3v.astype(_MXU)
        dxn = jnp.zeros((tb, d), F32)
        for j in range(f // ft):
            cur, back1, back2, hc = [], [], [], []
            for part, off in enumerate((j * ft, f + j * ft)):
                cols = slice(off, off + ft)
                c0 = hh_ref[:, cols].astype(F32)
                hext[part, 0:HALO, :] = jnp.where(blk == 0, 0.0, halo_ref[:, cols].astype(F32))
                hext[part, HALO:HALO + tb, :] = c0
                b1 = hext[part, HALO - 1:HALO - 1 + tb, :]
                b2 = hext[part, HALO - 2:HALO - 2 + tb, :]
                cur.append(c0)
                back1.append(b1)
                back2.append(b2)
                hc.append(cb_ref[:, cols] + cw_ref[0:1, cols] * b2 + cw_ref[1:2, cols] * b1 + cw_ref[2:3, cols] * c0)
            sg = jax.nn.sigmoid(hc[0])
            silu = hc[0] * sg
            act_ref[:, j * ft:(j + 1) * ft] = (silu * hc[1]).astype(_MXU)
            dact = _nt(dhm, wdown_ref[j * ft:(j + 1) * ft, :])
            dhc = (dact * hc[1] * sg * (1.0 + hc[0] * (1.0 - sg)), dact * silu)
            for part, off in enumerate((j * ft, f + j * ft)):
                cols = slice(off, off + ft)
                dc = dhc[part]
                gcb_ref[:, cols] += jnp.sum(dc, axis=0, keepdims=True)
                gcw_ref[0:1, cols] += jnp.sum(dc * back2[part], axis=0, keepdims=True)
                gcw_ref[1:2, cols] += jnp.sum(dc * back1[part], axis=0, keepdims=True)
                gcw_ref[2:3, cols] += jnp.sum(dc * cur[part], axis=0, keepdims=True)
                dext[part, 0:tb, :] = dc
                dext[part, tb:tb + 8, :] = dcarry[:, cols]
                dhh = (cw_ref[2:3, cols] * dc + cw_ref[1:2, cols] * dext[part, 1:1 + tb, :]
                       + cw_ref[0:1, cols] * dext[part, 2:2 + tb, :]).astype(_MXU)
                dcarry[:, cols] = dc[0:8, :]
                dhh_ref[:, cols] = dhh
                dxn = dxn + _nt(dhh, wup_ref[:, cols])
        xhat, r = _rms(h_ref[...])
        dx, gg = _rms_bwd(dxn, xhat, r, g_ref[...])
        gg_ref[...] += gg
        dh_ref[...] = dh3v + dx

    rev = lambda w: pl.BlockSpec((tb, w), lambda i: (nb - 1 - i, 0))
    halo = pl.BlockSpec((HALO, 2 * f), lambda i: (jnp.maximum((nb - 1 - i) * (tb // HALO) - 1, 0), 0))
    return _call_hosting(
        body, hosted, name="ffn_bwd", steps=nb,
        in_specs=[rev(d), rev(d), rev(2 * f), halo, VM, VM, VM, VM, VM],
        out_specs=[rev(d), rev(2 * f), rev(f), _const_spec((3, 2 * f)), _const_spec((1, 2 * f)), _const_spec((1, d))],
        out_shape=[jax.ShapeDtypeStruct((t, d), F32), jax.ShapeDtypeStruct((t, 2 * f), _MXU), jax.ShapeDtypeStruct((t, f), _MXU),
                   jax.ShapeDtypeStruct((3, 2 * f), F32), jax.ShapeDtypeStruct((1, 2 * f), F32), jax.ShapeDtypeStruct((1, d), F32)],
        scratch_shapes=[pltpu.VMEM((2, HALO + tb, ft), F32), pltpu.VMEM((2, tb + 8, ft), F32), pltpu.VMEM((8, 2 * f), F32)],
        operands=(dh3, h, hh, hh, g, w_up, conv_w, conv_b, w_down))


def _loss_head(h, g, target):
    t, d = h.shape
    tb = _token_block(t, 512)

    def body(h_ref, g_ref, tgt_ref, dh_ref, loss_ref, gg_ref):
        @pl.when(pl.program_id(0) == 0)
        def _():
            loss_ref[...] = jnp.zeros(loss_ref.shape, F32)
            gg_ref[...] = jnp.zeros(gg_ref.shape, F32)

        xhat, r = _rms(h_ref[...])
        err = xhat * g_ref[...] - tgt_ref[...]
        loss_ref[...] += 0.5 * jnp.sum(jnp.sum(err * err, axis=-1, keepdims=True), axis=0, keepdims=True) / d
        dx, gg = _rms_bwd(err / d, xhat, r, g_ref[...])
        gg_ref[...] += gg
        dh_ref[...] = dx

    blk = pl.BlockSpec((tb, d), lambda i: (i, 0))
    return pl.pallas_call(
        body, name="loss_head", grid=(t // tb,),
        in_specs=[blk, VM, blk], out_specs=[blk, _const_spec((1, 1)), _const_spec((1, d))],
        out_shape=[jax.ShapeDtypeStruct((t, d), F32), jax.ShapeDtypeStruct((1, 1), F32), jax.ShapeDtypeStruct((1, d), F32)],
        compiler_params=_params(("arbitrary",)),
    )(h, g, target)


def _largest_tile(n, cap, mult=128):
    best = None
    for c in range(mult, min(n, cap) + 1, mult):
        if n % c == 0:
            best = c
    return best if best is not None else n


def _grad_matmul(a, b, name, layer, n_layers, into=None):
    t, m = a.shape
    n = b.shape[1]
    tm, tn, tk = _largest_tile(m, 1408), _largest_tile(n, 1024), _largest_tile(t, 1024)
    nk = t // tk

    def body(a_ref, b_ref, *rest):
        o_ref = rest[-1]

        @pl.when(pl.program_id(2) == 0)
        def _():
            o_ref[...] = jnp.zeros(o_ref.shape, F32)

        o_ref[...] += _tn(a_ref[...].astype(_MXU), b_ref[...].astype(_MXU))

    in_specs = [pl.BlockSpec((tk, tm), lambda i, j, k: (k, i)), pl.BlockSpec((tk, tn), lambda i, j, k: (k, j))]
    operands = (a, b)
    aliases = {}
    if into is not None:
        in_specs.append(pl.BlockSpec(memory_space=pl.ANY))
        operands = (a, b, into)
        aliases = {2: 0}
    return pl.pallas_call(
        body, name=name, grid=(m // tm, n // tn, nk), in_specs=in_specs,
        out_specs=pl.BlockSpec((None, tm, tn), lambda i, j, k: (layer, i, j)),
        out_shape=jax.ShapeDtypeStruct((n_layers, m, n), F32), input_output_aliases=aliases,
        compiler_params=_params(("parallel", "parallel", "arbitrary")),
    )(*operands)


def _adamw_math(w, g, m, v):
    m = ADAM_B1 * m + (1.0 - ADAM_B1) * g
    v = ADAM_B2 * v + (1.0 - ADAM_B2) * (g * g)
    m_hat = m / (1.0 - ADAM_B1 ** ADAM_STEP)
    v_hat = v / (1.0 - ADAM_B2 ** ADAM_STEP)
    return -ADAM_LR * (m_hat / (jnp.sqrt(v_hat) + ADAM_EPS) + ADAM_WD * w), m, v


def _row_block(rows, cols, max_bytes=1 << 20, mult=16):
    best = None
    for r in range(mult, rows + 1, mult):
        if rows % r == 0 and r * cols * 4 <= max_bytes:
            best = r
    return best if best is not None else rows


def _adamw_big(w, g, m, v, name):
    shape = w.shape
    cols = shape[-1]
    flat = lambda a: a.reshape(-1, cols)
    rows = flat(w).shape[0]
    rb = _row_block(rows, cols)

    def body(w_ref, g_ref, m_ref, v_ref, d_ref, nm_ref, nv_ref):
        d_ref[...], nm_ref[...], nv_ref[...] = _adamw_math(w_ref[...], g_ref[...], m_ref[...], v_ref[...])

    blk = pl.BlockSpec((rb, cols), lambda i: (i, 0))
    outs = pl.pallas_call(
        body, name=name, grid=(rows // rb,), in_specs=[blk] * 4, out_specs=[blk] * 3,
        out_shape=[jax.ShapeDtypeStruct((rows, cols), F32)] * 3, compiler_params=_params(("parallel",)),
    )(flat(w), flat(g), flat(m), flat(v))
    return [o.reshape(shape) for o in outs]


def _adamw_small(ws, gs, ms, vs):
    n = len(ws)

    def body(*refs):
        for a in range(n):
            w_ref, g_ref, m_ref, v_ref = (refs[s * n + a] for s in range(4))
            d_ref, nm_ref, nv_ref = (refs[(4 + s) * n + a] for s in range(3))
            d_ref[...], nm_ref[...], nv_ref[...] = _adamw_math(w_ref[...], g_ref[...], m_ref[...], v_ref[...])

    outs = pl.pallas_call(
        body, name="adamw_small", in_specs=[VM] * (4 * n), out_specs=[VM] * (3 * n),
        out_shape=[jax.ShapeDtypeStruct(w.shape, F32) for w in ws] * 3, compiler_params=_params(),
    )(*ws, *gs, *ms, *vs)
    return outs[:n], outs[n:2 * n], outs[2 * n:]


def _place():
    x, y, c = lax.axis_index("x"), lax.axis_index("y"), lax.axis_index("c")
    chips = [(1 - x, y), (x, 1 - y), (1 - x, 1 - y)]
    return x, y, c, chips


def _rows(start, size, mult=16):
    return pl.ds(pl.multiple_of(start, mult), size)


def _full_window(ref, axis, chip, half=None):
    r, c = ref.shape
    if axis == 0:
        rs = r // 4
        if half is None:
            return ref.at[_rows(chip * rs, rs), :]
        return ref.at[_rows(chip * rs + half * (rs // 2), rs // 2), :]
    cs = c // 4
    if half is None:
        return ref.at[:, _rows(chip * cs, cs, 128)]
    return ref.at[_rows(half * (r // 2), r // 2), _rows(chip * cs, cs, 128)]


def _remote(src, dst, send_sem, recv_sem, to):
    return pltpu.make_async_remote_copy(src_ref=src, dst_ref=dst, send_sem=send_sem, recv_sem=recv_sem,
                                        device_id=to, device_id_type=MESH)


def _scalars(*vals):
    return jnp.stack([jnp.asarray(v, jnp.int32) for v in vals])


def _cast_place(shard, layer, axis, chip, name):
    _, rs, cs = shard.shape
    full = (rs * 4, cs) if axis == 0 else (rs, cs * 4)
    rb = _row_block(rs, cs)
    nrb = rs // rb

    def body(chip_ref, s_ref, o_ref):
        o_ref[...] = s_ref[...].astype(_PAY)

    if axis == 0:
        out_map = lambda i, chip_ref: (chip_ref[0] * nrb + i, 0)
    else:
        out_map = lambda i, chip_ref: (i, chip_ref[0])
    return pl.pallas_call(
        body, name=name,
        grid_spec=pltpu.PrefetchScalarGridSpec(
            num_scalar_prefetch=1, grid=(nrb,),
            in_specs=[pl.BlockSpec((None, rb, cs), lambda i, chip_ref: (layer, i, 0))],
            out_specs=pl.BlockSpec((rb, cs), out_map)),
        out_shape=jax.ShapeDtypeStruct(full, _PAY), compiler_params=_params(("parallel",)),
    )(_scalars(chip), shard)


def _hosted_allgather(placed, axes):
    n = len(placed)

    def each(outs, half_of):
        x, y, c, chips = _place()
        for i in range(n):
            for k, chip in enumerate(chips):
                yield i * 3 + k, (*chip, c), (x, y, 1 - c), _full_window(outs[i], axes[i], 2 * x + y, c), \
                    _full_window(outs[i], axes[i], 2 * chip[0] + chip[1], half_of(c))

    def start(_, outs, sems):
        send, recv, _, _ = sems
        for s, peer, _, mine, _ in each(outs, lambda c: c):
            _remote(mine, mine, send.at[s], recv.at[s], peer).start()

    def middle(_, outs, sems):
        send, recv, fsend, frecv = sems
        for s, _, sibling, _, got in each(outs, lambda c: c):
            _remote(got, got, send.at[s], recv.at[s], sibling).wait_recv()
            _remote(got, got, fsend.at[s], frecv.at[s], sibling).start()

    def finish(_, outs, sems):
        send, recv, fsend, frecv = sems
        for s, _, sibling, _, got in each(outs, lambda c: 1 - c):
            _remote(got, got, fsend.at[s], frecv.at[s], sibling).wait_recv()
        for s, peer, sibling, mine, got in each(outs, lambda c: c):
            _remote(mine, mine, send.at[s], recv.at[s], peer).wait_send()
            _remote(got, got, fsend.at[s], frecv.at[s], sibling).wait_send()

    return _Hosted(tuple(placed), True, (), (pltpu.SemaphoreType.DMA((n * 3,)),) * 4, (start, middle, finish))


def _allgather_conv(conv_shard):
    nl, taps, cs = conv_shard.shape

    def body(in_ref, out_ref, send, recv, local):
        x, y, c, chips = _place()
        mine = out_ref.at[:, :, _rows((2 * x + y) * cs, cs, 128)]
        own = pltpu.make_async_copy(in_ref, mine, local)
        own.start()
        sends = [_remote(in_ref, mine, send.at[k], recv.at[k], (*chip, c)) for k, chip in enumerate(chips)]
        for cp in sends:
            cp.start()
        for k, chip in enumerate(chips):
            got = out_ref.at[:, :, _rows((2 * chip[0] + chip[1]) * cs, cs, 128)]
            _remote(got, got, send.at[k], recv.at[k], (*chip, c)).wait_recv()
        for cp in sends:
            cp.wait_send()
        own.wait()

    return pl.pallas_call(
        body, name="allgather_conv", in_specs=[HB], out_specs=HB, out_shape=jax.ShapeDtypeStruct((nl, taps, cs * 4), conv_shard.dtype),
        scratch_shapes=[pltpu.SemaphoreType.DMA((3,)), pltpu.SemaphoreType.DMA((3,)), pltpu.SemaphoreType.DMA],
        compiler_params=pltpu.CompilerParams(has_side_effects=True),
    )(conv_shard)


def _sibling_exchange(grads, axes, layer):
    na = len(grads)
    views = [g.reshape(g.shape[0], 4, 2, g.shape[1] // 8, g.shape[2]) if ax == 0 else g for g, ax in zip(grads, axes)]

    def region(ref, axis, half):
        if axis == 0:
            return ref.at[layer, :, half]
        r = ref.shape[1]
        return ref.at[layer, _rows(half * (r // 2), r // 2), :]

    def body(*refs):
        ins, land = refs[:na], refs[na:2 * na]
        send, recv = refs[2 * na:]
        x, y, c, _ = _place()
        copies = []
        for a in range(na):
            cp = _remote(region(ins[a], axes[a], 1 - c), land[a], send.at[a], recv.at[a], (x, y, 1 - c))
            cp.start()
            copies.append(cp)
        for cp in copies:
            cp.wait()

    shapes = [(4, g.shape[1] // 8, g.shape[2]) if ax == 0 else (g.shape[1] // 2, g.shape[2]) for g, ax in zip(grads, axes)]
    return pl.pallas_call(
        body, name="grad_sibling_exchange", in_specs=[HB] * na, out_specs=[HB] * na,
        out_shape=[jax.ShapeDtypeStruct(s, F32) for s in shapes],
        scratch_shapes=[pltpu.SemaphoreType.DMA((na,))] * 2,
        compiler_params=pltpu.CompilerParams(has_side_effects=True),
    )(*views)


def _add_cast(mine, theirs, core, base, name):
    na, nb, cols = theirs.shape
    rb = _row_block(nb, cols)

    def body(core_ref, a_ref, b_ref, o_ref):
        o_ref[...] = (a_ref[...] + b_ref[...]).astype(_PAY)

    blk = pl.BlockSpec((None, rb, cols), lambda i, k, core_ref: (i, k, 0))
    return pl.pallas_call(
        body, name=name,
        grid_spec=pltpu.PrefetchScalarGridSpec(
            num_scalar_prefetch=1, grid=(na, nb // rb),
            in_specs=[pl.BlockSpec((None, None, rb, cols), lambda i, k, core_ref: (base + i, core_ref[0], k, 0)), blk], out_specs=blk),
        out_shape=jax.ShapeDtypeStruct((na, nb, cols), _PAY), compiler_params=_params(("parallel", "parallel")),
    )(_scalars(core), mine, theirs)


def _piece(ref, axis, chip):
    if axis == 0:
        return ref.at[chip]
    cs = ref.shape[1] // 4
    return ref.at[:, _rows(chip * cs, cs, 128)]


def _hosted_scatter(sums, axes):
    na = len(sums)

    def piece_shape(a):
        if axes[a] == 0:
            return (sums[a].shape[1], sums[a].shape[2])
        return (sums[a].shape[0], sums[a].shape[1] // 4)

    def copies(ins, slots, sems):
        send, recv = sems
        _, _, c, chips = _place()
        return [_remote(_piece(ins[a], axes[a], 2 * chip[0] + chip[1]), slots[a].at[k], send.at[a * 3 + k], recv.at[a * 3 + k], (*chip, c))
                for a in range(na) for k, chip in enumerate(chips)]

    def start(ins, slots, sems):
        for cp in copies(ins, slots, sems):
            cp.start()

    def finish(ins, slots, sems):
        for cp in copies(ins, slots, sems):
            cp.wait()

    return _Hosted(tuple(sums), False, tuple(jax.ShapeDtypeStruct((3,) + piece_shape(a), sums[a].dtype) for a in range(na)),
                   (pltpu.SemaphoreType.DMA((na * 3,)),) * 2, (start, None, finish))


def _sum_slots(sums, slots, axis, chip, core, layer, n_layers, name, into=None):
    _, hr, cs = slots.shape
    rb = _row_block(hr, cs)

    def body(at_ref, own_ref, s_ref, *rest):
        rest[-1][...] = ((own_ref[...].astype(F32) + s_ref[0].astype(F32)) + s_ref[1].astype(F32)) + s_ref[2].astype(F32)

    if axis == 0:
        own = pl.BlockSpec((None, rb, cs), lambda k, at_ref: (at_ref[0], k, 0))
    else:
        own = pl.BlockSpec((rb, cs), lambda k, at_ref: (k, at_ref[0]))
    in_specs = [own, pl.BlockSpec((3, rb, cs), lambda k, at_ref: (0, k, 0))]
    operands = (sums, slots)
    aliases = {}
    if into is not None:
        in_specs.append(pl.BlockSpec(memory_space=pl.ANY))
        operands = (sums, slots, into)
        aliases = {3: 0}
    return pl.pallas_call(
        body, name=name,
        grid_spec=pltpu.PrefetchScalarGridSpec(
            num_scalar_prefetch=1, grid=(hr // rb,), in_specs=in_specs,
            out_specs=pl.BlockSpec((None, None, rb, cs), lambda k, at_ref: (layer, at_ref[1], k, 0))),
        out_shape=jax.ShapeDtypeStruct((n_layers, 2, hr, cs), F32), input_output_aliases=aliases,
        compiler_params=_params(("parallel",)),
    )(_scalars(chip, core), *operands)


def _sibling_assemble(shards):
    na = len(shards)

    def body(*refs):
        outs = refs[na:2 * na]
        send, recv = refs[2 * na:]
        x, y, c, _ = _place()
        copies = []
        for a in range(na):
            hr = outs[a].shape[1] // 2
            mine = outs[a].at[:, _rows(c * hr, hr), :]
            cp = _remote(mine, mine, send.at[a], recv.at[a], (x, y, 1 - c))
            cp.start()
            copies.append(cp)
        for cp in copies:
            cp.wait()

    return pl.pallas_call(
        body, name="grad_sibling_assemble", in_specs=[HB] * na, out_specs=[HB] * na,
        out_shape=[jax.ShapeDtypeStruct(s.shape, F32) for s in shards], input_output_aliases={a: a for a in range(na)},
        scratch_shapes=[pltpu.SemaphoreType.DMA((na,))] * 2,
        compiler_params=pltpu.CompilerParams(has_side_effects=True),
    )(*shards)


def _allreduce_small(buf):
    rows, w = buf.shape
    half = rows // 2

    def body(buf_ref, out_ref, land, slots, red, sems_send, sems_recv):
        x, y, c, chips = _place()
        me = 2 * x + y
        sibling = (x, y, 1 - c)
        first = _remote(buf_ref, land, sems_send.at[0], sems_recv.at[0], sibling)
        first.start()
        first.wait()
        mine = pl.ds(pl.multiple_of(c * half, 8), half)
        slots[me] = buf_ref[mine, :] + land[mine, :]
        sends = []
        for k, chip in enumerate(chips):
            cp = _remote(slots.at[me], slots.at[me], sems_send.at[1 + k], sems_recv.at[1 + k], (*chip, c))
            cp.start()
            sends.append(cp)
        for k, chip in enumerate(chips):
            got = slots.at[2 * chip[0] + chip[1]]
            _remote(got, got, sems_send.at[1 + k], sems_recv.at[1 + k], sibling).wait_recv()
        red[...] = ((slots[0] + slots[1]) + slots[2]) + slots[3]
        out_ref[mine, :] = red[...]
        last = _remote(red, out_ref.at[mine, :], sems_send.at[4], sems_recv.at[4], sibling)
        last.start()
        theirs = out_ref.at[pl.ds(pl.multiple_of((1 - c) * half, 8), half), :]
        _remote(red, theirs, sems_send.at[4], sems_recv.at[4], sibling).wait_recv()
        for cp in sends:
            cp.wait_send()
        last.wait_send()

    return pl.pallas_call(
        body, name="allreduce_small", in_specs=[VM], out_specs=VM, out_shape=jax.ShapeDtypeStruct((rows, w), F32),
        scratch_shapes=[pltpu.VMEM((rows, w), F32), pltpu.VMEM((4, half, w), F32), pltpu.VMEM((half, w), F32),
                        pltpu.SemaphoreType.DMA((5,)), pltpu.SemaphoreType.DMA((5,))],
        compiler_params=pltpu.CompilerParams(has_side_effects=True, vmem_limit_bytes=VMEM_LIMIT),
    )(buf)


BIG = ("w_in", "w_out", "wq", "wk", "wv", "wo", "w_up", "w_down")
BIG_AXIS = {"w_in": 1, "w_out": 0, "wq": 0, "wk": 0, "wv": 0, "wo": 0, "w_up": 1, "w_down": 0}
SMALL = ("norm_mix_g", "pool_w", "pool_scale", "sgu_g", "sgu_w", "sgu_b", "norm_xattn_g", "mem_norm_g", "norm_ffn_g",
         "conv_w", "conv_b", "final_norm_g")
ORDER = ("norm_mix_g", "w_in", "pool_w", "pool_scale", "sgu_g", "sgu_w", "sgu_b", "w_out", "norm_xattn_g", "mem_norm_g",
         "wq", "wk", "wv", "wo", "norm_ffn_g", "w_up", "conv_w", "conv_b", "w_down", "final_norm_g")
PACK_WIDTH = 512


def kernel(x, mem, norm_mix_g, w_in, pool_w, pool_scale, sgu_g, sgu_w, sgu_b, w_out, norm_xattn_g, mem_norm_g, wq, wk, wv, wo, norm_ffn_g, w_up, conv_w, conv_b, w_down, final_norm_g, loss_target, m_norm_mix_g, m_w_in, m_pool_w, m_pool_scale, m_sgu_g, m_sgu_w, m_sgu_b, m_w_out, m_norm_xattn_g, m_mem_norm_g, m_wq, m_wk, m_wv, m_wo, m_norm_ffn_g, m_w_up, m_conv_w, m_conv_b, m_w_down, m_final_norm_g, v_norm_mix_g, v_w_in, v_pool_w, v_pool_scale, v_sgu_g, v_sgu_w, v_sgu_b, v_w_out, v_norm_xattn_g, v_mem_norm_g, v_wq, v_wk, v_wv, v_wo, v_norm_ffn_g, v_w_up, v_conv_w, v_conv_b, v_w_down, v_final_norm_g):
    given = dict(locals())
    w = {n: given[n] for n in ORDER}
    mom = {n: given["m_" + n] for n in ORDER}
    var = {n: given["v_" + n] for n in ORDER}
    nl = w_in.shape[0]
    xs, mems, tgt = x[0], mem[0], loss_target[0]
    chip = 2 * lax.axis_index("x") + lax.axis_index("y")
    core = lax.axis_index("c")

    axes = [BIG_AXIS[n] for n in BIG]
    placed = [[_cast_place(w[n], l, BIG_AXIS[n], chip, f"place_{n}_{l}") for n in BIG] for l in range(nl)]
    conv_full = _allgather_conv(conv_w)
    full = [dict(zip(BIG, _run_hosted(_hosted_allgather(placed[0], axes), "allgather_weights")))]

    row = lambda a, l: a[l][None, :]
    saved = []
    h = xs
    for l in range(nl):
        fw = full[l]
        sbt = jnp.broadcast_to(sgu_b[l][:, :, None], sgu_w[l].shape)
        h1, proj, xn1, mix = _mixer_fwd(h, row(norm_mix_g, l), fw["w_in"], pool_w[l], row(pool_scale, l), row(sgu_g, l), sgu_w[l], sbt, fw["w_out"])
        k, v, memn = _kv_fwd(mems, row(mem_norm_g, l), fw["wk"], fw["wv"])
        h2, q, o, xn2 = _xattn_fwd(h1, row(norm_xattn_g, l), fw["wq"], k, v, fw["wo"])
        gather_next = _hosted_allgather(placed[l + 1], axes) if l + 1 < nl else None
        (h3, hh, xn3), gathered = _ffn_fwd(h2, row(norm_ffn_g, l), fw["w_up"], conv_full[l], row(conv_b, l), fw["w_down"], gather_next)
        if gather_next is not None:
            full.append(dict(zip(BIG, gathered)))
        saved.append(dict(h=h, h1=h1, h2=h2, proj=proj, xn1=xn1, mix=mix, k=k, v=v, memn=memn, q=q, o=o, xn2=xn2, hh=hh, xn3=xn3, sbt=sbt))
        h = h3

    dh, loss_part, g_final = _loss_head(h, final_norm_g[None, :], tgt)

    big_grads = {}
    small_grads = [None] * nl

    def weight_grad(n, a, b, l):
        big_grads[n] = _grad_matmul(a, b, "grad_" + n, l, nl, big_grads.get(n))

    def chip_sums(l):
        whole = [big_grads[n] for n in BIG]
        out = []
        for g, t, ax, n in zip(whole, _sibling_exchange(whole, axes, l), axes, BIG):
            gl, gr, gc = g.shape
            if ax == 0:
                out.append(_add_cast(g.reshape(gl * 4, 2, gr // 8, gc), t, core, l * 4, "grad_chip_sum_" + n))
            else:
                out.append(_add_cast(g.reshape(gl, 2, gr // 2, gc), t[None], core, l, "grad_chip_sum_" + n)[0])
        return out

    sums, slots = [None] * nl, [None] * nl
    for l in reversed(range(nl)):
        fw, s = full[l], saved[l]
        dh3 = dh
        scatter_above = _hosted_scatter(sums[l + 1], axes) if l + 1 < nl else None
        (dh2, dhh, act, g_cw, g_cb, g_nf), got = _ffn_bwd(dh3, s["h2"], s["hh"], row(norm_ffn_g, l), fw["w_up"], conv_full[l], row(conv_b, l), fw["w_down"], scatter_above)
        if scatter_above is not None:
            slots[l + 1] = got
        weight_grad("w_up", s["xn3"], dhh, l)
        weight_grad("w_down", act, dh3, l)
        dh1, dq, dk, dv, g_nx = _xattn_bwd(dh2, s["h1"], s["q"], row(norm_xattn_g, l), fw["wq"], s["k"], s["v"], fw["wo"])
        weight_grad("wq", s["xn2"], dq, l)
        weight_grad("wo", s["o"], dh2, l)
        weight_grad("wk", s["memn"], dk, l)
        weight_grad("wv", s["memn"], dv, l)
        g_mn = _kv_bwd(dk, dv, mems, fw["wk"], fw["wv"])
        dh0, dproj, g_nm, g_pw, g_ps, g_sg, g_sw, g_sbt = _mixer_bwd(dh1, s["h"], s["proj"], row(norm_mix_g, l), fw["w_in"], pool_w[l], row(pool_scale, l), row(sgu_g, l), sgu_w[l], s["sbt"], fw["w_out"])
        weight_grad("w_in", s["xn1"], dproj, l)
        weight_grad("w_out", s["mix"], dh1, l)
        small_grads[l] = dict(norm_mix_g=g_nm, pool_w=g_pw, pool_scale=g_ps, sgu_g=g_sg, sgu_w=g_sw, sgu_b=jnp.sum(g_sbt, axis=-1),
                              norm_xattn_g=g_nx, mem_norm_g=g_mn, norm_ffn_g=g_nf, conv_w=g_cw, conv_b=g_cb)
        dh = dh0
        sums[l] = chip_sums(l)
    grad_x = dh[None]

    slots[0] = _run_hosted(_hosted_scatter(sums[0], axes), "grad_chip_scatter")
    halves = []
    for a, n in enumerate(BIG):
        buf = None
        for l in range(nl):
            buf = _sum_slots(sums[l][a], slots[l][a], axes[a], chip, core, l, nl, "grad_sum_" + n, buf)
        halves.append(buf.reshape(nl, 2 * buf.shape[2], buf.shape[3]))
    shard_grads = dict(zip(BIG, _sibling_assemble(halves)))

    layered = [n for n in SMALL if n != "final_norm_g"]
    parts = [small_grads[l][n].reshape(-1, PACK_WIDTH) for n in layered for l in range(nl)]
    parts.append(g_final.reshape(-1, PACK_WIDTH))
    parts.append(jnp.pad(loss_part, ((0, 0), (0, PACK_WIDTH - 1))))
    used = sum(p.shape[0] for p in parts)
    total = -(-used // 16) * 16
    packed = _allreduce_small(jnp.concatenate(parts + [jnp.zeros((total - used, PACK_WIDTH), F32)], axis=0))
    grads = dict(shard_grads)
    at = 0
    for n in layered:
        per_layer = []
        for l in range(nl):
            shape = small_grads[l][n].shape
            nrow = small_grads[l][n].size // PACK_WIDTH
            per_layer.append(packed[at:at + nrow].reshape(shape))
            at += nrow
        g = jnp.stack(per_layer)
        if n == "conv_w":
            cs = conv_w.shape[2]
            g = lax.dynamic_slice_in_dim(g, chip * cs, cs, axis=2)
        grads[n] = g.reshape(w[n].shape)
    grads["final_norm_g"] = packed[at:at + g_final.size // PACK_WIDTH].reshape(final_norm_g.shape)
    at += g_final.size // PACK_WIDTH
    loss = packed[at, 0]

    delta, new_m, new_v = {}, {}, {}
    for n in BIG:
        delta[n], new_m[n], new_v[n] = _adamw_big(w[n], grads[n], mom[n], var[n], "adamw_" + n)
    two_d = lambda a: a.reshape(-1, a.shape[-1])
    ds, nms, nvs = _adamw_small([two_d(w[n]) for n in SMALL], [two_d(grads[n]) for n in SMALL],
                                [two_d(mom[n]) for n in SMALL], [two_d(var[n]) for n in SMALL])
    for n, d_, m_, v_ in zip(SMALL, ds, nms, nvs):
        delta[n], new_m[n], new_v[n] = d_.reshape(w[n].shape), m_.reshape(w[n].shape), v_.reshape(w[n].shape)

    return (loss, grad_x, *[grads[n] for n in ORDER], *[delta[n] for n in ORDER], *[new_m[n] for n in ORDER], *[new_v[n] for n in ORDER])
```

```python
import math
from typing import NamedTuple

import jax
import jax.numpy as jnp
from jax import lax
from jax.experimental import pallas as pl
from jax.experimental.pallas import tpu as pltpu

F32 = jnp.float32
_MXU = jnp.bfloat16
_PAY = jnp.bfloat16
EPS = 1e-6
WINDOWS = (2, 4, 8, 16)
GROUP = 128
N_XHEADS = 4
HALO = 16
FF_TILE = 256
VMEM_LIMIT = 56 * 1024 * 1024
MESH = pl.DeviceIdType.MESH

ADAM_LR, ADAM_B1, ADAM_B2, ADAM_EPS, ADAM_WD, ADAM_STEP = 0.001, 0.9, 0.999, 1e-08, 0.01, 10

VM = pl.BlockSpec(memory_space=pltpu.VMEM)
HB = pl.BlockSpec(memory_space=pltpu.HBM)


def _nn(a, b):
    return jnp.dot(a, b, preferred_element_type=F32)


def _nt(a, b):
    return lax.dot_general(a, b, (((1,), (1,)), ((), ())), preferred_element_type=F32)


def _tn(a, b):
    return lax.dot_general(a, b, (((0,), (0,)), ((), ())), preferred_element_type=F32)


def _rms(x):
    r = lax.rsqrt(jnp.mean(x * x, axis=-1, keepdims=True) + EPS)
    return x * r, r


def _rms_bwd(dxn, xhat, r, g):
    dxh = dxn * g
    dx = r * (dxh - xhat * jnp.mean(dxh * xhat, axis=-1, keepdims=True))
    return dx, jnp.sum(dxn * xhat, axis=0, keepdims=True)


def _gelu(x):
    cdf = 0.5 * (1.0 + lax.erf(x * (2.0 ** -0.5)))
    return x * cdf, cdf


def _gelu_grad(x, cdf):
    return cdf + x * jnp.exp(-0.5 * x * x) * ((2.0 * math.pi) ** -0.5)


def _params(sem=None):
    return pltpu.CompilerParams(dimension_semantics=sem, vmem_limit_bytes=VMEM_LIMIT)


def _token_block(t, want):
    return want if t % want == 0 and t > want else GROUP


def _const_spec(shape):
    n = len(shape)
    return pl.BlockSpec(shape, lambda i: (0,) * n)


def _tril():
    return lax.broadcasted_iota(jnp.int32, (GROUP, GROUP), 0) >= lax.broadcasted_iota(jnp.int32, (GROUP, GROUP), 1)


def _shift_rows(x, k, edge):
    tb = x.shape[0]
    r8 = lax.broadcasted_iota(jnp.int32, (8, 1), 0)
    rolled = pltpu.roll(x, k % tb, 0)
    if k > 0:
        top = jnp.where(r8 < k, pltpu.roll(edge, k, 0), rolled[0:8, :])
        return jnp.concatenate([top, rolled[8:, :]], axis=0)
    bottom = jnp.where(r8 >= 8 + k, pltpu.roll(edge, 8 + k, 0), rolled[tb - 8:, :])
    return jnp.concatenate([rolled[:tb - 8, :], bottom], axis=0)


class _Hosted(NamedTuple):
    operands: tuple
    aliased: bool
    out_shapes: tuple
    sems: tuple
    stages: tuple


def _hosted_results(hosted):
    if hosted.aliased:
        return [jax.ShapeDtypeStruct(o.shape, o.dtype) for o in hosted.operands]
    return list(hosted.out_shapes)


def _call_hosting(main_body, hosted, *, name, steps, in_specs, out_specs, out_shape, scratch_shapes, operands):
    if hosted is None:
        outs = pl.pallas_call(main_body, name=name, grid=(steps,), in_specs=in_specs, out_specs=out_specs, out_shape=out_shape,
                              scratch_shapes=scratch_shapes, compiler_params=_params(("arbitrary",)))(*operands)
        return outs, ()
    n_in, n_out, n_sc, nh = len(in_specs), len(out_specs), len(scratch_shapes), len(hosted.operands)
    h_shapes = _hosted_results(hosted)
    start, middle, finish = hosted.stages

    def body(*refs):
        at = [0]

        def take(n):
            at[0] += n
            return refs[at[0] - n:at[0]]

        ins, h_in, outs, h_out, scratch, h_sems = take(n_in), take(nh), take(n_out), take(len(h_shapes)), take(n_sc), take(len(hosted.sems))
        step = pl.program_id(0)

        @pl.when(step == 0)
        def _():
            start(h_in, h_out, h_sems)

        if middle is not None:
            @pl.when(step == (3 * steps) // 4)
            def _():
                middle(h_in, h_out, h_sems)

        main_body(*ins, *outs, *scratch)

        @pl.when(step == steps - 1)
        def _():
            finish(h_in, h_out, h_sems)

    outs = pl.pallas_call(
        body, name=name, grid=(steps,), in_specs=list(in_specs) + [HB] * nh, out_specs=list(out_specs) + [HB] * len(h_shapes),
        out_shape=list(out_shape) + h_shapes, scratch_shapes=list(scratch_shapes) + list(hosted.sems),
        input_output_aliases={n_in + i: n_out + i for i in range(nh)} if hosted.aliased else {},
        compiler_params=_params(("arbitrary",)),
    )(*operands, *hosted.operands)
    return outs[:n_out], outs[n_out:]


def _run_hosted(hosted, name):
    nh = len(hosted.operands)
    h_shapes = _hosted_results(hosted)

    def body(*refs):
        h_in, h_out, h_sems = refs[:nh], refs[nh:nh + len(h_shapes)], refs[nh + len(h_shapes):]
        for stage in hosted.stages:
            if stage is not None:
                stage(h_in, h_out, h_sems)

    return pl.pallas_call(
        body, name=name, in_specs=[HB] * nh, out_specs=[HB] * len(h_shapes), out_shape=h_shapes, scratch_shapes=list(hosted.sems),
        input_output_aliases={i: i for i in range(nh)} if hosted.aliased else {},
        compiler_params=pltpu.CompilerParams(has_side_effects=True),
    )(*hosted.operands)


def _window_sums(e, win, back):
    n = e.shape[0]
    k = 1
    while k < win:
        e = e + pltpu.roll(e, k if back else n - k, 0)
        k *= 2
    return e


def _pool_diff(prev, p, t0, gi, win):
    sl = slice(gi * GROUP, (gi + 1) * GROUP)
    tb = p.shape[0]
    s = _window_sums(jnp.concatenate([prev[:, sl], p[:, sl]], axis=0), win, True)[HALO:, :]
    tglob = t0 + lax.broadcasted_iota(jnp.int32, (tb, 1), 0)
    cnt = jnp.minimum(tglob + 1, win).astype(F32)
    return s / cnt - p[:, sl], cnt


def _layernorm(v):
    xc = v - jnp.mean(v, axis=-1, keepdims=True)
    rstd = lax.rsqrt(jnp.mean(xc * xc, axis=-1, keepdims=True) + EPS)
    return xc * rstd, rstd


def _mixer_fwd(h, g, w_in, pool_w, pool_scale, sgu_g, sgu_w, sgu_bt, w_out):
    t, d = h.shape
    pw = pool_w.shape[0] * GROUP
    sw = sgu_w.shape[0] * GROUP
    tb = _token_block(t, 512)

    def body(h_ref, g_ref, win_ref, pw_ref, ps_ref, sg_ref, sw_ref, sbt_ref, wout_ref, h1_ref, proj_ref, xn_ref, mix_ref, pext):
        i = pl.program_id(0)

        @pl.when(i == 0)
        def _():
            pext[...] = jnp.zeros((HALO, pw), F32)

        x = h_ref[...]
        xhat, _ = _rms(x)
        xn = (xhat * g_ref[...]).astype(_MXU)
        xn_ref[...] = xn
        proj = _nn(xn, win_ref[...])
        proj_ref[...] = proj
        p = proj[:, :pw]
        prev = pext[...]
        for gi, win in enumerate(WINDOWS):
            sl = slice(gi * GROUP, (gi + 1) * GROUP)
            dg, _ = _pool_diff(prev, p, i * tb, gi, win)
            e = _nn(dg.astype(_MXU), pw_ref[gi].astype(_MXU))
            mix_ref[:, sl] = (e * ps_ref[:, sl]).astype(_MXU)
        pext[...] = p[tb - HALO:tb, :]
        uv, _ = _gelu(proj[:, pw:])
        u = uv[:, :sw]
        vhat, _ = _layernorm(uv[:, sw:])
        vn = (vhat * sg_ref[...]).astype(_MXU)
        mask = _tril()
        for hh in range(sw // GROUP):
            wm = jnp.where(mask, sw_ref[hh], 0.0).astype(_MXU)
            for n in range(tb // GROUP):
                rows = slice(n * GROUP, (n + 1) * GROUP)
                cols = slice(hh * GROUP, (hh + 1) * GROUP)
                z = _nn(wm, vn[rows, cols]) + sbt_ref[hh]
                mix_ref[rows, pw + hh * GROUP:pw + (hh + 1) * GROUP] = (u[rows, cols] * z).astype(_MXU)
        h1_ref[...] = x + _nn(mix_ref[...], wout_ref[...])

    blk = lambda w: pl.BlockSpec((tb, w), lambda i: (i, 0))
    return pl.pallas_call(
        body, name="mixer_fwd", grid=(t // tb,),
        in_specs=[blk(d), VM, VM, VM, VM, VM, VM, VM, VM],
        out_specs=[blk(d), blk(w_in.shape[1]), blk(d), blk(d)],
        out_shape=[jax.ShapeDtypeStruct((t, d), F32), jax.ShapeDtypeStruct((t, w_in.shape[1]), F32),
                   jax.ShapeDtypeStruct((t, d), _MXU), jax.ShapeDtypeStruct((t, d), _MXU)],
        scratch_shapes=[pltpu.VMEM((HALO, pw), F32)],
        compiler_params=_params(("arbitrary",)),
    )(h, g, w_in, pool_w, pool_scale, sgu_g, sgu_w, sgu_bt, w_out)


def _mixer_bwd(dh1, h, proj, g, w_in, pool_w, pool_scale, sgu_g, sgu_w, sgu_bt, w_out):
    t, d = h.shape
    ng, nh = pool_w.shape[0], sgu_w.shape[0]
    pw, sw = ng * GROUP, nh * GROUP
    tb = _token_block(t, 256)
    nb = t // tb

    def body(dh1_ref, h_ref, proj_ref, halo_ref, g_ref, win_ref, pw_ref, ps_ref, sg_ref, sw_ref, sbt_ref, wout_ref,
             dh_ref, dproj_ref, gg_ref, gpw_ref, gps_ref, gsg_ref, gsw_ref, gsbt_ref, dext, duv):
        i = pl.program_id(0)
        blk = nb - 1 - i

        @pl.when(i == 0)
        def _():
            for r in (gg_ref, gpw_ref, gps_ref, gsg_ref, gsw_ref, gsbt_ref, dext):
                r[...] = jnp.zeros(r.shape, F32)

        dh1v = dh1_ref[...]
        dmix = _nt(dh1v.astype(_MXU), wout_ref[...])
        proj_v = proj_ref[...]
        p = proj_v[:, :pw]
        prev = jnp.where(blk == 0, 0.0, halo_ref[...])
        for gi, win in enumerate(WINDOWS):
            sl = slice(gi * GROUP, (gi + 1) * GROUP)
            dg, cnt = _pool_diff(prev, p, blk * tb, gi, win)
            dgm = dg.astype(_MXU)
            pwm = pw_ref[gi].astype(_MXU)
            e = _nn(dgm, pwm)
            dy = dmix[:, sl]
            gps_ref[:, sl] += jnp.sum(dy * e, axis=0, keepdims=True)
            de = (dy * ps_ref[:, sl]).astype(_MXU)
            gpw_ref[gi] += _tn(dgm, de)
            dd = _nt(de, pwm)
            ddc = dd / cnt
            acc = _window_sums(jnp.concatenate([ddc, dext[:, sl]], axis=0), win, False)[:tb, :]
            dext[:, sl] = ddc[0:HALO, :]
            dproj_ref[:, sl] = (acc - dd).astype(_MXU)
        pre = proj_v[:, pw:]
        uv, cdf = _gelu(pre)
        u = uv[:, :sw]
        vhat, rstd = _layernorm(uv[:, sw:])
        vn = (vhat * sg_ref[...]).astype(_MXU)
        mask = _tril()
        for hh in range(nh):
            wm = jnp.where(mask, sw_ref[hh], 0.0).astype(_MXU)
            cols = slice(hh * GROUP, (hh + 1) * GROUP)
            gw = jnp.zeros((GROUP, GROUP), F32)
            gb = jnp.zeros((GROUP, GROUP), F32)
            for n in range(tb // GROUP):
                rows = slice(n * GROUP, (n + 1) * GROUP)
                vs = vn[rows, cols]
                z = _nn(wm, vs) + sbt_ref[hh]
                dy = dmix[rows, pw + hh * GROUP:pw + (hh + 1) * GROUP]
                dz = dy * u[rows, cols]
                gb = gb + dz
                dzm = dz.astype(_MXU)
                gw = gw + _nt(dzm, vs)
                duv[rows, cols] = dy * z
                duv[rows, sw + hh * GROUP:sw + (hh + 1) * GROUP] = _tn(wm, dzm)
            gsw_ref[hh] += jnp.where(mask, gw, 0.0)
            gsbt_ref[hh] += gb
        dvn = duv[:, sw:]
        gsg_ref[...] += jnp.sum(dvn * vhat, axis=0, keepdims=True)
        dxh = dvn * sg_ref[...]
        dv = rstd * (dxh - jnp.mean(dxh, axis=-1, keepdims=True) - vhat * jnp.mean(dxh * vhat, axis=-1, keepdims=True))
        gp = _gelu_grad(pre, cdf)
        dproj_ref[:, pw:pw + sw] = (duv[:, :sw] * gp[:, :sw]).astype(_MXU)
        dproj_ref[:, pw + sw:] = (dv * gp[:, sw:]).astype(_MXU)
        dxn = _nt(dproj_ref[...], win_ref[...])
        xhat, r = _rms(h_ref[...])
        dx, gg = _rms_bwd(dxn, xhat, r, g_ref[...])
        gg_ref[...] += gg
        dh_ref[...] = dh1v + dx

    rev = lambda w: pl.BlockSpec((tb, w), lambda i: (nb - 1 - i, 0))
    halo = pl.BlockSpec((HALO, pw), lambda i: (jnp.maximum((nb - 1 - i) * (tb // HALO) - 1, 0), 0))
    small = [(1, d), (ng, GROUP, GROUP), (1, pw), (1, sw), (nh, GROUP, GROUP), (nh, GROUP, GROUP)]
    return pl.pallas_call(
        body, name="mixer_bwd", grid=(nb,),
        in_specs=[rev(d), rev(d), rev(proj.shape[1]), halo, VM, VM, VM, VM, VM, VM, VM, VM],
        out_specs=[rev(d), rev(proj.shape[1])] + [_const_spec(s) for s in small],
        out_shape=[jax.ShapeDtypeStruct((t, d), F32), jax.ShapeDtypeStruct(proj.shape, _MXU)]
        + [jax.ShapeDtypeStruct(s, F32) for s in small],
        scratch_shapes=[pltpu.VMEM((HALO, pw), F32), pltpu.VMEM((tb, 2 * sw), F32)],
        compiler_params=_params(("arbitrary",)),
    )(dh1, h, proj, proj, g, w_in, pool_w, pool_scale, sgu_g, sgu_w, sgu_bt, w_out)


def _kv_fwd(mem, gm, wk, wv):
    n, d = mem.shape

    def body(mem_ref, gm_ref, wk_ref, wv_ref, k_ref, v_ref, memn_ref):
        xhat, _ = _rms(mem_ref[...])
        memn = (xhat * gm_ref[...]).astype(_MXU)
        memn_ref[...] = memn
        k_ref[...] = _nn(memn, wk_ref[...]).astype(_MXU)
        v_ref[...] = _nn(memn, wv_ref[...]).astype(_MXU)

    return pl.pallas_call(
        body, name="kv_fwd", in_specs=[VM] * 4, out_specs=[VM] * 3,
        out_shape=[jax.ShapeDtypeStruct((n, d), _MXU)] * 3, compiler_params=_params(),
    )(mem, gm, wk, wv)


def _kv_bwd(dk, dv, mem, wk, wv):
    n, d = mem.shape

    def body(dk_ref, dv_ref, mem_ref, wk_ref, wv_ref, ggm_ref):
        dmemn = _nt(dk_ref[...].astype(_MXU), wk_ref[...]) + _nt(dv_ref[...].astype(_MXU), wv_ref[...])
        xhat, _ = _rms(mem_ref[...])
        ggm_ref[...] = jnp.sum(dmemn * xhat, axis=0, keepdims=True)

    return pl.pallas_call(
        body, name="kv_bwd", in_specs=[VM] * 5, out_specs=VM,
        out_shape=jax.ShapeDtypeStruct((1, d), F32), compiler_params=_params(),
    )(dk, dv, mem, wk, wv)


def _softmax_rows(qm, k_ref, sl, scale):
    s = _nt(qm, k_ref[:, sl]) * scale
    e = jnp.exp(s - jnp.max(s, axis=-1, keepdims=True))
    return e / jnp.sum(e, axis=-1, keepdims=True)


def _xattn_fwd(h, g, wq, k, v, wo):
    t, d = h.shape
    hd = d // N_XHEADS
    scale = hd ** -0.5
    tb = _token_block(t, 512)

    def body(h_ref, g_ref, wq_ref, k_ref, v_ref, wo_ref, h2_ref, q_ref, o_ref, xn_ref):
        x = h_ref[...]
        xhat, _ = _rms(x)
        xn = (xhat * g_ref[...]).astype(_MXU)
        xn_ref[...] = xn
        qm = _nn(xn, wq_ref[...]).astype(_MXU)
        q_ref[...] = qm
        for a in range(N_XHEADS):
            sl = slice(a * hd, (a + 1) * hd)
            pr = _softmax_rows(qm[:, sl], k_ref, sl, scale)
            o_ref[:, sl] = _nn(pr.astype(_MXU), v_ref[:, sl]).astype(_MXU)
        h2_ref[...] = x + _nn(o_ref[...], wo_ref[...])

    blk = pl.BlockSpec((tb, d), lambda i: (i, 0))
    return pl.pallas_call(
        body, name="xattn_fwd", grid=(t // tb,),
        in_specs=[blk, VM, VM, VM, VM, VM], out_specs=[blk] * 4,
        out_shape=[jax.ShapeDtypeStruct((t, d), F32)] + [jax.ShapeDtypeStruct((t, d), _MXU)] * 3,
        compiler_params=_params(("arbitrary",)),
    )(h, g, wq, k, v, wo)


def _xattn_bwd(dh2, h, q, g, wq, k, v, wo):
    t, d = h.shape
    n = k.shape[0]
    hd = d // N_XHEADS
    scale = hd ** -0.5
    tb = _token_block(t, 512)

    def body(dh2_ref, h_ref, q_ref, g_ref, wq_ref, k_ref, v_ref, wo_ref, dh_ref, dq_ref, dk_ref, dv_ref, gg_ref):
        @pl.when(pl.program_id(0) == 0)
        def _():
            for r in (dk_ref, dv_ref, gg_ref):
                r[...] = jnp.zeros(r.shape, F32)

        dh2v = dh2_ref[...]
        dom = _nt(dh2v.astype(_MXU), wo_ref[...]).astype(_MXU)
        for a in range(N_XHEADS):
            sl = slice(a * hd, (a + 1) * hd)
            qh = q_ref[:, sl]
            pr = _softmax_rows(qh, k_ref, sl, scale)
            dv_ref[:, sl] += _tn(pr.astype(_MXU), dom[:, sl])
            dpr = _nt(dom[:, sl], v_ref[:, sl])
            ds = (pr * (dpr - jnp.sum(dpr * pr, axis=-1, keepdims=True)) * scale).astype(_MXU)
            dq_ref[:, sl] = _nn(ds, k_ref[:, sl]).astype(_MXU)
            dk_ref[:, sl] += _tn(ds, qh)
        dxn = _nt(dq_ref[...], wq_ref[...])
        xhat, r = _rms(h_ref[...])
        dx, gg = _rms_bwd(dxn, xhat, r, g_ref[...])
        gg_ref[...] += gg
        dh_ref[...] = dh2v + dx

    blk = pl.BlockSpec((tb, d), lambda i: (i, 0))
    return pl.pallas_call(
        body, name="xattn_bwd", grid=(t // tb,),
        in_specs=[blk, blk, blk, VM, VM, VM, VM, VM],
        out_specs=[blk, blk, _const_spec((n, d)), _const_spec((n, d)), _const_spec((1, d))],
        out_shape=[jax.ShapeDtypeStruct((t, d), F32), jax.ShapeDtypeStruct((t, d), _MXU),
                   jax.ShapeDtypeStruct((n, d), F32), jax.ShapeDtypeStruct((n, d), F32), jax.ShapeDtypeStruct((1, d), F32)],
        compiler_params=_params(("arbitrary",)),
    )(dh2, h, q, g, wq, k, v, wo)


def _ffn_fwd(h, g, w_up, conv_w, conv_b, w_down, hosted=None):
    t, d = h.shape
    f = w_down.shape[0]
    ft = FF_TILE
    tb = _token_block(t, 512)

    def body(h_ref, g_ref, wup_ref, cw_ref, cb_ref, wdown_ref, h3_ref, hh_ref, xn_ref, carry):
        @pl.when(pl.program_id(0) == 0)
        def _():
            carry[...] = jnp.zeros(carry.shape, F32)

        x = h_ref[...]
        xhat, _ = _rms(x)
        xn = (xhat * g_ref[...]).astype(_MXU)
        xn_ref[...] = xn
        acc = jnp.zeros((tb, d), F32)
        for j in range(f // ft):
            hc = []
            for part, off in enumerate((j * ft, f + j * ft)):
                cols = slice(off, off + ft)
                cur = _nn(xn, wup_ref[:, cols])
                hh_ref[:, cols] = cur.astype(_MXU)
                before = carry[:, cols]
                carry[:, cols] = cur[tb - 8:tb, :]
                hc.append(cb_ref[:, cols] + cw_ref[0:1, cols] * _shift_rows(cur, 2, before)
                          + cw_ref[1:2, cols] * _shift_rows(cur, 1, before) + cw_ref[2:3, cols] * cur)
            act = (hc[0] * jax.nn.sigmoid(hc[0]) * hc[1]).astype(_MXU)
            acc = acc + _nn(act, wdown_ref[j * ft:(j + 1) * ft, :])
        h3_ref[...] = x + acc

    blk = lambda w: pl.BlockSpec((tb, w), lambda i: (i, 0))
    return _call_hosting(
        body, hosted, name="ffn_fwd", steps=t // tb,
        in_specs=[blk(d), VM, VM, VM, VM, VM], out_specs=[blk(d), blk(2 * f), blk(d)],
        out_shape=[jax.ShapeDtypeStruct((t, d), F32), jax.ShapeDtypeStruct((t, 2 * f), _MXU), jax.ShapeDtypeStruct((t, d), _MXU)],
        scratch_shapes=[pltpu.VMEM((8, 2 * f), F32)],
        operands=(h, g, w_up, conv_w, conv_b, w_down))


def _ffn_bwd(dh3, h, hh, g, w_up, conv_w, conv_b, w_down, hosted=None):
    t, d = h.shape
    f = w_down.shape[0]
    ft = FF_TILE
    tb = _token_block(t, 256)
    nb = t // tb

    def body(dh3_ref, h_ref, hh_ref, halo_ref, g_ref, wup_ref, cw_ref, cb_ref, wdown_ref,
             dh_ref, dhh_ref, act_ref, gcw_ref, gcb_ref, gg_ref, dcarry):
        i = pl.program_id(0)
        blk = nb - 1 - i

        @pl.when(i == 0)
        def _():
            for r in (gcw_ref, gcb_ref, gg_ref, dcarry):
                r[...] = jnp.zeros(r.shape, F32)

        dh3v = dh3_ref[...]
        dhm = dh3v.astype(_MXU)
        dxn = jnp.zeros((tb, d), F32)
        for j in range(f // ft):
            cur, back1, back2, hc = [], [], [], []
            for part, off in enumerate((j * ft, f + j * ft)):
                cols = slice(off, off + ft)
                c0 = hh_ref[:, cols].astype(F32)
                before = jnp.where(blk == 0, 0.0, halo_ref[:, cols].astype(F32))[HALO - 8:HALO, :]
                b1 = _shift_rows(c0, 1, before)
                b2 = _shift_rows(c0, 2, before)
                cur.append(c0)
                back1.append(b1)
                back2.append(b2)
                hc.append(cb_ref[:, cols] + cw_ref[0:1, cols] * b2 + cw_ref[1:2, cols] * b1 + cw_ref[2:3, cols] * c0)
            sg = jax.nn.sigmoid(hc[0])
            silu = hc[0] * sg
            act_ref[:, j * ft:(j + 1) * ft] = (silu * hc[1]).astype(_MXU)
            dact = _nt(dhm, wdown_ref[j * ft:(j + 1) * ft, :])
            dhc = (dact * hc[1] * sg * (1.0 + hc[0] * (1.0 - sg)), dact * silu)
            for part, off in enumerate((j * ft, f + j * ft)):
                cols = slice(off, off + ft)
                dc = dhc[part]
                gcb_ref[:, cols] += jnp.sum(dc, axis=0, keepdims=True)
                gcw_ref[0:1, cols] += jnp.sum(dc * back2[part], axis=0, keepdims=True)
                gcw_ref[1:2, cols] += jnp.sum(dc * back1[part], axis=0, keepdims=True)
                gcw_ref[2:3, cols] += jnp.sum(dc * cur[part], axis=0, keepdims=True)
                after = dcarry[:, cols]
                dhh = (cw_ref[2:3, cols] * dc + cw_ref[1:2, cols] * _shift_rows(dc, -1, after)
                       + cw_ref[0:1, cols] * _shift_rows(dc, -2, after)).astype(_MXU)
                dcarry[:, cols] = dc[0:8, :]
                dhh_ref[:, cols] = dhh
                dxn = dxn + _nt(dhh, wup_ref[:, cols])
        xhat, r = _rms(h_ref[...])
        dx, gg = _rms_bwd(dxn, xhat, r, g_ref[...])
        gg_ref[...] += gg
        dh_ref[...] = dh3v + dx

    rev = lambda w: pl.BlockSpec((tb, w), lambda i: (nb - 1 - i, 0))
    halo = pl.BlockSpec((HALO, 2 * f), lambda i: (jnp.maximum((nb - 1 - i) * (tb // HALO) - 1, 0), 0))
    return _call_hosting(
        body, hosted, name="ffn_bwd", steps=nb,
        in_specs=[rev(d), rev(d), rev(2 * f), halo, VM, VM, VM, VM, VM],
        out_specs=[rev(d), rev(2 * f), rev(f), _const_spec((3, 2 * f)), _const_spec((1, 2 * f)), _const_spec((1, d))],
        out_shape=[jax.ShapeDtypeStruct((t, d), F32), jax.ShapeDtypeStruct((t, 2 * f), _MXU), jax.ShapeDtypeStruct((t, f), _MXU),
                   jax.ShapeDtypeStruct((3, 2 * f), F32), jax.ShapeDtypeStruct((1, 2 * f), F32), jax.ShapeDtypeStruct((1, d), F32)],
        scratch_shapes=[pltpu.VMEM((8, 2 * f), F32)],
        operands=(dh3, h, hh, hh, g, w_up, conv_w, conv_b, w_down))


def _loss_head(h, g, target):
    t, d = h.shape
    tb = _token_block(t, 512)

    def body(h_ref, g_ref, tgt_ref, dh_ref, loss_ref, gg_ref):
        @pl.when(pl.program_id(0) == 0)
        def _():
            loss_ref[...] = jnp.zeros(loss_ref.shape, F32)
            gg_ref[...] = jnp.zeros(gg_ref.shape, F32)

        xhat, r = _rms(h_ref[...])
        err = xhat * g_ref[...] - tgt_ref[...]
        loss_ref[...] += 0.5 * jnp.sum(jnp.sum(err * err, axis=-1, keepdims=True), axis=0, keepdims=True) / d
        dx, gg = _rms_bwd(err / d, xhat, r, g_ref[...])
        gg_ref[...] += gg
        dh_ref[...] = dx

    blk = pl.BlockSpec((tb, d), lambda i: (i, 0))
    return pl.pallas_call(
        body, name="loss_head", grid=(t // tb,),
        in_specs=[blk, VM, blk], out_specs=[blk, _const_spec((1, 1)), _const_spec((1, d))],
        out_shape=[jax.ShapeDtypeStruct((t, d), F32), jax.ShapeDtypeStruct((1, 1), F32), jax.ShapeDtypeStruct((1, d), F32)],
        compiler_params=_params(("arbitrary",)),
    )(h, g, target)


def _largest_tile(n, cap, mult=128):
    best = None
    for c in range(mult, min(n, cap) + 1, mult):
        if n % c == 0:
            best = c
    return best if best is not None else n


def _grad_matmul(a, b, name, layer, n_layers, into=None):
    t, m = a.shape
    n = b.shape[1]
    tm, tn, tk = _largest_tile(m, 1408), _largest_tile(n, 1024), _largest_tile(t, 1024)
    nk = t // tk

    def body(a_ref, b_ref, *rest):
        o_ref = rest[-1]

        @pl.when(pl.program_id(2) == 0)
        def _():
            o_ref[...] = jnp.zeros(o_ref.shape, F32)

        o_ref[...] += _tn(a_ref[...].astype(_MXU), b_ref[...].astype(_MXU))

    in_specs = [pl.BlockSpec((tk, tm), lambda i, j, k: (k, i)), pl.BlockSpec((tk, tn), lambda i, j, k: (k, j))]
    operands = (a, b)
    aliases = {}
    if into is not None:
        in_specs.append(pl.BlockSpec(memory_space=pl.ANY))
        operands = (a, b, into)
        aliases = {2: 0}
    return pl.pallas_call(
        body, name=name, grid=(m // tm, n // tn, nk), in_specs=in_specs,
        out_specs=pl.BlockSpec((None, tm, tn), lambda i, j, k: (layer, i, j)),
        out_shape=jax.ShapeDtypeStruct((n_layers, m, n), F32), input_output_aliases=aliases,
        compiler_params=_params(("parallel", "parallel", "arbitrary")),
    )(*operands)


def _adamw_math(w, g, m, v):
    m = ADAM_B1 * m + (1.0 - ADAM_B1) * g
    v = ADAM_B2 * v + (1.0 - ADAM_B2) * (g * g)
    m_hat = m / (1.0 - ADAM_B1 ** ADAM_STEP)
    v_hat = v / (1.0 - ADAM_B2 ** ADAM_STEP)
    return -ADAM_LR * (m_hat / (jnp.sqrt(v_hat) + ADAM_EPS) + ADAM_WD * w), m, v


def _row_block(rows, cols, max_bytes=1 << 20, mult=16):
    best = None
    for r in range(mult, rows + 1, mult):
        if rows % r == 0 and r * cols * 4 <= max_bytes:
            best = r
    return best if best is not None else rows


def _adamw_big(w, g, m, v, name):
    shape = w.shape
    cols = shape[-1]
    flat = lambda a: a.reshape(-1, cols)
    rows = flat(w).shape[0]
    rb = _row_block(rows, cols)

    def body(w_ref, g_ref, m_ref, v_ref, d_ref, nm_ref, nv_ref):
        d_ref[...], nm_ref[...], nv_ref[...] = _adamw_math(w_ref[...], g_ref[...], m_ref[...], v_ref[...])

    blk = pl.BlockSpec((rb, cols), lambda i: (i, 0))
    outs = pl.pallas_call(
        body, name=name, grid=(rows // rb,), in_specs=[blk] * 4, out_specs=[blk] * 3,
        out_shape=[jax.ShapeDtypeStruct((rows, cols), F32)] * 3, compiler_params=_params(("parallel",)),
    )(flat(w), flat(g), flat(m), flat(v))
    return [o.reshape(shape) for o in outs]


def _adamw_small(ws, gs, ms, vs):
    n = len(ws)

    def body(*refs):
        for a in range(n):
            w_ref, g_ref, m_ref, v_ref = (refs[s * n + a] for s in range(4))
            d_ref, nm_ref, nv_ref = (refs[(4 + s) * n + a] for s in range(3))
            d_ref[...], nm_ref[...], nv_ref[...] = _adamw_math(w_ref[...], g_ref[...], m_ref[...], v_ref[...])

    outs = pl.pallas_call(
        body, name="adamw_small", in_specs=[VM] * (4 * n), out_specs=[VM] * (3 * n),
        out_shape=[jax.ShapeDtypeStruct(w.shape, F32) for w in ws] * 3, compiler_params=_params(),
    )(*ws, *gs, *ms, *vs)
    return outs[:n], outs[n:2 * n], outs[2 * n:]


def _place():
    x, y, c = lax.axis_index("x"), lax.axis_index("y"), lax.axis_index("c")
    chips = [(1 - x, y), (x, 1 - y), (1 - x, 1 - y)]
    return x, y, c, chips


def _rows(start, size, mult=16):
    return pl.ds(pl.multiple_of(start, mult), size)


def _full_window(ref, axis, chip, half=None):
    r, c = ref.shape
    if axis == 0:
        rs = r // 4
        if half is None:
            return ref.at[_rows(chip * rs, rs), :]
        return ref.at[_rows(chip * rs + half * (rs // 2), rs // 2), :]
    cs = c // 4
    if half is None:
        return ref.at[:, _rows(chip * cs, cs, 128)]
    return ref.at[_rows(half * (r // 2), r // 2), _rows(chip * cs, cs, 128)]


def _remote(src, dst, send_sem, recv_sem, to):
    return pltpu.make_async_remote_copy(src_ref=src, dst_ref=dst, send_sem=send_sem, recv_sem=recv_sem,
                                        device_id=to, device_id_type=MESH)


def _scalars(*vals):
    return jnp.stack([jnp.asarray(v, jnp.int32) for v in vals])


def _cast_place(shard, layer, axis, chip, name):
    _, rs, cs = shard.shape
    full = (rs * 4, cs) if axis == 0 else (rs, cs * 4)
    rb = _row_block(rs, cs)
    nrb = rs // rb

    def body(chip_ref, s_ref, o_ref):
        o_ref[...] = s_ref[...].astype(_PAY)

    if axis == 0:
        out_map = lambda i, chip_ref: (chip_ref[0] * nrb + i, 0)
    else:
        out_map = lambda i, chip_ref: (i, chip_ref[0])
    return pl.pallas_call(
        body, name=name,
        grid_spec=pltpu.PrefetchScalarGridSpec(
            num_scalar_prefetch=1, grid=(nrb,),
            in_specs=[pl.BlockSpec((None, rb, cs), lambda i, chip_ref: (layer, i, 0))],
            out_specs=pl.BlockSpec((rb, cs), out_map)),
        out_shape=jax.ShapeDtypeStruct(full, _PAY), compiler_params=_params(("parallel",)),
    )(_scalars(chip), shard)


def _hosted_allgather(placed, axes):
    n = len(placed)

    def each(outs, half_of):
        x, y, c, chips = _place()
        for i in range(n):
            for k, chip in enumerate(chips):
                yield i * 3 + k, (*chip, c), (x, y, 1 - c), _full_window(outs[i], axes[i], 2 * x + y, c), \
                    _full_window(outs[i], axes[i], 2 * chip[0] + chip[1], half_of(c))

    def start(_, outs, sems):
        send, recv, _, _ = sems
        for s, peer, _, mine, _ in each(outs, lambda c: c):
            _remote(mine, mine, send.at[s], recv.at[s], peer).start()

    def middle(_, outs, sems):
        send, recv, fsend, frecv = sems
        for s, _, sibling, _, got in each(outs, lambda c: c):
            _remote(got, got, send.at[s], recv.at[s], sibling).wait_recv()
            _remote(got, got, fsend.at[s], frecv.at[s], sibling).start()

    def finish(_, outs, sems):
        send, recv, fsend, frecv = sems
        for s, _, sibling, _, got in each(outs, lambda c: 1 - c):
            _remote(got, got, fsend.at[s], frecv.at[s], sibling).wait_recv()
        for s, peer, sibling, mine, got in each(outs, lambda c: c):
            _remote(mine, mine, send.at[s], recv.at[s], peer).wait_send()
            _remote(got, got, fsend.at[s], frecv.at[s], sibling).wait_send()

    return _Hosted(tuple(placed), True, (), (pltpu.SemaphoreType.DMA((n * 3,)),) * 4, (start, middle, finish))


def _allgather_conv(conv_shard):
    nl, taps, cs = conv_shard.shape

    def body(in_ref, out_ref, send, recv, local):
        x, y, c, chips = _place()
        mine = out_ref.at[:, :, _rows((2 * x + y) * cs, cs, 128)]
        own = pltpu.make_async_copy(in_ref, mine, local)
        own.start()
        sends = [_remote(in_ref, mine, send.at[k], recv.at[k], (*chip, c)) for k, chip in enumerate(chips)]
        for cp in sends:
            cp.start()
        for k, chip in enumerate(chips):
            got = out_ref.at[:, :, _rows((2 * chip[0] + chip[1]) * cs, cs, 128)]
            _remote(got, got, send.at[k], recv.at[k], (*chip, c)).wait_recv()
        for cp in sends:
            cp.wait_send()
        own.wait()

    return pl.pallas_call(
        body, name="allgather_conv", in_specs=[HB], out_specs=HB, out_shape=jax.ShapeDtypeStruct((nl, taps, cs * 4), conv_shard.dtype),
        scratch_shapes=[pltpu.SemaphoreType.DMA((3,)), pltpu.SemaphoreType.DMA((3,)), pltpu.SemaphoreType.DMA],
        compiler_params=pltpu.CompilerParams(has_side_effects=True),
    )(conv_shard)


def _sibling_exchange(grads, axes, layer):
    na = len(grads)
    views = [g.reshape(g.shape[0], 4, 2, g.shape[1] // 8, g.shape[2]) if ax == 0 else g for g, ax in zip(grads, axes)]

    def region(ref, axis, half):
        if axis == 0:
            return ref.at[layer, :, half]
        r = ref.shape[1]
        return ref.at[layer, _rows(half * (r // 2), r // 2), :]

    def body(*refs):
        ins, land = refs[:na], refs[na:2 * na]
        send, recv = refs[2 * na:]
        x, y, c, _ = _place()
        copies = []
        for a in range(na):
            cp = _remote(region(ins[a], axes[a], 1 - c), land[a], send.at[a], recv.at[a], (x, y, 1 - c))
            cp.start()
            copies.append(cp)
        for cp in copies:
            cp.wait()

    shapes = [(4, g.shape[1] // 8, g.shape[2]) if ax == 0 else (g.shape[1] // 2, g.shape[2]) for g, ax in zip(grads, axes)]
    return pl.pallas_call(
        body, name="grad_sibling_exchange", in_specs=[HB] * na, out_specs=[HB] * na,
        out_shape=[jax.ShapeDtypeStruct(s, F32) for s in shapes],
        scratch_shapes=[pltpu.SemaphoreType.DMA((na,))] * 2,
        compiler_params=pltpu.CompilerParams(has_side_effects=True),
    )(*views)


def _add_cast(mine, theirs, core, base, name):
    na, nb, cols = theirs.shape
    rb = _row_block(nb, cols)

    def body(core_ref, a_ref, b_ref, o_ref):
        o_ref[...] = (a_ref[...] + b_ref[...]).astype(_PAY)

    blk = pl.BlockSpec((None, rb, cols), lambda i, k, core_ref: (i, k, 0))
    return pl.pallas_call(
        body, name=name,
        grid_spec=pltpu.PrefetchScalarGridSpec(
            num_scalar_prefetch=1, grid=(na, nb // rb),
            in_specs=[pl.BlockSpec((None, None, rb, cols), lambda i, k, core_ref: (base + i, core_ref[0], k, 0)), blk], out_specs=blk),
        out_shape=jax.ShapeDtypeStruct((na, nb, cols), _PAY), compiler_params=_params(("parallel", "parallel")),
    )(_scalars(core), mine, theirs)


def _piece(ref, axis, chip):
    if axis == 0:
        return ref.at[chip]
    cs = ref.shape[1] // 4
    return ref.at[:, _rows(chip * cs, cs, 128)]


def _hosted_scatter(sums, axes):
    na = len(sums)

    def piece_shape(a):
        if axes[a] == 0:
            return (sums[a].shape[1], sums[a].shape[2])
        return (sums[a].shape[0], sums[a].shape[1] // 4)

    def copies(ins, slots, sems):
        send, recv = sems
        _, _, c, chips = _place()
        return [_remote(_piece(ins[a], axes[a], 2 * chip[0] + chip[1]), slots[a].at[k], send.at[a * 3 + k], recv.at[a * 3 + k], (*chip, c))
                for a in range(na) for k, chip in enumerate(chips)]

    def start(ins, slots, sems):
        for cp in copies(ins, slots, sems):
            cp.start()

    def finish(ins, slots, sems):
        for cp in copies(ins, slots, sems):
            cp.wait()

    return _Hosted(tuple(sums), False, tuple(jax.ShapeDtypeStruct((3,) + piece_shape(a), sums[a].dtype) for a in range(na)),
                   (pltpu.SemaphoreType.DMA((na * 3,)),) * 2, (start, None, finish))


def _sum_slots(sums, slots, axis, chip, core, layer, n_layers, name, into=None):
    _, hr, cs = slots.shape
    rb = _row_block(hr, cs)

    def body(at_ref, own_ref, s_ref, *rest):
        rest[-1][...] = ((own_ref[...].astype(F32) + s_ref[0].astype(F32)) + s_ref[1].astype(F32)) + s_ref[2].astype(F32)

    if axis == 0:
        own = pl.BlockSpec((None, rb, cs), lambda k, at_ref: (at_ref[0], k, 0))
    else:
        own = pl.BlockSpec((rb, cs), lambda k, at_ref: (k, at_ref[0]))
    in_specs = [own, pl.BlockSpec((3, rb, cs), lambda k, at_ref: (0, k, 0))]
    operands = (sums, slots)
    aliases = {}
    if into is not None:
        in_specs.append(pl.BlockSpec(memory_space=pl.ANY))
        operands = (sums, slots, into)
        aliases = {3: 0}
    return pl.pallas_call(
        body, name=name,
        grid_spec=pltpu.PrefetchScalarGridSpec(
            num_scalar_prefetch=1, grid=(hr // rb,), in_specs=in_specs,
            out_specs=pl.BlockSpec((None, None, rb, cs), lambda k, at_ref: (layer, at_ref[1], k, 0))),
        out_shape=jax.ShapeDtypeStruct((n_layers, 2, hr, cs), F32), input_output_aliases=aliases,
        compiler_params=_params(("parallel",)),
    )(_scalars(chip, core), *operands)


def _sibling_assemble(shards):
    na = len(shards)

    def body(*refs):
        outs = refs[na:2 * na]
        send, recv = refs[2 * na:]
        x, y, c, _ = _place()
        copies = []
        for a in range(na):
            hr = outs[a].shape[1] // 2
            mine = outs[a].at[:, _rows(c * hr, hr), :]
            cp = _remote(mine, mine, send.at[a], recv.at[a], (x, y, 1 - c))
            cp.start()
            copies.append(cp)
        for cp in copies:
            cp.wait()

    return pl.pallas_call(
        body, name="grad_sibling_assemble", in_specs=[HB] * na, out_specs=[HB] * na,
        out_shape=[jax.ShapeDtypeStruct(s.shape, F32) for s in shards], input_output_aliases={a: a for a in range(na)},
        scratch_shapes=[pltpu.SemaphoreType.DMA((na,))] * 2,
        compiler_params=pltpu.CompilerParams(has_side_effects=True),
    )(*shards)


def _allreduce_small(buf):
    rows, w = buf.shape
    half = rows // 2

    def body(buf_ref, out_ref, land, slots, red, sems_send, sems_recv):
        x, y, c, chips = _place()
        me = 2 * x + y
        sibling = (x, y, 1 - c)
        first = _remote(buf_ref, land, sems_send.at[0], sems_recv.at[0], sibling)
        first.start()
        first.wait()
        mine = pl.ds(pl.multiple_of(c * half, 8), half)
        slots[me] = buf_ref[mine, :] + land[mine, :]
        sends = []
        for k, chip in enumerate(chips):
            cp = _remote(slots.at[me], slots.at[me], sems_send.at[1 + k], sems_recv.at[1 + k], (*chip, c))
            cp.start()
            sends.append(cp)
        for k, chip in enumerate(chips):
            got = slots.at[2 * chip[0] + chip[1]]
            _remote(got, got, sems_send.at[1 + k], sems_recv.at[1 + k], sibling).wait_recv()
        red[...] = ((slots[0] + slots[1]) + slots[2]) + slots[3]
        out_ref[mine, :] = red[...]
        last = _remote(red, out_ref.at[mine, :], sems_send.at[4], sems_recv.at[4], sibling)
        last.start()
        theirs = out_ref.at[pl.ds(pl.multiple_of((1 - c) * half, 8), half), :]
        _remote(red, theirs, sems_send.at[4], sems_recv.at[4], sibling).wait_recv()
        for cp in sends:
            cp.wait_send()
        last.wait_send()

    return pl.pallas_call(
        body, name="allreduce_small", in_specs=[VM], out_specs=VM, out_shape=jax.ShapeDtypeStruct((rows, w), F32),
        scratch_shapes=[pltpu.VMEM((rows, w), F32), pltpu.VMEM((4, half, w), F32), pltpu.VMEM((half, w), F32),
                        pltpu.SemaphoreType.DMA((5,)), pltpu.SemaphoreType.DMA((5,))],
        compiler_params=pltpu.CompilerParams(has_side_effects=True, vmem_limit_bytes=VMEM_LIMIT),
    )(buf)


BIG = ("w_in", "w_out", "wq", "wk", "wv", "wo", "w_up", "w_down")
BIG_AXIS = {"w_in": 1, "w_out": 0, "wq": 0, "wk": 0, "wv": 0, "wo": 0, "w_up": 1, "w_down": 0}
SMALL = ("norm_mix_g", "pool_w", "pool_scale", "sgu_g", "sgu_w", "sgu_b", "norm_xattn_g", "mem_norm_g", "norm_ffn_g",
         "conv_w", "conv_b", "final_norm_g")
ORDER = ("norm_mix_g", "w_in", "pool_w", "pool_scale", "sgu_g", "sgu_w", "sgu_b", "w_out", "norm_xattn_g", "mem_norm_g",
         "wq", "wk", "wv", "wo", "norm_ffn_g", "w_up", "conv_w", "conv_b", "w_down", "final_norm_g")
PACK_WIDTH = 512


def kernel(x, mem, norm_mix_g, w_in, pool_w, pool_scale, sgu_g, sgu_w, sgu_b, w_out, norm_xattn_g, mem_norm_g, wq, wk, wv, wo, norm_ffn_g, w_up, conv_w, conv_b, w_down, final_norm_g, loss_target, m_norm_mix_g, m_w_in, m_pool_w, m_pool_scale, m_sgu_g, m_sgu_w, m_sgu_b, m_w_out, m_norm_xattn_g, m_mem_norm_g, m_wq, m_wk, m_wv, m_wo, m_norm_ffn_g, m_w_up, m_conv_w, m_conv_b, m_w_down, m_final_norm_g, v_norm_mix_g, v_w_in, v_pool_w, v_pool_scale, v_sgu_g, v_sgu_w, v_sgu_b, v_w_out, v_norm_xattn_g, v_mem_norm_g, v_wq, v_wk, v_wv, v_wo, v_norm_ffn_g, v_w_up, v_conv_w, v_conv_b, v_w_down, v_final_norm_g):
    given = dict(locals())
    w = {n: given[n] for n in ORDER}
    mom = {n: given["m_" + n] for n in ORDER}
    var = {n: given["v_" + n] for n in ORDER}
    nl = w_in.shape[0]
    xs, mems, tgt = x[0], mem[0], loss_target[0]
    chip = 2 * lax.axis_index("x") + lax.axis_index("y")
    core = lax.axis_index("c")

    axes = [BIG_AXIS[n] for n in BIG]
    placed = [[_cast_place(w[n], l, BIG_AXIS[n], chip, f"place_{n}_{l}") for n in BIG] for l in range(nl)]
    conv_full = _allgather_conv(conv_w)
    full = [dict(zip(BIG, _run_hosted(_hosted_allgather(placed[0], axes), "allgather_weights")))]

    row = lambda a, l: a[l][None, :]
    saved = []
    h = xs
    for l in range(nl):
        fw = full[l]
        sbt = jnp.broadcast_to(sgu_b[l][:, :, None], sgu_w[l].shape)
        h1, proj, xn1, mix = _mixer_fwd(h, row(norm_mix_g, l), fw["w_in"], pool_w[l], row(pool_scale, l), row(sgu_g, l), sgu_w[l], sbt, fw["w_out"])
        k, v, memn = _kv_fwd(mems, row(mem_norm_g, l), fw["wk"], fw["wv"])
        h2, q, o, xn2 = _xattn_fwd(h1, row(norm_xattn_g, l), fw["wq"], k, v, fw["wo"])
        gather_next = _hosted_allgather(placed[l + 1], axes) if l + 1 < nl else None
        (h3, hh, xn3), gathered = _ffn_fwd(h2, row(norm_ffn_g, l), fw["w_up"], conv_full[l], row(conv_b, l), fw["w_down"], gather_next)
        if gather_next is not None:
            full.append(dict(zip(BIG, gathered)))
        saved.append(dict(h=h, h1=h1, h2=h2, proj=proj, xn1=xn1, mix=mix, k=k, v=v, memn=memn, q=q, o=o, xn2=xn2, hh=hh, xn3=xn3, sbt=sbt))
        h = h3

    dh, loss_part, g_final = _loss_head(h, final_norm_g[None, :], tgt)

    big_grads = {}
    small_grads = [None] * nl

    def weight_grad(n, a, b, l):
        big_grads[n] = _grad_matmul(a, b, "grad_" + n, l, nl, big_grads.get(n))

    def chip_sums(l):
        whole = [big_grads[n] for n in BIG]
        out = []
        for g, t, ax, n in zip(whole, _sibling_exchange(whole, axes, l), axes, BIG):
            gl, gr, gc = g.shape
            if ax == 0:
                out.append(_add_cast(g.reshape(gl * 4, 2, gr // 8, gc), t, core, l * 4, "grad_chip_sum_" + n))
            else:
                out.append(_add_cast(g.reshape(gl, 2, gr // 2, gc), t[None], core, l, "grad_chip_sum_" + n)[0])
        return out

    sums, slots = [None] * nl, [None] * nl
    for l in reversed(range(nl)):
        fw, s = full[l], saved[l]
        dh3 = dh
        scatter_above = _hosted_scatter(sums[l + 1], axes) if l + 1 < nl else None
        (dh2, dhh, act, g_cw, g_cb, g_nf), got = _ffn_bwd(dh3, s["h2"], s["hh"], row(norm_ffn_g, l), fw["w_up"], conv_full[l], row(conv_b, l), fw["w_down"], scatter_above)
        if scatter_above is not None:
            slots[l + 1] = got
        weight_grad("w_up", s["xn3"], dhh, l)
        weight_grad("w_down", act, dh3, l)
        dh1, dq, dk, dv, g_nx = _xattn_bwd(dh2, s["h1"], s["q"], row(norm_xattn_g, l), fw["wq"], s["k"], s["v"], fw["wo"])
        weight_grad("wq", s["xn2"], dq, l)
        weight_grad("wo", s["o"], dh2, l)
        weight_grad("wk", s["memn"], dk, l)
        weight_grad("wv", s["memn"], dv, l)
        g_mn = _kv_bwd(dk, dv, mems, fw["wk"], fw["wv"])
        dh0, dproj, g_nm, g_pw, g_ps, g_sg, g_sw, g_sbt = _mixer_bwd(dh1, s["h"], s["proj"], row(norm_mix_g, l), fw["w_in"], pool_w[l], row(pool_scale, l), row(sgu_g, l), sgu_w[l], s["sbt"], fw["w_out"])
        weight_grad("w_in", s["xn1"], dproj, l)
        weight_grad("w_out", s["mix"], dh1, l)
        small_grads[l] = dict(norm_mix_g=g_nm, pool_w=g_pw, pool_scale=g_ps, sgu_g=g_sg, sgu_w=g_sw, sgu_b=jnp.sum(g_sbt, axis=-1),
                              norm_xattn_g=g_nx, mem_norm_g=g_mn, norm_ffn_g=g_nf, conv_w=g_cw, conv_b=g_cb)
        dh = dh0
        sums[l] = chip_sums(l)
    grad_x = dh[None]

    slots[0] = _run_hosted(_hosted_scatter(sums[0], axes), "grad_chip_scatter")
    halves = []
    for a, n in enumerate(BIG):
        buf = None
        for l in range(nl):
            buf = _sum_slots(sums[l][a], slots[l][a], axes[a], chip, core, l, nl, "grad_sum_" + n, buf)
        halves.append(buf.reshape(nl, 2 * buf.shape[2], buf.shape[3]))
    shard_grads = dict(zip(BIG, _sibling_assemble(halves)))

    layered = [n for n in SMALL if n != "final_norm_g"]
    parts = [small_grads[l][n].reshape(-1, PACK_WIDTH) for n in layered for l in range(nl)]
    parts.append(g_final.reshape(-1, PACK_WIDTH))
    parts.append(jnp.pad(loss_part, ((0, 0), (0, PACK_WIDTH - 1))))
    used = sum(p.shape[0] for p in parts)
    total = -(-used // 16) * 16
    packed = _allreduce_small(jnp.concatenate(parts + [jnp.zeros((total - used, PACK_WIDTH), F32)], axis=0))
    grads = dict(shard_grads)
    at = 0
    for n in layered:
        per_layer = []
        for l in range(nl):
            shape = small_grads[l][n].shape
            nrow = small_grads[l][n].size // PACK_WIDTH
            per_layer.append(packed[at:at + nrow].reshape(shape))
            at += nrow
        g = jnp.stack(per_layer)
        if n == "conv_w":
            cs = conv_w.shape[2]
            g = lax.dynamic_slice_in_dim(g, chip * cs, cs, axis=2)
        grads[n] = g.reshape(w[n].shape)
    grads["final_norm_g"] = packed[at:at + g_final.size // PACK_WIDTH].reshape(final_norm_g.shape)
    at += g_final.size // PACK_WIDTH
    loss = packed[at, 0]

    delta, new_m, new_v = {}, {}, {}
    for n in BIG:
        delta[n], new_m[n], new_v[n] = _adamw_big(w[n], grads[n], mom[n], var[n], "adamw_" + n)
    two_d = lambda a: a.reshape(-1, a.shape[-1])
    ds, nms, nvs = _adamw_small([two_d(w[n]) for n in SMALL], [two_d(grads[n]) for n in SMALL],
                                [two_d(mom[n]) for n in SMALL], [two_d(var[n]) for n in SMALL])
    for n, d_, m_, v_ in zip(SMALL, ds, nms, nvs):
        delta[n], new_m[n], new_v[n] = d_.reshape(w[n].shape), m_.reshape(w[n].shape), v_.reshape(w[n].shape)

    return (loss, grad_x, *[grads[n] for n in ORDER], *[delta[n] for n in ORDER], *[new_m[n] for n in ORDER], *[new_v[n] for n in ORDER])
```

```python
import math
from typing import NamedTuple

import jax
import jax.numpy as jnp
from jax import lax
from jax.experimental import pallas as pl
from jax.experimental.pallas import tpu as pltpu

F32 = jnp.float32
_MXU = jnp.bfloat16
_PAY = jnp.bfloat16
EPS = 1e-6
WINDOWS = (2, 4, 8, 16)
GROUP = 128
N_XHEADS = 4
HALO = 16
FF_TILE = 256
VMEM_LIMIT = 56 * 1024 * 1024
MESH = pl.DeviceIdType.MESH

ADAM_LR, ADAM_B1, ADAM_B2, ADAM_EPS, ADAM_WD, ADAM_STEP = 0.001, 0.9, 0.999, 1e-08, 0.01, 10

VM = pl.BlockSpec(memory_space=pltpu.VMEM)
HB = pl.BlockSpec(memory_space=pltpu.HBM)


def _nn(a, b):
    return jnp.dot(a, b, preferred_element_type=F32)


def _nt(a, b):
    return lax.dot_general(a, b, (((1,), (1,)), ((), ())), preferred_element_type=F32)


def _tn(a, b):
    return lax.dot_general(a, b, (((0,), (0,)), ((), ())), preferred_element_type=F32)


def _rms(x):
    r = lax.rsqrt(jnp.mean(x * x, axis=-1, keepdims=True) + EPS)
    return x * r, r


def _rms_bwd(dxn, xhat, r, g):
    dxh = dxn * g
    dx = r * (dxh - xhat * jnp.mean(dxh * xhat, axis=-1, keepdims=True))
    return dx, jnp.sum(dxn * xhat, axis=0, keepdims=True)


def _gelu(x):
    cdf = 0.5 * (1.0 + lax.erf(x * (2.0 ** -0.5)))
    return x * cdf, cdf


def _gelu_grad(x, cdf):
    return cdf + x * jnp.exp(-0.5 * x * x) * ((2.0 * math.pi) ** -0.5)


def _params(sem=None):
    return pltpu.CompilerParams(dimension_semantics=sem, vmem_limit_bytes=VMEM_LIMIT)


def _token_block(t, want):
    return want if t % want == 0 and t > want else GROUP


def _const_spec(shape):
    n = len(shape)
    return pl.BlockSpec(shape, lambda i: (0,) * n)


def _tril():
    return lax.broadcasted_iota(jnp.int32, (GROUP, GROUP), 0) >= lax.broadcasted_iota(jnp.int32, (GROUP, GROUP), 1)


def _shift_rows(x, k, edge):
    tb = x.shape[0]
    r8 = lax.broadcasted_iota(jnp.int32, (8, 1), 0)
    rolled = pltpu.roll(x, k % tb, 0)
    if k > 0:
        top = jnp.where(r8 < k, pltpu.roll(edge, k, 0), rolled[0:8, :])
        return jnp.concatenate([top, rolled[8:, :]], axis=0)
    bottom = jnp.where(r8 >= 8 + k, pltpu.roll(edge, 8 + k, 0), rolled[tb - 8:, :])
    return jnp.concatenate([rolled[:tb - 8, :], bottom], axis=0)


class _Hosted(NamedTuple):
    operands: tuple
    aliased: bool
    out_shapes: tuple
    sems: tuple
    stages: tuple


def _hosted_results(hosted):
    if hosted.aliased:
        return [jax.ShapeDtypeStruct(o.shape, o.dtype) for o in hosted.operands]
    return list(hosted.out_shapes)


def _call_hosting(main_body, hosted, *, name, steps, in_specs, out_specs, out_shape, scratch_shapes, operands):
    hosted = [hs for hs in (hosted or ()) if hs is not None]
    if not hosted:
        outs = pl.pallas_call(main_body, name=name, grid=(steps,), in_specs=in_specs, out_specs=out_specs, out_shape=out_shape,
                              scratch_shapes=scratch_shapes, compiler_params=_params(("arbitrary",)))(*operands)
        return outs, ()
    n_in, n_out, n_sc = len(in_specs), len(out_specs), len(scratch_shapes)
    shapes = [_hosted_results(hs) for hs in hosted]
    aliases, in_at, out_at = {}, n_in, n_out
    for hs, sh in zip(hosted, shapes):
        if hs.aliased:
            aliases.update({in_at + i: out_at + i for i in range(len(hs.operands))})
        in_at += len(hs.operands)
        out_at += len(sh)

    def body(*refs):
        at = [0]

        def take(n):
            at[0] += n
            return refs[at[0] - n:at[0]]

        ins = take(n_in)
        h_in = [take(len(hs.operands)) for hs in hosted]
        outs = take(n_out)
        h_out = [take(len(sh)) for sh in shapes]
        scratch = take(n_sc)
        h_sems = [take(len(hs.sems)) for hs in hosted]
        step = pl.program_id(0)

        def run(stage):
            for hs, a, b, c in zip(hosted, h_in, h_out, h_sems):
                if hs.stages[stage] is not None:
                    hs.stages[stage](a, b, c)

        @pl.when(step == 0)
        def _():
            run(0)

        if any(hs.stages[1] is not None for hs in hosted):
            @pl.when(step == (3 * steps) // 4)
            def _():
                run(1)

        main_body(*ins, *outs, *scratch)

        @pl.when(step == steps - 1)
        def _():
            run(2)

    flat = lambda lists: [x for xs in lists for x in xs]
    outs = pl.pallas_call(
        body, name=name, grid=(steps,), in_specs=list(in_specs) + [HB] * (in_at - n_in), out_specs=list(out_specs) + [HB] * (out_at - n_out),
        out_shape=list(out_shape) + flat(shapes), scratch_shapes=list(scratch_shapes) + flat(hs.sems for hs in hosted),
        input_output_aliases=aliases, compiler_params=_params(("arbitrary",)),
    )(*operands, *flat(hs.operands for hs in hosted))
    results, at = [], n_out
    for sh in shapes:
        results.append(outs[at:at + len(sh)])
        at += len(sh)
    return outs[:n_out], results


def _run_hosted(hosted, name):
    nh = len(hosted.operands)
    h_shapes = _hosted_results(hosted)

    def body(*refs):
        h_in, h_out, h_sems = refs[:nh], refs[nh:nh + len(h_shapes)], refs[nh + len(h_shapes):]
        for stage in hosted.stages:
            if stage is not None:
                stage(h_in, h_out, h_sems)

    return pl.pallas_call(
        body, name=name, in_specs=[HB] * nh, out_specs=[HB] * len(h_shapes), out_shape=h_shapes, scratch_shapes=list(hosted.sems),
        input_output_aliases={i: i for i in range(nh)} if hosted.aliased else {},
        compiler_params=pltpu.CompilerParams(has_side_effects=True),
    )(*hosted.operands)


def _window_sums(e, win, back):
    n = e.shape[0]
    k = 1
    while k < win:
        e = e + pltpu.roll(e, k if back else n - k, 0)
        k *= 2
    return e


def _pool_diff(prev, p, t0, gi, win):
    sl = slice(gi * GROUP, (gi + 1) * GROUP)
    tb = p.shape[0]
    s = _window_sums(jnp.concatenate([prev[:, sl], p[:, sl]], axis=0), win, True)[HALO:, :]
    tglob = t0 + lax.broadcasted_iota(jnp.int32, (tb, 1), 0)
    cnt = jnp.minimum(tglob + 1, win).astype(F32)
    return s / cnt - p[:, sl], cnt


def _layernorm(v):
    xc = v - jnp.mean(v, axis=-1, keepdims=True)
    rstd = lax.rsqrt(jnp.mean(xc * xc, axis=-1, keepdims=True) + EPS)
    return xc * rstd, rstd


def _mixer_fwd(h, g, w_in, pool_w, pool_scale, sgu_g, sgu_w, sgu_bt, w_out, hosted=None):
    t, d = h.shape
    pw = pool_w.shape[0] * GROUP
    sw = sgu_w.shape[0] * GROUP
    tb = _token_block(t, 512)

    def body(h_ref, g_ref, win_ref, pw_ref, ps_ref, sg_ref, sw_ref, sbt_ref, wout_ref, h1_ref, proj_ref, xn_ref, mix_ref, pext):
        i = pl.program_id(0)

        @pl.when(i == 0)
        def _():
            pext[...] = jnp.zeros((HALO, pw), F32)

        x = h_ref[...]
        xhat, _ = _rms(x)
        xn = (xhat * g_ref[...]).astype(_MXU)
        xn_ref[...] = xn
        proj = _nn(xn, win_ref[...])
        proj_ref[...] = proj
        p = proj[:, :pw]
        prev = pext[...]
        for gi, win in enumerate(WINDOWS):
            sl = slice(gi * GROUP, (gi + 1) * GROUP)
            dg, _ = _pool_diff(prev, p, i * tb, gi, win)
            e = _nn(dg.astype(_MXU), pw_ref[gi].astype(_MXU))
            mix_ref[:, sl] = (e * ps_ref[:, sl]).astype(_MXU)
        pext[...] = p[tb - HALO:tb, :]
        uv, _ = _gelu(proj[:, pw:])
        u = uv[:, :sw]
        vhat, _ = _layernorm(uv[:, sw:])
        vn = (vhat * sg_ref[...]).astype(_MXU)
        mask = _tril()
        for hh in range(sw // GROUP):
            wm = jnp.where(mask, sw_ref[hh], 0.0).astype(_MXU)
            for n in range(tb // GROUP):
                rows = slice(n * GROUP, (n + 1) * GROUP)
                cols = slice(hh * GROUP, (hh + 1) * GROUP)
                z = _nn(wm, vn[rows, cols]) + sbt_ref[hh]
                mix_ref[rows, pw + hh * GROUP:pw + (hh + 1) * GROUP] = (u[rows, cols] * z).astype(_MXU)
        h1_ref[...] = x + _nn(mix_ref[...], wout_ref[...])

    blk = lambda w: pl.BlockSpec((tb, w), lambda i: (i, 0))
    return _call_hosting(
        body, hosted, name="mixer_fwd", steps=t // tb,
        in_specs=[blk(d), VM, VM, VM, VM, VM, VM, VM, VM],
        out_specs=[blk(d), blk(w_in.shape[1]), blk(d), blk(d)],
        out_shape=[jax.ShapeDtypeStruct((t, d), F32), jax.ShapeDtypeStruct((t, w_in.shape[1]), F32),
                   jax.ShapeDtypeStruct((t, d), _MXU), jax.ShapeDtypeStruct((t, d), _MXU)],
        scratch_shapes=[pltpu.VMEM((HALO, pw), F32)],
        operands=(h, g, w_in, pool_w, pool_scale, sgu_g, sgu_w, sgu_bt, w_out))


def _mixer_bwd(dh1, h, proj, g, w_in, pool_w, pool_scale, sgu_g, sgu_w, sgu_bt, w_out, hosted=None):
    t, d = h.shape
    ng, nh = pool_w.shape[0], sgu_w.shape[0]
    pw, sw = ng * GROUP, nh * GROUP
    tb = _token_block(t, 256)
    nb = t // tb

    def body(dh1_ref, h_ref, proj_ref, halo_ref, g_ref, win_ref, pw_ref, ps_ref, sg_ref, sw_ref, sbt_ref, wout_ref,
             dh_ref, dproj_ref, gg_ref, gpw_ref, gps_ref, gsg_ref, gsw_ref, gsbt_ref, dext, duv):
        i = pl.program_id(0)
        blk = nb - 1 - i

        @pl.when(i == 0)
        def _():
            for r in (gg_ref, gpw_ref, gps_ref, gsg_ref, gsw_ref, gsbt_ref, dext):
                r[...] = jnp.zeros(r.shape, F32)

        dh1v = dh1_ref[...]
        dmix = _nt(dh1v.astype(_MXU), wout_ref[...])
        proj_v = proj_ref[...]
        p = proj_v[:, :pw]
        prev = jnp.where(blk == 0, 0.0, halo_ref[...])
        for gi, win in enumerate(WINDOWS):
            sl = slice(gi * GROUP, (gi + 1) * GROUP)
            dg, cnt = _pool_diff(prev, p, blk * tb, gi, win)
            dgm = dg.astype(_MXU)
            pwm = pw_ref[gi].astype(_MXU)
            e = _nn(dgm, pwm)
            dy = dmix[:, sl]
            gps_ref[:, sl] += jnp.sum(dy * e, axis=0, keepdims=True)
            de = (dy * ps_ref[:, sl]).astype(_MXU)
            gpw_ref[gi] += _tn(dgm, de)
            dd = _nt(de, pwm)
            ddc = dd / cnt
            acc = _window_sums(jnp.concatenate([ddc, dext[:, sl]], axis=0), win, False)[:tb, :]
            dext[:, sl] = ddc[0:HALO, :]
            dproj_ref[:, sl] = (acc - dd).astype(_MXU)
        pre = proj_v[:, pw:]
        uv, cdf = _gelu(pre)
        u = uv[:, :sw]
        vhat, rstd = _layernorm(uv[:, sw:])
        vn = (vhat * sg_ref[...]).astype(_MXU)
        mask = _tril()
        for hh in range(nh):
            wm = jnp.where(mask, sw_ref[hh], 0.0).astype(_MXU)
            cols = slice(hh * GROUP, (hh + 1) * GROUP)
            gw = jnp.zeros((GROUP, GROUP), F32)
            gb = jnp.zeros((GROUP, GROUP), F32)
            for n in range(tb // GROUP):
                rows = slice(n * GROUP, (n + 1) * GROUP)
                vs = vn[rows, cols]
                z = _nn(wm, vs) + sbt_ref[hh]
                dy = dmix[rows, pw + hh * GROUP:pw + (hh + 1) * GROUP]
                dz = dy * u[rows, cols]
                gb = gb + dz
                dzm = dz.astype(_MXU)
                gw = gw + _nt(dzm, vs)
                duv[rows, cols] = dy * z
                duv[rows, sw + hh * GROUP:sw + (hh + 1) * GROUP] = _tn(wm, dzm)
            gsw_ref[hh] += jnp.where(mask, gw, 0.0)
            gsbt_ref[hh] += gb
        dvn = duv[:, sw:]
        gsg_ref[...] += jnp.sum(dvn * vhat, axis=0, keepdims=True)
        dxh = dvn * sg_ref[...]
        dv = rstd * (dxh - jnp.mean(dxh, axis=-1, keepdims=True) - vhat * jnp.mean(dxh * vhat, axis=-1, keepdims=True))
        gp = _gelu_grad(pre, cdf)
        dproj_ref[:, pw:pw + sw] = (duv[:, :sw] * gp[:, :sw]).astype(_MXU)
        dproj_ref[:, pw + sw:] = (dv * gp[:, sw:]).astype(_MXU)
        dxn = _nt(dproj_ref[...], win_ref[...])
        xhat, r = _rms(h_ref[...])
        dx, gg = _rms_bwd(dxn, xhat, r, g_ref[...])
        gg_ref[...] += gg
        dh_ref[...] = dh1v + dx

    rev = lambda w: pl.BlockSpec((tb, w), lambda i: (nb - 1 - i, 0))
    halo = pl.BlockSpec((HALO, pw), lambda i: (jnp.maximum((nb - 1 - i) * (tb // HALO) - 1, 0), 0))
    small = [(1, d), (ng, GROUP, GROUP), (1, pw), (1, sw), (nh, GROUP, GROUP), (nh, GROUP, GROUP)]
    return _call_hosting(
        body, hosted, name="mixer_bwd", steps=nb,
        in_specs=[rev(d), rev(d), rev(proj.shape[1]), halo, VM, VM, VM, VM, VM, VM, VM, VM],
        out_specs=[rev(d), rev(proj.shape[1])] + [_const_spec(s) for s in small],
        out_shape=[jax.ShapeDtypeStruct((t, d), F32), jax.ShapeDtypeStruct(proj.shape, _MXU)]
        + [jax.ShapeDtypeStruct(s, F32) for s in small],
        scratch_shapes=[pltpu.VMEM((HALO, pw), F32), pltpu.VMEM((tb, 2 * sw), F32)],
        operands=(dh1, h, proj, proj, g, w_in, pool_w, pool_scale, sgu_g, sgu_w, sgu_bt, w_out))


def _kv_fwd(mem, gm, wk, wv):
    n, d = mem.shape

    def body(mem_ref, gm_ref, wk_ref, wv_ref, k_ref, v_ref, memn_ref):
        xhat, _ = _rms(mem_ref[...])
        memn = (xhat * gm_ref[...]).astype(_MXU)
        memn_ref[...] = memn
        k_ref[...] = _nn(memn, wk_ref[...]).astype(_MXU)
        v_ref[...] = _nn(memn, wv_ref[...]).astype(_MXU)

    return pl.pallas_call(
        body, name="kv_fwd", in_specs=[VM] * 4, out_specs=[VM] * 3,
        out_shape=[jax.ShapeDtypeStruct((n, d), _MXU)] * 3, compiler_params=_params(),
    )(mem, gm, wk, wv)


def _kv_bwd(dk, dv, mem, wk, wv):
    n, d = mem.shape

    def body(dk_ref, dv_ref, mem_ref, wk_ref, wv_ref, ggm_ref):
        dmemn = _nt(dk_ref[...].astype(_MXU), wk_ref[...]) + _nt(dv_ref[...].astype(_MXU), wv_ref[...])
        xhat, _ = _rms(mem_ref[...])
        ggm_ref[...] = jnp.sum(dmemn * xhat, axis=0, keepdims=True)

    return pl.pallas_call(
        body, name="kv_bwd", in_specs=[VM] * 5, out_specs=VM,
        out_shape=jax.ShapeDtypeStruct((1, d), F32), compiler_params=_params(),
    )(dk, dv, mem, wk, wv)


def _softmax_rows(qm, k_ref, sl, scale):
    s = _nt(qm, k_ref[:, sl]) * scale
    e = jnp.exp(s - jnp.max(s, axis=-1, keepdims=True))
    return e / jnp.sum(e, axis=-1, keepdims=True)


def _xattn_fwd(h, g, wq, k, v, wo, hosted=None):
    t, d = h.shape
    hd = d // N_XHEADS
    scale = hd ** -0.5
    tb = _token_block(t, 512)

    def body(h_ref, g_ref, wq_ref, k_ref, v_ref, wo_ref, h2_ref, q_ref, o_ref, xn_ref):
        x = h_ref[...]
        xhat, _ = _rms(x)
        xn = (xhat * g_ref[...]).astype(_MXU)
        xn_ref[...] = xn
        qm = _nn(xn, wq_ref[...]).astype(_MXU)
        q_ref[...] = qm
        for a in range(N_XHEADS):
            sl = slice(a * hd, (a + 1) * hd)
            pr = _softmax_rows(qm[:, sl], k_ref, sl, scale)
            o_ref[:, sl] = _nn(pr.astype(_MXU), v_ref[:, sl]).astype(_MXU)
        h2_ref[...] = x + _nn(o_ref[...], wo_ref[...])

    blk = pl.BlockSpec((tb, d), lambda i: (i, 0))
    return _call_hosting(
        body, hosted, name="xattn_fwd", steps=t // tb,
        in_specs=[blk, VM, VM, VM, VM, VM], out_specs=[blk] * 4,
        out_shape=[jax.ShapeDtypeStruct((t, d), F32)] + [jax.ShapeDtypeStruct((t, d), _MXU)] * 3,
        scratch_shapes=[], operands=(h, g, wq, k, v, wo))


def _xattn_bwd(dh2, h, q, g, wq, k, v, wo, hosted=None):
    t, d = h.shape
    n = k.shape[0]
    hd = d // N_XHEADS
    scale = hd ** -0.5
    tb = _token_block(t, 512)

    def body(dh2_ref, h_ref, q_ref, g_ref, wq_ref, k_ref, v_ref, wo_ref, dh_ref, dq_ref, dk_ref, dv_ref, gg_ref):
        @pl.when(pl.program_id(0) == 0)
        def _():
            for r in (dk_ref, dv_ref, gg_ref):
                r[...] = jnp.zeros(r.shape, F32)

        dh2v = dh2_ref[...]
        dom = _nt(dh2v.astype(_MXU), wo_ref[...]).astype(_MXU)
        for a in range(N_XHEADS):
            sl = slice(a * hd, (a + 1) * hd)
            qh = q_ref[:, sl]
            pr = _softmax_rows(qh, k_ref, sl, scale)
            dv_ref[:, sl] += _tn(pr.astype(_MXU), dom[:, sl])
            dpr = _nt(dom[:, sl], v_ref[:, sl])
            ds = (pr * (dpr - jnp.sum(dpr * pr, axis=-1, keepdims=True)) * scale).astype(_MXU)
            dq_ref[:, sl] = _nn(ds, k_ref[:, sl]).astype(_MXU)
            dk_ref[:, sl] += _tn(ds, qh)
        dxn = _nt(dq_ref[...], wq_ref[...])
        xhat, r = _rms(h_ref[...])
        dx, gg = _rms_bwd(dxn, xhat, r, g_ref[...])
        gg_ref[...] += gg
        dh_ref[...] = dh2v + dx

    blk = pl.BlockSpec((tb, d), lambda i: (i, 0))
    return _call_hosting(
        body, hosted, name="xattn_bwd", steps=t // tb,
        in_specs=[blk, blk, blk, VM, VM, VM, VM, VM],
        out_specs=[blk, blk, _const_spec((n, d)), _const_spec((n, d)), _const_spec((1, d))],
        out_shape=[jax.ShapeDtypeStruct((t, d), F32), jax.ShapeDtypeStruct((t, d), _MXU),
                   jax.ShapeDtypeStruct((n, d), F32), jax.ShapeDtypeStruct((n, d), F32), jax.ShapeDtypeStruct((1, d), F32)],
        scratch_shapes=[], operands=(dh2, h, q, g, wq, k, v, wo))


def _ffn_fwd(h, g, w_up, conv_w, conv_b, w_down, hosted=None):
    t, d = h.shape
    f = w_down.shape[0]
    ft = FF_TILE
    tb = _token_block(t, 512)

    def body(h_ref, g_ref, wup_ref, cw_ref, cb_ref, wdown_ref, h3_ref, hh_ref, xn_ref, ext, carry):
        @pl.when(pl.program_id(0) == 0)
        def _():
            carry[...] = jnp.zeros(carry.shape, F32)

        x = h_ref[...]
        xhat, _ = _rms(x)
        xn = (xhat * g_ref[...]).astype(_MXU)
        xn_ref[...] = xn
        acc = jnp.zeros((tb, d), F32)
        for j in range(f // ft):
            hc = []
            for part, off in enumerate((j * ft, f + j * ft)):
                cols = slice(off, off + ft)
                cur = _nn(xn, wup_ref[:, cols])
                hh_ref[:, cols] = cur.astype(_MXU)
                ext[part, 0:8, :] = carry[:, cols]
                ext[part, 8:8 + tb, :] = cur
                carry[:, cols] = cur[tb - 8:tb, :]
                hc.append(cb_ref[:, cols] + cw_ref[0:1, cols] * ext[part, 6:6 + tb, :]
                          + cw_ref[1:2, cols] * ext[part, 7:7 + tb, :] + cw_ref[2:3, cols] * cur)
            act = (hc[0] * jax.nn.sigmoid(hc[0]) * hc[1]).astype(_MXU)
            acc = acc + _nn(act, wdown_ref[j * ft:(j + 1) * ft, :])
        h3_ref[...] = x + acc

    blk = lambda w: pl.BlockSpec((tb, w), lambda i: (i, 0))
    return _call_hosting(
        body, hosted, name="ffn_fwd", steps=t // tb,
        in_specs=[blk(d), VM, VM, VM, VM, VM], out_specs=[blk(d), blk(2 * f), blk(d)],
        out_shape=[jax.ShapeDtypeStruct((t, d), F32), jax.ShapeDtypeStruct((t, 2 * f), _MXU), jax.ShapeDtypeStruct((t, d), _MXU)],
        scratch_shapes=[pltpu.VMEM((2, 8 + tb, ft), F32), pltpu.VMEM((8, 2 * f), F32)],
        operands=(h, g, w_up, conv_w, conv_b, w_down))


def _ffn_bwd(dh3, h, hh, g, w_up, conv_w, conv_b, w_down, hosted=None):
    t, d = h.shape
    f = w_down.shape[0]
    ft = FF_TILE
    tb = _token_block(t, 256)
    nb = t // tb

    def body(dh3_ref, h_ref, hh_ref, halo_ref, g_ref, wup_ref, cw_ref, cb_ref, wdown_ref,
             dh_ref, dhh_ref, act_ref, gcw_ref, gcb_ref, gg_ref, dcarry):
        i = pl.program_id(0)
        blk = nb - 1 - i

        @pl.when(i == 0)
        def _():
            for r in (gcw_ref, gcb_ref, gg_ref, dcarry):
                r[...] = jnp.zeros(r.shape, F32)

        dh3v = dh3_ref[...]
        dhm = dh3v.astype(_MXU)
        dxn = jnp.zeros((tb, d), F32)
        for j in range(f // ft):
            cur, back1, back2, hc = [], [], [], []
            for part, off in enumerate((j * ft, f + j * ft)):
                cols = slice(off, off + ft)
                c0 = hh_ref[:, cols].astype(F32)
                before = jnp.where(blk == 0, 0.0, halo_ref[:, cols].astype(F32))[HALO - 8:HALO, :]
                b1 = _shift_rows(c0, 1, before)
                b2 = _shift_rows(c0, 2, before)
                cur.append(c0)
                back1.append(b1)
                back2.append(b2)
                hc.append(cb_ref[:, cols] + cw_ref[0:1, cols] * b2 + cw_ref[1:2, cols] * b1 + cw_ref[2:3, cols] * c0)
            sg = jax.nn.sigmoid(hc[0])
            silu = hc[0] * sg
            act_ref[:, j * ft:(j + 1) * ft] = (silu * hc[1]).astype(_MXU)
            dact = _nt(dhm, wdown_ref[j * ft:(j + 1) * ft, :])
            dhc = (dact * hc[1] * sg * (1.0 + hc[0] * (1.0 - sg)), dact * silu)
            for part, off in enumerate((j * ft, f + j * ft)):
                cols = slice(off, off + ft)
                dc = dhc[part]
                gcb_ref[:, cols] += jnp.sum(dc, axis=0, keepdims=True)
                gcw_ref[0:1, cols] += jnp.sum(dc * back2[part], axis=0, keepdims=True)
                gcw_ref[1:2, cols] += jnp.sum(dc * back1[part], axis=0, keepdims=True)
                gcw_ref[2:3, cols] += jnp.sum(dc * cur[part], axis=0, keepdims=True)
                after = dcarry[:, cols]
                dhh = (cw_ref[2:3, cols] * dc + cw_ref[1:2, cols] * _shift_rows(dc, -1, after)
                       + cw_ref[0:1, cols] * _shift_rows(dc, -2, after)).astype(_MXU)
                dcarry[:, cols] = dc[0:8, :]
                dhh_ref[:, cols] = dhh
                dxn = dxn + _nt(dhh, wup_ref[:, cols])
        xhat, r = _rms(h_ref[...])
        dx, gg = _rms_bwd(dxn, xhat, r, g_ref[...])
        gg_ref[...] += gg
        dh_ref[...] = dh3v + dx

    rev = lambda w: pl.BlockSpec((tb, w), lambda i: (nb - 1 - i, 0))
    halo = pl.BlockSpec((HALO, 2 * f), lambda i: (jnp.maximum((nb - 1 - i) * (tb // HALO) - 1, 0), 0))
    return _call_hosting(
        body, hosted, name="ffn_bwd", steps=nb,
        in_specs=[rev(d), rev(d), rev(2 * f), halo, VM, VM, VM, VM, VM],
        out_specs=[rev(d), rev(2 * f), rev(f), _const_spec((3, 2 * f)), _const_spec((1, 2 * f)), _const_spec((1, d))],
        out_shape=[jax.ShapeDtypeStruct((t, d), F32), jax.ShapeDtypeStruct((t, 2 * f), _MXU), jax.ShapeDtypeStruct((t, f), _MXU),
                   jax.ShapeDtypeStruct((3, 2 * f), F32), jax.ShapeDtypeStruct((1, 2 * f), F32), jax.ShapeDtypeStruct((1, d), F32)],
        scratch_shapes=[pltpu.VMEM((8, 2 * f), F32)],
        operands=(dh3, h, hh, hh, g, w_up, conv_w, conv_b, w_down))


def _loss_head(h, g, target):
    t, d = h.shape
    tb = _token_block(t, 512)

    def body(h_ref, g_ref, tgt_ref, dh_ref, loss_ref, gg_ref):
        @pl.when(pl.program_id(0) == 0)
        def _():
            loss_ref[...] = jnp.zeros(loss_ref.shape, F32)
            gg_ref[...] = jnp.zeros(gg_ref.shape, F32)

        xhat, r = _rms(h_ref[...])
        err = xhat * g_ref[...] - tgt_ref[...]
        loss_ref[...] += 0.5 * jnp.sum(jnp.sum(err * err, axis=-1, keepdims=True), axis=0, keepdims=True) / d
        dx, gg = _rms_bwd(err / d, xhat, r, g_ref[...])
        gg_ref[...] += gg
        dh_ref[...] = dx

    blk = pl.BlockSpec((tb, d), lambda i: (i, 0))
    return pl.pallas_call(
        body, name="loss_head", grid=(t // tb,),
        in_specs=[blk, VM, blk], out_specs=[blk, _const_spec((1, 1)), _const_spec((1, d))],
        out_shape=[jax.ShapeDtypeStruct((t, d), F32), jax.ShapeDtypeStruct((1, 1), F32), jax.ShapeDtypeStruct((1, d), F32)],
        compiler_params=_params(("arbitrary",)),
    )(h, g, target)


def _largest_tile(n, cap, mult=128):
    best = None
    for c in range(mult, min(n, cap) + 1, mult):
        if n % c == 0:
            best = c
    return best if best is not None else n


def _grad_matmul(a, b, name, layer, n_layers, into=None):
    t, m = a.shape
    n = b.shape[1]
    tm, tn, tk = _largest_tile(m, 1408), _largest_tile(n, 1024), _largest_tile(t, 1024)
    nk = t // tk

    def body(a_ref, b_ref, *rest):
        o_ref = rest[-1]

        @pl.when(pl.program_id(2) == 0)
        def _():
            o_ref[...] = jnp.zeros(o_ref.shape, F32)

        o_ref[...] += _tn(a_ref[...].astype(_MXU), b_ref[...].astype(_MXU))

    in_specs = [pl.BlockSpec((tk, tm), lambda i, j, k: (k, i)), pl.BlockSpec((tk, tn), lambda i, j, k: (k, j))]
    operands = (a, b)
    aliases = {}
    if into is not None:
        in_specs.append(pl.BlockSpec(memory_space=pl.ANY))
        operands = (a, b, into)
        aliases = {2: 0}
    return pl.pallas_call(
        body, name=name, grid=(m // tm, n // tn, nk), in_specs=in_specs,
        out_specs=pl.BlockSpec((None, tm, tn), lambda i, j, k: (layer, i, j)),
        out_shape=jax.ShapeDtypeStruct((n_layers, m, n), F32), input_output_aliases=aliases,
        compiler_params=_params(("parallel", "parallel", "arbitrary")),
    )(*operands)


def _adamw_math(w, g, m, v):
    m = ADAM_B1 * m + (1.0 - ADAM_B1) * g
    v = ADAM_B2 * v + (1.0 - ADAM_B2) * (g * g)
    m_hat = m / (1.0 - ADAM_B1 ** ADAM_STEP)
    v_hat = v / (1.0 - ADAM_B2 ** ADAM_STEP)
    return -ADAM_LR * (m_hat / (jnp.sqrt(v_hat) + ADAM_EPS) + ADAM_WD * w), m, v


def _row_block(rows, cols, max_bytes=1 << 20, mult=16):
    best = None
    for r in range(mult, rows + 1, mult):
        if rows % r == 0 and r * cols * 4 <= max_bytes:
            best = r
    return best if best is not None else rows


def _adamw_big(w, g, m, v, name):
    shape = w.shape
    cols = shape[-1]
    flat = lambda a: a.reshape(-1, cols)
    rows = flat(w).shape[0]
    rb = _row_block(rows, cols)

    def body(w_ref, g_ref, m_ref, v_ref, d_ref, nm_ref, nv_ref):
        d_ref[...], nm_ref[...], nv_ref[...] = _adamw_math(w_ref[...], g_ref[...], m_ref[...], v_ref[...])

    blk = pl.BlockSpec((rb, cols), lambda i: (i, 0))
    outs = pl.pallas_call(
        body, name=name, grid=(rows // rb,), in_specs=[blk] * 4, out_specs=[blk] * 3,
        out_shape=[jax.ShapeDtypeStruct((rows, cols), F32)] * 3, compiler_params=_params(("parallel",)),
    )(flat(w), flat(g), flat(m), flat(v))
    return [o.reshape(shape) for o in outs]


def _adamw_small(ws, gs, ms, vs):
    n = len(ws)

    def body(*refs):
        for a in range(n):
            w_ref, g_ref, m_ref, v_ref = (refs[s * n + a] for s in range(4))
            d_ref, nm_ref, nv_ref = (refs[(4 + s) * n + a] for s in range(3))
            d_ref[...], nm_ref[...], nv_ref[...] = _adamw_math(w_ref[...], g_ref[...], m_ref[...], v_ref[...])

    outs = pl.pallas_call(
        body, name="adamw_small", in_specs=[VM] * (4 * n), out_specs=[VM] * (3 * n),
        out_shape=[jax.ShapeDtypeStruct(w.shape, F32) for w in ws] * 3, compiler_params=_params(),
    )(*ws, *gs, *ms, *vs)
    return outs[:n], outs[n:2 * n], outs[2 * n:]


def _place():
    x, y, c = lax.axis_index("x"), lax.axis_index("y"), lax.axis_index("c")
    chips = [(1 - x, y), (x, 1 - y), (1 - x, 1 - y)]
    return x, y, c, chips


def _rows(start, size, mult=16):
    return pl.ds(pl.multiple_of(start, mult), size)


def _full_window(ref, axis, chip, half=None):
    r, c = ref.shape
    if axis == 0:
        rs = r // 4
        if half is None:
            return ref.at[_rows(chip * rs, rs), :]
        return ref.at[_rows(chip * rs + half * (rs // 2), rs // 2), :]
    cs = c // 4
    if half is None:
        return ref.at[:, _rows(chip * cs, cs, 128)]
    return ref.at[_rows(half * (r // 2), r // 2), _rows(chip * cs, cs, 128)]


def _remote(src, dst, send_sem, recv_sem, to):
    return pltpu.make_async_remote_copy(src_ref=src, dst_ref=dst, send_sem=send_sem, recv_sem=recv_sem,
                                        device_id=to, device_id_type=MESH)


def _scalars(*vals):
    return jnp.stack([jnp.asarray(v, jnp.int32) for v in vals])


def _cast_place(shard, layer, axis, chip, name):
    _, rs, cs = shard.shape
    full = (rs * 4, cs) if axis == 0 else (rs, cs * 4)
    rb = _row_block(rs, cs)
    nrb = rs // rb

    def body(chip_ref, s_ref, o_ref):
        o_ref[...] = s_ref[...].astype(_PAY)

    if axis == 0:
        out_map = lambda i, chip_ref: (chip_ref[0] * nrb + i, 0)
    else:
        out_map = lambda i, chip_ref: (i, chip_ref[0])
    return pl.pallas_call(
        body, name=name,
        grid_spec=pltpu.PrefetchScalarGridSpec(
            num_scalar_prefetch=1, grid=(nrb,),
            in_specs=[pl.BlockSpec((None, rb, cs), lambda i, chip_ref: (layer, i, 0))],
            out_specs=pl.BlockSpec((rb, cs), out_map)),
        out_shape=jax.ShapeDtypeStruct(full, _PAY), compiler_params=_params(("parallel",)),
    )(_scalars(chip), shard)


def _hosted_allgather(placed, axes):
    n = len(placed)

    def each(outs, half_of):
        x, y, c, chips = _place()
        for i in range(n):
            for k, chip in enumerate(chips):
                yield i * 3 + k, (*chip, c), (x, y, 1 - c), _full_window(outs[i], axes[i], 2 * x + y, c), \
                    _full_window(outs[i], axes[i], 2 * chip[0] + chip[1], half_of(c))

    def start(_, outs, sems):
        send, recv, _, _ = sems
        for s, peer, _, mine, _ in each(outs, lambda c: c):
            _remote(mine, mine, send.at[s], recv.at[s], peer).start()

    def middle(_, outs, sems):
        send, recv, fsend, frecv = sems
        for s, _, sibling, _, got in each(outs, lambda c: c):
            _remote(got, got, send.at[s], recv.at[s], sibling).wait_recv()
            _remote(got, got, fsend.at[s], frecv.at[s], sibling).start()

    def finish(_, outs, sems):
        send, recv, fsend, frecv = sems
        for s, _, sibling, _, got in each(outs, lambda c: 1 - c):
            _remote(got, got, fsend.at[s], frecv.at[s], sibling).wait_recv()
        for s, peer, sibling, mine, got in each(outs, lambda c: c):
            _remote(mine, mine, send.at[s], recv.at[s], peer).wait_send()
            _remote(got, got, fsend.at[s], frecv.at[s], sibling).wait_send()

    return _Hosted(tuple(placed), True, (), (pltpu.SemaphoreType.DMA((n * 3,)),) * 4, (start, middle, finish))


def _allgather_conv(conv_shard):
    nl, taps, cs = conv_shard.shape

    def body(in_ref, out_ref, send, recv, local):
        x, y, c, chips = _place()
        mine = out_ref.at[:, :, _rows((2 * x + y) * cs, cs, 128)]
        own = pltpu.make_async_copy(in_ref, mine, local)
        own.start()
        sends = [_remote(in_ref, mine, send.at[k], recv.at[k], (*chip, c)) for k, chip in enumerate(chips)]
        for cp in sends:
            cp.start()
        for k, chip in enumerate(chips):
            got = out_ref.at[:, :, _rows((2 * chip[0] + chip[1]) * cs, cs, 128)]
            _remote(got, got, send.at[k], recv.at[k], (*chip, c)).wait_recv()
        for cp in sends:
            cp.wait_send()
        own.wait()

    return pl.pallas_call(
        body, name="allgather_conv", in_specs=[HB], out_specs=HB, out_shape=jax.ShapeDtypeStruct((nl, taps, cs * 4), conv_shard.dtype),
        scratch_shapes=[pltpu.SemaphoreType.DMA((3,)), pltpu.SemaphoreType.DMA((3,)), pltpu.SemaphoreType.DMA],
        compiler_params=pltpu.CompilerParams(has_side_effects=True),
    )(conv_shard)


def _hosted_exchange(grads, axes, layer):
    na = len(grads)
    views = [g.reshape(g.shape[0], 4, 2, g.shape[1] // 8, g.shape[2]) if ax == 0 else g for g, ax in zip(grads, axes)]

    def region(ref, axis, half):
        if axis == 0:
            return ref.at[layer, :, half]
        r = ref.shape[1]
        return ref.at[layer, _rows(half * (r // 2), r // 2), :]

    def copies(ins, land, sems):
        send, recv = sems
        x, y, c, _ = _place()
        return [_remote(region(ins[a], axes[a], 1 - c), land[a], send.at[a], recv.at[a], (x, y, 1 - c)) for a in range(na)]

    def start(ins, land, sems):
        for cp in copies(ins, land, sems):
            cp.start()

    def finish(ins, land, sems):
        for cp in copies(ins, land, sems):
            cp.wait()

    shapes = [(4, g.shape[1] // 8, g.shape[2]) if ax == 0 else (g.shape[1] // 2, g.shape[2]) for g, ax in zip(grads, axes)]
    return _Hosted(tuple(views), False, tuple(jax.ShapeDtypeStruct(s, F32) for s in shapes),
                   (pltpu.SemaphoreType.DMA((na,)),) * 2, (start, None, finish))


def _add_cast(mine, theirs, core, base, name):
    na, nb, cols = theirs.shape
    rb = _row_block(nb, cols)

    def body(core_ref, a_ref, b_ref, o_ref):
        o_ref[...] = (a_ref[...] + b_ref[...]).astype(_PAY)

    blk = pl.BlockSpec((None, rb, cols), lambda i, k, core_ref: (i, k, 0))
    return pl.pallas_call(
        body, name=name,
        grid_spec=pltpu.PrefetchScalarGridSpec(
            num_scalar_prefetch=1, grid=(na, nb // rb),
            in_specs=[pl.BlockSpec((None, None, rb, cols), lambda i, k, core_ref: (base + i, core_ref[0], k, 0)), blk], out_specs=blk),
        out_shape=jax.ShapeDtypeStruct((na, nb, cols), _PAY), compiler_params=_params(("parallel", "parallel")),
    )(_scalars(core), mine, theirs)


def _piece(ref, axis, chip):
    if axis == 0:
        return ref.at[chip]
    cs = ref.shape[1] // 4
    return ref.at[:, _rows(chip * cs, cs, 128)]


def _hosted_scatter(sums, axes):
    na = len(sums)

    def piece_shape(a):
        if axes[a] == 0:
            return (sums[a].shape[1], sums[a].shape[2])
        return (sums[a].shape[0], sums[a].shape[1] // 4)

    def copies(ins, slots, sems):
        send, recv = sems
        _, _, c, chips = _place()
        return [_remote(_piece(ins[a], axes[a], 2 * chip[0] + chip[1]), slots[a].at[k], send.at[a * 3 + k], recv.at[a * 3 + k], (*chip, c))
                for a in range(na) for k, chip in enumerate(chips)]

    def start(ins, slots, sems):
        for cp in copies(ins, slots, sems):
            cp.start()

    def finish(ins, slots, sems):
        for cp in copies(ins, slots, sems):
            cp.wait()

    return _Hosted(tuple(sums), False, tuple(jax.ShapeDtypeStruct((3,) + piece_shape(a), sums[a].dtype) for a in range(na)),
                   (pltpu.SemaphoreType.DMA((na * 3,)),) * 2, (start, None, finish))


def _sum_slots(sums, slots, axis, chip, core, layer, n_layers, name, into=None):
    _, hr, cs = slots.shape
    rb = _row_block(hr, cs)

    def body(at_ref, own_ref, s_ref, *rest):
        rest[-1][...] = ((own_ref[...].astype(F32) + s_ref[0].astype(F32)) + s_ref[1].astype(F32)) + s_ref[2].astype(F32)

    if axis == 0:
        own = pl.BlockSpec((None, rb, cs), lambda k, at_ref: (at_ref[0], k, 0))
    else:
        own = pl.BlockSpec((rb, cs), lambda k, at_ref: (k, at_ref[0]))
    in_specs = [own, pl.BlockSpec((3, rb, cs), lambda k, at_ref: (0, k, 0))]
    operands = (sums, slots)
    aliases = {}
    if into is not None:
        in_specs.append(pl.BlockSpec(memory_space=pl.ANY))
        operands = (sums, slots, into)
        aliases = {3: 0}
    return pl.pallas_call(
        body, name=name,
        grid_spec=pltpu.PrefetchScalarGridSpec(
            num_scalar_prefetch=1, grid=(hr // rb,), in_specs=in_specs,
            out_specs=pl.BlockSpec((None, None, rb, cs), lambda k, at_ref: (layer, at_ref[1], k, 0))),
        out_shape=jax.ShapeDtypeStruct((n_layers, 2, hr, cs), F32), input_output_aliases=aliases,
        compiler_params=_params(("parallel",)),
    )(_scalars(chip, core), *operands)


def _sibling_assemble(shards):
    na = len(shards)

    def body(*refs):
        outs = refs[na:2 * na]
        send, recv = refs[2 * na:]
        x, y, c, _ = _place()
        copies = []
        for a in range(na):
            hr = outs[a].shape[1] // 2
            mine = outs[a].at[:, _rows(c * hr, hr), :]
            cp = _remote(mine, mine, send.at[a], recv.at[a], (x, y, 1 - c))
            cp.start()
            copies.append(cp)
        for cp in copies:
            cp.wait()

    return pl.pallas_call(
        body, name="grad_sibling_assemble", in_specs=[HB] * na, out_specs=[HB] * na,
        out_shape=[jax.ShapeDtypeStruct(s.shape, F32) for s in shards], input_output_aliases={a: a for a in range(na)},
        scratch_shapes=[pltpu.SemaphoreType.DMA((na,))] * 2,
        compiler_params=pltpu.CompilerParams(has_side_effects=True),
    )(*shards)


def _allreduce_small(buf):
    rows, w = buf.shape
    half = rows // 2

    def body(buf_ref, out_ref, land, slots, red, sems_send, sems_recv):
        x, y, c, chips = _place()
        me = 2 * x + y
        sibling = (x, y, 1 - c)
        first = _remote(buf_ref, land, sems_send.at[0], sems_recv.at[0], sibling)
        first.start()
        first.wait()
        mine = pl.ds(pl.multiple_of(c * half, 8), half)
        slots[me] = buf_ref[mine, :] + land[mine, :]
        sends = []
        for k, chip in enumerate(chips):
            cp = _remote(slots.at[me], slots.at[me], sems_send.at[1 + k], sems_recv.at[1 + k], (*chip, c))
            cp.start()
            sends.append(cp)
        for k, chip in enumerate(chips):
            got = slots.at[2 * chip[0] + chip[1]]
            _remote(got, got, sems_send.at[1 + k], sems_recv.at[1 + k], sibling).wait_recv()
        red[...] = ((slots[0] + slots[1]) + slots[2]) + slots[3]
        out_ref[mine, :] = red[...]
        last = _remote(red, out_ref.at[mine, :], sems_send.at[4], sems_recv.at[4], sibling)
        last.start()
        theirs = out_ref.at[pl.ds(pl.multiple_of((1 - c) * half, 8), half), :]
        _remote(red, theirs, sems_send.at[4], sems_recv.at[4], sibling).wait_recv()
        for cp in sends:
            cp.wait_send()
        last.wait_send()

    return pl.pallas_call(
        body, name="allreduce_small", in_specs=[VM], out_specs=VM, out_shape=jax.ShapeDtypeStruct((rows, w), F32),
        scratch_shapes=[pltpu.VMEM((rows, w), F32), pltpu.VMEM((4, half, w), F32), pltpu.VMEM((half, w), F32),
                        pltpu.SemaphoreType.DMA((5,)), pltpu.SemaphoreType.DMA((5,))],
        compiler_params=pltpu.CompilerParams(has_side_effects=True, vmem_limit_bytes=VMEM_LIMIT),
    )(buf)


BIG = ("w_in", "w_out", "wq", "wk", "wv", "wo", "w_up", "w_down")
EARLY = ("w_in", "w_out", "wq", "wk", "wv", "wo")
LATE = ("w_up", "w_down")
BIG_AXIS = {"w_in": 1, "w_out": 0, "wq": 0, "wk": 0, "wv": 0, "wo": 0, "w_up": 1, "w_down": 0}
SMALL = ("norm_mix_g", "pool_w", "pool_scale", "sgu_g", "sgu_w", "sgu_b", "norm_xattn_g", "mem_norm_g", "norm_ffn_g",
         "conv_w", "conv_b", "final_norm_g")
ORDER = ("norm_mix_g", "w_in", "pool_w", "pool_scale", "sgu_g", "sgu_w", "sgu_b", "w_out", "norm_xattn_g", "mem_norm_g",
         "wq", "wk", "wv", "wo", "norm_ffn_g", "w_up", "conv_w", "conv_b", "w_down", "final_norm_g")
PACK_WIDTH = 512


def kernel(x, mem, norm_mix_g, w_in, pool_w, pool_scale, sgu_g, sgu_w, sgu_b, w_out, norm_xattn_g, mem_norm_g, wq, wk, wv, wo, norm_ffn_g, w_up, conv_w, conv_b, w_down, final_norm_g, loss_target, m_norm_mix_g, m_w_in, m_pool_w, m_pool_scale, m_sgu_g, m_sgu_w, m_sgu_b, m_w_out, m_norm_xattn_g, m_mem_norm_g, m_wq, m_wk, m_wv, m_wo, m_norm_ffn_g, m_w_up, m_conv_w, m_conv_b, m_w_down, m_final_norm_g, v_norm_mix_g, v_w_in, v_pool_w, v_pool_scale, v_sgu_g, v_sgu_w, v_sgu_b, v_w_out, v_norm_xattn_g, v_mem_norm_g, v_wq, v_wk, v_wv, v_wo, v_norm_ffn_g, v_w_up, v_conv_w, v_conv_b, v_w_down, v_final_norm_g):
    given = dict(locals())
    w = {n: given[n] for n in ORDER}
    mom = {n: given["m_" + n] for n in ORDER}
    var = {n: given["v_" + n] for n in ORDER}
    nl = w_in.shape[0]
    xs, mems, tgt = x[0], mem[0], loss_target[0]
    chip = 2 * lax.axis_index("x") + lax.axis_index("y")
    core = lax.axis_index("c")

    axes_of = lambda names: [BIG_AXIS[n] for n in names]
    placed = [{n: _cast_place(w[n], l, BIG_AXIS[n], chip, f"place_{n}_{l}") for n in BIG} for l in range(nl)]
    conv_full = _allgather_conv(conv_w)

    def gather(names, l):
        return _hosted_allgather([placed[l][n] for n in names], axes_of(names))

    full = [dict(zip(EARLY, _run_hosted(gather(EARLY, 0), "allgather_weights")))]

    row = lambda a, l: a[l][None, :]
    saved = []
    h = xs
    for l in range(nl):
        fw = full[l]
        sbt = jnp.broadcast_to(sgu_b[l][:, :, None], sgu_w[l].shape)
        (h1, proj, xn1, mix), got = _mixer_fwd(h, row(norm_mix_g, l), fw["w_in"], pool_w[l], row(pool_scale, l), row(sgu_g, l), sgu_w[l], sbt, fw["w_out"],
                                               [gather(("w_up",), 0)] if l == 0 else None)
        if l == 0:
            fw["w_up"] = got[0][0]
        k, v, memn = _kv_fwd(mems, row(mem_norm_g, l), fw["wk"], fw["wv"])
        (h2, q, o, xn2), got = _xattn_fwd(h1, row(norm_xattn_g, l), fw["wq"], k, v, fw["wo"], [gather(("w_down",), 0)] if l == 0 else None)
        if l == 0:
            fw["w_down"] = got[0][0]
        (h3, hh, xn3), got = _ffn_fwd(h2, row(norm_ffn_g, l), fw["w_up"], conv_full[l], row(conv_b, l), fw["w_down"],
                                      [gather(BIG, l + 1)] if l + 1 < nl else None)
        if l + 1 < nl:
            full.append(dict(zip(BIG, got[0])))
        saved.append(dict(h=h, h1=h1, h2=h2, proj=proj, xn1=xn1, mix=mix, k=k, v=v, memn=memn, q=q, o=o, xn2=xn2, hh=hh, xn3=xn3, sbt=sbt))
        h = h3

    dh, loss_part, g_final = _loss_head(h, final_norm_g[None, :], tgt)

    big_grads = {}
    small_grads = [None] * nl

    def weight_grad(n, a, b, l):
        big_grads[n] = _grad_matmul(a, b, "grad_" + n, l, nl, big_grads.get(n))

    sums, slots = {}, {}

    def exchange(names, l):
        return _hosted_exchange([big_grads[n] for n in names], axes_of(names), l)

    def scatter(names, l):
        return _hosted_scatter([sums[n, l] for n in names], axes_of(names))

    def add_casts(names, theirs, l):
        for n, t in zip(names, theirs):
            g = big_grads[n]
            gl, gr, gc = g.shape
            if BIG_AXIS[n] == 0:
                sums[n, l] = _add_cast(g.reshape(gl * 4, 2, gr // 8, gc), t, core, l * 4, "grad_chip_sum_" + n)
            else:
                sums[n, l] = _add_cast(g.reshape(gl, 2, gr // 2, gc), t[None], core, l, "grad_chip_sum_" + n)[0]

    def keep_slots(names, got, l):
        for n, sl in zip(names, got):
            slots[n, l] = sl

    for l in reversed(range(nl)):
        fw, s = full[l], saved[l]
        above = l + 1 < nl
        dh3 = dh
        (dh2, dhh, act, g_cw, g_cb, g_nf), got = _ffn_bwd(dh3, s["h2"], s["hh"], row(norm_ffn_g, l), fw["w_up"], conv_full[l], row(conv_b, l), fw["w_down"],
                                                         [exchange(EARLY, l + 1)] if above else None)
        if above:
            add_casts(EARLY, got[0], l + 1)
        weight_grad("w_up", s["xn3"], dhh, l)
        weight_grad("w_down", act, dh3, l)
        (dh1, dq, dk, dv, g_nx), got = _xattn_bwd(dh2, s["h1"], s["q"], row(norm_xattn_g, l), fw["wq"], s["k"], s["v"], fw["wo"],
                                                  [exchange(LATE, l), scatter(EARLY, l + 1) if above else None])
        add_casts(LATE, got[0], l)
        if above:
            keep_slots(EARLY, got[1], l + 1)
        weight_grad("wq", s["xn2"], dq, l)
        weight_grad("wo", s["o"], dh2, l)
        weight_grad("wk", s["memn"], dk, l)
        weight_grad("wv", s["memn"], dv, l)
        g_mn = _kv_bwd(dk, dv, mems, fw["wk"], fw["wv"])
        (dh0, dproj, g_nm, g_pw, g_ps, g_sg, g_sw, g_sbt), got = _mixer_bwd(dh1, s["h"], s["proj"], row(norm_mix_g, l), fw["w_in"], pool_w[l], row(pool_scale, l), row(sgu_g, l), sgu_w[l], s["sbt"], fw["w_out"],
                                                                           [scatter(LATE, l)])
        keep_slots(LATE, got[0], l)
        weight_grad("w_in", s["xn1"], dproj, l)
        weight_grad("w_out", s["mix"], dh1, l)
        small_grads[l] = dict(norm_mix_g=g_nm, pool_w=g_pw, pool_scale=g_ps, sgu_g=g_sg, sgu_w=g_sw, sgu_b=jnp.sum(g_sbt, axis=-1),
                              norm_xattn_g=g_nx, mem_norm_g=g_mn, norm_ffn_g=g_nf, conv_w=g_cw, conv_b=g_cb)
        dh = dh0
    grad_x = dh[None]

    add_casts(EARLY, _run_hosted(exchange(EARLY, 0), "grad_sibling_exchange"), 0)
    keep_slots(EARLY, _run_hosted(scatter(EARLY, 0), "grad_chip_scatter"), 0)
    halves = []
    for n in BIG:
        buf = None
        for l in range(nl):
            buf = _sum_slots(sums[n, l], slots[n, l], BIG_AXIS[n], chip, core, l, nl, "grad_sum_" + n, buf)
        halves.append(buf.reshape(nl, 2 * buf.shape[2], buf.shape[3]))
    shard_grads = dict(zip(BIG, _sibling_assemble(halves)))

    layered = [n for n in SMALL if n != "final_norm_g"]
    parts = [small_grads[l][n].reshape(-1, PACK_WIDTH) for n in layered for l in range(nl)]
    parts.append(g_final.reshape(-1, PACK_WIDTH))
    parts.append(jnp.pad(loss_part, ((0, 0), (0, PACK_WIDTH - 1))))
    used = sum(p.shape[0] for p in parts)
    total = -(-used // 16) * 16
    packed = _allreduce_small(jnp.concatenate(parts + [jnp.zeros((total - used, PACK_WIDTH), F32)], axis=0))
    grads = dict(shard_grads)
    at = 0
    for n in layered:
        per_layer = []
        for l in range(nl):
            shape = small_grads[l][n].shape
            nrow = small_grads[l][n].size // PACK_WIDTH
            per_layer.append(packed[at:at + nrow].reshape(shape))
            at += nrow
        g = jnp.stack(per_layer)
        if n == "conv_w":
            cs = conv_w.shape[2]
            g = lax.dynamic_slice_in_dim(g, chip * cs, cs, axis=2)
        grads[n] = g.reshape(w[n].shape)
    grads["final_norm_g"] = packed[at:at + g_final.size // PACK_WIDTH].reshape(final_norm_g.shape)
    at += g_final.size // PACK_WIDTH
    loss = packed[at, 0]

    delta, new_m, new_v = {}, {}, {}
    for n in BIG:
        delta[n], new_m[n], new_v[n] = _adamw_big(w[n], grads[n], mom[n], var[n], "adamw_" + n)
    two_d = lambda a: a.reshape(-1, a.shape[-1])
    ds, nms, nvs = _adamw_small([two_d(w[n]) for n in SMALL], [two_d(grads[n]) for n in SMALL],
                                [two_d(mom[n]) for n in SMALL], [two_d(var[n]) for n in SMALL])
    for n, d_, m_, v_ in zip(SMALL, ds, nms, nvs):
        delta[n], new_m[n], new_v[n] = d_.reshape(w[n].shape), m_.reshape(w[n].shape), v_.reshape(w[n].shape)

    return (loss, grad_x, *[grads[n] for n in ORDER], *[delta[n] for n in ORDER], *[new_m[n] for n in ORDER], *[new_v[n] for n in ORDER])
```

```python
import math
from typing import NamedTuple

import jax
import jax.numpy as jnp
from jax import lax
from jax.experimental import pallas as pl
from jax.experimental.pallas import tpu as pltpu

F32 = jnp.float32
_MXU = jnp.bfloat16
_PAY = jnp.bfloat16
EPS = 1e-6
WINDOWS = (2, 4, 8, 16)
GROUP = 128
N_XHEADS = 4
HALO = 16
FF_TILE = 256
DOWN_TILES = 4
VMEM_LIMIT = 60 * 1024 * 1024
MESH = pl.DeviceIdType.MESH

ADAM_LR, ADAM_B1, ADAM_B2, ADAM_EPS, ADAM_WD, ADAM_STEP = 0.001, 0.9, 0.999, 1e-08, 0.01, 10

VM = pl.BlockSpec(memory_space=pltpu.VMEM)
HB = pl.BlockSpec(memory_space=pltpu.HBM)


def _nn(a, b):
    return jnp.dot(a, b, preferred_element_type=F32)


def _nt(a, b):
    return lax.dot_general(a, b, (((1,), (1,)), ((), ())), preferred_element_type=F32)


def _tn(a, b):
    return lax.dot_general(a, b, (((0,), (0,)), ((), ())), preferred_element_type=F32)


def _rms(x):
    r = lax.rsqrt(jnp.mean(x * x, axis=-1, keepdims=True) + EPS)
    return x * r, r


def _rms_bwd(dxn, xhat, r, g):
    dxh = dxn * g
    dx = r * (dxh - xhat * jnp.mean(dxh * xhat, axis=-1, keepdims=True))
    return dx, jnp.sum(dxn * xhat, axis=0, keepdims=True)


def _gelu(x):
    cdf = 0.5 * (1.0 + lax.erf(x * (2.0 ** -0.5)))
    return x * cdf, cdf


def _gelu_grad(x, cdf):
    return cdf + x * jnp.exp(-0.5 * x * x) * ((2.0 * math.pi) ** -0.5)


def _params(sem=None):
    return pltpu.CompilerParams(dimension_semantics=sem, vmem_limit_bytes=VMEM_LIMIT)


def _token_block(t, want):
    return want if t % want == 0 and t > want else GROUP


def _const_spec(shape):
    n = len(shape)
    return pl.BlockSpec(shape, lambda i: (0,) * n)


def _tril():
    return lax.broadcasted_iota(jnp.int32, (GROUP, GROUP), 0) >= lax.broadcasted_iota(jnp.int32, (GROUP, GROUP), 1)


def _shift_rows(x, k, edge):
    tb = x.shape[0]
    r8 = lax.broadcasted_iota(jnp.int32, (8, 1), 0)
    rolled = pltpu.roll(x, k % tb, 0)
    if k > 0:
        top = jnp.where(r8 < k, pltpu.roll(edge, k, 0), rolled[0:8, :])
        return jnp.concatenate([top, rolled[8:, :]], axis=0)
    bottom = jnp.where(r8 >= 8 + k, pltpu.roll(edge, 8 + k, 0), rolled[tb - 8:, :])
    return jnp.concatenate([rolled[:tb - 8, :], bottom], axis=0)


class _Hosted(NamedTuple):
    operands: tuple
    aliased: bool
    out_shapes: tuple
    sems: tuple
    stages: tuple


def _hosted_results(hosted):
    if hosted.aliased:
        return [jax.ShapeDtypeStruct(o.shape, o.dtype) for o in hosted.operands]
    return list(hosted.out_shapes)


def _call_hosting(main_body, hosted, *, name, steps, in_specs, out_specs, out_shape, scratch_shapes, operands):
    hosted = [hs for hs in (hosted or ()) if hs is not None]
    if not hosted:
        outs = pl.pallas_call(main_body, name=name, grid=(steps,), in_specs=in_specs, out_specs=out_specs, out_shape=out_shape,
                              scratch_shapes=scratch_shapes, compiler_params=_params(("arbitrary",)))(*operands)
        return outs, ()
    n_in, n_out, n_sc = len(in_specs), len(out_specs), len(scratch_shapes)
    shapes = [_hosted_results(hs) for hs in hosted]
    aliases, in_at, out_at = {}, n_in, n_out
    for hs, sh in zip(hosted, shapes):
        if hs.aliased:
            aliases.update({in_at + i: out_at + i for i in range(len(hs.operands))})
        in_at += len(hs.operands)
        out_at += len(sh)

    def body(*refs):
        at = [0]

        def take(n):
            at[0] += n
            return refs[at[0] - n:at[0]]

        ins = take(n_in)
        h_in = [take(len(hs.operands)) for hs in hosted]
        outs = take(n_out)
        h_out = [take(len(sh)) for sh in shapes]
        scratch = take(n_sc)
        h_sems = [take(len(hs.sems)) for hs in hosted]
        step = pl.program_id(0)

        def run(stage):
            for hs, a, b, c in zip(hosted, h_in, h_out, h_sems):
                if hs.stages[stage] is not None:
                    hs.stages[stage](a, b, c)

        @pl.when(step == 0)
        def _():
            run(0)

        if any(hs.stages[1] is not None for hs in hosted):
            @pl.when(step == (3 * steps) // 4)
            def _():
                run(1)

        main_body(*ins, *outs, *scratch)

        @pl.when(step == steps - 1)
        def _():
            run(2)

    flat = lambda lists: [x for xs in lists for x in xs]
    outs = pl.pallas_call(
        body, name=name, grid=(steps,), in_specs=list(in_specs) + [HB] * (in_at - n_in), out_specs=list(out_specs) + [HB] * (out_at - n_out),
        out_shape=list(out_shape) + flat(shapes), scratch_shapes=list(scratch_shapes) + flat(hs.sems for hs in hosted),
        input_output_aliases=aliases, compiler_params=_params(("arbitrary",)),
    )(*operands, *flat(hs.operands for hs in hosted))
    results, at = [], n_out
    for sh in shapes:
        results.append(outs[at:at + len(sh)])
        at += len(sh)
    return outs[:n_out], results


def _run_hosted(hosted, name):
    nh = len(hosted.operands)
    h_shapes = _hosted_results(hosted)

    def body(*refs):
        h_in, h_out, h_sems = refs[:nh], refs[nh:nh + len(h_shapes)], refs[nh + len(h_shapes):]
        for stage in hosted.stages:
            if stage is not None:
                stage(h_in, h_out, h_sems)

    return pl.pallas_call(
        body, name=name, in_specs=[HB] * nh, out_specs=[HB] * len(h_shapes), out_shape=h_shapes, scratch_shapes=list(hosted.sems),
        input_output_aliases={i: i for i in range(nh)} if hosted.aliased else {},
        compiler_params=pltpu.CompilerParams(has_side_effects=True),
    )(*hosted.operands)


def _window_sums(e, win, back):
    n = e.shape[0]
    k = 1
    while k < win:
        e = e + pltpu.roll(e, k if back else n - k, 0)
        k *= 2
    return e


def _pool_diff(prev, p, t0, gi, win):
    sl = slice(gi * GROUP, (gi + 1) * GROUP)
    tb = p.shape[0]
    s = _window_sums(jnp.concatenate([prev[:, sl], p[:, sl]], axis=0), win, True)[HALO:, :]
    tglob = t0 + lax.broadcasted_iota(jnp.int32, (tb, 1), 0)
    cnt = jnp.minimum(tglob + 1, win).astype(F32)
    return s / cnt - p[:, sl], cnt


def _layernorm(v):
    xc = v - jnp.mean(v, axis=-1, keepdims=True)
    rstd = lax.rsqrt(jnp.mean(xc * xc, axis=-1, keepdims=True) + EPS)
    return xc * rstd, rstd


def _mixer_fwd(h, g, w_in, pool_w, pool_scale, sgu_g, sgu_w, sgu_bt, w_out, hosted=None):
    t, d = h.shape
    pw = pool_w.shape[0] * GROUP
    sw = sgu_w.shape[0] * GROUP
    tb = _token_block(t, 512)

    def body(h_ref, g_ref, win_ref, pw_ref, ps_ref, sg_ref, sw_ref, sbt_ref, wout_ref, h1_ref, proj_ref, xn_ref, mix_ref, pext):
        i = pl.program_id(0)

        @pl.when(i == 0)
        def _():
            pext[...] = jnp.zeros((HALO, pw), F32)

        x = h_ref[...]
        xhat, _ = _rms(x)
        xn = (xhat * g_ref[...]).astype(_MXU)
        xn_ref[...] = xn
        proj = _nn(xn, win_ref[...])
        proj_ref[...] = proj
        p = proj[:, :pw]
        prev = pext[...]
        for gi, win in enumerate(WINDOWS):
            sl = slice(gi * GROUP, (gi + 1) * GROUP)
            dg, _ = _pool_diff(prev, p, i * tb, gi, win)
            e = _nn(dg.astype(_MXU), pw_ref[gi].astype(_MXU))
            mix_ref[:, sl] = (e * ps_ref[:, sl]).astype(_MXU)
        pext[...] = p[tb - HALO:tb, :]
        uv, _ = _gelu(proj[:, pw:])
        u = uv[:, :sw]
        vhat, _ = _layernorm(uv[:, sw:])
        vn = (vhat * sg_ref[...]).astype(_MXU)
        mask = _tril()
        for hh in range(sw // GROUP):
            wm = jnp.where(mask, sw_ref[hh], 0.0).astype(_MXU)
            for n in range(tb // GROUP):
                rows = slice(n * GROUP, (n + 1) * GROUP)
                cols = slice(hh * GROUP, (hh + 1) * GROUP)
                z = _nn(wm, vn[rows, cols]) + sbt_ref[hh]
                mix_ref[rows, pw + hh * GROUP:pw + (hh + 1) * GROUP] = (u[rows, cols] * z).astype(_MXU)
        h1_ref[...] = x + _nn(mix_ref[...], wout_ref[...])

    blk = lambda w: pl.BlockSpec((tb, w), lambda i: (i, 0))
    return _call_hosting(
        body, hosted, name="mixer_fwd", steps=t // tb,
        in_specs=[blk(d), VM, VM, VM, VM, VM, VM, VM, VM],
        out_specs=[blk(d), blk(w_in.shape[1]), blk(d), blk(d)],
        out_shape=[jax.ShapeDtypeStruct((t, d), F32), jax.ShapeDtypeStruct((t, w_in.shape[1]), F32),
                   jax.ShapeDtypeStruct((t, d), _MXU), jax.ShapeDtypeStruct((t, d), _MXU)],
        scratch_shapes=[pltpu.VMEM((HALO, pw), F32)],
        operands=(h, g, w_in, pool_w, pool_scale, sgu_g, sgu_w, sgu_bt, w_out))


def _mixer_bwd(dh1, h, proj, g, w_in, pool_w, pool_scale, sgu_g, sgu_w, sgu_bt, w_out, hosted=None):
    t, d = h.shape
    ng, nh = pool_w.shape[0], sgu_w.shape[0]
    pw, sw = ng * GROUP, nh * GROUP
    tb = _token_block(t, 256)
    nb = t // tb

    def body(dh1_ref, h_ref, proj_ref, halo_ref, g_ref, win_ref, pw_ref, ps_ref, sg_ref, sw_ref, sbt_ref, wout_ref,
             dh_ref, dproj_ref, gg_ref, gpw_ref, gps_ref, gsg_ref, gsw_ref, gsbt_ref, dext, duv):
        i = pl.program_id(0)
        blk = nb - 1 - i

        @pl.when(i == 0)
        def _():
            for r in (gg_ref, gpw_ref, gps_ref, gsg_ref, gsw_ref, gsbt_ref, dext):
                r[...] = jnp.zeros(r.shape, F32)

        dh1v = dh1_ref[...]
        dmix = _nt(dh1v.astype(_MXU), wout_ref[...])
        proj_v = proj_ref[...]
        p = proj_v[:, :pw]
        prev = jnp.where(blk == 0, 0.0, halo_ref[...])
        for gi, win in enumerate(WINDOWS):
            sl = slice(gi * GROUP, (gi + 1) * GROUP)
            dg, cnt = _pool_diff(prev, p, blk * tb, gi, win)
            dgm = dg.astype(_MXU)
            pwm = pw_ref[gi].astype(_MXU)
            e = _nn(dgm, pwm)
            dy = dmix[:, sl]
            gps_ref[:, sl] += jnp.sum(dy * e, axis=0, keepdims=True)
            de = (dy * ps_ref[:, sl]).astype(_MXU)
            gpw_ref[gi] += _tn(dgm, de)
            dd = _nt(de, pwm)
            ddc = dd / cnt
            acc = _window_sums(jnp.concatenate([ddc, dext[:, sl]], axis=0), win, False)[:tb, :]
            dext[:, sl] = ddc[0:HALO, :]
            dproj_ref[:, sl] = (acc - dd).astype(_MXU)
        pre = proj_v[:, pw:]
        uv, cdf = _gelu(pre)
        u = uv[:, :sw]
        vhat, rstd = _layernorm(uv[:, sw:])
        vn = (vhat * sg_ref[...]).astype(_MXU)
        mask = _tril()
        for hh in range(nh):
            wm = jnp.where(mask, sw_ref[hh], 0.0).astype(_MXU)
            cols = slice(hh * GROUP, (hh + 1) * GROUP)
            gw = jnp.zeros((GROUP, GROUP), F32)
            gb = jnp.zeros((GROUP, GROUP), F32)
            for n in range(tb // GROUP):
                rows = slice(n * GROUP, (n + 1) * GROUP)
                vs = vn[rows, cols]
                z = _nn(wm, vs) + sbt_ref[hh]
                dy = dmix[rows, pw + hh * GROUP:pw + (hh + 1) * GROUP]
                dz = dy * u[rows, cols]
                gb = gb + dz
                dzm = dz.astype(_MXU)
                gw = gw + _nt(dzm, vs)
                duv[rows, cols] = dy * z
                duv[rows, sw + hh * GROUP:sw + (hh + 1) * GROUP] = _tn(wm, dzm)
            gsw_ref[hh] += jnp.where(mask, gw, 0.0)
            gsbt_ref[hh] += gb
        dvn = duv[:, sw:]
        gsg_ref[...] += jnp.sum(dvn * vhat, axis=0, keepdims=True)
        dxh = dvn * sg_ref[...]
        dv = rstd * (dxh - jnp.mean(dxh, axis=-1, keepdims=True) - vhat * jnp.mean(dxh * vhat, axis=-1, keepdims=True))
        gp = _gelu_grad(pre, cdf)
        dproj_ref[:, pw:pw + sw] = (duv[:, :sw] * gp[:, :sw]).astype(_MXU)
        dproj_ref[:, pw + sw:] = (dv * gp[:, sw:]).astype(_MXU)
        dxn = _nt(dproj_ref[...], win_ref[...])
        xhat, r = _rms(h_ref[...])
        dx, gg = _rms_bwd(dxn, xhat, r, g_ref[...])
        gg_ref[...] += gg
        dh_ref[...] = dh1v + dx

    rev = lambda w: pl.BlockSpec((tb, w), lambda i: (nb - 1 - i, 0))
    halo = pl.BlockSpec((HALO, pw), lambda i: (jnp.maximum((nb - 1 - i) * (tb // HALO) - 1, 0), 0))
    small = [(1, d), (ng, GROUP, GROUP), (1, pw), (1, sw), (nh, GROUP, GROUP), (nh, GROUP, GROUP)]
    return _call_hosting(
        body, hosted, name="mixer_bwd", steps=nb,
        in_specs=[rev(d), rev(d), rev(proj.shape[1]), halo, VM, VM, VM, VM, VM, VM, VM, VM],
        out_specs=[rev(d), rev(proj.shape[1])] + [_const_spec(s) for s in small],
        out_shape=[jax.ShapeDtypeStruct((t, d), F32), jax.ShapeDtypeStruct(proj.shape, _MXU)]
        + [jax.ShapeDtypeStruct(s, F32) for s in small],
        scratch_shapes=[pltpu.VMEM((HALO, pw), F32), pltpu.VMEM((tb, 2 * sw), F32)],
        operands=(dh1, h, proj, proj, g, w_in, pool_w, pool_scale, sgu_g, sgu_w, sgu_bt, w_out))


def _kv_fwd(mem, gm, wk, wv):
    n, d = mem.shape

    def body(mem_ref, gm_ref, wk_ref, wv_ref, k_ref, v_ref, memn_ref):
        xhat, _ = _rms(mem_ref[...])
        memn = (xhat * gm_ref[...]).astype(_MXU)
        memn_ref[...] = memn
        k_ref[...] = _nn(memn, wk_ref[...]).astype(_MXU)
        v_ref[...] = _nn(memn, wv_ref[...]).astype(_MXU)

    return pl.pallas_call(
        body, name="kv_fwd", in_specs=[VM] * 4, out_specs=[VM] * 3,
        out_shape=[jax.ShapeDtypeStruct((n, d), _MXU)] * 3, compiler_params=_params(),
    )(mem, gm, wk, wv)


def _kv_bwd(dk, dv, mem, wk, wv):
    n, d = mem.shape

    def body(dk_ref, dv_ref, mem_ref, wk_ref, wv_ref, ggm_ref):
        dmemn = _nt(dk_ref[...].astype(_MXU), wk_ref[...]) + _nt(dv_ref[...].astype(_MXU), wv_ref[...])
        xhat, _ = _rms(mem_ref[...])
        ggm_ref[...] = jnp.sum(dmemn * xhat, axis=0, keepdims=True)

    return pl.pallas_call(
        body, name="kv_bwd", in_specs=[VM] * 5, out_specs=VM,
        out_shape=jax.ShapeDtypeStruct((1, d), F32), compiler_params=_params(),
    )(dk, dv, mem, wk, wv)


def _softmax_rows(qm, k_ref, sl, scale):
    s = _nt(qm, k_ref[:, sl]) * scale
    e = jnp.exp(s - jnp.max(s, axis=-1, keepdims=True))
    return e / jnp.sum(e, axis=-1, keepdims=True)


def _xattn_fwd(h, g, wq, k, v, wo, hosted=None):
    t, d = h.shape
    hd = d // N_XHEADS
    scale = hd ** -0.5
    tb = _token_block(t, 512)

    def body(h_ref, g_ref, wq_ref, k_ref, v_ref, wo_ref, h2_ref, q_ref, o_ref, xn_ref):
        x = h_ref[...]
        xhat, _ = _rms(x)
        xn = (xhat * g_ref[...]).astype(_MXU)
        xn_ref[...] = xn
        qm = _nn(xn, wq_ref[...]).astype(_MXU)
        q_ref[...] = qm
        for a in range(N_XHEADS):
            sl = slice(a * hd, (a + 1) * hd)
            pr = _softmax_rows(qm[:, sl], k_ref, sl, scale)
            o_ref[:, sl] = _nn(pr.astype(_MXU), v_ref[:, sl]).astype(_MXU)
        h2_ref[...] = x + _nn(o_ref[...], wo_ref[...])

    blk = pl.BlockSpec((tb, d), lambda i: (i, 0))
    return _call_hosting(
        body, hosted, name="xattn_fwd", steps=t // tb,
        in_specs=[blk, VM, VM, VM, VM, VM], out_specs=[blk] * 4,
        out_shape=[jax.ShapeDtypeStruct((t, d), F32)] + [jax.ShapeDtypeStruct((t, d), _MXU)] * 3,
        scratch_shapes=[], operands=(h, g, wq, k, v, wo))


def _xattn_bwd(dh2, h, q, g, wq, k, v, wo, hosted=None):
    t, d = h.shape
    n = k.shape[0]
    hd = d // N_XHEADS
    scale = hd ** -0.5
    tb = _token_block(t, 512)

    def body(dh2_ref, h_ref, q_ref, g_ref, wq_ref, k_ref, v_ref, wo_ref, dh_ref, dq_ref, dk_ref, dv_ref, gg_ref):
        @pl.when(pl.program_id(0) == 0)
        def _():
            for r in (dk_ref, dv_ref, gg_ref):
                r[...] = jnp.zeros(r.shape, F32)

        dh2v = dh2_ref[...]
        dom = _nt(dh2v.astype(_MXU), wo_ref[...]).astype(_MXU)
        for a in range(N_XHEADS):
            sl = slice(a * hd, (a + 1) * hd)
            qh = q_ref[:, sl]
            pr = _softmax_rows(qh, k_ref, sl, scale)
            dv_ref[:, sl] += _tn(pr.astype(_MXU), dom[:, sl])
            dpr = _nt(dom[:, sl], v_ref[:, sl])
            ds = (pr * (dpr - jnp.sum(dpr * pr, axis=-1, keepdims=True)) * scale).astype(_MXU)
            dq_ref[:, sl] = _nn(ds, k_ref[:, sl]).astype(_MXU)
            dk_ref[:, sl] += _tn(ds, qh)
        dxn = _nt(dq_ref[...], wq_ref[...])
        xhat, r = _rms(h_ref[...])
        dx, gg = _rms_bwd(dxn, xhat, r, g_ref[...])
        gg_ref[...] += gg
        dh_ref[...] = dh2v + dx

    blk = pl.BlockSpec((tb, d), lambda i: (i, 0))
    return _call_hosting(
        body, hosted, name="xattn_bwd", steps=t // tb,
        in_specs=[blk, blk, blk, VM, VM, VM, VM, VM],
        out_specs=[blk, blk, _const_spec((n, d)), _const_spec((n, d)), _const_spec((1, d))],
        out_shape=[jax.ShapeDtypeStruct((t, d), F32), jax.ShapeDtypeStruct((t, d), _MXU),
                   jax.ShapeDtypeStruct((n, d), F32), jax.ShapeDtypeStruct((n, d), F32), jax.ShapeDtypeStruct((1, d), F32)],
        scratch_shapes=[], operands=(dh2, h, q, g, wq, k, v, wo))


def _ffn_fwd(h, g, w_up, conv_w, conv_b, w_down, hosted=None):
    t, d = h.shape
    f = w_down.shape[0]
    ft = FF_TILE
    tb = _token_block(t, 512)

    def body(h_ref, g_ref, wup_ref, cw_ref, cb_ref, wdown_ref, h3_ref, hh_ref, hc_ref, ext, carry, act_sc):
        @pl.when(pl.program_id(0) == 0)
        def _():
            carry[...] = jnp.zeros(carry.shape, F32)

        x = h_ref[...]
        xhat, _ = _rms(x)
        xn = (xhat * g_ref[...]).astype(_MXU)
        acc = jnp.zeros((tb, d), F32)
        up = lambda j: [_nn(xn, wup_ref[:, off:off + ft]) for off in (j * ft, f + j * ft)]
        up_next = up(0)
        for j in range(f // ft):
            hc = []
            up_cur = up_next
            if j + 1 < f // ft:
                up_next = up(j + 1)
            for part, off in enumerate((j * ft, f + j * ft)):
                cols = slice(off, off + ft)
                cur = up_cur[part]
                hh_ref[:, cols] = cur.astype(_MXU)
                ext[part, 0:8, :] = carry[:, cols]
                ext[part, 8:8 + tb, :] = cur
                carry[:, cols] = cur[tb - 8:tb, :]
                hc.append(cb_ref[:, cols] + cw_ref[0:1, cols] * ext[part, 6:6 + tb, :]
                          + cw_ref[1:2, cols] * ext[part, 7:7 + tb, :] + cw_ref[2:3, cols] * cur)
                hc_ref[:, cols] = hc[part].astype(_MXU)
            at = j % DOWN_TILES
            act_sc[:, at * ft:(at + 1) * ft] = (hc[0] * jax.nn.sigmoid(hc[0]) * hc[1]).astype(_MXU)
            if at + 1 == DOWN_TILES or j + 1 == f // ft:
                acc = acc + _nn(act_sc[:, 0:(at + 1) * ft], wdown_ref[(j - at) * ft:(j + 1) * ft, :])
        h3_ref[...] = x + acc

    blk = lambda w: pl.BlockSpec((tb, w), lambda i: (i, 0))
    return _call_hosting(
        body, hosted, name="ffn_fwd", steps=t // tb,
        in_specs=[blk(d), VM, VM, VM, VM, VM], out_specs=[blk(d), blk(2 * f), blk(2 * f)],
        out_shape=[jax.ShapeDtypeStruct((t, d), F32), jax.ShapeDtypeStruct((t, 2 * f), _MXU), jax.ShapeDtypeStruct((t, 2 * f), _MXU)],
        scratch_shapes=[pltpu.VMEM((2, 8 + tb, ft), F32), pltpu.VMEM((8, 2 * f), F32), pltpu.VMEM((tb, DOWN_TILES * ft), _MXU)],
        operands=(h, g, w_up, conv_w, conv_b, w_down))


def _ffn_bwd(dh3, h, hh, hc, g, w_up, conv_w, w_down, hosted=None):
    t, d = h.shape
    f = w_down.shape[0]
    ft = FF_TILE
    tb = _token_block(t, 256)
    nb = t // tb

    def body(dh3_ref, h_ref, hh_ref, hc_ref, g_ref, wup_ref, cw_ref, wdown_ref,
             dh_ref, dhh_ref, act_ref, xn_ref, gcw_ref, gcb_ref, gg_ref, dcarry):
        @pl.when(pl.program_id(0) == 0)
        def _():
            for r in (gcw_ref, gcb_ref, gg_ref, dcarry):
                r[...] = jnp.zeros(r.shape, F32)

        dh3v = dh3_ref[...]
        dhm = dh3v.astype(_MXU)
        dxn = jnp.zeros((tb, d), F32)
        dact_next = _nt(dhm, wdown_ref[0:ft, :])
        for j in range(f // ft):
            dact = dact_next
            if j + 1 < f // ft:
                dact_next = _nt(dhm, wdown_ref[(j + 1) * ft:(j + 2) * ft, :])
            gate = hc_ref[:, j * ft:(j + 1) * ft].astype(F32)
            val = hc_ref[:, f + j * ft:f + (j + 1) * ft].astype(F32)
            sg = jax.nn.sigmoid(gate)
            silu = gate * sg
            act_ref[:, j * ft:(j + 1) * ft] = (silu * val).astype(_MXU)
            dhc = (dact * val * sg * (1.0 + gate * (1.0 - sg)), dact * silu)
            for part, off in enumerate((j * ft, f + j * ft)):
                cols = slice(off, off + ft)
                dc = dhc[part]
                c0 = hh_ref[:, cols].astype(F32)
                after = dcarry[:, cols]
                ahead1 = _shift_rows(dc, -1, after)
                ahead2 = _shift_rows(dc, -2, after)
                dcarry[:, cols] = dc[0:8, :]
                gcb_ref[:, cols] += jnp.sum(dc, axis=0, keepdims=True)
                gcw_ref[0:1, cols] += jnp.sum(ahead2 * c0, axis=0, keepdims=True)
                gcw_ref[1:2, cols] += jnp.sum(ahead1 * c0, axis=0, keepdims=True)
                gcw_ref[2:3, cols] += jnp.sum(dc * c0, axis=0, keepdims=True)
                dhh = (cw_ref[2:3, cols] * dc + cw_ref[1:2, cols] * ahead1 + cw_ref[0:1, cols] * ahead2).astype(_MXU)
                dhh_ref[:, cols] = dhh
                dxn = dxn + _nt(dhh, wup_ref[:, cols])
        xhat, r = _rms(h_ref[...])
        xn_ref[...] = (xhat * g_ref[...]).astype(_MXU)
        dx, gg = _rms_bwd(dxn, xhat, r, g_ref[...])
        gg_ref[...] += gg
        dh_ref[...] = dh3v + dx

    rev = lambda w: pl.BlockSpec((tb, w), lambda i: (nb - 1 - i, 0))
    return _call_hosting(
        body, hosted, name="ffn_bwd", steps=nb,
        in_specs=[rev(d), rev(d), rev(2 * f), rev(2 * f), VM, VM, VM, VM],
        out_specs=[rev(d), rev(2 * f), rev(f), rev(d), _const_spec((3, 2 * f)), _const_spec((1, 2 * f)), _const_spec((1, d))],
        out_shape=[jax.ShapeDtypeStruct((t, d), F32), jax.ShapeDtypeStruct((t, 2 * f), _MXU), jax.ShapeDtypeStruct((t, f), _MXU),
                   jax.ShapeDtypeStruct((t, d), _MXU),
                   jax.ShapeDtypeStruct((3, 2 * f), F32), jax.ShapeDtypeStruct((1, 2 * f), F32), jax.ShapeDtypeStruct((1, d), F32)],
        scratch_shapes=[pltpu.VMEM((8, 2 * f), F32)],
        operands=(dh3, h, hh, hc, g, w_up, conv_w, w_down))


def _loss_head(h, g, target):
    t, d = h.shape
    tb = _token_block(t, 512)

    def body(h_ref, g_ref, tgt_ref, dh_ref, loss_ref, gg_ref):
        @pl.when(pl.program_id(0) == 0)
        def _():
            loss_ref[...] = jnp.zeros(loss_ref.shape, F32)
            gg_ref[...] = jnp.zeros(gg_ref.shape, F32)

        xhat, r = _rms(h_ref[...])
        err = xhat * g_ref[...] - tgt_ref[...]
        loss_ref[...] += 0.5 * jnp.sum(jnp.sum(err * err, axis=-1, keepdims=True), axis=0, keepdims=True) / d
        dx, gg = _rms_bwd(err / d, xhat, r, g_ref[...])
        gg_ref[...] += gg
        dh_ref[...] = dx

    blk = pl.BlockSpec((tb, d), lambda i: (i, 0))
    return pl.pallas_call(
        body, name="loss_head", grid=(t // tb,),
        in_specs=[blk, VM, blk], out_specs=[blk, _const_spec((1, 1)), _const_spec((1, d))],
        out_shape=[jax.ShapeDtypeStruct((t, d), F32), jax.ShapeDtypeStruct((1, 1), F32), jax.ShapeDtypeStruct((1, d), F32)],
        compiler_params=_params(("arbitrary",)),
    )(h, g, target)


def _largest_tile(n, cap, mult=128):
    best = None
    for c in range(mult, min(n, cap) + 1, mult):
        if n % c == 0:
            best = c
    return best if best is not None else n


def _grad_matmul(a, b, name, layer, n_layers, into=None):
    t, m = a.shape
    n = b.shape[1]
    tm, tn, tk = _largest_tile(m, 1408), _largest_tile(n, 1408), _largest_tile(t, 1024)
    nk = t // tk

    def body(a_ref, b_ref, *rest):
        o_ref = rest[-1]

        @pl.when(pl.program_id(2) == 0)
        def _():
            o_ref[...] = jnp.zeros(o_ref.shape, F32)

        o_ref[...] += _tn(a_ref[...].astype(_MXU), b_ref[...].astype(_MXU))

    in_specs = [pl.BlockSpec((tk, tm), lambda i, j, k: (k, i)), pl.BlockSpec((tk, tn), lambda i, j, k: (k, j))]
    operands = (a, b)
    aliases = {}
    if into is not None:
        in_specs.append(pl.BlockSpec(memory_space=pl.ANY))
        operands = (a, b, into)
        aliases = {2: 0}
    return pl.pallas_call(
        body, name=name, grid=(m // tm, n // tn, nk), in_specs=in_specs,
        out_specs=pl.BlockSpec((None, tm, tn), lambda i, j, k: (layer, i, j)),
        out_shape=jax.ShapeDtypeStruct((n_layers, m, n), F32), input_output_aliases=aliases,
        compiler_params=_params(("parallel", "parallel", "arbitrary")),
    )(*operands)


def _adamw_math(w, g, m, v):
    m = ADAM_B1 * m + (1.0 - ADAM_B1) * g
    v = ADAM_B2 * v + (1.0 - ADAM_B2) * (g * g)
    m_hat = m / (1.0 - ADAM_B1 ** ADAM_STEP)
    v_hat = v / (1.0 - ADAM_B2 ** ADAM_STEP)
    return -ADAM_LR * (m_hat / (jnp.sqrt(v_hat) + ADAM_EPS) + ADAM_WD * w), m, v


def _row_block(rows, cols, max_bytes=1 << 20, mult=16):
    best = None
    for r in range(mult, rows + 1, mult):
        if rows % r == 0 and r * cols * 4 <= max_bytes:
            best = r
    return best if best is not None else rows


def _adamw_big(w, g, m, v, name):
    shape = w.shape
    cols = shape[-1]
    flat = lambda a: a.reshape(-1, cols)
    rows = flat(w).shape[0]
    rb = _row_block(rows, cols)

    def body(w_ref, g_ref, m_ref, v_ref, d_ref, nm_ref, nv_ref):
        d_ref[...], nm_ref[...], nv_ref[...] = _adamw_math(w_ref[...], g_ref[...], m_ref[...], v_ref[...])

    blk = pl.BlockSpec((rb, cols), lambda i: (i, 0))
    outs = pl.pallas_call(
        body, name=name, grid=(rows // rb,), in_specs=[blk] * 4, out_specs=[blk] * 3,
        out_shape=[jax.ShapeDtypeStruct((rows, cols), F32)] * 3, compiler_params=_params(("parallel",)),
    )(flat(w), flat(g), flat(m), flat(v))
    return [o.reshape(shape) for o in outs]


def _adamw_small(ws, gs, ms, vs):
    n = len(ws)

    def body(*refs):
        for a in range(n):
            w_ref, g_ref, m_ref, v_ref = (refs[s * n + a] for s in range(4))
            d_ref, nm_ref, nv_ref = (refs[(4 + s) * n + a] for s in range(3))
            d_ref[...], nm_ref[...], nv_ref[...] = _adamw_math(w_ref[...], g_ref[...], m_ref[...], v_ref[...])

    outs = pl.pallas_call(
        body, name="adamw_small", in_specs=[VM] * (4 * n), out_specs=[VM] * (3 * n),
        out_shape=[jax.ShapeDtypeStruct(w.shape, F32) for w in ws] * 3, compiler_params=_params(),
    )(*ws, *gs, *ms, *vs)
    return outs[:n], outs[n:2 * n], outs[2 * n:]


def _place():
    x, y, c = lax.axis_index("x"), lax.axis_index("y"), lax.axis_index("c")
    chips = [(1 - x, y), (x, 1 - y), (1 - x, 1 - y)]
    return x, y, c, chips


def _rows(start, size, mult=16):
    return pl.ds(pl.multiple_of(start, mult), size)


def _full_window(ref, axis, chip, half=None):
    r, c = ref.shape
    if axis == 0:
        rs = r // 4
        if half is None:
            return ref.at[_rows(chip * rs, rs), :]
        return ref.at[_rows(chip * rs + half * (rs // 2), rs // 2), :]
    cs = c // 4
    if half is None:
        return ref.at[:, _rows(chip * cs, cs, 128)]
    return ref.at[_rows(half * (r // 2), r // 2), _rows(chip * cs, cs, 128)]


def _remote(src, dst, send_sem, recv_sem, to):
    return pltpu.make_async_remote_copy(src_ref=src, dst_ref=dst, send_sem=send_sem, recv_sem=recv_sem,
                                        device_id=to, device_id_type=MESH)


def _scalars(*vals):
    return jnp.stack([jnp.asarray(v, jnp.int32) for v in vals])


def _cast_place(shard, layer, axis, chip, name):
    _, rs, cs = shard.shape
    full = (rs * 4, cs) if axis == 0 else (rs, cs * 4)
    rb = _row_block(rs, cs)
    nrb = rs // rb

    def body(chip_ref, s_ref, o_ref):
        o_ref[...] = s_ref[...].astype(_PAY)

    if axis == 0:
        out_map = lambda i, chip_ref: (chip_ref[0] * nrb + i, 0)
    else:
        out_map = lambda i, chip_ref: (i, chip_ref[0])
    return pl.pallas_call(
        body, name=name,
        grid_spec=pltpu.PrefetchScalarGridSpec(
            num_scalar_prefetch=1, grid=(nrb,),
            in_specs=[pl.BlockSpec((None, rb, cs), lambda i, chip_ref: (layer, i, 0))],
            out_specs=pl.BlockSpec((rb, cs), out_map)),
        out_shape=jax.ShapeDtypeStruct(full, _PAY), compiler_params=_params(("parallel",)),
    )(_scalars(chip), shard)


def _hosted_allgather(placed, axes):
    n = len(placed)

    def each(outs, half_of):
        x, y, c, chips = _place()
        for i in range(n):
            for k, chip in enumerate(chips):
                yield i * 3 + k, (*chip, c), (x, y, 1 - c), _full_window(outs[i], axes[i], 2 * x + y, c), \
                    _full_window(outs[i], axes[i], 2 * chip[0] + chip[1], half_of(c))

    def start(_, outs, sems):
        send, recv, _, _ = sems
        for s, peer, _, mine, _ in each(outs, lambda c: c):
            _remote(mine, mine, send.at[s], recv.at[s], peer).start()

    def middle(_, outs, sems):
        send, recv, fsend, frecv = sems
        for s, _, sibling, _, got in each(outs, lambda c: c):
            _remote(got, got, send.at[s], recv.at[s], sibling).wait_recv()
            _remote(got, got, fsend.at[s], frecv.at[s], sibling).start()

    def finish(_, outs, sems):
        send, recv, fsend, frecv = sems
        for s, _, sibling, _, got in each(outs, lambda c: 1 - c):
            _remote(got, got, fsend.at[s], frecv.at[s], sibling).wait_recv()
        for s, peer, sibling, mine, got in each(outs, lambda c: c):
            _remote(mine, mine, send.at[s], recv.at[s], peer).wait_send()
            _remote(got, got, fsend.at[s], frecv.at[s], sibling).wait_send()

    return _Hosted(tuple(placed), True, (), (pltpu.SemaphoreType.DMA((n * 3,)),) * 4, (start, middle, finish))


def _allgather_conv(conv_shard):
    nl, taps, cs = conv_shard.shape

    def body(in_ref, out_ref, send, recv, local):
        x, y, c, chips = _place()
        mine = out_ref.at[:, :, _rows((2 * x + y) * cs, cs, 128)]
        own = pltpu.make_async_copy(in_ref, mine, local)
        own.start()
        sends = [_remote(in_ref, mine, send.at[k], recv.at[k], (*chip, c)) for k, chip in enumerate(chips)]
        for cp in sends:
            cp.start()
        for k, chip in enumerate(chips):
            got = out_ref.at[:, :, _rows((2 * chip[0] + chip[1]) * cs, cs, 128)]
            _remote(got, got, send.at[k], recv.at[k], (*chip, c)).wait_recv()
        for cp in sends:
            cp.wait_send()
        own.wait()

    return pl.pallas_call(
        body, name="allgather_conv", in_specs=[HB], out_specs=HB, out_shape=jax.ShapeDtypeStruct((nl, taps, cs * 4), conv_shard.dtype),
        scratch_shapes=[pltpu.SemaphoreType.DMA((3,)), pltpu.SemaphoreType.DMA((3,)), pltpu.SemaphoreType.DMA],
        compiler_params=pltpu.CompilerParams(has_side_effects=True),
    )(conv_shard)


def _hosted_exchange(grads, axes, layer):
    na = len(grads)
    views = [g.reshape(g.shape[0], 4, 2, g.shape[1] // 8, g.shape[2]) if ax == 0 else g for g, ax in zip(grads, axes)]

    def region(ref, axis, half):
        if axis == 0:
            return ref.at[layer, :, half]
        r = ref.shape[1]
        return ref.at[layer, _rows(half * (r // 2), r // 2), :]

    def copies(ins, land, sems):
        send, recv = sems
        x, y, c, _ = _place()
        return [_remote(region(ins[a], axes[a], 1 - c), land[a], send.at[a], recv.at[a], (x, y, 1 - c)) for a in range(na)]

    def start(ins, land, sems):
        for cp in copies(ins, land, sems):
            cp.start()

    def finish(ins, land, sems):
        for cp in copies(ins, land, sems):
            cp.wait()

    shapes = [(4, g.shape[1] // 8, g.shape[2]) if ax == 0 else (g.shape[1] // 2, g.shape[2]) for g, ax in zip(grads, axes)]
    return _Hosted(tuple(views), False, tuple(jax.ShapeDtypeStruct(s, F32) for s in shapes),
                   (pltpu.SemaphoreType.DMA((na,)),) * 2, (start, None, finish))


def _add_cast(mine, theirs, core, base, name):
    na, nb, cols = theirs.shape
    rb = _row_block(nb, cols)

    def body(core_ref, a_ref, b_ref, o_ref):
        o_ref[...] = (a_ref[...] + b_ref[...]).astype(_PAY)

    blk = pl.BlockSpec((None, rb, cols), lambda i, k, core_ref: (i, k, 0))
    return pl.pallas_call(
        body, name=name,
        grid_spec=pltpu.PrefetchScalarGridSpec(
            num_scalar_prefetch=1, grid=(na, nb // rb),
            in_specs=[pl.BlockSpec((None, None, rb, cols), lambda i, k, core_ref: (base + i, core_ref[0], k, 0)), blk], out_specs=blk),
        out_shape=jax.ShapeDtypeStruct((na, nb, cols), _PAY), compiler_params=_params(("parallel", "parallel")),
    )(_scalars(core), mine, theirs)


def _piece(ref, axis, chip):
    if axis == 0:
        return ref.at[chip]
    cs = ref.shape[1] // 4
    return ref.at[:, _rows(chip * cs, cs, 128)]


def _hosted_scatter(sums, axes):
    na = len(sums)

    def piece_shape(a):
        if axes[a] == 0:
            return (sums[a].shape[1], sums[a].shape[2])
        return (sums[a].shape[0], sums[a].shape[1] // 4)

    def copies(ins, slots, sems):
        send, recv = sems
        _, _, c, chips = _place()
        return [_remote(_piece(ins[a], axes[a], 2 * chip[0] + chip[1]), slots[a].at[k], send.at[a * 3 + k], recv.at[a * 3 + k], (*chip, c))
                for a in range(na) for k, chip in enumerate(chips)]

    def start(ins, slots, sems):
        for cp in copies(ins, slots, sems):
            cp.start()

    def finish(ins, slots, sems):
        for cp in copies(ins, slots, sems):
            cp.wait()

    return _Hosted(tuple(sums), False, tuple(jax.ShapeDtypeStruct((3,) + piece_shape(a), sums[a].dtype) for a in range(na)),
                   (pltpu.SemaphoreType.DMA((na * 3,)),) * 2, (start, None, finish))


def _sum_slots(sums, slots, axis, chip, core, layer, n_layers, name, into=None):
    _, hr, cs = slots.shape
    rb = _row_block(hr, cs)

    def body(at_ref, own_ref, s_ref, *rest):
        rest[-1][...] = ((own_ref[...].astype(F32) + s_ref[0].astype(F32)) + s_ref[1].astype(F32)) + s_ref[2].astype(F32)

    if axis == 0:
        own = pl.BlockSpec((None, rb, cs), lambda k, at_ref: (at_ref[0], k, 0))
    else:
        own = pl.BlockSpec((rb, cs), lambda k, at_ref: (k, at_ref[0]))
    in_specs = [own, pl.BlockSpec((3, rb, cs), lambda k, at_ref: (0, k, 0))]
    operands = (sums, slots)
    aliases = {}
    if into is not None:
        in_specs.append(pl.BlockSpec(memory_space=pl.ANY))
        operands = (sums, slots, into)
        aliases = {3: 0}
    return pl.pallas_call(
        body, name=name,
        grid_spec=pltpu.PrefetchScalarGridSpec(
            num_scalar_prefetch=1, grid=(hr // rb,), in_specs=in_specs,
            out_specs=pl.BlockSpec((None, None, rb, cs), lambda k, at_ref: (layer, at_ref[1], k, 0))),
        out_shape=jax.ShapeDtypeStruct((n_layers, 2, hr, cs), F32), input_output_aliases=aliases,
        compiler_params=_params(("parallel",)),
    )(_scalars(chip, core), *operands)


def _sibling_assemble(shards):
    na = len(shards)

    def body(*refs):
        outs = refs[na:2 * na]
        send, recv = refs[2 * na:]
        x, y, c, _ = _place()
        copies = []
        for a in range(na):
            hr = outs[a].shape[1] // 2
            mine = outs[a].at[:, _rows(c * hr, hr), :]
            cp = _remote(mine, mine, send.at[a], recv.at[a], (x, y, 1 - c))
            cp.start()
            copies.append(cp)
        for cp in copies:
            cp.wait()

    return pl.pallas_call(
        body, name="grad_sibling_assemble", in_specs=[HB] * na, out_specs=[HB] * na,
        out_shape=[jax.ShapeDtypeStruct(s.shape, F32) for s in shards], input_output_aliases={a: a for a in range(na)},
        scratch_shapes=[pltpu.SemaphoreType.DMA((na,))] * 2,
        compiler_params=pltpu.CompilerParams(has_side_effects=True),
    )(*shards)


def _allreduce_small(buf):
    rows, w = buf.shape
    half = rows // 2

    def body(buf_ref, out_ref, land, slots, red, sems_send, sems_recv):
        x, y, c, chips = _place()
        me = 2 * x + y
        sibling = (x, y, 1 - c)
        first = _remote(buf_ref, land, sems_send.at[0], sems_recv.at[0], sibling)
        first.start()
        first.wait()
        mine = pl.ds(pl.multiple_of(c * half, 8), half)
        slots[me] = buf_ref[mine, :] + land[mine, :]
        sends = []
        for k, chip in enumerate(chips):
            cp = _remote(slots.at[me], slots.at[me], sems_send.at[1 + k], sems_recv.at[1 + k], (*chip, c))
            cp.start()
            sends.append(cp)
        for k, chip in enumerate(chips):
            got = slots.at[2 * chip[0] + chip[1]]
            _remote(got, got, sems_send.at[1 + k], sems_recv.at[1 + k], sibling).wait_recv()
        red[...] = ((slots[0] + slots[1]) + slots[2]) + slots[3]
        out_ref[mine, :] = red[...]
        last = _remote(red, out_ref.at[mine, :], sems_send.at[4], sems_recv.at[4], sibling)
        last.start()
        theirs = out_ref.at[pl.ds(pl.multiple_of((1 - c) * half, 8), half), :]
        _remote(red, theirs, sems_send.at[4], sems_recv.at[4], sibling).wait_recv()
        for cp in sends:
            cp.wait_send()
        last.wait_send()

    return pl.pallas_call(
        body, name="allreduce_small", in_specs=[VM], out_specs=VM, out_shape=jax.ShapeDtypeStruct((rows, w), F32),
        scratch_shapes=[pltpu.VMEM((rows, w), F32), pltpu.VMEM((4, half, w), F32), pltpu.VMEM((half, w), F32),
                        pltpu.SemaphoreType.DMA((5,)), pltpu.SemaphoreType.DMA((5,))],
        compiler_params=pltpu.CompilerParams(has_side_effects=True, vmem_limit_bytes=VMEM_LIMIT),
    )(buf)


BIG = ("w_in", "w_out", "wq", "wk", "wv", "wo", "w_up", "w_down")
EARLY = ("w_in", "w_out", "wq", "wk", "wv", "wo")
LATE = ("w_up", "w_down")
BIG_AXIS = {"w_in": 1, "w_out": 0, "wq": 0, "wk": 0, "wv": 0, "wo": 0, "w_up": 1, "w_down": 0}
SMALL = ("norm_mix_g", "pool_w", "pool_scale", "sgu_g", "sgu_w", "sgu_b", "norm_xattn_g", "mem_norm_g", "norm_ffn_g",
         "conv_w", "conv_b", "final_norm_g")
ORDER = ("norm_mix_g", "w_in", "pool_w", "pool_scale", "sgu_g", "sgu_w", "sgu_b", "w_out", "norm_xattn_g", "mem_norm_g",
         "wq", "wk", "wv", "wo", "norm_ffn_g", "w_up", "conv_w", "conv_b", "w_down", "final_norm_g")
PACK_WIDTH = 512


def kernel(x, mem, norm_mix_g, w_in, pool_w, pool_scale, sgu_g, sgu_w, sgu_b, w_out, norm_xattn_g, mem_norm_g, wq, wk, wv, wo, norm_ffn_g, w_up, conv_w, conv_b, w_down, final_norm_g, loss_target, m_norm_mix_g, m_w_in, m_pool_w, m_pool_scale, m_sgu_g, m_sgu_w, m_sgu_b, m_w_out, m_norm_xattn_g, m_mem_norm_g, m_wq, m_wk, m_wv, m_wo, m_norm_ffn_g, m_w_up, m_conv_w, m_conv_b, m_w_down, m_final_norm_g, v_norm_mix_g, v_w_in, v_pool_w, v_pool_scale, v_sgu_g, v_sgu_w, v_sgu_b, v_w_out, v_norm_xattn_g, v_mem_norm_g, v_wq, v_wk, v_wv, v_wo, v_norm_ffn_g, v_w_up, v_conv_w, v_conv_b, v_w_down, v_final_norm_g):
    given = dict(locals())
    w = {n: given[n] for n in ORDER}
    mom = {n: given["m_" + n] for n in ORDER}
    var = {n: given["v_" + n] for n in ORDER}
    nl = w_in.shape[0]
    xs, mems, tgt = x[0], mem[0], loss_target[0]
    chip = 2 * lax.axis_index("x") + lax.axis_index("y")
    core = lax.axis_index("c")

    axes_of = lambda names: [BIG_AXIS[n] for n in names]
    placed = [{n: _cast_place(w[n], l, BIG_AXIS[n], chip, f"place_{n}_{l}") for n in BIG} for l in range(nl)]
    conv_full = _allgather_conv(conv_w)

    def gather(names, l):
        return _hosted_allgather([placed[l][n] for n in names], axes_of(names))

    full = [dict(zip(EARLY, _run_hosted(gather(EARLY, 0), "allgather_weights")))]

    row = lambda a, l: a[l][None, :]
    saved = []
    h = xs
    for l in range(nl):
        fw = full[l]
        sbt = jnp.broadcast_to(sgu_b[l][:, :, None], sgu_w[l].shape)
        (h1, proj, xn1, mix), got = _mixer_fwd(h, row(norm_mix_g, l), fw["w_in"], pool_w[l], row(pool_scale, l), row(sgu_g, l), sgu_w[l], sbt, fw["w_out"],
                                               [gather(("w_up",), 0)] if l == 0 else None)
        if l == 0:
            fw["w_up"] = got[0][0]
        k, v, memn = _kv_fwd(mems, row(mem_norm_g, l), fw["wk"], fw["wv"])
        (h2, q, o, xn2), got = _xattn_fwd(h1, row(norm_xattn_g, l), fw["wq"], k, v, fw["wo"], [gather(("w_down",), 0)] if l == 0 else None)
        if l == 0:
            fw["w_down"] = got[0][0]
        (h3, hh, hc), got = _ffn_fwd(h2, row(norm_ffn_g, l), fw["w_up"], conv_full[l], row(conv_b, l), fw["w_down"],
                                      [gather(BIG, l + 1)] if l + 1 < nl else None)
        if l + 1 < nl:
            full.append(dict(zip(BIG, got[0])))
        saved.append(dict(h=h, h1=h1, h2=h2, proj=proj, xn1=xn1, mix=mix, k=k, v=v, memn=memn, q=q, o=o, xn2=xn2, hh=hh, hc=hc, sbt=sbt))
        h = h3

    dh, loss_part, g_final = _loss_head(h, final_norm_g[None, :], tgt)

    big_grads = {}
    small_grads = [None] * nl

    def weight_grad(n, a, b, l):
        big_grads[n] = _grad_matmul(a, b, "grad_" + n, l, nl, big_grads.get(n))

    sums, slots = {}, {}

    def exchange(names, l):
        return _hosted_exchange([big_grads[n] for n in names], axes_of(names), l)

    def scatter(names, l):
        return _hosted_scatter([sums[n, l] for n in names], axes_of(names))

    def add_casts(names, theirs, l):
        for n, t in zip(names, theirs):
            g = big_grads[n]
            gl, gr, gc = g.shape
            if BIG_AXIS[n] == 0:
                sums[n, l] = _add_cast(g.reshape(gl * 4, 2, gr // 8, gc), t, core, l * 4, "grad_chip_sum_" + n)
            else:
                sums[n, l] = _add_cast(g.reshape(gl, 2, gr // 2, gc), t[None], core, l, "grad_chip_sum_" + n)[0]

    def keep_slots(names, got, l):
        for n, sl in zip(names, got):
            slots[n, l] = sl

    for l in reversed(range(nl)):
        fw, s = full[l], saved[l]
        above = l + 1 < nl
        dh3 = dh
        (dh2, dhh, act, xn3, g_cw, g_cb, g_nf), got = _ffn_bwd(dh3, s["h2"], s["hh"], s["hc"], row(norm_ffn_g, l), fw["w_up"], conv_full[l], fw["w_down"],
                                                         [exchange(EARLY, l + 1)] if above else None)
        if above:
            add_casts(EARLY, got[0], l + 1)
        weight_grad("w_up", xn3, dhh, l)
        weight_grad("w_down", act, dh3, l)
        (dh1, dq, dk, dv, g_nx), got = _xattn_bwd(dh2, s["h1"], s["q"], row(norm_xattn_g, l), fw["wq"], s["k"], s["v"], fw["wo"],
                                                  [exchange(LATE, l), scatter(EARLY, l + 1) if above else None])
        add_casts(LATE, got[0], l)
        if above:
            keep_slots(EARLY, got[1], l + 1)
        weight_grad("wq", s["xn2"], dq, l)
        weight_grad("wo", s["o"], dh2, l)
        weight_grad("wk", s["memn"], dk, l)
        weight_grad("wv", s["memn"], dv, l)
        g_mn = _kv_bwd(dk, dv, mems, fw["wk"], fw["wv"])
        (dh0, dproj, g_nm, g_pw, g_ps, g_sg, g_sw, g_sbt), got = _mixer_bwd(dh1, s["h"], s["proj"], row(norm_mix_g, l), fw["w_in"], pool_w[l], row(pool_scale, l), row(sgu_g, l), sgu_w[l], s["sbt"], fw["w_out"],
                                                                           [scatter(LATE, l)])
        keep_slots(LATE, got[0], l)
        weight_grad("w_in", s["xn1"], dproj, l)
        weight_grad("w_out", s["mix"], dh1, l)
        small_grads[l] = dict(norm_mix_g=g_nm, pool_w=g_pw, pool_scale=g_ps, sgu_g=g_sg, sgu_w=g_sw, sgu_b=jnp.sum(g_sbt, axis=-1),
                              norm_xattn_g=g_nx, mem_norm_g=g_mn, norm_ffn_g=g_nf, conv_w=g_cw, conv_b=g_cb)
        dh = dh0
    grad_x = dh[None]

    add_casts(EARLY, _run_hosted(exchange(EARLY, 0), "grad_sibling_exchange"), 0)
    keep_slots(EARLY, _run_hosted(scatter(EARLY, 0), "grad_chip_scatter"), 0)
    halves = []
    for n in BIG:
        buf = None
        for l in range(nl):
            buf = _sum_slots(sums[n, l], slots[n, l], BIG_AXIS[n], chip, core, l, nl, "grad_sum_" + n, buf)
        halves.append(buf.reshape(nl, 2 * buf.shape[2], buf.shape[3]))
    shard_grads = dict(zip(BIG, _sibling_assemble(halves)))

    layered = [n for n in SMALL if n != "final_norm_g"]
    parts = [small_grads[l][n].reshape(-1, PACK_WIDTH) for n in layered for l in range(nl)]
    parts.append(g_final.reshape(-1, PACK_WIDTH))
    parts.append(jnp.pad(loss_part, ((0, 0), (0, PACK_WIDTH - 1))))
    used = sum(p.shape[0] for p in parts)
    total = -(-used // 16) * 16
    packed = _allreduce_small(jnp.concatenate(parts + [jnp.zeros((total - used, PACK_WIDTH), F32)], axis=0))
    grads = dict(shard_grads)
    at = 0
    for n in layered:
        per_layer = []
        for l in range(nl):
            shape = small_grads[l][n].shape
            nrow = small_grads[l][n].size // PACK_WIDTH
            per_layer.append(packed[at:at + nrow].reshape(shape))
            at += nrow
        g = jnp.stack(per_layer)
        if n == "conv_w":
            cs = conv_w.shape[2]
            g = lax.dynamic_slice_in_dim(g, chip * cs, cs, axis=2)
        grads[n] = g.reshape(w[n].shape)
    grads["final_norm_g"] = packed[at:at + g_final.size // PACK_WIDTH].reshape(final_norm_g.shape)
    at += g_final.size // PACK_WIDTH
    loss = packed[at, 0]

    delta, new_m, new_v = {}, {}, {}
    for n in BIG:
        delta[n], new_m[n], new_v[n] = _adamw_big(w[n], grads[n], mom[n], var[n], "adamw_" + n)
    two_d = lambda a: a.reshape(-1, a.shape[-1])
    ds, nms, nvs = _adamw_small([two_d(w[n]) for n in SMALL], [two_d(grads[n]) for n in SMALL],
                                [two_d(mom[n]) for n in SMALL], [two_d(var[n]) for n in SMALL])
    for n, d_, m_, v_ in zip(SMALL, ds, nms, nvs):
        delta[n], new_m[n], new_v[n] = d_.reshape(w[n].shape), m_.reshape(w[n].shape), v_.reshape(w[n].shape)

    return (loss, grad_x, *[grads[n] for n in ORDER], *[delta[n] for n in ORDER], *[new_m[n] for n in ORDER], *[new_v[n] for n in ORDER])
```

```python
import math
from typing import NamedTuple

import jax
import jax.numpy as jnp
from jax import lax
from jax.experimental import pallas as pl
from jax.experimental.pallas import tpu as pltpu

F32 = jnp.float32
_MXU = jnp.bfloat16
_PAY = jnp.bfloat16
EPS = 1e-6
WINDOWS = (2, 4, 8, 16)
GROUP = 128
N_XHEADS = 4
HALO = 16
FF_TILE = 256
DOWN_TILES = 4
VMEM_LIMIT = 60 * 1024 * 1024
MESH = pl.DeviceIdType.MESH

ADAM_LR, ADAM_B1, ADAM_B2, ADAM_EPS, ADAM_WD, ADAM_STEP = 0.001, 0.9, 0.999, 1e-08, 0.01, 10

VM = pl.BlockSpec(memory_space=pltpu.VMEM)
HB = pl.BlockSpec(memory_space=pltpu.HBM)


def _nn(a, b):
    return jnp.dot(a, b, preferred_element_type=F32)


def _nt(a, b):
    return lax.dot_general(a, b, (((1,), (1,)), ((), ())), preferred_element_type=F32)


def _tn(a, b):
    return lax.dot_general(a, b, (((0,), (0,)), ((), ())), preferred_element_type=F32)


def _rms(x):
    r = lax.rsqrt(jnp.mean(x * x, axis=-1, keepdims=True) + EPS)
    return x * r, r


def _rms_bwd(dxn, xhat, r, g):
    dxh = dxn * g
    dx = r * (dxh - xhat * jnp.mean(dxh * xhat, axis=-1, keepdims=True))
    return dx, jnp.sum(dxn * xhat, axis=0, keepdims=True)


def _gelu(x):
    cdf = 0.5 * (1.0 + lax.erf(x * (2.0 ** -0.5)))
    return x * cdf, cdf


def _gelu_grad(x, cdf):
    return cdf + x * jnp.exp(-0.5 * x * x) * ((2.0 * math.pi) ** -0.5)


def _params(sem=None):
    return pltpu.CompilerParams(dimension_semantics=sem, vmem_limit_bytes=VMEM_LIMIT)


def _token_block(t, want):
    return want if t % want == 0 and t > want else GROUP


def _const_spec(shape):
    n = len(shape)
    return pl.BlockSpec(shape, lambda i: (0,) * n)


def _tril():
    return lax.broadcasted_iota(jnp.int32, (GROUP, GROUP), 0) >= lax.broadcasted_iota(jnp.int32, (GROUP, GROUP), 1)


def _shift_rows(x, k, edge):
    tb = x.shape[0]
    r8 = lax.broadcasted_iota(jnp.int32, (8, 1), 0)
    rolled = pltpu.roll(x, k % tb, 0)
    if k > 0:
        top = jnp.where(r8 < k, pltpu.roll(edge, k, 0), rolled[0:8, :])
        return jnp.concatenate([top, rolled[8:, :]], axis=0)
    bottom = jnp.where(r8 >= 8 + k, pltpu.roll(edge, 8 + k, 0), rolled[tb - 8:, :])
    return jnp.concatenate([rolled[:tb - 8, :], bottom], axis=0)


class _Hosted(NamedTuple):
    operands: tuple
    aliased: bool
    out_shapes: tuple
    sems: tuple
    stages: tuple


def _hosted_results(hosted):
    if hosted.aliased:
        return [jax.ShapeDtypeStruct(o.shape, o.dtype) for o in hosted.operands]
    return list(hosted.out_shapes)


def _call_hosting(main_body, hosted, *, name, steps, in_specs, out_specs, out_shape, scratch_shapes, operands, aliases=None):
    grid = steps if isinstance(steps, tuple) else (steps,)
    semantics = ("arbitrary",) * len(grid)
    hosted = [hs for hs in (hosted or ()) if hs is not None]
    if not hosted:
        outs = pl.pallas_call(main_body, name=name, grid=grid, in_specs=in_specs, out_specs=out_specs, out_shape=out_shape,
                              scratch_shapes=scratch_shapes, input_output_aliases=aliases or {}, compiler_params=_params(semantics))(*operands)
        return outs, ()
    n_in, n_out, n_sc = len(in_specs), len(out_specs), len(scratch_shapes)
    shapes = [_hosted_results(hs) for hs in hosted]
    aliases, in_at, out_at = dict(aliases or {}), n_in, n_out
    for hs, sh in zip(hosted, shapes):
        if hs.aliased:
            aliases.update({in_at + i: out_at + i for i in range(len(hs.operands))})
        in_at += len(hs.operands)
        out_at += len(sh)

    def body(*refs):
        at = [0]

        def take(n):
            at[0] += n
            return refs[at[0] - n:at[0]]

        ins = take(n_in)
        h_in = [take(len(hs.operands)) for hs in hosted]
        outs = take(n_out)
        h_out = [take(len(sh)) for sh in shapes]
        scratch = take(n_sc)
        h_sems = [take(len(hs.sems)) for hs in hosted]
        ids = [pl.program_id(a) for a in range(len(grid))]

        def at_step(where):
            lead, rest = where
            ok = ids[0] == lead
            for a in range(1, len(grid)):
                ok = jnp.logical_and(ok, ids[a] == (grid[a] - 1 if rest else 0))
            return ok

        def run(stage):
            for hs, a, b, c in zip(hosted, h_in, h_out, h_sems):
                if hs.stages[stage] is not None:
                    hs.stages[stage](a, b, c)

        @pl.when(at_step((0, 0)))
        def _():
            run(0)

        if any(hs.stages[1] is not None for hs in hosted):
            @pl.when(at_step(((3 * grid[0]) // 4, 0)))
            def _():
                run(1)

        main_body(*ins, *outs, *scratch)

        @pl.when(at_step((grid[0] - 1, -1)))
        def _():
            run(2)

    flat = lambda lists: [x for xs in lists for x in xs]
    outs = pl.pallas_call(
        body, name=name, grid=grid, in_specs=list(in_specs) + [HB] * (in_at - n_in), out_specs=list(out_specs) + [HB] * (out_at - n_out),
        out_shape=list(out_shape) + flat(shapes), scratch_shapes=list(scratch_shapes) + flat(hs.sems for hs in hosted),
        input_output_aliases=aliases, compiler_params=_params(semantics),
    )(*operands, *flat(hs.operands for hs in hosted))
    results, at = [], n_out
    for sh in shapes:
        results.append(outs[at:at + len(sh)])
        at += len(sh)
    return outs[:n_out], results


def _run_hosted(hosted, name):
    nh = len(hosted.operands)
    h_shapes = _hosted_results(hosted)

    def body(*refs):
        h_in, h_out, h_sems = refs[:nh], refs[nh:nh + len(h_shapes)], refs[nh + len(h_shapes):]
        for stage in hosted.stages:
            if stage is not None:
                stage(h_in, h_out, h_sems)

    return pl.pallas_call(
        body, name=name, in_specs=[HB] * nh, out_specs=[HB] * len(h_shapes), out_shape=h_shapes, scratch_shapes=list(hosted.sems),
        input_output_aliases={i: i for i in range(nh)} if hosted.aliased else {},
        compiler_params=pltpu.CompilerParams(has_side_effects=True),
    )(*hosted.operands)


def _window_sums(e, win, back):
    n = e.shape[0]
    k = 1
    while k < win:
        e = e + pltpu.roll(e, k if back else n - k, 0)
        k *= 2
    return e


def _pool_diff(prev, p, t0, gi, win):
    sl = slice(gi * GROUP, (gi + 1) * GROUP)
    tb = p.shape[0]
    s = _window_sums(jnp.concatenate([prev[:, sl], p[:, sl]], axis=0), win, True)[HALO:, :]
    tglob = t0 + lax.broadcasted_iota(jnp.int32, (tb, 1), 0)
    cnt = jnp.minimum(tglob + 1, win).astype(F32)
    return s / cnt - p[:, sl], cnt


def _layernorm(v):
    xc = v - jnp.mean(v, axis=-1, keepdims=True)
    rstd = lax.rsqrt(jnp.mean(xc * xc, axis=-1, keepdims=True) + EPS)
    return xc * rstd, rstd


def _mixer_fwd(h, g, w_in, pool_w, pool_scale, sgu_g, sgu_w, sgu_bt, w_out, hosted=None):
    t, d = h.shape
    pw = pool_w.shape[0] * GROUP
    sw = sgu_w.shape[0] * GROUP
    tb = _token_block(t, 512)

    def body(h_ref, g_ref, win_ref, pw_ref, ps_ref, sg_ref, sw_ref, sbt_ref, wout_ref, h1_ref, proj_ref, xn_ref, mix_ref, pext):
        i = pl.program_id(0)

        @pl.when(i == 0)
        def _():
            pext[...] = jnp.zeros((HALO, pw), F32)

        x = h_ref[...]
        xhat, _ = _rms(x)
        xn = (xhat * g_ref[...]).astype(_MXU)
        xn_ref[...] = xn
        proj = _nn(xn, win_ref[...])
        proj_ref[...] = proj
        p = proj[:, :pw]
        prev = pext[...]
        for gi, win in enumerate(WINDOWS):
            sl = slice(gi * GROUP, (gi + 1) * GROUP)
            dg, _ = _pool_diff(prev, p, i * tb, gi, win)
            e = _nn(dg.astype(_MXU), pw_ref[gi].astype(_MXU))
            mix_ref[:, sl] = (e * ps_ref[:, sl]).astype(_MXU)
        pext[...] = p[tb - HALO:tb, :]
        uv, _ = _gelu(proj[:, pw:])
        u = uv[:, :sw]
        vhat, _ = _layernorm(uv[:, sw:])
        vn = (vhat * sg_ref[...]).astype(_MXU)
        mask = _tril()
        chunks = [slice(n * GROUP, (n + 1) * GROUP) for n in range(tb // GROUP)]
        for hh in range(sw // GROUP):
            wm = jnp.where(mask, sw_ref[hh], 0.0).astype(_MXU)
            cols = slice(hh * GROUP, (hh + 1) * GROUP)
            z = _nn(wm, jnp.concatenate([vn[rows, cols] for rows in chunks], axis=1))
            for n, rows in enumerate(chunks):
                mix_ref[rows, pw + hh * GROUP:pw + (hh + 1) * GROUP] = (u[rows, cols] * (z[:, chunks[n]] + sbt_ref[hh])).astype(_MXU)
        h1_ref[...] = x + _nn(mix_ref[...], wout_ref[...])

    blk = lambda w: pl.BlockSpec((tb, w), lambda i: (i, 0))
    return _call_hosting(
        body, hosted, name="mixer_fwd", steps=t // tb,
        in_specs=[blk(d), VM, VM, VM, VM, VM, VM, VM, VM],
        out_specs=[blk(d), blk(w_in.shape[1]), blk(d), blk(d)],
        out_shape=[jax.ShapeDtypeStruct((t, d), F32), jax.ShapeDtypeStruct((t, w_in.shape[1]), F32),
                   jax.ShapeDtypeStruct((t, d), _MXU), jax.ShapeDtypeStruct((t, d), _MXU)],
        scratch_shapes=[pltpu.VMEM((HALO, pw), F32)],
        operands=(h, g, w_in, pool_w, pool_scale, sgu_g, sgu_w, sgu_bt, w_out))


def _mixer_bwd(dh1, h, proj, g, w_in, pool_w, pool_scale, sgu_g, sgu_w, sgu_bt, w_out, hosted=None):
    t, d = h.shape
    ng, nh = pool_w.shape[0], sgu_w.shape[0]
    pw, sw = ng * GROUP, nh * GROUP
    tb = _token_block(t, 256)
    nb = t // tb

    def body(dh1_ref, h_ref, proj_ref, halo_ref, g_ref, win_ref, pw_ref, ps_ref, sg_ref, sw_ref, sbt_ref, wout_ref,
             dh_ref, dproj_ref, gg_ref, gpw_ref, gps_ref, gsg_ref, gsw_ref, gsbt_ref, dext, duv):
        i = pl.program_id(0)
        blk = nb - 1 - i

        @pl.when(i == 0)
        def _():
            for r in (gg_ref, gpw_ref, gps_ref, gsg_ref, gsw_ref, gsbt_ref, dext):
                r[...] = jnp.zeros(r.shape, F32)

        dh1v = dh1_ref[...]
        dmix = _nt(dh1v.astype(_MXU), wout_ref[...])
        proj_v = proj_ref[...]
        p = proj_v[:, :pw]
        prev = jnp.where(blk == 0, 0.0, halo_ref[...])
        for gi, win in enumerate(WINDOWS):
            sl = slice(gi * GROUP, (gi + 1) * GROUP)
            dg, cnt = _pool_diff(prev, p, blk * tb, gi, win)
            dgm = dg.astype(_MXU)
            pwm = pw_ref[gi].astype(_MXU)
            e = _nn(dgm, pwm)
            dy = dmix[:, sl]
            gps_ref[:, sl] += jnp.sum(dy * e, axis=0, keepdims=True)
            de = (dy * ps_ref[:, sl]).astype(_MXU)
            gpw_ref[gi] += _tn(dgm, de)
            dd = _nt(de, pwm)
            ddc = dd / cnt
            acc = _window_sums(jnp.concatenate([ddc, dext[:, sl]], axis=0), win, False)[:tb, :]
            dext[:, sl] = ddc[0:HALO, :]
            dproj_ref[:, sl] = (acc - dd).astype(_MXU)
        pre = proj_v[:, pw:]
        uv, cdf = _gelu(pre)
        u = uv[:, :sw]
        vhat, rstd = _layernorm(uv[:, sw:])
        vn = (vhat * sg_ref[...]).astype(_MXU)
        mask = _tril()
        chunks = [slice(n * GROUP, (n + 1) * GROUP) for n in range(tb // GROUP)]
        side_by_side = lambda a, cols: jnp.concatenate([a[rows, cols] for rows in chunks], axis=1)
        for hh in range(nh):
            wm = jnp.where(mask, sw_ref[hh], 0.0).astype(_MXU)
            cols = slice(hh * GROUP, (hh + 1) * GROUP)
            vs = side_by_side(vn, cols)
            z = _nn(wm, vs)
            dy = side_by_side(dmix, slice(pw + hh * GROUP, pw + (hh + 1) * GROUP))
            dz = dy * side_by_side(u, cols)
            dzm = dz.astype(_MXU)
            dvs = _tn(wm, dzm)
            gsw_ref[hh] += jnp.where(mask, _nt(dzm, vs), 0.0)
            gb = jnp.zeros((GROUP, GROUP), F32)
            for n, rows in enumerate(chunks):
                gb = gb + dz[:, chunks[n]]
                duv[rows, cols] = dy[:, chunks[n]] * (z[:, chunks[n]] + sbt_ref[hh])
                duv[rows, sw + hh * GROUP:sw + (hh + 1) * GROUP] = dvs[:, chunks[n]]
            gsbt_ref[hh] += gb
        dvn = duv[:, sw:]
        gsg_ref[...] += jnp.sum(dvn * vhat, axis=0, keepdims=True)
        dxh = dvn * sg_ref[...]
        dv = rstd * (dxh - jnp.mean(dxh, axis=-1, keepdims=True) - vhat * jnp.mean(dxh * vhat, axis=-1, keepdims=True))
        gp = _gelu_grad(pre, cdf)
        dproj_ref[:, pw:pw + sw] = (duv[:, :sw] * gp[:, :sw]).astype(_MXU)
        dproj_ref[:, pw + sw:] = (dv * gp[:, sw:]).astype(_MXU)
        dxn = _nt(dproj_ref[...], win_ref[...])
        xhat, r = _rms(h_ref[...])
        dx, gg = _rms_bwd(dxn, xhat, r, g_ref[...])
        gg_ref[...] += gg
        dh_ref[...] = dh1v + dx

    rev = lambda w: pl.BlockSpec((tb, w), lambda i: (nb - 1 - i, 0))
    halo = pl.BlockSpec((HALO, pw), lambda i: (jnp.maximum((nb - 1 - i) * (tb // HALO) - 1, 0), 0))
    small = [(1, d), (ng, GROUP, GROUP), (1, pw), (1, sw), (nh, GROUP, GROUP), (nh, GROUP, GROUP)]
    return _call_hosting(
        body, hosted, name="mixer_bwd", steps=nb,
        in_specs=[rev(d), rev(d), rev(proj.shape[1]), halo, VM, VM, VM, VM, VM, VM, VM, VM],
        out_specs=[rev(d), rev(proj.shape[1])] + [_const_spec(s) for s in small],
        out_shape=[jax.ShapeDtypeStruct((t, d), F32), jax.ShapeDtypeStruct(proj.shape, _MXU)]
        + [jax.ShapeDtypeStruct(s, F32) for s in small],
        scratch_shapes=[pltpu.VMEM((HALO, pw), F32), pltpu.VMEM((tb, 2 * sw), F32)],
        operands=(dh1, h, proj, proj, g, w_in, pool_w, pool_scale, sgu_g, sgu_w, sgu_bt, w_out))


def _kv_fwd(mem, gm, wk, wv):
    n, d = mem.shape

    def body(mem_ref, gm_ref, wk_ref, wv_ref, k_ref, v_ref, memn_ref):
        xhat, _ = _rms(mem_ref[...])
        memn = (xhat * gm_ref[...]).astype(_MXU)
        memn_ref[...] = memn
        k_ref[...] = _nn(memn, wk_ref[...]).astype(_MXU)
        v_ref[...] = _nn(memn, wv_ref[...]).astype(_MXU)

    return pl.pallas_call(
        body, name="kv_fwd", in_specs=[VM] * 4, out_specs=[VM] * 3,
        out_shape=[jax.ShapeDtypeStruct((n, d), _MXU)] * 3, compiler_params=_params(),
    )(mem, gm, wk, wv)


def _kv_bwd(dk, dv, mem, wk, wv):
    n, d = mem.shape

    def body(dk_ref, dv_ref, mem_ref, wk_ref, wv_ref, ggm_ref):
        dmemn = _nt(dk_ref[...].astype(_MXU), wk_ref[...]) + _nt(dv_ref[...].astype(_MXU), wv_ref[...])
        xhat, _ = _rms(mem_ref[...])
        ggm_ref[...] = jnp.sum(dmemn * xhat, axis=0, keepdims=True)

    return pl.pallas_call(
        body, name="kv_bwd", in_specs=[VM] * 5, out_specs=VM,
        out_shape=jax.ShapeDtypeStruct((1, d), F32), compiler_params=_params(),
    )(dk, dv, mem, wk, wv)


def _softmax(s):
    e = jnp.exp(s - jnp.max(s, axis=-1, keepdims=True))
    return e / jnp.sum(e, axis=-1, keepdims=True)


def _one_ahead(n, issue):
    nxt = issue(0)
    for a in range(n):
        cur = nxt
        if a + 1 < n:
            nxt = issue(a + 1)
        yield a, cur


def _xattn_fwd(h, g, wq, k, v, wo, hosted=None):
    t, d = h.shape
    hd = d // N_XHEADS
    scale = hd ** -0.5
    tb = _token_block(t, 512)

    def body(h_ref, g_ref, wq_ref, k_ref, v_ref, wo_ref, h2_ref, q_ref, o_ref, xn_ref):
        x = h_ref[...]
        xhat, _ = _rms(x)
        xn = (xhat * g_ref[...]).astype(_MXU)
        xn_ref[...] = xn
        qm = _nn(xn, wq_ref[...]).astype(_MXU)
        q_ref[...] = qm
        heads = [slice(a * hd, (a + 1) * hd) for a in range(N_XHEADS)]
        for a, s in _one_ahead(N_XHEADS, lambda a: _nt(qm[:, heads[a]], k_ref[:, heads[a]]) * scale):
            o_ref[:, heads[a]] = _nn(_softmax(s).astype(_MXU), v_ref[:, heads[a]]).astype(_MXU)
        h2_ref[...] = x + _nn(o_ref[...], wo_ref[...])

    blk = pl.BlockSpec((tb, d), lambda i: (i, 0))
    return _call_hosting(
        body, hosted, name="xattn_fwd", steps=t // tb,
        in_specs=[blk, VM, VM, VM, VM, VM], out_specs=[blk] * 4,
        out_shape=[jax.ShapeDtypeStruct((t, d), F32)] + [jax.ShapeDtypeStruct((t, d), _MXU)] * 3,
        scratch_shapes=[], operands=(h, g, wq, k, v, wo))


def _xattn_bwd(dh2, h, q, g, wq, k, v, wo, hosted=None):
    t, d = h.shape
    n = k.shape[0]
    hd = d // N_XHEADS
    scale = hd ** -0.5
    tb = _token_block(t, 512)

    def body(dh2_ref, h_ref, q_ref, g_ref, wq_ref, k_ref, v_ref, wo_ref, dh_ref, dq_ref, dk_ref, dv_ref, gg_ref):
        @pl.when(pl.program_id(0) == 0)
        def _():
            for r in (dk_ref, dv_ref, gg_ref):
                r[...] = jnp.zeros(r.shape, F32)

        dh2v = dh2_ref[...]
        dom = _nt(dh2v.astype(_MXU), wo_ref[...]).astype(_MXU)
        heads = [slice(a * hd, (a + 1) * hd) for a in range(N_XHEADS)]
        issue = lambda a: (_nt(q_ref[:, heads[a]], k_ref[:, heads[a]]) * scale, _nt(dom[:, heads[a]], v_ref[:, heads[a]]))
        for a, (s, dpr) in _one_ahead(N_XHEADS, issue):
            sl = heads[a]
            pr = _softmax(s)
            dv_ref[:, sl] += _tn(pr.astype(_MXU), dom[:, sl])
            ds = (pr * (dpr - jnp.sum(dpr * pr, axis=-1, keepdims=True)) * scale).astype(_MXU)
            dq_ref[:, sl] = _nn(ds, k_ref[:, sl]).astype(_MXU)
            dk_ref[:, sl] += _tn(ds, q_ref[:, sl])
        dxn = _nt(dq_ref[...], wq_ref[...])
        xhat, r = _rms(h_ref[...])
        dx, gg = _rms_bwd(dxn, xhat, r, g_ref[...])
        gg_ref[...] += gg
        dh_ref[...] = dh2v + dx

    blk = pl.BlockSpec((tb, d), lambda i: (i, 0))
    return _call_hosting(
        body, hosted, name="xattn_bwd", steps=t // tb,
        in_specs=[blk, blk, blk, VM, VM, VM, VM, VM],
        out_specs=[blk, blk, _const_spec((n, d)), _const_spec((n, d)), _const_spec((1, d))],
        out_shape=[jax.ShapeDtypeStruct((t, d), F32), jax.ShapeDtypeStruct((t, d), _MXU),
                   jax.ShapeDtypeStruct((n, d), F32), jax.ShapeDtypeStruct((n, d), F32), jax.ShapeDtypeStruct((1, d), F32)],
        scratch_shapes=[], operands=(dh2, h, q, g, wq, k, v, wo))


def _ffn_fwd(h, g, w_up, conv_w, conv_b, w_down, hosted=None):
    t, d = h.shape
    f = w_down.shape[0]
    ft = FF_TILE
    tb = _token_block(t, 512)

    def body(h_ref, g_ref, wup_ref, cw_ref, cb_ref, wdown_ref, h3_ref, hh_ref, hc_ref, ext, carry, act_sc):
        @pl.when(pl.program_id(0) == 0)
        def _():
            carry[...] = jnp.zeros(carry.shape, F32)

        x = h_ref[...]
        xhat, _ = _rms(x)
        xn = (xhat * g_ref[...]).astype(_MXU)
        acc = jnp.zeros((tb, d), F32)
        up = lambda j: [_nn(xn, wup_ref[:, off:off + ft]) for off in (j * ft, f + j * ft)]
        up_next = up(0)
        for j in range(f // ft):
            hc = []
            up_cur = up_next
            if j + 1 < f // ft:
                up_next = up(j + 1)
            for part, off in enumerate((j * ft, f + j * ft)):
                cols = slice(off, off + ft)
                cur = up_cur[part]
                hh_ref[:, cols] = cur.astype(_MXU)
                ext[part, 0:8, :] = carry[:, cols]
                ext[part, 8:8 + tb, :] = cur
                carry[:, cols] = cur[tb - 8:tb, :]
                hc.append(cb_ref[:, cols] + cw_ref[0:1, cols] * ext[part, 6:6 + tb, :]
                          + cw_ref[1:2, cols] * ext[part, 7:7 + tb, :] + cw_ref[2:3, cols] * cur)
                hc_ref[:, cols] = hc[part].astype(_MXU)
            at = j % DOWN_TILES
            act_sc[:, at * ft:(at + 1) * ft] = (hc[0] * jax.nn.sigmoid(hc[0]) * hc[1]).astype(_MXU)
            if at + 1 == DOWN_TILES or j + 1 == f // ft:
                acc = acc + _nn(act_sc[:, 0:(at + 1) * ft], wdown_ref[(j - at) * ft:(j + 1) * ft, :])
        h3_ref[...] = x + acc

    blk = lambda w: pl.BlockSpec((tb, w), lambda i: (i, 0))
    return _call_hosting(
        body, hosted, name="ffn_fwd", steps=t // tb,
        in_specs=[blk(d), VM, VM, VM, VM, VM], out_specs=[blk(d), blk(2 * f), blk(2 * f)],
        out_shape=[jax.ShapeDtypeStruct((t, d), F32), jax.ShapeDtypeStruct((t, 2 * f), _MXU), jax.ShapeDtypeStruct((t, 2 * f), _MXU)],
        scratch_shapes=[pltpu.VMEM((2, 8 + tb, ft), F32), pltpu.VMEM((8, 2 * f), F32), pltpu.VMEM((tb, DOWN_TILES * ft), _MXU)],
        operands=(h, g, w_up, conv_w, conv_b, w_down))


def _ffn_bwd(dh3, h, hh, hc, g, w_up, conv_w, w_down, hosted=None):
    t, d = h.shape
    f = w_down.shape[0]
    ft = FF_TILE
    tb = _token_block(t, 256)
    nb = t // tb

    def body(dh3_ref, h_ref, hh_ref, hc_ref, g_ref, wup_ref, cw_ref, wdown_ref,
             dh_ref, dhh_ref, act_ref, xn_ref, gcw_ref, gcb_ref, gg_ref, dcarry):
        @pl.when(pl.program_id(0) == 0)
        def _():
            for r in (gcw_ref, gcb_ref, gg_ref, dcarry):
                r[...] = jnp.zeros(r.shape, F32)

        dh3v = dh3_ref[...]
        dhm = dh3v.astype(_MXU)
        dxn = jnp.zeros((tb, d), F32)
        dact_next = _nt(dhm, wdown_ref[0:ft, :])
        for j in range(f // ft):
            dact = dact_next
            if j + 1 < f // ft:
                dact_next = _nt(dhm, wdown_ref[(j + 1) * ft:(j + 2) * ft, :])
            gate = hc_ref[:, j * ft:(j + 1) * ft].astype(F32)
            val = hc_ref[:, f + j * ft:f + (j + 1) * ft].astype(F32)
            sg = jax.nn.sigmoid(gate)
            silu = gate * sg
            act_ref[:, j * ft:(j + 1) * ft] = (silu * val).astype(_MXU)
            dhc = (dact * val * sg * (1.0 + gate * (1.0 - sg)), dact * silu)
            for part, off in enumerate((j * ft, f + j * ft)):
                cols = slice(off, off + ft)
                dc = dhc[part]
                c0 = hh_ref[:, cols].astype(F32)
                after = dcarry[:, cols]
                ahead1 = _shift_rows(dc, -1, after)
                ahead2 = _shift_rows(dc, -2, after)
                dcarry[:, cols] = dc[0:8, :]
                gcb_ref[:, cols] += jnp.sum(dc, axis=0, keepdims=True)
                gcw_ref[0:1, cols] += jnp.sum(ahead2 * c0, axis=0, keepdims=True)
                gcw_ref[1:2, cols] += jnp.sum(ahead1 * c0, axis=0, keepdims=True)
                gcw_ref[2:3, cols] += jnp.sum(dc * c0, axis=0, keepdims=True)
                dhh = (cw_ref[2:3, cols] * dc + cw_ref[1:2, cols] * ahead1 + cw_ref[0:1, cols] * ahead2).astype(_MXU)
                dhh_ref[:, cols] = dhh
                dxn = dxn + _nt(dhh, wup_ref[:, cols])
        xhat, r = _rms(h_ref[...])
        xn_ref[...] = (xhat * g_ref[...]).astype(_MXU)
        dx, gg = _rms_bwd(dxn, xhat, r, g_ref[...])
        gg_ref[...] += gg
        dh_ref[...] = dh3v + dx

    rev = lambda w: pl.BlockSpec((tb, w), lambda i: (nb - 1 - i, 0))
    return _call_hosting(
        body, hosted, name="ffn_bwd", steps=nb,
        in_specs=[rev(d), rev(d), rev(2 * f), rev(2 * f), VM, VM, VM, VM],
        out_specs=[rev(d), rev(2 * f), rev(f), rev(d), _const_spec((3, 2 * f)), _const_spec((1, 2 * f)), _const_spec((1, d))],
        out_shape=[jax.ShapeDtypeStruct((t, d), F32), jax.ShapeDtypeStruct((t, 2 * f), _MXU), jax.ShapeDtypeStruct((t, f), _MXU),
                   jax.ShapeDtypeStruct((t, d), _MXU),
                   jax.ShapeDtypeStruct((3, 2 * f), F32), jax.ShapeDtypeStruct((1, 2 * f), F32), jax.ShapeDtypeStruct((1, d), F32)],
        scratch_shapes=[pltpu.VMEM((8, 2 * f), F32)],
        operands=(dh3, h, hh, hc, g, w_up, conv_w, w_down))


def _loss_head(h, g, target):
    t, d = h.shape
    tb = _token_block(t, 512)

    def body(h_ref, g_ref, tgt_ref, dh_ref, loss_ref, gg_ref):
        @pl.when(pl.program_id(0) == 0)
        def _():
            loss_ref[...] = jnp.zeros(loss_ref.shape, F32)
            gg_ref[...] = jnp.zeros(gg_ref.shape, F32)

        xhat, r = _rms(h_ref[...])
        err = xhat * g_ref[...] - tgt_ref[...]
        loss_ref[...] += 0.5 * jnp.sum(jnp.sum(err * err, axis=-1, keepdims=True), axis=0, keepdims=True) / d
        dx, gg = _rms_bwd(err / d, xhat, r, g_ref[...])
        gg_ref[...] += gg
        dh_ref[...] = dx

    blk = pl.BlockSpec((tb, d), lambda i: (i, 0))
    return pl.pallas_call(
        body, name="loss_head", grid=(t // tb,),
        in_specs=[blk, VM, blk], out_specs=[blk, _const_spec((1, 1)), _const_spec((1, d))],
        out_shape=[jax.ShapeDtypeStruct((t, d), F32), jax.ShapeDtypeStruct((1, 1), F32), jax.ShapeDtypeStruct((1, d), F32)],
        compiler_params=_params(("arbitrary",)),
    )(h, g, target)


def _largest_tile(n, cap, mult=128):
    best = None
    for c in range(mult, min(n, cap) + 1, mult):
        if n % c == 0:
            best = c
    return best if best is not None else n


def _grad_matmul(a, b, name, layer, n_layers, into=None, hosted=None):
    t, m = a.shape
    n = b.shape[1]
    tm, tn, tk = _largest_tile(m, 1408), _largest_tile(n, 1408), _largest_tile(t, 1024)
    nk = t // tk

    def body(a_ref, b_ref, *rest):
        o_ref = rest[-1]

        @pl.when(pl.program_id(2) == 0)
        def _():
            o_ref[...] = jnp.zeros(o_ref.shape, F32)

        o_ref[...] += _tn(a_ref[...].astype(_MXU), b_ref[...].astype(_MXU))

    in_specs = [pl.BlockSpec((tk, tm), lambda i, j, k: (k, i)), pl.BlockSpec((tk, tn), lambda i, j, k: (k, j))]
    operands = (a, b)
    aliases = {}
    if into is not None:
        in_specs.append(pl.BlockSpec(memory_space=pl.ANY))
        operands = (a, b, into)
        aliases = {2: 0}
    (out,), got = _call_hosting(
        body, hosted, name=name, steps=(m // tm, n // tn, nk), in_specs=in_specs,
        out_specs=[pl.BlockSpec((None, tm, tn), lambda i, j, k: (layer, i, j))],
        out_shape=[jax.ShapeDtypeStruct((n_layers, m, n), F32)], scratch_shapes=[], operands=operands, aliases=aliases)
    return out, got


def _adamw_math(w, g, m, v):
    m = ADAM_B1 * m + (1.0 - ADAM_B1) * g
    v = ADAM_B2 * v + (1.0 - ADAM_B2) * (g * g)
    m_hat = m / (1.0 - ADAM_B1 ** ADAM_STEP)
    v_hat = v / (1.0 - ADAM_B2 ** ADAM_STEP)
    return -ADAM_LR * (m_hat / (jnp.sqrt(v_hat) + ADAM_EPS) + ADAM_WD * w), m, v


def _row_block(rows, cols, max_bytes=1 << 20, mult=16):
    best = None
    for r in range(mult, rows + 1, mult):
        if rows % r == 0 and r * cols * 4 <= max_bytes:
            best = r
    return best if best is not None else rows


def _adamw_big(w, g, m, v, name):
    shape = w.shape
    cols = shape[-1]
    flat = lambda a: a.reshape(-1, cols)
    rows = flat(w).shape[0]
    rb = _row_block(rows, cols, 2 << 20)

    def body(w_ref, g_ref, m_ref, v_ref, go_ref, d_ref, nm_ref, nv_ref):
        g = g_ref[...]
        go_ref[...] = g
        d_ref[...], nm_ref[...], nv_ref[...] = _adamw_math(w_ref[...], g, m_ref[...], v_ref[...])

    blk = pl.BlockSpec((rb, cols), lambda i: (i, 0))
    outs = pl.pallas_call(
        body, name=name, grid=(rows // rb,), in_specs=[blk] * 4, out_specs=[blk] * 4,
        out_shape=[jax.ShapeDtypeStruct((rows, cols), F32)] * 4, compiler_params=_params(("parallel",)),
    )(flat(w), flat(g), flat(m), flat(v))
    return [o.reshape(shape) for o in outs]


def _adamw_small(ws, gs, ms, vs):
    n = len(ws)

    def body(*refs):
        for a in range(n):
            w_ref, g_ref, m_ref, v_ref = (refs[s * n + a] for s in range(4))
            d_ref, nm_ref, nv_ref = (refs[(4 + s) * n + a] for s in range(3))
            d_ref[...], nm_ref[...], nv_ref[...] = _adamw_math(w_ref[...], g_ref[...], m_ref[...], v_ref[...])

    outs = pl.pallas_call(
        body, name="adamw_small", in_specs=[VM] * (4 * n), out_specs=[VM] * (3 * n),
        out_shape=[jax.ShapeDtypeStruct(w.shape, F32) for w in ws] * 3, compiler_params=_params(),
    )(*ws, *gs, *ms, *vs)
    return outs[:n], outs[n:2 * n], outs[2 * n:]


def _place():
    x, y, c = lax.axis_index("x"), lax.axis_index("y"), lax.axis_index("c")
    chips = [(1 - x, y), (x, 1 - y), (1 - x, 1 - y)]
    return x, y, c, chips


def _rows(start, size, mult=16):
    return pl.ds(pl.multiple_of(start, mult), size)


def _full_window(ref, axis, chip, half=None):
    r, c = ref.shape
    if axis == 0:
        rs = r // 4
        if half is None:
            return ref.at[_rows(chip * rs, rs), :]
        return ref.at[_rows(chip * rs + half * (rs // 2), rs // 2), :]
    cs = c // 4
    if half is None:
        return ref.at[:, _rows(chip * cs, cs, 128)]
    return ref.at[_rows(half * (r // 2), r // 2), _rows(chip * cs, cs, 128)]


def _remote(src, dst, send_sem, recv_sem, to):
    return pltpu.make_async_remote_copy(src_ref=src, dst_ref=dst, send_sem=send_sem, recv_sem=recv_sem,
                                        device_id=to, device_id_type=MESH)


def _scalars(*vals):
    return jnp.stack([jnp.asarray(v, jnp.int32) for v in vals])


def _cast_place(shard, layer, axis, chip, name):
    _, rs, cs = shard.shape
    full = (rs * 4, cs) if axis == 0 else (rs, cs * 4)
    rb = _row_block(rs, cs, 4 << 20)
    nrb = rs // rb

    def body(chip_ref, s_ref, o_ref):
        o_ref[...] = s_ref[...].astype(_PAY)

    if axis == 0:
        out_map = lambda i, chip_ref: (chip_ref[0] * nrb + i, 0)
    else:
        out_map = lambda i, chip_ref: (i, chip_ref[0])
    return pl.pallas_call(
        body, name=name,
        grid_spec=pltpu.PrefetchScalarGridSpec(
            num_scalar_prefetch=1, grid=(nrb,),
            in_specs=[pl.BlockSpec((None, rb, cs), lambda i, chip_ref: (layer, i, 0))],
            out_specs=pl.BlockSpec((rb, cs), out_map)),
        out_shape=jax.ShapeDtypeStruct(full, _PAY), compiler_params=_params(("parallel",)),
    )(_scalars(chip), shard)


def _hosted_allgather(placed, axes):
    n = len(placed)

    def each(outs, half_of):
        x, y, c, chips = _place()
        for i in range(n):
            for k, chip in enumerate(chips):
                yield i * 3 + k, (*chip, c), (x, y, 1 - c), _full_window(outs[i], axes[i], 2 * x + y, c), \
                    _full_window(outs[i], axes[i], 2 * chip[0] + chip[1], half_of(c))

    def start(_, outs, sems):
        send, recv, _, _ = sems
        for s, peer, _, mine, _ in each(outs, lambda c: c):
            _remote(mine, mine, send.at[s], recv.at[s], peer).start()

    def middle(_, outs, sems):
        send, recv, fsend, frecv = sems
        for s, _, sibling, _, got in each(outs, lambda c: c):
            _remote(got, got, send.at[s], recv.at[s], sibling).wait_recv()
            _remote(got, got, fsend.at[s], frecv.at[s], sibling).start()

    def finish(_, outs, sems):
        send, recv, fsend, frecv = sems
        for s, _, sibling, _, got in each(outs, lambda c: 1 - c):
            _remote(got, got, fsend.at[s], frecv.at[s], sibling).wait_recv()
        for s, peer, sibling, mine, got in each(outs, lambda c: c):
            _remote(mine, mine, send.at[s], recv.at[s], peer).wait_send()
            _remote(got, got, fsend.at[s], frecv.at[s], sibling).wait_send()

    return _Hosted(tuple(placed), True, (), (pltpu.SemaphoreType.DMA((n * 3,)),) * 4, (start, middle, finish))


def _allgather_conv(conv_shard):
    nl, taps, cs = conv_shard.shape

    def body(in_ref, out_ref, send, recv, local):
        x, y, c, chips = _place()
        mine = out_ref.at[:, :, _rows((2 * x + y) * cs, cs, 128)]
        own = pltpu.make_async_copy(in_ref, mine, local)
        own.start()
        sends = [_remote(in_ref, mine, send.at[k], recv.at[k], (*chip, c)) for k, chip in enumerate(chips)]
        for cp in sends:
            cp.start()
        for k, chip in enumerate(chips):
            got = out_ref.at[:, :, _rows((2 * chip[0] + chip[1]) * cs, cs, 128)]
            _remote(got, got, send.at[k], recv.at[k], (*chip, c)).wait_recv()
        for cp in sends:
            cp.wait_send()
        own.wait()

    return pl.pallas_call(
        body, name="allgather_conv", in_specs=[HB], out_specs=HB, out_shape=jax.ShapeDtypeStruct((nl, taps, cs * 4), conv_shard.dtype),
        scratch_shapes=[pltpu.SemaphoreType.DMA((3,)), pltpu.SemaphoreType.DMA((3,)), pltpu.SemaphoreType.DMA],
        compiler_params=pltpu.CompilerParams(has_side_effects=True),
    )(conv_shard)


def _hosted_exchange(grads, axes, layer):
    na = len(grads)
    views = [g.reshape(g.shape[0], 4, 2, g.shape[1] // 8, g.shape[2]) if ax == 0 else g for g, ax in zip(grads, axes)]

    def region(ref, axis, half):
        if axis == 0:
            return ref.at[layer, :, half]
        r = ref.shape[1]
        return ref.at[layer, _rows(half * (r // 2), r // 2), :]

    def copies(ins, land, sems):
        send, recv = sems
        x, y, c, _ = _place()
        return [_remote(region(ins[a], axes[a], 1 - c), land[a], send.at[a], recv.at[a], (x, y, 1 - c)) for a in range(na)]

    def start(ins, land, sems):
        for cp in copies(ins, land, sems):
            cp.start()

    def finish(ins, land, sems):
        for cp in copies(ins, land, sems):
            cp.wait()

    shapes = [(4, g.shape[1] // 8, g.shape[2]) if ax == 0 else (g.shape[1] // 2, g.shape[2]) for g, ax in zip(grads, axes)]
    return _Hosted(tuple(views), False, tuple(jax.ShapeDtypeStruct(s, F32) for s in shapes),
                   (pltpu.SemaphoreType.DMA((na,)),) * 2, (start, None, finish))


def _add_cast(mine, theirs, core, base, name):
    na, nb, cols = theirs.shape
    rb = _row_block(nb, cols, 4 << 20)

    def body(core_ref, a_ref, b_ref, o_ref):
        o_ref[...] = (a_ref[...] + b_ref[...]).astype(_PAY)

    blk = pl.BlockSpec((None, rb, cols), lambda i, k, core_ref: (i, k, 0))
    return pl.pallas_call(
        body, name=name,
        grid_spec=pltpu.PrefetchScalarGridSpec(
            num_scalar_prefetch=1, grid=(na, nb // rb),
            in_specs=[pl.BlockSpec((None, None, rb, cols), lambda i, k, core_ref: (base + i, core_ref[0], k, 0)), blk], out_specs=blk),
        out_shape=jax.ShapeDtypeStruct((na, nb, cols), _PAY), compiler_params=_params(("parallel", "parallel")),
    )(_scalars(core), mine, theirs)


def _piece(ref, axis, chip):
    if axis == 0:
        return ref.at[chip]
    cs = ref.shape[1] // 4
    return ref.at[:, _rows(chip * cs, cs, 128)]


def _hosted_scatter(sums, axes):
    na = len(sums)

    def piece_shape(a):
        if axes[a] == 0:
            return (sums[a].shape[1], sums[a].shape[2])
        return (sums[a].shape[0], sums[a].shape[1] // 4)

    def copies(ins, slots, sems):
        send, recv = sems
        _, _, c, chips = _place()
        return [_remote(_piece(ins[a], axes[a], 2 * chip[0] + chip[1]), slots[a].at[k], send.at[a * 3 + k], recv.at[a * 3 + k], (*chip, c))
                for a in range(na) for k, chip in enumerate(chips)]

    def start(ins, slots, sems):
        for cp in copies(ins, slots, sems):
            cp.start()

    def finish(ins, slots, sems):
        for cp in copies(ins, slots, sems):
            cp.wait()

    return _Hosted(tuple(sums), False, tuple(jax.ShapeDtypeStruct((3,) + piece_shape(a), sums[a].dtype) for a in range(na)),
                   (pltpu.SemaphoreType.DMA((na * 3,)),) * 2, (start, None, finish))


def _sum_slots(sums, slots, axis, chip, core, layer, n_layers, name, into=None):
    _, hr, cs = slots.shape
    rb = _row_block(hr, cs, 4 << 20)

    def body(at_ref, own_ref, s_ref, *rest):
        rest[-1][...] = ((own_ref[...].astype(F32) + s_ref[0].astype(F32)) + s_ref[1].astype(F32)) + s_ref[2].astype(F32)

    if axis == 0:
        own = pl.BlockSpec((None, rb, cs), lambda k, at_ref: (at_ref[0], k, 0))
    else:
        own = pl.BlockSpec((rb, cs), lambda k, at_ref: (k, at_ref[0]))
    in_specs = [own, pl.BlockSpec((3, rb, cs), lambda k, at_ref: (0, k, 0))]
    operands = (sums, slots)
    aliases = {}
    if into is not None:
        in_specs.append(pl.BlockSpec(memory_space=pl.ANY))
        operands = (sums, slots, into)
        aliases = {3: 0}
    return pl.pallas_call(
        body, name=name,
        grid_spec=pltpu.PrefetchScalarGridSpec(
            num_scalar_prefetch=1, grid=(hr // rb,), in_specs=in_specs,
            out_specs=pl.BlockSpec((None, None, rb, cs), lambda k, at_ref: (layer, at_ref[1], k, 0))),
        out_shape=jax.ShapeDtypeStruct((n_layers, 2, hr, cs), F32), input_output_aliases=aliases,
        compiler_params=_params(("parallel",)),
    )(_scalars(chip, core), *operands)


def _sibling_assemble(shards):
    na = len(shards)

    def body(*refs):
        outs = refs[na:2 * na]
        send, recv = refs[2 * na:]
        x, y, c, _ = _place()
        copies = []
        for a in range(na):
            hr = outs[a].shape[1] // 2
            mine = outs[a].at[:, _rows(c * hr, hr), :]
            cp = _remote(mine, mine, send.at[a], recv.at[a], (x, y, 1 - c))
            cp.start()
            copies.append(cp)
        for cp in copies:
            cp.wait()

    return pl.pallas_call(
        body, name="grad_sibling_assemble", in_specs=[HB] * na, out_specs=[HB] * na,
        out_shape=[jax.ShapeDtypeStruct(s.shape, F32) for s in shards], input_output_aliases={a: a for a in range(na)},
        scratch_shapes=[pltpu.SemaphoreType.DMA((na,))] * 2,
        compiler_params=pltpu.CompilerParams(has_side_effects=True),
    )(*shards)


def _allreduce_small(buf):
    rows, w = buf.shape
    half = rows // 2

    def body(buf_ref, out_ref, land, slots, red, sems_send, sems_recv):
        x, y, c, chips = _place()
        me = 2 * x + y
        sibling = (x, y, 1 - c)
        first = _remote(buf_ref, land, sems_send.at[0], sems_recv.at[0], sibling)
        first.start()
        first.wait()
        mine = pl.ds(pl.multiple_of(c * half, 8), half)
        slots[me] = buf_ref[mine, :] + land[mine, :]
        sends = []
        for k, chip in enumerate(chips):
            cp = _remote(slots.at[me], slots.at[me], sems_send.at[1 + k], sems_recv.at[1 + k], (*chip, c))
            cp.start()
            sends.append(cp)
        for k, chip in enumerate(chips):
            got = slots.at[2 * chip[0] + chip[1]]
            _remote(got, got, sems_send.at[1 + k], sems_recv.at[1 + k], sibling).wait_recv()
        red[...] = ((slots[0] + slots[1]) + slots[2]) + slots[3]
        out_ref[mine, :] = red[...]
        last = _remote(red, out_ref.at[mine, :], sems_send.at[4], sems_recv.at[4], sibling)
        last.start()
        theirs = out_ref.at[pl.ds(pl.multiple_of((1 - c) * half, 8), half), :]
        _remote(red, theirs, sems_send.at[4], sems_recv.at[4], sibling).wait_recv()
        for cp in sends:
            cp.wait_send()
        last.wait_send()

    return pl.pallas_call(
        body, name="allreduce_small", in_specs=[VM], out_specs=VM, out_shape=jax.ShapeDtypeStruct((rows, w), F32),
        scratch_shapes=[pltpu.VMEM((rows, w), F32), pltpu.VMEM((4, half, w), F32), pltpu.VMEM((half, w), F32),
                        pltpu.SemaphoreType.DMA((5,)), pltpu.SemaphoreType.DMA((5,))],
        compiler_params=pltpu.CompilerParams(has_side_effects=True, vmem_limit_bytes=VMEM_LIMIT),
    )(buf)


BIG = ("w_in", "w_out", "wq", "wk", "wv", "wo", "w_up", "w_down")
MIXER, ATTN, MLP = ("w_in", "w_out"), ("wq", "wk", "wv", "wo"), ("w_up", "w_down")
BIG_AXIS = {"w_in": 1, "w_out": 0, "wq": 0, "wk": 0, "wv": 0, "wo": 0, "w_up": 1, "w_down": 0}
SMALL = ("norm_mix_g", "pool_w", "pool_scale", "sgu_g", "sgu_w", "sgu_b", "norm_xattn_g", "mem_norm_g", "norm_ffn_g",
         "conv_w", "conv_b", "final_norm_g")
ORDER = ("norm_mix_g", "w_in", "pool_w", "pool_scale", "sgu_g", "sgu_w", "sgu_b", "w_out", "norm_xattn_g", "mem_norm_g",
         "wq", "wk", "wv", "wo", "norm_ffn_g", "w_up", "conv_w", "conv_b", "w_down", "final_norm_g")
PACK_WIDTH = 512


def kernel(x, mem, norm_mix_g, w_in, pool_w, pool_scale, sgu_g, sgu_w, sgu_b, w_out, norm_xattn_g, mem_norm_g, wq, wk, wv, wo, norm_ffn_g, w_up, conv_w, conv_b, w_down, final_norm_g, loss_target, m_norm_mix_g, m_w_in, m_pool_w, m_pool_scale, m_sgu_g, m_sgu_w, m_sgu_b, m_w_out, m_norm_xattn_g, m_mem_norm_g, m_wq, m_wk, m_wv, m_wo, m_norm_ffn_g, m_w_up, m_conv_w, m_conv_b, m_w_down, m_final_norm_g, v_norm_mix_g, v_w_in, v_pool_w, v_pool_scale, v_sgu_g, v_sgu_w, v_sgu_b, v_w_out, v_norm_xattn_g, v_mem_norm_g, v_wq, v_wk, v_wv, v_wo, v_norm_ffn_g, v_w_up, v_conv_w, v_conv_b, v_w_down, v_final_norm_g):
    given = dict(locals())
    w = {n: given[n] for n in ORDER}
    mom = {n: given["m_" + n] for n in ORDER}
    var = {n: given["v_" + n] for n in ORDER}
    nl = w_in.shape[0]
    xs, mems, tgt = x[0], mem[0], loss_target[0]
    chip = 2 * lax.axis_index("x") + lax.axis_index("y")
    core = lax.axis_index("c")

    axes_of = lambda names: [BIG_AXIS[n] for n in names]
    placed = [{n: _cast_place(w[n], l, BIG_AXIS[n], chip, f"place_{n}_{l}") for n in BIG} for l in range(nl)]
    conv_full = _allgather_conv(conv_w)

    def gather(names, l):
        return _hosted_allgather([placed[l][n] for n in names], axes_of(names))

    full = [dict(zip(MIXER, _run_hosted(gather(MIXER, 0), "allgather_weights")))]

    row = lambda a, l: a[l][None, :]
    saved = []
    h = xs
    for l in range(nl):
        fw = full[l]
        sbt = jnp.broadcast_to(sgu_b[l][:, :, None], sgu_w[l].shape)
        (h1, proj, xn1, mix), got = _mixer_fwd(h, row(norm_mix_g, l), fw["w_in"], pool_w[l], row(pool_scale, l), row(sgu_g, l), sgu_w[l], sbt, fw["w_out"],
                                               [gather(ATTN, 0), gather(("w_up",), 0)] if l == 0 else None)
        if l == 0:
            fw.update(zip(ATTN, got[0]))
            fw["w_up"] = got[1][0]
        k, v, memn = _kv_fwd(mems, row(mem_norm_g, l), fw["wk"], fw["wv"])
        (h2, q, o, xn2), got = _xattn_fwd(h1, row(norm_xattn_g, l), fw["wq"], k, v, fw["wo"], [gather(("w_down",), 0)] if l == 0 else None)
        if l == 0:
            fw["w_down"] = got[0][0]
        (h3, hh, hc), got = _ffn_fwd(h2, row(norm_ffn_g, l), fw["w_up"], conv_full[l], row(conv_b, l), fw["w_down"],
                                      [gather(BIG, l + 1)] if l + 1 < nl else None)
        if l + 1 < nl:
            full.append(dict(zip(BIG, got[0])))
        saved.append(dict(h=h, h1=h1, h2=h2, proj=proj, xn1=xn1, mix=mix, k=k, v=v, memn=memn, q=q, o=o, xn2=xn2, hh=hh, hc=hc, sbt=sbt))
        h = h3

    dh, loss_part, g_final = _loss_head(h, final_norm_g[None, :], tgt)

    big_grads = {}
    small_grads = [None] * nl

    def weight_grad(n, a, b, l, hosted=None):
        big_grads[n], got = _grad_matmul(a, b, "grad_" + n, l, nl, big_grads.get(n), hosted)
        return got

    sums, slots = {}, {}

    def exchange(names, l):
        return _hosted_exchange([big_grads[n] for n in names], axes_of(names), l)

    def scatter(names, l):
        return _hosted_scatter([sums[n, l] for n in names], axes_of(names))

    def add_casts(names, theirs, l):
        for n, t in zip(names, theirs):
            g = big_grads[n]
            gl, gr, gc = g.shape
            if BIG_AXIS[n] == 0:
                sums[n, l] = _add_cast(g.reshape(gl * 4, 2, gr // 8, gc), t, core, l * 4, "grad_chip_sum_" + n)
            else:
                sums[n, l] = _add_cast(g.reshape(gl, 2, gr // 2, gc), t[None], core, l, "grad_chip_sum_" + n)[0]

    def keep_slots(names, got, l):
        for n, sl in zip(names, got):
            slots[n, l] = sl

    for l in reversed(range(nl)):
        fw, s = full[l], saved[l]
        above = l + 1 < nl
        dh3 = dh
        (dh2, dhh, act, xn3, g_cw, g_cb, g_nf), got = _ffn_bwd(dh3, s["h2"], s["hh"], s["hc"], row(norm_ffn_g, l), fw["w_up"], conv_full[l], fw["w_down"],
                                                         [exchange(MIXER, l + 1), scatter(ATTN, l + 1)] if above else None)
        if above:
            add_casts(MIXER, got[0], l + 1)
            keep_slots(ATTN, got[1], l + 1)
        weight_grad("w_up", xn3, dhh, l)
        weight_grad("w_down", act, dh3, l)
        (dh1, dq, dk, dv, g_nx), got = _xattn_bwd(dh2, s["h1"], s["q"], row(norm_xattn_g, l), fw["wq"], s["k"], s["v"], fw["wo"],
                                                  [exchange(MLP, l), scatter(MIXER, l + 1) if above else None])
        add_casts(MLP, got[0], l)
        if above:
            keep_slots(MIXER, got[1], l + 1)
        weight_grad("wq", s["xn2"], dq, l)
        weight_grad("wo", s["o"], dh2, l)
        weight_grad("wk", s["memn"], dk, l)
        weight_grad("wv", s["memn"], dv, l)
        g_mn = _kv_bwd(dk, dv, mems, fw["wk"], fw["wv"])
        (dh0, dproj, g_nm, g_pw, g_ps, g_sg, g_sw, g_sbt), got = _mixer_bwd(dh1, s["h"], s["proj"], row(norm_mix_g, l), fw["w_in"], pool_w[l], row(pool_scale, l), row(sgu_g, l), sgu_w[l], s["sbt"], fw["w_out"],
                                                                           [scatter(MLP, l), exchange(ATTN, l)])
        keep_slots(MLP, got[0], l)
        add_casts(ATTN, got[1], l)
        got = weight_grad("w_in", s["xn1"], dproj, l, [scatter(ATTN, l)] if l == 0 else None)
        if l == 0:
            keep_slots(ATTN, got[0], l)
        weight_grad("w_out", s["mix"], dh1, l)
        small_grads[l] = dict(norm_mix_g=g_nm, pool_w=g_pw, pool_scale=g_ps, sgu_g=g_sg, sgu_w=g_sw, sgu_b=jnp.sum(g_sbt, axis=-1),
                              norm_xattn_g=g_nx, mem_norm_g=g_mn, norm_ffn_g=g_nf, conv_w=g_cw, conv_b=g_cb)
        dh = dh0
    grad_x = dh[None]

    add_casts(MIXER, _run_hosted(exchange(MIXER, 0), "grad_sibling_exchange"), 0)
    keep_slots(MIXER, _run_hosted(scatter(MIXER, 0), "grad_chip_scatter"), 0)
    halves = []
    for n in BIG:
        buf = None
        for l in range(nl):
            buf = _sum_slots(sums[n, l], slots[n, l], BIG_AXIS[n], chip, core, l, nl, "grad_sum_" + n, buf)
        halves.append(buf.reshape(nl, 2 * buf.shape[2], buf.shape[3]))
    shard_grads = dict(zip(BIG, _sibling_assemble(halves)))

    layered = [n for n in SMALL if n != "final_norm_g"]
    parts = [small_grads[l][n].reshape(-1, PACK_WIDTH) for n in layered for l in range(nl)]
    parts.append(g_final.reshape(-1, PACK_WIDTH))
    parts.append(jnp.pad(loss_part, ((0, 0), (0, PACK_WIDTH - 1))))
    used = sum(p.shape[0] for p in parts)
    total = -(-used // 16) * 16
    packed = _allreduce_small(jnp.concatenate(parts + [jnp.zeros((total - used, PACK_WIDTH), F32)], axis=0))
    grads = dict(shard_grads)
    at = 0
    for n in layered:
        per_layer = []
        for l in range(nl):
            shape = small_grads[l][n].shape
            nrow = small_grads[l][n].size // PACK_WIDTH
            per_layer.append(packed[at:at + nrow].reshape(shape))
            at += nrow
        g = jnp.stack(per_layer)
        if n == "conv_w":
            cs = conv_w.shape[2]
            g = lax.dynamic_slice_in_dim(g, chip * cs, cs, axis=2)
        grads[n] = g.reshape(w[n].shape)
    grads["final_norm_g"] = packed[at:at + g_final.size // PACK_WIDTH].reshape(final_norm_g.shape)
    at += g_final.size // PACK_WIDTH
    loss = packed[at, 0]

    delta, new_m, new_v = {}, {}, {}
    for n in BIG:
        grads[n], delta[n], new_m[n], new_v[n] = _adamw_big(w[n], grads[n], mom[n], var[n], "adamw_" + n)
    two_d = lambda a: a.reshape(-1, a.shape[-1])
    ds, nms, nvs = _adamw_small([two_d(w[n]) for n in SMALL], [two_d(grads[n]) for n in SMALL],
                                [two_d(mom[n]) for n in SMALL], [two_d(var[n]) for n in SMALL])
    for n, d_, m_, v_ in zip(SMALL, ds, nms, nvs):
        delta[n], new_m[n], new_v[n] = d_.reshape(w[n].shape), m_.reshape(w[n].shape), v_.reshape(w[n].shape)

    return (loss, grad_x, *[grads[n] for n in ORDER], *[delta[n] for n in ORDER], *[new_m[n] for n in ORDER], *[new_v[n] for n in ORDER])
```

```python
import math
from typing import NamedTuple

import jax
import jax.numpy as jnp
from jax import lax
from jax.experimental import pallas as pl
from jax.experimental.pallas import tpu as pltpu

F32 = jnp.float32
_MXU = jnp.bfloat16
_PAY = jnp.bfloat16
EPS = 1e-6
WINDOWS = (2, 4, 8, 16)
GROUP = 128
N_XHEADS = 4
HALO = 16
FF_TILE = 256
DOWN_TILES = 4
VMEM_LIMIT = 60 * 1024 * 1024
MESH = pl.DeviceIdType.MESH

ADAM_LR, ADAM_B1, ADAM_B2, ADAM_EPS, ADAM_WD, ADAM_STEP = 0.001, 0.9, 0.999, 1e-08, 0.01, 10

VM = pl.BlockSpec(memory_space=pltpu.VMEM)
HB = pl.BlockSpec(memory_space=pltpu.HBM)


def _nn(a, b):
    return jnp.dot(a, b, preferred_element_type=F32)


def _nt(a, b):
    return lax.dot_general(a, b, (((1,), (1,)), ((), ())), preferred_element_type=F32)


def _tn(a, b):
    return lax.dot_general(a, b, (((0,), (0,)), ((), ())), preferred_element_type=F32)


def _rms(x):
    r = lax.rsqrt(jnp.mean(x * x, axis=-1, keepdims=True) + EPS)
    return x * r, r


def _rms_bwd(dxn, xhat, r, g):
    dxh = dxn * g
    dx = r * (dxh - xhat * jnp.mean(dxh * xhat, axis=-1, keepdims=True))
    return dx, jnp.sum(dxn * xhat, axis=0, keepdims=True)


def _gelu(x):
    cdf = 0.5 * (1.0 + lax.erf(x * (2.0 ** -0.5)))
    return x * cdf, cdf


def _gelu_grad(x, cdf):
    return cdf + x * jnp.exp(-0.5 * x * x) * ((2.0 * math.pi) ** -0.5)


def _params(sem=None):
    return pltpu.CompilerParams(dimension_semantics=sem, vmem_limit_bytes=VMEM_LIMIT)


def _token_block(t, want):
    return want if t % want == 0 and t > want else GROUP


def _const_spec(shape):
    n = len(shape)
    return pl.BlockSpec(shape, lambda i: (0,) * n)


def _tril():
    return lax.broadcasted_iota(jnp.int32, (GROUP, GROUP), 0) >= lax.broadcasted_iota(jnp.int32, (GROUP, GROUP), 1)


def _shift_rows(x, k, edge):
    tb = x.shape[0]
    r8 = lax.broadcasted_iota(jnp.int32, (8, 1), 0)
    rolled = pltpu.roll(x, k % tb, 0)
    if k > 0:
        top = jnp.where(r8 < k, pltpu.roll(edge, k, 0), rolled[0:8, :])
        return jnp.concatenate([top, rolled[8:, :]], axis=0)
    bottom = jnp.where(r8 >= 8 + k, pltpu.roll(edge, 8 + k, 0), rolled[tb - 8:, :])
    return jnp.concatenate([rolled[:tb - 8, :], bottom], axis=0)


def _in_turns(parts):
    parts = list(parts)
    while parts:
        for p in list(parts):
            try:
                next(p)
            except StopIteration:
                parts.remove(p)


class _Hosted(NamedTuple):
    operands: tuple
    aliased: bool
    out_shapes: tuple
    sems: tuple
    stages: tuple


def _hosted_results(hosted):
    if hosted.aliased:
        return [jax.ShapeDtypeStruct(o.shape, o.dtype) for o in hosted.operands]
    return list(hosted.out_shapes)


def _call_hosting(main_body, hosted, *, name, steps, in_specs, out_specs, out_shape, scratch_shapes, operands, aliases=None):
    grid = steps if isinstance(steps, tuple) else (steps,)
    semantics = ("arbitrary",) * len(grid)
    hosted = [hs for hs in (hosted or ()) if hs is not None]
    if not hosted:
        outs = pl.pallas_call(main_body, name=name, grid=grid, in_specs=in_specs, out_specs=out_specs, out_shape=out_shape,
                              scratch_shapes=scratch_shapes, input_output_aliases=aliases or {}, compiler_params=_params(semantics))(*operands)
        return outs, ()
    n_in, n_out, n_sc = len(in_specs), len(out_specs), len(scratch_shapes)
    shapes = [_hosted_results(hs) for hs in hosted]
    aliases, in_at, out_at = dict(aliases or {}), n_in, n_out
    for hs, sh in zip(hosted, shapes):
        if hs.aliased:
            aliases.update({in_at + i: out_at + i for i in range(len(hs.operands))})
        in_at += len(hs.operands)
        out_at += len(sh)

    def body(*refs):
        at = [0]

        def take(n):
            at[0] += n
            return refs[at[0] - n:at[0]]

        ins = take(n_in)
        h_in = [take(len(hs.operands)) for hs in hosted]
        outs = take(n_out)
        h_out = [take(len(sh)) for sh in shapes]
        scratch = take(n_sc)
        h_sems = [take(len(hs.sems)) for hs in hosted]
        ids = [pl.program_id(a) for a in range(len(grid))]

        def at_step(where):
            lead, rest = where
            ok = ids[0] == lead
            for a in range(1, len(grid)):
                ok = jnp.logical_and(ok, ids[a] == (grid[a] - 1 if rest else 0))
            return ok

        def run(stage):
            for hs, a, b, c in zip(hosted, h_in, h_out, h_sems):
                if hs.stages[stage] is not None:
                    hs.stages[stage](a, b, c)

        @pl.when(at_step((0, 0)))
        def _():
            run(0)

        if any(hs.stages[1] is not None for hs in hosted):
            @pl.when(at_step(((3 * grid[0]) // 4, 0)))
            def _():
                run(1)

        main_body(*ins, *outs, *scratch)

        @pl.when(at_step((grid[0] - 1, -1)))
        def _():
            run(2)

    flat = lambda lists: [x for xs in lists for x in xs]
    outs = pl.pallas_call(
        body, name=name, grid=grid, in_specs=list(in_specs) + [HB] * (in_at - n_in), out_specs=list(out_specs) + [HB] * (out_at - n_out),
        out_shape=list(out_shape) + flat(shapes), scratch_shapes=list(scratch_shapes) + flat(hs.sems for hs in hosted),
        input_output_aliases=aliases, compiler_params=_params(semantics),
    )(*operands, *flat(hs.operands for hs in hosted))
    results, at = [], n_out
    for sh in shapes:
        results.append(outs[at:at + len(sh)])
        at += len(sh)
    return outs[:n_out], results


def _run_hosted(hosted, name):
    nh = len(hosted.operands)
    h_shapes = _hosted_results(hosted)

    def body(*refs):
        h_in, h_out, h_sems = refs[:nh], refs[nh:nh + len(h_shapes)], refs[nh + len(h_shapes):]
        for stage in hosted.stages:
            if stage is not None:
                stage(h_in, h_out, h_sems)

    return pl.pallas_call(
        body, name=name, in_specs=[HB] * nh, out_specs=[HB] * len(h_shapes), out_shape=h_shapes, scratch_shapes=list(hosted.sems),
        input_output_aliases={i: i for i in range(nh)} if hosted.aliased else {},
        compiler_params=pltpu.CompilerParams(has_side_effects=True),
    )(*hosted.operands)


def _window_sums(e, win, back):
    n = e.shape[0]
    k = 1
    while k < win:
        e = e + pltpu.roll(e, k if back else n - k, 0)
        k *= 2
    return e


def _pool_diff(prev, p, t0, gi, win):
    sl = slice(gi * GROUP, (gi + 1) * GROUP)
    tb = p.shape[0]
    s = _window_sums(jnp.concatenate([prev[:, sl], p[:, sl]], axis=0), win, True)[HALO:, :]
    tglob = t0 + lax.broadcasted_iota(jnp.int32, (tb, 1), 0)
    cnt = jnp.minimum(tglob + 1, win).astype(F32)
    return s / cnt - p[:, sl], cnt


def _layernorm(v):
    xc = v - jnp.mean(v, axis=-1, keepdims=True)
    rstd = lax.rsqrt(jnp.mean(xc * xc, axis=-1, keepdims=True) + EPS)
    return xc * rstd, rstd


def _mixer_fwd(h, g, w_in, pool_w, pool_scale, sgu_g, sgu_w, sgu_bt, w_out, hosted=None):
    t, d = h.shape
    pw = pool_w.shape[0] * GROUP
    sw = sgu_w.shape[0] * GROUP
    tb = _token_block(t, 512)

    def body(h_ref, g_ref, win_ref, pw_ref, ps_ref, sg_ref, sw_ref, sbt_ref, wout_ref, h1_ref, proj_ref, xn_ref, mix_ref, pext):
        i = pl.program_id(0)

        @pl.when(i == 0)
        def _():
            pext[...] = jnp.zeros((HALO, pw), F32)

        x = h_ref[...]
        xhat, _ = _rms(x)
        xn = (xhat * g_ref[...]).astype(_MXU)
        xn_ref[...] = xn
        proj = _nn(xn, win_ref[...])
        proj_ref[...] = proj
        p = proj[:, :pw]
        prev = pext[...]
        for gi, win in enumerate(WINDOWS):
            sl = slice(gi * GROUP, (gi + 1) * GROUP)
            dg, _ = _pool_diff(prev, p, i * tb, gi, win)
            e = _nn(dg.astype(_MXU), pw_ref[gi].astype(_MXU))
            mix_ref[:, sl] = (e * ps_ref[:, sl]).astype(_MXU)
        pext[...] = p[tb - HALO:tb, :]
        uv, _ = _gelu(proj[:, pw:])
        u = uv[:, :sw]
        vhat, _ = _layernorm(uv[:, sw:])
        vn = (vhat * sg_ref[...]).astype(_MXU)
        mask = _tril()
        chunks = [slice(n * GROUP, (n + 1) * GROUP) for n in range(tb // GROUP)]
        for hh in range(sw // GROUP):
            wm = jnp.where(mask, sw_ref[hh], 0.0).astype(_MXU)
            cols = slice(hh * GROUP, (hh + 1) * GROUP)
            z = _nn(wm, jnp.concatenate([vn[rows, cols] for rows in chunks], axis=1))
            for n, rows in enumerate(chunks):
                mix_ref[rows, pw + hh * GROUP:pw + (hh + 1) * GROUP] = (u[rows, cols] * (z[:, chunks[n]] + sbt_ref[hh])).astype(_MXU)
        h1_ref[...] = x + _nn(mix_ref[...], wout_ref[...])

    blk = lambda w: pl.BlockSpec((tb, w), lambda i: (i, 0))
    return _call_hosting(
        body, hosted, name="mixer_fwd", steps=t // tb,
        in_specs=[blk(d), VM, VM, VM, VM, VM, VM, VM, VM],
        out_specs=[blk(d), blk(w_in.shape[1]), blk(d), blk(d)],
        out_shape=[jax.ShapeDtypeStruct((t, d), F32), jax.ShapeDtypeStruct((t, w_in.shape[1]), F32),
                   jax.ShapeDtypeStruct((t, d), _MXU), jax.ShapeDtypeStruct((t, d), _MXU)],
        scratch_shapes=[pltpu.VMEM((HALO, pw), F32)],
        operands=(h, g, w_in, pool_w, pool_scale, sgu_g, sgu_w, sgu_bt, w_out))


def _mixer_bwd(dh1, h, proj, g, w_in, pool_w, pool_scale, sgu_g, sgu_w, sgu_bt, w_out, hosted=None):
    t, d = h.shape
    ng, nh = pool_w.shape[0], sgu_w.shape[0]
    pw, sw = ng * GROUP, nh * GROUP
    tb = _token_block(t, 512)
    nb = t // tb
    n_parts = 2 if tb % (2 * GROUP) == 0 else 1
    pt = tb // n_parts

    def body(dh1_ref, h_ref, proj_ref, halo_ref, g_ref, win_ref, pw_ref, ps_ref, sg_ref, sw_ref, sbt_ref, wout_ref,
             dh_ref, dproj_ref, gg_ref, gpw_ref, gps_ref, gsg_ref, gsw_ref, gsbt_ref, dext, duv):
        i = pl.program_id(0)
        blk = nb - 1 - i

        @pl.when(i == 0)
        def _():
            for r in (gg_ref, gpw_ref, gps_ref, gsg_ref, gsw_ref, gsbt_ref, dext):
                r[...] = jnp.zeros(r.shape, F32)

        mask = _tril()

        def part(at):
            rows = slice(at, at + pt)
            dh1v = dh1_ref[rows, :]
            dmix = _nt(dh1v.astype(_MXU), wout_ref[...])
            yield
            proj_v = proj_ref[rows, :]
            p = proj_v[:, :pw]
            prev = jnp.where(blk == 0, 0.0, halo_ref[...]) if at == 0 else proj_ref[at - HALO:at, 0:pw]
            for gi, win in enumerate(WINDOWS):
                sl = slice(gi * GROUP, (gi + 1) * GROUP)
                dg, cnt = _pool_diff(prev, p, blk * tb + at, gi, win)
                dgm = dg.astype(_MXU)
                pwm = pw_ref[gi].astype(_MXU)
                e = _nn(dgm, pwm)
                dy = dmix[:, sl]
                gps_ref[:, sl] += jnp.sum(dy * e, axis=0, keepdims=True)
                de = (dy * ps_ref[:, sl]).astype(_MXU)
                gpw_ref[gi] += _tn(dgm, de)
                dd = _nt(de, pwm)
                ddc = dd / cnt
                acc = _window_sums(jnp.concatenate([ddc, dext[:, sl]], axis=0), win, False)[:pt, :]
                dext[:, sl] = ddc[0:HALO, :]
                dproj_ref[rows, sl] = (acc - dd).astype(_MXU)
            yield
            pre = proj_v[:, pw:]
            uv, cdf = _gelu(pre)
            u = uv[:, :sw]
            vhat, rstd = _layernorm(uv[:, sw:])
            vn = (vhat * sg_ref[...]).astype(_MXU)
            chunks = [slice(n * GROUP, (n + 1) * GROUP) for n in range(pt // GROUP)]
            side_by_side = lambda a, cols: jnp.concatenate([a[c, cols] for c in chunks], axis=1)
            for hh in range(nh):
                wm = jnp.where(mask, sw_ref[hh], 0.0).astype(_MXU)
                cols = slice(hh * GROUP, (hh + 1) * GROUP)
                vs = side_by_side(vn, cols)
                z = _nn(wm, vs)
                dy = side_by_side(dmix, slice(pw + hh * GROUP, pw + (hh + 1) * GROUP))
                dz = dy * side_by_side(u, cols)
                dzm = dz.astype(_MXU)
                dvs = _tn(wm, dzm)
                gsw_ref[hh] += jnp.where(mask, _nt(dzm, vs), 0.0)
                gb = jnp.zeros((GROUP, GROUP), F32)
                for n, c in enumerate(chunks):
                    gb = gb + dz[:, c]
                    duv[at + n * GROUP:at + (n + 1) * GROUP, cols] = dy[:, c] * (z[:, c] + sbt_ref[hh])
                    duv[at + n * GROUP:at + (n + 1) * GROUP, sw + hh * GROUP:sw + (hh + 1) * GROUP] = dvs[:, c]
                gsbt_ref[hh] += gb
            yield
            dvn = duv[rows, sw:]
            gsg_ref[...] += jnp.sum(dvn * vhat, axis=0, keepdims=True)
            dxh = dvn * sg_ref[...]
            dv = rstd * (dxh - jnp.mean(dxh, axis=-1, keepdims=True) - vhat * jnp.mean(dxh * vhat, axis=-1, keepdims=True))
            gp = _gelu_grad(pre, cdf)
            dproj_ref[rows, pw:pw + sw] = (duv[rows, :sw] * gp[:, :sw]).astype(_MXU)
            dproj_ref[rows, pw + sw:] = (dv * gp[:, sw:]).astype(_MXU)
            dxn = _nt(dproj_ref[rows, :], win_ref[...])
            yield
            xhat, r = _rms(h_ref[rows, :])
            dx, gg = _rms_bwd(dxn, xhat, r, g_ref[...])
            gg_ref[...] += gg
            dh_ref[rows, :] = dh1v + dx

        _in_turns([part(at) for at in reversed(range(0, tb, pt))])

    rev = lambda w: pl.BlockSpec((tb, w), lambda i: (nb - 1 - i, 0))
    halo = pl.BlockSpec((HALO, pw), lambda i: (jnp.maximum((nb - 1 - i) * (tb // HALO) - 1, 0), 0))
    small = [(1, d), (ng, GROUP, GROUP), (1, pw), (1, sw), (nh, GROUP, GROUP), (nh, GROUP, GROUP)]
    return _call_hosting(
        body, hosted, name="mixer_bwd", steps=nb,
        in_specs=[rev(d), rev(d), rev(proj.shape[1]), halo, VM, VM, VM, VM, VM, VM, VM, VM],
        out_specs=[rev(d), rev(proj.shape[1])] + [_const_spec(s) for s in small],
        out_shape=[jax.ShapeDtypeStruct((t, d), F32), jax.ShapeDtypeStruct(proj.shape, _MXU)]
        + [jax.ShapeDtypeStruct(s, F32) for s in small],
        scratch_shapes=[pltpu.VMEM((HALO, pw), F32), pltpu.VMEM((tb, 2 * sw), F32)],
        operands=(dh1, h, proj, proj, g, w_in, pool_w, pool_scale, sgu_g, sgu_w, sgu_bt, w_out))


def _kv_fwd(mem, gm, wk, wv):
    n, d = mem.shape

    def body(mem_ref, gm_ref, wk_ref, wv_ref, k_ref, v_ref, memn_ref):
        xhat, _ = _rms(mem_ref[...])
        memn = (xhat * gm_ref[...]).astype(_MXU)
        memn_ref[...] = memn
        k_ref[...] = _nn(memn, wk_ref[...]).astype(_MXU)
        v_ref[...] = _nn(memn, wv_ref[...]).astype(_MXU)

    return pl.pallas_call(
        body, name="kv_fwd", in_specs=[VM] * 4, out_specs=[VM] * 3,
        out_shape=[jax.ShapeDtypeStruct((n, d), _MXU)] * 3, compiler_params=_params(),
    )(mem, gm, wk, wv)


def _kv_bwd(dk, dv, mem, wk, wv):
    n, d = mem.shape

    def body(dk_ref, dv_ref, mem_ref, wk_ref, wv_ref, ggm_ref):
        dmemn = _nt(dk_ref[...].astype(_MXU), wk_ref[...]) + _nt(dv_ref[...].astype(_MXU), wv_ref[...])
        xhat, _ = _rms(mem_ref[...])
        ggm_ref[...] = jnp.sum(dmemn * xhat, axis=0, keepdims=True)

    return pl.pallas_call(
        body, name="kv_bwd", in_specs=[VM] * 5, out_specs=VM,
        out_shape=jax.ShapeDtypeStruct((1, d), F32), compiler_params=_params(),
    )(dk, dv, mem, wk, wv)


def _softmax(s):
    e = jnp.exp(s - jnp.max(s, axis=-1, keepdims=True))
    return e / jnp.sum(e, axis=-1, keepdims=True)


def _one_ahead(n, issue):
    nxt = issue(0)
    for a in range(n):
        cur = nxt
        if a + 1 < n:
            nxt = issue(a + 1)
        yield a, cur


def _xattn_fwd(h, g, wq, k, v, wo, hosted=None):
    t, d = h.shape
    hd = d // N_XHEADS
    scale = hd ** -0.5
    tb = _token_block(t, 512)

    def body(h_ref, g_ref, wq_ref, k_ref, v_ref, wo_ref, h2_ref, q_ref, o_ref, xn_ref):
        x = h_ref[...]
        xhat, _ = _rms(x)
        xn = (xhat * g_ref[...]).astype(_MXU)
        xn_ref[...] = xn
        qm = _nn(xn, wq_ref[...]).astype(_MXU)
        q_ref[...] = qm
        heads = [slice(a * hd, (a + 1) * hd) for a in range(N_XHEADS)]
        for a, s in _one_ahead(N_XHEADS, lambda a: _nt(qm[:, heads[a]], k_ref[:, heads[a]]) * scale):
            o_ref[:, heads[a]] = _nn(_softmax(s).astype(_MXU), v_ref[:, heads[a]]).astype(_MXU)
        h2_ref[...] = x + _nn(o_ref[...], wo_ref[...])

    blk = pl.BlockSpec((tb, d), lambda i: (i, 0))
    return _call_hosting(
        body, hosted, name="xattn_fwd", steps=t // tb,
        in_specs=[blk, VM, VM, VM, VM, VM], out_specs=[blk] * 4,
        out_shape=[jax.ShapeDtypeStruct((t, d), F32)] + [jax.ShapeDtypeStruct((t, d), _MXU)] * 3,
        scratch_shapes=[], operands=(h, g, wq, k, v, wo))


def _xattn_bwd(dh2, h, q, g, wq, k, v, wo, hosted=None):
    t, d = h.shape
    n = k.shape[0]
    hd = d // N_XHEADS
    scale = hd ** -0.5
    tb = _token_block(t, 512)

    def body(dh2_ref, h_ref, q_ref, g_ref, wq_ref, k_ref, v_ref, wo_ref, dh_ref, dq_ref, dk_ref, dv_ref, gg_ref):
        @pl.when(pl.program_id(0) == 0)
        def _():
            for r in (dk_ref, dv_ref, gg_ref):
                r[...] = jnp.zeros(r.shape, F32)

        dh2v = dh2_ref[...]
        dom = _nt(dh2v.astype(_MXU), wo_ref[...]).astype(_MXU)
        heads = [slice(a * hd, (a + 1) * hd) for a in range(N_XHEADS)]
        issue = lambda a: (_nt(q_ref[:, heads[a]], k_ref[:, heads[a]]) * scale, _nt(dom[:, heads[a]], v_ref[:, heads[a]]))
        for a, (s, dpr) in _one_ahead(N_XHEADS, issue):
            sl = heads[a]
            pr = _softmax(s)
            dv_ref[:, sl] += _tn(pr.astype(_MXU), dom[:, sl])
            ds = (pr * (dpr - jnp.sum(dpr * pr, axis=-1, keepdims=True)) * scale).astype(_MXU)
            dq_ref[:, sl] = _nn(ds, k_ref[:, sl]).astype(_MXU)
            dk_ref[:, sl] += _tn(ds, q_ref[:, sl])
        dxn = _nt(dq_ref[...], wq_ref[...])
        xhat, r = _rms(h_ref[...])
        dx, gg = _rms_bwd(dxn, xhat, r, g_ref[...])
        gg_ref[...] += gg
        dh_ref[...] = dh2v + dx

    blk = pl.BlockSpec((tb, d), lambda i: (i, 0))
    return _call_hosting(
        body, hosted, name="xattn_bwd", steps=t // tb,
        in_specs=[blk, blk, blk, VM, VM, VM, VM, VM],
        out_specs=[blk, blk, _const_spec((n, d)), _const_spec((n, d)), _const_spec((1, d))],
        out_shape=[jax.ShapeDtypeStruct((t, d), F32), jax.ShapeDtypeStruct((t, d), _MXU),
                   jax.ShapeDtypeStruct((n, d), F32), jax.ShapeDtypeStruct((n, d), F32), jax.ShapeDtypeStruct((1, d), F32)],
        scratch_shapes=[], operands=(dh2, h, q, g, wq, k, v, wo))


def _ffn_fwd(h, g, w_up, conv_w, conv_b, w_down, hosted=None):
    t, d = h.shape
    f = w_down.shape[0]
    ft = FF_TILE
    tb = _token_block(t, 512)

    def body(h_ref, g_ref, wup_ref, cw_ref, cb_ref, wdown_ref, h3_ref, hh_ref, hc_ref, ext, carry, act_sc):
        @pl.when(pl.program_id(0) == 0)
        def _():
            carry[...] = jnp.zeros(carry.shape, F32)

        x = h_ref[...]
        xhat, _ = _rms(x)
        xn = (xhat * g_ref[...]).astype(_MXU)
        acc = jnp.zeros((tb, d), F32)
        up = lambda j: [_nn(xn, wup_ref[:, off:off + ft]) for off in (j * ft, f + j * ft)]
        up_next = up(0)
        for j in range(f // ft):
            hc = []
            up_cur = up_next
            if j + 1 < f // ft:
                up_next = up(j + 1)
            for part, off in enumerate((j * ft, f + j * ft)):
                cols = slice(off, off + ft)
                cur = up_cur[part]
                hh_ref[:, cols] = cur.astype(_MXU)
                ext[part, 0:8, :] = carry[:, cols]
                ext[part, 8:8 + tb, :] = cur
                carry[:, cols] = cur[tb - 8:tb, :]
                hc.append(cb_ref[:, cols] + cw_ref[0:1, cols] * ext[part, 6:6 + tb, :]
                          + cw_ref[1:2, cols] * ext[part, 7:7 + tb, :] + cw_ref[2:3, cols] * cur)
                hc_ref[:, cols] = hc[part].astype(_MXU)
            at = j % DOWN_TILES
            act_sc[:, at * ft:(at + 1) * ft] = (hc[0] * jax.nn.sigmoid(hc[0]) * hc[1]).astype(_MXU)
            if at + 1 == DOWN_TILES or j + 1 == f // ft:
                acc = acc + _nn(act_sc[:, 0:(at + 1) * ft], wdown_ref[(j - at) * ft:(j + 1) * ft, :])
        h3_ref[...] = x + acc

    blk = lambda w: pl.BlockSpec((tb, w), lambda i: (i, 0))
    return _call_hosting(
        body, hosted, name="ffn_fwd", steps=t // tb,
        in_specs=[blk(d), VM, VM, VM, VM, VM], out_specs=[blk(d), blk(2 * f), blk(2 * f)],
        out_shape=[jax.ShapeDtypeStruct((t, d), F32), jax.ShapeDtypeStruct((t, 2 * f), _MXU), jax.ShapeDtypeStruct((t, 2 * f), _MXU)],
        scratch_shapes=[pltpu.VMEM((2, 8 + tb, ft), F32), pltpu.VMEM((8, 2 * f), F32), pltpu.VMEM((tb, DOWN_TILES * ft), _MXU)],
        operands=(h, g, w_up, conv_w, conv_b, w_down))


def _ffn_bwd(dh3, h, hh, hc, g, w_up, conv_w, w_down, hosted=None):
    t, d = h.shape
    f = w_down.shape[0]
    ft = FF_TILE
    tb = _token_block(t, 256)
    nb = t // tb

    def body(dh3_ref, h_ref, hh_ref, hc_ref, g_ref, wup_ref, cw_ref, wdown_ref,
             dh_ref, dhh_ref, act_ref, xn_ref, gcw_ref, gcb_ref, gg_ref, dcarry):
        @pl.when(pl.program_id(0) == 0)
        def _():
            for r in (gcw_ref, gcb_ref, gg_ref, dcarry):
                r[...] = jnp.zeros(r.shape, F32)

        dh3v = dh3_ref[...]
        dhm = dh3v.astype(_MXU)
        dxn = jnp.zeros((tb, d), F32)
        dact_next = _nt(dhm, wdown_ref[0:ft, :])
        for j in range(f // ft):
            dact = dact_next
            if j + 1 < f // ft:
                dact_next = _nt(dhm, wdown_ref[(j + 1) * ft:(j + 2) * ft, :])
            gate = hc_ref[:, j * ft:(j + 1) * ft].astype(F32)
            val = hc_ref[:, f + j * ft:f + (j + 1) * ft].astype(F32)
            sg = jax.nn.sigmoid(gate)
            silu = gate * sg
            act_ref[:, j * ft:(j + 1) * ft] = (silu * val).astype(_MXU)
            dhc = (dact * val * sg * (1.0 + gate * (1.0 - sg)), dact * silu)
            for part, off in enumerate((j * ft, f + j * ft)):
                cols = slice(off, off + ft)
                dc = dhc[part]
                c0 = hh_ref[:, cols].astype(F32)
                after = dcarry[:, cols]
                ahead1 = _shift_rows(dc, -1, after)
                ahead2 = _shift_rows(dc, -2, after)
                dcarry[:, cols] = dc[0:8, :]
                gcb_ref[:, cols] += jnp.sum(dc, axis=0, keepdims=True)
                gcw_ref[0:1, cols] += jnp.sum(ahead2 * c0, axis=0, keepdims=True)
                gcw_ref[1:2, cols] += jnp.sum(ahead1 * c0, axis=0, keepdims=True)
                gcw_ref[2:3, cols] += jnp.sum(dc * c0, axis=0, keepdims=True)
                dhh = (cw_ref[2:3, cols] * dc + cw_ref[1:2, cols] * ahead1 + cw_ref[0:1, cols] * ahead2).astype(_MXU)
                dhh_ref[:, cols] = dhh
                dxn = dxn + _nt(dhh, wup_ref[:, cols])
        xhat, r = _rms(h_ref[...])
        xn_ref[...] = (xhat * g_ref[...]).astype(_MXU)
        dx, gg = _rms_bwd(dxn, xhat, r, g_ref[...])
        gg_ref[...] += gg
        dh_ref[...] = dh3v + dx

    rev = lambda w: pl.BlockSpec((tb, w), lambda i: (nb - 1 - i, 0))
    return _call_hosting(
        body, hosted, name="ffn_bwd", steps=nb,
        in_specs=[rev(d), rev(d), rev(2 * f), rev(2 * f), VM, VM, VM, VM],
        out_specs=[rev(d), rev(2 * f), rev(f), rev(d), _const_spec((3, 2 * f)), _const_spec((1, 2 * f)), _const_spec((1, d))],
        out_shape=[jax.ShapeDtypeStruct((t, d), F32), jax.ShapeDtypeStruct((t, 2 * f), _MXU), jax.ShapeDtypeStruct((t, f), _MXU),
                   jax.ShapeDtypeStruct((t, d), _MXU),
                   jax.ShapeDtypeStruct((3, 2 * f), F32), jax.ShapeDtypeStruct((1, 2 * f), F32), jax.ShapeDtypeStruct((1, d), F32)],
        scratch_shapes=[pltpu.VMEM((8, 2 * f), F32)],
        operands=(dh3, h, hh, hc, g, w_up, conv_w, w_down))


def _loss_head(h, g, target):
    t, d = h.shape
    tb = _token_block(t, 512)

    def body(h_ref, g_ref, tgt_ref, dh_ref, loss_ref, gg_ref):
        @pl.when(pl.program_id(0) == 0)
        def _():
            loss_ref[...] = jnp.zeros(loss_ref.shape, F32)
            gg_ref[...] = jnp.zeros(gg_ref.shape, F32)

        xhat, r = _rms(h_ref[...])
        err = xhat * g_ref[...] - tgt_ref[...]
        loss_ref[...] += 0.5 * jnp.sum(jnp.sum(err * err, axis=-1, keepdims=True), axis=0, keepdims=True) / d
        dx, gg = _rms_bwd(err / d, xhat, r, g_ref[...])
        gg_ref[...] += gg
        dh_ref[...] = dx

    blk = pl.BlockSpec((tb, d), lambda i: (i, 0))
    return pl.pallas_call(
        body, name="loss_head", grid=(t // tb,),
        in_specs=[blk, VM, blk], out_specs=[blk, _const_spec((1, 1)), _const_spec((1, d))],
        out_shape=[jax.ShapeDtypeStruct((t, d), F32), jax.ShapeDtypeStruct((1, 1), F32), jax.ShapeDtypeStruct((1, d), F32)],
        compiler_params=_params(("arbitrary",)),
    )(h, g, target)


def _largest_tile(n, cap, mult=128):
    best = None
    for c in range(mult, min(n, cap) + 1, mult):
        if n % c == 0:
            best = c
    return best if best is not None else n


def _grad_matmul(a, b, name, layer, n_layers, into=None, hosted=None):
    t, m = a.shape
    n = b.shape[1]
    tm, tn, tk = _largest_tile(m, 1408), _largest_tile(n, 1408), _largest_tile(t, 1024)
    nk = t // tk

    def body(a_ref, b_ref, *rest):
        o_ref = rest[-1]

        @pl.when(pl.program_id(2) == 0)
        def _():
            o_ref[...] = jnp.zeros(o_ref.shape, F32)

        o_ref[...] += _tn(a_ref[...].astype(_MXU), b_ref[...].astype(_MXU))

    in_specs = [pl.BlockSpec((tk, tm), lambda i, j, k: (k, i)), pl.BlockSpec((tk, tn), lambda i, j, k: (k, j))]
    operands = (a, b)
    aliases = {}
    if into is not None:
        in_specs.append(pl.BlockSpec(memory_space=pl.ANY))
        operands = (a, b, into)
        aliases = {2: 0}
    (out,), got = _call_hosting(
        body, hosted, name=name, steps=(m // tm, n // tn, nk), in_specs=in_specs,
        out_specs=[pl.BlockSpec((None, tm, tn), lambda i, j, k: (layer, i, j))],
        out_shape=[jax.ShapeDtypeStruct((n_layers, m, n), F32)], scratch_shapes=[], operands=operands, aliases=aliases)
    return out, got


def _adamw_math(w, g, m, v):
    m = ADAM_B1 * m + (1.0 - ADAM_B1) * g
    v = ADAM_B2 * v + (1.0 - ADAM_B2) * (g * g)
    m_hat = m / (1.0 - ADAM_B1 ** ADAM_STEP)
    v_hat = v / (1.0 - ADAM_B2 ** ADAM_STEP)
    return -ADAM_LR * (m_hat / (jnp.sqrt(v_hat) + ADAM_EPS) + ADAM_WD * w), m, v


def _row_block(rows, cols, max_bytes=1 << 20, mult=16):
    best = None
    for r in range(mult, rows + 1, mult):
        if rows % r == 0 and r * cols * 4 <= max_bytes:
            best = r
    return best if best is not None else rows


def _adamw_big(w, g, m, v, name):
    shape = w.shape
    cols = shape[-1]
    flat = lambda a: a.reshape(-1, cols)
    rows = flat(w).shape[0]
    rb = _row_block(rows, cols, 2 << 20)

    def body(w_ref, g_ref, m_ref, v_ref, go_ref, d_ref, nm_ref, nv_ref):
        g = g_ref[...]
        go_ref[...] = g
        d_ref[...], nm_ref[...], nv_ref[...] = _adamw_math(w_ref[...], g, m_ref[...], v_ref[...])

    blk = pl.BlockSpec((rb, cols), lambda i: (i, 0))
    outs = pl.pallas_call(
        body, name=name, grid=(rows // rb,), in_specs=[blk] * 4, out_specs=[blk] * 4,
        out_shape=[jax.ShapeDtypeStruct((rows, cols), F32)] * 4, compiler_params=_params(("parallel",)),
    )(flat(w), flat(g), flat(m), flat(v))
    return [o.reshape(shape) for o in outs]


def _adamw_small(ws, gs, ms, vs):
    n = len(ws)

    def body(*refs):
        for a in range(n):
            w_ref, g_ref, m_ref, v_ref = (refs[s * n + a] for s in range(4))
            d_ref, nm_ref, nv_ref = (refs[(4 + s) * n + a] for s in range(3))
            d_ref[...], nm_ref[...], nv_ref[...] = _adamw_math(w_ref[...], g_ref[...], m_ref[...], v_ref[...])

    outs = pl.pallas_call(
        body, name="adamw_small", in_specs=[VM] * (4 * n), out_specs=[VM] * (3 * n),
        out_shape=[jax.ShapeDtypeStruct(w.shape, F32) for w in ws] * 3, compiler_params=_params(),
    )(*ws, *gs, *ms, *vs)
    return outs[:n], outs[n:2 * n], outs[2 * n:]


def _place():
    x, y, c = lax.axis_index("x"), lax.axis_index("y"), lax.axis_index("c")
    chips = [(1 - x, y), (x, 1 - y), (1 - x, 1 - y)]
    return x, y, c, chips


def _rows(start, size, mult=16):
    return pl.ds(pl.multiple_of(start, mult), size)


def _full_window(ref, axis, chip, half=None):
    r, c = ref.shape
    if axis == 0:
        rs = r // 4
        if half is None:
            return ref.at[_rows(chip * rs, rs), :]
        return ref.at[_rows(chip * rs + half * (rs // 2), rs // 2), :]
    cs = c // 4
    if half is None:
        return ref.at[:, _rows(chip * cs, cs, 128)]
    return ref.at[_rows(half * (r // 2), r // 2), _rows(chip * cs, cs, 128)]


def _remote(src, dst, send_sem, recv_sem, to):
    return pltpu.make_async_remote_copy(src_ref=src, dst_ref=dst, send_sem=send_sem, recv_sem=recv_sem,
                                        device_id=to, device_id_type=MESH)


def _scalars(*vals):
    return jnp.stack([jnp.asarray(v, jnp.int32) for v in vals])


def _cast_place(shard, layer, axis, chip, name):
    _, rs, cs = shard.shape
    full = (rs * 4, cs) if axis == 0 else (rs, cs * 4)
    rb = _row_block(rs, cs, 4 << 20)
    nrb = rs // rb

    def body(chip_ref, s_ref, o_ref):
        o_ref[...] = s_ref[...].astype(_PAY)

    if axis == 0:
        out_map = lambda i, chip_ref: (chip_ref[0] * nrb + i, 0)
    else:
        out_map = lambda i, chip_ref: (i, chip_ref[0])
    return pl.pallas_call(
        body, name=name,
        grid_spec=pltpu.PrefetchScalarGridSpec(
            num_scalar_prefetch=1, grid=(nrb,),
            in_specs=[pl.BlockSpec((None, rb, cs), lambda i, chip_ref: (layer, i, 0))],
            out_specs=pl.BlockSpec((rb, cs), out_map)),
        out_shape=jax.ShapeDtypeStruct(full, _PAY), compiler_params=_params(("parallel",)),
    )(_scalars(chip), shard)


def _hosted_allgather(placed, axes):
    n = len(placed)

    def each(outs, half_of):
        x, y, c, chips = _place()
        for i in range(n):
            for k, chip in enumerate(chips):
                yield i * 3 + k, (*chip, c), (x, y, 1 - c), _full_window(outs[i], axes[i], 2 * x + y, c), \
                    _full_window(outs[i], axes[i], 2 * chip[0] + chip[1], half_of(c))

    def start(_, outs, sems):
        send, recv, _, _ = sems
        for s, peer, _, mine, _ in each(outs, lambda c: c):
            _remote(mine, mine, send.at[s], recv.at[s], peer).start()

    def middle(_, outs, sems):
        send, recv, fsend, frecv = sems
        for s, _, sibling, _, got in each(outs, lambda c: c):
            _remote(got, got, send.at[s], recv.at[s], sibling).wait_recv()
            _remote(got, got, fsend.at[s], frecv.at[s], sibling).start()

    def finish(_, outs, sems):
        send, recv, fsend, frecv = sems
        for s, _, sibling, _, got in each(outs, lambda c: 1 - c):
            _remote(got, got, fsend.at[s], frecv.at[s], sibling).wait_recv()
        for s, peer, sibling, mine, got in each(outs, lambda c: c):
            _remote(mine, mine, send.at[s], recv.at[s], peer).wait_send()
            _remote(got, got, fsend.at[s], frecv.at[s], sibling).wait_send()

    return _Hosted(tuple(placed), True, (), (pltpu.SemaphoreType.DMA((n * 3,)),) * 4, (start, middle, finish))


def _allgather_conv(conv_shard):
    nl, taps, cs = conv_shard.shape

    def body(in_ref, out_ref, send, recv, local):
        x, y, c, chips = _place()
        mine = out_ref.at[:, :, _rows((2 * x + y) * cs, cs, 128)]
        own = pltpu.make_async_copy(in_ref, mine, local)
        own.start()
        sends = [_remote(in_ref, mine, send.at[k], recv.at[k], (*chip, c)) for k, chip in enumerate(chips)]
        for cp in sends:
            cp.start()
        for k, chip in enumerate(chips):
            got = out_ref.at[:, :, _rows((2 * chip[0] + chip[1]) * cs, cs, 128)]
            _remote(got, got, send.at[k], recv.at[k], (*chip, c)).wait_recv()
        for cp in sends:
            cp.wait_send()
        own.wait()

    return pl.pallas_call(
        body, name="allgather_conv", in_specs=[HB], out_specs=HB, out_shape=jax.ShapeDtypeStruct((nl, taps, cs * 4), conv_shard.dtype),
        scratch_shapes=[pltpu.SemaphoreType.DMA((3,)), pltpu.SemaphoreType.DMA((3,)), pltpu.SemaphoreType.DMA],
        compiler_params=pltpu.CompilerParams(has_side_effects=True),
    )(conv_shard)


def _hosted_exchange(grads, axes, layer):
    na = len(grads)
    views = [g.reshape(g.shape[0], 4, 2, g.shape[1] // 8, g.shape[2]) if ax == 0 else g for g, ax in zip(grads, axes)]

    def region(ref, axis, half):
        if axis == 0:
            return ref.at[layer, :, half]
        r = ref.shape[1]
        return ref.at[layer, _rows(half * (r // 2), r // 2), :]

    def copies(ins, land, sems):
        send, recv = sems
        x, y, c, _ = _place()
        return [_remote(region(ins[a], axes[a], 1 - c), land[a], send.at[a], recv.at[a], (x, y, 1 - c)) for a in range(na)]

    def start(ins, land, sems):
        for cp in copies(ins, land, sems):
            cp.start()

    def finish(ins, land, sems):
        for cp in copies(ins, land, sems):
            cp.wait()

    shapes = [(4, g.shape[1] // 8, g.shape[2]) if ax == 0 else (g.shape[1] // 2, g.shape[2]) for g, ax in zip(grads, axes)]
    return _Hosted(tuple(views), False, tuple(jax.ShapeDtypeStruct(s, F32) for s in shapes),
                   (pltpu.SemaphoreType.DMA((na,)),) * 2, (start, None, finish))


def _add_cast(mine, theirs, core, base, name):
    na, nb, cols = theirs.shape
    rb = _row_block(nb, cols, 4 << 20)

    def body(core_ref, a_ref, b_ref, o_ref):
        o_ref[...] = (a_ref[...] + b_ref[...]).astype(_PAY)

    blk = pl.BlockSpec((None, rb, cols), lambda i, k, core_ref: (i, k, 0))
    return pl.pallas_call(
        body, name=name,
        grid_spec=pltpu.PrefetchScalarGridSpec(
            num_scalar_prefetch=1, grid=(na, nb // rb),
            in_specs=[pl.BlockSpec((None, None, rb, cols), lambda i, k, core_ref: (base + i, core_ref[0], k, 0)), blk], out_specs=blk),
        out_shape=jax.ShapeDtypeStruct((na, nb, cols), _PAY), compiler_params=_params(("parallel", "parallel")),
    )(_scalars(core), mine, theirs)


def _piece(ref, axis, chip):
    if axis == 0:
        return ref.at[chip]
    cs = ref.shape[1] // 4
    return ref.at[:, _rows(chip * cs, cs, 128)]


def _hosted_scatter(sums, axes):
    na = len(sums)

    def piece_shape(a):
        if axes[a] == 0:
            return (sums[a].shape[1], sums[a].shape[2])
        return (sums[a].shape[0], sums[a].shape[1] // 4)

    def copies(ins, slots, sems):
        send, recv = sems
        _, _, c, chips = _place()
        return [_remote(_piece(ins[a], axes[a], 2 * chip[0] + chip[1]), slots[a].at[k], send.at[a * 3 + k], recv.at[a * 3 + k], (*chip, c))
                for a in range(na) for k, chip in enumerate(chips)]

    def start(ins, slots, sems):
        for cp in copies(ins, slots, sems):
            cp.start()

    def finish(ins, slots, sems):
        for cp in copies(ins, slots, sems):
            cp.wait()

    return _Hosted(tuple(sums), False, tuple(jax.ShapeDtypeStruct((3,) + piece_shape(a), sums[a].dtype) for a in range(na)),
                   (pltpu.SemaphoreType.DMA((na * 3,)),) * 2, (start, None, finish))


def _sum_slots(sums, slots, axis, chip, core, layer, n_layers, name, into=None):
    _, hr, cs = slots.shape
    rb = _row_block(hr, cs, 4 << 20)

    def body(at_ref, own_ref, s_ref, *rest):
        rest[-1][...] = ((own_ref[...].astype(F32) + s_ref[0].astype(F32)) + s_ref[1].astype(F32)) + s_ref[2].astype(F32)

    if axis == 0:
        own = pl.BlockSpec((None, rb, cs), lambda k, at_ref: (at_ref[0], k, 0))
    else:
        own = pl.BlockSpec((rb, cs), lambda k, at_ref: (k, at_ref[0]))
    in_specs = [own, pl.BlockSpec((3, rb, cs), lambda k, at_ref: (0, k, 0))]
    operands = (sums, slots)
    aliases = {}
    if into is not None:
        in_specs.append(pl.BlockSpec(memory_space=pl.ANY))
        operands = (sums, slots, into)
        aliases = {3: 0}
    return pl.pallas_call(
        body, name=name,
        grid_spec=pltpu.PrefetchScalarGridSpec(
            num_scalar_prefetch=1, grid=(hr // rb,), in_specs=in_specs,
            out_specs=pl.BlockSpec((None, None, rb, cs), lambda k, at_ref: (layer, at_ref[1], k, 0))),
        out_shape=jax.ShapeDtypeStruct((n_layers, 2, hr, cs), F32), input_output_aliases=aliases,
        compiler_params=_params(("parallel",)),
    )(_scalars(chip, core), *operands)


def _sibling_assemble(shards):
    na = len(shards)

    def body(*refs):
        outs = refs[na:2 * na]
        send, recv = refs[2 * na:]
        x, y, c, _ = _place()
        copies = []
        for a in range(na):
            hr = outs[a].shape[1] // 2
            mine = outs[a].at[:, _rows(c * hr, hr), :]
            cp = _remote(mine, mine, send.at[a], recv.at[a], (x, y, 1 - c))
            cp.start()
            copies.append(cp)
        for cp in copies:
            cp.wait()

    return pl.pallas_call(
        body, name="grad_sibling_assemble", in_specs=[HB] * na, out_specs=[HB] * na,
        out_shape=[jax.ShapeDtypeStruct(s.shape, F32) for s in shards], input_output_aliases={a: a for a in range(na)},
        scratch_shapes=[pltpu.SemaphoreType.DMA((na,))] * 2,
        compiler_params=pltpu.CompilerParams(has_side_effects=True),
    )(*shards)


def _allreduce_small(buf, hosted):
    rows, w = buf.shape
    half = rows // 2
    nh, h_shapes = len(hosted.operands), _hosted_results(hosted)

    def body(buf_ref, *refs):
        h_in, out_ref, h_out = refs[:nh], refs[nh], refs[nh + 1:nh + 1 + len(h_shapes)]
        land, slots, red, sems_send, sems_recv = refs[nh + 1 + len(h_shapes):nh + 6 + len(h_shapes)]
        h_sems = refs[nh + 6 + len(h_shapes):]
        hosted.stages[0](h_in, h_out, h_sems)
        x, y, c, chips = _place()
        me = 2 * x + y
        sibling = (x, y, 1 - c)
        first = _remote(buf_ref, land, sems_send.at[0], sems_recv.at[0], sibling)
        first.start()
        first.wait()
        mine = pl.ds(pl.multiple_of(c * half, 8), half)
        slots[me] = buf_ref[mine, :] + land[mine, :]
        sends = []
        for k, chip in enumerate(chips):
            cp = _remote(slots.at[me], slots.at[me], sems_send.at[1 + k], sems_recv.at[1 + k], (*chip, c))
            cp.start()
            sends.append(cp)
        for k, chip in enumerate(chips):
            got = slots.at[2 * chip[0] + chip[1]]
            _remote(got, got, sems_send.at[1 + k], sems_recv.at[1 + k], sibling).wait_recv()
        red[...] = ((slots[0] + slots[1]) + slots[2]) + slots[3]
        out_ref[mine, :] = red[...]
        last = _remote(red, out_ref.at[mine, :], sems_send.at[4], sems_recv.at[4], sibling)
        last.start()
        theirs = out_ref.at[pl.ds(pl.multiple_of((1 - c) * half, 8), half), :]
        _remote(red, theirs, sems_send.at[4], sems_recv.at[4], sibling).wait_recv()
        for cp in sends:
            cp.wait_send()
        last.wait_send()
        hosted.stages[2](h_in, h_out, h_sems)

    outs = pl.pallas_call(
        body, name="allreduce_small", in_specs=[VM] + [HB] * nh, out_specs=[VM] + [HB] * len(h_shapes),
        out_shape=[jax.ShapeDtypeStruct((rows, w), F32)] + h_shapes,
        scratch_shapes=[pltpu.VMEM((rows, w), F32), pltpu.VMEM((4, half, w), F32), pltpu.VMEM((half, w), F32),
                        pltpu.SemaphoreType.DMA((5,)), pltpu.SemaphoreType.DMA((5,))] + list(hosted.sems),
        compiler_params=pltpu.CompilerParams(has_side_effects=True, vmem_limit_bytes=VMEM_LIMIT),
    )(buf, *hosted.operands)
    return outs[0], outs[1:]


BIG = ("w_in", "w_out", "wq", "wk", "wv", "wo", "w_up", "w_down")
MIXER, ATTN, MLP = ("w_in", "w_out"), ("wq", "wk", "wv", "wo"), ("w_up", "w_down")
BIG_AXIS = {"w_in": 1, "w_out": 0, "wq": 0, "wk": 0, "wv": 0, "wo": 0, "w_up": 1, "w_down": 0}
SMALL = ("norm_mix_g", "pool_w", "pool_scale", "sgu_g", "sgu_w", "sgu_b", "norm_xattn_g", "mem_norm_g", "norm_ffn_g",
         "conv_w", "conv_b", "final_norm_g")
ORDER = ("norm_mix_g", "w_in", "pool_w", "pool_scale", "sgu_g", "sgu_w", "sgu_b", "w_out", "norm_xattn_g", "mem_norm_g",
         "wq", "wk", "wv", "wo", "norm_ffn_g", "w_up", "conv_w", "conv_b", "w_down", "final_norm_g")
PACK_WIDTH = 512


def kernel(x, mem, norm_mix_g, w_in, pool_w, pool_scale, sgu_g, sgu_w, sgu_b, w_out, norm_xattn_g, mem_norm_g, wq, wk, wv, wo, norm_ffn_g, w_up, conv_w, conv_b, w_down, final_norm_g, loss_target, m_norm_mix_g, m_w_in, m_pool_w, m_pool_scale, m_sgu_g, m_sgu_w, m_sgu_b, m_w_out, m_norm_xattn_g, m_mem_norm_g, m_wq, m_wk, m_wv, m_wo, m_norm_ffn_g, m_w_up, m_conv_w, m_conv_b, m_w_down, m_final_norm_g, v_norm_mix_g, v_w_in, v_pool_w, v_pool_scale, v_sgu_g, v_sgu_w, v_sgu_b, v_w_out, v_norm_xattn_g, v_mem_norm_g, v_wq, v_wk, v_wv, v_wo, v_norm_ffn_g, v_w_up, v_conv_w, v_conv_b, v_w_down, v_final_norm_g):
    given = dict(locals())
    w = {n: given[n] for n in ORDER}
    mom = {n: given["m_" + n] for n in ORDER}
    var = {n: given["v_" + n] for n in ORDER}
    nl = w_in.shape[0]
    xs, mems, tgt = x[0], mem[0], loss_target[0]
    chip = 2 * lax.axis_index("x") + lax.axis_index("y")
    core = lax.axis_index("c")

    axes_of = lambda names: [BIG_AXIS[n] for n in names]
    placed = [{n: _cast_place(w[n], l, BIG_AXIS[n], chip, f"place_{n}_{l}") for n in BIG} for l in range(nl)]
    conv_full = _allgather_conv(conv_w)

    def gather(names, l):
        return _hosted_allgather([placed[l][n] for n in names], axes_of(names))

    full = [dict(zip(MIXER, _run_hosted(gather(MIXER, 0), "allgather_weights")))]

    row = lambda a, l: a[l][None, :]
    saved = []
    h = xs
    for l in range(nl):
        fw = full[l]
        sbt = jnp.broadcast_to(sgu_b[l][:, :, None], sgu_w[l].shape)
        (h1, proj, xn1, mix), got = _mixer_fwd(h, row(norm_mix_g, l), fw["w_in"], pool_w[l], row(pool_scale, l), row(sgu_g, l), sgu_w[l], sbt, fw["w_out"],
                                               [gather(ATTN, 0), gather(("w_down",), 0)] if l == 0 else None)
        if l == 0:
            fw.update(zip(ATTN, got[0]))
            fw["w_down"] = got[1][0]
        k, v, memn = _kv_fwd(mems, row(mem_norm_g, l), fw["wk"], fw["wv"])
        (h2, q, o, xn2), got = _xattn_fwd(h1, row(norm_xattn_g, l), fw["wq"], k, v, fw["wo"], [gather(("w_up",), 0)] if l == 0 else None)
        if l == 0:
            fw["w_up"] = got[0][0]
        (h3, hh, hc), got = _ffn_fwd(h2, row(norm_ffn_g, l), fw["w_up"], conv_full[l], row(conv_b, l), fw["w_down"],
                                      [gather(BIG, l + 1)] if l + 1 < nl else None)
        if l + 1 < nl:
            full.append(dict(zip(BIG, got[0])))
        saved.append(dict(h=h, h1=h1, h2=h2, proj=proj, xn1=xn1, mix=mix, k=k, v=v, memn=memn, q=q, o=o, xn2=xn2, hh=hh, hc=hc, sbt=sbt))
        h = h3

    dh, loss_part, g_final = _loss_head(h, final_norm_g[None, :], tgt)

    big_grads = {}
    small_grads = [None] * nl

    def weight_grad(n, a, b, l, hosted=None):
        big_grads[n], got = _grad_matmul(a, b, "grad_" + n, l, nl, big_grads.get(n), hosted)
        return got

    sums, slots = {}, {}

    def exchange(names, l):
        return _hosted_exchange([big_grads[n] for n in names], axes_of(names), l)

    def scatter(names, l):
        return _hosted_scatter([sums[n, l] for n in names], axes_of(names))

    def add_casts(names, theirs, l):
        for n, t in zip(names, theirs):
            g = big_grads[n]
            gl, gr, gc = g.shape
            if BIG_AXIS[n] == 0:
                sums[n, l] = _add_cast(g.reshape(gl * 4, 2, gr // 8, gc), t, core, l * 4, "grad_chip_sum_" + n)
            else:
                sums[n, l] = _add_cast(g.reshape(gl, 2, gr // 2, gc), t[None], core, l, "grad_chip_sum_" + n)[0]

    def keep_slots(names, got, l):
        for n, sl in zip(names, got):
            slots[n, l] = sl

    for l in reversed(range(nl)):
        fw, s = full[l], saved[l]
        above = l + 1 < nl
        dh3 = dh
        (dh2, dhh, act, xn3, g_cw, g_cb, g_nf), got = _ffn_bwd(dh3, s["h2"], s["hh"], s["hc"], row(norm_ffn_g, l), fw["w_up"], conv_full[l], fw["w_down"],
                                                         [exchange(MIXER, l + 1), scatter(ATTN, l + 1)] if above else None)
        if above:
            add_casts(MIXER, got[0], l + 1)
            keep_slots(ATTN, got[1], l + 1)
        weight_grad("w_up", xn3, dhh, l)
        weight_grad("w_down", act, dh3, l)
        (dh1, dq, dk, dv, g_nx), got = _xattn_bwd(dh2, s["h1"], s["q"], row(norm_xattn_g, l), fw["wq"], s["k"], s["v"], fw["wo"],
                                                  [exchange(MLP, l), scatter(MIXER, l + 1) if above else None])
        add_casts(MLP, got[0], l)
        if above:
            keep_slots(MIXER, got[1], l + 1)
        weight_grad("wq", s["xn2"], dq, l)
        weight_grad("wo", s["o"], dh2, l)
        weight_grad("wk", s["memn"], dk, l)
        weight_grad("wv", s["memn"], dv, l)
        g_mn = _kv_bwd(dk, dv, mems, fw["wk"], fw["wv"])
        (dh0, dproj, g_nm, g_pw, g_ps, g_sg, g_sw, g_sbt), got = _mixer_bwd(dh1, s["h"], s["proj"], row(norm_mix_g, l), fw["w_in"], pool_w[l], row(pool_scale, l), row(sgu_g, l), sgu_w[l], s["sbt"], fw["w_out"],
                                                                           [scatter(MLP, l), exchange(ATTN, l)])
        keep_slots(MLP, got[0], l)
        add_casts(ATTN, got[1], l)
        got = weight_grad("w_in", s["xn1"], dproj, l, [scatter(ATTN, l)] if l == 0 else None)
        if l == 0:
            keep_slots(ATTN, got[0], l)
        weight_grad("w_out", s["mix"], dh1, l)
        small_grads[l] = dict(norm_mix_g=g_nm, pool_w=g_pw, pool_scale=g_ps, sgu_g=g_sg, sgu_w=g_sw, sgu_b=jnp.sum(g_sbt, axis=-1),
                              norm_xattn_g=g_nx, mem_norm_g=g_mn, norm_ffn_g=g_nf, conv_w=g_cw, conv_b=g_cb)
        dh = dh0
    grad_x = dh[None]

    add_casts(MIXER, _run_hosted(exchange(MIXER, 0), "grad_sibling_exchange"), 0)

    layered = [n for n in SMALL if n != "final_norm_g"]
    parts = [small_grads[l][n].reshape(-1, PACK_WIDTH) for n in layered for l in range(nl)]
    parts.append(g_final.reshape(-1, PACK_WIDTH))
    parts.append(jnp.pad(loss_part, ((0, 0), (0, PACK_WIDTH - 1))))
    used = sum(p.shape[0] for p in parts)
    total = -(-used // 16) * 16
    packed, got = _allreduce_small(jnp.concatenate(parts + [jnp.zeros((total - used, PACK_WIDTH), F32)], axis=0), scatter(MIXER, 0))
    keep_slots(MIXER, got, 0)

    halves = []
    for n in BIG:
        buf = None
        for l in range(nl):
            buf = _sum_slots(sums[n, l], slots[n, l], BIG_AXIS[n], chip, core, l, nl, "grad_sum_" + n, buf)
        halves.append(buf.reshape(nl, 2 * buf.shape[2], buf.shape[3]))
    shard_grads = dict(zip(BIG, _sibling_assemble(halves)))
    grads = dict(shard_grads)
    at = 0
    for n in layered:
        per_layer = []
        for l in range(nl):
            shape = small_grads[l][n].shape
            nrow = small_grads[l][n].size // PACK_WIDTH
            per_layer.append(packed[at:at + nrow].reshape(shape))
            at += nrow
        g = jnp.stack(per_layer)
        if n == "conv_w":
            cs = conv_w.shape[2]
            g = lax.dynamic_slice_in_dim(g, chip * cs, cs, axis=2)
        grads[n] = g.reshape(w[n].shape)
    grads["final_norm_g"] = packed[at:at + g_final.size // PACK_WIDTH].reshape(final_norm_g.shape)
    at += g_final.size // PACK_WIDTH
    loss = packed[at, 0]

    delta, new_m, new_v = {}, {}, {}
    for n in BIG:
        grads[n], delta[n], new_m[n], new_v[n] = _adamw_big(w[n], grads[n], mom[n], var[n], "adamw_" + n)
    two_d = lambda a: a.reshape(-1, a.shape[-1])
    ds, nms, nvs = _adamw_small([two_d(w[n]) for n in SMALL], [two_d(grads[n]) for n in SMALL],
                                [two_d(mom[n]) for n in SMALL], [two_d(var[n]) for n in SMALL])
    for n, d_, m_, v_ in zip(SMALL, ds, nms, nvs):
        delta[n], new_m[n], new_v[n] = d_.reshape(w[n].shape), m_.reshape(w[n].shape), v_.reshape(w[n].shape)

    return (loss, grad_x, *[grads[n] for n in ORDER], *[delta[n] for n in ORDER], *[new_m[n] for n in ORDER], *[new_v[n] for n in ORDER])
```

```python
import math
from typing import NamedTuple

import jax
import jax.numpy as jnp
from jax import lax
from jax.experimental import pallas as pl
from jax.experimental.pallas import tpu as pltpu

F32 = jnp.float32
_MXU = jnp.bfloat16
_PAY = jnp.bfloat16
EPS = 1e-6
WINDOWS = (2, 4, 8, 16)
GROUP = 128
N_XHEADS = 4
HALO = 16
FF_TILE = 256
DOWN_TILES = 4
VMEM_LIMIT = 60 * 1024 * 1024
MESH = pl.DeviceIdType.MESH

ADAM_LR, ADAM_B1, ADAM_B2, ADAM_EPS, ADAM_WD, ADAM_STEP = 0.001, 0.9, 0.999, 1e-08, 0.01, 10

VM = pl.BlockSpec(memory_space=pltpu.VMEM)
HB = pl.BlockSpec(memory_space=pltpu.HBM)


def _nn(a, b):
    return jnp.dot(a, b, preferred_element_type=F32)


def _nt(a, b):
    return lax.dot_general(a, b, (((1,), (1,)), ((), ())), preferred_element_type=F32)


def _tn(a, b):
    return lax.dot_general(a, b, (((0,), (0,)), ((), ())), preferred_element_type=F32)


def _rms(x):
    r = lax.rsqrt(jnp.mean(x * x, axis=-1, keepdims=True) + EPS)
    return x * r, r


def _rms_bwd(dxn, xhat, r, g):
    dxh = dxn * g
    dx = r * (dxh - xhat * jnp.mean(dxh * xhat, axis=-1, keepdims=True))
    return dx, jnp.sum(dxn * xhat, axis=0, keepdims=True)


def _gelu(x):
    cdf = 0.5 * (1.0 + lax.erf(x * (2.0 ** -0.5)))
    return x * cdf, cdf


def _gelu_grad(x, cdf):
    return cdf + x * jnp.exp(-0.5 * x * x) * ((2.0 * math.pi) ** -0.5)


def _params(sem=None):
    return pltpu.CompilerParams(dimension_semantics=sem, vmem_limit_bytes=VMEM_LIMIT)


def _token_block(t, want):
    return want if t % want == 0 and t > want else GROUP


def _const_spec(shape):
    n = len(shape)
    return pl.BlockSpec(shape, lambda i: (0,) * n)


def _tril():
    return lax.broadcasted_iota(jnp.int32, (GROUP, GROUP), 0) >= lax.broadcasted_iota(jnp.int32, (GROUP, GROUP), 1)


def _shift_rows(x, k, edge):
    tb = x.shape[0]
    r8 = lax.broadcasted_iota(jnp.int32, (8, 1), 0)
    rolled = pltpu.roll(x, k % tb, 0)
    if k > 0:
        top = jnp.where(r8 < k, pltpu.roll(edge, k, 0), rolled[0:8, :])
        return jnp.concatenate([top, rolled[8:, :]], axis=0)
    bottom = jnp.where(r8 >= 8 + k, pltpu.roll(edge, 8 + k, 0), rolled[tb - 8:, :])
    return jnp.concatenate([rolled[:tb - 8, :], bottom], axis=0)


def _in_turns(parts):
    parts = list(parts)
    while parts:
        for p in list(parts):
            try:
                next(p)
            except StopIteration:
                parts.remove(p)


class _Hosted(NamedTuple):
    operands: tuple
    aliased: bool
    out_shapes: tuple
    sems: tuple
    stages: tuple


def _hosted_results(hosted):
    if hosted.aliased:
        return [jax.ShapeDtypeStruct(o.shape, o.dtype) for o in hosted.operands]
    return list(hosted.out_shapes)


def _call_hosting(main_body, hosted, *, name, steps, in_specs, out_specs, out_shape, scratch_shapes, operands, aliases=None):
    grid = steps if isinstance(steps, tuple) else (steps,)
    semantics = ("arbitrary",) * len(grid)
    hosted = [hs for hs in (hosted or ()) if hs is not None]
    if not hosted:
        outs = pl.pallas_call(main_body, name=name, grid=grid, in_specs=in_specs, out_specs=out_specs, out_shape=out_shape,
                              scratch_shapes=scratch_shapes, input_output_aliases=aliases or {}, compiler_params=_params(semantics))(*operands)
        return outs, ()
    n_in, n_out, n_sc = len(in_specs), len(out_specs), len(scratch_shapes)
    shapes = [_hosted_results(hs) for hs in hosted]
    aliases, in_at, out_at = dict(aliases or {}), n_in, n_out
    for hs, sh in zip(hosted, shapes):
        if hs.aliased:
            aliases.update({in_at + i: out_at + i for i in range(len(hs.operands))})
        in_at += len(hs.operands)
        out_at += len(sh)

    def body(*refs):
        at = [0]

        def take(n):
            at[0] += n
            return refs[at[0] - n:at[0]]

        ins = take(n_in)
        h_in = [take(len(hs.operands)) for hs in hosted]
        outs = take(n_out)
        h_out = [take(len(sh)) for sh in shapes]
        scratch = take(n_sc)
        h_sems = [take(len(hs.sems)) for hs in hosted]
        ids = [pl.program_id(a) for a in range(len(grid))]

        def at_step(where):
            lead, rest = where
            ok = ids[0] == lead
            for a in range(1, len(grid)):
                ok = jnp.logical_and(ok, ids[a] == (grid[a] - 1 if rest else 0))
            return ok

        def run(stage):
            for hs, a, b, c in zip(hosted, h_in, h_out, h_sems):
                if hs.stages[stage] is not None:
                    hs.stages[stage](a, b, c)

        @pl.when(at_step((0, 0)))
        def _():
            run(0)

        if any(hs.stages[1] is not None for hs in hosted):
            @pl.when(at_step(((3 * grid[0]) // 4, 0)))
            def _():
                run(1)

        main_body(*ins, *outs, *scratch)

        @pl.when(at_step((grid[0] - 1, -1)))
        def _():
            run(2)

    flat = lambda lists: [x for xs in lists for x in xs]
    outs = pl.pallas_call(
        body, name=name, grid=grid, in_specs=list(in_specs) + [HB] * (in_at - n_in), out_specs=list(out_specs) + [HB] * (out_at - n_out),
        out_shape=list(out_shape) + flat(shapes), scratch_shapes=list(scratch_shapes) + flat(hs.sems for hs in hosted),
        input_output_aliases=aliases, compiler_params=_params(semantics),
    )(*operands, *flat(hs.operands for hs in hosted))
    results, at = [], n_out
    for sh in shapes:
        results.append(outs[at:at + len(sh)])
        at += len(sh)
    return outs[:n_out], results


def _run_hosted(hosted, name):
    nh = len(hosted.operands)
    h_shapes = _hosted_results(hosted)

    def body(*refs):
        h_in, h_out, h_sems = refs[:nh], refs[nh:nh + len(h_shapes)], refs[nh + len(h_shapes):]
        for stage in hosted.stages:
            if stage is not None:
                stage(h_in, h_out, h_sems)

    return pl.pallas_call(
        body, name=name, in_specs=[HB] * nh, out_specs=[HB] * len(h_shapes), out_shape=h_shapes, scratch_shapes=list(hosted.sems),
        input_output_aliases={i: i for i in range(nh)} if hosted.aliased else {},
        compiler_params=pltpu.CompilerParams(has_side_effects=True),
    )(*hosted.operands)


def _window_sums(e, win, back):
    n = e.shape[0]
    k = 1
    while k < win:
        e = e + pltpu.roll(e, k if back else n - k, 0)
        k *= 2
    return e


def _pool_diff(prev, p, t0, gi, win):
    sl = slice(gi * GROUP, (gi + 1) * GROUP)
    tb = p.shape[0]
    s = _window_sums(jnp.concatenate([prev[:, sl], p[:, sl]], axis=0), win, True)[HALO:, :]
    tglob = t0 + lax.broadcasted_iota(jnp.int32, (tb, 1), 0)
    cnt = jnp.minimum(tglob + 1, win).astype(F32)
    return s / cnt - p[:, sl], cnt


def _layernorm(v):
    xc = v - jnp.mean(v, axis=-1, keepdims=True)
    rstd = lax.rsqrt(jnp.mean(xc * xc, axis=-1, keepdims=True) + EPS)
    return xc * rstd, rstd


def _mixer_fwd(h, g, w_in, pool_w, pool_scale, sgu_g, sgu_w, sgu_bt, w_out, hosted=None):
    t, d = h.shape
    pw = pool_w.shape[0] * GROUP
    sw = sgu_w.shape[0] * GROUP
    tb = _token_block(t, 512)

    def body(h_ref, g_ref, win_ref, pw_ref, ps_ref, sg_ref, sw_ref, sbt_ref, wout_ref, h1_ref, proj_ref, xn_ref, mix_ref, pext):
        i = pl.program_id(0)

        @pl.when(i == 0)
        def _():
            pext[...] = jnp.zeros((HALO, pw), F32)

        x = h_ref[...]
        xhat, _ = _rms(x)
        xn = (xhat * g_ref[...]).astype(_MXU)
        xn_ref[...] = xn
        proj = _nn(xn, win_ref[...])
        proj_ref[...] = proj
        p = proj[:, :pw]
        prev = pext[...]
        for gi, win in enumerate(WINDOWS):
            sl = slice(gi * GROUP, (gi + 1) * GROUP)
            dg, _ = _pool_diff(prev, p, i * tb, gi, win)
            e = _nn(dg.astype(_MXU), pw_ref[gi].astype(_MXU))
            mix_ref[:, sl] = (e * ps_ref[:, sl]).astype(_MXU)
        pext[...] = p[tb - HALO:tb, :]
        uv, _ = _gelu(proj[:, pw:])
        u = uv[:, :sw]
        vhat, _ = _layernorm(uv[:, sw:])
        vn = (vhat * sg_ref[...]).astype(_MXU)
        mask = _tril()
        chunks = [slice(n * GROUP, (n + 1) * GROUP) for n in range(tb // GROUP)]
        for hh in range(sw // GROUP):
            wm = jnp.where(mask, sw_ref[hh], 0.0).astype(_MXU)
            cols = slice(hh * GROUP, (hh + 1) * GROUP)
            z = _nn(wm, jnp.concatenate([vn[rows, cols] for rows in chunks], axis=1))
            for n, rows in enumerate(chunks):
                mix_ref[rows, pw + hh * GROUP:pw + (hh + 1) * GROUP] = (u[rows, cols] * (z[:, chunks[n]] + sbt_ref[hh])).astype(_MXU)
        h1_ref[...] = x + _nn(mix_ref[...], wout_ref[...])

    blk = lambda w: pl.BlockSpec((tb, w), lambda i: (i, 0))
    return _call_hosting(
        body, hosted, name="mixer_fwd", steps=t // tb,
        in_specs=[blk(d), VM, VM, VM, VM, VM, VM, VM, VM],
        out_specs=[blk(d), blk(w_in.shape[1]), blk(d), blk(d)],
        out_shape=[jax.ShapeDtypeStruct((t, d), F32), jax.ShapeDtypeStruct((t, w_in.shape[1]), F32),
                   jax.ShapeDtypeStruct((t, d), _MXU), jax.ShapeDtypeStruct((t, d), _MXU)],
        scratch_shapes=[pltpu.VMEM((HALO, pw), F32)],
        operands=(h, g, w_in, pool_w, pool_scale, sgu_g, sgu_w, sgu_bt, w_out))


def _mixer_bwd(dh1, h, proj, g, w_in, pool_w, pool_scale, sgu_g, sgu_w, sgu_bt, w_out, hosted=None):
    t, d = h.shape
    ng, nh = pool_w.shape[0], sgu_w.shape[0]
    pw, sw = ng * GROUP, nh * GROUP
    tb = _token_block(t, 512)
    nb = t // tb
    n_parts = 2 if tb % (2 * GROUP) == 0 else 1
    pt = tb // n_parts

    def body(dh1_ref, h_ref, proj_ref, halo_ref, g_ref, win_ref, pw_ref, ps_ref, sg_ref, sw_ref, sbt_ref, wout_ref,
             dh_ref, dproj_ref, gg_ref, gpw_ref, gps_ref, gsg_ref, gsw_ref, gsbt_ref, dext, duv):
        i = pl.program_id(0)
        blk = nb - 1 - i

        @pl.when(i == 0)
        def _():
            for r in (gg_ref, gpw_ref, gps_ref, gsg_ref, gsw_ref, gsbt_ref, dext):
                r[...] = jnp.zeros(r.shape, F32)

        mask = _tril()

        def part(at):
            rows = slice(at, at + pt)
            dh1v = dh1_ref[rows, :]
            dmix = _nt(dh1v.astype(_MXU), wout_ref[...])
            yield
            proj_v = proj_ref[rows, :]
            p = proj_v[:, :pw]
            prev = jnp.where(blk == 0, 0.0, halo_ref[...]) if at == 0 else proj_ref[at - HALO:at, 0:pw]
            for gi, win in enumerate(WINDOWS):
                sl = slice(gi * GROUP, (gi + 1) * GROUP)
                dg, cnt = _pool_diff(prev, p, blk * tb + at, gi, win)
                dgm = dg.astype(_MXU)
                pwm = pw_ref[gi].astype(_MXU)
                e = _nn(dgm, pwm)
                dy = dmix[:, sl]
                gps_ref[:, sl] += jnp.sum(dy * e, axis=0, keepdims=True)
                de = (dy * ps_ref[:, sl]).astype(_MXU)
                gpw_ref[gi] += _tn(dgm, de)
                dd = _nt(de, pwm)
                ddc = dd / cnt
                acc = _window_sums(jnp.concatenate([ddc, dext[:, sl]], axis=0), win, False)[:pt, :]
                dext[:, sl] = ddc[0:HALO, :]
                dproj_ref[rows, sl] = (acc - dd).astype(_MXU)
            yield
            pre = proj_v[:, pw:]
            uv, cdf = _gelu(pre)
            u = uv[:, :sw]
            vhat, rstd = _layernorm(uv[:, sw:])
            vn = (vhat * sg_ref[...]).astype(_MXU)
            chunks = [slice(n * GROUP, (n + 1) * GROUP) for n in range(pt // GROUP)]
            side_by_side = lambda a, cols: jnp.concatenate([a[c, cols] for c in chunks], axis=1)
            for hh in range(nh):
                wm = jnp.where(mask, sw_ref[hh], 0.0).astype(_MXU)
                cols = slice(hh * GROUP, (hh + 1) * GROUP)
                vs = side_by_side(vn, cols)
                z = _nn(wm, vs)
                dy = side_by_side(dmix, slice(pw + hh * GROUP, pw + (hh + 1) * GROUP))
                dz = dy * side_by_side(u, cols)
                dzm = dz.astype(_MXU)
                dvs = _tn(wm, dzm)
                gsw_ref[hh] += jnp.where(mask, _nt(dzm, vs), 0.0)
                gb = jnp.zeros((GROUP, GROUP), F32)
                for n, c in enumerate(chunks):
                    gb = gb + dz[:, c]
                    duv[at + n * GROUP:at + (n + 1) * GROUP, cols] = dy[:, c] * (z[:, c] + sbt_ref[hh])
                    duv[at + n * GROUP:at + (n + 1) * GROUP, sw + hh * GROUP:sw + (hh + 1) * GROUP] = dvs[:, c]
                gsbt_ref[hh] += gb
            yield
            dvn = duv[rows, sw:]
            gsg_ref[...] += jnp.sum(dvn * vhat, axis=0, keepdims=True)
            dxh = dvn * sg_ref[...]
            dv = rstd * (dxh - jnp.mean(dxh, axis=-1, keepdims=True) - vhat * jnp.mean(dxh * vhat, axis=-1, keepdims=True))
            gp = _gelu_grad(pre, cdf)
            dproj_ref[rows, pw:pw + sw] = (duv[rows, :sw] * gp[:, :sw]).astype(_MXU)
            dproj_ref[rows, pw + sw:] = (dv * gp[:, sw:]).astype(_MXU)
            dxn = _nt(dproj_ref[rows, :], win_ref[...])
            yield
            xhat, r = _rms(h_ref[rows, :])
            dx, gg = _rms_bwd(dxn, xhat, r, g_ref[...])
            gg_ref[...] += gg
            dh_ref[rows, :] = dh1v + dx

        _in_turns([part(at) for at in reversed(range(0, tb, pt))])

    rev = lambda w: pl.BlockSpec((tb, w), lambda i: (nb - 1 - i, 0))
    halo = pl.BlockSpec((HALO, pw), lambda i: (jnp.maximum((nb - 1 - i) * (tb // HALO) - 1, 0), 0))
    small = [(1, d), (ng, GROUP, GROUP), (1, pw), (1, sw), (nh, GROUP, GROUP), (nh, GROUP, GROUP)]
    return _call_hosting(
        body, hosted, name="mixer_bwd", steps=nb,
        in_specs=[rev(d), rev(d), rev(proj.shape[1]), halo, VM, VM, VM, VM, VM, VM, VM, VM],
        out_specs=[rev(d), rev(proj.shape[1])] + [_const_spec(s) for s in small],
        out_shape=[jax.ShapeDtypeStruct((t, d), F32), jax.ShapeDtypeStruct(proj.shape, _MXU)]
        + [jax.ShapeDtypeStruct(s, F32) for s in small],
        scratch_shapes=[pltpu.VMEM((HALO, pw), F32), pltpu.VMEM((tb, 2 * sw), F32)],
        operands=(dh1, h, proj, proj, g, w_in, pool_w, pool_scale, sgu_g, sgu_w, sgu_bt, w_out))


def _kv_fwd(mem, gm, wk, wv):
    n, d = mem.shape

    def body(mem_ref, gm_ref, wk_ref, wv_ref, k_ref, v_ref, memn_ref):
        xhat, _ = _rms(mem_ref[...])
        memn = (xhat * gm_ref[...]).astype(_MXU)
        memn_ref[...] = memn
        k_ref[...] = _nn(memn, wk_ref[...]).astype(_MXU)
        v_ref[...] = _nn(memn, wv_ref[...]).astype(_MXU)

    return pl.pallas_call(
        body, name="kv_fwd", in_specs=[VM] * 4, out_specs=[VM] * 3,
        out_shape=[jax.ShapeDtypeStruct((n, d), _MXU)] * 3, compiler_params=_params(),
    )(mem, gm, wk, wv)


def _kv_bwd(dk, dv, mem, wk, wv):
    n, d = mem.shape

    def body(dk_ref, dv_ref, mem_ref, wk_ref, wv_ref, ggm_ref):
        dmemn = _nt(dk_ref[...].astype(_MXU), wk_ref[...]) + _nt(dv_ref[...].astype(_MXU), wv_ref[...])
        xhat, _ = _rms(mem_ref[...])
        ggm_ref[...] = jnp.sum(dmemn * xhat, axis=0, keepdims=True)

    return pl.pallas_call(
        body, name="kv_bwd", in_specs=[VM] * 5, out_specs=VM,
        out_shape=jax.ShapeDtypeStruct((1, d), F32), compiler_params=_params(),
    )(dk, dv, mem, wk, wv)


def _softmax(s):
    e = jnp.exp(s - jnp.max(s, axis=-1, keepdims=True))
    return e / jnp.sum(e, axis=-1, keepdims=True)


def _one_ahead(n, issue):
    nxt = issue(0)
    for a in range(n):
        cur = nxt
        if a + 1 < n:
            nxt = issue(a + 1)
        yield a, cur


def _xattn_fwd(h, g, wq, k, v, wo, hosted=None):
    t, d = h.shape
    hd = d // N_XHEADS
    scale = hd ** -0.5
    tb = _token_block(t, 512)

    def body(h_ref, g_ref, wq_ref, k_ref, v_ref, wo_ref, h2_ref, q_ref, o_ref, xn_ref):
        x = h_ref[...]
        xhat, _ = _rms(x)
        xn = (xhat * g_ref[...]).astype(_MXU)
        xn_ref[...] = xn
        qm = _nn(xn, wq_ref[...]).astype(_MXU)
        q_ref[...] = qm
        heads = [slice(a * hd, (a + 1) * hd) for a in range(N_XHEADS)]
        for a, s in _one_ahead(N_XHEADS, lambda a: _nt(qm[:, heads[a]], k_ref[:, heads[a]]) * scale):
            o_ref[:, heads[a]] = _nn(_softmax(s).astype(_MXU), v_ref[:, heads[a]]).astype(_MXU)
        h2_ref[...] = x + _nn(o_ref[...], wo_ref[...])

    blk = pl.BlockSpec((tb, d), lambda i: (i, 0))
    return _call_hosting(
        body, hosted, name="xattn_fwd", steps=t // tb,
        in_specs=[blk, VM, VM, VM, VM, VM], out_specs=[blk] * 4,
        out_shape=[jax.ShapeDtypeStruct((t, d), F32)] + [jax.ShapeDtypeStruct((t, d), _MXU)] * 3,
        scratch_shapes=[], operands=(h, g, wq, k, v, wo))


def _xattn_bwd(dh2, h, q, g, wq, k, v, wo, hosted=None):
    t, d = h.shape
    n = k.shape[0]
    hd = d // N_XHEADS
    scale = hd ** -0.5
    tb = _token_block(t, 512)
    pt = tb // 2 if tb % (2 * GROUP) == 0 else tb

    def body(dh2_ref, h_ref, q_ref, g_ref, wq_ref, k_ref, v_ref, wo_ref, dh_ref, dq_ref, dk_ref, dv_ref, gg_ref):
        @pl.when(pl.program_id(0) == 0)
        def _():
            for r in (dk_ref, dv_ref, gg_ref):
                r[...] = jnp.zeros(r.shape, F32)

        heads = [slice(a * hd, (a + 1) * hd) for a in range(N_XHEADS)]

        def part(at):
            rows = slice(at, at + pt)
            dh2v = dh2_ref[rows, :]
            dom = _nt(dh2v.astype(_MXU), wo_ref[...]).astype(_MXU)
            yield
            issue = lambda a: (_nt(q_ref[rows, heads[a]], k_ref[:, heads[a]]) * scale, _nt(dom[:, heads[a]], v_ref[:, heads[a]]))
            for a, (s, dpr) in _one_ahead(N_XHEADS, issue):
                sl = heads[a]
                pr = _softmax(s)
                dv_ref[:, sl] += _tn(pr.astype(_MXU), dom[:, sl])
                ds = (pr * (dpr - jnp.sum(dpr * pr, axis=-1, keepdims=True)) * scale).astype(_MXU)
                dq_ref[rows, sl] = _nn(ds, k_ref[:, sl]).astype(_MXU)
                dk_ref[:, sl] += _tn(ds, q_ref[rows, sl])
                yield
            dxn = _nt(dq_ref[rows, :], wq_ref[...])
            yield
            xhat, r = _rms(h_ref[rows, :])
            dx, gg = _rms_bwd(dxn, xhat, r, g_ref[...])
            gg_ref[...] += gg
            dh_ref[rows, :] = dh2v + dx

        _in_turns([part(at) for at in range(0, tb, pt)])

    blk = pl.BlockSpec((tb, d), lambda i: (i, 0))
    return _call_hosting(
        body, hosted, name="xattn_bwd", steps=t // tb,
        in_specs=[blk, blk, blk, VM, VM, VM, VM, VM],
        out_specs=[blk, blk, _const_spec((n, d)), _const_spec((n, d)), _const_spec((1, d))],
        out_shape=[jax.ShapeDtypeStruct((t, d), F32), jax.ShapeDtypeStruct((t, d), _MXU),
                   jax.ShapeDtypeStruct((n, d), F32), jax.ShapeDtypeStruct((n, d), F32), jax.ShapeDtypeStruct((1, d), F32)],
        scratch_shapes=[], operands=(dh2, h, q, g, wq, k, v, wo))


def _ffn_fwd(h, g, w_up, conv_w, conv_b, w_down, hosted=None):
    t, d = h.shape
    f = w_down.shape[0]
    ft = FF_TILE
    tb = _token_block(t, 512)

    def body(h_ref, g_ref, wup_ref, cw_ref, cb_ref, wdown_ref, h3_ref, hh_ref, hc_ref, ext, carry, act_sc):
        @pl.when(pl.program_id(0) == 0)
        def _():
            carry[...] = jnp.zeros(carry.shape, F32)

        x = h_ref[...]
        xhat, _ = _rms(x)
        xn = (xhat * g_ref[...]).astype(_MXU)
        acc = jnp.zeros((tb, d), F32)
        up = lambda j: [_nn(xn, wup_ref[:, off:off + ft]) for off in (j * ft, f + j * ft)]
        up_next = up(0)
        for j in range(f // ft):
            hc = []
            up_cur = up_next
            if j + 1 < f // ft:
                up_next = up(j + 1)
            for part, off in enumerate((j * ft, f + j * ft)):
                cols = slice(off, off + ft)
                cur = up_cur[part]
                hh_ref[:, cols] = cur.astype(_MXU)
                ext[part, 0:8, :] = carry[:, cols]
                ext[part, 8:8 + tb, :] = cur
                carry[:, cols] = cur[tb - 8:tb, :]
                hc.append(cb_ref[:, cols] + cw_ref[0:1, cols] * ext[part, 6:6 + tb, :]
                          + cw_ref[1:2, cols] * ext[part, 7:7 + tb, :] + cw_ref[2:3, cols] * cur)
                hc_ref[:, cols] = hc[part].astype(_MXU)
            at = j % DOWN_TILES
            act_sc[:, at * ft:(at + 1) * ft] = (hc[0] * jax.nn.sigmoid(hc[0]) * hc[1]).astype(_MXU)
            if at + 1 == DOWN_TILES or j + 1 == f // ft:
                acc = acc + _nn(act_sc[:, 0:(at + 1) * ft], wdown_ref[(j - at) * ft:(j + 1) * ft, :])
        h3_ref[...] = x + acc

    blk = lambda w: pl.BlockSpec((tb, w), lambda i: (i, 0))
    return _call_hosting(
        body, hosted, name="ffn_fwd", steps=t // tb,
        in_specs=[blk(d), VM, VM, VM, VM, VM], out_specs=[blk(d), blk(2 * f), blk(2 * f)],
        out_shape=[jax.ShapeDtypeStruct((t, d), F32), jax.ShapeDtypeStruct((t, 2 * f), _MXU), jax.ShapeDtypeStruct((t, 2 * f), _MXU)],
        scratch_shapes=[pltpu.VMEM((2, 8 + tb, ft), F32), pltpu.VMEM((8, 2 * f), F32), pltpu.VMEM((tb, DOWN_TILES * ft), _MXU)],
        operands=(h, g, w_up, conv_w, conv_b, w_down))


def _ffn_bwd(dh3, h, hh, hc, g, w_up, conv_w, w_down, hosted=None):
    t, d = h.shape
    f = w_down.shape[0]
    ft = FF_TILE
    tb = _token_block(t, 256)
    nb = t // tb

    def body(dh3_ref, h_ref, hh_ref, hc_ref, g_ref, wup_ref, cw_ref, wdown_ref,
             dh_ref, dhh_ref, act_ref, xn_ref, gcw_ref, gcb_ref, gg_ref, dcarry):
        @pl.when(pl.program_id(0) == 0)
        def _():
            for r in (gcw_ref, gcb_ref, gg_ref, dcarry):
                r[...] = jnp.zeros(r.shape, F32)

        dh3v = dh3_ref[...]
        dhm = dh3v.astype(_MXU)
        dxn = jnp.zeros((tb, d), F32)
        dact_next = _nt(dhm, wdown_ref[0:ft, :])
        for j in range(f // ft):
            dact = dact_next
            if j + 1 < f // ft:
                dact_next = _nt(dhm, wdown_ref[(j + 1) * ft:(j + 2) * ft, :])
            gate = hc_ref[:, j * ft:(j + 1) * ft].astype(F32)
            val = hc_ref[:, f + j * ft:f + (j + 1) * ft].astype(F32)
            sg = jax.nn.sigmoid(gate)
            silu = gate * sg
            act_ref[:, j * ft:(j + 1) * ft] = (silu * val).astype(_MXU)
            dhc = (dact * val * sg * (1.0 + gate * (1.0 - sg)), dact * silu)
            for part, off in enumerate((j * ft, f + j * ft)):
                cols = slice(off, off + ft)
                dc = dhc[part]
                c0 = hh_ref[:, cols].astype(F32)
                after = dcarry[:, cols]
                ahead1 = _shift_rows(dc, -1, after)
                ahead2 = _shift_rows(dc, -2, after)
                dcarry[:, cols] = dc[0:8, :]
                gcb_ref[:, cols] += jnp.sum(dc, axis=0, keepdims=True)
                gcw_ref[0:1, cols] += jnp.sum(ahead2 * c0, axis=0, keepdims=True)
                gcw_ref[1:2, cols] += jnp.sum(ahead1 * c0, axis=0, keepdims=True)
                gcw_ref[2:3, cols] += jnp.sum(dc * c0, axis=0, keepdims=True)
                dhh = (cw_ref[2:3, cols] * dc + cw_ref[1:2, cols] * ahead1 + cw_ref[0:1, cols] * ahead2).astype(_MXU)
                dhh_ref[:, cols] = dhh
                dxn = dxn + _nt(dhh, wup_ref[:, cols])
        xhat, r = _rms(h_ref[...])
        xn_ref[...] = (xhat * g_ref[...]).astype(_MXU)
        dx, gg = _rms_bwd(dxn, xhat, r, g_ref[...])
        gg_ref[...] += gg
        dh_ref[...] = dh3v + dx

    rev = lambda w: pl.BlockSpec((tb, w), lambda i: (nb - 1 - i, 0))
    return _call_hosting(
        body, hosted, name="ffn_bwd", steps=nb,
        in_specs=[rev(d), rev(d), rev(2 * f), rev(2 * f), VM, VM, VM, VM],
        out_specs=[rev(d), rev(2 * f), rev(f), rev(d), _const_spec((3, 2 * f)), _const_spec((1, 2 * f)), _const_spec((1, d))],
        out_shape=[jax.ShapeDtypeStruct((t, d), F32), jax.ShapeDtypeStruct((t, 2 * f), _MXU), jax.ShapeDtypeStruct((t, f), _MXU),
                   jax.ShapeDtypeStruct((t, d), _MXU),
                   jax.ShapeDtypeStruct((3, 2 * f), F32), jax.ShapeDtypeStruct((1, 2 * f), F32), jax.ShapeDtypeStruct((1, d), F32)],
        scratch_shapes=[pltpu.VMEM((8, 2 * f), F32)],
        operands=(dh3, h, hh, hc, g, w_up, conv_w, w_down))


def _loss_head(h, g, target):
    t, d = h.shape
    tb = _token_block(t, 512)

    def body(h_ref, g_ref, tgt_ref, dh_ref, loss_ref, gg_ref):
        @pl.when(pl.program_id(0) == 0)
        def _():
            loss_ref[...] = jnp.zeros(loss_ref.shape, F32)
            gg_ref[...] = jnp.zeros(gg_ref.shape, F32)

        xhat, r = _rms(h_ref[...])
        err = xhat * g_ref[...] - tgt_ref[...]
        loss_ref[...] += 0.5 * jnp.sum(jnp.sum(err * err, axis=-1, keepdims=True), axis=0, keepdims=True) / d
        dx, gg = _rms_bwd(err / d, xhat, r, g_ref[...])
        gg_ref[...] += gg
        dh_ref[...] = dx

    blk = pl.BlockSpec((tb, d), lambda i: (i, 0))
    return pl.pallas_call(
        body, name="loss_head", grid=(t // tb,),
        in_specs=[blk, VM, blk], out_specs=[blk, _const_spec((1, 1)), _const_spec((1, d))],
        out_shape=[jax.ShapeDtypeStruct((t, d), F32), jax.ShapeDtypeStruct((1, 1), F32), jax.ShapeDtypeStruct((1, d), F32)],
        compiler_params=_params(("arbitrary",)),
    )(h, g, target)


def _largest_tile(n, cap, mult=128):
    best = None
    for c in range(mult, min(n, cap) + 1, mult):
        if n % c == 0:
            best = c
    return best if best is not None else n


def _grad_matmul(a, b, name, layer, n_layers, into=None, hosted=None):
    t, m = a.shape
    n = b.shape[1]
    tm, tn, tk = _largest_tile(m, 1408), _largest_tile(n, 1408), _largest_tile(t, 1024)
    nk = t // tk

    def body(a_ref, b_ref, *rest):
        o_ref = rest[-1]

        @pl.when(pl.program_id(2) == 0)
        def _():
            o_ref[...] = jnp.zeros(o_ref.shape, F32)

        o_ref[...] += _tn(a_ref[...].astype(_MXU), b_ref[...].astype(_MXU))

    in_specs = [pl.BlockSpec((tk, tm), lambda i, j, k: (k, i)), pl.BlockSpec((tk, tn), lambda i, j, k: (k, j))]
    operands = (a, b)
    aliases = {}
    if into is not None:
        in_specs.append(pl.BlockSpec(memory_space=pl.ANY))
        operands = (a, b, into)
        aliases = {2: 0}
    (out,), got = _call_hosting(
        body, hosted, name=name, steps=(m // tm, n // tn, nk), in_specs=in_specs,
        out_specs=[pl.BlockSpec((None, tm, tn), lambda i, j, k: (layer, i, j))],
        out_shape=[jax.ShapeDtypeStruct((n_layers, m, n), F32)], scratch_shapes=[], operands=operands, aliases=aliases)
    return out, got


def _adamw_math(w, g, m, v):
    m = ADAM_B1 * m + (1.0 - ADAM_B1) * g
    v = ADAM_B2 * v + (1.0 - ADAM_B2) * (g * g)
    m_hat = m / (1.0 - ADAM_B1 ** ADAM_STEP)
    v_hat = v / (1.0 - ADAM_B2 ** ADAM_STEP)
    return -ADAM_LR * (m_hat / (jnp.sqrt(v_hat) + ADAM_EPS) + ADAM_WD * w), m, v


def _row_block(rows, cols, max_bytes=1 << 20, mult=16):
    best = None
    for r in range(mult, rows + 1, mult):
        if rows % r == 0 and r * cols * 4 <= max_bytes:
            best = r
    return best if best is not None else rows


def _adamw_big(ws, gs, ms, vs, name):
    n = len(ws)
    shape = ws[0].shape
    cols = shape[-1]
    flat = lambda a: a.reshape(-1, cols)
    rows = flat(ws[0]).shape[0]
    rb = _row_block(rows, cols, (2 << 20) // n)

    def body(*refs):
        for a in range(n):
            w_ref, g_ref, m_ref, v_ref = (refs[s * n + a] for s in range(4))
            go_ref, d_ref, nm_ref, nv_ref = (refs[(4 + s) * n + a] for s in range(4))
            g = g_ref[...]
            go_ref[...] = g
            d_ref[...], nm_ref[...], nv_ref[...] = _adamw_math(w_ref[...], g, m_ref[...], v_ref[...])

    blk = pl.BlockSpec((rb, cols), lambda i: (i, 0))
    outs = pl.pallas_call(
        body, name=name, grid=(rows // rb,), in_specs=[blk] * (4 * n), out_specs=[blk] * (4 * n),
        out_shape=[jax.ShapeDtypeStruct((rows, cols), F32)] * (4 * n), compiler_params=_params(("parallel",)),
    )(*[flat(a) for group in (ws, gs, ms, vs) for a in group])
    return [[outs[s * n + a].reshape(shape) for s in range(4)] for a in range(n)]


def _adamw_small(ws, gs, ms, vs):
    n = len(ws)

    def body(*refs):
        for a in range(n):
            w_ref, g_ref, m_ref, v_ref = (refs[s * n + a] for s in range(4))
            d_ref, nm_ref, nv_ref = (refs[(4 + s) * n + a] for s in range(3))
            d_ref[...], nm_ref[...], nv_ref[...] = _adamw_math(w_ref[...], g_ref[...], m_ref[...], v_ref[...])

    outs = pl.pallas_call(
        body, name="adamw_small", in_specs=[VM] * (4 * n), out_specs=[VM] * (3 * n),
        out_shape=[jax.ShapeDtypeStruct(w.shape, F32) for w in ws] * 3, compiler_params=_params(),
    )(*ws, *gs, *ms, *vs)
    return outs[:n], outs[n:2 * n], outs[2 * n:]


def _place():
    x, y, c = lax.axis_index("x"), lax.axis_index("y"), lax.axis_index("c")
    chips = [(1 - x, y), (x, 1 - y), (1 - x, 1 - y)]
    return x, y, c, chips


def _rows(start, size, mult=16):
    return pl.ds(pl.multiple_of(start, mult), size)


def _full_window(ref, axis, chip, half=None):
    r, c = ref.shape
    if axis == 0:
        rs = r // 4
        if half is None:
            return ref.at[_rows(chip * rs, rs), :]
        return ref.at[_rows(chip * rs + half * (rs // 2), rs // 2), :]
    cs = c // 4
    if half is None:
        return ref.at[:, _rows(chip * cs, cs, 128)]
    return ref.at[_rows(half * (r // 2), r // 2), _rows(chip * cs, cs, 128)]


def _remote(src, dst, send_sem, recv_sem, to):
    return pltpu.make_async_remote_copy(src_ref=src, dst_ref=dst, send_sem=send_sem, recv_sem=recv_sem,
                                        device_id=to, device_id_type=MESH)


def _scalars(*vals):
    return jnp.stack([jnp.asarray(v, jnp.int32) for v in vals])


def _cast_place(shards, axis, chip, name):
    n = len(shards)
    nl, rs, cs = shards[0].shape
    full = (rs * 4, cs) if axis == 0 else (rs, cs * 4)
    rb = _row_block(rs, cs, (4 << 20) // (n * nl))
    nrb = rs // rb

    def body(chip_ref, *refs):
        for a in range(n):
            for l in range(nl):
                refs[n + a * nl + l][...] = refs[a][l].astype(_PAY)

    if axis == 0:
        out_map = lambda i, chip_ref: (chip_ref[0] * nrb + i, 0)
    else:
        out_map = lambda i, chip_ref: (i, chip_ref[0])
    outs = pl.pallas_call(
        body, name=name,
        grid_spec=pltpu.PrefetchScalarGridSpec(
            num_scalar_prefetch=1, grid=(nrb,),
            in_specs=[pl.BlockSpec((nl, rb, cs), lambda i, chip_ref: (0, i, 0))] * n,
            out_specs=[pl.BlockSpec((rb, cs), out_map)] * (n * nl)),
        out_shape=[jax.ShapeDtypeStruct(full, _PAY)] * (n * nl), compiler_params=_params(("parallel",)),
    )(_scalars(chip), *shards)
    return [[outs[a * nl + l] for l in range(nl)] for a in range(n)]


def _hosted_allgather(placed, axes):
    n = len(placed)

    def each(outs, half_of):
        x, y, c, chips = _place()
        for i in range(n):
            for k, chip in enumerate(chips):
                yield i * 3 + k, (*chip, c), (x, y, 1 - c), _full_window(outs[i], axes[i], 2 * x + y, c), \
                    _full_window(outs[i], axes[i], 2 * chip[0] + chip[1], half_of(c))

    def start(_, outs, sems):
        send, recv, _, _ = sems
        for s, peer, _, mine, _ in each(outs, lambda c: c):
            _remote(mine, mine, send.at[s], recv.at[s], peer).start()

    def middle(_, outs, sems):
        send, recv, fsend, frecv = sems
        for s, _, sibling, _, got in each(outs, lambda c: c):
            _remote(got, got, send.at[s], recv.at[s], sibling).wait_recv()
            _remote(got, got, fsend.at[s], frecv.at[s], sibling).start()

    def finish(_, outs, sems):
        send, recv, fsend, frecv = sems
        for s, _, sibling, _, got in each(outs, lambda c: 1 - c):
            _remote(got, got, fsend.at[s], frecv.at[s], sibling).wait_recv()
        for s, peer, sibling, mine, got in each(outs, lambda c: c):
            _remote(mine, mine, send.at[s], recv.at[s], peer).wait_send()
            _remote(got, got, fsend.at[s], frecv.at[s], sibling).wait_send()

    return _Hosted(tuple(placed), True, (), (pltpu.SemaphoreType.DMA((n * 3,)),) * 4, (start, middle, finish))


def _allgather_conv(conv_shard):
    nl, taps, cs = conv_shard.shape

    def body(in_ref, out_ref, send, recv, local):
        x, y, c, chips = _place()
        mine = out_ref.at[:, :, _rows((2 * x + y) * cs, cs, 128)]
        own = pltpu.make_async_copy(in_ref, mine, local)
        own.start()
        sends = [_remote(in_ref, mine, send.at[k], recv.at[k], (*chip, c)) for k, chip in enumerate(chips)]
        for cp in sends:
            cp.start()
        for k, chip in enumerate(chips):
            got = out_ref.at[:, :, _rows((2 * chip[0] + chip[1]) * cs, cs, 128)]
            _remote(got, got, send.at[k], recv.at[k], (*chip, c)).wait_recv()
        for cp in sends:
            cp.wait_send()
        own.wait()

    return pl.pallas_call(
        body, name="allgather_conv", in_specs=[HB], out_specs=HB, out_shape=jax.ShapeDtypeStruct((nl, taps, cs * 4), conv_shard.dtype),
        scratch_shapes=[pltpu.SemaphoreType.DMA((3,)), pltpu.SemaphoreType.DMA((3,)), pltpu.SemaphoreType.DMA],
        compiler_params=pltpu.CompilerParams(has_side_effects=True),
    )(conv_shard)


def _hosted_exchange(grads, axes, layer):
    na = len(grads)
    views = [g.reshape(g.shape[0], 4, 2, g.shape[1] // 8, g.shape[2]) if ax == 0 else g for g, ax in zip(grads, axes)]

    def region(ref, axis, half):
        if axis == 0:
            return ref.at[layer, :, half]
        r = ref.shape[1]
        return ref.at[layer, _rows(half * (r // 2), r // 2), :]

    def copies(ins, land, sems):
        send, recv = sems
        x, y, c, _ = _place()
        return [_remote(region(ins[a], axes[a], 1 - c), land[a], send.at[a], recv.at[a], (x, y, 1 - c)) for a in range(na)]

    def start(ins, land, sems):
        for cp in copies(ins, land, sems):
            cp.start()

    def finish(ins, land, sems):
        for cp in copies(ins, land, sems):
            cp.wait()

    shapes = [(4, g.shape[1] // 8, g.shape[2]) if ax == 0 else (g.shape[1] // 2, g.shape[2]) for g, ax in zip(grads, axes)]
    return _Hosted(tuple(views), False, tuple(jax.ShapeDtypeStruct(s, F32) for s in shapes),
                   (pltpu.SemaphoreType.DMA((na,)),) * 2, (start, None, finish))


def _add_cast(mines, theirs, core, base, name):
    n = len(mines)
    na, nb, cols = theirs[0].shape
    rb = _row_block(nb, cols, (4 << 20) // n)

    def body(core_ref, *refs):
        for a in range(n):
            refs[2 * n + a][...] = (refs[a][...] + refs[n + a][...]).astype(_PAY)

    blk = pl.BlockSpec((None, rb, cols), lambda i, k, core_ref: (i, k, 0))
    return pl.pallas_call(
        body, name=name,
        grid_spec=pltpu.PrefetchScalarGridSpec(
            num_scalar_prefetch=1, grid=(na, nb // rb),
            in_specs=[pl.BlockSpec((None, None, rb, cols), lambda i, k, core_ref: (base + i, core_ref[0], k, 0))] * n + [blk] * n,
            out_specs=[blk] * n),
        out_shape=[jax.ShapeDtypeStruct((na, nb, cols), _PAY)] * n, compiler_params=_params(("parallel", "parallel")),
    )(_scalars(core), *mines, *theirs)


def _piece(ref, axis, chip):
    if axis == 0:
        return ref.at[chip]
    cs = ref.shape[1] // 4
    return ref.at[:, _rows(chip * cs, cs, 128)]


def _hosted_scatter(sums, axes):
    na = len(sums)

    def piece_shape(a):
        if axes[a] == 0:
            return (sums[a].shape[1], sums[a].shape[2])
        return (sums[a].shape[0], sums[a].shape[1] // 4)

    def copies(ins, slots, sems):
        send, recv = sems
        _, _, c, chips = _place()
        return [_remote(_piece(ins[a], axes[a], 2 * chip[0] + chip[1]), slots[a].at[k], send.at[a * 3 + k], recv.at[a * 3 + k], (*chip, c))
                for a in range(na) for k, chip in enumerate(chips)]

    def start(ins, slots, sems):
        for cp in copies(ins, slots, sems):
            cp.start()

    def finish(ins, slots, sems):
        for cp in copies(ins, slots, sems):
            cp.wait()

    return _Hosted(tuple(sums), False, tuple(jax.ShapeDtypeStruct((3,) + piece_shape(a), sums[a].dtype) for a in range(na)),
                   (pltpu.SemaphoreType.DMA((na * 3,)),) * 2, (start, None, finish))


def _sum_slots(sums, slots, axis, chip, core, layer, n_layers, name, into=None):
    _, hr, cs = slots.shape
    rb = _row_block(hr, cs, 4 << 20)

    def body(at_ref, own_ref, s_ref, *rest):
        rest[-1][...] = ((own_ref[...].astype(F32) + s_ref[0].astype(F32)) + s_ref[1].astype(F32)) + s_ref[2].astype(F32)

    if axis == 0:
        own = pl.BlockSpec((None, rb, cs), lambda k, at_ref: (at_ref[0], k, 0))
    else:
        own = pl.BlockSpec((rb, cs), lambda k, at_ref: (k, at_ref[0]))
    in_specs = [own, pl.BlockSpec((3, rb, cs), lambda k, at_ref: (0, k, 0))]
    operands = (sums, slots)
    aliases = {}
    if into is not None:
        in_specs.append(pl.BlockSpec(memory_space=pl.ANY))
        operands = (sums, slots, into)
        aliases = {3: 0}
    return pl.pallas_call(
        body, name=name,
        grid_spec=pltpu.PrefetchScalarGridSpec(
            num_scalar_prefetch=1, grid=(hr // rb,), in_specs=in_specs,
            out_specs=pl.BlockSpec((None, None, rb, cs), lambda k, at_ref: (layer, at_ref[1], k, 0))),
        out_shape=jax.ShapeDtypeStruct((n_layers, 2, hr, cs), F32), input_output_aliases=aliases,
        compiler_params=_params(("parallel",)),
    )(_scalars(chip, core), *operands)


def _sibling_assemble(shards):
    na = len(shards)

    def body(*refs):
        outs = refs[na:2 * na]
        send, recv = refs[2 * na:]
        x, y, c, _ = _place()
        copies = []
        for a in range(na):
            hr = outs[a].shape[1] // 2
            mine = outs[a].at[:, _rows(c * hr, hr), :]
            cp = _remote(mine, mine, send.at[a], recv.at[a], (x, y, 1 - c))
            cp.start()
            copies.append(cp)
        for cp in copies:
            cp.wait()

    return pl.pallas_call(
        body, name="grad_sibling_assemble", in_specs=[HB] * na, out_specs=[HB] * na,
        out_shape=[jax.ShapeDtypeStruct(s.shape, F32) for s in shards], input_output_aliases={a: a for a in range(na)},
        scratch_shapes=[pltpu.SemaphoreType.DMA((na,))] * 2,
        compiler_params=pltpu.CompilerParams(has_side_effects=True),
    )(*shards)


def _allreduce_small(buf, hosted):
    rows, w = buf.shape
    half = rows // 2
    nh, h_shapes = len(hosted.operands), _hosted_results(hosted)

    def body(buf_ref, *refs):
        h_in, out_ref, h_out = refs[:nh], refs[nh], refs[nh + 1:nh + 1 + len(h_shapes)]
        land, slots, red, sems_send, sems_recv = refs[nh + 1 + len(h_shapes):nh + 6 + len(h_shapes)]
        h_sems = refs[nh + 6 + len(h_shapes):]
        hosted.stages[0](h_in, h_out, h_sems)
        x, y, c, chips = _place()
        me = 2 * x + y
        sibling = (x, y, 1 - c)
        first = _remote(buf_ref, land, sems_send.at[0], sems_recv.at[0], sibling)
        first.start()
        first.wait()
        mine = pl.ds(pl.multiple_of(c * half, 8), half)
        slots[me] = buf_ref[mine, :] + land[mine, :]
        sends = []
        for k, chip in enumerate(chips):
            cp = _remote(slots.at[me], slots.at[me], sems_send.at[1 + k], sems_recv.at[1 + k], (*chip, c))
            cp.start()
            sends.append(cp)
        for k, chip in enumerate(chips):
            got = slots.at[2 * chip[0] + chip[1]]
            _remote(got, got, sems_send.at[1 + k], sems_recv.at[1 + k], sibling).wait_recv()
        red[...] = ((slots[0] + slots[1]) + slots[2]) + slots[3]
        out_ref[mine, :] = red[...]
        last = _remote(red, out_ref.at[mine, :], sems_send.at[4], sems_recv.at[4], sibling)
        last.start()
        theirs = out_ref.at[pl.ds(pl.multiple_of((1 - c) * half, 8), half), :]
        _remote(red, theirs, sems_send.at[4], sems_recv.at[4], sibling).wait_recv()
        for cp in sends:
            cp.wait_send()
        last.wait_send()
        hosted.stages[2](h_in, h_out, h_sems)

    outs = pl.pallas_call(
        body, name="allreduce_small", in_specs=[VM] + [HB] * nh, out_specs=[VM] + [HB] * len(h_shapes),
        out_shape=[jax.ShapeDtypeStruct((rows, w), F32)] + h_shapes,
        scratch_shapes=[pltpu.VMEM((rows, w), F32), pltpu.VMEM((4, half, w), F32), pltpu.VMEM((half, w), F32),
                        pltpu.SemaphoreType.DMA((5,)), pltpu.SemaphoreType.DMA((5,))] + list(hosted.sems),
        compiler_params=pltpu.CompilerParams(has_side_effects=True, vmem_limit_bytes=VMEM_LIMIT),
    )(buf, *hosted.operands)
    return outs[0], outs[1:]


BIG = ("w_in", "w_out", "wq", "wk", "wv", "wo", "w_up", "w_down")
MIXER, ATTN, MLP = ("w_in", "w_out"), ("wq", "wk", "wv", "wo"), ("w_up", "w_down")
BIG_AXIS = {"w_in": 1, "w_out": 0, "wq": 0, "wk": 0, "wv": 0, "wo": 0, "w_up": 1, "w_down": 0}
SMALL = ("norm_mix_g", "pool_w", "pool_scale", "sgu_g", "sgu_w", "sgu_b", "norm_xattn_g", "mem_norm_g", "norm_ffn_g",
         "conv_w", "conv_b", "final_norm_g")
ORDER = ("norm_mix_g", "w_in", "pool_w", "pool_scale", "sgu_g", "sgu_w", "sgu_b", "w_out", "norm_xattn_g", "mem_norm_g",
         "wq", "wk", "wv", "wo", "norm_ffn_g", "w_up", "conv_w", "conv_b", "w_down", "final_norm_g")
PACK_WIDTH = 512


def kernel(x, mem, norm_mix_g, w_in, pool_w, pool_scale, sgu_g, sgu_w, sgu_b, w_out, norm_xattn_g, mem_norm_g, wq, wk, wv, wo, norm_ffn_g, w_up, conv_w, conv_b, w_down, final_norm_g, loss_target, m_norm_mix_g, m_w_in, m_pool_w, m_pool_scale, m_sgu_g, m_sgu_w, m_sgu_b, m_w_out, m_norm_xattn_g, m_mem_norm_g, m_wq, m_wk, m_wv, m_wo, m_norm_ffn_g, m_w_up, m_conv_w, m_conv_b, m_w_down, m_final_norm_g, v_norm_mix_g, v_w_in, v_pool_w, v_pool_scale, v_sgu_g, v_sgu_w, v_sgu_b, v_w_out, v_norm_xattn_g, v_mem_norm_g, v_wq, v_wk, v_wv, v_wo, v_norm_ffn_g, v_w_up, v_conv_w, v_conv_b, v_w_down, v_final_norm_g):
    given = dict(locals())
    w = {n: given[n] for n in ORDER}
    mom = {n: given["m_" + n] for n in ORDER}
    var = {n: given["v_" + n] for n in ORDER}
    nl = w_in.shape[0]
    xs, mems, tgt = x[0], mem[0], loss_target[0]
    chip = 2 * lax.axis_index("x") + lax.axis_index("y")
    core = lax.axis_index("c")

    axes_of = lambda names: [BIG_AXIS[n] for n in names]
    alike = {}
    for n in BIG:
        alike.setdefault((w[n].shape, BIG_AXIS[n]), []).append(n)
    placed = [{} for _ in range(nl)]
    for (_, axis), names in alike.items():
        for n, per_layer in zip(names, _cast_place([w[n] for n in names], axis, chip, "place_" + names[0])):
            for l in range(nl):
                placed[l][n] = per_layer[l]
    conv_full = _allgather_conv(conv_w)

    def gather(names, l):
        return _hosted_allgather([placed[l][n] for n in names], axes_of(names))

    full = [dict(zip(MIXER, _run_hosted(gather(MIXER, 0), "allgather_weights")))]

    row = lambda a, l: a[l][None, :]
    saved = []
    h = xs
    for l in range(nl):
        fw = full[l]
        sbt = jnp.broadcast_to(sgu_b[l][:, :, None], sgu_w[l].shape)
        (h1, proj, xn1, mix), got = _mixer_fwd(h, row(norm_mix_g, l), fw["w_in"], pool_w[l], row(pool_scale, l), row(sgu_g, l), sgu_w[l], sbt, fw["w_out"],
                                               [gather(ATTN, 0), gather(("w_down",), 0)] if l == 0 else None)
        if l == 0:
            fw.update(zip(ATTN, got[0]))
            fw["w_down"] = got[1][0]
        k, v, memn = _kv_fwd(mems, row(mem_norm_g, l), fw["wk"], fw["wv"])
        (h2, q, o, xn2), got = _xattn_fwd(h1, row(norm_xattn_g, l), fw["wq"], k, v, fw["wo"], [gather(("w_up",), 0)] if l == 0 else None)
        if l == 0:
            fw["w_up"] = got[0][0]
        (h3, hh, hc), got = _ffn_fwd(h2, row(norm_ffn_g, l), fw["w_up"], conv_full[l], row(conv_b, l), fw["w_down"],
                                      [gather(BIG, l + 1)] if l + 1 < nl else None)
        if l + 1 < nl:
            full.append(dict(zip(BIG, got[0])))
        saved.append(dict(h=h, h1=h1, h2=h2, proj=proj, xn1=xn1, mix=mix, k=k, v=v, memn=memn, q=q, o=o, xn2=xn2, hh=hh, hc=hc, sbt=sbt))
        h = h3

    dh, loss_part, g_final = _loss_head(h, final_norm_g[None, :], tgt)

    big_grads = {}
    small_grads = [None] * nl

    def weight_grad(n, a, b, l, hosted=None):
        big_grads[n], got = _grad_matmul(a, b, "grad_" + n, l, nl, big_grads.get(n), hosted)
        return got

    sums, slots = {}, {}

    def exchange(names, l):
        return _hosted_exchange([big_grads[n] for n in names], axes_of(names), l)

    def scatter(names, l):
        return _hosted_scatter([sums[n, l] for n in names], axes_of(names))

    def add_casts(names, theirs, l):
        theirs = dict(zip(names, theirs))
        for group in alike.values():
            group = [n for n in group if n in theirs]
            if not group:
                continue
            gl, gr, gc = big_grads[group[0]].shape
            if BIG_AXIS[group[0]] == 0:
                outs = _add_cast([big_grads[n].reshape(gl * 4, 2, gr // 8, gc) for n in group], [theirs[n] for n in group],
                                 core, l * 4, "grad_chip_sum_" + group[0])
            else:
                outs = [o[0] for o in _add_cast([big_grads[n].reshape(gl, 2, gr // 2, gc) for n in group], [theirs[n][None] for n in group],
                                                core, l, "grad_chip_sum_" + group[0])]
            for n, o in zip(group, outs):
                sums[n, l] = o

    def keep_slots(names, got, l):
        for n, sl in zip(names, got):
            slots[n, l] = sl

    for l in reversed(range(nl)):
        fw, s = full[l], saved[l]
        above = l + 1 < nl
        dh3 = dh
        (dh2, dhh, act, xn3, g_cw, g_cb, g_nf), got = _ffn_bwd(dh3, s["h2"], s["hh"], s["hc"], row(norm_ffn_g, l), fw["w_up"], conv_full[l], fw["w_down"],
                                                         [exchange(MIXER, l + 1), scatter(ATTN, l + 1)] if above else None)
        if above:
            add_casts(MIXER, got[0], l + 1)
            keep_slots(ATTN, got[1], l + 1)
        weight_grad("w_up", xn3, dhh, l)
        weight_grad("w_down", act, dh3, l)
        (dh1, dq, dk, dv, g_nx), got = _xattn_bwd(dh2, s["h1"], s["q"], row(norm_xattn_g, l), fw["wq"], s["k"], s["v"], fw["wo"],
                                                  [exchange(MLP, l), scatter(MIXER, l + 1) if above else None])
        add_casts(MLP, got[0], l)
        if above:
            keep_slots(MIXER, got[1], l + 1)
        weight_grad("wq", s["xn2"], dq, l)
        weight_grad("wo", s["o"], dh2, l)
        weight_grad("wk", s["memn"], dk, l)
        weight_grad("wv", s["memn"], dv, l)
        g_mn = _kv_bwd(dk, dv, mems, fw["wk"], fw["wv"])
        (dh0, dproj, g_nm, g_pw, g_ps, g_sg, g_sw, g_sbt), got = _mixer_bwd(dh1, s["h"], s["proj"], row(norm_mix_g, l), fw["w_in"], pool_w[l], row(pool_scale, l), row(sgu_g, l), sgu_w[l], s["sbt"], fw["w_out"],
                                                                           [scatter(MLP, l), exchange(ATTN, l)])
        keep_slots(MLP, got[0], l)
        add_casts(ATTN, got[1], l)
        got = weight_grad("w_in", s["xn1"], dproj, l, [scatter(ATTN, l)] if l == 0 else None)
        if l == 0:
            keep_slots(ATTN, got[0], l)
        weight_grad("w_out", s["mix"], dh1, l)
        small_grads[l] = dict(norm_mix_g=g_nm, pool_w=g_pw, pool_scale=g_ps, sgu_g=g_sg, sgu_w=g_sw, sgu_b=jnp.sum(g_sbt, axis=-1),
                              norm_xattn_g=g_nx, mem_norm_g=g_mn, norm_ffn_g=g_nf, conv_w=g_cw, conv_b=g_cb)
        dh = dh0
    grad_x = dh[None]

    add_casts(MIXER, _run_hosted(exchange(MIXER, 0), "grad_sibling_exchange"), 0)

    layered = [n for n in SMALL if n != "final_norm_g"]
    parts = [small_grads[l][n].reshape(-1, PACK_WIDTH) for n in layered for l in range(nl)]
    parts.append(g_final.reshape(-1, PACK_WIDTH))
    parts.append(jnp.pad(loss_part, ((0, 0), (0, PACK_WIDTH - 1))))
    used = sum(p.shape[0] for p in parts)
    total = -(-used // 16) * 16
    packed, got = _allreduce_small(jnp.concatenate(parts + [jnp.zeros((total - used, PACK_WIDTH), F32)], axis=0), scatter(MIXER, 0))
    keep_slots(MIXER, got, 0)

    halves = []
    for n in BIG:
        buf = None
        for l in range(nl):
            buf = _sum_slots(sums[n, l], slots[n, l], BIG_AXIS[n], chip, core, l, nl, "grad_sum_" + n, buf)
        halves.append(buf.reshape(nl, 2 * buf.shape[2], buf.shape[3]))
    shard_grads = dict(zip(BIG, _sibling_assemble(halves)))
    grads = dict(shard_grads)
    at = 0
    for n in layered:
        per_layer = []
        for l in range(nl):
            shape = small_grads[l][n].shape
            nrow = small_grads[l][n].size // PACK_WIDTH
            per_layer.append(packed[at:at + nrow].reshape(shape))
            at += nrow
        g = jnp.stack(per_layer)
        if n == "conv_w":
            cs = conv_w.shape[2]
            g = lax.dynamic_slice_in_dim(g, chip * cs, cs, axis=2)
        grads[n] = g.reshape(w[n].shape)
    grads["final_norm_g"] = packed[at:at + g_final.size // PACK_WIDTH].reshape(final_norm_g.shape)
    at += g_final.size // PACK_WIDTH
    loss = packed[at, 0]

    delta, new_m, new_v = {}, {}, {}
    for names in alike.values():
        outs = _adamw_big([w[n] for n in names], [grads[n] for n in names], [mom[n] for n in names], [var[n] for n in names], "adamw_" + names[0])
        for n, out in zip(names, outs):
            grads[n], delta[n], new_m[n], new_v[n] = out
    two_d = lambda a: a.reshape(-1, a.shape[-1])
    ds, nms, nvs = _adamw_small([two_d(w[n]) for n in SMALL], [two_d(grads[n]) for n in SMALL],
                                [two_d(mom[n]) for n in SMALL], [two_d(var[n]) for n in SMALL])
    for n, d_, m_, v_ in zip(SMALL, ds, nms, nvs):
        delta[n], new_m[n], new_v[n] = d_.reshape(w[n].shape), m_.reshape(w[n].shape), v_.reshape(w[n].shape)

    return (loss, grad_x, *[grads[n] for n in ORDER], *[delta[n] for n in ORDER], *[new_m[n] for n in ORDER], *[new_v[n] for n in ORDER])
```

```python
import math
from typing import NamedTuple

import jax
import jax.numpy as jnp
from jax import lax
from jax.experimental import pallas as pl
from jax.experimental.pallas import tpu as pltpu

F32 = jnp.float32
_MXU = jnp.bfloat16
_PAY = jnp.bfloat16
EPS = 1e-6
WINDOWS = (2, 4, 8, 16)
GROUP = 128
N_XHEADS = 4
HALO = 16
FF_TILE = 256
DOWN_TILES = 6
VMEM_LIMIT = 60 * 1024 * 1024
MESH = pl.DeviceIdType.MESH

ADAM_LR, ADAM_B1, ADAM_B2, ADAM_EPS, ADAM_WD, ADAM_STEP = 0.001, 0.9, 0.999, 1e-08, 0.01, 10

VM = pl.BlockSpec(memory_space=pltpu.VMEM)
HB = pl.BlockSpec(memory_space=pltpu.HBM)


def _nn(a, b):
    return jnp.dot(a, b, preferred_element_type=F32)


def _nt(a, b):
    return lax.dot_general(a, b, (((1,), (1,)), ((), ())), preferred_element_type=F32)


def _tn(a, b):
    return lax.dot_general(a, b, (((0,), (0,)), ((), ())), preferred_element_type=F32)


def _rms(x):
    r = lax.rsqrt(jnp.mean(x * x, axis=-1, keepdims=True) + EPS)
    return x * r, r


def _rms_bwd(dxn, xhat, r, g):
    dxh = dxn * g
    dx = r * (dxh - xhat * jnp.mean(dxh * xhat, axis=-1, keepdims=True))
    return dx, jnp.sum(dxn * xhat, axis=0, keepdims=True)


def _gelu(x):
    cdf = 0.5 * (1.0 + lax.erf(x * (2.0 ** -0.5)))
    return x * cdf, cdf


def _gelu_grad(x, cdf):
    return cdf + x * jnp.exp(-0.5 * x * x) * ((2.0 * math.pi) ** -0.5)


def _params(sem=None):
    return pltpu.CompilerParams(dimension_semantics=sem, vmem_limit_bytes=VMEM_LIMIT)


def _token_block(t, want):
    return want if t % want == 0 and t > want else GROUP


def _const_spec(shape):
    n = len(shape)
    return pl.BlockSpec(shape, lambda i: (0,) * n)


def _tril():
    return lax.broadcasted_iota(jnp.int32, (GROUP, GROUP), 0) >= lax.broadcasted_iota(jnp.int32, (GROUP, GROUP), 1)


def _shift_rows(x, k, edge):
    tb = x.shape[0]
    r8 = lax.broadcasted_iota(jnp.int32, (8, 1), 0)
    rolled = pltpu.roll(x, k % tb, 0)
    if k > 0:
        top = jnp.where(r8 < k, pltpu.roll(edge, k, 0), rolled[0:8, :])
        return jnp.concatenate([top, rolled[8:, :]], axis=0)
    bottom = jnp.where(r8 >= 8 + k, pltpu.roll(edge, 8 + k, 0), rolled[tb - 8:, :])
    return jnp.concatenate([rolled[:tb - 8, :], bottom], axis=0)


def _in_turns(parts):
    parts = list(parts)
    while parts:
        for p in list(parts):
            try:
                next(p)
            except StopIteration:
                parts.remove(p)


class _Hosted(NamedTuple):
    operands: tuple
    aliased: bool
    out_shapes: tuple
    sems: tuple
    stages: tuple


def _hosted_results(hosted):
    if hosted.aliased:
        return [jax.ShapeDtypeStruct(o.shape, o.dtype) for o in hosted.operands]
    return list(hosted.out_shapes)


def _call_hosting(main_body, hosted, *, name, steps, in_specs, out_specs, out_shape, scratch_shapes, operands, aliases=None):
    grid = steps if isinstance(steps, tuple) else (steps,)
    semantics = ("arbitrary",) * len(grid)
    hosted = [hs for hs in (hosted or ()) if hs is not None]
    if not hosted:
        outs = pl.pallas_call(main_body, name=name, grid=grid, in_specs=in_specs, out_specs=out_specs, out_shape=out_shape,
                              scratch_shapes=scratch_shapes, input_output_aliases=aliases or {}, compiler_params=_params(semantics))(*operands)
        return outs, ()
    n_in, n_out, n_sc = len(in_specs), len(out_specs), len(scratch_shapes)
    shapes = [_hosted_results(hs) for hs in hosted]
    aliases, in_at, out_at = dict(aliases or {}), n_in, n_out
    for hs, sh in zip(hosted, shapes):
        if hs.aliased:
            aliases.update({in_at + i: out_at + i for i in range(len(hs.operands))})
        in_at += len(hs.operands)
        out_at += len(sh)

    def body(*refs):
        at = [0]

        def take(n):
            at[0] += n
            return refs[at[0] - n:at[0]]

        ins = take(n_in)
        h_in = [take(len(hs.operands)) for hs in hosted]
        outs = take(n_out)
        h_out = [take(len(sh)) for sh in shapes]
        scratch = take(n_sc)
        h_sems = [take(len(hs.sems)) for hs in hosted]
        ids = [pl.program_id(a) for a in range(len(grid))]

        def at_step(where):
            lead, rest = where
            ok = ids[0] == lead
            for a in range(1, len(grid)):
                ok = jnp.logical_and(ok, ids[a] == (grid[a] - 1 if rest else 0))
            return ok

        def run(stage):
            for hs, a, b, c in zip(hosted, h_in, h_out, h_sems):
                if hs.stages[stage] is not None:
                    hs.stages[stage](a, b, c)

        @pl.when(at_step((0, 0)))
        def _():
            run(0)

        if any(hs.stages[1] is not None for hs in hosted):
            @pl.when(at_step(((3 * grid[0]) // 4, 0)))
            def _():
                run(1)

        main_body(*ins, *outs, *scratch)

        @pl.when(at_step((grid[0] - 1, -1)))
        def _():
            run(2)

    flat = lambda lists: [x for xs in lists for x in xs]
    outs = pl.pallas_call(
        body, name=name, grid=grid, in_specs=list(in_specs) + [HB] * (in_at - n_in), out_specs=list(out_specs) + [HB] * (out_at - n_out),
        out_shape=list(out_shape) + flat(shapes), scratch_shapes=list(scratch_shapes) + flat(hs.sems for hs in hosted),
        input_output_aliases=aliases, compiler_params=_params(semantics),
    )(*operands, *flat(hs.operands for hs in hosted))
    results, at = [], n_out
    for sh in shapes:
        results.append(outs[at:at + len(sh)])
        at += len(sh)
    return outs[:n_out], results


def _run_hosted(hosted, name):
    nh = len(hosted.operands)
    h_shapes = _hosted_results(hosted)

    def body(*refs):
        h_in, h_out, h_sems = refs[:nh], refs[nh:nh + len(h_shapes)], refs[nh + len(h_shapes):]
        for stage in hosted.stages:
            if stage is not None:
                stage(h_in, h_out, h_sems)

    return pl.pallas_call(
        body, name=name, in_specs=[HB] * nh, out_specs=[HB] * len(h_shapes), out_shape=h_shapes, scratch_shapes=list(hosted.sems),
        input_output_aliases={i: i for i in range(nh)} if hosted.aliased else {},
        compiler_params=pltpu.CompilerParams(has_side_effects=True),
    )(*hosted.operands)


def _window_sums(e, win, back):
    n = e.shape[0]
    k = 1
    while k < win:
        e = e + pltpu.roll(e, k if back else n - k, 0)
        k *= 2
    return e


def _pool_diff(prev, p, t0, gi, win):
    sl = slice(gi * GROUP, (gi + 1) * GROUP)
    tb = p.shape[0]
    s = _window_sums(jnp.concatenate([prev[:, sl], p[:, sl]], axis=0), win, True)[HALO:, :]
    tglob = t0 + lax.broadcasted_iota(jnp.int32, (tb, 1), 0)
    cnt = jnp.minimum(tglob + 1, win).astype(F32)
    return s / cnt - p[:, sl], cnt


def _layernorm(v):
    xc = v - jnp.mean(v, axis=-1, keepdims=True)
    rstd = lax.rsqrt(jnp.mean(xc * xc, axis=-1, keepdims=True) + EPS)
    return xc * rstd, rstd


def _mixer_fwd(h, g, w_in, pool_w, pool_scale, sgu_g, sgu_w, sgu_bt, w_out, hosted=None):
    t, d = h.shape
    pw = pool_w.shape[0] * GROUP
    sw = sgu_w.shape[0] * GROUP
    tb = _token_block(t, 512)

    def body(h_ref, g_ref, win_ref, pw_ref, ps_ref, sg_ref, sw_ref, sbt_ref, wout_ref, h1_ref, proj_ref, xn_ref, mix_ref, pext):
        i = pl.program_id(0)

        @pl.when(i == 0)
        def _():
            pext[...] = jnp.zeros((HALO, pw), F32)

        x = h_ref[...]
        xhat, _ = _rms(x)
        xn = (xhat * g_ref[...]).astype(_MXU)
        xn_ref[...] = xn
        proj = _nn(xn, win_ref[...])
        proj_ref[...] = proj
        p = proj[:, :pw]
        prev = pext[...]
        for gi, win in enumerate(WINDOWS):
            sl = slice(gi * GROUP, (gi + 1) * GROUP)
            dg, _ = _pool_diff(prev, p, i * tb, gi, win)
            e = _nn(dg.astype(_MXU), pw_ref[gi].astype(_MXU))
            mix_ref[:, sl] = (e * ps_ref[:, sl]).astype(_MXU)
        pext[...] = p[tb - HALO:tb, :]
        uv, _ = _gelu(proj[:, pw:])
        u = uv[:, :sw]
        vhat, _ = _layernorm(uv[:, sw:])
        vn = (vhat * sg_ref[...]).astype(_MXU)
        mask = _tril()
        chunks = [slice(n * GROUP, (n + 1) * GROUP) for n in range(tb // GROUP)]
        for hh in range(sw // GROUP):
            wm = jnp.where(mask, sw_ref[hh], 0.0).astype(_MXU)
            cols = slice(hh * GROUP, (hh + 1) * GROUP)
            z = _nn(wm, jnp.concatenate([vn[rows, cols] for rows in chunks], axis=1))
            for n, rows in enumerate(chunks):
                mix_ref[rows, pw + hh * GROUP:pw + (hh + 1) * GROUP] = (u[rows, cols] * (z[:, chunks[n]] + sbt_ref[hh])).astype(_MXU)
        h1_ref[...] = x + _nn(mix_ref[...], wout_ref[...])

    blk = lambda w: pl.BlockSpec((tb, w), lambda i: (i, 0))
    return _call_hosting(
        body, hosted, name="mixer_fwd", steps=t // tb,
        in_specs=[blk(d), VM, VM, VM, VM, VM, VM, VM, VM],
        out_specs=[blk(d), blk(w_in.shape[1]), blk(d), blk(d)],
        out_shape=[jax.ShapeDtypeStruct((t, d), F32), jax.ShapeDtypeStruct((t, w_in.shape[1]), F32),
                   jax.ShapeDtypeStruct((t, d), _MXU), jax.ShapeDtypeStruct((t, d), _MXU)],
        scratch_shapes=[pltpu.VMEM((HALO, pw), F32)],
        operands=(h, g, w_in, pool_w, pool_scale, sgu_g, sgu_w, sgu_bt, w_out))


def _mixer_bwd(dh1, h, proj, g, w_in, pool_w, pool_scale, sgu_g, sgu_w, sgu_bt, w_out, hosted=None):
    t, d = h.shape
    ng, nh = pool_w.shape[0], sgu_w.shape[0]
    pw, sw = ng * GROUP, nh * GROUP
    tb = _token_block(t, 512)
    nb = t // tb
    n_parts = 2 if tb % (2 * GROUP) == 0 else 1
    pt = tb // n_parts

    def body(dh1_ref, h_ref, proj_ref, halo_ref, g_ref, win_ref, pw_ref, ps_ref, sg_ref, sw_ref, sbt_ref, wout_ref,
             dh_ref, dproj_ref, gg_ref, gpw_ref, gps_ref, gsg_ref, gsw_ref, gsbt_ref, dext, duv):
        i = pl.program_id(0)
        blk = nb - 1 - i

        @pl.when(i == 0)
        def _():
            for r in (gg_ref, gpw_ref, gps_ref, gsg_ref, gsw_ref, gsbt_ref, dext):
                r[...] = jnp.zeros(r.shape, F32)

        mask = _tril()

        def part(at):
            rows = slice(at, at + pt)
            dh1v = dh1_ref[rows, :]
            dmix = _nt(dh1v.astype(_MXU), wout_ref[...])
            yield
            proj_v = proj_ref[rows, :]
            p = proj_v[:, :pw]
            prev = jnp.where(blk == 0, 0.0, halo_ref[...]) if at == 0 else proj_ref[at - HALO:at, 0:pw]
            for gi, win in enumerate(WINDOWS):
                sl = slice(gi * GROUP, (gi + 1) * GROUP)
                dg, cnt = _pool_diff(prev, p, blk * tb + at, gi, win)
                dgm = dg.astype(_MXU)
                pwm = pw_ref[gi].astype(_MXU)
                e = _nn(dgm, pwm)
                dy = dmix[:, sl]
                gps_ref[:, sl] += jnp.sum(dy * e, axis=0, keepdims=True)
                de = (dy * ps_ref[:, sl]).astype(_MXU)
                gpw_ref[gi] += _tn(dgm, de)
                dd = _nt(de, pwm)
                ddc = dd / cnt
                acc = _window_sums(jnp.concatenate([ddc, dext[:, sl]], axis=0), win, False)[:pt, :]
                dext[:, sl] = ddc[0:HALO, :]
                dproj_ref[rows, sl] = (acc - dd).astype(_MXU)
            yield
            pre = proj_v[:, pw:]
            uv, cdf = _gelu(pre)
            u = uv[:, :sw]
            vhat, rstd = _layernorm(uv[:, sw:])
            vn = (vhat * sg_ref[...]).astype(_MXU)
            chunks = [slice(n * GROUP, (n + 1) * GROUP) for n in range(pt // GROUP)]
            side_by_side = lambda a, cols: jnp.concatenate([a[c, cols] for c in chunks], axis=1)
            for hh in range(nh):
                wm = jnp.where(mask, sw_ref[hh], 0.0).astype(_MXU)
                cols = slice(hh * GROUP, (hh + 1) * GROUP)
                vs = side_by_side(vn, cols)
                z = _nn(wm, vs)
                dy = side_by_side(dmix, slice(pw + hh * GROUP, pw + (hh + 1) * GROUP))
                dz = dy * side_by_side(u, cols)
                dzm = dz.astype(_MXU)
                dvs = _tn(wm, dzm)
                gsw_ref[hh] += jnp.where(mask, _nt(dzm, vs), 0.0)
                gb = jnp.zeros((GROUP, GROUP), F32)
                for n, c in enumerate(chunks):
                    gb = gb + dz[:, c]
                    duv[at + n * GROUP:at + (n + 1) * GROUP, cols] = dy[:, c] * (z[:, c] + sbt_ref[hh])
                    duv[at + n * GROUP:at + (n + 1) * GROUP, sw + hh * GROUP:sw + (hh + 1) * GROUP] = dvs[:, c]
                gsbt_ref[hh] += gb
            yield
            dvn = duv[rows, sw:]
            gsg_ref[...] += jnp.sum(dvn * vhat, axis=0, keepdims=True)
            dxh = dvn * sg_ref[...]
            dv = rstd * (dxh - jnp.mean(dxh, axis=-1, keepdims=True) - vhat * jnp.mean(dxh * vhat, axis=-1, keepdims=True))
            gp = _gelu_grad(pre, cdf)
            dproj_ref[rows, pw:pw + sw] = (duv[rows, :sw] * gp[:, :sw]).astype(_MXU)
            dproj_ref[rows, pw + sw:] = (dv * gp[:, sw:]).astype(_MXU)
            dxn = _nt(dproj_ref[rows, :], win_ref[...])
            yield
            xhat, r = _rms(h_ref[rows, :])
            dx, gg = _rms_bwd(dxn, xhat, r, g_ref[...])
            gg_ref[...] += gg
            dh_ref[rows, :] = dh1v + dx

        _in_turns([part(at) for at in reversed(range(0, tb, pt))])

    rev = lambda w: pl.BlockSpec((tb, w), lambda i: (nb - 1 - i, 0))
    halo = pl.BlockSpec((HALO, pw), lambda i: (jnp.maximum((nb - 1 - i) * (tb // HALO) - 1, 0), 0))
    small = [(1, d), (ng, GROUP, GROUP), (1, pw), (1, sw), (nh, GROUP, GROUP), (nh, GROUP, GROUP)]
    return _call_hosting(
        body, hosted, name="mixer_bwd", steps=nb,
        in_specs=[rev(d), rev(d), rev(proj.shape[1]), halo, VM, VM, VM, VM, VM, VM, VM, VM],
        out_specs=[rev(d), rev(proj.shape[1])] + [_const_spec(s) for s in small],
        out_shape=[jax.ShapeDtypeStruct((t, d), F32), jax.ShapeDtypeStruct(proj.shape, _MXU)]
        + [jax.ShapeDtypeStruct(s, F32) for s in small],
        scratch_shapes=[pltpu.VMEM((HALO, pw), F32), pltpu.VMEM((tb, 2 * sw), F32)],
        operands=(dh1, h, proj, proj, g, w_in, pool_w, pool_scale, sgu_g, sgu_w, sgu_bt, w_out))


def _kv_fwd(mem, gm, wk, wv):
    n, d = mem.shape

    def body(mem_ref, gm_ref, wk_ref, wv_ref, k_ref, v_ref, memn_ref):
        xhat, _ = _rms(mem_ref[...])
        memn = (xhat * gm_ref[...]).astype(_MXU)
        memn_ref[...] = memn
        k_ref[...] = _nn(memn, wk_ref[...]).astype(_MXU)
        v_ref[...] = _nn(memn, wv_ref[...]).astype(_MXU)

    return pl.pallas_call(
        body, name="kv_fwd", in_specs=[VM] * 4, out_specs=[VM] * 3,
        out_shape=[jax.ShapeDtypeStruct((n, d), _MXU)] * 3, compiler_params=_params(),
    )(mem, gm, wk, wv)


def _kv_bwd(dk, dv, mem, wk, wv):
    n, d = mem.shape

    def body(dk_ref, dv_ref, mem_ref, wk_ref, wv_ref, ggm_ref):
        dmemn = _nt(dk_ref[...].astype(_MXU), wk_ref[...]) + _nt(dv_ref[...].astype(_MXU), wv_ref[...])
        xhat, _ = _rms(mem_ref[...])
        ggm_ref[...] = jnp.sum(dmemn * xhat, axis=0, keepdims=True)

    return pl.pallas_call(
        body, name="kv_bwd", in_specs=[VM] * 5, out_specs=VM,
        out_shape=jax.ShapeDtypeStruct((1, d), F32), compiler_params=_params(),
    )(dk, dv, mem, wk, wv)


def _softmax(s):
    e = jnp.exp(s - jnp.max(s, axis=-1, keepdims=True))
    return e / jnp.sum(e, axis=-1, keepdims=True)


def _one_ahead(n, issue):
    nxt = issue(0)
    for a in range(n):
        cur = nxt
        if a + 1 < n:
            nxt = issue(a + 1)
        yield a, cur


def _xattn_fwd(h, g, wq, k, v, wo, hosted=None):
    t, d = h.shape
    hd = d // N_XHEADS
    scale = hd ** -0.5
    tb = _token_block(t, 512)

    def body(h_ref, g_ref, wq_ref, k_ref, v_ref, wo_ref, h2_ref, q_ref, o_ref, xn_ref):
        x = h_ref[...]
        xhat, _ = _rms(x)
        xn = (xhat * g_ref[...]).astype(_MXU)
        xn_ref[...] = xn
        qm = _nn(xn, wq_ref[...]).astype(_MXU)
        q_ref[...] = qm
        heads = [slice(a * hd, (a + 1) * hd) for a in range(N_XHEADS)]
        for a, s in _one_ahead(N_XHEADS, lambda a: _nt(qm[:, heads[a]], k_ref[:, heads[a]]) * scale):
            o_ref[:, heads[a]] = _nn(_softmax(s).astype(_MXU), v_ref[:, heads[a]]).astype(_MXU)
        h2_ref[...] = x + _nn(o_ref[...], wo_ref[...])

    blk = pl.BlockSpec((tb, d), lambda i: (i, 0))
    return _call_hosting(
        body, hosted, name="xattn_fwd", steps=t // tb,
        in_specs=[blk, VM, VM, VM, VM, VM], out_specs=[blk] * 4,
        out_shape=[jax.ShapeDtypeStruct((t, d), F32)] + [jax.ShapeDtypeStruct((t, d), _MXU)] * 3,
        scratch_shapes=[], operands=(h, g, wq, k, v, wo))


def _xattn_bwd(dh2, h, q, g, wq, k, v, wo, hosted=None):
    t, d = h.shape
    n = k.shape[0]
    hd = d // N_XHEADS
    scale = hd ** -0.5
    tb = _token_block(t, 512)
    pt = tb // 2 if tb % (2 * GROUP) == 0 else tb

    def body(dh2_ref, h_ref, q_ref, g_ref, wq_ref, k_ref, v_ref, wo_ref, dh_ref, dq_ref, dk_ref, dv_ref, gg_ref):
        @pl.when(pl.program_id(0) == 0)
        def _():
            for r in (dk_ref, dv_ref, gg_ref):
                r[...] = jnp.zeros(r.shape, F32)

        heads = [slice(a * hd, (a + 1) * hd) for a in range(N_XHEADS)]

        def part(at):
            rows = slice(at, at + pt)
            dh2v = dh2_ref[rows, :]
            dom = _nt(dh2v.astype(_MXU), wo_ref[...]).astype(_MXU)
            yield
            issue = lambda a: (_nt(q_ref[rows, heads[a]], k_ref[:, heads[a]]) * scale, _nt(dom[:, heads[a]], v_ref[:, heads[a]]))
            for a, (s, dpr) in _one_ahead(N_XHEADS, issue):
                sl = heads[a]
                pr = _softmax(s)
                dv_ref[:, sl] += _tn(pr.astype(_MXU), dom[:, sl])
                ds = (pr * (dpr - jnp.sum(dpr * pr, axis=-1, keepdims=True)) * scale).astype(_MXU)
                dq_ref[rows, sl] = _nn(ds, k_ref[:, sl]).astype(_MXU)
                dk_ref[:, sl] += _tn(ds, q_ref[rows, sl])
                yield
            dxn = _nt(dq_ref[rows, :], wq_ref[...])
            yield
            xhat, r = _rms(h_ref[rows, :])
            dx, gg = _rms_bwd(dxn, xhat, r, g_ref[...])
            gg_ref[...] += gg
            dh_ref[rows, :] = dh2v + dx

        _in_turns([part(at) for at in range(0, tb, pt)])

    blk = pl.BlockSpec((tb, d), lambda i: (i, 0))
    return _call_hosting(
        body, hosted, name="xattn_bwd", steps=t // tb,
        in_specs=[blk, blk, blk, VM, VM, VM, VM, VM],
        out_specs=[blk, blk, _const_spec((n, d)), _const_spec((n, d)), _const_spec((1, d))],
        out_shape=[jax.ShapeDtypeStruct((t, d), F32), jax.ShapeDtypeStruct((t, d), _MXU),
                   jax.ShapeDtypeStruct((n, d), F32), jax.ShapeDtypeStruct((n, d), F32), jax.ShapeDtypeStruct((1, d), F32)],
        scratch_shapes=[], operands=(dh2, h, q, g, wq, k, v, wo))


def _ffn_fwd(h, g, w_up, conv_w, conv_b, w_down, hosted=None, head=None):
    t, d = h.shape
    f = w_down.shape[0]
    ft = FF_TILE
    tb = _token_block(t, 256)

    def body(h_ref, g_ref, wup_ref, cw_ref, cb_ref, wdown_ref, *rest):
        if head is None:
            h3_ref, hh_ref, hc_ref, ext, carry, act_sc = rest
        else:
            gf_ref, tgt_ref, h3_ref, hh_ref, hc_ref, loss_ref, ggf_ref, ext, carry, act_sc = rest

        @pl.when(pl.program_id(0) == 0)
        def _():
            carry[...] = jnp.zeros(carry.shape, F32)
            if head is not None:
                loss_ref[...] = jnp.zeros(loss_ref.shape, F32)
                ggf_ref[...] = jnp.zeros(ggf_ref.shape, F32)

        x = h_ref[...]
        xhat, _ = _rms(x)
        xn = (xhat * g_ref[...]).astype(_MXU)
        acc = jnp.zeros((tb, d), F32)
        up = lambda j: [_nn(xn, wup_ref[:, off:off + ft]) for off in (j * ft, f + j * ft)]
        up_next = up(0)
        for j in range(f // ft):
            hc = []
            up_cur = up_next
            if j + 1 < f // ft:
                up_next = up(j + 1)
            for part, off in enumerate((j * ft, f + j * ft)):
                cols = slice(off, off + ft)
                cur = up_cur[part]
                hh_ref[:, cols] = cur.astype(_MXU)
                ext[part, 0:8, :] = carry[:, cols]
                ext[part, 8:8 + tb, :] = cur
                carry[:, cols] = cur[tb - 8:tb, :]
                hc.append(cb_ref[:, cols] + cw_ref[0:1, cols] * ext[part, 6:6 + tb, :]
                          + cw_ref[1:2, cols] * ext[part, 7:7 + tb, :] + cw_ref[2:3, cols] * cur)
                hc_ref[:, cols] = hc[part].astype(_MXU)
            at = j % DOWN_TILES
            act_sc[:, at * ft:(at + 1) * ft] = (hc[0] * jax.nn.sigmoid(hc[0]) * hc[1]).astype(_MXU)
            if at + 1 == DOWN_TILES or j + 1 == f // ft:
                acc = acc + _nn(act_sc[:, 0:(at + 1) * ft], wdown_ref[(j - at) * ft:(j + 1) * ft, :])
        if head is None:
            h3_ref[...] = x + acc
        else:
            yhat, r = _rms(x + acc)
            err = yhat * gf_ref[...] - tgt_ref[...]
            loss_ref[...] += 0.5 * jnp.sum(jnp.sum(err * err, axis=-1, keepdims=True), axis=0, keepdims=True) / d
            dx, gg = _rms_bwd(err / d, yhat, r, gf_ref[...])
            ggf_ref[...] += gg
            h3_ref[...] = dx

    blk = lambda w: pl.BlockSpec((tb, w), lambda i: (i, 0))
    in_specs = [blk(d), VM, VM, VM, VM, VM]
    out_specs = [blk(d), blk(2 * f), blk(2 * f)]
    out_shape = [jax.ShapeDtypeStruct((t, d), F32), jax.ShapeDtypeStruct((t, 2 * f), _MXU), jax.ShapeDtypeStruct((t, 2 * f), _MXU)]
    operands = (h, g, w_up, conv_w, conv_b, w_down)
    if head is not None:
        in_specs += [VM, blk(d)]
        out_specs += [_const_spec((1, 1)), _const_spec((1, d))]
        out_shape += [jax.ShapeDtypeStruct((1, 1), F32), jax.ShapeDtypeStruct((1, d), F32)]
        operands += tuple(head)
    return _call_hosting(
        body, hosted, name="ffn_fwd", steps=t // tb, in_specs=in_specs, out_specs=out_specs, out_shape=out_shape,
        scratch_shapes=[pltpu.VMEM((2, 8 + tb, ft), F32), pltpu.VMEM((8, 2 * f), F32), pltpu.VMEM((tb, DOWN_TILES * ft), _MXU)],
        operands=operands)


def _ffn_bwd(dh3, h, hh, hc, g, w_up, conv_w, w_down, hosted=None):
    t, d = h.shape
    f = w_down.shape[0]
    ft = FF_TILE
    tb = _token_block(t, 256)
    nb = t // tb

    def body(dh3_ref, h_ref, hh_ref, hc_ref, g_ref, wup_ref, cw_ref, wdown_ref,
             dh_ref, dhh_ref, act_ref, xn_ref, gcw_ref, gcb_ref, gg_ref, dcarry):
        @pl.when(pl.program_id(0) == 0)
        def _():
            for r in (gcw_ref, gcb_ref, gg_ref, dcarry):
                r[...] = jnp.zeros(r.shape, F32)

        dh3v = dh3_ref[...]
        dhm = dh3v.astype(_MXU)
        dxn = jnp.zeros((tb, d), F32)
        dact_next = _nt(dhm, wdown_ref[0:ft, :])
        for j in range(f // ft):
            dact = dact_next
            if j + 1 < f // ft:
                dact_next = _nt(dhm, wdown_ref[(j + 1) * ft:(j + 2) * ft, :])
            gate = hc_ref[:, j * ft:(j + 1) * ft].astype(F32)
            val = hc_ref[:, f + j * ft:f + (j + 1) * ft].astype(F32)
            sg = jax.nn.sigmoid(gate)
            silu = gate * sg
            act_ref[:, j * ft:(j + 1) * ft] = (silu * val).astype(_MXU)
            dhc = (dact * val * sg * (1.0 + gate * (1.0 - sg)), dact * silu)
            for part, off in enumerate((j * ft, f + j * ft)):
                cols = slice(off, off + ft)
                dc = dhc[part]
                c0 = hh_ref[:, cols].astype(F32)
                after = dcarry[:, cols]
                ahead1 = _shift_rows(dc, -1, after)
                ahead2 = _shift_rows(dc, -2, after)
                dcarry[:, cols] = dc[0:8, :]
                gcb_ref[:, cols] += jnp.sum(dc, axis=0, keepdims=True)
                gcw_ref[0:1, cols] += jnp.sum(ahead2 * c0, axis=0, keepdims=True)
                gcw_ref[1:2, cols] += jnp.sum(ahead1 * c0, axis=0, keepdims=True)
                gcw_ref[2:3, cols] += jnp.sum(dc * c0, axis=0, keepdims=True)
                dhh = (cw_ref[2:3, cols] * dc + cw_ref[1:2, cols] * ahead1 + cw_ref[0:1, cols] * ahead2).astype(_MXU)
                dhh_ref[:, cols] = dhh
                dxn = dxn + _nt(dhh, wup_ref[:, cols])
        xhat, r = _rms(h_ref[...])
        xn_ref[...] = (xhat * g_ref[...]).astype(_MXU)
        dx, gg = _rms_bwd(dxn, xhat, r, g_ref[...])
        gg_ref[...] += gg
        dh_ref[...] = dh3v + dx

    rev = lambda w: pl.BlockSpec((tb, w), lambda i: (nb - 1 - i, 0))
    return _call_hosting(
        body, hosted, name="ffn_bwd", steps=nb,
        in_specs=[rev(d), rev(d), rev(2 * f), rev(2 * f), VM, VM, VM, VM],
        out_specs=[rev(d), rev(2 * f), rev(f), rev(d), _const_spec((3, 2 * f)), _const_spec((1, 2 * f)), _const_spec((1, d))],
        out_shape=[jax.ShapeDtypeStruct((t, d), F32), jax.ShapeDtypeStruct((t, 2 * f), _MXU), jax.ShapeDtypeStruct((t, f), _MXU),
                   jax.ShapeDtypeStruct((t, d), _MXU),
                   jax.ShapeDtypeStruct((3, 2 * f), F32), jax.ShapeDtypeStruct((1, 2 * f), F32), jax.ShapeDtypeStruct((1, d), F32)],
        scratch_shapes=[pltpu.VMEM((8, 2 * f), F32)],
        operands=(dh3, h, hh, hc, g, w_up, conv_w, w_down))


def _largest_tile(n, cap, mult=128):
    best = None
    for c in range(mult, min(n, cap) + 1, mult):
        if n % c == 0:
            best = c
    return best if best is not None else n


def _grad_matmul(a, b, name, layer, n_layers, into=None, hosted=None):
    t, m = a.shape
    n = b.shape[1]
    tm, tn, tk = _largest_tile(m, 1408), _largest_tile(n, 1536), _largest_tile(t, 1024)
    nk = t // tk

    def body(a_ref, b_ref, *rest):
        o_ref = rest[-1]

        @pl.when(pl.program_id(2) == 0)
        def _():
            o_ref[...] = jnp.zeros(o_ref.shape, F32)

        o_ref[...] += _tn(a_ref[...].astype(_MXU), b_ref[...].astype(_MXU))

    in_specs = [pl.BlockSpec((tk, tm), lambda i, j, k: (k, i)), pl.BlockSpec((tk, tn), lambda i, j, k: (k, j))]
    operands = (a, b)
    aliases = {}
    if into is not None:
        in_specs.append(pl.BlockSpec(memory_space=pl.ANY))
        operands = (a, b, into)
        aliases = {2: 0}
    (out,), got = _call_hosting(
        body, hosted, name=name, steps=(m // tm, n // tn, nk), in_specs=in_specs,
        out_specs=[pl.BlockSpec((None, tm, tn), lambda i, j, k: (layer, i, j))],
        out_shape=[jax.ShapeDtypeStruct((n_layers, m, n), F32)], scratch_shapes=[], operands=operands, aliases=aliases)
    return out, got


def _adamw_math(w, g, m, v):
    m = ADAM_B1 * m + (1.0 - ADAM_B1) * g
    v = ADAM_B2 * v + (1.0 - ADAM_B2) * (g * g)
    m_hat = m / (1.0 - ADAM_B1 ** ADAM_STEP)
    v_hat = v / (1.0 - ADAM_B2 ** ADAM_STEP)
    return -ADAM_LR * (m_hat / (jnp.sqrt(v_hat) + ADAM_EPS) + ADAM_WD * w), m, v


def _row_block(rows, cols, max_bytes=1 << 20, mult=16):
    best = None
    for r in range(mult, rows + 1, mult):
        if rows % r == 0 and r * cols * 4 <= max_bytes:
            best = r
    return best if best is not None else rows


def _adamw_big(ws, gs, ms, vs, name):
    n = len(ws)
    shape = ws[0].shape
    cols = shape[-1]
    flat = lambda a: a.reshape(-1, cols)
    rows = flat(ws[0]).shape[0]
    rb = _row_block(rows, cols, (2 << 20) // n)

    def body(*refs):
        for a in range(n):
            w_ref, g_ref, m_ref, v_ref = (refs[s * n + a] for s in range(4))
            go_ref, d_ref, nm_ref, nv_ref = (refs[(4 + s) * n + a] for s in range(4))
            g = g_ref[...]
            go_ref[...] = g
            d_ref[...], nm_ref[...], nv_ref[...] = _adamw_math(w_ref[...], g, m_ref[...], v_ref[...])

    blk = pl.BlockSpec((rb, cols), lambda i: (i, 0))
    outs = pl.pallas_call(
        body, name=name, grid=(rows // rb,), in_specs=[blk] * (4 * n), out_specs=[blk] * (4 * n),
        out_shape=[jax.ShapeDtypeStruct((rows, cols), F32)] * (4 * n), compiler_params=_params(("parallel",)),
    )(*[flat(a) for group in (ws, gs, ms, vs) for a in group])
    return [[outs[s * n + a].reshape(shape) for s in range(4)] for a in range(n)]


def _adamw_small(ws, gs, ms, vs):
    n = len(ws)

    def body(*refs):
        for a in range(n):
            w_ref, g_ref, m_ref, v_ref = (refs[s * n + a] for s in range(4))
            d_ref, nm_ref, nv_ref = (refs[(4 + s) * n + a] for s in range(3))
            d_ref[...], nm_ref[...], nv_ref[...] = _adamw_math(w_ref[...], g_ref[...], m_ref[...], v_ref[...])

    outs = pl.pallas_call(
        body, name="adamw_small", in_specs=[VM] * (4 * n), out_specs=[VM] * (3 * n),
        out_shape=[jax.ShapeDtypeStruct(w.shape, F32) for w in ws] * 3, compiler_params=_params(),
    )(*ws, *gs, *ms, *vs)
    return outs[:n], outs[n:2 * n], outs[2 * n:]


def _place():
    x, y, c = lax.axis_index("x"), lax.axis_index("y"), lax.axis_index("c")
    chips = [(1 - x, y), (x, 1 - y), (1 - x, 1 - y)]
    return x, y, c, chips


def _rows(start, size, mult=16):
    return pl.ds(pl.multiple_of(start, mult), size)


def _full_window(ref, axis, chip, half=None):
    r, c = ref.shape
    if axis == 0:
        rs = r // 4
        if half is None:
            return ref.at[_rows(chip * rs, rs), :]
        return ref.at[_rows(chip * rs + half * (rs // 2), rs // 2), :]
    cs = c // 4
    if half is None:
        return ref.at[:, _rows(chip * cs, cs, 128)]
    return ref.at[_rows(half * (r // 2), r // 2), _rows(chip * cs, cs, 128)]


def _remote(src, dst, send_sem, recv_sem, to):
    return pltpu.make_async_remote_copy(src_ref=src, dst_ref=dst, send_sem=send_sem, recv_sem=recv_sem,
                                        device_id=to, device_id_type=MESH)


def _scalars(*vals):
    return jnp.stack([jnp.asarray(v, jnp.int32) for v in vals])


def _cast_place(shards, axis, chip, name):
    n = len(shards)
    nl, rs, cs = shards[0].shape
    full = (rs * 4, cs) if axis == 0 else (rs, cs * 4)
    rb = _row_block(rs, cs, (4 << 20) // (n * nl))
    nrb = rs // rb

    def body(chip_ref, *refs):
        for a in range(n):
            for l in range(nl):
                refs[n + a * nl + l][...] = refs[a][l].astype(_PAY)

    if axis == 0:
        out_map = lambda i, chip_ref: (chip_ref[0] * nrb + i, 0)
    else:
        out_map = lambda i, chip_ref: (i, chip_ref[0])
    outs = pl.pallas_call(
        body, name=name,
        grid_spec=pltpu.PrefetchScalarGridSpec(
            num_scalar_prefetch=1, grid=(nrb,),
            in_specs=[pl.BlockSpec((nl, rb, cs), lambda i, chip_ref: (0, i, 0))] * n,
            out_specs=[pl.BlockSpec((rb, cs), out_map)] * (n * nl)),
        out_shape=[jax.ShapeDtypeStruct(full, _PAY)] * (n * nl), compiler_params=_params(("parallel",)),
    )(_scalars(chip), *shards)
    return [[outs[a * nl + l] for l in range(nl)] for a in range(n)]


def _hosted_allgather(placed, axes):
    n = len(placed)

    def each(outs, half_of):
        x, y, c, chips = _place()
        for i in range(n):
            for k, chip in enumerate(chips):
                yield i * 3 + k, (*chip, c), (x, y, 1 - c), _full_window(outs[i], axes[i], 2 * x + y, c), \
                    _full_window(outs[i], axes[i], 2 * chip[0] + chip[1], half_of(c))

    def start(_, outs, sems):
        send, recv, _, _ = sems
        for s, peer, _, mine, _ in each(outs, lambda c: c):
            _remote(mine, mine, send.at[s], recv.at[s], peer).start()

    def middle(_, outs, sems):
        send, recv, fsend, frecv = sems
        for s, _, sibling, _, got in each(outs, lambda c: c):
            _remote(got, got, send.at[s], recv.at[s], sibling).wait_recv()
            _remote(got, got, fsend.at[s], frecv.at[s], sibling).start()

    def finish(_, outs, sems):
        send, recv, fsend, frecv = sems
        for s, _, sibling, _, got in each(outs, lambda c: 1 - c):
            _remote(got, got, fsend.at[s], frecv.at[s], sibling).wait_recv()
        for s, peer, sibling, mine, got in each(outs, lambda c: c):
            _remote(mine, mine, send.at[s], recv.at[s], peer).wait_send()
            _remote(got, got, fsend.at[s], frecv.at[s], sibling).wait_send()

    return _Hosted(tuple(placed), True, (), (pltpu.SemaphoreType.DMA((n * 3,)),) * 4, (start, middle, finish))


def _allgather_conv(conv_shard):
    nl, taps, cs = conv_shard.shape

    def body(in_ref, out_ref, send, recv, local):
        x, y, c, chips = _place()
        mine = out_ref.at[:, :, _rows((2 * x + y) * cs, cs, 128)]
        own = pltpu.make_async_copy(in_ref, mine, local)
        own.start()
        sends = [_remote(in_ref, mine, send.at[k], recv.at[k], (*chip, c)) for k, chip in enumerate(chips)]
        for cp in sends:
            cp.start()
        for k, chip in enumerate(chips):
            got = out_ref.at[:, :, _rows((2 * chip[0] + chip[1]) * cs, cs, 128)]
            _remote(got, got, send.at[k], recv.at[k], (*chip, c)).wait_recv()
        for cp in sends:
            cp.wait_send()
        own.wait()

    return pl.pallas_call(
        body, name="allgather_conv", in_specs=[HB], out_specs=HB, out_shape=jax.ShapeDtypeStruct((nl, taps, cs * 4), conv_shard.dtype),
        scratch_shapes=[pltpu.SemaphoreType.DMA((3,)), pltpu.SemaphoreType.DMA((3,)), pltpu.SemaphoreType.DMA],
        compiler_params=pltpu.CompilerParams(has_side_effects=True),
    )(conv_shard)


def _hosted_exchange(grads, axes, layer):
    na = len(grads)
    views = [g.reshape(g.shape[0], 4, 2, g.shape[1] // 8, g.shape[2]) if ax == 0 else g for g, ax in zip(grads, axes)]

    def region(ref, axis, half):
        if axis == 0:
            return ref.at[layer, :, half]
        r = ref.shape[1]
        return ref.at[layer, _rows(half * (r // 2), r // 2), :]

    def copies(ins, land, sems):
        send, recv = sems
        x, y, c, _ = _place()
        return [_remote(region(ins[a], axes[a], 1 - c), land[a], send.at[a], recv.at[a], (x, y, 1 - c)) for a in range(na)]

    def start(ins, land, sems):
        for cp in copies(ins, land, sems):
            cp.start()

    def finish(ins, land, sems):
        for cp in copies(ins, land, sems):
            cp.wait()

    shapes = [(4, g.shape[1] // 8, g.shape[2]) if ax == 0 else (g.shape[1] // 2, g.shape[2]) for g, ax in zip(grads, axes)]
    return _Hosted(tuple(views), False, tuple(jax.ShapeDtypeStruct(s, F32) for s in shapes),
                   (pltpu.SemaphoreType.DMA((na,)),) * 2, (start, None, finish))


def _add_cast(mines, theirs, core, base, name):
    n = len(mines)
    na, nb, cols = theirs[0].shape
    rb = _row_block(nb, cols, (4 << 20) // n)

    def body(core_ref, *refs):
        for a in range(n):
            refs[2 * n + a][...] = (refs[a][...] + refs[n + a][...]).astype(_PAY)

    blk = pl.BlockSpec((None, rb, cols), lambda i, k, core_ref: (i, k, 0))
    return pl.pallas_call(
        body, name=name,
        grid_spec=pltpu.PrefetchScalarGridSpec(
            num_scalar_prefetch=1, grid=(na, nb // rb),
            in_specs=[pl.BlockSpec((None, None, rb, cols), lambda i, k, core_ref: (base + i, core_ref[0], k, 0))] * n + [blk] * n,
            out_specs=[blk] * n),
        out_shape=[jax.ShapeDtypeStruct((na, nb, cols), _PAY)] * n, compiler_params=_params(("parallel", "parallel")),
    )(_scalars(core), *mines, *theirs)


def _piece(ref, axis, chip):
    if axis == 0:
        return ref.at[chip]
    cs = ref.shape[1] // 4
    return ref.at[:, _rows(chip * cs, cs, 128)]


def _hosted_scatter(sums, axes):
    na = len(sums)

    def piece_shape(a):
        if axes[a] == 0:
            return (sums[a].shape[1], sums[a].shape[2])
        return (sums[a].shape[0], sums[a].shape[1] // 4)

    def copies(ins, slots, sems):
        send, recv = sems
        _, _, c, chips = _place()
        return [_remote(_piece(ins[a], axes[a], 2 * chip[0] + chip[1]), slots[a].at[k], send.at[a * 3 + k], recv.at[a * 3 + k], (*chip, c))
                for a in range(na) for k, chip in enumerate(chips)]

    def start(ins, slots, sems):
        for cp in copies(ins, slots, sems):
            cp.start()

    def finish(ins, slots, sems):
        for cp in copies(ins, slots, sems):
            cp.wait()

    return _Hosted(tuple(sums), False, tuple(jax.ShapeDtypeStruct((3,) + piece_shape(a), sums[a].dtype) for a in range(na)),
                   (pltpu.SemaphoreType.DMA((na * 3,)),) * 2, (start, None, finish))


def _sum_slots(sums, slots, axis, chip, core, layer, n_layers, name, into=None):
    _, hr, cs = slots.shape
    rb = _row_block(hr, cs, 4 << 20)

    def body(at_ref, own_ref, s_ref, *rest):
        rest[-1][...] = ((own_ref[...].astype(F32) + s_ref[0].astype(F32)) + s_ref[1].astype(F32)) + s_ref[2].astype(F32)

    if axis == 0:
        own = pl.BlockSpec((None, rb, cs), lambda k, at_ref: (at_ref[0], k, 0))
    else:
        own = pl.BlockSpec((rb, cs), lambda k, at_ref: (k, at_ref[0]))
    in_specs = [own, pl.BlockSpec((3, rb, cs), lambda k, at_ref: (0, k, 0))]
    operands = (sums, slots)
    aliases = {}
    if into is not None:
        in_specs.append(pl.BlockSpec(memory_space=pl.ANY))
        operands = (sums, slots, into)
        aliases = {3: 0}
    return pl.pallas_call(
        body, name=name,
        grid_spec=pltpu.PrefetchScalarGridSpec(
            num_scalar_prefetch=1, grid=(hr // rb,), in_specs=in_specs,
            out_specs=pl.BlockSpec((None, None, rb, cs), lambda k, at_ref: (layer, at_ref[1], k, 0))),
        out_shape=jax.ShapeDtypeStruct((n_layers, 2, hr, cs), F32), input_output_aliases=aliases,
        compiler_params=_params(("parallel",)),
    )(_scalars(chip, core), *operands)


def _sibling_assemble(shards):
    na = len(shards)

    def body(*refs):
        outs = refs[na:2 * na]
        send, recv = refs[2 * na:]
        x, y, c, _ = _place()
        copies = []
        for a in range(na):
            hr = outs[a].shape[1] // 2
            mine = outs[a].at[:, _rows(c * hr, hr), :]
            cp = _remote(mine, mine, send.at[a], recv.at[a], (x, y, 1 - c))
            cp.start()
            copies.append(cp)
        for cp in copies:
            cp.wait()

    return pl.pallas_call(
        body, name="grad_sibling_assemble", in_specs=[HB] * na, out_specs=[HB] * na,
        out_shape=[jax.ShapeDtypeStruct(s.shape, F32) for s in shards], input_output_aliases={a: a for a in range(na)},
        scratch_shapes=[pltpu.SemaphoreType.DMA((na,))] * 2,
        compiler_params=pltpu.CompilerParams(has_side_effects=True),
    )(*shards)


def _allreduce_small(buf, hosted):
    rows, w = buf.shape
    half = rows // 2
    nh, h_shapes = len(hosted.operands), _hosted_results(hosted)

    def body(buf_ref, *refs):
        h_in, out_ref, h_out = refs[:nh], refs[nh], refs[nh + 1:nh + 1 + len(h_shapes)]
        land, slots, red, sems_send, sems_recv = refs[nh + 1 + len(h_shapes):nh + 6 + len(h_shapes)]
        h_sems = refs[nh + 6 + len(h_shapes):]
        hosted.stages[0](h_in, h_out, h_sems)
        x, y, c, chips = _place()
        me = 2 * x + y
        sibling = (x, y, 1 - c)
        first = _remote(buf_ref, land, sems_send.at[0], sems_recv.at[0], sibling)
        first.start()
        first.wait()
        mine = pl.ds(pl.multiple_of(c * half, 8), half)
        slots[me] = buf_ref[mine, :] + land[mine, :]
        sends = []
        for k, chip in enumerate(chips):
            cp = _remote(slots.at[me], slots.at[me], sems_send.at[1 + k], sems_recv.at[1 + k], (*chip, c))
            cp.start()
            sends.append(cp)
        for k, chip in enumerate(chips):
            got = slots.at[2 * chip[0] + chip[1]]
            _remote(got, got, sems_send.at[1 + k], sems_recv.at[1 + k], sibling).wait_recv()
        red[...] = ((slots[0] + slots[1]) + slots[2]) + slots[3]
        out_ref[mine, :] = red[...]
        last = _remote(red, out_ref.at[mine, :], sems_send.at[4], sems_recv.at[4], sibling)
        last.start()
        theirs = out_ref.at[pl.ds(pl.multiple_of((1 - c) * half, 8), half), :]
        _remote(red, theirs, sems_send.at[4], sems_recv.at[4], sibling).wait_recv()
        for cp in sends:
            cp.wait_send()
        last.wait_send()
        hosted.stages[2](h_in, h_out, h_sems)

    outs = pl.pallas_call(
        body, name="allreduce_small", in_specs=[VM] + [HB] * nh, out_specs=[VM] + [HB] * len(h_shapes),
        out_shape=[jax.ShapeDtypeStruct((rows, w), F32)] + h_shapes,
        scratch_shapes=[pltpu.VMEM((rows, w), F32), pltpu.VMEM((4, half, w), F32), pltpu.VMEM((half, w), F32),
                        pltpu.SemaphoreType.DMA((5,)), pltpu.SemaphoreType.DMA((5,))] + list(hosted.sems),
        compiler_params=pltpu.CompilerParams(has_side_effects=True, vmem_limit_bytes=VMEM_LIMIT),
    )(buf, *hosted.operands)
    return outs[0], outs[1:]


BIG = ("w_in", "w_out", "wq", "wk", "wv", "wo", "w_up", "w_down")
MIXER, ATTN, MLP = ("w_in", "w_out"), ("wq", "wk", "wv", "wo"), ("w_up", "w_down")
BIG_AXIS = {"w_in": 1, "w_out": 0, "wq": 0, "wk": 0, "wv": 0, "wo": 0, "w_up": 1, "w_down": 0}
SMALL = ("norm_mix_g", "pool_w", "pool_scale", "sgu_g", "sgu_w", "sgu_b", "norm_xattn_g", "mem_norm_g", "norm_ffn_g",
         "conv_w", "conv_b", "final_norm_g")
ORDER = ("norm_mix_g", "w_in", "pool_w", "pool_scale", "sgu_g", "sgu_w", "sgu_b", "w_out", "norm_xattn_g", "mem_norm_g",
         "wq", "wk", "wv", "wo", "norm_ffn_g", "w_up", "conv_w", "conv_b", "w_down", "final_norm_g")
PACK_WIDTH = 512


def kernel(x, mem, norm_mix_g, w_in, pool_w, pool_scale, sgu_g, sgu_w, sgu_b, w_out, norm_xattn_g, mem_norm_g, wq, wk, wv, wo, norm_ffn_g, w_up, conv_w, conv_b, w_down, final_norm_g, loss_target, m_norm_mix_g, m_w_in, m_pool_w, m_pool_scale, m_sgu_g, m_sgu_w, m_sgu_b, m_w_out, m_norm_xattn_g, m_mem_norm_g, m_wq, m_wk, m_wv, m_wo, m_norm_ffn_g, m_w_up, m_conv_w, m_conv_b, m_w_down, m_final_norm_g, v_norm_mix_g, v_w_in, v_pool_w, v_pool_scale, v_sgu_g, v_sgu_w, v_sgu_b, v_w_out, v_norm_xattn_g, v_mem_norm_g, v_wq, v_wk, v_wv, v_wo, v_norm_ffn_g, v_w_up, v_conv_w, v_conv_b, v_w_down, v_final_norm_g):
    given = dict(locals())
    w = {n: given[n] for n in ORDER}
    mom = {n: given["m_" + n] for n in ORDER}
    var = {n: given["v_" + n] for n in ORDER}
    nl = w_in.shape[0]
    xs, mems, tgt = x[0], mem[0], loss_target[0]
    chip = 2 * lax.axis_index("x") + lax.axis_index("y")
    core = lax.axis_index("c")

    axes_of = lambda names: [BIG_AXIS[n] for n in names]
    alike = {}
    for n in BIG:
        alike.setdefault((w[n].shape, BIG_AXIS[n]), []).append(n)
    placed = [{} for _ in range(nl)]
    for (_, axis), names in alike.items():
        for n, per_layer in zip(names, _cast_place([w[n] for n in names], axis, chip, "place_" + names[0])):
            for l in range(nl):
                placed[l][n] = per_layer[l]
    conv_full = _allgather_conv(conv_w)

    def gather(names, l):
        return _hosted_allgather([placed[l][n] for n in names], axes_of(names))

    full = [dict(zip(MIXER, _run_hosted(gather(MIXER, 0), "allgather_weights")))]

    row = lambda a, l: a[l][None, :]
    saved = []
    h = xs
    for l in range(nl):
        fw = full[l]
        sbt = jnp.broadcast_to(sgu_b[l][:, :, None], sgu_w[l].shape)
        (h1, proj, xn1, mix), got = _mixer_fwd(h, row(norm_mix_g, l), fw["w_in"], pool_w[l], row(pool_scale, l), row(sgu_g, l), sgu_w[l], sbt, fw["w_out"],
                                               [gather(ATTN, 0), gather(("w_down",), 0)] if l == 0 else None)
        if l == 0:
            fw.update(zip(ATTN, got[0]))
            fw["w_down"] = got[1][0]
        k, v, memn = _kv_fwd(mems, row(mem_norm_g, l), fw["wk"], fw["wv"])
        (h2, q, o, xn2), got = _xattn_fwd(h1, row(norm_xattn_g, l), fw["wq"], k, v, fw["wo"], [gather(("w_up",), 0)] if l == 0 else None)
        if l == 0:
            fw["w_up"] = got[0][0]
        outs, got = _ffn_fwd(h2, row(norm_ffn_g, l), fw["w_up"], conv_full[l], row(conv_b, l), fw["w_down"],
                             [gather(BIG, l + 1)] if l + 1 < nl else None, None if l + 1 < nl else (final_norm_g[None, :], tgt))
        h3, hh, hc = outs[:3]
        if l + 1 < nl:
            full.append(dict(zip(BIG, got[0])))
        else:
            dh, loss_part, g_final = h3, outs[3], outs[4]
        saved.append(dict(h=h, h1=h1, h2=h2, proj=proj, xn1=xn1, mix=mix, k=k, v=v, memn=memn, q=q, o=o, xn2=xn2, hh=hh, hc=hc, sbt=sbt))
        h = h3


    big_grads = {}
    small_grads = [None] * nl

    def weight_grad(n, a, b, l, hosted=None):
        big_grads[n], got = _grad_matmul(a, b, "grad_" + n, l, nl, big_grads.get(n), hosted)
        return got

    sums, slots = {}, {}

    def exchange(names, l):
        return _hosted_exchange([big_grads[n] for n in names], axes_of(names), l)

    def scatter(names, l):
        return _hosted_scatter([sums[n, l] for n in names], axes_of(names))

    def add_casts(names, theirs, l):
        theirs = dict(zip(names, theirs))
        for group in alike.values():
            group = [n for n in group if n in theirs]
            if not group:
                continue
            gl, gr, gc = big_grads[group[0]].shape
            if BIG_AXIS[group[0]] == 0:
                outs = _add_cast([big_grads[n].reshape(gl * 4, 2, gr // 8, gc) for n in group], [theirs[n] for n in group],
                                 core, l * 4, "grad_chip_sum_" + group[0])
            else:
                outs = [o[0] for o in _add_cast([big_grads[n].reshape(gl, 2, gr // 2, gc) for n in group], [theirs[n][None] for n in group],
                                                core, l, "grad_chip_sum_" + group[0])]
            for n, o in zip(group, outs):
                sums[n, l] = o

    def keep_slots(names, got, l):
        for n, sl in zip(names, got):
            slots[n, l] = sl

    for l in reversed(range(nl)):
        fw, s = full[l], saved[l]
        above = l + 1 < nl
        dh3 = dh
        (dh2, dhh, act, xn3, g_cw, g_cb, g_nf), got = _ffn_bwd(dh3, s["h2"], s["hh"], s["hc"], row(norm_ffn_g, l), fw["w_up"], conv_full[l], fw["w_down"],
                                                         [exchange(MIXER, l + 1), scatter(ATTN, l + 1)] if above else None)
        if above:
            add_casts(MIXER, got[0], l + 1)
            keep_slots(ATTN, got[1], l + 1)
        weight_grad("w_up", xn3, dhh, l)
        weight_grad("w_down", act, dh3, l)
        (dh1, dq, dk, dv, g_nx), got = _xattn_bwd(dh2, s["h1"], s["q"], row(norm_xattn_g, l), fw["wq"], s["k"], s["v"], fw["wo"],
                                                  [exchange(MLP, l), scatter(MIXER, l + 1) if above else None])
        add_casts(MLP, got[0], l)
        if above:
            keep_slots(MIXER, got[1], l + 1)
        weight_grad("wq", s["xn2"], dq, l)
        weight_grad("wo", s["o"], dh2, l)
        weight_grad("wk", s["memn"], dk, l)
        weight_grad("wv", s["memn"], dv, l)
        g_mn = _kv_bwd(dk, dv, mems, fw["wk"], fw["wv"])
        (dh0, dproj, g_nm, g_pw, g_ps, g_sg, g_sw, g_sbt), got = _mixer_bwd(dh1, s["h"], s["proj"], row(norm_mix_g, l), fw["w_in"], pool_w[l], row(pool_scale, l), row(sgu_g, l), sgu_w[l], s["sbt"], fw["w_out"],
                                                                           [scatter(MLP, l), exchange(ATTN, l)])
        keep_slots(MLP, got[0], l)
        add_casts(ATTN, got[1], l)
        got = weight_grad("w_in", s["xn1"], dproj, l, [scatter(ATTN, l)] if l == 0 else None)
        if l == 0:
            keep_slots(ATTN, got[0], l)
        weight_grad("w_out", s["mix"], dh1, l)
        small_grads[l] = dict(norm_mix_g=g_nm, pool_w=g_pw, pool_scale=g_ps, sgu_g=g_sg, sgu_w=g_sw, sgu_b=jnp.sum(g_sbt, axis=-1),
                              norm_xattn_g=g_nx, mem_norm_g=g_mn, norm_ffn_g=g_nf, conv_w=g_cw, conv_b=g_cb)
        dh = dh0
    grad_x = dh[None]

    add_casts(MIXER, _run_hosted(exchange(MIXER, 0), "grad_sibling_exchange"), 0)

    layered = [n for n in SMALL if n != "final_norm_g"]
    parts = [small_grads[l][n].reshape(-1, PACK_WIDTH) for n in layered for l in range(nl)]
    parts.append(g_final.reshape(-1, PACK_WIDTH))
    parts.append(jnp.pad(loss_part, ((0, 0), (0, PACK_WIDTH - 1))))
    used = sum(p.shape[0] for p in parts)
    total = -(-used // 16) * 16
    packed, got = _allreduce_small(jnp.concatenate(parts + [jnp.zeros((total - used, PACK_WIDTH), F32)], axis=0), scatter(MIXER, 0))
    keep_slots(MIXER, got, 0)

    halves = []
    for n in BIG:
        buf = None
        for l in range(nl):
            buf = _sum_slots(sums[n, l], slots[n, l], BIG_AXIS[n], chip, core, l, nl, "grad_sum_" + n, buf)
        halves.append(buf.reshape(nl, 2 * buf.shape[2], buf.shape[3]))
    shard_grads = dict(zip(BIG, _sibling_assemble(halves)))
    grads = dict(shard_grads)
    at = 0
    for n in layered:
        per_layer = []
        for l in range(nl):
            shape = small_grads[l][n].shape
            nrow = small_grads[l][n].size // PACK_WIDTH
            per_layer.append(packed[at:at + nrow].reshape(shape))
            at += nrow
        g = jnp.stack(per_layer)
        if n == "conv_w":
            cs = conv_w.shape[2]
            g = lax.dynamic_slice_in_dim(g, chip * cs, cs, axis=2)
        grads[n] = g.reshape(w[n].shape)
    grads["final_norm_g"] = packed[at:at + g_final.size // PACK_WIDTH].reshape(final_norm_g.shape)
    at += g_final.size // PACK_WIDTH
    loss = packed[at, 0]

    delta, new_m, new_v = {}, {}, {}
    for names in alike.values():
        outs = _adamw_big([w[n] for n in names], [grads[n] for n in names], [mom[n] for n in names], [var[n] for n in names], "adamw_" + names[0])
        for n, out in zip(names, outs):
            grads[n], delta[n], new_m[n], new_v[n] = out
    two_d = lambda a: a.reshape(-1, a.shape[-1])
    ds, nms, nvs = _adamw_small([two_d(w[n]) for n in SMALL], [two_d(grads[n]) for n in SMALL],
                                [two_d(mom[n]) for n in SMALL], [two_d(var[n]) for n in SMALL])
    for n, d_, m_, v_ in zip(SMALL, ds, nms, nvs):
        delta[n], new_m[n], new_v[n] = d_.reshape(w[n].shape), m_.reshape(w[n].shape), v_.reshape(w[n].shape)

    return (loss, grad_x, *[grads[n] for n in ORDER], *[delta[n] for n in ORDER], *[new_m[n] for n in ORDER], *[new_v[n] for n in ORDER])
```

```python
import math
from typing import NamedTuple

import jax
import jax.numpy as jnp
from jax import lax
from jax.experimental import pallas as pl
from jax.experimental.pallas import tpu as pltpu

F32 = jnp.float32
_MXU = jnp.bfloat16
_PAY = jnp.bfloat16
EPS = 1e-6
WINDOWS = (2, 4, 8, 16)
GROUP = 128
N_XHEADS = 4
HALO = 16
FF_TILE = 256
DOWN_TILES = 6
VMEM_LIMIT = 60 * 1024 * 1024
MESH = pl.DeviceIdType.MESH

ADAM_LR, ADAM_B1, ADAM_B2, ADAM_EPS, ADAM_WD, ADAM_STEP = 0.001, 0.9, 0.999, 1e-08, 0.01, 10

VM = pl.BlockSpec(memory_space=pltpu.VMEM)
HB = pl.BlockSpec(memory_space=pltpu.HBM)


def _nn(a, b):
    return jnp.dot(a, b, preferred_element_type=F32)


def _nt(a, b):
    return lax.dot_general(a, b, (((1,), (1,)), ((), ())), preferred_element_type=F32)


def _tn(a, b):
    return lax.dot_general(a, b, (((0,), (0,)), ((), ())), preferred_element_type=F32)


def _rms(x):
    r = lax.rsqrt(jnp.mean(x * x, axis=-1, keepdims=True) + EPS)
    return x * r, r


def _rms_bwd(dxn, xhat, r, g):
    dxh = dxn * g
    dx = r * (dxh - xhat * jnp.mean(dxh * xhat, axis=-1, keepdims=True))
    return dx, jnp.sum(dxn * xhat, axis=0, keepdims=True)


def _gelu(x):
    cdf = 0.5 * (1.0 + lax.erf(x * (2.0 ** -0.5)))
    return x * cdf, cdf


def _gelu_grad(x, cdf):
    return cdf + x * jnp.exp(-0.5 * x * x) * ((2.0 * math.pi) ** -0.5)


def _params(sem=None):
    return pltpu.CompilerParams(dimension_semantics=sem, vmem_limit_bytes=VMEM_LIMIT)


def _token_block(t, want):
    return want if t % want == 0 and t > want else GROUP


def _const_spec(shape):
    n = len(shape)
    return pl.BlockSpec(shape, lambda i: (0,) * n)


def _tril():
    return lax.broadcasted_iota(jnp.int32, (GROUP, GROUP), 0) >= lax.broadcasted_iota(jnp.int32, (GROUP, GROUP), 1)


def _shift_rows(x, k, edge):
    tb = x.shape[0]
    r8 = lax.broadcasted_iota(jnp.int32, (8, 1), 0)
    rolled = pltpu.roll(x, k % tb, 0)
    if k > 0:
        top = jnp.where(r8 < k, pltpu.roll(edge, k, 0), rolled[0:8, :])
        return jnp.concatenate([top, rolled[8:, :]], axis=0)
    bottom = jnp.where(r8 >= 8 + k, pltpu.roll(edge, 8 + k, 0), rolled[tb - 8:, :])
    return jnp.concatenate([rolled[:tb - 8, :], bottom], axis=0)


def _in_turns(parts):
    parts = list(parts)
    while parts:
        for p in list(parts):
            try:
                next(p)
            except StopIteration:
                parts.remove(p)


class _Hosted(NamedTuple):
    operands: tuple
    aliased: bool
    out_shapes: tuple
    sems: tuple
    stages: tuple


def _hosted_results(hosted):
    if hosted.aliased:
        return [jax.ShapeDtypeStruct(o.shape, o.dtype) for o in hosted.operands]
    return list(hosted.out_shapes)


def _call_hosting(main_body, hosted, *, name, steps, in_specs, out_specs, out_shape, scratch_shapes, operands, aliases=None):
    grid = steps if isinstance(steps, tuple) else (steps,)
    semantics = ("arbitrary",) * len(grid)
    hosted = [hs for hs in (hosted or ()) if hs is not None]
    if not hosted:
        outs = pl.pallas_call(main_body, name=name, grid=grid, in_specs=in_specs, out_specs=out_specs, out_shape=out_shape,
                              scratch_shapes=scratch_shapes, input_output_aliases=aliases or {}, compiler_params=_params(semantics))(*operands)
        return outs, ()
    n_in, n_out, n_sc = len(in_specs), len(out_specs), len(scratch_shapes)
    shapes = [_hosted_results(hs) for hs in hosted]
    aliases, in_at, out_at = dict(aliases or {}), n_in, n_out
    for hs, sh in zip(hosted, shapes):
        if hs.aliased:
            aliases.update({in_at + i: out_at + i for i in range(len(hs.operands))})
        in_at += len(hs.operands)
        out_at += len(sh)

    def body(*refs):
        at = [0]

        def take(n):
            at[0] += n
            return refs[at[0] - n:at[0]]

        ins = take(n_in)
        h_in = [take(len(hs.operands)) for hs in hosted]
        outs = take(n_out)
        h_out = [take(len(sh)) for sh in shapes]
        scratch = take(n_sc)
        h_sems = [take(len(hs.sems)) for hs in hosted]
        ids = [pl.program_id(a) for a in range(len(grid))]

        def at_step(where):
            lead, rest = where
            ok = ids[0] == lead
            for a in range(1, len(grid)):
                ok = jnp.logical_and(ok, ids[a] == (grid[a] - 1 if rest else 0))
            return ok

        def run(stage):
            for hs, a, b, c in zip(hosted, h_in, h_out, h_sems):
                if hs.stages[stage] is not None:
                    hs.stages[stage](a, b, c)

        @pl.when(at_step((0, 0)))
        def _():
            run(0)

        if any(hs.stages[1] is not None for hs in hosted):
            @pl.when(at_step(((3 * grid[0]) // 4, 0)))
            def _():
                run(1)

        main_body(*ins, *outs, *scratch)

        @pl.when(at_step((grid[0] - 1, -1)))
        def _():
            run(2)

    flat = lambda lists: [x for xs in lists for x in xs]
    outs = pl.pallas_call(
        body, name=name, grid=grid, in_specs=list(in_specs) + [HB] * (in_at - n_in), out_specs=list(out_specs) + [HB] * (out_at - n_out),
        out_shape=list(out_shape) + flat(shapes), scratch_shapes=list(scratch_shapes) + flat(hs.sems for hs in hosted),
        input_output_aliases=aliases, compiler_params=_params(semantics),
    )(*operands, *flat(hs.operands for hs in hosted))
    results, at = [], n_out
    for sh in shapes:
        results.append(outs[at:at + len(sh)])
        at += len(sh)
    return outs[:n_out], results


def _run_hosted(hosted, name):
    nh = len(hosted.operands)
    h_shapes = _hosted_results(hosted)

    def body(*refs):
        h_in, h_out, h_sems = refs[:nh], refs[nh:nh + len(h_shapes)], refs[nh + len(h_shapes):]
        for stage in hosted.stages:
            if stage is not None:
                stage(h_in, h_out, h_sems)

    return pl.pallas_call(
        body, name=name, in_specs=[HB] * nh, out_specs=[HB] * len(h_shapes), out_shape=h_shapes, scratch_shapes=list(hosted.sems),
        input_output_aliases={i: i for i in range(nh)} if hosted.aliased else {},
        compiler_params=pltpu.CompilerParams(has_side_effects=True),
    )(*hosted.operands)


def _window_sums(e, win, back):
    n = e.shape[0]
    k = 1
    while k < win:
        e = e + pltpu.roll(e, k if back else n - k, 0)
        k *= 2
    return e


def _pool_diff(prev, p, t0, gi, win):
    sl = slice(gi * GROUP, (gi + 1) * GROUP)
    tb = p.shape[0]
    s = _window_sums(jnp.concatenate([prev[:, sl], p[:, sl]], axis=0), win, True)[HALO:, :]
    tglob = t0 + lax.broadcasted_iota(jnp.int32, (tb, 1), 0)
    cnt = jnp.minimum(tglob + 1, win).astype(F32)
    return s / cnt - p[:, sl], cnt


def _layernorm(v):
    xc = v - jnp.mean(v, axis=-1, keepdims=True)
    rstd = lax.rsqrt(jnp.mean(xc * xc, axis=-1, keepdims=True) + EPS)
    return xc * rstd, rstd


def _mixer_fwd(h, g, w_in, pool_w, pool_scale, sgu_g, sgu_w, sgu_bt, w_out, hosted=None):
    t, d = h.shape
    pw = pool_w.shape[0] * GROUP
    sw = sgu_w.shape[0] * GROUP
    tb = _token_block(t, 512)

    def body(h_ref, g_ref, win_ref, pw_ref, ps_ref, sg_ref, sw_ref, sbt_ref, wout_ref, h1_ref, proj_ref, xn_ref, mix_ref, pext):
        i = pl.program_id(0)

        @pl.when(i == 0)
        def _():
            pext[...] = jnp.zeros((HALO, pw), F32)

        x = h_ref[...]
        xhat, _ = _rms(x)
        xn = (xhat * g_ref[...]).astype(_MXU)
        xn_ref[...] = xn
        proj = _nn(xn, win_ref[...])
        proj_ref[...] = proj
        p = proj[:, :pw]
        prev = pext[...]
        for gi, win in enumerate(WINDOWS):
            sl = slice(gi * GROUP, (gi + 1) * GROUP)
            dg, _ = _pool_diff(prev, p, i * tb, gi, win)
            e = _nn(dg.astype(_MXU), pw_ref[gi].astype(_MXU))
            mix_ref[:, sl] = (e * ps_ref[:, sl]).astype(_MXU)
        pext[...] = p[tb - HALO:tb, :]
        uv, _ = _gelu(proj[:, pw:])
        u = uv[:, :sw]
        vhat, _ = _layernorm(uv[:, sw:])
        vn = (vhat * sg_ref[...]).astype(_MXU)
        mask = _tril()
        chunks = [slice(n * GROUP, (n + 1) * GROUP) for n in range(tb // GROUP)]
        for hh in range(sw // GROUP):
            wm = jnp.where(mask, sw_ref[hh], 0.0).astype(_MXU)
            cols = slice(hh * GROUP, (hh + 1) * GROUP)
            z = _nn(wm, jnp.concatenate([vn[rows, cols] for rows in chunks], axis=1))
            for n, rows in enumerate(chunks):
                mix_ref[rows, pw + hh * GROUP:pw + (hh + 1) * GROUP] = (u[rows, cols] * (z[:, chunks[n]] + sbt_ref[hh])).astype(_MXU)
        h1_ref[...] = x + _nn(mix_ref[...], wout_ref[...])

    blk = lambda w: pl.BlockSpec((tb, w), lambda i: (i, 0))
    return _call_hosting(
        body, hosted, name="mixer_fwd", steps=t // tb,
        in_specs=[blk(d), VM, VM, VM, VM, VM, VM, VM, VM],
        out_specs=[blk(d), blk(w_in.shape[1]), blk(d), blk(d)],
        out_shape=[jax.ShapeDtypeStruct((t, d), F32), jax.ShapeDtypeStruct((t, w_in.shape[1]), F32),
                   jax.ShapeDtypeStruct((t, d), _MXU), jax.ShapeDtypeStruct((t, d), _MXU)],
        scratch_shapes=[pltpu.VMEM((HALO, pw), F32)],
        operands=(h, g, w_in, pool_w, pool_scale, sgu_g, sgu_w, sgu_bt, w_out))


def _mixer_bwd(dh1, h, proj, g, w_in, pool_w, pool_scale, sgu_g, sgu_w, sgu_bt, w_out, hosted=None):
    t, d = h.shape
    ng, nh = pool_w.shape[0], sgu_w.shape[0]
    pw, sw = ng * GROUP, nh * GROUP
    tb = _token_block(t, 512)
    nb = t // tb
    n_parts = 2 if tb % (2 * GROUP) == 0 else 1
    pt = tb // n_parts

    def body(dh1_ref, h_ref, proj_ref, halo_ref, g_ref, win_ref, pw_ref, ps_ref, sg_ref, sw_ref, sbt_ref, wout_ref,
             dh_ref, dproj_ref, gg_ref, gpw_ref, gps_ref, gsg_ref, gsw_ref, gsbt_ref, dext, duv):
        i = pl.program_id(0)
        blk = nb - 1 - i

        @pl.when(i == 0)
        def _():
            for r in (gg_ref, gpw_ref, gps_ref, gsg_ref, gsw_ref, gsbt_ref, dext):
                r[...] = jnp.zeros(r.shape, F32)

        mask = _tril()

        def part(at):
            rows = slice(at, at + pt)
            dh1v = dh1_ref[rows, :]
            dmix = _nt(dh1v.astype(_MXU), wout_ref[...])
            yield
            proj_v = proj_ref[rows, :]
            p = proj_v[:, :pw]
            prev = jnp.where(blk == 0, 0.0, halo_ref[...]) if at == 0 else proj_ref[at - HALO:at, 0:pw]
            for gi, win in enumerate(WINDOWS):
                sl = slice(gi * GROUP, (gi + 1) * GROUP)
                dg, cnt = _pool_diff(prev, p, blk * tb + at, gi, win)
                dgm = dg.astype(_MXU)
                pwm = pw_ref[gi].astype(_MXU)
                e = _nn(dgm, pwm)
                dy = dmix[:, sl]
                gps_ref[:, sl] += jnp.sum(dy * e, axis=0, keepdims=True)
                de = (dy * ps_ref[:, sl]).astype(_MXU)
                gpw_ref[gi] += _tn(dgm, de)
                dd = _nt(de, pwm)
                ddc = dd / cnt
                acc = _window_sums(jnp.concatenate([ddc, dext[:, sl]], axis=0), win, False)[:pt, :]
                dext[:, sl] = ddc[0:HALO, :]
                dproj_ref[rows, sl] = (acc - dd).astype(_MXU)
            yield
            pre = proj_v[:, pw:]
            uv, cdf = _gelu(pre)
            u = uv[:, :sw]
            vhat, rstd = _layernorm(uv[:, sw:])
            vn = (vhat * sg_ref[...]).astype(_MXU)
            chunks = [slice(n * GROUP, (n + 1) * GROUP) for n in range(pt // GROUP)]
            side_by_side = lambda a, cols: jnp.concatenate([a[c, cols] for c in chunks], axis=1)
            for hh in range(nh):
                wm = jnp.where(mask, sw_ref[hh], 0.0).astype(_MXU)
                cols = slice(hh * GROUP, (hh + 1) * GROUP)
                vs = side_by_side(vn, cols)
                z = _nn(wm, vs)
                dy = side_by_side(dmix, slice(pw + hh * GROUP, pw + (hh + 1) * GROUP))
                dz = dy * side_by_side(u, cols)
                dzm = dz.astype(_MXU)
                dvs = _tn(wm, dzm)
                gsw_ref[hh] += jnp.where(mask, _nt(dzm, vs), 0.0)
                gb = jnp.zeros((GROUP, GROUP), F32)
                for n, c in enumerate(chunks):
                    gb = gb + dz[:, c]
                    duv[at + n * GROUP:at + (n + 1) * GROUP, cols] = dy[:, c] * (z[:, c] + sbt_ref[hh])
                    duv[at + n * GROUP:at + (n + 1) * GROUP, sw + hh * GROUP:sw + (hh + 1) * GROUP] = dvs[:, c]
                gsbt_ref[hh] += gb
            yield
            dvn = duv[rows, sw:]
            gsg_ref[...] += jnp.sum(dvn * vhat, axis=0, keepdims=True)
            dxh = dvn * sg_ref[...]
            dv = rstd * (dxh - jnp.mean(dxh, axis=-1, keepdims=True) - vhat * jnp.mean(dxh * vhat, axis=-1, keepdims=True))
            gp = _gelu_grad(pre, cdf)
            dproj_ref[rows, pw:pw + sw] = (duv[rows, :sw] * gp[:, :sw]).astype(_MXU)
            dproj_ref[rows, pw + sw:] = (dv * gp[:, sw:]).astype(_MXU)
            dxn = _nt(dproj_ref[rows, :], win_ref[...])
            yield
            xhat, r = _rms(h_ref[rows, :])
            dx, gg = _rms_bwd(dxn, xhat, r, g_ref[...])
            gg_ref[...] += gg
            dh_ref[rows, :] = dh1v + dx

        _in_turns([part(at) for at in reversed(range(0, tb, pt))])

    rev = lambda w: pl.BlockSpec((tb, w), lambda i: (nb - 1 - i, 0))
    halo = pl.BlockSpec((HALO, pw), lambda i: (jnp.maximum((nb - 1 - i) * (tb // HALO) - 1, 0), 0))
    small = [(1, d), (ng, GROUP, GROUP), (1, pw), (1, sw), (nh, GROUP, GROUP), (nh, GROUP, GROUP)]
    return _call_hosting(
        body, hosted, name="mixer_bwd", steps=nb,
        in_specs=[rev(d), rev(d), rev(proj.shape[1]), halo, VM, VM, VM, VM, VM, VM, VM, VM],
        out_specs=[rev(d), rev(proj.shape[1])] + [_const_spec(s) for s in small],
        out_shape=[jax.ShapeDtypeStruct((t, d), F32), jax.ShapeDtypeStruct(proj.shape, _MXU)]
        + [jax.ShapeDtypeStruct(s, F32) for s in small],
        scratch_shapes=[pltpu.VMEM((HALO, pw), F32), pltpu.VMEM((tb, 2 * sw), F32)],
        operands=(dh1, h, proj, proj, g, w_in, pool_w, pool_scale, sgu_g, sgu_w, sgu_bt, w_out))


def _kv_fwd(mem, gm, wk, wv):
    n, d = mem.shape

    def body(mem_ref, gm_ref, wk_ref, wv_ref, k_ref, v_ref, memn_ref):
        xhat, _ = _rms(mem_ref[...])
        memn = (xhat * gm_ref[...]).astype(_MXU)
        memn_ref[...] = memn
        k_ref[...] = _nn(memn, wk_ref[...]).astype(_MXU)
        v_ref[...] = _nn(memn, wv_ref[...]).astype(_MXU)

    return pl.pallas_call(
        body, name="kv_fwd", in_specs=[VM] * 4, out_specs=[VM] * 3,
        out_shape=[jax.ShapeDtypeStruct((n, d), _MXU)] * 3, compiler_params=_params(),
    )(mem, gm, wk, wv)


def _kv_bwd(dk, dv, mem, wk, wv):
    n, d = mem.shape

    def body(dk_ref, dv_ref, mem_ref, wk_ref, wv_ref, ggm_ref):
        dmemn = _nt(dk_ref[...].astype(_MXU), wk_ref[...]) + _nt(dv_ref[...].astype(_MXU), wv_ref[...])
        xhat, _ = _rms(mem_ref[...])
        ggm_ref[...] = jnp.sum(dmemn * xhat, axis=0, keepdims=True)

    return pl.pallas_call(
        body, name="kv_bwd", in_specs=[VM] * 5, out_specs=VM,
        out_shape=jax.ShapeDtypeStruct((1, d), F32), compiler_params=_params(),
    )(dk, dv, mem, wk, wv)


def _softmax(s):
    e = jnp.exp(s - jnp.max(s, axis=-1, keepdims=True))
    return e / jnp.sum(e, axis=-1, keepdims=True)


def _one_ahead(n, issue):
    nxt = issue(0)
    for a in range(n):
        cur = nxt
        if a + 1 < n:
            nxt = issue(a + 1)
        yield a, cur


def _xattn_fwd(h, g, wq, k, v, wo, hosted=None):
    t, d = h.shape
    hd = d // N_XHEADS
    scale = hd ** -0.5
    tb = _token_block(t, 512)

    def body(h_ref, g_ref, wq_ref, k_ref, v_ref, wo_ref, h2_ref, q_ref, o_ref, xn_ref):
        x = h_ref[...]
        xhat, _ = _rms(x)
        xn = (xhat * g_ref[...]).astype(_MXU)
        xn_ref[...] = xn
        qm = _nn(xn, wq_ref[...]).astype(_MXU)
        q_ref[...] = qm
        heads = [slice(a * hd, (a + 1) * hd) for a in range(N_XHEADS)]
        for a, s in _one_ahead(N_XHEADS, lambda a: _nt(qm[:, heads[a]], k_ref[:, heads[a]]) * scale):
            o_ref[:, heads[a]] = _nn(_softmax(s).astype(_MXU), v_ref[:, heads[a]]).astype(_MXU)
        h2_ref[...] = x + _nn(o_ref[...], wo_ref[...])

    blk = pl.BlockSpec((tb, d), lambda i: (i, 0))
    return _call_hosting(
        body, hosted, name="xattn_fwd", steps=t // tb,
        in_specs=[blk, VM, VM, VM, VM, VM], out_specs=[blk] * 4,
        out_shape=[jax.ShapeDtypeStruct((t, d), F32)] + [jax.ShapeDtypeStruct((t, d), _MXU)] * 3,
        scratch_shapes=[], operands=(h, g, wq, k, v, wo))


def _xattn_bwd(dh2, h, q, g, wq, k, v, wo, hosted=None):
    t, d = h.shape
    n = k.shape[0]
    hd = d // N_XHEADS
    scale = hd ** -0.5
    tb = _token_block(t, 512)
    pt = tb // 2 if tb % (2 * GROUP) == 0 else tb

    def body(dh2_ref, h_ref, q_ref, g_ref, wq_ref, k_ref, v_ref, wo_ref, dh_ref, dq_ref, dk_ref, dv_ref, gg_ref):
        @pl.when(pl.program_id(0) == 0)
        def _():
            for r in (dk_ref, dv_ref, gg_ref):
                r[...] = jnp.zeros(r.shape, F32)

        heads = [slice(a * hd, (a + 1) * hd) for a in range(N_XHEADS)]

        def part(at):
            rows = slice(at, at + pt)
            dh2v = dh2_ref[rows, :]
            dom = _nt(dh2v.astype(_MXU), wo_ref[...]).astype(_MXU)
            yield
            issue = lambda a: (_nt(q_ref[rows, heads[a]], k_ref[:, heads[a]]) * scale, _nt(dom[:, heads[a]], v_ref[:, heads[a]]))
            for a, (s, dpr) in _one_ahead(N_XHEADS, issue):
                sl = heads[a]
                pr = _softmax(s)
                dv_ref[:, sl] += _tn(pr.astype(_MXU), dom[:, sl])
                ds = (pr * (dpr - jnp.sum(dpr * pr, axis=-1, keepdims=True)) * scale).astype(_MXU)
                dq_ref[rows, sl] = _nn(ds, k_ref[:, sl]).astype(_MXU)
                dk_ref[:, sl] += _tn(ds, q_ref[rows, sl])
                yield
            dxn = _nt(dq_ref[rows, :], wq_ref[...])
            yield
            xhat, r = _rms(h_ref[rows, :])
            dx, gg = _rms_bwd(dxn, xhat, r, g_ref[...])
            gg_ref[...] += gg
            dh_ref[rows, :] = dh2v + dx

        _in_turns([part(at) for at in range(0, tb, pt)])

    blk = pl.BlockSpec((tb, d), lambda i: (i, 0))
    return _call_hosting(
        body, hosted, name="xattn_bwd", steps=t // tb,
        in_specs=[blk, blk, blk, VM, VM, VM, VM, VM],
        out_specs=[blk, blk, _const_spec((n, d)), _const_spec((n, d)), _const_spec((1, d))],
        out_shape=[jax.ShapeDtypeStruct((t, d), F32), jax.ShapeDtypeStruct((t, d), _MXU),
                   jax.ShapeDtypeStruct((n, d), F32), jax.ShapeDtypeStruct((n, d), F32), jax.ShapeDtypeStruct((1, d), F32)],
        scratch_shapes=[], operands=(dh2, h, q, g, wq, k, v, wo))


def _ffn_fwd(h, g, w_up, conv_w, conv_b, w_down, hosted=None, head=None):
    t, d = h.shape
    f = w_down.shape[0]
    ft = FF_TILE
    tb = _token_block(t, 256)

    def body(h_ref, g_ref, wup_ref, cw_ref, cb_ref, wdown_ref, *rest):
        if head is None:
            h3_ref, hh_ref, hc_ref, ext, carry, act_sc = rest
        else:
            gf_ref, tgt_ref, h3_ref, hh_ref, hc_ref, loss_ref, ggf_ref, ext, carry, act_sc = rest

        @pl.when(pl.program_id(0) == 0)
        def _():
            carry[...] = jnp.zeros(carry.shape, F32)
            if head is not None:
                loss_ref[...] = jnp.zeros(loss_ref.shape, F32)
                ggf_ref[...] = jnp.zeros(ggf_ref.shape, F32)

        x = h_ref[...]
        xhat, _ = _rms(x)
        xn = (xhat * g_ref[...]).astype(_MXU)
        acc = jnp.zeros((tb, d), F32)
        up = lambda j: [_nn(xn, wup_ref[:, off:off + ft]) for off in (j * ft, f + j * ft)]
        up_next = up(0)
        for j in range(f // ft):
            hc = []
            up_cur = up_next
            if j + 1 < f // ft:
                up_next = up(j + 1)
            for part, off in enumerate((j * ft, f + j * ft)):
                cols = slice(off, off + ft)
                cur = up_cur[part]
                hh_ref[:, cols] = cur.astype(_MXU)
                ext[part, 0:8, :] = carry[:, cols]
                ext[part, 8:8 + tb, :] = cur
                carry[:, cols] = cur[tb - 8:tb, :]
                hc.append(cb_ref[:, cols] + cw_ref[0:1, cols] * ext[part, 6:6 + tb, :]
                          + cw_ref[1:2, cols] * ext[part, 7:7 + tb, :] + cw_ref[2:3, cols] * cur)
                hc_ref[:, cols] = hc[part].astype(_MXU)
            at = j % DOWN_TILES
            act_sc[:, at * ft:(at + 1) * ft] = (hc[0] * jax.nn.sigmoid(hc[0]) * hc[1]).astype(_MXU)
            if at + 1 == DOWN_TILES or j + 1 == f // ft:
                acc = acc + _nn(act_sc[:, 0:(at + 1) * ft], wdown_ref[(j - at) * ft:(j + 1) * ft, :])
        if head is None:
            h3_ref[...] = x + acc
        else:
            yhat, r = _rms(x + acc)
            err = yhat * gf_ref[...] - tgt_ref[...]
            loss_ref[...] += 0.5 * jnp.sum(jnp.sum(err * err, axis=-1, keepdims=True), axis=0, keepdims=True) / d
            dx, gg = _rms_bwd(err / d, yhat, r, gf_ref[...])
            ggf_ref[...] += gg
            h3_ref[...] = dx

    blk = lambda w: pl.BlockSpec((tb, w), lambda i: (i, 0))
    in_specs = [blk(d), VM, VM, VM, VM, VM]
    out_specs = [blk(d), blk(2 * f), blk(2 * f)]
    out_shape = [jax.ShapeDtypeStruct((t, d), F32), jax.ShapeDtypeStruct((t, 2 * f), _MXU), jax.ShapeDtypeStruct((t, 2 * f), _MXU)]
    operands = (h, g, w_up, conv_w, conv_b, w_down)
    if head is not None:
        in_specs += [VM, blk(d)]
        out_specs += [_const_spec((1, 1)), _const_spec((1, d))]
        out_shape += [jax.ShapeDtypeStruct((1, 1), F32), jax.ShapeDtypeStruct((1, d), F32)]
        operands += tuple(head)
    return _call_hosting(
        body, hosted, name="ffn_fwd", steps=t // tb, in_specs=in_specs, out_specs=out_specs, out_shape=out_shape,
        scratch_shapes=[pltpu.VMEM((2, 8 + tb, ft), F32), pltpu.VMEM((8, 2 * f), F32), pltpu.VMEM((tb, DOWN_TILES * ft), _MXU)],
        operands=operands)


def _ffn_bwd(dh3, h, hh, hc, g, w_up, conv_w, w_down, hosted=None):
    t, d = h.shape
    f = w_down.shape[0]
    ft = FF_TILE
    tb = _token_block(t, 256)
    nb = t // tb

    def body(dh3_ref, h_ref, hh_ref, hc_ref, g_ref, wup_ref, cw_ref, wdown_ref,
             dh_ref, dhh_ref, act_ref, xn_ref, gcw_ref, gcb_ref, gg_ref, dcarry):
        @pl.when(pl.program_id(0) == 0)
        def _():
            for r in (gcw_ref, gcb_ref, gg_ref, dcarry):
                r[...] = jnp.zeros(r.shape, F32)

        dh3v = dh3_ref[...]
        dhm = dh3v.astype(_MXU)
        dxn = jnp.zeros((tb, d), F32)
        dact_next = _nt(dhm, wdown_ref[0:ft, :])
        for j in range(f // ft):
            dact = dact_next
            if j + 1 < f // ft:
                dact_next = _nt(dhm, wdown_ref[(j + 1) * ft:(j + 2) * ft, :])
            gate = hc_ref[:, j * ft:(j + 1) * ft].astype(F32)
            val = hc_ref[:, f + j * ft:f + (j + 1) * ft].astype(F32)
            sg = jax.nn.sigmoid(gate)
            silu = gate * sg
            act_ref[:, j * ft:(j + 1) * ft] = (silu * val).astype(_MXU)
            dhc = (dact * val * sg * (1.0 + gate * (1.0 - sg)), dact * silu)
            for part, off in enumerate((j * ft, f + j * ft)):
                cols = slice(off, off + ft)
                dc = dhc[part]
                c0 = hh_ref[:, cols].astype(F32)
                after = dcarry[:, cols]
                ahead1 = _shift_rows(dc, -1, after)
                ahead2 = _shift_rows(dc, -2, after)
                dcarry[:, cols] = dc[0:8, :]
                gcb_ref[:, cols] += jnp.sum(dc, axis=0, keepdims=True)
                gcw_ref[0:1, cols] += jnp.sum(ahead2 * c0, axis=0, keepdims=True)
                gcw_ref[1:2, cols] += jnp.sum(ahead1 * c0, axis=0, keepdims=True)
                gcw_ref[2:3, cols] += jnp.sum(dc * c0, axis=0, keepdims=True)
                dhh = (cw_ref[2:3, cols] * dc + cw_ref[1:2, cols] * ahead1 + cw_ref[0:1, cols] * ahead2).astype(_MXU)
                dhh_ref[:, cols] = dhh
                dxn = dxn + _nt(dhh, wup_ref[:, cols])
        xhat, r = _rms(h_ref[...])
        xn_ref[...] = (xhat * g_ref[...]).astype(_MXU)
        dx, gg = _rms_bwd(dxn, xhat, r, g_ref[...])
        gg_ref[...] += gg
        dh_ref[...] = dh3v + dx

    rev = lambda w: pl.BlockSpec((tb, w), lambda i: (nb - 1 - i, 0))
    return _call_hosting(
        body, hosted, name="ffn_bwd", steps=nb,
        in_specs=[rev(d), rev(d), rev(2 * f), rev(2 * f), VM, VM, VM, VM],
        out_specs=[rev(d), rev(2 * f), rev(f), rev(d), _const_spec((3, 2 * f)), _const_spec((1, 2 * f)), _const_spec((1, d))],
        out_shape=[jax.ShapeDtypeStruct((t, d), F32), jax.ShapeDtypeStruct((t, 2 * f), _MXU), jax.ShapeDtypeStruct((t, f), _MXU),
                   jax.ShapeDtypeStruct((t, d), _MXU),
                   jax.ShapeDtypeStruct((3, 2 * f), F32), jax.ShapeDtypeStruct((1, 2 * f), F32), jax.ShapeDtypeStruct((1, d), F32)],
        scratch_shapes=[pltpu.VMEM((8, 2 * f), F32)],
        operands=(dh3, h, hh, hc, g, w_up, conv_w, w_down))


def _largest_tile(n, cap, mult=128):
    best = None
    for c in range(mult, min(n, cap) + 1, mult):
        if n % c == 0:
            best = c
    return best if best is not None else n


def _grad_matmul(a, b, name, layer, n_layers, into=None, hosted=None):
    t, m = a.shape
    n = b.shape[1]
    tm, tn, tk = _largest_tile(m, 1408), _largest_tile(n, 1536), _largest_tile(t, 1024)
    nk = t // tk

    def body(a_ref, b_ref, *rest):
        o_ref = rest[-1]

        @pl.when(pl.program_id(2) == 0)
        def _():
            o_ref[...] = jnp.zeros(o_ref.shape, F32)

        o_ref[...] += _tn(a_ref[...].astype(_MXU), b_ref[...].astype(_MXU))

    in_specs = [pl.BlockSpec((tk, tm), lambda i, j, k: (k, i)), pl.BlockSpec((tk, tn), lambda i, j, k: (k, j))]
    operands = (a, b)
    aliases = {}
    if into is not None:
        in_specs.append(pl.BlockSpec(memory_space=pl.ANY))
        operands = (a, b, into)
        aliases = {2: 0}
    (out,), got = _call_hosting(
        body, hosted, name=name, steps=(m // tm, n // tn, nk), in_specs=in_specs,
        out_specs=[pl.BlockSpec((None, tm, tn), lambda i, j, k: (layer, i, j))],
        out_shape=[jax.ShapeDtypeStruct((n_layers, m, n), F32)], scratch_shapes=[], operands=operands, aliases=aliases)
    return out, got


def _adamw_math(w, g, m, v):
    m = ADAM_B1 * m + (1.0 - ADAM_B1) * g
    v = ADAM_B2 * v + (1.0 - ADAM_B2) * (g * g)
    m_hat = m / (1.0 - ADAM_B1 ** ADAM_STEP)
    v_hat = v / (1.0 - ADAM_B2 ** ADAM_STEP)
    return -ADAM_LR * (m_hat / (jnp.sqrt(v_hat) + ADAM_EPS) + ADAM_WD * w), m, v


def _row_block(rows, cols, max_bytes=1 << 20, mult=16):
    best = None
    for r in range(mult, rows + 1, mult):
        if rows % r == 0 and r * cols * 4 <= max_bytes:
            best = r
    return best if best is not None else rows


def _adamw_big(ws, gs, ms, vs, name):
    n = len(ws)
    shape = ws[0].shape
    cols = shape[-1]
    flat = lambda a: a.reshape(-1, cols)
    rows = flat(ws[0]).shape[0]
    rb = _row_block(rows, cols, (2 << 20) // n)

    def body(*refs):
        for a in range(n):
            w_ref, g_ref, m_ref, v_ref = (refs[s * n + a] for s in range(4))
            go_ref, d_ref, nm_ref, nv_ref = (refs[(4 + s) * n + a] for s in range(4))
            g = g_ref[...]
            go_ref[...] = g
            d_ref[...], nm_ref[...], nv_ref[...] = _adamw_math(w_ref[...], g, m_ref[...], v_ref[...])

    blk = pl.BlockSpec((rb, cols), lambda i: (i, 0))
    outs = pl.pallas_call(
        body, name=name, grid=(rows // rb,), in_specs=[blk] * (4 * n), out_specs=[blk] * (4 * n),
        out_shape=[jax.ShapeDtypeStruct((rows, cols), F32)] * (4 * n), compiler_params=_params(("parallel",)),
    )(*[flat(a) for group in (ws, gs, ms, vs) for a in group])
    return [[outs[s * n + a].reshape(shape) for s in range(4)] for a in range(n)]


def _adamw_small(ws, gs, ms, vs):
    n = len(ws)

    def body(*refs):
        for a in range(n):
            w_ref, g_ref, m_ref, v_ref = (refs[s * n + a] for s in range(4))
            d_ref, nm_ref, nv_ref = (refs[(4 + s) * n + a] for s in range(3))
            d_ref[...], nm_ref[...], nv_ref[...] = _adamw_math(w_ref[...], g_ref[...], m_ref[...], v_ref[...])

    outs = pl.pallas_call(
        body, name="adamw_small", in_specs=[VM] * (4 * n), out_specs=[VM] * (3 * n),
        out_shape=[jax.ShapeDtypeStruct(w.shape, F32) for w in ws] * 3, compiler_params=_params(),
    )(*ws, *gs, *ms, *vs)
    return outs[:n], outs[n:2 * n], outs[2 * n:]


def _place():
    x, y, c = lax.axis_index("x"), lax.axis_index("y"), lax.axis_index("c")
    chips = [(1 - x, y), (x, 1 - y), (1 - x, 1 - y)]
    return x, y, c, chips


def _rows(start, size, mult=16):
    return pl.ds(pl.multiple_of(start, mult), size)


def _full_window(ref, axis, chip, half=None):
    r, c = ref.shape
    if axis == 0:
        rs = r // 4
        if half is None:
            return ref.at[_rows(chip * rs, rs), :]
        return ref.at[_rows(chip * rs + half * (rs // 2), rs // 2), :]
    cs = c // 4
    if half is None:
        return ref.at[:, _rows(chip * cs, cs, 128)]
    return ref.at[_rows(half * (r // 2), r // 2), _rows(chip * cs, cs, 128)]


def _remote(src, dst, send_sem, recv_sem, to):
    return pltpu.make_async_remote_copy(src_ref=src, dst_ref=dst, send_sem=send_sem, recv_sem=recv_sem,
                                        device_id=to, device_id_type=MESH)


def _scalars(*vals):
    return jnp.stack([jnp.asarray(v, jnp.int32) for v in vals])


def _cast_place(shards, axis, chip, name):
    n = len(shards)
    nl, rs, cs = shards[0].shape
    full = (rs * 4, cs) if axis == 0 else (rs, cs * 4)
    rb = _row_block(rs, cs, (4 << 20) // (n * nl))
    nrb = rs // rb

    def body(chip_ref, *refs):
        for a in range(n):
            for l in range(nl):
                refs[n + a * nl + l][...] = refs[a][l].astype(_PAY)

    if axis == 0:
        out_map = lambda i, chip_ref: (chip_ref[0] * nrb + i, 0)
    else:
        out_map = lambda i, chip_ref: (i, chip_ref[0])
    outs = pl.pallas_call(
        body, name=name,
        grid_spec=pltpu.PrefetchScalarGridSpec(
            num_scalar_prefetch=1, grid=(nrb,),
            in_specs=[pl.BlockSpec((nl, rb, cs), lambda i, chip_ref: (0, i, 0))] * n,
            out_specs=[pl.BlockSpec((rb, cs), out_map)] * (n * nl)),
        out_shape=[jax.ShapeDtypeStruct(full, _PAY)] * (n * nl), compiler_params=_params(("parallel",)),
    )(_scalars(chip), *shards)
    return [[outs[a * nl + l] for l in range(nl)] for a in range(n)]


def _hosted_allgather(placed, axes):
    n = len(placed)

    def each(outs, half_of):
        x, y, c, chips = _place()
        for i in range(n):
            for k, chip in enumerate(chips):
                yield i * 3 + k, (*chip, c), (x, y, 1 - c), _full_window(outs[i], axes[i], 2 * x + y, c), \
                    _full_window(outs[i], axes[i], 2 * chip[0] + chip[1], half_of(c))

    def start(_, outs, sems):
        send, recv, _, _ = sems
        for s, peer, _, mine, _ in each(outs, lambda c: c):
            _remote(mine, mine, send.at[s], recv.at[s], peer).start()

    def middle(_, outs, sems):
        send, recv, fsend, frecv = sems
        for s, _, sibling, _, got in each(outs, lambda c: c):
            _remote(got, got, send.at[s], recv.at[s], sibling).wait_recv()
            _remote(got, got, fsend.at[s], frecv.at[s], sibling).start()

    def finish(_, outs, sems):
        send, recv, fsend, frecv = sems
        for s, _, sibling, _, got in each(outs, lambda c: 1 - c):
            _remote(got, got, fsend.at[s], frecv.at[s], sibling).wait_recv()
        for s, peer, sibling, mine, got in each(outs, lambda c: c):
            _remote(mine, mine, send.at[s], recv.at[s], peer).wait_send()
            _remote(got, got, fsend.at[s], frecv.at[s], sibling).wait_send()

    return _Hosted(tuple(placed), True, (), (pltpu.SemaphoreType.DMA((n * 3,)),) * 4, (start, middle, finish))


def _allgather_conv(conv_shard):
    nl, taps, cs = conv_shard.shape

    def body(in_ref, out_ref, send, recv, local):
        x, y, c, chips = _place()
        mine = out_ref.at[:, :, _rows((2 * x + y) * cs, cs, 128)]
        own = pltpu.make_async_copy(in_ref, mine, local)
        own.start()
        sends = [_remote(in_ref, mine, send.at[k], recv.at[k], (*chip, c)) for k, chip in enumerate(chips)]
        for cp in sends:
            cp.start()
        for k, chip in enumerate(chips):
            got = out_ref.at[:, :, _rows((2 * chip[0] + chip[1]) * cs, cs, 128)]
            _remote(got, got, send.at[k], recv.at[k], (*chip, c)).wait_recv()
        for cp in sends:
            cp.wait_send()
        own.wait()

    return pl.pallas_call(
        body, name="allgather_conv", in_specs=[HB], out_specs=HB, out_shape=jax.ShapeDtypeStruct((nl, taps, cs * 4), conv_shard.dtype),
        scratch_shapes=[pltpu.SemaphoreType.DMA((3,)), pltpu.SemaphoreType.DMA((3,)), pltpu.SemaphoreType.DMA],
        compiler_params=pltpu.CompilerParams(has_side_effects=True),
    )(conv_shard)


def _hosted_exchange(grads, axes, layer):
    na = len(grads)
    views = [g.reshape(g.shape[0], 4, 2, g.shape[1] // 8, g.shape[2]) if ax == 0 else g for g, ax in zip(grads, axes)]

    def region(ref, axis, half):
        if axis == 0:
            return ref.at[layer, :, half]
        r = ref.shape[1]
        return ref.at[layer, _rows(half * (r // 2), r // 2), :]

    def copies(ins, land, sems):
        send, recv = sems
        x, y, c, _ = _place()
        return [_remote(region(ins[a], axes[a], 1 - c), land[a], send.at[a], recv.at[a], (x, y, 1 - c)) for a in range(na)]

    def start(ins, land, sems):
        for cp in copies(ins, land, sems):
            cp.start()

    def finish(ins, land, sems):
        for cp in copies(ins, land, sems):
            cp.wait()

    shapes = [(4, g.shape[1] // 8, g.shape[2]) if ax == 0 else (g.shape[1] // 2, g.shape[2]) for g, ax in zip(grads, axes)]
    return _Hosted(tuple(views), False, tuple(jax.ShapeDtypeStruct(s, F32) for s in shapes),
                   (pltpu.SemaphoreType.DMA((na,)),) * 2, (start, None, finish))


def _add_cast(mines, theirs, core, base, name):
    n = len(mines)
    na, nb, cols = theirs[0].shape
    rb = _row_block(nb, cols, (4 << 20) // n)

    def body(core_ref, *refs):
        for a in range(n):
            refs[2 * n + a][...] = (refs[a][...] + refs[n + a][...]).astype(_PAY)

    blk = pl.BlockSpec((None, rb, cols), lambda i, k, core_ref: (i, k, 0))
    return pl.pallas_call(
        body, name=name,
        grid_spec=pltpu.PrefetchScalarGridSpec(
            num_scalar_prefetch=1, grid=(na, nb // rb),
            in_specs=[pl.BlockSpec((None, None, rb, cols), lambda i, k, core_ref: (base + i, core_ref[0], k, 0))] * n + [blk] * n,
            out_specs=[blk] * n),
        out_shape=[jax.ShapeDtypeStruct((na, nb, cols), _PAY)] * n, compiler_params=_params(("parallel", "parallel")),
    )(_scalars(core), *mines, *theirs)


def _piece(ref, axis, chip):
    if axis == 0:
        return ref.at[chip]
    cs = ref.shape[1] // 4
    return ref.at[:, _rows(chip * cs, cs, 128)]


def _hosted_scatter(sums, axes):
    na = len(sums)

    def piece_shape(a):
        if axes[a] == 0:
            return (sums[a].shape[1], sums[a].shape[2])
        return (sums[a].shape[0], sums[a].shape[1] // 4)

    def copies(ins, slots, sems):
        send, recv = sems
        _, _, c, chips = _place()
        return [_remote(_piece(ins[a], axes[a], 2 * chip[0] + chip[1]), slots[a].at[k], send.at[a * 3 + k], recv.at[a * 3 + k], (*chip, c))
                for a in range(na) for k, chip in enumerate(chips)]

    def start(ins, slots, sems):
        for cp in copies(ins, slots, sems):
            cp.start()

    def finish(ins, slots, sems):
        for cp in copies(ins, slots, sems):
            cp.wait()

    return _Hosted(tuple(sums), False, tuple(jax.ShapeDtypeStruct((3,) + piece_shape(a), sums[a].dtype) for a in range(na)),
                   (pltpu.SemaphoreType.DMA((na * 3,)),) * 2, (start, None, finish))


def _sum_slots(sums, slots, axis, chip, core, layer, n_layers, name, into=None):
    _, hr, cs = slots.shape
    rb = _row_block(hr, cs, 4 << 20)

    def body(at_ref, own_ref, s_ref, *rest):
        rest[-1][...] = ((own_ref[...].astype(F32) + s_ref[0].astype(F32)) + s_ref[1].astype(F32)) + s_ref[2].astype(F32)

    if axis == 0:
        own = pl.BlockSpec((None, rb, cs), lambda k, at_ref: (at_ref[0], k, 0))
    else:
        own = pl.BlockSpec((rb, cs), lambda k, at_ref: (k, at_ref[0]))
    in_specs = [own, pl.BlockSpec((3, rb, cs), lambda k, at_ref: (0, k, 0))]
    operands = (sums, slots)
    aliases = {}
    if into is not None:
        in_specs.append(pl.BlockSpec(memory_space=pl.ANY))
        operands = (sums, slots, into)
        aliases = {3: 0}
    return pl.pallas_call(
        body, name=name,
        grid_spec=pltpu.PrefetchScalarGridSpec(
            num_scalar_prefetch=1, grid=(hr // rb,), in_specs=in_specs,
            out_specs=pl.BlockSpec((None, None, rb, cs), lambda k, at_ref: (layer, at_ref[1], k, 0))),
        out_shape=jax.ShapeDtypeStruct((n_layers, 2, hr, cs), F32), input_output_aliases=aliases,
        compiler_params=_params(("parallel",)),
    )(_scalars(chip, core), *operands)


def _sibling_assemble(shards):
    na = len(shards)

    def body(*refs):
        outs = refs[na:2 * na]
        send, recv = refs[2 * na:]
        x, y, c, _ = _place()
        copies = []
        for a in range(na):
            hr = outs[a].shape[1] // 2
            mine = outs[a].at[:, _rows(c * hr, hr), :]
            cp = _remote(mine, mine, send.at[a], recv.at[a], (x, y, 1 - c))
            cp.start()
            copies.append(cp)
        for cp in copies:
            cp.wait()

    return pl.pallas_call(
        body, name="grad_sibling_assemble", in_specs=[HB] * na, out_specs=[HB] * na,
        out_shape=[jax.ShapeDtypeStruct(s.shape, F32) for s in shards], input_output_aliases={a: a for a in range(na)},
        scratch_shapes=[pltpu.SemaphoreType.DMA((na,))] * 2,
        compiler_params=pltpu.CompilerParams(has_side_effects=True),
    )(*shards)


def _allreduce_small(buf, hosted):
    rows, w = buf.shape
    half = rows // 2
    nh, h_shapes = len(hosted.operands), _hosted_results(hosted)

    def body(buf_ref, *refs):
        h_in, out_ref, h_out = refs[:nh], refs[nh], refs[nh + 1:nh + 1 + len(h_shapes)]
        land, slots, red, sems_send, sems_recv = refs[nh + 1 + len(h_shapes):nh + 6 + len(h_shapes)]
        h_sems = refs[nh + 6 + len(h_shapes):]
        hosted.stages[0](h_in, h_out, h_sems)
        x, y, c, chips = _place()
        me = 2 * x + y
        sibling = (x, y, 1 - c)
        first = _remote(buf_ref, land, sems_send.at[0], sems_recv.at[0], sibling)
        first.start()
        first.wait()
        mine = pl.ds(pl.multiple_of(c * half, 8), half)
        slots[me] = buf_ref[mine, :] + land[mine, :]
        sends = []
        for k, chip in enumerate(chips):
            cp = _remote(slots.at[me], slots.at[me], sems_send.at[1 + k], sems_recv.at[1 + k], (*chip, c))
            cp.start()
            sends.append(cp)
        for k, chip in enumerate(chips):
            got = slots.at[2 * chip[0] + chip[1]]
            _remote(got, got, sems_send.at[1 + k], sems_recv.at[1 + k], sibling).wait_recv()
        red[...] = ((slots[0] + slots[1]) + slots[2]) + slots[3]
        out_ref[mine, :] = red[...]
        last = _remote(red, out_ref.at[mine, :], sems_send.at[4], sems_recv.at[4], sibling)
        last.start()
        theirs = out_ref.at[pl.ds(pl.multiple_of((1 - c) * half, 8), half), :]
        _remote(red, theirs, sems_send.at[4], sems_recv.at[4], sibling).wait_recv()
        for cp in sends:
            cp.wait_send()
        last.wait_send()
        hosted.stages[2](h_in, h_out, h_sems)

    outs = pl.pallas_call(
        body, name="allreduce_small", in_specs=[VM] + [HB] * nh, out_specs=[VM] + [HB] * len(h_shapes),
        out_shape=[jax.ShapeDtypeStruct((rows, w), F32)] + h_shapes,
        scratch_shapes=[pltpu.VMEM((rows, w), F32), pltpu.VMEM((4, half, w), F32), pltpu.VMEM((half, w), F32),
                        pltpu.SemaphoreType.DMA((5,)), pltpu.SemaphoreType.DMA((5,))] + list(hosted.sems),
        compiler_params=pltpu.CompilerParams(has_side_effects=True, vmem_limit_bytes=VMEM_LIMIT),
    )(buf, *hosted.operands)
    return outs[0], outs[1:]


BIG = ("w_in", "w_out", "wq", "wk", "wv", "wo", "w_up", "w_down")
MIXER, ATTN, MLP = ("w_in", "w_out"), ("wq", "wk", "wv", "wo"), ("w_up", "w_down")
BIG_AXIS = {"w_in": 1, "w_out": 0, "wq": 0, "wk": 0, "wv": 0, "wo": 0, "w_up": 1, "w_down": 0}
SMALL = ("norm_mix_g", "pool_w", "pool_scale", "sgu_g", "sgu_w", "sgu_b", "norm_xattn_g", "mem_norm_g", "norm_ffn_g",
         "conv_w", "conv_b", "final_norm_g")
ORDER = ("norm_mix_g", "w_in", "pool_w", "pool_scale", "sgu_g", "sgu_w", "sgu_b", "w_out", "norm_xattn_g", "mem_norm_g",
         "wq", "wk", "wv", "wo", "norm_ffn_g", "w_up", "conv_w", "conv_b", "w_down", "final_norm_g")
PACK_WIDTH = 512


def kernel(x, mem, norm_mix_g, w_in, pool_w, pool_scale, sgu_g, sgu_w, sgu_b, w_out, norm_xattn_g, mem_norm_g, wq, wk, wv, wo, norm_ffn_g, w_up, conv_w, conv_b, w_down, final_norm_g, loss_target, m_norm_mix_g, m_w_in, m_pool_w, m_pool_scale, m_sgu_g, m_sgu_w, m_sgu_b, m_w_out, m_norm_xattn_g, m_mem_norm_g, m_wq, m_wk, m_wv, m_wo, m_norm_ffn_g, m_w_up, m_conv_w, m_conv_b, m_w_down, m_final_norm_g, v_norm_mix_g, v_w_in, v_pool_w, v_pool_scale, v_sgu_g, v_sgu_w, v_sgu_b, v_w_out, v_norm_xattn_g, v_mem_norm_g, v_wq, v_wk, v_wv, v_wo, v_norm_ffn_g, v_w_up, v_conv_w, v_conv_b, v_w_down, v_final_norm_g):
    given = dict(locals())
    w = {n: given[n] for n in ORDER}
    mom = {n: given["m_" + n] for n in ORDER}
    var = {n: given["v_" + n] for n in ORDER}
    nl = w_in.shape[0]
    xs, mems, tgt = x[0], mem[0], loss_target[0]
    chip = 2 * lax.axis_index("x") + lax.axis_index("y")
    core = lax.axis_index("c")

    axes_of = lambda names: [BIG_AXIS[n] for n in names]
    alike = {}
    for n in BIG:
        alike.setdefault((w[n].shape, BIG_AXIS[n]), []).append(n)
    placed = [{} for _ in range(nl)]
    for (_, axis), names in alike.items():
        for n, per_layer in zip(names, _cast_place([w[n] for n in names], axis, chip, "place_" + names[0])):
            for l in range(nl):
                placed[l][n] = per_layer[l]
    conv_full = _allgather_conv(conv_w)

    def gather(names, l):
        return _hosted_allgather([placed[l][n] for n in names], axes_of(names))

    full = [dict(zip(MIXER, _run_hosted(gather(MIXER, 0), "allgather_weights")))]

    row = lambda a, l: a[l][None, :]
    saved = []
    h = xs
    for l in range(nl):
        fw = full[l]
        sbt = jnp.broadcast_to(sgu_b[l][:, :, None], sgu_w[l].shape)
        (h1, proj, xn1, mix), got = _mixer_fwd(h, row(norm_mix_g, l), fw["w_in"], pool_w[l], row(pool_scale, l), row(sgu_g, l), sgu_w[l], sbt, fw["w_out"],
                                               [gather(ATTN, 0), gather(("w_down",), 0)] if l == 0 else None)
        if l == 0:
            fw.update(zip(ATTN, got[0]))
            fw["w_down"] = got[1][0]
        k, v, memn = _kv_fwd(mems, row(mem_norm_g, l), fw["wk"], fw["wv"])
        (h2, q, o, xn2), got = _xattn_fwd(h1, row(norm_xattn_g, l), fw["wq"], k, v, fw["wo"], [gather(("w_up",), 0)] if l == 0 else None)
        if l == 0:
            fw["w_up"] = got[0][0]
        outs, got = _ffn_fwd(h2, row(norm_ffn_g, l), fw["w_up"], conv_full[l], row(conv_b, l), fw["w_down"],
                             [gather(BIG, l + 1)] if l + 1 < nl else None, None if l + 1 < nl else (final_norm_g[None, :], tgt))
        h3, hh, hc = outs[:3]
        if l + 1 < nl:
            full.append(dict(zip(BIG, got[0])))
        else:
            dh, loss_part, g_final = h3, outs[3], outs[4]
        saved.append(dict(h=h, h1=h1, h2=h2, proj=proj, xn1=xn1, mix=mix, k=k, v=v, memn=memn, q=q, o=o, xn2=xn2, hh=hh, hc=hc, sbt=sbt))
        h = h3


    big_grads = {}
    small_grads = [None] * nl

    def weight_grad(n, a, b, l, hosted=None):
        big_grads[n], got = _grad_matmul(a, b, "grad_" + n, l, nl, big_grads.get(n), hosted)
        return got

    sums, slots = {}, {}

    def exchange(names, l):
        return _hosted_exchange([big_grads[n] for n in names], axes_of(names), l)

    def scatter(names, l):
        return _hosted_scatter([sums[n, l] for n in names], axes_of(names))

    def add_casts(names, theirs, l):
        theirs = dict(zip(names, theirs))
        for group in alike.values():
            group = [n for n in group if n in theirs]
            if not group:
                continue
            gl, gr, gc = big_grads[group[0]].shape
            if BIG_AXIS[group[0]] == 0:
                outs = _add_cast([big_grads[n].reshape(gl * 4, 2, gr // 8, gc) for n in group], [theirs[n] for n in group],
                                 core, l * 4, "grad_chip_sum_" + group[0])
            else:
                outs = [o[0] for o in _add_cast([big_grads[n].reshape(gl, 2, gr // 2, gc) for n in group], [theirs[n][None] for n in group],
                                                core, l, "grad_chip_sum_" + group[0])]
            for n, o in zip(group, outs):
                sums[n, l] = o

    def keep_slots(names, got, l):
        for n, sl in zip(names, got):
            slots[n, l] = sl

    for l in reversed(range(nl)):
        fw, s = full[l], saved[l]
        above = l + 1 < nl
        dh3 = dh
        (dh2, dhh, act, xn3, g_cw, g_cb, g_nf), got = _ffn_bwd(dh3, s["h2"], s["hh"], s["hc"], row(norm_ffn_g, l), fw["w_up"], conv_full[l], fw["w_down"],
                                                         [exchange(MIXER, l + 1), scatter(ATTN, l + 1)] if above else None)
        if above:
            add_casts(MIXER, got[0], l + 1)
            keep_slots(ATTN, got[1], l + 1)
        weight_grad("w_up", xn3, dhh, l)
        weight_grad("w_down", act, dh3, l)
        (dh1, dq, dk, dv, g_nx), got = _xattn_bwd(dh2, s["h1"], s["q"], row(norm_xattn_g, l), fw["wq"], s["k"], s["v"], fw["wo"],
                                                  [exchange(MLP, l), scatter(MIXER, l + 1) if above else None])
        add_casts(MLP, got[0], l)
        if above:
            keep_slots(MIXER, got[1], l + 1)
        weight_grad("wq", s["xn2"], dq, l)
        weight_grad("wo", s["o"], dh2, l)
        weight_grad("wk", s["memn"], dk, l)
        weight_grad("wv", s["memn"], dv, l)
        g_mn = _kv_bwd(dk, dv, mems, fw["wk"], fw["wv"])
        (dh0, dproj, g_nm, g_pw, g_ps, g_sg, g_sw, g_sbt), got = _mixer_bwd(dh1, s["h"], s["proj"], row(norm_mix_g, l), fw["w_in"], pool_w[l], row(pool_scale, l), row(sgu_g, l), sgu_w[l], s["sbt"], fw["w_out"],
                                                                           [scatter(MLP, l), exchange(ATTN, l)])
        keep_slots(MLP, got[0], l)
        add_casts(ATTN, got[1], l)
        half = len(ATTN) // 2
        got = weight_grad("w_in", s["xn1"], dproj, l, [scatter(ATTN[:half], l)] if l == 0 else None)
        if l == 0:
            keep_slots(ATTN[:half], got[0], l)
        got = weight_grad("w_out", s["mix"], dh1, l, [scatter(ATTN[half:], l)] if l == 0 else None)
        if l == 0:
            keep_slots(ATTN[half:], got[0], l)
        small_grads[l] = dict(norm_mix_g=g_nm, pool_w=g_pw, pool_scale=g_ps, sgu_g=g_sg, sgu_w=g_sw, sgu_b=jnp.sum(g_sbt, axis=-1),
                              norm_xattn_g=g_nx, mem_norm_g=g_mn, norm_ffn_g=g_nf, conv_w=g_cw, conv_b=g_cb)
        dh = dh0
    grad_x = dh[None]

    add_casts(MIXER, _run_hosted(exchange(MIXER, 0), "grad_sibling_exchange"), 0)

    layered = [n for n in SMALL if n != "final_norm_g"]
    parts = [small_grads[l][n].reshape(-1, PACK_WIDTH) for n in layered for l in range(nl)]
    parts.append(g_final.reshape(-1, PACK_WIDTH))
    parts.append(jnp.pad(loss_part, ((0, 0), (0, PACK_WIDTH - 1))))
    used = sum(p.shape[0] for p in parts)
    total = -(-used // 16) * 16
    packed, got = _allreduce_small(jnp.concatenate(parts + [jnp.zeros((total - used, PACK_WIDTH), F32)], axis=0), scatter(MIXER, 0))
    keep_slots(MIXER, got, 0)

    halves = []
    for n in BIG:
        buf = None
        for l in range(nl):
            buf = _sum_slots(sums[n, l], slots[n, l], BIG_AXIS[n], chip, core, l, nl, "grad_sum_" + n, buf)
        halves.append(buf.reshape(nl, 2 * buf.shape[2], buf.shape[3]))
    shard_grads = dict(zip(BIG, _sibling_assemble(halves)))
    grads = dict(shard_grads)
    at = 0
    for n in layered:
        per_layer = []
        for l in range(nl):
            shape = small_grads[l][n].shape
            nrow = small_grads[l][n].size // PACK_WIDTH
            per_layer.append(packed[at:at + nrow].reshape(shape))
            at += nrow
        g = jnp.stack(per_layer)
        if n == "conv_w":
            cs = conv_w.shape[2]
            g = lax.dynamic_slice_in_dim(g, chip * cs, cs, axis=2)
        grads[n] = g.reshape(w[n].shape)
    grads["final_norm_g"] = packed[at:at + g_final.size // PACK_WIDTH].reshape(final_norm_g.shape)
    at += g_final.size // PACK_WIDTH
    loss = packed[at, 0]

    delta, new_m, new_v = {}, {}, {}
    for names in alike.values():
        outs = _adamw_big([w[n] for n in names], [grads[n] for n in names], [mom[n] for n in names], [var[n] for n in names], "adamw_" + names[0])
        for n, out in zip(names, outs):
            grads[n], delta[n], new_m[n], new_v[n] = out
    two_d = lambda a: a.reshape(-1, a.shape[-1])
    ds, nms, nvs = _adamw_small([two_d(w[n]) for n in SMALL], [two_d(grads[n]) for n in SMALL],
                                [two_d(mom[n]) for n in SMALL], [two_d(var[n]) for n in SMALL])
    for n, d_, m_, v_ in zip(SMALL, ds, nms, nvs):
        delta[n], new_m[n], new_v[n] = d_.reshape(w[n].shape), m_.reshape(w[n].shape), v_.reshape(w[n].shape)

    return (loss, grad_x, *[grads[n] for n in ORDER], *[delta[n] for n in ORDER], *[new_m[n] for n in ORDER], *[new_v[n] for n in ORDER])
```

```python
import math
from typing import NamedTuple

import jax
import jax.numpy as jnp
from jax import lax
from jax.experimental import pallas as pl
from jax.experimental.pallas import tpu as pltpu

F32 = jnp.float32
_MXU = jnp.bfloat16
_PAY = jnp.bfloat16
EPS = 1e-6
WINDOWS = (2, 4, 8, 16)
GROUP = 128
N_XHEADS = 4
HALO = 16
FF_TILE = 256
DOWN_TILES = 6
VMEM_LIMIT = 60 * 1024 * 1024
MESH = pl.DeviceIdType.MESH

ADAM_LR, ADAM_B1, ADAM_B2, ADAM_EPS, ADAM_WD, ADAM_STEP = 0.001, 0.9, 0.999, 1e-08, 0.01, 10

VM = pl.BlockSpec(memory_space=pltpu.VMEM)
HB = pl.BlockSpec(memory_space=pltpu.HBM)


def _nn(a, b):
    return jnp.dot(a, b, preferred_element_type=F32)


def _nt(a, b):
    return lax.dot_general(a, b, (((1,), (1,)), ((), ())), preferred_element_type=F32)


def _tn(a, b):
    return lax.dot_general(a, b, (((0,), (0,)), ((), ())), preferred_element_type=F32)


def _rms(x):
    r = lax.rsqrt(jnp.mean(x * x, axis=-1, keepdims=True) + EPS)
    return x * r, r


def _rms_bwd(dxn, xhat, r, g):
    dxh = dxn * g
    dx = r * (dxh - xhat * jnp.mean(dxh * xhat, axis=-1, keepdims=True))
    return dx, jnp.sum(dxn * xhat, axis=0, keepdims=True)


def _gelu(x):
    cdf = 0.5 * (1.0 + lax.erf(x * (2.0 ** -0.5)))
    return x * cdf, cdf


def _gelu_grad(x, cdf):
    return cdf + x * jnp.exp(-0.5 * x * x) * ((2.0 * math.pi) ** -0.5)


def _params(sem=None):
    return pltpu.CompilerParams(dimension_semantics=sem, vmem_limit_bytes=VMEM_LIMIT)


def _token_block(t, want):
    return want if t % want == 0 and t > want else GROUP


def _const_spec(shape):
    n = len(shape)
    return pl.BlockSpec(shape, lambda i: (0,) * n)


def _tril():
    return lax.broadcasted_iota(jnp.int32, (GROUP, GROUP), 0) >= lax.broadcasted_iota(jnp.int32, (GROUP, GROUP), 1)


def _shift_rows(x, k, edge):
    tb = x.shape[0]
    r8 = lax.broadcasted_iota(jnp.int32, (8, 1), 0)
    rolled = pltpu.roll(x, k % tb, 0)
    if k > 0:
        top = jnp.where(r8 < k, pltpu.roll(edge, k, 0), rolled[0:8, :])
        return jnp.concatenate([top, rolled[8:, :]], axis=0)
    bottom = jnp.where(r8 >= 8 + k, pltpu.roll(edge, 8 + k, 0), rolled[tb - 8:, :])
    return jnp.concatenate([rolled[:tb - 8, :], bottom], axis=0)


def _in_turns(parts):
    parts = list(parts)
    while parts:
        for p in list(parts):
            try:
                next(p)
            except StopIteration:
                parts.remove(p)


class _Hosted(NamedTuple):
    operands: tuple
    aliased: bool
    out_shapes: tuple
    sems: tuple
    stages: tuple


def _hosted_results(hosted):
    if hosted.aliased:
        return [jax.ShapeDtypeStruct(o.shape, o.dtype) for o in hosted.operands]
    return list(hosted.out_shapes)


def _call_hosting(main_body, hosted, *, name, steps, in_specs, out_specs, out_shape, scratch_shapes, operands, aliases=None):
    grid = steps if isinstance(steps, tuple) else (steps,)
    semantics = ("arbitrary",) * len(grid)
    hosted = [hs for hs in (hosted or ()) if hs is not None]
    if not hosted:
        outs = pl.pallas_call(main_body, name=name, grid=grid, in_specs=in_specs, out_specs=out_specs, out_shape=out_shape,
                              scratch_shapes=scratch_shapes, input_output_aliases=aliases or {}, compiler_params=_params(semantics))(*operands)
        return outs, ()
    n_in, n_out, n_sc = len(in_specs), len(out_specs), len(scratch_shapes)
    shapes = [_hosted_results(hs) for hs in hosted]
    aliases, in_at, out_at = dict(aliases or {}), n_in, n_out
    for hs, sh in zip(hosted, shapes):
        if hs.aliased:
            aliases.update({in_at + i: out_at + i for i in range(len(hs.operands))})
        in_at += len(hs.operands)
        out_at += len(sh)

    def body(*refs):
        at = [0]

        def take(n):
            at[0] += n
            return refs[at[0] - n:at[0]]

        ins = take(n_in)
        h_in = [take(len(hs.operands)) for hs in hosted]
        outs = take(n_out)
        h_out = [take(len(sh)) for sh in shapes]
        scratch = take(n_sc)
        h_sems = [take(len(hs.sems)) for hs in hosted]
        ids = [pl.program_id(a) for a in range(len(grid))]

        def at_step(where):
            lead, rest = where
            ok = ids[0] == lead
            for a in range(1, len(grid)):
                ok = jnp.logical_and(ok, ids[a] == (grid[a] - 1 if rest else 0))
            return ok

        def run(stage):
            for hs, a, b, c in zip(hosted, h_in, h_out, h_sems):
                if hs.stages[stage] is not None:
                    hs.stages[stage](a, b, c)

        @pl.when(at_step((0, 0)))
        def _():
            run(0)

        if any(hs.stages[1] is not None for hs in hosted):
            @pl.when(at_step(((3 * grid[0]) // 4, 0)))
            def _():
                run(1)

        main_body(*ins, *outs, *scratch)

        @pl.when(at_step((grid[0] - 1, -1)))
        def _():
            run(2)

    flat = lambda lists: [x for xs in lists for x in xs]
    outs = pl.pallas_call(
        body, name=name, grid=grid, in_specs=list(in_specs) + [HB] * (in_at - n_in), out_specs=list(out_specs) + [HB] * (out_at - n_out),
        out_shape=list(out_shape) + flat(shapes), scratch_shapes=list(scratch_shapes) + flat(hs.sems for hs in hosted),
        input_output_aliases=aliases, compiler_params=_params(semantics),
    )(*operands, *flat(hs.operands for hs in hosted))
    results, at = [], n_out
    for sh in shapes:
        results.append(outs[at:at + len(sh)])
        at += len(sh)
    return outs[:n_out], results


def _run_hosted(hosted, name):
    nh = len(hosted.operands)
    h_shapes = _hosted_results(hosted)

    def body(*refs):
        h_in, h_out, h_sems = refs[:nh], refs[nh:nh + len(h_shapes)], refs[nh + len(h_shapes):]
        for stage in hosted.stages:
            if stage is not None:
                stage(h_in, h_out, h_sems)

    return pl.pallas_call(
        body, name=name, in_specs=[HB] * nh, out_specs=[HB] * len(h_shapes), out_shape=h_shapes, scratch_shapes=list(hosted.sems),
        input_output_aliases={i: i for i in range(nh)} if hosted.aliased else {},
        compiler_params=pltpu.CompilerParams(has_side_effects=True),
    )(*hosted.operands)


def _window_sums(e, win, back):
    n = e.shape[0]
    k = 1
    while k < win:
        e = e + pltpu.roll(e, k if back else n - k, 0)
        k *= 2
    return e


def _pool_diff(prev, p, t0, gi, win):
    sl = slice(gi * GROUP, (gi + 1) * GROUP)
    tb = p.shape[0]
    s = _window_sums(jnp.concatenate([prev[:, sl], p[:, sl]], axis=0), win, True)[HALO:, :]
    tglob = t0 + lax.broadcasted_iota(jnp.int32, (tb, 1), 0)
    cnt = jnp.minimum(tglob + 1, win).astype(F32)
    return s / cnt - p[:, sl], cnt


def _layernorm(v):
    xc = v - jnp.mean(v, axis=-1, keepdims=True)
    rstd = lax.rsqrt(jnp.mean(xc * xc, axis=-1, keepdims=True) + EPS)
    return xc * rstd, rstd


def _mixer_fwd(h, g, w_in, pool_w, pool_scale, sgu_g, sgu_w, sgu_bt, w_out, hosted=None):
    t, d = h.shape
    pw = pool_w.shape[0] * GROUP
    sw = sgu_w.shape[0] * GROUP
    tb = _token_block(t, 512)

    def body(h_ref, g_ref, win_ref, pw_ref, ps_ref, sg_ref, sw_ref, sbt_ref, wout_ref, h1_ref, proj_ref, xn_ref, mix_ref, pext):
        i = pl.program_id(0)

        @pl.when(i == 0)
        def _():
            pext[...] = jnp.zeros((HALO, pw), F32)

        x = h_ref[...]
        xhat, _ = _rms(x)
        xn = (xhat * g_ref[...]).astype(_MXU)
        xn_ref[...] = xn
        proj = _nn(xn, win_ref[...])
        proj_ref[...] = proj
        p = proj[:, :pw]
        prev = pext[...]
        for gi, win in enumerate(WINDOWS):
            sl = slice(gi * GROUP, (gi + 1) * GROUP)
            dg, _ = _pool_diff(prev, p, i * tb, gi, win)
            e = _nn(dg.astype(_MXU), pw_ref[gi].astype(_MXU))
            mix_ref[:, sl] = (e * ps_ref[:, sl]).astype(_MXU)
        pext[...] = p[tb - HALO:tb, :]
        uv, _ = _gelu(proj[:, pw:])
        u = uv[:, :sw]
        vhat, _ = _layernorm(uv[:, sw:])
        vn = (vhat * sg_ref[...]).astype(_MXU)
        mask = _tril()
        chunks = [slice(n * GROUP, (n + 1) * GROUP) for n in range(tb // GROUP)]
        for hh in range(sw // GROUP):
            wm = jnp.where(mask, sw_ref[hh], 0.0).astype(_MXU)
            cols = slice(hh * GROUP, (hh + 1) * GROUP)
            z = _nn(wm, jnp.concatenate([vn[rows, cols] for rows in chunks], axis=1))
            for n, rows in enumerate(chunks):
                mix_ref[rows, pw + hh * GROUP:pw + (hh + 1) * GROUP] = (u[rows, cols] * (z[:, chunks[n]] + sbt_ref[hh])).astype(_MXU)
        h1_ref[...] = x + _nn(mix_ref[...], wout_ref[...])

    blk = lambda w: pl.BlockSpec((tb, w), lambda i: (i, 0))
    return _call_hosting(
        body, hosted, name="mixer_fwd", steps=t // tb,
        in_specs=[blk(d), VM, VM, VM, VM, VM, VM, VM, VM],
        out_specs=[blk(d), blk(w_in.shape[1]), blk(d), blk(d)],
        out_shape=[jax.ShapeDtypeStruct((t, d), F32), jax.ShapeDtypeStruct((t, w_in.shape[1]), F32),
                   jax.ShapeDtypeStruct((t, d), _MXU), jax.ShapeDtypeStruct((t, d), _MXU)],
        scratch_shapes=[pltpu.VMEM((HALO, pw), F32)],
        operands=(h, g, w_in, pool_w, pool_scale, sgu_g, sgu_w, sgu_bt, w_out))


def _mixer_bwd(dh1, h, proj, g, w_in, pool_w, pool_scale, sgu_g, sgu_w, sgu_bt, w_out, hosted=None):
    t, d = h.shape
    ng, nh = pool_w.shape[0], sgu_w.shape[0]
    pw, sw = ng * GROUP, nh * GROUP
    tb = _token_block(t, 512)
    nb = t // tb
    n_parts = 2 if tb % (2 * GROUP) == 0 else 1
    pt = tb // n_parts

    def body(dh1_ref, h_ref, proj_ref, halo_ref, g_ref, win_ref, pw_ref, ps_ref, sg_ref, sw_ref, sbt_ref, wout_ref,
             dh_ref, dproj_ref, gg_ref, gpw_ref, gps_ref, gsg_ref, gsw_ref, gsbt_ref, dext, duv):
        i = pl.program_id(0)
        blk = nb - 1 - i

        @pl.when(i == 0)
        def _():
            for r in (gg_ref, gpw_ref, gps_ref, gsg_ref, gsw_ref, gsbt_ref, dext):
                r[...] = jnp.zeros(r.shape, F32)

        mask = _tril()

        def part(at):
            rows = slice(at, at + pt)
            dh1v = dh1_ref[rows, :]
            dmix = _nt(dh1v.astype(_MXU), wout_ref[...])
            yield
            proj_v = proj_ref[rows, :]
            p = proj_v[:, :pw]
            prev = jnp.where(blk == 0, 0.0, halo_ref[...]) if at == 0 else proj_ref[at - HALO:at, 0:pw]
            for gi, win in enumerate(WINDOWS):
                sl = slice(gi * GROUP, (gi + 1) * GROUP)
                dg, cnt = _pool_diff(prev, p, blk * tb + at, gi, win)
                dgm = dg.astype(_MXU)
                pwm = pw_ref[gi].astype(_MXU)
                e = _nn(dgm, pwm)
                dy = dmix[:, sl]
                gps_ref[:, sl] += jnp.sum(dy * e, axis=0, keepdims=True)
                de = (dy * ps_ref[:, sl]).astype(_MXU)
                gpw_ref[gi] += _tn(dgm, de)
                dd = _nt(de, pwm)
                ddc = dd / cnt
                acc = _window_sums(jnp.concatenate([ddc, dext[:, sl]], axis=0), win, False)[:pt, :]
                dext[:, sl] = ddc[0:HALO, :]
                dproj_ref[rows, sl] = (acc - dd).astype(_MXU)
            yield
            pre = proj_v[:, pw:]
            uv, cdf = _gelu(pre)
            u = uv[:, :sw]
            vhat, rstd = _layernorm(uv[:, sw:])
            vn = (vhat * sg_ref[...]).astype(_MXU)
            chunks = [slice(n * GROUP, (n + 1) * GROUP) for n in range(pt // GROUP)]
            side_by_side = lambda a, cols: jnp.concatenate([a[c, cols] for c in chunks], axis=1)
            for hh in range(nh):
                wm = jnp.where(mask, sw_ref[hh], 0.0).astype(_MXU)
                cols = slice(hh * GROUP, (hh + 1) * GROUP)
                vs = side_by_side(vn, cols)
                z = _nn(wm, vs)
                dy = side_by_side(dmix, slice(pw + hh * GROUP, pw + (hh + 1) * GROUP))
                dz = dy * side_by_side(u, cols)
                dzm = dz.astype(_MXU)
                dvs = _tn(wm, dzm)
                gsw_ref[hh] += jnp.where(mask, _nt(dzm, vs), 0.0)
                gb = jnp.zeros((GROUP, GROUP), F32)
                for n, c in enumerate(chunks):
                    gb = gb + dz[:, c]
                    duv[at + n * GROUP:at + (n + 1) * GROUP, cols] = dy[:, c] * (z[:, c] + sbt_ref[hh])
                    duv[at + n * GROUP:at + (n + 1) * GROUP, sw + hh * GROUP:sw + (hh + 1) * GROUP] = dvs[:, c]
                gsbt_ref[hh] += gb
            yield
            dvn = duv[rows, sw:]
            gsg_ref[...] += jnp.sum(dvn * vhat, axis=0, keepdims=True)
            dxh = dvn * sg_ref[...]
            dv = rstd * (dxh - jnp.mean(dxh, axis=-1, keepdims=True) - vhat * jnp.mean(dxh * vhat, axis=-1, keepdims=True))
            gp = _gelu_grad(pre, cdf)
            dproj_ref[rows, pw:pw + sw] = (duv[rows, :sw] * gp[:, :sw]).astype(_MXU)
            dproj_ref[rows, pw + sw:] = (dv * gp[:, sw:]).astype(_MXU)
            dxn = _nt(dproj_ref[rows, :], win_ref[...])
            yield
            xhat, r = _rms(h_ref[rows, :])
            dx, gg = _rms_bwd(dxn, xhat, r, g_ref[...])
            gg_ref[...] += gg
            dh_ref[rows, :] = dh1v + dx

        _in_turns([part(at) for at in reversed(range(0, tb, pt))])

    rev = lambda w: pl.BlockSpec((tb, w), lambda i: (nb - 1 - i, 0))
    halo = pl.BlockSpec((HALO, pw), lambda i: (jnp.maximum((nb - 1 - i) * (tb // HALO) - 1, 0), 0))
    small = [(1, d), (ng, GROUP, GROUP), (1, pw), (1, sw), (nh, GROUP, GROUP), (nh, GROUP, GROUP)]
    return _call_hosting(
        body, hosted, name="mixer_bwd", steps=nb,
        in_specs=[rev(d), rev(d), rev(proj.shape[1]), halo, VM, VM, VM, VM, VM, VM, VM, VM],
        out_specs=[rev(d), rev(proj.shape[1])] + [_const_spec(s) for s in small],
        out_shape=[jax.ShapeDtypeStruct((t, d), F32), jax.ShapeDtypeStruct(proj.shape, _MXU)]
        + [jax.ShapeDtypeStruct(s, F32) for s in small],
        scratch_shapes=[pltpu.VMEM((HALO, pw), F32), pltpu.VMEM((tb, 2 * sw), F32)],
        operands=(dh1, h, proj, proj, g, w_in, pool_w, pool_scale, sgu_g, sgu_w, sgu_bt, w_out))


def _kv_fwd(mem, gm, wk, wv):
    n, d = mem.shape

    def body(mem_ref, gm_ref, wk_ref, wv_ref, k_ref, v_ref, memn_ref):
        xhat, _ = _rms(mem_ref[...])
        memn = (xhat * gm_ref[...]).astype(_MXU)
        memn_ref[...] = memn
        k_ref[...] = _nn(memn, wk_ref[...]).astype(_MXU)
        v_ref[...] = _nn(memn, wv_ref[...]).astype(_MXU)

    return pl.pallas_call(
        body, name="kv_fwd", in_specs=[VM] * 4, out_specs=[VM] * 3,
        out_shape=[jax.ShapeDtypeStruct((n, d), _MXU)] * 3, compiler_params=_params(),
    )(mem, gm, wk, wv)


def _kv_bwd(dk, dv, mem, wk, wv):
    n, d = mem.shape

    def body(dk_ref, dv_ref, mem_ref, wk_ref, wv_ref, ggm_ref):
        dmemn = _nt(dk_ref[...].astype(_MXU), wk_ref[...]) + _nt(dv_ref[...].astype(_MXU), wv_ref[...])
        xhat, _ = _rms(mem_ref[...])
        ggm_ref[...] = jnp.sum(dmemn * xhat, axis=0, keepdims=True)

    return pl.pallas_call(
        body, name="kv_bwd", in_specs=[VM] * 5, out_specs=VM,
        out_shape=jax.ShapeDtypeStruct((1, d), F32), compiler_params=_params(),
    )(dk, dv, mem, wk, wv)


def _softmax(s):
    e = jnp.exp(s - jnp.max(s, axis=-1, keepdims=True))
    return e / jnp.sum(e, axis=-1, keepdims=True)


def _one_ahead(n, issue):
    nxt = issue(0)
    for a in range(n):
        cur = nxt
        if a + 1 < n:
            nxt = issue(a + 1)
        yield a, cur


def _xattn_fwd(h, g, wq, k, v, wo, hosted=None):
    t, d = h.shape
    hd = d // N_XHEADS
    scale = hd ** -0.5
    tb = _token_block(t, 512)

    def body(h_ref, g_ref, wq_ref, k_ref, v_ref, wo_ref, h2_ref, q_ref, o_ref, xn_ref):
        x = h_ref[...]
        xhat, _ = _rms(x)
        xn = (xhat * g_ref[...]).astype(_MXU)
        xn_ref[...] = xn
        qm = _nn(xn, wq_ref[...]).astype(_MXU)
        q_ref[...] = qm
        heads = [slice(a * hd, (a + 1) * hd) for a in range(N_XHEADS)]
        for a, s in _one_ahead(N_XHEADS, lambda a: _nt(qm[:, heads[a]], k_ref[:, heads[a]]) * scale):
            o_ref[:, heads[a]] = _nn(_softmax(s).astype(_MXU), v_ref[:, heads[a]]).astype(_MXU)
        h2_ref[...] = x + _nn(o_ref[...], wo_ref[...])

    blk = pl.BlockSpec((tb, d), lambda i: (i, 0))
    return _call_hosting(
        body, hosted, name="xattn_fwd", steps=t // tb,
        in_specs=[blk, VM, VM, VM, VM, VM], out_specs=[blk] * 4,
        out_shape=[jax.ShapeDtypeStruct((t, d), F32)] + [jax.ShapeDtypeStruct((t, d), _MXU)] * 3,
        scratch_shapes=[], operands=(h, g, wq, k, v, wo))


def _xattn_bwd(dh2, h, q, g, wq, k, v, wo, hosted=None):
    t, d = h.shape
    n = k.shape[0]
    hd = d // N_XHEADS
    scale = hd ** -0.5
    tb = _token_block(t, 512)
    pt = tb // 2 if tb % (2 * GROUP) == 0 else tb

    def body(dh2_ref, h_ref, q_ref, g_ref, wq_ref, k_ref, v_ref, wo_ref, dh_ref, dq_ref, dk_ref, dv_ref, gg_ref):
        @pl.when(pl.program_id(0) == 0)
        def _():
            for r in (dk_ref, dv_ref, gg_ref):
                r[...] = jnp.zeros(r.shape, F32)

        heads = [slice(a * hd, (a + 1) * hd) for a in range(N_XHEADS)]

        def part(at):
            rows = slice(at, at + pt)
            dh2v = dh2_ref[rows, :]
            dom = _nt(dh2v.astype(_MXU), wo_ref[...]).astype(_MXU)
            yield
            issue = lambda a: (_nt(q_ref[rows, heads[a]], k_ref[:, heads[a]]) * scale, _nt(dom[:, heads[a]], v_ref[:, heads[a]]))
            for a, (s, dpr) in _one_ahead(N_XHEADS, issue):
                sl = heads[a]
                pr = _softmax(s)
                dv_ref[:, sl] += _tn(pr.astype(_MXU), dom[:, sl])
                ds = (pr * (dpr - jnp.sum(dpr * pr, axis=-1, keepdims=True)) * scale).astype(_MXU)
                dq_ref[rows, sl] = _nn(ds, k_ref[:, sl]).astype(_MXU)
                dk_ref[:, sl] += _tn(ds, q_ref[rows, sl])
                yield
            dxn = _nt(dq_ref[rows, :], wq_ref[...])
            yield
            xhat, r = _rms(h_ref[rows, :])
            dx, gg = _rms_bwd(dxn, xhat, r, g_ref[...])
            gg_ref[...] += gg
            dh_ref[rows, :] = dh2v + dx

        _in_turns([part(at) for at in range(0, tb, pt)])

    blk = pl.BlockSpec((tb, d), lambda i: (i, 0))
    return _call_hosting(
        body, hosted, name="xattn_bwd", steps=t // tb,
        in_specs=[blk, blk, blk, VM, VM, VM, VM, VM],
        out_specs=[blk, blk, _const_spec((n, d)), _const_spec((n, d)), _const_spec((1, d))],
        out_shape=[jax.ShapeDtypeStruct((t, d), F32), jax.ShapeDtypeStruct((t, d), _MXU),
                   jax.ShapeDtypeStruct((n, d), F32), jax.ShapeDtypeStruct((n, d), F32), jax.ShapeDtypeStruct((1, d), F32)],
        scratch_shapes=[], operands=(dh2, h, q, g, wq, k, v, wo))


def _ffn_fwd(h, g, w_up, conv_w, conv_b, w_down, hosted=None, head=None):
    t, d = h.shape
    f = w_down.shape[0]
    ft = FF_TILE
    tb = _token_block(t, 256)

    def body(h_ref, g_ref, wup_ref, cw_ref, cb_ref, wdown_ref, *rest):
        if head is None:
            h3_ref, hh_ref, hc_ref, ext, carry, act_sc = rest
        else:
            gf_ref, tgt_ref, h3_ref, hh_ref, hc_ref, loss_ref, ggf_ref, ext, carry, act_sc = rest

        @pl.when(pl.program_id(0) == 0)
        def _():
            carry[...] = jnp.zeros(carry.shape, F32)
            if head is not None:
                loss_ref[...] = jnp.zeros(loss_ref.shape, F32)
                ggf_ref[...] = jnp.zeros(ggf_ref.shape, F32)

        x = h_ref[...]
        xhat, _ = _rms(x)
        xn = (xhat * g_ref[...]).astype(_MXU)
        acc = jnp.zeros((tb, d), F32)
        up = lambda j: [_nn(xn, wup_ref[:, off:off + ft]) for off in (j * ft, f + j * ft)]
        up_next = up(0)
        for j in range(f // ft):
            hc = []
            up_cur = up_next
            if j + 1 < f // ft:
                up_next = up(j + 1)
            for part, off in enumerate((j * ft, f + j * ft)):
                cols = slice(off, off + ft)
                cur = up_cur[part]
                hh_ref[:, cols] = cur.astype(_MXU)
                ext[part, 0:8, :] = carry[:, cols]
                ext[part, 8:8 + tb, :] = cur
                carry[:, cols] = cur[tb - 8:tb, :]
                hc.append(cb_ref[:, cols] + cw_ref[0:1, cols] * ext[part, 6:6 + tb, :]
                          + cw_ref[1:2, cols] * ext[part, 7:7 + tb, :] + cw_ref[2:3, cols] * cur)
                hc_ref[:, cols] = hc[part].astype(_MXU)
            at = j % DOWN_TILES
            act_sc[:, at * ft:(at + 1) * ft] = (hc[0] * jax.nn.sigmoid(hc[0]) * hc[1]).astype(_MXU)
            if at + 1 == DOWN_TILES or j + 1 == f // ft:
                acc = acc + _nn(act_sc[:, 0:(at + 1) * ft], wdown_ref[(j - at) * ft:(j + 1) * ft, :])
        if head is None:
            h3_ref[...] = x + acc
        else:
            yhat, r = _rms(x + acc)
            err = yhat * gf_ref[...] - tgt_ref[...]
            loss_ref[...] += 0.5 * jnp.sum(jnp.sum(err * err, axis=-1, keepdims=True), axis=0, keepdims=True) / d
            dx, gg = _rms_bwd(err / d, yhat, r, gf_ref[...])
            ggf_ref[...] += gg
            h3_ref[...] = dx

    blk = lambda w: pl.BlockSpec((tb, w), lambda i: (i, 0))
    in_specs = [blk(d), VM, VM, VM, VM, VM]
    out_specs = [blk(d), blk(2 * f), blk(2 * f)]
    out_shape = [jax.ShapeDtypeStruct((t, d), F32), jax.ShapeDtypeStruct((t, 2 * f), _MXU), jax.ShapeDtypeStruct((t, 2 * f), _MXU)]
    operands = (h, g, w_up, conv_w, conv_b, w_down)
    if head is not None:
        in_specs += [VM, blk(d)]
        out_specs += [_const_spec((1, 1)), _const_spec((1, d))]
        out_shape += [jax.ShapeDtypeStruct((1, 1), F32), jax.ShapeDtypeStruct((1, d), F32)]
        operands += tuple(head)
    return _call_hosting(
        body, hosted, name="ffn_fwd", steps=t // tb, in_specs=in_specs, out_specs=out_specs, out_shape=out_shape,
        scratch_shapes=[pltpu.VMEM((2, 8 + tb, ft), F32), pltpu.VMEM((8, 2 * f), F32), pltpu.VMEM((tb, DOWN_TILES * ft), _MXU)],
        operands=operands)


def _ffn_bwd(dh3, h, hh, hc, g, w_up, conv_w, w_down, hosted=None):
    t, d = h.shape
    f = w_down.shape[0]
    ft = FF_TILE
    tb = _token_block(t, 256)
    nb = t // tb

    def body(dh3_ref, h_ref, hh_ref, hc_ref, g_ref, wup_ref, cw_ref, wdown_ref,
             dh_ref, dhh_ref, act_ref, xn_ref, gcw_ref, gcb_ref, gg_ref, dcarry):
        @pl.when(pl.program_id(0) == 0)
        def _():
            for r in (gcw_ref, gcb_ref, gg_ref, dcarry):
                r[...] = jnp.zeros(r.shape, F32)

        dh3v = dh3_ref[...]
        dhm = dh3v.astype(_MXU)
        dxn = jnp.zeros((tb, d), F32)
        dact_next = _nt(dhm, wdown_ref[0:ft, :])
        for j in range(f // ft):
            dact = dact_next
            if j + 1 < f // ft:
                dact_next = _nt(dhm, wdown_ref[(j + 1) * ft:(j + 2) * ft, :])
            gate = hc_ref[:, j * ft:(j + 1) * ft].astype(F32)
            val = hc_ref[:, f + j * ft:f + (j + 1) * ft].astype(F32)
            sg = jax.nn.sigmoid(gate)
            silu = gate * sg
            act_ref[:, j * ft:(j + 1) * ft] = (silu * val).astype(_MXU)
            dhc = (dact * val * sg * (1.0 + gate * (1.0 - sg)), dact * silu)
            for part, off in enumerate((j * ft, f + j * ft)):
                cols = slice(off, off + ft)
                dc = dhc[part]
                c0 = hh_ref[:, cols].astype(F32)
                after = dcarry[:, cols]
                ahead1 = _shift_rows(dc, -1, after)
                ahead2 = _shift_rows(dc, -2, after)
                dcarry[:, cols] = dc[0:8, :]
                gcb_ref[:, cols] += jnp.sum(dc, axis=0, keepdims=True)
                gcw_ref[0:1, cols] += jnp.sum(ahead2 * c0, axis=0, keepdims=True)
                gcw_ref[1:2, cols] += jnp.sum(ahead1 * c0, axis=0, keepdims=True)
                gcw_ref[2:3, cols] += jnp.sum(dc * c0, axis=0, keepdims=True)
                dhh = (cw_ref[2:3, cols] * dc + cw_ref[1:2, cols] * ahead1 + cw_ref[0:1, cols] * ahead2).astype(_MXU)
                dhh_ref[:, cols] = dhh
                dxn = dxn + _nt(dhh, wup_ref[:, cols])
        xhat, r = _rms(h_ref[...])
        xn_ref[...] = (xhat * g_ref[...]).astype(_MXU)
        dx, gg = _rms_bwd(dxn, xhat, r, g_ref[...])
        gg_ref[...] += gg
        dh_ref[...] = dh3v + dx

    rev = lambda w: pl.BlockSpec((tb, w), lambda i: (nb - 1 - i, 0))
    return _call_hosting(
        body, hosted, name="ffn_bwd", steps=nb,
        in_specs=[rev(d), rev(d), rev(2 * f), rev(2 * f), VM, VM, VM, VM],
        out_specs=[rev(d), rev(2 * f), rev(f), rev(d), _const_spec((3, 2 * f)), _const_spec((1, 2 * f)), _const_spec((1, d))],
        out_shape=[jax.ShapeDtypeStruct((t, d), F32), jax.ShapeDtypeStruct((t, 2 * f), _MXU), jax.ShapeDtypeStruct((t, f), _MXU),
                   jax.ShapeDtypeStruct((t, d), _MXU),
                   jax.ShapeDtypeStruct((3, 2 * f), F32), jax.ShapeDtypeStruct((1, 2 * f), F32), jax.ShapeDtypeStruct((1, d), F32)],
        scratch_shapes=[pltpu.VMEM((8, 2 * f), F32)],
        operands=(dh3, h, hh, hc, g, w_up, conv_w, w_down))


def _largest_tile(n, cap, mult=128):
    best = None
    for c in range(mult, min(n, cap) + 1, mult):
        if n % c == 0:
            best = c
    return best if best is not None else n


def _grad_matmul(a, b, name, layer, n_layers, into=None, hosted=None):
    t, m = a.shape
    n = b.shape[1]
    tm, tn, tk = _largest_tile(m, 1408), _largest_tile(n, 1536), _largest_tile(t, 1024)
    nk = t // tk

    def body(a_ref, b_ref, *rest):
        o_ref = rest[-1]

        @pl.when(pl.program_id(2) == 0)
        def _():
            o_ref[...] = jnp.zeros(o_ref.shape, F32)

        o_ref[...] += _tn(a_ref[...].astype(_MXU), b_ref[...].astype(_MXU))

    in_specs = [pl.BlockSpec((tk, tm), lambda i, j, k: (k, i)), pl.BlockSpec((tk, tn), lambda i, j, k: (k, j))]
    operands = (a, b)
    aliases = {}
    if into is not None:
        in_specs.append(pl.BlockSpec(memory_space=pl.ANY))
        operands = (a, b, into)
        aliases = {2: 0}
    (out,), got = _call_hosting(
        body, hosted, name=name, steps=(m // tm, n // tn, nk), in_specs=in_specs,
        out_specs=[pl.BlockSpec((None, tm, tn), lambda i, j, k: (layer, i, j))],
        out_shape=[jax.ShapeDtypeStruct((n_layers, m, n), F32)], scratch_shapes=[], operands=operands, aliases=aliases)
    return out, got


def _adamw_math(w, g, m, v):
    m = ADAM_B1 * m + (1.0 - ADAM_B1) * g
    v = ADAM_B2 * v + (1.0 - ADAM_B2) * (g * g)
    m_hat = m / (1.0 - ADAM_B1 ** ADAM_STEP)
    v_hat = v / (1.0 - ADAM_B2 ** ADAM_STEP)
    return -ADAM_LR * (m_hat / (jnp.sqrt(v_hat) + ADAM_EPS) + ADAM_WD * w), m, v


def _row_block(rows, cols, max_bytes=1 << 20, mult=16):
    best = None
    for r in range(mult, rows + 1, mult):
        if rows % r == 0 and r * cols * 4 <= max_bytes:
            best = r
    return best if best is not None else rows


def _adamw_big(ws, gs, ms, vs, name, hosted=None):
    n = len(ws)
    shape = ws[0].shape
    cols = shape[-1]
    flat = lambda a: a.reshape(-1, cols)
    rows = flat(ws[0]).shape[0]
    rb = _row_block(rows, cols, (2 << 20) // n)

    def body(*refs):
        for a in range(n):
            w_ref, g_ref, m_ref, v_ref = (refs[s * n + a] for s in range(4))
            go_ref, d_ref, nm_ref, nv_ref = (refs[(4 + s) * n + a] for s in range(4))
            g = g_ref[...]
            go_ref[...] = g
            d_ref[...], nm_ref[...], nv_ref[...] = _adamw_math(w_ref[...], g, m_ref[...], v_ref[...])

    blk = pl.BlockSpec((rb, cols), lambda i: (i, 0))
    outs, got = _call_hosting(
        body, hosted, name=name, steps=rows // rb, in_specs=[blk] * (4 * n), out_specs=[blk] * (4 * n),
        out_shape=[jax.ShapeDtypeStruct((rows, cols), F32)] * (4 * n), scratch_shapes=[],
        operands=[flat(a) for group in (ws, gs, ms, vs) for a in group])
    return [[outs[s * n + a].reshape(shape) for s in range(4)] for a in range(n)], got


def _adamw_small(ws, gs, ms, vs):
    n = len(ws)

    def body(*refs):
        for a in range(n):
            w_ref, g_ref, m_ref, v_ref = (refs[s * n + a] for s in range(4))
            d_ref, nm_ref, nv_ref = (refs[(4 + s) * n + a] for s in range(3))
            d_ref[...], nm_ref[...], nv_ref[...] = _adamw_math(w_ref[...], g_ref[...], m_ref[...], v_ref[...])

    outs = pl.pallas_call(
        body, name="adamw_small", in_specs=[VM] * (4 * n), out_specs=[VM] * (3 * n),
        out_shape=[jax.ShapeDtypeStruct(w.shape, F32) for w in ws] * 3, compiler_params=_params(),
    )(*ws, *gs, *ms, *vs)
    return outs[:n], outs[n:2 * n], outs[2 * n:]


def _place():
    x, y, c = lax.axis_index("x"), lax.axis_index("y"), lax.axis_index("c")
    chips = [(1 - x, y), (x, 1 - y), (1 - x, 1 - y)]
    return x, y, c, chips


def _rows(start, size, mult=16):
    return pl.ds(pl.multiple_of(start, mult), size)


def _full_window(ref, axis, chip, half=None):
    r, c = ref.shape
    if axis == 0:
        rs = r // 4
        if half is None:
            return ref.at[_rows(chip * rs, rs), :]
        return ref.at[_rows(chip * rs + half * (rs // 2), rs // 2), :]
    cs = c // 4
    if half is None:
        return ref.at[:, _rows(chip * cs, cs, 128)]
    return ref.at[_rows(half * (r // 2), r // 2), _rows(chip * cs, cs, 128)]


def _remote(src, dst, send_sem, recv_sem, to):
    return pltpu.make_async_remote_copy(src_ref=src, dst_ref=dst, send_sem=send_sem, recv_sem=recv_sem,
                                        device_id=to, device_id_type=MESH)


def _scalars(*vals):
    return jnp.stack([jnp.asarray(v, jnp.int32) for v in vals])


def _cast_place(shards, axis, chip, name):
    n = len(shards)
    nl, rs, cs = shards[0].shape
    full = (rs * 4, cs) if axis == 0 else (rs, cs * 4)
    rb = _row_block(rs, cs, (4 << 20) // (n * nl))
    nrb = rs // rb

    def body(chip_ref, *refs):
        for a in range(n):
            for l in range(nl):
                refs[n + a * nl + l][...] = refs[a][l].astype(_PAY)

    if axis == 0:
        out_map = lambda i, chip_ref: (chip_ref[0] * nrb + i, 0)
    else:
        out_map = lambda i, chip_ref: (i, chip_ref[0])
    outs = pl.pallas_call(
        body, name=name,
        grid_spec=pltpu.PrefetchScalarGridSpec(
            num_scalar_prefetch=1, grid=(nrb,),
            in_specs=[pl.BlockSpec((nl, rb, cs), lambda i, chip_ref: (0, i, 0))] * n,
            out_specs=[pl.BlockSpec((rb, cs), out_map)] * (n * nl)),
        out_shape=[jax.ShapeDtypeStruct(full, _PAY)] * (n * nl), compiler_params=_params(("parallel",)),
    )(_scalars(chip), *shards)
    return [[outs[a * nl + l] for l in range(nl)] for a in range(n)]


def _hosted_allgather(placed, axes):
    n = len(placed)

    def each(outs, half_of):
        x, y, c, chips = _place()
        for i in range(n):
            for k, chip in enumerate(chips):
                yield i * 3 + k, (*chip, c), (x, y, 1 - c), _full_window(outs[i], axes[i], 2 * x + y, c), \
                    _full_window(outs[i], axes[i], 2 * chip[0] + chip[1], half_of(c))

    def start(_, outs, sems):
        send, recv, _, _ = sems
        for s, peer, _, mine, _ in each(outs, lambda c: c):
            _remote(mine, mine, send.at[s], recv.at[s], peer).start()

    def middle(_, outs, sems):
        send, recv, fsend, frecv = sems
        for s, _, sibling, _, got in each(outs, lambda c: c):
            _remote(got, got, send.at[s], recv.at[s], sibling).wait_recv()
            _remote(got, got, fsend.at[s], frecv.at[s], sibling).start()

    def finish(_, outs, sems):
        send, recv, fsend, frecv = sems
        for s, _, sibling, _, got in each(outs, lambda c: 1 - c):
            _remote(got, got, fsend.at[s], frecv.at[s], sibling).wait_recv()
        for s, peer, sibling, mine, got in each(outs, lambda c: c):
            _remote(mine, mine, send.at[s], recv.at[s], peer).wait_send()
            _remote(got, got, fsend.at[s], frecv.at[s], sibling).wait_send()

    return _Hosted(tuple(placed), True, (), (pltpu.SemaphoreType.DMA((n * 3,)),) * 4, (start, middle, finish))


def _allgather_conv(conv_shard):
    nl, taps, cs = conv_shard.shape

    def body(in_ref, out_ref, send, recv, local):
        x, y, c, chips = _place()
        mine = out_ref.at[:, :, _rows((2 * x + y) * cs, cs, 128)]
        own = pltpu.make_async_copy(in_ref, mine, local)
        own.start()
        sends = [_remote(in_ref, mine, send.at[k], recv.at[k], (*chip, c)) for k, chip in enumerate(chips)]
        for cp in sends:
            cp.start()
        for k, chip in enumerate(chips):
            got = out_ref.at[:, :, _rows((2 * chip[0] + chip[1]) * cs, cs, 128)]
            _remote(got, got, send.at[k], recv.at[k], (*chip, c)).wait_recv()
        for cp in sends:
            cp.wait_send()
        own.wait()

    return pl.pallas_call(
        body, name="allgather_conv", in_specs=[HB], out_specs=HB, out_shape=jax.ShapeDtypeStruct((nl, taps, cs * 4), conv_shard.dtype),
        scratch_shapes=[pltpu.SemaphoreType.DMA((3,)), pltpu.SemaphoreType.DMA((3,)), pltpu.SemaphoreType.DMA],
        compiler_params=pltpu.CompilerParams(has_side_effects=True),
    )(conv_shard)


def _hosted_exchange(grads, axes, layer):
    na = len(grads)
    views = [g.reshape(g.shape[0], 4, 2, g.shape[1] // 8, g.shape[2]) if ax == 0 else g for g, ax in zip(grads, axes)]

    def region(ref, axis, half):
        if axis == 0:
            return ref.at[layer, :, half]
        r = ref.shape[1]
        return ref.at[layer, _rows(half * (r // 2), r // 2), :]

    def copies(ins, land, sems):
        send, recv = sems
        x, y, c, _ = _place()
        return [_remote(region(ins[a], axes[a], 1 - c), land[a], send.at[a], recv.at[a], (x, y, 1 - c)) for a in range(na)]

    def start(ins, land, sems):
        for cp in copies(ins, land, sems):
            cp.start()

    def finish(ins, land, sems):
        for cp in copies(ins, land, sems):
            cp.wait()

    shapes = [(4, g.shape[1] // 8, g.shape[2]) if ax == 0 else (g.shape[1] // 2, g.shape[2]) for g, ax in zip(grads, axes)]
    return _Hosted(tuple(views), False, tuple(jax.ShapeDtypeStruct(s, F32) for s in shapes),
                   (pltpu.SemaphoreType.DMA((na,)),) * 2, (start, None, finish))


def _add_cast(mines, theirs, core, base, name):
    n = len(mines)
    na, nb, cols = theirs[0].shape
    rb = _row_block(nb, cols, (4 << 20) // n)

    def body(core_ref, *refs):
        for a in range(n):
            refs[2 * n + a][...] = (refs[a][...] + refs[n + a][...]).astype(_PAY)

    blk = pl.BlockSpec((None, rb, cols), lambda i, k, core_ref: (i, k, 0))
    return pl.pallas_call(
        body, name=name,
        grid_spec=pltpu.PrefetchScalarGridSpec(
            num_scalar_prefetch=1, grid=(na, nb // rb),
            in_specs=[pl.BlockSpec((None, None, rb, cols), lambda i, k, core_ref: (base + i, core_ref[0], k, 0))] * n + [blk] * n,
            out_specs=[blk] * n),
        out_shape=[jax.ShapeDtypeStruct((na, nb, cols), _PAY)] * n, compiler_params=_params(("parallel", "parallel")),
    )(_scalars(core), *mines, *theirs)


def _piece(ref, axis, chip):
    if axis == 0:
        return ref.at[chip]
    cs = ref.shape[1] // 4
    return ref.at[:, _rows(chip * cs, cs, 128)]


def _hosted_scatter(sums, axes):
    na = len(sums)

    def piece_shape(a):
        if axes[a] == 0:
            return (sums[a].shape[1], sums[a].shape[2])
        return (sums[a].shape[0], sums[a].shape[1] // 4)

    def copies(ins, slots, sems):
        send, recv = sems
        _, _, c, chips = _place()
        return [_remote(_piece(ins[a], axes[a], 2 * chip[0] + chip[1]), slots[a].at[k], send.at[a * 3 + k], recv.at[a * 3 + k], (*chip, c))
                for a in range(na) for k, chip in enumerate(chips)]

    def start(ins, slots, sems):
        for cp in copies(ins, slots, sems):
            cp.start()

    def finish(ins, slots, sems):
        for cp in copies(ins, slots, sems):
            cp.wait()

    return _Hosted(tuple(sums), False, tuple(jax.ShapeDtypeStruct((3,) + piece_shape(a), sums[a].dtype) for a in range(na)),
                   (pltpu.SemaphoreType.DMA((na * 3,)),) * 2, (start, None, finish))


def _sum_slots(sums, slots, axis, chip, core, layer, n_layers, name, into=None):
    _, hr, cs = slots.shape
    rb = _row_block(hr, cs, 4 << 20)

    def body(at_ref, own_ref, s_ref, *rest):
        rest[-1][...] = ((own_ref[...].astype(F32) + s_ref[0].astype(F32)) + s_ref[1].astype(F32)) + s_ref[2].astype(F32)

    if axis == 0:
        own = pl.BlockSpec((None, rb, cs), lambda k, at_ref: (at_ref[0], k, 0))
    else:
        own = pl.BlockSpec((rb, cs), lambda k, at_ref: (k, at_ref[0]))
    in_specs = [own, pl.BlockSpec((3, rb, cs), lambda k, at_ref: (0, k, 0))]
    operands = (sums, slots)
    aliases = {}
    if into is not None:
        in_specs.append(pl.BlockSpec(memory_space=pl.ANY))
        operands = (sums, slots, into)
        aliases = {3: 0}
    return pl.pallas_call(
        body, name=name,
        grid_spec=pltpu.PrefetchScalarGridSpec(
            num_scalar_prefetch=1, grid=(hr // rb,), in_specs=in_specs,
            out_specs=pl.BlockSpec((None, None, rb, cs), lambda k, at_ref: (layer, at_ref[1], k, 0))),
        out_shape=jax.ShapeDtypeStruct((n_layers, 2, hr, cs), F32), input_output_aliases=aliases,
        compiler_params=_params(("parallel",)),
    )(_scalars(chip, core), *operands)


def _hosted_assemble(shards):
    na = len(shards)

    def copies(_, outs, sems):
        send, recv = sems
        x, y, c, _ = _place()
        halves = [outs[a].at[:, _rows(c * (outs[a].shape[1] // 2), outs[a].shape[1] // 2), :] for a in range(na)]
        return [_remote(mine, mine, send.at[a], recv.at[a], (x, y, 1 - c)) for a, mine in enumerate(halves)]

    def start(ins, outs, sems):
        for cp in copies(ins, outs, sems):
            cp.start()

    def finish(ins, outs, sems):
        for cp in copies(ins, outs, sems):
            cp.wait()

    return _Hosted(tuple(shards), True, (), (pltpu.SemaphoreType.DMA((na,)),) * 2, (start, None, finish))


def _allreduce_small(buf, hosted):
    rows, w = buf.shape
    half = rows // 2
    nh, h_shapes = len(hosted.operands), _hosted_results(hosted)

    def body(buf_ref, *refs):
        h_in, out_ref, h_out = refs[:nh], refs[nh], refs[nh + 1:nh + 1 + len(h_shapes)]
        land, slots, red, sems_send, sems_recv = refs[nh + 1 + len(h_shapes):nh + 6 + len(h_shapes)]
        h_sems = refs[nh + 6 + len(h_shapes):]
        hosted.stages[0](h_in, h_out, h_sems)
        x, y, c, chips = _place()
        me = 2 * x + y
        sibling = (x, y, 1 - c)
        first = _remote(buf_ref, land, sems_send.at[0], sems_recv.at[0], sibling)
        first.start()
        first.wait()
        mine = pl.ds(pl.multiple_of(c * half, 8), half)
        slots[me] = buf_ref[mine, :] + land[mine, :]
        sends = []
        for k, chip in enumerate(chips):
            cp = _remote(slots.at[me], slots.at[me], sems_send.at[1 + k], sems_recv.at[1 + k], (*chip, c))
            cp.start()
            sends.append(cp)
        for k, chip in enumerate(chips):
            got = slots.at[2 * chip[0] + chip[1]]
            _remote(got, got, sems_send.at[1 + k], sems_recv.at[1 + k], sibling).wait_recv()
        red[...] = ((slots[0] + slots[1]) + slots[2]) + slots[3]
        out_ref[mine, :] = red[...]
        last = _remote(red, out_ref.at[mine, :], sems_send.at[4], sems_recv.at[4], sibling)
        last.start()
        theirs = out_ref.at[pl.ds(pl.multiple_of((1 - c) * half, 8), half), :]
        _remote(red, theirs, sems_send.at[4], sems_recv.at[4], sibling).wait_recv()
        for cp in sends:
            cp.wait_send()
        last.wait_send()
        hosted.stages[2](h_in, h_out, h_sems)

    outs = pl.pallas_call(
        body, name="allreduce_small", in_specs=[VM] + [HB] * nh, out_specs=[VM] + [HB] * len(h_shapes),
        out_shape=[jax.ShapeDtypeStruct((rows, w), F32)] + h_shapes,
        input_output_aliases={1 + i: 1 + i for i in range(nh)} if hosted.aliased else {},
        scratch_shapes=[pltpu.VMEM((rows, w), F32), pltpu.VMEM((4, half, w), F32), pltpu.VMEM((half, w), F32),
                        pltpu.SemaphoreType.DMA((5,)), pltpu.SemaphoreType.DMA((5,))] + list(hosted.sems),
        compiler_params=pltpu.CompilerParams(has_side_effects=True, vmem_limit_bytes=VMEM_LIMIT),
    )(buf, *hosted.operands)
    return outs[0], outs[1:]


BIG = ("w_in", "w_out", "wq", "wk", "wv", "wo", "w_up", "w_down")
MIXER, ATTN, MLP = ("w_in", "w_out"), ("wq", "wk", "wv", "wo"), ("w_up", "w_down")
BIG_AXIS = {"w_in": 1, "w_out": 0, "wq": 0, "wk": 0, "wv": 0, "wo": 0, "w_up": 1, "w_down": 0}
SMALL = ("norm_mix_g", "pool_w", "pool_scale", "sgu_g", "sgu_w", "sgu_b", "norm_xattn_g", "mem_norm_g", "norm_ffn_g",
         "conv_w", "conv_b", "final_norm_g")
ORDER = ("norm_mix_g", "w_in", "pool_w", "pool_scale", "sgu_g", "sgu_w", "sgu_b", "w_out", "norm_xattn_g", "mem_norm_g",
         "wq", "wk", "wv", "wo", "norm_ffn_g", "w_up", "conv_w", "conv_b", "w_down", "final_norm_g")
PACK_WIDTH = 512


def kernel(x, mem, norm_mix_g, w_in, pool_w, pool_scale, sgu_g, sgu_w, sgu_b, w_out, norm_xattn_g, mem_norm_g, wq, wk, wv, wo, norm_ffn_g, w_up, conv_w, conv_b, w_down, final_norm_g, loss_target, m_norm_mix_g, m_w_in, m_pool_w, m_pool_scale, m_sgu_g, m_sgu_w, m_sgu_b, m_w_out, m_norm_xattn_g, m_mem_norm_g, m_wq, m_wk, m_wv, m_wo, m_norm_ffn_g, m_w_up, m_conv_w, m_conv_b, m_w_down, m_final_norm_g, v_norm_mix_g, v_w_in, v_pool_w, v_pool_scale, v_sgu_g, v_sgu_w, v_sgu_b, v_w_out, v_norm_xattn_g, v_mem_norm_g, v_wq, v_wk, v_wv, v_wo, v_norm_ffn_g, v_w_up, v_conv_w, v_conv_b, v_w_down, v_final_norm_g):
    given = dict(locals())
    w = {n: given[n] for n in ORDER}
    mom = {n: given["m_" + n] for n in ORDER}
    var = {n: given["v_" + n] for n in ORDER}
    nl = w_in.shape[0]
    xs, mems, tgt = x[0], mem[0], loss_target[0]
    chip = 2 * lax.axis_index("x") + lax.axis_index("y")
    core = lax.axis_index("c")

    axes_of = lambda names: [BIG_AXIS[n] for n in names]
    alike = {}
    for n in BIG:
        alike.setdefault((w[n].shape, BIG_AXIS[n]), []).append(n)
    placed = [{} for _ in range(nl)]
    for (_, axis), names in alike.items():
        for n, per_layer in zip(names, _cast_place([w[n] for n in names], axis, chip, "place_" + names[0])):
            for l in range(nl):
                placed[l][n] = per_layer[l]
    conv_full = _allgather_conv(conv_w)

    def gather(names, l):
        return _hosted_allgather([placed[l][n] for n in names], axes_of(names))

    full = [dict(zip(MIXER, _run_hosted(gather(MIXER, 0), "allgather_weights")))]

    row = lambda a, l: a[l][None, :]
    saved = []
    h = xs
    for l in range(nl):
        fw = full[l]
        sbt = jnp.broadcast_to(sgu_b[l][:, :, None], sgu_w[l].shape)
        (h1, proj, xn1, mix), got = _mixer_fwd(h, row(norm_mix_g, l), fw["w_in"], pool_w[l], row(pool_scale, l), row(sgu_g, l), sgu_w[l], sbt, fw["w_out"],
                                               [gather(ATTN, 0), gather(("w_down",), 0)] if l == 0 else None)
        if l == 0:
            fw.update(zip(ATTN, got[0]))
            fw["w_down"] = got[1][0]
        k, v, memn = _kv_fwd(mems, row(mem_norm_g, l), fw["wk"], fw["wv"])
        (h2, q, o, xn2), got = _xattn_fwd(h1, row(norm_xattn_g, l), fw["wq"], k, v, fw["wo"], [gather(("w_up",), 0)] if l == 0 else None)
        if l == 0:
            fw["w_up"] = got[0][0]
        outs, got = _ffn_fwd(h2, row(norm_ffn_g, l), fw["w_up"], conv_full[l], row(conv_b, l), fw["w_down"],
                             [gather(BIG, l + 1)] if l + 1 < nl else None, None if l + 1 < nl else (final_norm_g[None, :], tgt))
        h3, hh, hc = outs[:3]
        if l + 1 < nl:
            full.append(dict(zip(BIG, got[0])))
        else:
            dh, loss_part, g_final = h3, outs[3], outs[4]
        saved.append(dict(h=h, h1=h1, h2=h2, proj=proj, xn1=xn1, mix=mix, k=k, v=v, memn=memn, q=q, o=o, xn2=xn2, hh=hh, hc=hc, sbt=sbt))
        h = h3


    big_grads = {}
    small_grads = [None] * nl

    def weight_grad(n, a, b, l, hosted=None):
        big_grads[n], got = _grad_matmul(a, b, "grad_" + n, l, nl, big_grads.get(n), hosted)
        return got

    sums, slots = {}, {}

    def exchange(names, l):
        return _hosted_exchange([big_grads[n] for n in names], axes_of(names), l)

    def scatter(names, l):
        return _hosted_scatter([sums[n, l] for n in names], axes_of(names))

    def add_casts(names, theirs, l):
        theirs = dict(zip(names, theirs))
        for group in alike.values():
            group = [n for n in group if n in theirs]
            if not group:
                continue
            gl, gr, gc = big_grads[group[0]].shape
            if BIG_AXIS[group[0]] == 0:
                outs = _add_cast([big_grads[n].reshape(gl * 4, 2, gr // 8, gc) for n in group], [theirs[n] for n in group],
                                 core, l * 4, "grad_chip_sum_" + group[0])
            else:
                outs = [o[0] for o in _add_cast([big_grads[n].reshape(gl, 2, gr // 2, gc) for n in group], [theirs[n][None] for n in group],
                                                core, l, "grad_chip_sum_" + group[0])]
            for n, o in zip(group, outs):
                sums[n, l] = o

    def keep_slots(names, got, l):
        for n, sl in zip(names, got):
            slots[n, l] = sl

    shard_grads = {}

    def shard_halves(names):
        out = []
        for n in names:
            buf = None
            for l in range(nl):
                buf = _sum_slots(sums[n, l], slots[n, l], BIG_AXIS[n], chip, core, l, nl, "grad_sum_" + n, buf)
            out.append(buf.reshape(nl, 2 * buf.shape[2], buf.shape[3]))
        return out

    for l in reversed(range(nl)):
        fw, s = full[l], saved[l]
        above = l + 1 < nl
        dh3 = dh
        (dh2, dhh, act, xn3, g_cw, g_cb, g_nf), got = _ffn_bwd(dh3, s["h2"], s["hh"], s["hc"], row(norm_ffn_g, l), fw["w_up"], conv_full[l], fw["w_down"],
                                                         [exchange(MIXER, l + 1), scatter(ATTN, l + 1)] if above else None)
        if above:
            add_casts(MIXER, got[0], l + 1)
            keep_slots(ATTN, got[1], l + 1)
        weight_grad("w_up", xn3, dhh, l)
        weight_grad("w_down", act, dh3, l)
        (dh1, dq, dk, dv, g_nx), got = _xattn_bwd(dh2, s["h1"], s["q"], row(norm_xattn_g, l), fw["wq"], s["k"], s["v"], fw["wo"],
                                                  [exchange(MLP, l), scatter(MIXER, l + 1) if above else None])
        add_casts(MLP, got[0], l)
        if above:
            keep_slots(MIXER, got[1], l + 1)
        weight_grad("wq", s["xn2"], dq, l)
        weight_grad("wo", s["o"], dh2, l)
        weight_grad("wk", s["memn"], dk, l)
        weight_grad("wv", s["memn"], dv, l)
        g_mn = _kv_bwd(dk, dv, mems, fw["wk"], fw["wv"])
        (dh0, dproj, g_nm, g_pw, g_ps, g_sg, g_sw, g_sbt), got = _mixer_bwd(dh1, s["h"], s["proj"], row(norm_mix_g, l), fw["w_in"], pool_w[l], row(pool_scale, l), row(sgu_g, l), sgu_w[l], s["sbt"], fw["w_out"],
                                                                           [scatter(MLP, l), exchange(ATTN, l)])
        keep_slots(MLP, got[0], l)
        add_casts(ATTN, got[1], l)
        half = len(ATTN) // 2
        got = weight_grad("w_in", s["xn1"], dproj, l, [scatter(ATTN[:half], l), _hosted_assemble(shard_halves(MLP))] if l == 0 else None)
        if l == 0:
            keep_slots(ATTN[:half], got[0], l)
            shard_grads.update(zip(MLP, got[1]))
        got = weight_grad("w_out", s["mix"], dh1, l, [scatter(ATTN[half:], l)] if l == 0 else None)
        if l == 0:
            keep_slots(ATTN[half:], got[0], l)
        small_grads[l] = dict(norm_mix_g=g_nm, pool_w=g_pw, pool_scale=g_ps, sgu_g=g_sg, sgu_w=g_sw, sgu_b=jnp.sum(g_sbt, axis=-1),
                              norm_xattn_g=g_nx, mem_norm_g=g_mn, norm_ffn_g=g_nf, conv_w=g_cw, conv_b=g_cb)
        dh = dh0
    grad_x = dh[None]

    delta, new_m, new_v, grads = {}, {}, {}, {}

    def adamw(names, hosted=None):
        outs, got = _adamw_big([w[n] for n in names], [shard_grads[n] for n in names], [mom[n] for n in names], [var[n] for n in names],
                               "adamw_" + names[0], hosted)
        for n, out in zip(names, outs):
            grads[n], delta[n], new_m[n], new_v[n] = out
        return got

    add_casts(MIXER, adamw(("w_up",), [exchange(MIXER, 0)])[0], 0)
    keep_slots(MIXER, adamw(("w_down",), [scatter(MIXER, 0)])[0], 0)

    layered = [n for n in SMALL if n != "final_norm_g"]
    parts = [small_grads[l][n].reshape(-1, PACK_WIDTH) for n in layered for l in range(nl)]
    parts.append(g_final.reshape(-1, PACK_WIDTH))
    parts.append(jnp.pad(loss_part, ((0, 0), (0, PACK_WIDTH - 1))))
    used = sum(p.shape[0] for p in parts)
    total = -(-used // 16) * 16
    packed, got = _allreduce_small(jnp.concatenate(parts + [jnp.zeros((total - used, PACK_WIDTH), F32)], axis=0),
                                   _hosted_assemble(shard_halves(ATTN)))
    shard_grads.update(zip(ATTN, got))
    shard_grads.update(zip(MIXER, _run_hosted(_hosted_assemble(shard_halves(MIXER)), "grad_sibling_assemble")))
    for names in alike.values():
        if names[0] not in MLP:
            adamw(names)
    at = 0
    for n in layered:
        per_layer = []
        for l in range(nl):
            shape = small_grads[l][n].shape
            nrow = small_grads[l][n].size // PACK_WIDTH
            per_layer.append(packed[at:at + nrow].reshape(shape))
            at += nrow
        g = jnp.stack(per_layer)
        if n == "conv_w":
            cs = conv_w.shape[2]
            g = lax.dynamic_slice_in_dim(g, chip * cs, cs, axis=2)
        grads[n] = g.reshape(w[n].shape)
    grads["final_norm_g"] = packed[at:at + g_final.size // PACK_WIDTH].reshape(final_norm_g.shape)
    at += g_final.size // PACK_WIDTH
    loss = packed[at, 0]

    two_d = lambda a: a.reshape(-1, a.shape[-1])
    ds, nms, nvs = _adamw_small([two_d(w[n]) for n in SMALL], [two_d(grads[n]) for n in SMALL],
                                [two_d(mom[n]) for n in SMALL], [two_d(var[n]) for n in SMALL])
    for n, d_, m_, v_ in zip(SMALL, ds, nms, nvs):
        delta[n], new_m[n], new_v[n] = d_.reshape(w[n].shape), m_.reshape(w[n].shape), v_.reshape(w[n].shape)

    return (loss, grad_x, *[grads[n] for n in ORDER], *[delta[n] for n in ORDER], *[new_m[n] for n in ORDER], *[new_v[n] for n in ORDER])
```

```python
import math
from typing import NamedTuple

import jax
import jax.numpy as jnp
from jax import lax
from jax.experimental import pallas as pl
from jax.experimental.pallas import tpu as pltpu

F32 = jnp.float32
_MXU = jnp.bfloat16
_PAY = jnp.bfloat16
EPS = 1e-6
WINDOWS = (2, 4, 8, 16)
GROUP = 128
N_XHEADS = 4
HALO = 16
FF_TILE = 256
DOWN_TILES = 6
VMEM_LIMIT = 60 * 1024 * 1024
MESH = pl.DeviceIdType.MESH

ADAM_LR, ADAM_B1, ADAM_B2, ADAM_EPS, ADAM_WD, ADAM_STEP = 0.001, 0.9, 0.999, 1e-08, 0.01, 10

VM = pl.BlockSpec(memory_space=pltpu.VMEM)
HB = pl.BlockSpec(memory_space=pltpu.HBM)


def _nn(a, b):
    return jnp.dot(a, b, preferred_element_type=F32)


def _nt(a, b):
    return lax.dot_general(a, b, (((1,), (1,)), ((), ())), preferred_element_type=F32)


def _tn(a, b):
    return lax.dot_general(a, b, (((0,), (0,)), ((), ())), preferred_element_type=F32)


def _rms(x):
    r = lax.rsqrt(jnp.mean(x * x, axis=-1, keepdims=True) + EPS)
    return x * r, r


def _rms_bwd(dxn, xhat, r, g):
    dxh = dxn * g
    dx = r * (dxh - xhat * jnp.mean(dxh * xhat, axis=-1, keepdims=True))
    return dx, jnp.sum(dxn * xhat, axis=0, keepdims=True)


def _gelu(x):
    cdf = 0.5 * (1.0 + lax.erf(x * (2.0 ** -0.5)))
    return x * cdf, cdf


def _gelu_grad(x, cdf):
    return cdf + x * jnp.exp(-0.5 * x * x) * ((2.0 * math.pi) ** -0.5)


def _params(sem=None):
    return pltpu.CompilerParams(dimension_semantics=sem, vmem_limit_bytes=VMEM_LIMIT)


def _token_block(t, want):
    return want if t % want == 0 and t > want else GROUP


def _const_spec(shape):
    n = len(shape)
    return pl.BlockSpec(shape, lambda i: (0,) * n)


def _tril():
    return lax.broadcasted_iota(jnp.int32, (GROUP, GROUP), 0) >= lax.broadcasted_iota(jnp.int32, (GROUP, GROUP), 1)


def _shift_rows(x, k, edge):
    tb = x.shape[0]
    r8 = lax.broadcasted_iota(jnp.int32, (8, 1), 0)
    rolled = pltpu.roll(x, k % tb, 0)
    if k > 0:
        top = jnp.where(r8 < k, pltpu.roll(edge, k, 0), rolled[0:8, :])
        return jnp.concatenate([top, rolled[8:, :]], axis=0)
    bottom = jnp.where(r8 >= 8 + k, pltpu.roll(edge, 8 + k, 0), rolled[tb - 8:, :])
    return jnp.concatenate([rolled[:tb - 8, :], bottom], axis=0)


def _in_turns(parts):
    parts = list(parts)
    while parts:
        for p in list(parts):
            try:
                next(p)
            except StopIteration:
                parts.remove(p)


class _Hosted(NamedTuple):
    operands: tuple
    aliased: bool
    out_shapes: tuple
    sems: tuple
    stages: tuple


def _hosted_results(hosted):
    if hosted.aliased:
        return [jax.ShapeDtypeStruct(o.shape, o.dtype) for o in hosted.operands]
    return list(hosted.out_shapes)


def _call_hosting(main_body, hosted, *, name, steps, in_specs, out_specs, out_shape, scratch_shapes, operands, aliases=None):
    grid = steps if isinstance(steps, tuple) else (steps,)
    semantics = ("arbitrary",) * len(grid)
    hosted = [hs for hs in (hosted or ()) if hs is not None]
    if not hosted:
        outs = pl.pallas_call(main_body, name=name, grid=grid, in_specs=in_specs, out_specs=out_specs, out_shape=out_shape,
                              scratch_shapes=scratch_shapes, input_output_aliases=aliases or {}, compiler_params=_params(semantics))(*operands)
        return outs, ()
    n_in, n_out, n_sc = len(in_specs), len(out_specs), len(scratch_shapes)
    shapes = [_hosted_results(hs) for hs in hosted]
    aliases, in_at, out_at = dict(aliases or {}), n_in, n_out
    for hs, sh in zip(hosted, shapes):
        if hs.aliased:
            aliases.update({in_at + i: out_at + i for i in range(len(hs.operands))})
        in_at += len(hs.operands)
        out_at += len(sh)

    def body(*refs):
        at = [0]

        def take(n):
            at[0] += n
            return refs[at[0] - n:at[0]]

        ins = take(n_in)
        h_in = [take(len(hs.operands)) for hs in hosted]
        outs = take(n_out)
        h_out = [take(len(sh)) for sh in shapes]
        scratch = take(n_sc)
        h_sems = [take(len(hs.sems)) for hs in hosted]
        ids = [pl.program_id(a) for a in range(len(grid))]

        def at_step(where):
            lead, rest = where
            ok = ids[0] == lead
            for a in range(1, len(grid)):
                ok = jnp.logical_and(ok, ids[a] == (grid[a] - 1 if rest else 0))
            return ok

        def run(stage):
            for hs, a, b, c in zip(hosted, h_in, h_out, h_sems):
                if hs.stages[stage] is not None:
                    hs.stages[stage](a, b, c)

        @pl.when(at_step((0, 0)))
        def _():
            run(0)

        if any(hs.stages[1] is not None for hs in hosted):
            @pl.when(at_step(((3 * grid[0]) // 4, 0)))
            def _():
                run(1)

        main_body(*ins, *outs, *scratch)

        @pl.when(at_step((grid[0] - 1, -1)))
        def _():
            run(2)

    flat = lambda lists: [x for xs in lists for x in xs]
    outs = pl.pallas_call(
        body, name=name, grid=grid, in_specs=list(in_specs) + [HB] * (in_at - n_in), out_specs=list(out_specs) + [HB] * (out_at - n_out),
        out_shape=list(out_shape) + flat(shapes), scratch_shapes=list(scratch_shapes) + flat(hs.sems for hs in hosted),
        input_output_aliases=aliases, compiler_params=_params(semantics),
    )(*operands, *flat(hs.operands for hs in hosted))
    results, at = [], n_out
    for sh in shapes:
        results.append(outs[at:at + len(sh)])
        at += len(sh)
    return outs[:n_out], results


def _run_hosted(hosted, name):
    nh = len(hosted.operands)
    h_shapes = _hosted_results(hosted)

    def body(*refs):
        h_in, h_out, h_sems = refs[:nh], refs[nh:nh + len(h_shapes)], refs[nh + len(h_shapes):]
        for stage in hosted.stages:
            if stage is not None:
                stage(h_in, h_out, h_sems)

    return pl.pallas_call(
        body, name=name, in_specs=[HB] * nh, out_specs=[HB] * len(h_shapes), out_shape=h_shapes, scratch_shapes=list(hosted.sems),
        input_output_aliases={i: i for i in range(nh)} if hosted.aliased else {},
        compiler_params=pltpu.CompilerParams(has_side_effects=True),
    )(*hosted.operands)


def _window_sums(e, win, back):
    n = e.shape[0]
    k = 1
    while k < win:
        e = e + pltpu.roll(e, k if back else n - k, 0)
        k *= 2
    return e


def _pool_diff(prev, p, t0, gi, win):
    sl = slice(gi * GROUP, (gi + 1) * GROUP)
    tb = p.shape[0]
    s = _window_sums(jnp.concatenate([prev[:, sl], p[:, sl]], axis=0), win, True)[HALO:, :]
    tglob = t0 + lax.broadcasted_iota(jnp.int32, (tb, 1), 0)
    cnt = jnp.minimum(tglob + 1, win).astype(F32)
    return s / cnt - p[:, sl], cnt


def _layernorm(v):
    xc = v - jnp.mean(v, axis=-1, keepdims=True)
    rstd = lax.rsqrt(jnp.mean(xc * xc, axis=-1, keepdims=True) + EPS)
    return xc * rstd, rstd


def _mixer_fwd(h, g, w_in, pool_w, pool_scale, sgu_g, sgu_w, sgu_bt, w_out, hosted=None):
    t, d = h.shape
    pw = pool_w.shape[0] * GROUP
    sw = sgu_w.shape[0] * GROUP
    tb = _token_block(t, 512)

    def body(h_ref, g_ref, win_ref, pw_ref, ps_ref, sg_ref, sw_ref, sbt_ref, wout_ref, h1_ref, proj_ref, xn_ref, mix_ref, pext):
        i = pl.program_id(0)

        @pl.when(i == 0)
        def _():
            pext[...] = jnp.zeros((HALO, pw), F32)

        x = h_ref[...]
        xhat, _ = _rms(x)
        xn = (xhat * g_ref[...]).astype(_MXU)
        xn_ref[...] = xn
        proj = _nn(xn, win_ref[...])
        proj_ref[...] = proj
        p = proj[:, :pw]
        prev = pext[...]
        for gi, win in enumerate(WINDOWS):
            sl = slice(gi * GROUP, (gi + 1) * GROUP)
            dg, _ = _pool_diff(prev, p, i * tb, gi, win)
            e = _nn(dg.astype(_MXU), pw_ref[gi].astype(_MXU))
            mix_ref[:, sl] = (e * ps_ref[:, sl]).astype(_MXU)
        pext[...] = p[tb - HALO:tb, :]
        uv, _ = _gelu(proj[:, pw:])
        u = uv[:, :sw]
        vhat, _ = _layernorm(uv[:, sw:])
        vn = (vhat * sg_ref[...]).astype(_MXU)
        mask = _tril()
        chunks = [slice(n * GROUP, (n + 1) * GROUP) for n in range(tb // GROUP)]
        for hh in range(sw // GROUP):
            wm = jnp.where(mask, sw_ref[hh], 0.0).astype(_MXU)
            cols = slice(hh * GROUP, (hh + 1) * GROUP)
            z = _nn(wm, jnp.concatenate([vn[rows, cols] for rows in chunks], axis=1))
            for n, rows in enumerate(chunks):
                mix_ref[rows, pw + hh * GROUP:pw + (hh + 1) * GROUP] = (u[rows, cols] * (z[:, chunks[n]] + sbt_ref[hh])).astype(_MXU)
        h1_ref[...] = x + _nn(mix_ref[...], wout_ref[...])

    blk = lambda w: pl.BlockSpec((tb, w), lambda i: (i, 0))
    return _call_hosting(
        body, hosted, name="mixer_fwd", steps=t // tb,
        in_specs=[blk(d), VM, VM, VM, VM, VM, VM, VM, VM],
        out_specs=[blk(d), blk(w_in.shape[1]), blk(d), blk(d)],
        out_shape=[jax.ShapeDtypeStruct((t, d), F32), jax.ShapeDtypeStruct((t, w_in.shape[1]), F32),
                   jax.ShapeDtypeStruct((t, d), _MXU), jax.ShapeDtypeStruct((t, d), _MXU)],
        scratch_shapes=[pltpu.VMEM((HALO, pw), F32)],
        operands=(h, g, w_in, pool_w, pool_scale, sgu_g, sgu_w, sgu_bt, w_out))


def _mixer_bwd(dh1, h, proj, g, w_in, pool_w, pool_scale, sgu_g, sgu_w, sgu_bt, w_out, hosted=None):
    t, d = h.shape
    ng, nh = pool_w.shape[0], sgu_w.shape[0]
    pw, sw = ng * GROUP, nh * GROUP
    tb = _token_block(t, 512)
    nb = t // tb
    n_parts = 2 if tb % (2 * GROUP) == 0 else 1
    pt = tb // n_parts

    def body(dh1_ref, h_ref, proj_ref, halo_ref, g_ref, win_ref, pw_ref, ps_ref, sg_ref, sw_ref, sbt_ref, wout_ref,
             dh_ref, dproj_ref, gg_ref, gpw_ref, gps_ref, gsg_ref, gsw_ref, gsbt_ref, dext, duv):
        i = pl.program_id(0)
        blk = nb - 1 - i

        @pl.when(i == 0)
        def _():
            for r in (gg_ref, gpw_ref, gps_ref, gsg_ref, gsw_ref, gsbt_ref, dext):
                r[...] = jnp.zeros(r.shape, F32)

        mask = _tril()

        def part(at):
            rows = slice(at, at + pt)
            dh1v = dh1_ref[rows, :]
            dmix = _nt(dh1v.astype(_MXU), wout_ref[...])
            yield
            proj_v = proj_ref[rows, :]
            p = proj_v[:, :pw]
            prev = jnp.where(blk == 0, 0.0, halo_ref[...]) if at == 0 else proj_ref[at - HALO:at, 0:pw]
            for gi, win in enumerate(WINDOWS):
                sl = slice(gi * GROUP, (gi + 1) * GROUP)
                dg, cnt = _pool_diff(prev, p, blk * tb + at, gi, win)
                dgm = dg.astype(_MXU)
                pwm = pw_ref[gi].astype(_MXU)
                e = _nn(dgm, pwm)
                dy = dmix[:, sl]
                gps_ref[:, sl] += jnp.sum(dy * e, axis=0, keepdims=True)
                de = (dy * ps_ref[:, sl]).astype(_MXU)
                gpw_ref[gi] += _tn(dgm, de)
                dd = _nt(de, pwm)
                ddc = dd / cnt
                acc = _window_sums(jnp.concatenate([ddc, dext[:, sl]], axis=0), win, False)[:pt, :]
                dext[:, sl] = ddc[0:HALO, :]
                dproj_ref[rows, sl] = (acc - dd).astype(_MXU)
            yield
            pre = proj_v[:, pw:]
            uv, cdf = _gelu(pre)
            u = uv[:, :sw]
            vhat, rstd = _layernorm(uv[:, sw:])
            vn = (vhat * sg_ref[...]).astype(_MXU)
            chunks = [slice(n * GROUP, (n + 1) * GROUP) for n in range(pt // GROUP)]
            side_by_side = lambda a, cols: jnp.concatenate([a[c, cols] for c in chunks], axis=1)
            for hh in range(nh):
                wm = jnp.where(mask, sw_ref[hh], 0.0).astype(_MXU)
                cols = slice(hh * GROUP, (hh + 1) * GROUP)
                vs = side_by_side(vn, cols)
                z = _nn(wm, vs)
                dy = side_by_side(dmix, slice(pw + hh * GROUP, pw + (hh + 1) * GROUP))
                dz = dy * side_by_side(u, cols)
                dzm = dz.astype(_MXU)
                dvs = _tn(wm, dzm)
                gsw_ref[hh] += jnp.where(mask, _nt(dzm, vs), 0.0)
                gb = jnp.zeros((GROUP, GROUP), F32)
                for n, c in enumerate(chunks):
                    gb = gb + dz[:, c]
                    duv[at + n * GROUP:at + (n + 1) * GROUP, cols] = dy[:, c] * (z[:, c] + sbt_ref[hh])
                    duv[at + n * GROUP:at + (n + 1) * GROUP, sw + hh * GROUP:sw + (hh + 1) * GROUP] = dvs[:, c]
                gsbt_ref[hh] += gb
            yield
            dvn = duv[rows, sw:]
            gsg_ref[...] += jnp.sum(dvn * vhat, axis=0, keepdims=True)
            dxh = dvn * sg_ref[...]
            dv = rstd * (dxh - jnp.mean(dxh, axis=-1, keepdims=True) - vhat * jnp.mean(dxh * vhat, axis=-1, keepdims=True))
            gp = _gelu_grad(pre, cdf)
            dproj_ref[rows, pw:pw + sw] = (duv[rows, :sw] * gp[:, :sw]).astype(_MXU)
            dproj_ref[rows, pw + sw:] = (dv * gp[:, sw:]).astype(_MXU)
            dxn = _nt(dproj_ref[rows, :], win_ref[...])
            yield
            xhat, r = _rms(h_ref[rows, :])
            dx, gg = _rms_bwd(dxn, xhat, r, g_ref[...])
            gg_ref[...] += gg
            dh_ref[rows, :] = dh1v + dx

        _in_turns([part(at) for at in reversed(range(0, tb, pt))])

    rev = lambda w: pl.BlockSpec((tb, w), lambda i: (nb - 1 - i, 0))
    halo = pl.BlockSpec((HALO, pw), lambda i: (jnp.maximum((nb - 1 - i) * (tb // HALO) - 1, 0), 0))
    small = [(1, d), (ng, GROUP, GROUP), (1, pw), (1, sw), (nh, GROUP, GROUP), (nh, GROUP, GROUP)]
    return _call_hosting(
        body, hosted, name="mixer_bwd", steps=nb,
        in_specs=[rev(d), rev(d), rev(proj.shape[1]), halo, VM, VM, VM, VM, VM, VM, VM, VM],
        out_specs=[rev(d), rev(proj.shape[1])] + [_const_spec(s) for s in small],
        out_shape=[jax.ShapeDtypeStruct((t, d), F32), jax.ShapeDtypeStruct(proj.shape, _MXU)]
        + [jax.ShapeDtypeStruct(s, F32) for s in small],
        scratch_shapes=[pltpu.VMEM((HALO, pw), F32), pltpu.VMEM((tb, 2 * sw), F32)],
        operands=(dh1, h, proj, proj, g, w_in, pool_w, pool_scale, sgu_g, sgu_w, sgu_bt, w_out))


def _kv_fwd(mem, gm, wk, wv):
    n, d = mem.shape

    def body(mem_ref, gm_ref, wk_ref, wv_ref, k_ref, v_ref, memn_ref):
        xhat, _ = _rms(mem_ref[...])
        memn = (xhat * gm_ref[...]).astype(_MXU)
        memn_ref[...] = memn
        k_ref[...] = _nn(memn, wk_ref[...]).astype(_MXU)
        v_ref[...] = _nn(memn, wv_ref[...]).astype(_MXU)

    return pl.pallas_call(
        body, name="kv_fwd", in_specs=[VM] * 4, out_specs=[VM] * 3,
        out_shape=[jax.ShapeDtypeStruct((n, d), _MXU)] * 3, compiler_params=_params(),
    )(mem, gm, wk, wv)


def _kv_bwd(dk, dv, mem, wk, wv):
    n, d = mem.shape

    def body(dk_ref, dv_ref, mem_ref, wk_ref, wv_ref, ggm_ref):
        dmemn = _nt(dk_ref[...].astype(_MXU), wk_ref[...]) + _nt(dv_ref[...].astype(_MXU), wv_ref[...])
        xhat, _ = _rms(mem_ref[...])
        ggm_ref[...] = jnp.sum(dmemn * xhat, axis=0, keepdims=True)

    return pl.pallas_call(
        body, name="kv_bwd", in_specs=[VM] * 5, out_specs=VM,
        out_shape=jax.ShapeDtypeStruct((1, d), F32), compiler_params=_params(),
    )(dk, dv, mem, wk, wv)


def _softmax(s):
    e = jnp.exp(s - jnp.max(s, axis=-1, keepdims=True))
    return e / jnp.sum(e, axis=-1, keepdims=True)


def _one_ahead(n, issue):
    nxt = issue(0)
    for a in range(n):
        cur = nxt
        if a + 1 < n:
            nxt = issue(a + 1)
        yield a, cur


def _xattn_fwd(h, g, wq, k, v, wo, hosted=None):
    t, d = h.shape
    hd = d // N_XHEADS
    scale = hd ** -0.5
    tb = _token_block(t, 512)

    def body(h_ref, g_ref, wq_ref, k_ref, v_ref, wo_ref, h2_ref, q_ref, o_ref, xn_ref):
        x = h_ref[...]
        xhat, _ = _rms(x)
        xn = (xhat * g_ref[...]).astype(_MXU)
        xn_ref[...] = xn
        qm = _nn(xn, wq_ref[...]).astype(_MXU)
        q_ref[...] = qm
        heads = [slice(a * hd, (a + 1) * hd) for a in range(N_XHEADS)]
        for a, s in _one_ahead(N_XHEADS, lambda a: _nt(qm[:, heads[a]], k_ref[:, heads[a]]) * scale):
            o_ref[:, heads[a]] = _nn(_softmax(s).astype(_MXU), v_ref[:, heads[a]]).astype(_MXU)
        h2_ref[...] = x + _nn(o_ref[...], wo_ref[...])

    blk = pl.BlockSpec((tb, d), lambda i: (i, 0))
    return _call_hosting(
        body, hosted, name="xattn_fwd", steps=t // tb,
        in_specs=[blk, VM, VM, VM, VM, VM], out_specs=[blk] * 4,
        out_shape=[jax.ShapeDtypeStruct((t, d), F32)] + [jax.ShapeDtypeStruct((t, d), _MXU)] * 3,
        scratch_shapes=[], operands=(h, g, wq, k, v, wo))


def _xattn_bwd(dh2, h, q, g, wq, k, v, wo, hosted=None):
    t, d = h.shape
    n = k.shape[0]
    hd = d // N_XHEADS
    scale = hd ** -0.5
    tb = _token_block(t, 512)
    pt = tb // 2 if tb % (2 * GROUP) == 0 else tb

    def body(dh2_ref, h_ref, q_ref, g_ref, wq_ref, k_ref, v_ref, wo_ref, dh_ref, dq_ref, dk_ref, dv_ref, gg_ref):
        @pl.when(pl.program_id(0) == 0)
        def _():
            for r in (dk_ref, dv_ref, gg_ref):
                r[...] = jnp.zeros(r.shape, F32)

        heads = [slice(a * hd, (a + 1) * hd) for a in range(N_XHEADS)]

        def part(at):
            rows = slice(at, at + pt)
            dh2v = dh2_ref[rows, :]
            dom = _nt(dh2v.astype(_MXU), wo_ref[...]).astype(_MXU)
            yield
            issue = lambda a: (_nt(q_ref[rows, heads[a]], k_ref[:, heads[a]]) * scale, _nt(dom[:, heads[a]], v_ref[:, heads[a]]))
            for a, (s, dpr) in _one_ahead(N_XHEADS, issue):
                sl = heads[a]
                pr = _softmax(s)
                dv_ref[:, sl] += _tn(pr.astype(_MXU), dom[:, sl])
                ds = (pr * (dpr - jnp.sum(dpr * pr, axis=-1, keepdims=True)) * scale).astype(_MXU)
                dq_ref[rows, sl] = _nn(ds, k_ref[:, sl]).astype(_MXU)
                dk_ref[:, sl] += _tn(ds, q_ref[rows, sl])
                yield
            dxn = _nt(dq_ref[rows, :], wq_ref[...])
            yield
            xhat, r = _rms(h_ref[rows, :])
            dx, gg = _rms_bwd(dxn, xhat, r, g_ref[...])
            gg_ref[...] += gg
            dh_ref[rows, :] = dh2v + dx

        _in_turns([part(at) for at in range(0, tb, pt)])

    blk = pl.BlockSpec((tb, d), lambda i: (i, 0))
    return _call_hosting(
        body, hosted, name="xattn_bwd", steps=t // tb,
        in_specs=[blk, blk, blk, VM, VM, VM, VM, VM],
        out_specs=[blk, blk, _const_spec((n, d)), _const_spec((n, d)), _const_spec((1, d))],
        out_shape=[jax.ShapeDtypeStruct((t, d), F32), jax.ShapeDtypeStruct((t, d), _MXU),
                   jax.ShapeDtypeStruct((n, d), F32), jax.ShapeDtypeStruct((n, d), F32), jax.ShapeDtypeStruct((1, d), F32)],
        scratch_shapes=[], operands=(dh2, h, q, g, wq, k, v, wo))


def _ffn_fwd(h, g, w_up, conv_w, conv_b, w_down, hosted=None, head=None):
    t, d = h.shape
    f = w_down.shape[0]
    ft = FF_TILE
    tb = _token_block(t, 256)

    def body(h_ref, g_ref, wup_ref, cw_ref, cb_ref, wdown_ref, *rest):
        if head is None:
            h3_ref, hh_ref, hc_ref, ext, carry, act_sc = rest
        else:
            gf_ref, tgt_ref, h3_ref, hh_ref, hc_ref, loss_ref, ggf_ref, ext, carry, act_sc = rest

        @pl.when(pl.program_id(0) == 0)
        def _():
            carry[...] = jnp.zeros(carry.shape, F32)
            if head is not None:
                loss_ref[...] = jnp.zeros(loss_ref.shape, F32)
                ggf_ref[...] = jnp.zeros(ggf_ref.shape, F32)

        x = h_ref[...]
        xhat, _ = _rms(x)
        xn = (xhat * g_ref[...]).astype(_MXU)
        acc = jnp.zeros((tb, d), F32)
        up = lambda j: [_nn(xn, wup_ref[:, off:off + ft]) for off in (j * ft, f + j * ft)]
        up_next = up(0)
        for j in range(f // ft):
            hc = []
            up_cur = up_next
            if j + 1 < f // ft:
                up_next = up(j + 1)
            for part, off in enumerate((j * ft, f + j * ft)):
                cols = slice(off, off + ft)
                cur = up_cur[part]
                hh_ref[:, cols] = cur.astype(_MXU)
                ext[part, 0:8, :] = carry[:, cols]
                ext[part, 8:8 + tb, :] = cur
                carry[:, cols] = cur[tb - 8:tb, :]
                hc.append(cb_ref[:, cols] + cw_ref[0:1, cols] * ext[part, 6:6 + tb, :]
                          + cw_ref[1:2, cols] * ext[part, 7:7 + tb, :] + cw_ref[2:3, cols] * cur)
                hc_ref[:, cols] = hc[part].astype(_MXU)
            at = j % DOWN_TILES
            act_sc[:, at * ft:(at + 1) * ft] = (hc[0] * jax.nn.sigmoid(hc[0]) * hc[1]).astype(_MXU)
            if at + 1 == DOWN_TILES or j + 1 == f // ft:
                acc = acc + _nn(act_sc[:, 0:(at + 1) * ft], wdown_ref[(j - at) * ft:(j + 1) * ft, :])
        if head is None:
            h3_ref[...] = x + acc
        else:
            yhat, r = _rms(x + acc)
            err = yhat * gf_ref[...] - tgt_ref[...]
            loss_ref[...] += 0.5 * jnp.sum(jnp.sum(err * err, axis=-1, keepdims=True), axis=0, keepdims=True) / d
            dx, gg = _rms_bwd(err / d, yhat, r, gf_ref[...])
            ggf_ref[...] += gg
            h3_ref[...] = dx

    blk = lambda w: pl.BlockSpec((tb, w), lambda i: (i, 0))
    in_specs = [blk(d), VM, VM, VM, VM, VM]
    out_specs = [blk(d), blk(2 * f), blk(2 * f)]
    out_shape = [jax.ShapeDtypeStruct((t, d), F32), jax.ShapeDtypeStruct((t, 2 * f), _MXU), jax.ShapeDtypeStruct((t, 2 * f), _MXU)]
    operands = (h, g, w_up, conv_w, conv_b, w_down)
    if head is not None:
        in_specs += [VM, blk(d)]
        out_specs += [_const_spec((1, 1)), _const_spec((1, d))]
        out_shape += [jax.ShapeDtypeStruct((1, 1), F32), jax.ShapeDtypeStruct((1, d), F32)]
        operands += tuple(head)
    return _call_hosting(
        body, hosted, name="ffn_fwd", steps=t // tb, in_specs=in_specs, out_specs=out_specs, out_shape=out_shape,
        scratch_shapes=[pltpu.VMEM((2, 8 + tb, ft), F32), pltpu.VMEM((8, 2 * f), F32), pltpu.VMEM((tb, DOWN_TILES * ft), _MXU)],
        operands=operands)


def _ffn_bwd(dh3, h, hh, hc, g, w_up, conv_w, w_down, hosted=None):
    t, d = h.shape
    f = w_down.shape[0]
    ft = FF_TILE
    tb = _token_block(t, 256)
    nb = t // tb

    def body(dh3_ref, h_ref, hh_ref, hc_ref, g_ref, wup_ref, cw_ref, wdown_ref,
             dh_ref, dhh_ref, act_ref, xn_ref, gcw_ref, gcb_ref, gg_ref, dcarry):
        @pl.when(pl.program_id(0) == 0)
        def _():
            for r in (gcw_ref, gcb_ref, gg_ref, dcarry):
                r[...] = jnp.zeros(r.shape, F32)

        dh3v = dh3_ref[...]
        dhm = dh3v.astype(_MXU)
        dxn = jnp.zeros((tb, d), F32)
        dact_next = _nt(dhm, wdown_ref[0:ft, :])
        for j in range(f // ft):
            dact = dact_next
            if j + 1 < f // ft:
                dact_next = _nt(dhm, wdown_ref[(j + 1) * ft:(j + 2) * ft, :])
            gate = hc_ref[:, j * ft:(j + 1) * ft].astype(F32)
            val = hc_ref[:, f + j * ft:f + (j + 1) * ft].astype(F32)
            sg = jax.nn.sigmoid(gate)
            silu = gate * sg
            act_ref[:, j * ft:(j + 1) * ft] = (silu * val).astype(_MXU)
            dhc = (dact * val * sg * (1.0 + gate * (1.0 - sg)), dact * silu)
            for part, off in enumerate((j * ft, f + j * ft)):
                cols = slice(off, off + ft)
                dc = dhc[part]
                c0 = hh_ref[:, cols].astype(F32)
                after = dcarry[:, cols]
                ahead1 = _shift_rows(dc, -1, after)
                ahead2 = _shift_rows(dc, -2, after)
                dcarry[:, cols] = dc[0:8, :]
                gcb_ref[:, cols] += jnp.sum(dc, axis=0, keepdims=True)
                gcw_ref[0:1, cols] += jnp.sum(ahead2 * c0, axis=0, keepdims=True)
                gcw_ref[1:2, cols] += jnp.sum(ahead1 * c0, axis=0, keepdims=True)
                gcw_ref[2:3, cols] += jnp.sum(dc * c0, axis=0, keepdims=True)
                dhh = (cw_ref[2:3, cols] * dc + cw_ref[1:2, cols] * ahead1 + cw_ref[0:1, cols] * ahead2).astype(_MXU)
                dhh_ref[:, cols] = dhh
                dxn = dxn + _nt(dhh, wup_ref[:, cols])
        xhat, r = _rms(h_ref[...])
        xn_ref[...] = (xhat * g_ref[...]).astype(_MXU)
        dx, gg = _rms_bwd(dxn, xhat, r, g_ref[...])
        gg_ref[...] += gg
        dh_ref[...] = dh3v + dx

    rev = lambda w: pl.BlockSpec((tb, w), lambda i: (nb - 1 - i, 0))
    return _call_hosting(
        body, hosted, name="ffn_bwd", steps=nb,
        in_specs=[rev(d), rev(d), rev(2 * f), rev(2 * f), VM, VM, VM, VM],
        out_specs=[rev(d), rev(2 * f), rev(f), rev(d), _const_spec((3, 2 * f)), _const_spec((1, 2 * f)), _const_spec((1, d))],
        out_shape=[jax.ShapeDtypeStruct((t, d), F32), jax.ShapeDtypeStruct((t, 2 * f), _MXU), jax.ShapeDtypeStruct((t, f), _MXU),
                   jax.ShapeDtypeStruct((t, d), _MXU),
                   jax.ShapeDtypeStruct((3, 2 * f), F32), jax.ShapeDtypeStruct((1, 2 * f), F32), jax.ShapeDtypeStruct((1, d), F32)],
        scratch_shapes=[pltpu.VMEM((8, 2 * f), F32)],
        operands=(dh3, h, hh, hc, g, w_up, conv_w, w_down))


def _largest_tile(n, cap, mult=128):
    best = None
    for c in range(mult, min(n, cap) + 1, mult):
        if n % c == 0:
            best = c
    return best if best is not None else n


def _grad_matmul(a, b, name, layer, n_layers, into=None, hosted=None):
    t, m = a.shape
    n = b.shape[1]
    tm, tn, tk = _largest_tile(m, 1408), _largest_tile(n, 1536), _largest_tile(t, 1024)
    nk = t // tk

    def body(a_ref, b_ref, *rest):
        o_ref = rest[-1]

        @pl.when(pl.program_id(2) == 0)
        def _():
            o_ref[...] = jnp.zeros(o_ref.shape, F32)

        o_ref[...] += _tn(a_ref[...].astype(_MXU), b_ref[...].astype(_MXU))

    in_specs = [pl.BlockSpec((tk, tm), lambda i, j, k: (k, i)), pl.BlockSpec((tk, tn), lambda i, j, k: (k, j))]
    operands = (a, b)
    aliases = {}
    if into is not None:
        in_specs.append(pl.BlockSpec(memory_space=pl.ANY))
        operands = (a, b, into)
        aliases = {2: 0}
    (out,), got = _call_hosting(
        body, hosted, name=name, steps=(m // tm, n // tn, nk), in_specs=in_specs,
        out_specs=[pl.BlockSpec((None, tm, tn), lambda i, j, k: (layer, i, j))],
        out_shape=[jax.ShapeDtypeStruct((n_layers, m, n), F32)], scratch_shapes=[], operands=operands, aliases=aliases)
    return out, got


def _adamw_math(w, g, m, v):
    m = ADAM_B1 * m + (1.0 - ADAM_B1) * g
    v = ADAM_B2 * v + (1.0 - ADAM_B2) * (g * g)
    m_hat = m / (1.0 - ADAM_B1 ** ADAM_STEP)
    v_hat = v / (1.0 - ADAM_B2 ** ADAM_STEP)
    return -ADAM_LR * (m_hat / (jnp.sqrt(v_hat) + ADAM_EPS) + ADAM_WD * w), m, v


def _row_block(rows, cols, max_bytes=1 << 20, mult=16):
    best = None
    for r in range(mult, rows + 1, mult):
        if rows % r == 0 and r * cols * 4 <= max_bytes:
            best = r
    return best if best is not None else rows


def _adamw_big(ws, gs, ms, vs, name):
    n = len(ws)
    shape = ws[0].shape
    cols = shape[-1]
    flat = lambda a: a.reshape(-1, cols)
    rows = flat(ws[0]).shape[0]
    rb = _row_block(rows, cols, (2 << 20) // n)

    def body(*refs):
        for a in range(n):
            w_ref, g_ref, m_ref, v_ref = (refs[s * n + a] for s in range(4))
            go_ref, d_ref, nm_ref, nv_ref = (refs[(4 + s) * n + a] for s in range(4))
            g = g_ref[...]
            go_ref[...] = g
            d_ref[...], nm_ref[...], nv_ref[...] = _adamw_math(w_ref[...], g, m_ref[...], v_ref[...])

    blk = pl.BlockSpec((rb, cols), lambda i: (i, 0))
    outs = pl.pallas_call(
        body, name=name, grid=(rows // rb,), in_specs=[blk] * (4 * n), out_specs=[blk] * (4 * n),
        out_shape=[jax.ShapeDtypeStruct((rows, cols), F32)] * (4 * n), compiler_params=_params(("parallel",)),
    )(*[flat(a) for group in (ws, gs, ms, vs) for a in group])
    return [[outs[s * n + a].reshape(shape) for s in range(4)] for a in range(n)]


def _adamw_small(ws, gs, ms, vs):
    n = len(ws)

    def body(*refs):
        for a in range(n):
            w_ref, g_ref, m_ref, v_ref = (refs[s * n + a] for s in range(4))
            d_ref, nm_ref, nv_ref = (refs[(4 + s) * n + a] for s in range(3))
            d_ref[...], nm_ref[...], nv_ref[...] = _adamw_math(w_ref[...], g_ref[...], m_ref[...], v_ref[...])

    outs = pl.pallas_call(
        body, name="adamw_small", in_specs=[VM] * (4 * n), out_specs=[VM] * (3 * n),
        out_shape=[jax.ShapeDtypeStruct(w.shape, F32) for w in ws] * 3, compiler_params=_params(),
    )(*ws, *gs, *ms, *vs)
    return outs[:n], outs[n:2 * n], outs[2 * n:]


def _place():
    x, y, c = lax.axis_index("x"), lax.axis_index("y"), lax.axis_index("c")
    chips = [(1 - x, y), (x, 1 - y), (1 - x, 1 - y)]
    return x, y, c, chips


def _rows(start, size, mult=16):
    return pl.ds(pl.multiple_of(start, mult), size)


def _full_window(ref, axis, chip, half=None):
    r, c = ref.shape
    if axis == 0:
        rs = r // 4
        if half is None:
            return ref.at[_rows(chip * rs, rs), :]
        return ref.at[_rows(chip * rs + half * (rs // 2), rs // 2), :]
    cs = c // 4
    if half is None:
        return ref.at[:, _rows(chip * cs, cs, 128)]
    return ref.at[_rows(half * (r // 2), r // 2), _rows(chip * cs, cs, 128)]


def _remote(src, dst, send_sem, recv_sem, to):
    return pltpu.make_async_remote_copy(src_ref=src, dst_ref=dst, send_sem=send_sem, recv_sem=recv_sem,
                                        device_id=to, device_id_type=MESH)


def _scalars(*vals):
    return jnp.stack([jnp.asarray(v, jnp.int32) for v in vals])


def _cast_place(shards, axis, chip, name):
    n = len(shards)
    nl, rs, cs = shards[0].shape
    full = (rs * 4, cs) if axis == 0 else (rs, cs * 4)
    rb = _row_block(rs, cs, (4 << 20) // (n * nl))
    nrb = rs // rb

    def body(chip_ref, *refs):
        for a in range(n):
            for l in range(nl):
                refs[n + a * nl + l][...] = refs[a][l].astype(_PAY)

    if axis == 0:
        out_map = lambda i, chip_ref: (chip_ref[0] * nrb + i, 0)
    else:
        out_map = lambda i, chip_ref: (i, chip_ref[0])
    outs = pl.pallas_call(
        body, name=name,
        grid_spec=pltpu.PrefetchScalarGridSpec(
            num_scalar_prefetch=1, grid=(nrb,),
            in_specs=[pl.BlockSpec((nl, rb, cs), lambda i, chip_ref: (0, i, 0))] * n,
            out_specs=[pl.BlockSpec((rb, cs), out_map)] * (n * nl)),
        out_shape=[jax.ShapeDtypeStruct(full, _PAY)] * (n * nl), compiler_params=_params(("parallel",)),
    )(_scalars(chip), *shards)
    return [[outs[a * nl + l] for l in range(nl)] for a in range(n)]


def _hosted_allgather(placed, axes):
    n = len(placed)

    def each(outs, half_of):
        x, y, c, chips = _place()
        for i in range(n):
            for k, chip in enumerate(chips):
                yield i * 3 + k, (*chip, c), (x, y, 1 - c), _full_window(outs[i], axes[i], 2 * x + y, c), \
                    _full_window(outs[i], axes[i], 2 * chip[0] + chip[1], half_of(c))

    def start(_, outs, sems):
        send, recv, _, _ = sems
        for s, peer, _, mine, _ in each(outs, lambda c: c):
            _remote(mine, mine, send.at[s], recv.at[s], peer).start()

    def middle(_, outs, sems):
        send, recv, fsend, frecv = sems
        for s, _, sibling, _, got in each(outs, lambda c: c):
            _remote(got, got, send.at[s], recv.at[s], sibling).wait_recv()
            _remote(got, got, fsend.at[s], frecv.at[s], sibling).start()

    def finish(_, outs, sems):
        send, recv, fsend, frecv = sems
        for s, _, sibling, _, got in each(outs, lambda c: 1 - c):
            _remote(got, got, fsend.at[s], frecv.at[s], sibling).wait_recv()
        for s, peer, sibling, mine, got in each(outs, lambda c: c):
            _remote(mine, mine, send.at[s], recv.at[s], peer).wait_send()
            _remote(got, got, fsend.at[s], frecv.at[s], sibling).wait_send()

    return _Hosted(tuple(placed), True, (), (pltpu.SemaphoreType.DMA((n * 3,)),) * 4, (start, middle, finish))


def _allgather_conv(conv_shard):
    nl, taps, cs = conv_shard.shape

    def body(in_ref, out_ref, send, recv, local):
        x, y, c, chips = _place()
        mine = out_ref.at[:, :, _rows((2 * x + y) * cs, cs, 128)]
        own = pltpu.make_async_copy(in_ref, mine, local)
        own.start()
        sends = [_remote(in_ref, mine, send.at[k], recv.at[k], (*chip, c)) for k, chip in enumerate(chips)]
        for cp in sends:
            cp.start()
        for k, chip in enumerate(chips):
            got = out_ref.at[:, :, _rows((2 * chip[0] + chip[1]) * cs, cs, 128)]
            _remote(got, got, send.at[k], recv.at[k], (*chip, c)).wait_recv()
        for cp in sends:
            cp.wait_send()
        own.wait()

    return pl.pallas_call(
        body, name="allgather_conv", in_specs=[HB], out_specs=HB, out_shape=jax.ShapeDtypeStruct((nl, taps, cs * 4), conv_shard.dtype),
        scratch_shapes=[pltpu.SemaphoreType.DMA((3,)), pltpu.SemaphoreType.DMA((3,)), pltpu.SemaphoreType.DMA],
        compiler_params=pltpu.CompilerParams(has_side_effects=True),
    )(conv_shard)


def _hosted_exchange(grads, axes, layer):
    na = len(grads)
    views = [g.reshape(g.shape[0], 4, 2, g.shape[1] // 8, g.shape[2]) if ax == 0 else g for g, ax in zip(grads, axes)]

    def region(ref, axis, half):
        if axis == 0:
            return ref.at[layer, :, half]
        r = ref.shape[1]
        return ref.at[layer, _rows(half * (r // 2), r // 2), :]

    def copies(ins, land, sems):
        send, recv = sems
        x, y, c, _ = _place()
        return [_remote(region(ins[a], axes[a], 1 - c), land[a], send.at[a], recv.at[a], (x, y, 1 - c)) for a in range(na)]

    def start(ins, land, sems):
        for cp in copies(ins, land, sems):
            cp.start()

    def finish(ins, land, sems):
        for cp in copies(ins, land, sems):
            cp.wait()

    shapes = [(4, g.shape[1] // 8, g.shape[2]) if ax == 0 else (g.shape[1] // 2, g.shape[2]) for g, ax in zip(grads, axes)]
    return _Hosted(tuple(views), False, tuple(jax.ShapeDtypeStruct(s, F32) for s in shapes),
                   (pltpu.SemaphoreType.DMA((na,)),) * 2, (start, None, finish))


def _add_cast(mines, theirs, core, base, name):
    n = len(mines)
    na, nb, cols = theirs[0].shape
    rb = _row_block(nb, cols, (4 << 20) // n)

    def body(core_ref, *refs):
        for a in range(n):
            refs[2 * n + a][...] = (refs[a][...] + refs[n + a][...]).astype(_PAY)

    blk = pl.BlockSpec((None, rb, cols), lambda i, k, core_ref: (i, k, 0))
    return pl.pallas_call(
        body, name=name,
        grid_spec=pltpu.PrefetchScalarGridSpec(
            num_scalar_prefetch=1, grid=(na, nb // rb),
            in_specs=[pl.BlockSpec((None, None, rb, cols), lambda i, k, core_ref: (base + i, core_ref[0], k, 0))] * n + [blk] * n,
            out_specs=[blk] * n),
        out_shape=[jax.ShapeDtypeStruct((na, nb, cols), _PAY)] * n, compiler_params=_params(("parallel", "parallel")),
    )(_scalars(core), *mines, *theirs)


def _piece(ref, axis, chip):
    if axis == 0:
        return ref.at[chip]
    cs = ref.shape[1] // 4
    return ref.at[:, _rows(chip * cs, cs, 128)]


def _hosted_scatter(sums, axes):
    na = len(sums)

    def piece_shape(a):
        if axes[a] == 0:
            return (sums[a].shape[1], sums[a].shape[2])
        return (sums[a].shape[0], sums[a].shape[1] // 4)

    def copies(ins, slots, sems):
        send, recv = sems
        _, _, c, chips = _place()
        return [_remote(_piece(ins[a], axes[a], 2 * chip[0] + chip[1]), slots[a].at[k], send.at[a * 3 + k], recv.at[a * 3 + k], (*chip, c))
                for a in range(na) for k, chip in enumerate(chips)]

    def start(ins, slots, sems):
        for cp in copies(ins, slots, sems):
            cp.start()

    def finish(ins, slots, sems):
        for cp in copies(ins, slots, sems):
            cp.wait()

    return _Hosted(tuple(sums), False, tuple(jax.ShapeDtypeStruct((3,) + piece_shape(a), sums[a].dtype) for a in range(na)),
                   (pltpu.SemaphoreType.DMA((na * 3,)),) * 2, (start, None, finish))


def _sum_slots(sums, slots, axis, chip, core, layer, n_layers, name, into=None):
    _, hr, cs = slots.shape
    rb = _row_block(hr, cs, 4 << 20)

    def body(at_ref, own_ref, s_ref, *rest):
        rest[-1][...] = ((own_ref[...].astype(F32) + s_ref[0].astype(F32)) + s_ref[1].astype(F32)) + s_ref[2].astype(F32)

    if axis == 0:
        own = pl.BlockSpec((None, rb, cs), lambda k, at_ref: (at_ref[0], k, 0))
    else:
        own = pl.BlockSpec((rb, cs), lambda k, at_ref: (k, at_ref[0]))
    in_specs = [own, pl.BlockSpec((3, rb, cs), lambda k, at_ref: (0, k, 0))]
    operands = (sums, slots)
    aliases = {}
    if into is not None:
        in_specs.append(pl.BlockSpec(memory_space=pl.ANY))
        operands = (sums, slots, into)
        aliases = {3: 0}
    return pl.pallas_call(
        body, name=name,
        grid_spec=pltpu.PrefetchScalarGridSpec(
            num_scalar_prefetch=1, grid=(hr // rb,), in_specs=in_specs,
            out_specs=pl.BlockSpec((None, None, rb, cs), lambda k, at_ref: (layer, at_ref[1], k, 0))),
        out_shape=jax.ShapeDtypeStruct((n_layers, 2, hr, cs), F32), input_output_aliases=aliases,
        compiler_params=_params(("parallel",)),
    )(_scalars(chip, core), *operands)


def _hosted_assemble(shards):
    na = len(shards)

    def copies(_, outs, sems):
        send, recv = sems
        x, y, c, _ = _place()
        halves = [outs[a].at[:, _rows(c * (outs[a].shape[1] // 2), outs[a].shape[1] // 2), :] for a in range(na)]
        return [_remote(mine, mine, send.at[a], recv.at[a], (x, y, 1 - c)) for a, mine in enumerate(halves)]

    def start(ins, outs, sems):
        for cp in copies(ins, outs, sems):
            cp.start()

    def finish(ins, outs, sems):
        for cp in copies(ins, outs, sems):
            cp.wait()

    return _Hosted(tuple(shards), True, (), (pltpu.SemaphoreType.DMA((na,)),) * 2, (start, None, finish))


def _allreduce_small(buf, hosted):
    rows, w = buf.shape
    half = rows // 2
    shapes = [_hosted_results(hs) for hs in hosted]
    flat = lambda lists: [x for xs in lists for x in xs]
    aliases, in_at, out_at = {}, 1, 1
    for hs, sh in zip(hosted, shapes):
        if hs.aliased:
            aliases.update({in_at + i: out_at + i for i in range(len(hs.operands))})
        in_at += len(hs.operands)
        out_at += len(sh)

    def body(buf_ref, *refs):
        at = [0]

        def take(n):
            at[0] += n
            return refs[at[0] - n:at[0]]

        h_in = [take(len(hs.operands)) for hs in hosted]
        (out_ref,) = take(1)
        h_out = [take(len(sh)) for sh in shapes]
        land, slots, red, sems_send, sems_recv = take(5)
        h_sems = [take(len(hs.sems)) for hs in hosted]
        for hs, a, b, s in zip(hosted, h_in, h_out, h_sems):
            hs.stages[0](a, b, s)
        x, y, c, chips = _place()
        me = 2 * x + y
        sibling = (x, y, 1 - c)
        first = _remote(buf_ref, land, sems_send.at[0], sems_recv.at[0], sibling)
        first.start()
        first.wait()
        mine = pl.ds(pl.multiple_of(c * half, 8), half)
        slots[me] = buf_ref[mine, :] + land[mine, :]
        sends = []
        for k, chip in enumerate(chips):
            cp = _remote(slots.at[me], slots.at[me], sems_send.at[1 + k], sems_recv.at[1 + k], (*chip, c))
            cp.start()
            sends.append(cp)
        for k, chip in enumerate(chips):
            got = slots.at[2 * chip[0] + chip[1]]
            _remote(got, got, sems_send.at[1 + k], sems_recv.at[1 + k], sibling).wait_recv()
        red[...] = ((slots[0] + slots[1]) + slots[2]) + slots[3]
        out_ref[mine, :] = red[...]
        last = _remote(red, out_ref.at[mine, :], sems_send.at[4], sems_recv.at[4], sibling)
        last.start()
        theirs = out_ref.at[pl.ds(pl.multiple_of((1 - c) * half, 8), half), :]
        _remote(red, theirs, sems_send.at[4], sems_recv.at[4], sibling).wait_recv()
        for cp in sends:
            cp.wait_send()
        last.wait_send()
        for hs, a, b, s in zip(hosted, h_in, h_out, h_sems):
            hs.stages[2](a, b, s)

    outs = pl.pallas_call(
        body, name="allreduce_small", in_specs=[VM] + [HB] * (in_at - 1), out_specs=[VM] + [HB] * (out_at - 1),
        out_shape=[jax.ShapeDtypeStruct((rows, w), F32)] + flat(shapes), input_output_aliases=aliases,
        scratch_shapes=[pltpu.VMEM((rows, w), F32), pltpu.VMEM((4, half, w), F32), pltpu.VMEM((half, w), F32),
                        pltpu.SemaphoreType.DMA((5,)), pltpu.SemaphoreType.DMA((5,))] + flat(hs.sems for hs in hosted),
        compiler_params=pltpu.CompilerParams(has_side_effects=True, vmem_limit_bytes=VMEM_LIMIT),
    )(buf, *flat(hs.operands for hs in hosted))
    results, at = [], 1
    for sh in shapes:
        results.append(outs[at:at + len(sh)])
        at += len(sh)
    return outs[0], results


BIG = ("w_in", "w_out", "wq", "wk", "wv", "wo", "w_up", "w_down")
MIXER, ATTN, MLP = ("w_in", "w_out"), ("wq", "wk", "wv", "wo"), ("w_up", "w_down")
BIG_AXIS = {"w_in": 1, "w_out": 0, "wq": 0, "wk": 0, "wv": 0, "wo": 0, "w_up": 1, "w_down": 0}
SMALL = ("norm_mix_g", "pool_w", "pool_scale", "sgu_g", "sgu_w", "sgu_b", "norm_xattn_g", "mem_norm_g", "norm_ffn_g",
         "conv_w", "conv_b", "final_norm_g")
ORDER = ("norm_mix_g", "w_in", "pool_w", "pool_scale", "sgu_g", "sgu_w", "sgu_b", "w_out", "norm_xattn_g", "mem_norm_g",
         "wq", "wk", "wv", "wo", "norm_ffn_g", "w_up", "conv_w", "conv_b", "w_down", "final_norm_g")
PACK_WIDTH = 512


def kernel(x, mem, norm_mix_g, w_in, pool_w, pool_scale, sgu_g, sgu_w, sgu_b, w_out, norm_xattn_g, mem_norm_g, wq, wk, wv, wo, norm_ffn_g, w_up, conv_w, conv_b, w_down, final_norm_g, loss_target, m_norm_mix_g, m_w_in, m_pool_w, m_pool_scale, m_sgu_g, m_sgu_w, m_sgu_b, m_w_out, m_norm_xattn_g, m_mem_norm_g, m_wq, m_wk, m_wv, m_wo, m_norm_ffn_g, m_w_up, m_conv_w, m_conv_b, m_w_down, m_final_norm_g, v_norm_mix_g, v_w_in, v_pool_w, v_pool_scale, v_sgu_g, v_sgu_w, v_sgu_b, v_w_out, v_norm_xattn_g, v_mem_norm_g, v_wq, v_wk, v_wv, v_wo, v_norm_ffn_g, v_w_up, v_conv_w, v_conv_b, v_w_down, v_final_norm_g):
    given = dict(locals())
    w = {n: given[n] for n in ORDER}
    mom = {n: given["m_" + n] for n in ORDER}
    var = {n: given["v_" + n] for n in ORDER}
    nl = w_in.shape[0]
    xs, mems, tgt = x[0], mem[0], loss_target[0]
    chip = 2 * lax.axis_index("x") + lax.axis_index("y")
    core = lax.axis_index("c")

    axes_of = lambda names: [BIG_AXIS[n] for n in names]
    alike = {}
    for n in BIG:
        alike.setdefault((w[n].shape, BIG_AXIS[n]), []).append(n)
    placed = [{} for _ in range(nl)]
    for (_, axis), names in alike.items():
        for n, per_layer in zip(names, _cast_place([w[n] for n in names], axis, chip, "place_" + names[0])):
            for l in range(nl):
                placed[l][n] = per_layer[l]
    conv_full = _allgather_conv(conv_w)

    def gather(names, l):
        return _hosted_allgather([placed[l][n] for n in names], axes_of(names))

    full = [dict(zip(MIXER, _run_hosted(gather(MIXER, 0), "allgather_weights")))]

    row = lambda a, l: a[l][None, :]
    saved = []
    h = xs
    for l in range(nl):
        fw = full[l]
        sbt = jnp.broadcast_to(sgu_b[l][:, :, None], sgu_w[l].shape)
        (h1, proj, xn1, mix), got = _mixer_fwd(h, row(norm_mix_g, l), fw["w_in"], pool_w[l], row(pool_scale, l), row(sgu_g, l), sgu_w[l], sbt, fw["w_out"],
                                               [gather(ATTN, 0), gather(("w_down",), 0)] if l == 0 else None)
        if l == 0:
            fw.update(zip(ATTN, got[0]))
            fw["w_down"] = got[1][0]
        k, v, memn = _kv_fwd(mems, row(mem_norm_g, l), fw["wk"], fw["wv"])
        (h2, q, o, xn2), got = _xattn_fwd(h1, row(norm_xattn_g, l), fw["wq"], k, v, fw["wo"], [gather(("w_up",), 0)] if l == 0 else None)
        if l == 0:
            fw["w_up"] = got[0][0]
        outs, got = _ffn_fwd(h2, row(norm_ffn_g, l), fw["w_up"], conv_full[l], row(conv_b, l), fw["w_down"],
                             [gather(BIG, l + 1)] if l + 1 < nl else None, None if l + 1 < nl else (final_norm_g[None, :], tgt))
        h3, hh, hc = outs[:3]
        if l + 1 < nl:
            full.append(dict(zip(BIG, got[0])))
        else:
            dh, loss_part, g_final = h3, outs[3], outs[4]
        saved.append(dict(h=h, h1=h1, h2=h2, proj=proj, xn1=xn1, mix=mix, k=k, v=v, memn=memn, q=q, o=o, xn2=xn2, hh=hh, hc=hc, sbt=sbt))
        h = h3


    big_grads = {}
    small_grads = [None] * nl

    def weight_grad(n, a, b, l, hosted=None):
        big_grads[n], got = _grad_matmul(a, b, "grad_" + n, l, nl, big_grads.get(n), hosted)
        return got

    sums, slots = {}, {}

    def exchange(names, l):
        return _hosted_exchange([big_grads[n] for n in names], axes_of(names), l)

    def scatter(names, l):
        return _hosted_scatter([sums[n, l] for n in names], axes_of(names))

    def add_casts(names, theirs, l):
        theirs = dict(zip(names, theirs))
        for group in alike.values():
            group = [n for n in group if n in theirs]
            if not group:
                continue
            gl, gr, gc = big_grads[group[0]].shape
            if BIG_AXIS[group[0]] == 0:
                outs = _add_cast([big_grads[n].reshape(gl * 4, 2, gr // 8, gc) for n in group], [theirs[n] for n in group],
                                 core, l * 4, "grad_chip_sum_" + group[0])
            else:
                outs = [o[0] for o in _add_cast([big_grads[n].reshape(gl, 2, gr // 2, gc) for n in group], [theirs[n][None] for n in group],
                                                core, l, "grad_chip_sum_" + group[0])]
            for n, o in zip(group, outs):
                sums[n, l] = o

    def keep_slots(names, got, l):
        for n, sl in zip(names, got):
            slots[n, l] = sl

    shard_grads = {}

    def shard_halves(names):
        out = []
        for n in names:
            buf = None
            for l in range(nl):
                buf = _sum_slots(sums[n, l], slots[n, l], BIG_AXIS[n], chip, core, l, nl, "grad_sum_" + n, buf)
            out.append(buf.reshape(nl, 2 * buf.shape[2], buf.shape[3]))
        return out

    for l in reversed(range(nl)):
        fw, s = full[l], saved[l]
        above = l + 1 < nl
        dh3 = dh
        (dh2, dhh, act, xn3, g_cw, g_cb, g_nf), got = _ffn_bwd(dh3, s["h2"], s["hh"], s["hc"], row(norm_ffn_g, l), fw["w_up"], conv_full[l], fw["w_down"],
                                                         [exchange(MIXER, l + 1), scatter(ATTN, l + 1)] if above else None)
        if above:
            add_casts(MIXER, got[0], l + 1)
            keep_slots(ATTN, got[1], l + 1)
        weight_grad("w_up", xn3, dhh, l)
        weight_grad("w_down", act, dh3, l)
        (dh1, dq, dk, dv, g_nx), got = _xattn_bwd(dh2, s["h1"], s["q"], row(norm_xattn_g, l), fw["wq"], s["k"], s["v"], fw["wo"],
                                                  [exchange(MLP, l), scatter(MIXER, l + 1) if above else None])
        add_casts(MLP, got[0], l)
        if above:
            keep_slots(MIXER, got[1], l + 1)
        weight_grad("wq", s["xn2"], dq, l)
        weight_grad("wo", s["o"], dh2, l)
        weight_grad("wk", s["memn"], dk, l)
        weight_grad("wv", s["memn"], dv, l)
        g_mn = _kv_bwd(dk, dv, mems, fw["wk"], fw["wv"])
        (dh0, dproj, g_nm, g_pw, g_ps, g_sg, g_sw, g_sbt), got = _mixer_bwd(dh1, s["h"], s["proj"], row(norm_mix_g, l), fw["w_in"], pool_w[l], row(pool_scale, l), row(sgu_g, l), sgu_w[l], s["sbt"], fw["w_out"],
                                                                           [scatter(MLP, l), exchange(ATTN, l)])
        keep_slots(MLP, got[0], l)
        add_casts(ATTN, got[1], l)
        half = len(ATTN) // 2
        got = weight_grad("w_in", s["xn1"], dproj, l, [scatter(ATTN[:half], l), _hosted_assemble(shard_halves(MLP))] if l == 0 else None)
        if l == 0:
            keep_slots(ATTN[:half], got[0], l)
            shard_grads.update(zip(MLP, got[1]))
        got = weight_grad("w_out", s["mix"], dh1, l, [scatter(ATTN[half:], l)] if l == 0 else None)
        if l == 0:
            keep_slots(ATTN[half:], got[0], l)
        small_grads[l] = dict(norm_mix_g=g_nm, pool_w=g_pw, pool_scale=g_ps, sgu_g=g_sg, sgu_w=g_sw, sgu_b=jnp.sum(g_sbt, axis=-1),
                              norm_xattn_g=g_nx, mem_norm_g=g_mn, norm_ffn_g=g_nf, conv_w=g_cw, conv_b=g_cb)
        dh = dh0
    grad_x = dh[None]

    add_casts(MIXER, _run_hosted(exchange(MIXER, 0), "grad_sibling_exchange"), 0)

    layered = [n for n in SMALL if n != "final_norm_g"]
    parts = [small_grads[l][n].reshape(-1, PACK_WIDTH) for n in layered for l in range(nl)]
    parts.append(g_final.reshape(-1, PACK_WIDTH))
    parts.append(jnp.pad(loss_part, ((0, 0), (0, PACK_WIDTH - 1))))
    used = sum(p.shape[0] for p in parts)
    total = -(-used // 16) * 16
    packed, got = _allreduce_small(jnp.concatenate(parts + [jnp.zeros((total - used, PACK_WIDTH), F32)], axis=0),
                                   [scatter(MIXER, 0), _hosted_assemble(shard_halves(ATTN))])
    keep_slots(MIXER, got[0], 0)
    shard_grads.update(zip(ATTN, got[1]))
    shard_grads.update(zip(MIXER, _run_hosted(_hosted_assemble(shard_halves(MIXER)), "grad_sibling_assemble")))

    delta, new_m, new_v, grads = {}, {}, {}, {}
    for names in alike.values():
        outs = _adamw_big([w[n] for n in names], [shard_grads[n] for n in names], [mom[n] for n in names], [var[n] for n in names], "adamw_" + names[0])
        for n, out in zip(names, outs):
            grads[n], delta[n], new_m[n], new_v[n] = out
    at = 0
    for n in layered:
        per_layer = []
        for l in range(nl):
            shape = small_grads[l][n].shape
            nrow = small_grads[l][n].size // PACK_WIDTH
            per_layer.append(packed[at:at + nrow].reshape(shape))
            at += nrow
        g = jnp.stack(per_layer)
        if n == "conv_w":
            cs = conv_w.shape[2]
            g = lax.dynamic_slice_in_dim(g, chip * cs, cs, axis=2)
        grads[n] = g.reshape(w[n].shape)
    grads["final_norm_g"] = packed[at:at + g_final.size // PACK_WIDTH].reshape(final_norm_g.shape)
    at += g_final.size // PACK_WIDTH
    loss = packed[at, 0]

    two_d = lambda a: a.reshape(-1, a.shape[-1])
    ds, nms, nvs = _adamw_small([two_d(w[n]) for n in SMALL], [two_d(grads[n]) for n in SMALL],
                                [two_d(mom[n]) for n in SMALL], [two_d(var[n]) for n in SMALL])
    for n, d_, m_, v_ in zip(SMALL, ds, nms, nvs):
        delta[n], new_m[n], new_v[n] = d_.reshape(w[n].shape), m_.reshape(w[n].shape), v_.reshape(w[n].shape)

    return (loss, grad_x, *[grads[n] for n in ORDER], *[delta[n] for n in ORDER], *[new_m[n] for n in ORDER], *[new_v[n] for n in ORDER])
```

```python
import math
from typing import NamedTuple

import jax
import jax.numpy as jnp
from jax import lax
from jax.experimental import pallas as pl
from jax.experimental.pallas import tpu as pltpu

F32 = jnp.float32
_MXU = jnp.bfloat16
_PAY = jnp.bfloat16
EPS = 1e-6
WINDOWS = (2, 4, 8, 16)
GROUP = 128
N_XHEADS = 4
HALO = 16
FF_TILE = 256
DOWN_TILES = 6
LATE_MIDDLE = 0.875
VMEM_LIMIT = 60 * 1024 * 1024
MESH = pl.DeviceIdType.MESH

ADAM_LR, ADAM_B1, ADAM_B2, ADAM_EPS, ADAM_WD, ADAM_STEP = 0.001, 0.9, 0.999, 1e-08, 0.01, 10

VM = pl.BlockSpec(memory_space=pltpu.VMEM)
HB = pl.BlockSpec(memory_space=pltpu.HBM)


def _nn(a, b):
    return jnp.dot(a, b, preferred_element_type=F32)


def _nt(a, b):
    return lax.dot_general(a, b, (((1,), (1,)), ((), ())), preferred_element_type=F32)


def _tn(a, b):
    return lax.dot_general(a, b, (((0,), (0,)), ((), ())), preferred_element_type=F32)


def _rms(x):
    r = lax.rsqrt(jnp.mean(x * x, axis=-1, keepdims=True) + EPS)
    return x * r, r


def _rms_bwd(dxn, xhat, r, g):
    dxh = dxn * g
    dx = r * (dxh - xhat * jnp.mean(dxh * xhat, axis=-1, keepdims=True))
    return dx, jnp.sum(dxn * xhat, axis=0, keepdims=True)


def _gelu(x):
    cdf = 0.5 * (1.0 + lax.erf(x * (2.0 ** -0.5)))
    return x * cdf, cdf


def _gelu_grad(x, cdf):
    return cdf + x * jnp.exp(-0.5 * x * x) * ((2.0 * math.pi) ** -0.5)


def _params(sem=None):
    return pltpu.CompilerParams(dimension_semantics=sem, vmem_limit_bytes=VMEM_LIMIT)


def _token_block(t, want):
    return want if t % want == 0 and t > want else GROUP


def _const_spec(shape):
    n = len(shape)
    return pl.BlockSpec(shape, lambda i: (0,) * n)


def _tril():
    return lax.broadcasted_iota(jnp.int32, (GROUP, GROUP), 0) >= lax.broadcasted_iota(jnp.int32, (GROUP, GROUP), 1)


def _shift_rows(x, k, edge):
    tb = x.shape[0]
    r8 = lax.broadcasted_iota(jnp.int32, (8, 1), 0)
    rolled = pltpu.roll(x, k % tb, 0)
    if k > 0:
        top = jnp.where(r8 < k, pltpu.roll(edge, k, 0), rolled[0:8, :])
        return jnp.concatenate([top, rolled[8:, :]], axis=0)
    bottom = jnp.where(r8 >= 8 + k, pltpu.roll(edge, 8 + k, 0), rolled[tb - 8:, :])
    return jnp.concatenate([rolled[:tb - 8, :], bottom], axis=0)


def _in_turns(parts):
    parts = list(parts)
    while parts:
        for p in list(parts):
            try:
                next(p)
            except StopIteration:
                parts.remove(p)


class _Hosted(NamedTuple):
    operands: tuple
    aliased: bool
    out_shapes: tuple
    sems: tuple
    stages: tuple


def _hosted_results(hosted):
    if hosted.aliased:
        return [jax.ShapeDtypeStruct(o.shape, o.dtype) for o in hosted.operands]
    return list(hosted.out_shapes)


def _call_hosting(main_body, hosted, *, name, steps, in_specs, out_specs, out_shape, scratch_shapes, operands, aliases=None, middle_at=0.75):
    grid = steps if isinstance(steps, tuple) else (steps,)
    semantics = ("arbitrary",) * len(grid)
    hosted = [hs for hs in (hosted or ()) if hs is not None]
    if not hosted:
        outs = pl.pallas_call(main_body, name=name, grid=grid, in_specs=in_specs, out_specs=out_specs, out_shape=out_shape,
                              scratch_shapes=scratch_shapes, input_output_aliases=aliases or {}, compiler_params=_params(semantics))(*operands)
        return outs, ()
    n_in, n_out, n_sc = len(in_specs), len(out_specs), len(scratch_shapes)
    shapes = [_hosted_results(hs) for hs in hosted]
    aliases, in_at, out_at = dict(aliases or {}), n_in, n_out
    for hs, sh in zip(hosted, shapes):
        if hs.aliased:
            aliases.update({in_at + i: out_at + i for i in range(len(hs.operands))})
        in_at += len(hs.operands)
        out_at += len(sh)

    def body(*refs):
        at = [0]

        def take(n):
            at[0] += n
            return refs[at[0] - n:at[0]]

        ins = take(n_in)
        h_in = [take(len(hs.operands)) for hs in hosted]
        outs = take(n_out)
        h_out = [take(len(sh)) for sh in shapes]
        scratch = take(n_sc)
        h_sems = [take(len(hs.sems)) for hs in hosted]
        ids = [pl.program_id(a) for a in range(len(grid))]

        def at_step(where):
            lead, rest = where
            ok = ids[0] == lead
            for a in range(1, len(grid)):
                ok = jnp.logical_and(ok, ids[a] == (grid[a] - 1 if rest else 0))
            return ok

        def run(stage):
            for hs, a, b, c in zip(hosted, h_in, h_out, h_sems):
                if hs.stages[stage] is not None:
                    hs.stages[stage](a, b, c)

        @pl.when(at_step((0, 0)))
        def _():
            run(0)

        if any(hs.stages[1] is not None for hs in hosted):
            @pl.when(at_step((min(int(middle_at * grid[0]), grid[0] - 1), 0)))
            def _():
                run(1)

        main_body(*ins, *outs, *scratch)

        @pl.when(at_step((grid[0] - 1, -1)))
        def _():
            run(2)

    flat = lambda lists: [x for xs in lists for x in xs]
    outs = pl.pallas_call(
        body, name=name, grid=grid, in_specs=list(in_specs) + [HB] * (in_at - n_in), out_specs=list(out_specs) + [HB] * (out_at - n_out),
        out_shape=list(out_shape) + flat(shapes), scratch_shapes=list(scratch_shapes) + flat(hs.sems for hs in hosted),
        input_output_aliases=aliases, compiler_params=_params(semantics),
    )(*operands, *flat(hs.operands for hs in hosted))
    results, at = [], n_out
    for sh in shapes:
        results.append(outs[at:at + len(sh)])
        at += len(sh)
    return outs[:n_out], results


def _run_hosted(hosted, name):
    nh = len(hosted.operands)
    h_shapes = _hosted_results(hosted)

    def body(*refs):
        h_in, h_out, h_sems = refs[:nh], refs[nh:nh + len(h_shapes)], refs[nh + len(h_shapes):]
        for stage in hosted.stages:
            if stage is not None:
                stage(h_in, h_out, h_sems)

    return pl.pallas_call(
        body, name=name, in_specs=[HB] * nh, out_specs=[HB] * len(h_shapes), out_shape=h_shapes, scratch_shapes=list(hosted.sems),
        input_output_aliases={i: i for i in range(nh)} if hosted.aliased else {},
        compiler_params=pltpu.CompilerParams(has_side_effects=True),
    )(*hosted.operands)


def _window_sums(e, win, back):
    n = e.shape[0]
    k = 1
    while k < win:
        e = e + pltpu.roll(e, k if back else n - k, 0)
        k *= 2
    return e


def _pool_diff(prev, p, t0, gi, win):
    sl = slice(gi * GROUP, (gi + 1) * GROUP)
    tb = p.shape[0]
    s = _window_sums(jnp.concatenate([prev[:, sl], p[:, sl]], axis=0), win, True)[HALO:, :]
    tglob = t0 + lax.broadcasted_iota(jnp.int32, (tb, 1), 0)
    cnt = jnp.minimum(tglob + 1, win).astype(F32)
    return s / cnt - p[:, sl], cnt


def _layernorm(v):
    xc = v - jnp.mean(v, axis=-1, keepdims=True)
    rstd = lax.rsqrt(jnp.mean(xc * xc, axis=-1, keepdims=True) + EPS)
    return xc * rstd, rstd


def _mixer_fwd(h, g, w_in, pool_w, pool_scale, sgu_g, sgu_w, sgu_bt, w_out, hosted=None):
    t, d = h.shape
    pw = pool_w.shape[0] * GROUP
    sw = sgu_w.shape[0] * GROUP
    tb = _token_block(t, 512)

    def body(h_ref, g_ref, win_ref, pw_ref, ps_ref, sg_ref, sw_ref, sbt_ref, wout_ref, h1_ref, proj_ref, xn_ref, mix_ref, pext):
        i = pl.program_id(0)

        @pl.when(i == 0)
        def _():
            pext[...] = jnp.zeros((HALO, pw), F32)

        x = h_ref[...]
        xhat, _ = _rms(x)
        xn = (xhat * g_ref[...]).astype(_MXU)
        xn_ref[...] = xn
        proj = _nn(xn, win_ref[...])
        proj_ref[...] = proj
        p = proj[:, :pw]
        prev = pext[...]
        for gi, win in enumerate(WINDOWS):
            sl = slice(gi * GROUP, (gi + 1) * GROUP)
            dg, _ = _pool_diff(prev, p, i * tb, gi, win)
            e = _nn(dg.astype(_MXU), pw_ref[gi].astype(_MXU))
            mix_ref[:, sl] = (e * ps_ref[:, sl]).astype(_MXU)
        pext[...] = p[tb - HALO:tb, :]
        uv, _ = _gelu(proj[:, pw:])
        u = uv[:, :sw]
        vhat, _ = _layernorm(uv[:, sw:])
        vn = (vhat * sg_ref[...]).astype(_MXU)
        mask = _tril()
        chunks = [slice(n * GROUP, (n + 1) * GROUP) for n in range(tb // GROUP)]
        for hh in range(sw // GROUP):
            wm = jnp.where(mask, sw_ref[hh], 0.0).astype(_MXU)
            cols = slice(hh * GROUP, (hh + 1) * GROUP)
            z = _nn(wm, jnp.concatenate([vn[rows, cols] for rows in chunks], axis=1))
            for n, rows in enumerate(chunks):
                mix_ref[rows, pw + hh * GROUP:pw + (hh + 1) * GROUP] = (u[rows, cols] * (z[:, chunks[n]] + sbt_ref[hh])).astype(_MXU)
        h1_ref[...] = x + _nn(mix_ref[...], wout_ref[...])

    blk = lambda w: pl.BlockSpec((tb, w), lambda i: (i, 0))
    return _call_hosting(
        body, hosted, name="mixer_fwd", steps=t // tb,
        in_specs=[blk(d), VM, VM, VM, VM, VM, VM, VM, VM],
        out_specs=[blk(d), blk(w_in.shape[1]), blk(d), blk(d)],
        out_shape=[jax.ShapeDtypeStruct((t, d), F32), jax.ShapeDtypeStruct((t, w_in.shape[1]), F32),
                   jax.ShapeDtypeStruct((t, d), _MXU), jax.ShapeDtypeStruct((t, d), _MXU)],
        scratch_shapes=[pltpu.VMEM((HALO, pw), F32)],
        operands=(h, g, w_in, pool_w, pool_scale, sgu_g, sgu_w, sgu_bt, w_out), middle_at=LATE_MIDDLE)


def _mixer_bwd(dh1, h, proj, g, w_in, pool_w, pool_scale, sgu_g, sgu_w, sgu_bt, w_out, hosted=None):
    t, d = h.shape
    ng, nh = pool_w.shape[0], sgu_w.shape[0]
    pw, sw = ng * GROUP, nh * GROUP
    tb = _token_block(t, 512)
    nb = t // tb
    n_parts = 2 if tb % (2 * GROUP) == 0 else 1
    pt = tb // n_parts

    def body(dh1_ref, h_ref, proj_ref, halo_ref, g_ref, win_ref, pw_ref, ps_ref, sg_ref, sw_ref, sbt_ref, wout_ref,
             dh_ref, dproj_ref, gg_ref, gpw_ref, gps_ref, gsg_ref, gsw_ref, gsbt_ref, dext, duv):
        i = pl.program_id(0)
        blk = nb - 1 - i

        @pl.when(i == 0)
        def _():
            for r in (gg_ref, gpw_ref, gps_ref, gsg_ref, gsw_ref, gsbt_ref, dext):
                r[...] = jnp.zeros(r.shape, F32)

        mask = _tril()

        def part(at):
            rows = slice(at, at + pt)
            dh1v = dh1_ref[rows, :]
            dmix = _nt(dh1v.astype(_MXU), wout_ref[...])
            yield
            proj_v = proj_ref[rows, :]
            p = proj_v[:, :pw]
            prev = jnp.where(blk == 0, 0.0, halo_ref[...]) if at == 0 else proj_ref[at - HALO:at, 0:pw]
            for gi, win in enumerate(WINDOWS):
                sl = slice(gi * GROUP, (gi + 1) * GROUP)
                dg, cnt = _pool_diff(prev, p, blk * tb + at, gi, win)
                dgm = dg.astype(_MXU)
                pwm = pw_ref[gi].astype(_MXU)
                e = _nn(dgm, pwm)
                dy = dmix[:, sl]
                gps_ref[:, sl] += jnp.sum(dy * e, axis=0, keepdims=True)
                de = (dy * ps_ref[:, sl]).astype(_MXU)
                gpw_ref[gi] += _tn(dgm, de)
                dd = _nt(de, pwm)
                ddc = dd / cnt
                acc = _window_sums(jnp.concatenate([ddc, dext[:, sl]], axis=0), win, False)[:pt, :]
                dext[:, sl] = ddc[0:HALO, :]
                dproj_ref[rows, sl] = (acc - dd).astype(_MXU)
            yield
            pre = proj_v[:, pw:]
            uv, cdf = _gelu(pre)
            u = uv[:, :sw]
            vhat, rstd = _layernorm(uv[:, sw:])
            vn = (vhat * sg_ref[...]).astype(_MXU)
            chunks = [slice(n * GROUP, (n + 1) * GROUP) for n in range(pt // GROUP)]
            side_by_side = lambda a, cols: jnp.concatenate([a[c, cols] for c in chunks], axis=1)
            for hh in range(nh):
                wm = jnp.where(mask, sw_ref[hh], 0.0).astype(_MXU)
                cols = slice(hh * GROUP, (hh + 1) * GROUP)
                vs = side_by_side(vn, cols)
                z = _nn(wm, vs)
                dy = side_by_side(dmix, slice(pw + hh * GROUP, pw + (hh + 1) * GROUP))
                dz = dy * side_by_side(u, cols)
                dzm = dz.astype(_MXU)
                dvs = _tn(wm, dzm)
                gsw_ref[hh] += jnp.where(mask, _nt(dzm, vs), 0.0)
                gb = jnp.zeros((GROUP, GROUP), F32)
                for n, c in enumerate(chunks):
                    gb = gb + dz[:, c]
                    duv[at + n * GROUP:at + (n + 1) * GROUP, cols] = dy[:, c] * (z[:, c] + sbt_ref[hh])
                    duv[at + n * GROUP:at + (n + 1) * GROUP, sw + hh * GROUP:sw + (hh + 1) * GROUP] = dvs[:, c]
                gsbt_ref[hh] += gb
            yield
            dvn = duv[rows, sw:]
            gsg_ref[...] += jnp.sum(dvn * vhat, axis=0, keepdims=True)
            dxh = dvn * sg_ref[...]
            dv = rstd * (dxh - jnp.mean(dxh, axis=-1, keepdims=True) - vhat * jnp.mean(dxh * vhat, axis=-1, keepdims=True))
            gp = _gelu_grad(pre, cdf)
            dproj_ref[rows, pw:pw + sw] = (duv[rows, :sw] * gp[:, :sw]).astype(_MXU)
            dproj_ref[rows, pw + sw:] = (dv * gp[:, sw:]).astype(_MXU)
            dxn = _nt(dproj_ref[rows, :], win_ref[...])
            yield
            xhat, r = _rms(h_ref[rows, :])
            dx, gg = _rms_bwd(dxn, xhat, r, g_ref[...])
            gg_ref[...] += gg
            dh_ref[rows, :] = dh1v + dx

        _in_turns([part(at) for at in reversed(range(0, tb, pt))])

    rev = lambda w: pl.BlockSpec((tb, w), lambda i: (nb - 1 - i, 0))
    halo = pl.BlockSpec((HALO, pw), lambda i: (jnp.maximum((nb - 1 - i) * (tb // HALO) - 1, 0), 0))
    small = [(1, d), (ng, GROUP, GROUP), (1, pw), (1, sw), (nh, GROUP, GROUP), (nh, GROUP, GROUP)]
    return _call_hosting(
        body, hosted, name="mixer_bwd", steps=nb,
        in_specs=[rev(d), rev(d), rev(proj.shape[1]), halo, VM, VM, VM, VM, VM, VM, VM, VM],
        out_specs=[rev(d), rev(proj.shape[1])] + [_const_spec(s) for s in small],
        out_shape=[jax.ShapeDtypeStruct((t, d), F32), jax.ShapeDtypeStruct(proj.shape, _MXU)]
        + [jax.ShapeDtypeStruct(s, F32) for s in small],
        scratch_shapes=[pltpu.VMEM((HALO, pw), F32), pltpu.VMEM((tb, 2 * sw), F32)],
        operands=(dh1, h, proj, proj, g, w_in, pool_w, pool_scale, sgu_g, sgu_w, sgu_bt, w_out))


def _kv_fwd(mem, gm, wk, wv):
    n, d = mem.shape

    def body(mem_ref, gm_ref, wk_ref, wv_ref, k_ref, v_ref, memn_ref):
        xhat, _ = _rms(mem_ref[...])
        memn = (xhat * gm_ref[...]).astype(_MXU)
        memn_ref[...] = memn
        k_ref[...] = _nn(memn, wk_ref[...]).astype(_MXU)
        v_ref[...] = _nn(memn, wv_ref[...]).astype(_MXU)

    return pl.pallas_call(
        body, name="kv_fwd", in_specs=[VM] * 4, out_specs=[VM] * 3,
        out_shape=[jax.ShapeDtypeStruct((n, d), _MXU)] * 3, compiler_params=_params(),
    )(mem, gm, wk, wv)


def _kv_bwd(dk, dv, mem, wk, wv):
    n, d = mem.shape

    def body(dk_ref, dv_ref, mem_ref, wk_ref, wv_ref, ggm_ref):
        dmemn = _nt(dk_ref[...].astype(_MXU), wk_ref[...]) + _nt(dv_ref[...].astype(_MXU), wv_ref[...])
        xhat, _ = _rms(mem_ref[...])
        ggm_ref[...] = jnp.sum(dmemn * xhat, axis=0, keepdims=True)

    return pl.pallas_call(
        body, name="kv_bwd", in_specs=[VM] * 5, out_specs=VM,
        out_shape=jax.ShapeDtypeStruct((1, d), F32), compiler_params=_params(),
    )(dk, dv, mem, wk, wv)


def _softmax(s):
    e = jnp.exp(s - jnp.max(s, axis=-1, keepdims=True))
    return e / jnp.sum(e, axis=-1, keepdims=True)


def _one_ahead(n, issue):
    nxt = issue(0)
    for a in range(n):
        cur = nxt
        if a + 1 < n:
            nxt = issue(a + 1)
        yield a, cur


def _xattn_fwd(h, g, wq, k, v, wo, hosted=None):
    t, d = h.shape
    hd = d // N_XHEADS
    scale = hd ** -0.5
    tb = _token_block(t, 512)

    def body(h_ref, g_ref, wq_ref, k_ref, v_ref, wo_ref, h2_ref, q_ref, o_ref, xn_ref):
        x = h_ref[...]
        xhat, _ = _rms(x)
        xn = (xhat * g_ref[...]).astype(_MXU)
        xn_ref[...] = xn
        qm = _nn(xn, wq_ref[...]).astype(_MXU)
        q_ref[...] = qm
        heads = [slice(a * hd, (a + 1) * hd) for a in range(N_XHEADS)]
        for a, s in _one_ahead(N_XHEADS, lambda a: _nt(qm[:, heads[a]], k_ref[:, heads[a]]) * scale):
            o_ref[:, heads[a]] = _nn(_softmax(s).astype(_MXU), v_ref[:, heads[a]]).astype(_MXU)
        h2_ref[...] = x + _nn(o_ref[...], wo_ref[...])

    blk = pl.BlockSpec((tb, d), lambda i: (i, 0))
    return _call_hosting(
        body, hosted, name="xattn_fwd", steps=t // tb,
        in_specs=[blk, VM, VM, VM, VM, VM], out_specs=[blk] * 4,
        out_shape=[jax.ShapeDtypeStruct((t, d), F32)] + [jax.ShapeDtypeStruct((t, d), _MXU)] * 3,
        scratch_shapes=[], operands=(h, g, wq, k, v, wo), middle_at=LATE_MIDDLE)


def _xattn_bwd(dh2, h, q, g, wq, k, v, wo, hosted=None):
    t, d = h.shape
    n = k.shape[0]
    hd = d // N_XHEADS
    scale = hd ** -0.5
    tb = _token_block(t, 512)
    pt = tb // 2 if tb % (2 * GROUP) == 0 else tb

    def body(dh2_ref, h_ref, q_ref, g_ref, wq_ref, k_ref, v_ref, wo_ref, dh_ref, dq_ref, dk_ref, dv_ref, gg_ref):
        @pl.when(pl.program_id(0) == 0)
        def _():
            for r in (dk_ref, dv_ref, gg_ref):
                r[...] = jnp.zeros(r.shape, F32)

        heads = [slice(a * hd, (a + 1) * hd) for a in range(N_XHEADS)]

        def part(at):
            rows = slice(at, at + pt)
            dh2v = dh2_ref[rows, :]
            dom = _nt(dh2v.astype(_MXU), wo_ref[...]).astype(_MXU)
            yield
            issue = lambda a: (_nt(q_ref[rows, heads[a]], k_ref[:, heads[a]]) * scale, _nt(dom[:, heads[a]], v_ref[:, heads[a]]))
            for a, (s, dpr) in _one_ahead(N_XHEADS, issue):
                sl = heads[a]
                pr = _softmax(s)
                dv_ref[:, sl] += _tn(pr.astype(_MXU), dom[:, sl])
                ds = (pr * (dpr - jnp.sum(dpr * pr, axis=-1, keepdims=True)) * scale).astype(_MXU)
                dq_ref[rows, sl] = _nn(ds, k_ref[:, sl]).astype(_MXU)
                dk_ref[:, sl] += _tn(ds, q_ref[rows, sl])
                yield
            dxn = _nt(dq_ref[rows, :], wq_ref[...])
            yield
            xhat, r = _rms(h_ref[rows, :])
            dx, gg = _rms_bwd(dxn, xhat, r, g_ref[...])
            gg_ref[...] += gg
            dh_ref[rows, :] = dh2v + dx

        _in_turns([part(at) for at in range(0, tb, pt)])

    blk = pl.BlockSpec((tb, d), lambda i: (i, 0))
    return _call_hosting(
        body, hosted, name="xattn_bwd", steps=t // tb,
        in_specs=[blk, blk, blk, VM, VM, VM, VM, VM],
        out_specs=[blk, blk, _const_spec((n, d)), _const_spec((n, d)), _const_spec((1, d))],
        out_shape=[jax.ShapeDtypeStruct((t, d), F32), jax.ShapeDtypeStruct((t, d), _MXU),
                   jax.ShapeDtypeStruct((n, d), F32), jax.ShapeDtypeStruct((n, d), F32), jax.ShapeDtypeStruct((1, d), F32)],
        scratch_shapes=[], operands=(dh2, h, q, g, wq, k, v, wo))


def _ffn_fwd(h, g, w_up, conv_w, conv_b, w_down, hosted=None, head=None):
    t, d = h.shape
    f = w_down.shape[0]
    ft = FF_TILE
    tb = _token_block(t, 256)

    def body(h_ref, g_ref, wup_ref, cw_ref, cb_ref, wdown_ref, *rest):
        if head is None:
            h3_ref, hh_ref, hc_ref, ext, carry, act_sc = rest
        else:
            gf_ref, tgt_ref, h3_ref, hh_ref, hc_ref, loss_ref, ggf_ref, ext, carry, act_sc = rest

        @pl.when(pl.program_id(0) == 0)
        def _():
            carry[...] = jnp.zeros(carry.shape, F32)
            if head is not None:
                loss_ref[...] = jnp.zeros(loss_ref.shape, F32)
                ggf_ref[...] = jnp.zeros(ggf_ref.shape, F32)

        x = h_ref[...]
        xhat, _ = _rms(x)
        xn = (xhat * g_ref[...]).astype(_MXU)
        acc = jnp.zeros((tb, d), F32)
        up = lambda j: [_nn(xn, wup_ref[:, off:off + ft]) for off in (j * ft, f + j * ft)]
        up_next = up(0)
        for j in range(f // ft):
            hc = []
            up_cur = up_next
            if j + 1 < f // ft:
                up_next = up(j + 1)
            for part, off in enumerate((j * ft, f + j * ft)):
                cols = slice(off, off + ft)
                cur = up_cur[part]
                hh_ref[:, cols] = cur.astype(_MXU)
                ext[part, 0:8, :] = carry[:, cols]
                ext[part, 8:8 + tb, :] = cur
                carry[:, cols] = cur[tb - 8:tb, :]
                hc.append(cb_ref[:, cols] + cw_ref[0:1, cols] * ext[part, 6:6 + tb, :]
                          + cw_ref[1:2, cols] * ext[part, 7:7 + tb, :] + cw_ref[2:3, cols] * cur)
                hc_ref[:, cols] = hc[part].astype(_MXU)
            at = j % DOWN_TILES
            act_sc[:, at * ft:(at + 1) * ft] = (hc[0] * jax.nn.sigmoid(hc[0]) * hc[1]).astype(_MXU)
            if at + 1 == DOWN_TILES or j + 1 == f // ft:
                acc = acc + _nn(act_sc[:, 0:(at + 1) * ft], wdown_ref[(j - at) * ft:(j + 1) * ft, :])
        if head is None:
            h3_ref[...] = x + acc
        else:
            yhat, r = _rms(x + acc)
            err = yhat * gf_ref[...] - tgt_ref[...]
            loss_ref[...] += 0.5 * jnp.sum(jnp.sum(err * err, axis=-1, keepdims=True), axis=0, keepdims=True) / d
            dx, gg = _rms_bwd(err / d, yhat, r, gf_ref[...])
            ggf_ref[...] += gg
            h3_ref[...] = dx

    blk = lambda w: pl.BlockSpec((tb, w), lambda i: (i, 0))
    in_specs = [blk(d), VM, VM, VM, VM, VM]
    out_specs = [blk(d), blk(2 * f), blk(2 * f)]
    out_shape = [jax.ShapeDtypeStruct((t, d), F32), jax.ShapeDtypeStruct((t, 2 * f), _MXU), jax.ShapeDtypeStruct((t, 2 * f), _MXU)]
    operands = (h, g, w_up, conv_w, conv_b, w_down)
    if head is not None:
        in_specs += [VM, blk(d)]
        out_specs += [_const_spec((1, 1)), _const_spec((1, d))]
        out_shape += [jax.ShapeDtypeStruct((1, 1), F32), jax.ShapeDtypeStruct((1, d), F32)]
        operands += tuple(head)
    return _call_hosting(
        body, hosted, name="ffn_fwd", steps=t // tb, in_specs=in_specs, out_specs=out_specs, out_shape=out_shape,
        scratch_shapes=[pltpu.VMEM((2, 8 + tb, ft), F32), pltpu.VMEM((8, 2 * f), F32), pltpu.VMEM((tb, DOWN_TILES * ft), _MXU)],
        operands=operands)


def _ffn_bwd(dh3, h, hh, hc, g, w_up, conv_w, w_down, hosted=None):
    t, d = h.shape
    f = w_down.shape[0]
    ft = FF_TILE
    tb = _token_block(t, 256)
    nb = t // tb

    def body(dh3_ref, h_ref, hh_ref, hc_ref, g_ref, wup_ref, cw_ref, wdown_ref,
             dh_ref, dhh_ref, act_ref, xn_ref, gcw_ref, gcb_ref, gg_ref, dcarry):
        @pl.when(pl.program_id(0) == 0)
        def _():
            for r in (gcw_ref, gcb_ref, gg_ref, dcarry):
                r[...] = jnp.zeros(r.shape, F32)

        dh3v = dh3_ref[...]
        dhm = dh3v.astype(_MXU)
        dxn = jnp.zeros((tb, d), F32)
        dact_next = _nt(dhm, wdown_ref[0:ft, :])
        for j in range(f // ft):
            dact = dact_next
            if j + 1 < f // ft:
                dact_next = _nt(dhm, wdown_ref[(j + 1) * ft:(j + 2) * ft, :])
            gate = hc_ref[:, j * ft:(j + 1) * ft].astype(F32)
            val = hc_ref[:, f + j * ft:f + (j + 1) * ft].astype(F32)
            sg = jax.nn.sigmoid(gate)
            silu = gate * sg
            act_ref[:, j * ft:(j + 1) * ft] = (silu * val).astype(_MXU)
            dhc = (dact * val * sg * (1.0 + gate * (1.0 - sg)), dact * silu)
            for part, off in enumerate((j * ft, f + j * ft)):
                cols = slice(off, off + ft)
                dc = dhc[part]
                c0 = hh_ref[:, cols].astype(F32)
                after = dcarry[:, cols]
                ahead1 = _shift_rows(dc, -1, after)
                ahead2 = _shift_rows(dc, -2, after)
                dcarry[:, cols] = dc[0:8, :]
                gcb_ref[:, cols] += jnp.sum(dc, axis=0, keepdims=True)
                gcw_ref[0:1, cols] += jnp.sum(ahead2 * c0, axis=0, keepdims=True)
                gcw_ref[1:2, cols] += jnp.sum(ahead1 * c0, axis=0, keepdims=True)
                gcw_ref[2:3, cols] += jnp.sum(dc * c0, axis=0, keepdims=True)
                dhh = (cw_ref[2:3, cols] * dc + cw_ref[1:2, cols] * ahead1 + cw_ref[0:1, cols] * ahead2).astype(_MXU)
                dhh_ref[:, cols] = dhh
                dxn = dxn + _nt(dhh, wup_ref[:, cols])
        xhat, r = _rms(h_ref[...])
        xn_ref[...] = (xhat * g_ref[...]).astype(_MXU)
        dx, gg = _rms_bwd(dxn, xhat, r, g_ref[...])
        gg_ref[...] += gg
        dh_ref[...] = dh3v + dx

    rev = lambda w: pl.BlockSpec((tb, w), lambda i: (nb - 1 - i, 0))
    return _call_hosting(
        body, hosted, name="ffn_bwd", steps=nb,
        in_specs=[rev(d), rev(d), rev(2 * f), rev(2 * f), VM, VM, VM, VM],
        out_specs=[rev(d), rev(2 * f), rev(f), rev(d), _const_spec((3, 2 * f)), _const_spec((1, 2 * f)), _const_spec((1, d))],
        out_shape=[jax.ShapeDtypeStruct((t, d), F32), jax.ShapeDtypeStruct((t, 2 * f), _MXU), jax.ShapeDtypeStruct((t, f), _MXU),
                   jax.ShapeDtypeStruct((t, d), _MXU),
                   jax.ShapeDtypeStruct((3, 2 * f), F32), jax.ShapeDtypeStruct((1, 2 * f), F32), jax.ShapeDtypeStruct((1, d), F32)],
        scratch_shapes=[pltpu.VMEM((8, 2 * f), F32)],
        operands=(dh3, h, hh, hc, g, w_up, conv_w, w_down))


def _largest_tile(n, cap, mult=128):
    best = None
    for c in range(mult, min(n, cap) + 1, mult):
        if n % c == 0:
            best = c
    return best if best is not None else n


def _grad_matmul(a, b, name, layer, n_layers, into=None, hosted=None):
    t, m = a.shape
    n = b.shape[1]
    tm, tn, tk = _largest_tile(m, 1408), _largest_tile(n, 1536), _largest_tile(t, 1024)
    nk = t // tk

    def body(a_ref, b_ref, *rest):
        o_ref = rest[-1]

        @pl.when(pl.program_id(2) == 0)
        def _():
            o_ref[...] = jnp.zeros(o_ref.shape, F32)

        o_ref[...] += _tn(a_ref[...].astype(_MXU), b_ref[...].astype(_MXU))

    in_specs = [pl.BlockSpec((tk, tm), lambda i, j, k: (k, i)), pl.BlockSpec((tk, tn), lambda i, j, k: (k, j))]
    operands = (a, b)
    aliases = {}
    if into is not None:
        in_specs.append(pl.BlockSpec(memory_space=pl.ANY))
        operands = (a, b, into)
        aliases = {2: 0}
    (out,), got = _call_hosting(
        body, hosted, name=name, steps=(m // tm, n // tn, nk), in_specs=in_specs,
        out_specs=[pl.BlockSpec((None, tm, tn), lambda i, j, k: (layer, i, j))],
        out_shape=[jax.ShapeDtypeStruct((n_layers, m, n), F32)], scratch_shapes=[], operands=operands, aliases=aliases)
    return out, got


def _adamw_math(w, g, m, v):
    m = ADAM_B1 * m + (1.0 - ADAM_B1) * g
    v = ADAM_B2 * v + (1.0 - ADAM_B2) * (g * g)
    m_hat = m / (1.0 - ADAM_B1 ** ADAM_STEP)
    v_hat = v / (1.0 - ADAM_B2 ** ADAM_STEP)
    return -ADAM_LR * (m_hat / (jnp.sqrt(v_hat) + ADAM_EPS) + ADAM_WD * w), m, v


def _row_block(rows, cols, max_bytes=1 << 20, mult=16):
    best = None
    for r in range(mult, rows + 1, mult):
        if rows % r == 0 and r * cols * 4 <= max_bytes:
            best = r
    return best if best is not None else rows


def _adamw_big(ws, gs, ms, vs, name):
    n = len(ws)
    shape = ws[0].shape
    cols = shape[-1]
    flat = lambda a: a.reshape(-1, cols)
    rows = flat(ws[0]).shape[0]
    rb = _row_block(rows, cols, (2 << 20) // n)

    def body(*refs):
        for a in range(n):
            w_ref, g_ref, m_ref, v_ref = (refs[s * n + a] for s in range(4))
            go_ref, d_ref, nm_ref, nv_ref = (refs[(4 + s) * n + a] for s in range(4))
            g = g_ref[...]
            go_ref[...] = g
            d_ref[...], nm_ref[...], nv_ref[...] = _adamw_math(w_ref[...], g, m_ref[...], v_ref[...])

    blk = pl.BlockSpec((rb, cols), lambda i: (i, 0))
    outs = pl.pallas_call(
        body, name=name, grid=(rows // rb,), in_specs=[blk] * (4 * n), out_specs=[blk] * (4 * n),
        out_shape=[jax.ShapeDtypeStruct((rows, cols), F32)] * (4 * n), compiler_params=_params(("parallel",)),
    )(*[flat(a) for group in (ws, gs, ms, vs) for a in group])
    return [[outs[s * n + a].reshape(shape) for s in range(4)] for a in range(n)]


def _adamw_small(ws, gs, ms, vs):
    n = len(ws)

    def body(*refs):
        for a in range(n):
            w_ref, g_ref, m_ref, v_ref = (refs[s * n + a] for s in range(4))
            d_ref, nm_ref, nv_ref = (refs[(4 + s) * n + a] for s in range(3))
            d_ref[...], nm_ref[...], nv_ref[...] = _adamw_math(w_ref[...], g_ref[...], m_ref[...], v_ref[...])

    outs = pl.pallas_call(
        body, name="adamw_small", in_specs=[VM] * (4 * n), out_specs=[VM] * (3 * n),
        out_shape=[jax.ShapeDtypeStruct(w.shape, F32) for w in ws] * 3, compiler_params=_params(),
    )(*ws, *gs, *ms, *vs)
    return outs[:n], outs[n:2 * n], outs[2 * n:]


def _place():
    x, y, c = lax.axis_index("x"), lax.axis_index("y"), lax.axis_index("c")
    chips = [(1 - x, y), (x, 1 - y), (1 - x, 1 - y)]
    return x, y, c, chips


def _rows(start, size, mult=16):
    return pl.ds(pl.multiple_of(start, mult), size)


def _full_window(ref, axis, chip, half=None):
    r, c = ref.shape
    if axis == 0:
        rs = r // 4
        if half is None:
            return ref.at[_rows(chip * rs, rs), :]
        return ref.at[_rows(chip * rs + half * (rs // 2), rs // 2), :]
    cs = c // 4
    if half is None:
        return ref.at[:, _rows(chip * cs, cs, 128)]
    return ref.at[_rows(half * (r // 2), r // 2), _rows(chip * cs, cs, 128)]


def _remote(src, dst, send_sem, recv_sem, to):
    return pltpu.make_async_remote_copy(src_ref=src, dst_ref=dst, send_sem=send_sem, recv_sem=recv_sem,
                                        device_id=to, device_id_type=MESH)


def _scalars(*vals):
    return jnp.stack([jnp.asarray(v, jnp.int32) for v in vals])


def _cast_place(shards, axis, chip, name):
    n = len(shards)
    nl, rs, cs = shards[0].shape
    full = (rs * 4, cs) if axis == 0 else (rs, cs * 4)
    rb = _row_block(rs, cs, (4 << 20) // (n * nl))
    nrb = rs // rb

    def body(chip_ref, *refs):
        for a in range(n):
            for l in range(nl):
                refs[n + a * nl + l][...] = refs[a][l].astype(_PAY)

    if axis == 0:
        out_map = lambda i, chip_ref: (chip_ref[0] * nrb + i, 0)
    else:
        out_map = lambda i, chip_ref: (i, chip_ref[0])
    outs = pl.pallas_call(
        body, name=name,
        grid_spec=pltpu.PrefetchScalarGridSpec(
            num_scalar_prefetch=1, grid=(nrb,),
            in_specs=[pl.BlockSpec((nl, rb, cs), lambda i, chip_ref: (0, i, 0))] * n,
            out_specs=[pl.BlockSpec((rb, cs), out_map)] * (n * nl)),
        out_shape=[jax.ShapeDtypeStruct(full, _PAY)] * (n * nl), compiler_params=_params(("parallel",)),
    )(_scalars(chip), *shards)
    return [[outs[a * nl + l] for l in range(nl)] for a in range(n)]


def _hosted_allgather(placed, axes):
    n = len(placed)

    def each(outs, half_of):
        x, y, c, chips = _place()
        for i in range(n):
            for k, chip in enumerate(chips):
                yield i * 3 + k, (*chip, c), (x, y, 1 - c), _full_window(outs[i], axes[i], 2 * x + y, c), \
                    _full_window(outs[i], axes[i], 2 * chip[0] + chip[1], half_of(c))

    def start(_, outs, sems):
        send, recv, _, _ = sems
        for s, peer, _, mine, _ in each(outs, lambda c: c):
            _remote(mine, mine, send.at[s], recv.at[s], peer).start()

    def middle(_, outs, sems):
        send, recv, fsend, frecv = sems
        for s, _, sibling, _, got in each(outs, lambda c: c):
            _remote(got, got, send.at[s], recv.at[s], sibling).wait_recv()
            _remote(got, got, fsend.at[s], frecv.at[s], sibling).start()

    def finish(_, outs, sems):
        send, recv, fsend, frecv = sems
        for s, _, sibling, _, got in each(outs, lambda c: 1 - c):
            _remote(got, got, fsend.at[s], frecv.at[s], sibling).wait_recv()
        for s, peer, sibling, mine, got in each(outs, lambda c: c):
            _remote(mine, mine, send.at[s], recv.at[s], peer).wait_send()
            _remote(got, got, fsend.at[s], frecv.at[s], sibling).wait_send()

    return _Hosted(tuple(placed), True, (), (pltpu.SemaphoreType.DMA((n * 3,)),) * 4, (start, middle, finish))


def _allgather_conv(conv_shard):
    nl, taps, cs = conv_shard.shape

    def body(in_ref, out_ref, send, recv, local):
        x, y, c, chips = _place()
        mine = out_ref.at[:, :, _rows((2 * x + y) * cs, cs, 128)]
        own = pltpu.make_async_copy(in_ref, mine, local)
        own.start()
        sends = [_remote(in_ref, mine, send.at[k], recv.at[k], (*chip, c)) for k, chip in enumerate(chips)]
        for cp in sends:
            cp.start()
        for k, chip in enumerate(chips):
            got = out_ref.at[:, :, _rows((2 * chip[0] + chip[1]) * cs, cs, 128)]
            _remote(got, got, send.at[k], recv.at[k], (*chip, c)).wait_recv()
        for cp in sends:
            cp.wait_send()
        own.wait()

    return pl.pallas_call(
        body, name="allgather_conv", in_specs=[HB], out_specs=HB, out_shape=jax.ShapeDtypeStruct((nl, taps, cs * 4), conv_shard.dtype),
        scratch_shapes=[pltpu.SemaphoreType.DMA((3,)), pltpu.SemaphoreType.DMA((3,)), pltpu.SemaphoreType.DMA],
        compiler_params=pltpu.CompilerParams(has_side_effects=True),
    )(conv_shard)


def _hosted_exchange(grads, axes, layer):
    na = len(grads)
    views = [g.reshape(g.shape[0], 4, 2, g.shape[1] // 8, g.shape[2]) if ax == 0 else g for g, ax in zip(grads, axes)]

    def region(ref, axis, half):
        if axis == 0:
            return ref.at[layer, :, half]
        r = ref.shape[1]
        return ref.at[layer, _rows(half * (r // 2), r // 2), :]

    def copies(ins, land, sems):
        send, recv = sems
        x, y, c, _ = _place()
        return [_remote(region(ins[a], axes[a], 1 - c), land[a], send.at[a], recv.at[a], (x, y, 1 - c)) for a in range(na)]

    def start(ins, land, sems):
        for cp in copies(ins, land, sems):
            cp.start()

    def finish(ins, land, sems):
        for cp in copies(ins, land, sems):
            cp.wait()

    shapes = [(4, g.shape[1] // 8, g.shape[2]) if ax == 0 else (g.shape[1] // 2, g.shape[2]) for g, ax in zip(grads, axes)]
    return _Hosted(tuple(views), False, tuple(jax.ShapeDtypeStruct(s, F32) for s in shapes),
                   (pltpu.SemaphoreType.DMA((na,)),) * 2, (start, None, finish))


def _add_cast(mines, theirs, core, base, name):
    n = len(mines)
    na, nb, cols = theirs[0].shape
    rb = _row_block(nb, cols, (4 << 20) // n)

    def body(core_ref, *refs):
        for a in range(n):
            refs[2 * n + a][...] = (refs[a][...] + refs[n + a][...]).astype(_PAY)

    blk = pl.BlockSpec((None, rb, cols), lambda i, k, core_ref: (i, k, 0))
    return pl.pallas_call(
        body, name=name,
        grid_spec=pltpu.PrefetchScalarGridSpec(
            num_scalar_prefetch=1, grid=(na, nb // rb),
            in_specs=[pl.BlockSpec((None, None, rb, cols), lambda i, k, core_ref: (base + i, core_ref[0], k, 0))] * n + [blk] * n,
            out_specs=[blk] * n),
        out_shape=[jax.ShapeDtypeStruct((na, nb, cols), _PAY)] * n, compiler_params=_params(("parallel", "parallel")),
    )(_scalars(core), *mines, *theirs)


def _piece(ref, axis, chip):
    if axis == 0:
        return ref.at[chip]
    cs = ref.shape[1] // 4
    return ref.at[:, _rows(chip * cs, cs, 128)]


def _hosted_scatter(sums, axes):
    na = len(sums)

    def piece_shape(a):
        if axes[a] == 0:
            return (sums[a].shape[1], sums[a].shape[2])
        return (sums[a].shape[0], sums[a].shape[1] // 4)

    def copies(ins, slots, sems):
        send, recv = sems
        _, _, c, chips = _place()
        return [_remote(_piece(ins[a], axes[a], 2 * chip[0] + chip[1]), slots[a].at[k], send.at[a * 3 + k], recv.at[a * 3 + k], (*chip, c))
                for a in range(na) for k, chip in enumerate(chips)]

    def start(ins, slots, sems):
        for cp in copies(ins, slots, sems):
            cp.start()

    def finish(ins, slots, sems):
        for cp in copies(ins, slots, sems):
            cp.wait()

    return _Hosted(tuple(sums), False, tuple(jax.ShapeDtypeStruct((3,) + piece_shape(a), sums[a].dtype) for a in range(na)),
                   (pltpu.SemaphoreType.DMA((na * 3,)),) * 2, (start, None, finish))


def _sum_slots(sums, slots, axis, chip, core, layer, n_layers, name, into=None):
    _, hr, cs = slots.shape
    rb = _row_block(hr, cs, 4 << 20)

    def body(at_ref, own_ref, s_ref, *rest):
        rest[-1][...] = ((own_ref[...].astype(F32) + s_ref[0].astype(F32)) + s_ref[1].astype(F32)) + s_ref[2].astype(F32)

    if axis == 0:
        own = pl.BlockSpec((None, rb, cs), lambda k, at_ref: (at_ref[0], k, 0))
    else:
        own = pl.BlockSpec((rb, cs), lambda k, at_ref: (k, at_ref[0]))
    in_specs = [own, pl.BlockSpec((3, rb, cs), lambda k, at_ref: (0, k, 0))]
    operands = (sums, slots)
    aliases = {}
    if into is not None:
        in_specs.append(pl.BlockSpec(memory_space=pl.ANY))
        operands = (sums, slots, into)
        aliases = {3: 0}
    return pl.pallas_call(
        body, name=name,
        grid_spec=pltpu.PrefetchScalarGridSpec(
            num_scalar_prefetch=1, grid=(hr // rb,), in_specs=in_specs,
            out_specs=pl.BlockSpec((None, None, rb, cs), lambda k, at_ref: (layer, at_ref[1], k, 0))),
        out_shape=jax.ShapeDtypeStruct((n_layers, 2, hr, cs), F32), input_output_aliases=aliases,
        compiler_params=_params(("parallel",)),
    )(_scalars(chip, core), *operands)


def _hosted_assemble(shards):
    na = len(shards)

    def copies(_, outs, sems):
        send, recv = sems
        x, y, c, _ = _place()
        halves = [outs[a].at[:, _rows(c * (outs[a].shape[1] // 2), outs[a].shape[1] // 2), :] for a in range(na)]
        return [_remote(mine, mine, send.at[a], recv.at[a], (x, y, 1 - c)) for a, mine in enumerate(halves)]

    def start(ins, outs, sems):
        for cp in copies(ins, outs, sems):
            cp.start()

    def finish(ins, outs, sems):
        for cp in copies(ins, outs, sems):
            cp.wait()

    return _Hosted(tuple(shards), True, (), (pltpu.SemaphoreType.DMA((na,)),) * 2, (start, None, finish))


def _allreduce_small(buf, hosted):
    rows, w = buf.shape
    half = rows // 2
    shapes = [_hosted_results(hs) for hs in hosted]
    flat = lambda lists: [x for xs in lists for x in xs]
    aliases, in_at, out_at = {}, 1, 1
    for hs, sh in zip(hosted, shapes):
        if hs.aliased:
            aliases.update({in_at + i: out_at + i for i in range(len(hs.operands))})
        in_at += len(hs.operands)
        out_at += len(sh)

    def body(buf_ref, *refs):
        at = [0]

        def take(n):
            at[0] += n
            return refs[at[0] - n:at[0]]

        h_in = [take(len(hs.operands)) for hs in hosted]
        (out_ref,) = take(1)
        h_out = [take(len(sh)) for sh in shapes]
        land, slots, red, sems_send, sems_recv = take(5)
        h_sems = [take(len(hs.sems)) for hs in hosted]
        for hs, a, b, s in zip(hosted, h_in, h_out, h_sems):
            hs.stages[0](a, b, s)
        x, y, c, chips = _place()
        me = 2 * x + y
        sibling = (x, y, 1 - c)
        first = _remote(buf_ref, land, sems_send.at[0], sems_recv.at[0], sibling)
        first.start()
        first.wait()
        mine = pl.ds(pl.multiple_of(c * half, 8), half)
        slots[me] = buf_ref[mine, :] + land[mine, :]
        sends = []
        for k, chip in enumerate(chips):
            cp = _remote(slots.at[me], slots.at[me], sems_send.at[1 + k], sems_recv.at[1 + k], (*chip, c))
            cp.start()
            sends.append(cp)
        for k, chip in enumerate(chips):
            got = slots.at[2 * chip[0] + chip[1]]
            _remote(got, got, sems_send.at[1 + k], sems_recv.at[1 + k], sibling).wait_recv()
        red[...] = ((slots[0] + slots[1]) + slots[2]) + slots[3]
        out_ref[mine, :] = red[...]
        last = _remote(red, out_ref.at[mine, :], sems_send.at[4], sems_recv.at[4], sibling)
        last.start()
        theirs = out_ref.at[pl.ds(pl.multiple_of((1 - c) * half, 8), half), :]
        _remote(red, theirs, sems_send.at[4], sems_recv.at[4], sibling).wait_recv()
        for cp in sends:
            cp.wait_send()
        last.wait_send()
        for hs, a, b, s in zip(hosted, h_in, h_out, h_sems):
            hs.stages[2](a, b, s)

    outs = pl.pallas_call(
        body, name="allreduce_small", in_specs=[VM] + [HB] * (in_at - 1), out_specs=[VM] + [HB] * (out_at - 1),
        out_shape=[jax.ShapeDtypeStruct((rows, w), F32)] + flat(shapes), input_output_aliases=aliases,
        scratch_shapes=[pltpu.VMEM((rows, w), F32), pltpu.VMEM((4, half, w), F32), pltpu.VMEM((half, w), F32),
                        pltpu.SemaphoreType.DMA((5,)), pltpu.SemaphoreType.DMA((5,))] + flat(hs.sems for hs in hosted),
        compiler_params=pltpu.CompilerParams(has_side_effects=True, vmem_limit_bytes=VMEM_LIMIT),
    )(buf, *flat(hs.operands for hs in hosted))
    results, at = [], 1
    for sh in shapes:
        results.append(outs[at:at + len(sh)])
        at += len(sh)
    return outs[0], results


BIG = ("w_in", "w_out", "wq", "wk", "wv", "wo", "w_up", "w_down")
MIXER, ATTN, MLP = ("w_in", "w_out"), ("wq", "wk", "wv", "wo"), ("w_up", "w_down")
BIG_AXIS = {"w_in": 1, "w_out": 0, "wq": 0, "wk": 0, "wv": 0, "wo": 0, "w_up": 1, "w_down": 0}
SMALL = ("norm_mix_g", "pool_w", "pool_scale", "sgu_g", "sgu_w", "sgu_b", "norm_xattn_g", "mem_norm_g", "norm_ffn_g",
         "conv_w", "conv_b", "final_norm_g")
ORDER = ("norm_mix_g", "w_in", "pool_w", "pool_scale", "sgu_g", "sgu_w", "sgu_b", "w_out", "norm_xattn_g", "mem_norm_g",
         "wq", "wk", "wv", "wo", "norm_ffn_g", "w_up", "conv_w", "conv_b", "w_down", "final_norm_g")
PACK_WIDTH = 512


def kernel(x, mem, norm_mix_g, w_in, pool_w, pool_scale, sgu_g, sgu_w, sgu_b, w_out, norm_xattn_g, mem_norm_g, wq, wk, wv, wo, norm_ffn_g, w_up, conv_w, conv_b, w_down, final_norm_g, loss_target, m_norm_mix_g, m_w_in, m_pool_w, m_pool_scale, m_sgu_g, m_sgu_w, m_sgu_b, m_w_out, m_norm_xattn_g, m_mem_norm_g, m_wq, m_wk, m_wv, m_wo, m_norm_ffn_g, m_w_up, m_conv_w, m_conv_b, m_w_down, m_final_norm_g, v_norm_mix_g, v_w_in, v_pool_w, v_pool_scale, v_sgu_g, v_sgu_w, v_sgu_b, v_w_out, v_norm_xattn_g, v_mem_norm_g, v_wq, v_wk, v_wv, v_wo, v_norm_ffn_g, v_w_up, v_conv_w, v_conv_b, v_w_down, v_final_norm_g):
    given = dict(locals())
    w = {n: given[n] for n in ORDER}
    mom = {n: given["m_" + n] for n in ORDER}
    var = {n: given["v_" + n] for n in ORDER}
    nl = w_in.shape[0]
    xs, mems, tgt = x[0], mem[0], loss_target[0]
    chip = 2 * lax.axis_index("x") + lax.axis_index("y")
    core = lax.axis_index("c")

    axes_of = lambda names: [BIG_AXIS[n] for n in names]
    alike = {}
    for n in BIG:
        alike.setdefault((w[n].shape, BIG_AXIS[n]), []).append(n)
    placed = [{} for _ in range(nl)]
    for (_, axis), names in alike.items():
        for n, per_layer in zip(names, _cast_place([w[n] for n in names], axis, chip, "place_" + names[0])):
            for l in range(nl):
                placed[l][n] = per_layer[l]
    conv_full = _allgather_conv(conv_w)

    def gather(names, l):
        return _hosted_allgather([placed[l][n] for n in names], axes_of(names))

    full = [dict(zip(MIXER, _run_hosted(gather(MIXER, 0), "allgather_weights")))]

    row = lambda a, l: a[l][None, :]
    saved = []
    h = xs
    for l in range(nl):
        fw = full[l]
        sbt = jnp.broadcast_to(sgu_b[l][:, :, None], sgu_w[l].shape)
        (h1, proj, xn1, mix), got = _mixer_fwd(h, row(norm_mix_g, l), fw["w_in"], pool_w[l], row(pool_scale, l), row(sgu_g, l), sgu_w[l], sbt, fw["w_out"],
                                               [gather(ATTN, 0), gather(("w_down",), 0)] if l == 0 else None)
        if l == 0:
            fw.update(zip(ATTN, got[0]))
            fw["w_down"] = got[1][0]
        k, v, memn = _kv_fwd(mems, row(mem_norm_g, l), fw["wk"], fw["wv"])
        (h2, q, o, xn2), got = _xattn_fwd(h1, row(norm_xattn_g, l), fw["wq"], k, v, fw["wo"], [gather(("w_up",), 0)] if l == 0 else None)
        if l == 0:
            fw["w_up"] = got[0][0]
        outs, got = _ffn_fwd(h2, row(norm_ffn_g, l), fw["w_up"], conv_full[l], row(conv_b, l), fw["w_down"],
                             [gather(BIG, l + 1)] if l + 1 < nl else None, None if l + 1 < nl else (final_norm_g[None, :], tgt))
        h3, hh, hc = outs[:3]
        if l + 1 < nl:
            full.append(dict(zip(BIG, got[0])))
        else:
            dh, loss_part, g_final = h3, outs[3], outs[4]
        saved.append(dict(h=h, h1=h1, h2=h2, proj=proj, xn1=xn1, mix=mix, k=k, v=v, memn=memn, q=q, o=o, xn2=xn2, hh=hh, hc=hc, sbt=sbt))
        h = h3


    big_grads = {}
    small_grads = [None] * nl

    def weight_grad(n, a, b, l, hosted=None):
        big_grads[n], got = _grad_matmul(a, b, "grad_" + n, l, nl, big_grads.get(n), hosted)
        return got

    sums, slots = {}, {}

    def exchange(names, l):
        return _hosted_exchange([big_grads[n] for n in names], axes_of(names), l)

    def scatter(names, l):
        return _hosted_scatter([sums[n, l] for n in names], axes_of(names))

    def add_casts(names, theirs, l):
        theirs = dict(zip(names, theirs))
        for group in alike.values():
            group = [n for n in group if n in theirs]
            if not group:
                continue
            gl, gr, gc = big_grads[group[0]].shape
            if BIG_AXIS[group[0]] == 0:
                outs = _add_cast([big_grads[n].reshape(gl * 4, 2, gr // 8, gc) for n in group], [theirs[n] for n in group],
                                 core, l * 4, "grad_chip_sum_" + group[0])
            else:
                outs = [o[0] for o in _add_cast([big_grads[n].reshape(gl, 2, gr // 2, gc) for n in group], [theirs[n][None] for n in group],
                                                core, l, "grad_chip_sum_" + group[0])]
            for n, o in zip(group, outs):
                sums[n, l] = o

    def keep_slots(names, got, l):
        for n, sl in zip(names, got):
            slots[n, l] = sl

    shard_grads = {}

    def shard_halves(names):
        out = []
        for n in names:
            buf = None
            for l in range(nl):
                buf = _sum_slots(sums[n, l], slots[n, l], BIG_AXIS[n], chip, core, l, nl, "grad_sum_" + n, buf)
            out.append(buf.reshape(nl, 2 * buf.shape[2], buf.shape[3]))
        return out

    for l in reversed(range(nl)):
        fw, s = full[l], saved[l]
        above = l + 1 < nl
        dh3 = dh
        (dh2, dhh, act, xn3, g_cw, g_cb, g_nf), got = _ffn_bwd(dh3, s["h2"], s["hh"], s["hc"], row(norm_ffn_g, l), fw["w_up"], conv_full[l], fw["w_down"],
                                                         [exchange(MIXER, l + 1), scatter(ATTN, l + 1)] if above else None)
        if above:
            add_casts(MIXER, got[0], l + 1)
            keep_slots(ATTN, got[1], l + 1)
        weight_grad("w_up", xn3, dhh, l)
        weight_grad("w_down", act, dh3, l)
        (dh1, dq, dk, dv, g_nx), got = _xattn_bwd(dh2, s["h1"], s["q"], row(norm_xattn_g, l), fw["wq"], s["k"], s["v"], fw["wo"],
                                                  [exchange(MLP, l), scatter(MIXER, l + 1) if above else None])
        add_casts(MLP, got[0], l)
        if above:
            keep_slots(MIXER, got[1], l + 1)
        weight_grad("wq", s["xn2"], dq, l)
        weight_grad("wo", s["o"], dh2, l)
        weight_grad("wk", s["memn"], dk, l)
        weight_grad("wv", s["memn"], dv, l)
        g_mn = _kv_bwd(dk, dv, mems, fw["wk"], fw["wv"])
        (dh0, dproj, g_nm, g_pw, g_ps, g_sg, g_sw, g_sbt), got = _mixer_bwd(dh1, s["h"], s["proj"], row(norm_mix_g, l), fw["w_in"], pool_w[l], row(pool_scale, l), row(sgu_g, l), sgu_w[l], s["sbt"], fw["w_out"],
                                                                           [scatter(MLP, l), exchange(ATTN, l)])
        keep_slots(MLP, got[0], l)
        add_casts(ATTN, got[1], l)
        half = len(ATTN) // 2
        got = weight_grad("w_in", s["xn1"], dproj, l, [scatter(ATTN[:half], l), _hosted_assemble(shard_halves(MLP))] if l == 0 else None)
        if l == 0:
            keep_slots(ATTN[:half], got[0], l)
            shard_grads.update(zip(MLP, got[1]))
        got = weight_grad("w_out", s["mix"], dh1, l, [scatter(ATTN[half:], l)] if l == 0 else None)
        if l == 0:
            keep_slots(ATTN[half:], got[0], l)
        small_grads[l] = dict(norm_mix_g=g_nm, pool_w=g_pw, pool_scale=g_ps, sgu_g=g_sg, sgu_w=g_sw, sgu_b=jnp.sum(g_sbt, axis=-1),
                              norm_xattn_g=g_nx, mem_norm_g=g_mn, norm_ffn_g=g_nf, conv_w=g_cw, conv_b=g_cb)
        dh = dh0
    grad_x = dh[None]

    add_casts(MIXER, _run_hosted(exchange(MIXER, 0), "grad_sibling_exchange"), 0)

    layered = [n for n in SMALL if n != "final_norm_g"]
    parts = [small_grads[l][n].reshape(-1, PACK_WIDTH) for n in layered for l in range(nl)]
    parts.append(g_final.reshape(-1, PACK_WIDTH))
    parts.append(jnp.pad(loss_part, ((0, 0), (0, PACK_WIDTH - 1))))
    used = sum(p.shape[0] for p in parts)
    total = -(-used // 16) * 16
    packed, got = _allreduce_small(jnp.concatenate(parts + [jnp.zeros((total - used, PACK_WIDTH), F32)], axis=0),
                                   [scatter(MIXER, 0), _hosted_assemble(shard_halves(ATTN))])
    keep_slots(MIXER, got[0], 0)
    shard_grads.update(zip(ATTN, got[1]))
    shard_grads.update(zip(MIXER, _run_hosted(_hosted_assemble(shard_halves(MIXER)), "grad_sibling_assemble")))

    delta, new_m, new_v, grads = {}, {}, {}, {}
    for names in alike.values():
        outs = _adamw_big([w[n] for n in names], [shard_grads[n] for n in names], [mom[n] for n in names], [var[n] for n in names], "adamw_" + names[0])
        for n, out in zip(names, outs):
            grads[n], delta[n], new_m[n], new_v[n] = out
    at = 0
    for n in layered:
        per_layer = []
        for l in range(nl):
            shape = small_grads[l][n].shape
            nrow = small_grads[l][n].size // PACK_WIDTH
            per_layer.append(packed[at:at + nrow].reshape(shape))
            at += nrow
        g = jnp.stack(per_layer)
        if n == "conv_w":
            cs = conv_w.shape[2]
            g = lax.dynamic_slice_in_dim(g, chip * cs, cs, axis=2)
        grads[n] = g.reshape(w[n].shape)
    grads["final_norm_g"] = packed[at:at + g_final.size // PACK_WIDTH].reshape(final_norm_g.shape)
    at += g_final.size // PACK_WIDTH
    loss = packed[at, 0]

    two_d = lambda a: a.reshape(-1, a.shape[-1])
    ds, nms, nvs = _adamw_small([two_d(w[n]) for n in SMALL], [two_d(grads[n]) for n in SMALL],
                                [two_d(mom[n]) for n in SMALL], [two_d(var[n]) for n in SMALL])
    for n, d_, m_, v_ in zip(SMALL, ds, nms, nvs):
        delta[n], new_m[n], new_v[n] = d_.reshape(w[n].shape), m_.reshape(w[n].shape), v_.reshape(w[n].shape)

    return (loss, grad_x, *[grads[n] for n in ORDER], *[delta[n] for n in ORDER], *[new_m[n] for n in ORDER], *[new_v[n] for n in ORDER])
```

```python
import math
from typing import NamedTuple

import jax
import jax.numpy as jnp
from jax import lax
from jax.experimental import pallas as pl
from jax.experimental.pallas import tpu as pltpu

F32 = jnp.float32
_MXU = jnp.bfloat16
_PAY = jnp.bfloat16
EPS = 1e-6
WINDOWS = (2, 4, 8, 16)
GROUP = 128
N_XHEADS = 4
HALO = 16
FF_TILE = 256
DOWN_TILES = 6
LATE_MIDDLE = 0.9375
VMEM_LIMIT = 60 * 1024 * 1024
MESH = pl.DeviceIdType.MESH

ADAM_LR, ADAM_B1, ADAM_B2, ADAM_EPS, ADAM_WD, ADAM_STEP = 0.001, 0.9, 0.999, 1e-08, 0.01, 10

VM = pl.BlockSpec(memory_space=pltpu.VMEM)
HB = pl.BlockSpec(memory_space=pltpu.HBM)


def _nn(a, b):
    return jnp.dot(a, b, preferred_element_type=F32)


def _nt(a, b):
    return lax.dot_general(a, b, (((1,), (1,)), ((), ())), preferred_element_type=F32)


def _tn(a, b):
    return lax.dot_general(a, b, (((0,), (0,)), ((), ())), preferred_element_type=F32)


def _rms(x):
    r = lax.rsqrt(jnp.mean(x * x, axis=-1, keepdims=True) + EPS)
    return x * r, r


def _rms_bwd(dxn, xhat, r, g):
    dxh = dxn * g
    dx = r * (dxh - xhat * jnp.mean(dxh * xhat, axis=-1, keepdims=True))
    return dx, jnp.sum(dxn * xhat, axis=0, keepdims=True)


def _gelu(x):
    cdf = 0.5 * (1.0 + lax.erf(x * (2.0 ** -0.5)))
    return x * cdf, cdf


def _gelu_grad(x, cdf):
    return cdf + x * jnp.exp(-0.5 * x * x) * ((2.0 * math.pi) ** -0.5)


def _params(sem=None):
    return pltpu.CompilerParams(dimension_semantics=sem, vmem_limit_bytes=VMEM_LIMIT)


def _token_block(t, want):
    return want if t % want == 0 and t > want else GROUP


def _const_spec(shape):
    n = len(shape)
    return pl.BlockSpec(shape, lambda i: (0,) * n)


def _tril():
    return lax.broadcasted_iota(jnp.int32, (GROUP, GROUP), 0) >= lax.broadcasted_iota(jnp.int32, (GROUP, GROUP), 1)


def _shift_rows(x, k, edge):
    tb = x.shape[0]
    r8 = lax.broadcasted_iota(jnp.int32, (8, 1), 0)
    rolled = pltpu.roll(x, k % tb, 0)
    if k > 0:
        top = jnp.where(r8 < k, pltpu.roll(edge, k, 0), rolled[0:8, :])
        return jnp.concatenate([top, rolled[8:, :]], axis=0)
    bottom = jnp.where(r8 >= 8 + k, pltpu.roll(edge, 8 + k, 0), rolled[tb - 8:, :])
    return jnp.concatenate([rolled[:tb - 8, :], bottom], axis=0)


def _in_turns(parts):
    parts = list(parts)
    while parts:
        for p in list(parts):
            try:
                next(p)
            except StopIteration:
                parts.remove(p)


class _Hosted(NamedTuple):
    operands: tuple
    aliased: bool
    out_shapes: tuple
    sems: tuple
    stages: tuple


def _hosted_results(hosted):
    if hosted.aliased:
        return [jax.ShapeDtypeStruct(o.shape, o.dtype) for o in hosted.operands]
    return list(hosted.out_shapes)


def _call_hosting(main_body, hosted, *, name, steps, in_specs, out_specs, out_shape, scratch_shapes, operands, aliases=None, middle_at=0.75):
    grid = steps if isinstance(steps, tuple) else (steps,)
    semantics = ("arbitrary",) * len(grid)
    hosted = [hs for hs in (hosted or ()) if hs is not None]
    if not hosted:
        outs = pl.pallas_call(main_body, name=name, grid=grid, in_specs=in_specs, out_specs=out_specs, out_shape=out_shape,
                              scratch_shapes=scratch_shapes, input_output_aliases=aliases or {}, compiler_params=_params(semantics))(*operands)
        return outs, ()
    n_in, n_out, n_sc = len(in_specs), len(out_specs), len(scratch_shapes)
    shapes = [_hosted_results(hs) for hs in hosted]
    aliases, in_at, out_at = dict(aliases or {}), n_in, n_out
    for hs, sh in zip(hosted, shapes):
        if hs.aliased:
            aliases.update({in_at + i: out_at + i for i in range(len(hs.operands))})
        in_at += len(hs.operands)
        out_at += len(sh)

    def body(*refs):
        at = [0]

        def take(n):
            at[0] += n
            return refs[at[0] - n:at[0]]

        ins = take(n_in)
        h_in = [take(len(hs.operands)) for hs in hosted]
        outs = take(n_out)
        h_out = [take(len(sh)) for sh in shapes]
        scratch = take(n_sc)
        h_sems = [take(len(hs.sems)) for hs in hosted]
        ids = [pl.program_id(a) for a in range(len(grid))]

        def at_step(where):
            lead, rest = where
            ok = ids[0] == lead
            for a in range(1, len(grid)):
                ok = jnp.logical_and(ok, ids[a] == (grid[a] - 1 if rest else 0))
            return ok

        def run(stage):
            for hs, a, b, c in zip(hosted, h_in, h_out, h_sems):
                if hs.stages[stage] is not None:
                    hs.stages[stage](a, b, c)

        @pl.when(at_step((0, 0)))
        def _():
            run(0)

        if any(hs.stages[1] is not None for hs in hosted):
            @pl.when(at_step((min(int(middle_at * grid[0]), grid[0] - 1), 0)))
            def _():
                run(1)

        main_body(*ins, *outs, *scratch)

        @pl.when(at_step((grid[0] - 1, -1)))
        def _():
            run(2)

    flat = lambda lists: [x for xs in lists for x in xs]
    outs = pl.pallas_call(
        body, name=name, grid=grid, in_specs=list(in_specs) + [HB] * (in_at - n_in), out_specs=list(out_specs) + [HB] * (out_at - n_out),
        out_shape=list(out_shape) + flat(shapes), scratch_shapes=list(scratch_shapes) + flat(hs.sems for hs in hosted),
        input_output_aliases=aliases, compiler_params=_params(semantics),
    )(*operands, *flat(hs.operands for hs in hosted))
    results, at = [], n_out
    for sh in shapes:
        results.append(outs[at:at + len(sh)])
        at += len(sh)
    return outs[:n_out], results


def _run_hosted(hosted, name):
    nh = len(hosted.operands)
    h_shapes = _hosted_results(hosted)

    def body(*refs):
        h_in, h_out, h_sems = refs[:nh], refs[nh:nh + len(h_shapes)], refs[nh + len(h_shapes):]
        for stage in hosted.stages:
            if stage is not None:
                stage(h_in, h_out, h_sems)

    return pl.pallas_call(
        body, name=name, in_specs=[HB] * nh, out_specs=[HB] * len(h_shapes), out_shape=h_shapes, scratch_shapes=list(hosted.sems),
        input_output_aliases={i: i for i in range(nh)} if hosted.aliased else {},
        compiler_params=pltpu.CompilerParams(has_side_effects=True),
    )(*hosted.operands)


def _window_sums(e, win, back):
    n = e.shape[0]
    k = 1
    while k < win:
        e = e + pltpu.roll(e, k if back else n - k, 0)
        k *= 2
    return e


def _pool_diff(prev, p, t0, gi, win):
    sl = slice(gi * GROUP, (gi + 1) * GROUP)
    tb = p.shape[0]
    s = _window_sums(jnp.concatenate([prev[:, sl], p[:, sl]], axis=0), win, True)[HALO:, :]
    tglob = t0 + lax.broadcasted_iota(jnp.int32, (tb, 1), 0)
    cnt = jnp.minimum(tglob + 1, win).astype(F32)
    return s / cnt - p[:, sl], cnt


def _layernorm(v):
    xc = v - jnp.mean(v, axis=-1, keepdims=True)
    rstd = lax.rsqrt(jnp.mean(xc * xc, axis=-1, keepdims=True) + EPS)
    return xc * rstd, rstd


def _mixer_fwd(h, g, w_in, pool_w, pool_scale, sgu_g, sgu_w, sgu_bt, w_out, hosted=None):
    t, d = h.shape
    pw = pool_w.shape[0] * GROUP
    sw = sgu_w.shape[0] * GROUP
    tb = _token_block(t, 512)

    def body(h_ref, g_ref, win_ref, pw_ref, ps_ref, sg_ref, sw_ref, sbt_ref, wout_ref, h1_ref, proj_ref, xn_ref, mix_ref, pext):
        i = pl.program_id(0)

        @pl.when(i == 0)
        def _():
            pext[...] = jnp.zeros((HALO, pw), F32)

        x = h_ref[...]
        xhat, _ = _rms(x)
        xn = (xhat * g_ref[...]).astype(_MXU)
        xn_ref[...] = xn
        proj = _nn(xn, win_ref[...])
        proj_ref[...] = proj
        p = proj[:, :pw]
        prev = pext[...]
        for gi, win in enumerate(WINDOWS):
            sl = slice(gi * GROUP, (gi + 1) * GROUP)
            dg, _ = _pool_diff(prev, p, i * tb, gi, win)
            e = _nn(dg.astype(_MXU), pw_ref[gi].astype(_MXU))
            mix_ref[:, sl] = (e * ps_ref[:, sl]).astype(_MXU)
        pext[...] = p[tb - HALO:tb, :]
        uv, _ = _gelu(proj[:, pw:])
        u = uv[:, :sw]
        vhat, _ = _layernorm(uv[:, sw:])
        vn = (vhat * sg_ref[...]).astype(_MXU)
        mask = _tril()
        chunks = [slice(n * GROUP, (n + 1) * GROUP) for n in range(tb // GROUP)]
        for hh in range(sw // GROUP):
            wm = jnp.where(mask, sw_ref[hh], 0.0).astype(_MXU)
            cols = slice(hh * GROUP, (hh + 1) * GROUP)
            z = _nn(wm, jnp.concatenate([vn[rows, cols] for rows in chunks], axis=1))
            for n, rows in enumerate(chunks):
                mix_ref[rows, pw + hh * GROUP:pw + (hh + 1) * GROUP] = (u[rows, cols] * (z[:, chunks[n]] + sbt_ref[hh])).astype(_MXU)
        h1_ref[...] = x + _nn(mix_ref[...], wout_ref[...])

    blk = lambda w: pl.BlockSpec((tb, w), lambda i: (i, 0))
    return _call_hosting(
        body, hosted, name="mixer_fwd", steps=t // tb,
        in_specs=[blk(d), VM, VM, VM, VM, VM, VM, VM, VM],
        out_specs=[blk(d), blk(w_in.shape[1]), blk(d), blk(d)],
        out_shape=[jax.ShapeDtypeStruct((t, d), F32), jax.ShapeDtypeStruct((t, w_in.shape[1]), F32),
                   jax.ShapeDtypeStruct((t, d), _MXU), jax.ShapeDtypeStruct((t, d), _MXU)],
        scratch_shapes=[pltpu.VMEM((HALO, pw), F32)],
        operands=(h, g, w_in, pool_w, pool_scale, sgu_g, sgu_w, sgu_bt, w_out), middle_at=LATE_MIDDLE)


def _mixer_bwd(dh1, h, proj, g, w_in, pool_w, pool_scale, sgu_g, sgu_w, sgu_bt, w_out, hosted=None):
    t, d = h.shape
    ng, nh = pool_w.shape[0], sgu_w.shape[0]
    pw, sw = ng * GROUP, nh * GROUP
    tb = _token_block(t, 512)
    nb = t // tb
    n_parts = 2 if tb % (2 * GROUP) == 0 else 1
    pt = tb // n_parts

    def body(dh1_ref, h_ref, proj_ref, halo_ref, g_ref, win_ref, pw_ref, ps_ref, sg_ref, sw_ref, sbt_ref, wout_ref,
             dh_ref, dproj_ref, gg_ref, gpw_ref, gps_ref, gsg_ref, gsw_ref, gsbt_ref, dext, duv):
        i = pl.program_id(0)
        blk = nb - 1 - i

        @pl.when(i == 0)
        def _():
            for r in (gg_ref, gpw_ref, gps_ref, gsg_ref, gsw_ref, gsbt_ref, dext):
                r[...] = jnp.zeros(r.shape, F32)

        mask = _tril()

        def part(at):
            rows = slice(at, at + pt)
            dh1v = dh1_ref[rows, :]
            dmix = _nt(dh1v.astype(_MXU), wout_ref[...])
            yield
            proj_v = proj_ref[rows, :]
            p = proj_v[:, :pw]
            prev = jnp.where(blk == 0, 0.0, halo_ref[...]) if at == 0 else proj_ref[at - HALO:at, 0:pw]
            for gi, win in enumerate(WINDOWS):
                sl = slice(gi * GROUP, (gi + 1) * GROUP)
                dg, cnt = _pool_diff(prev, p, blk * tb + at, gi, win)
                dgm = dg.astype(_MXU)
                pwm = pw_ref[gi].astype(_MXU)
                e = _nn(dgm, pwm)
                dy = dmix[:, sl]
                gps_ref[:, sl] += jnp.sum(dy * e, axis=0, keepdims=True)
                de = (dy * ps_ref[:, sl]).astype(_MXU)
                gpw_ref[gi] += _tn(dgm, de)
                dd = _nt(de, pwm)
                ddc = dd / cnt
                acc = _window_sums(jnp.concatenate([ddc, dext[:, sl]], axis=0), win, False)[:pt, :]
                dext[:, sl] = ddc[0:HALO, :]
                dproj_ref[rows, sl] = (acc - dd).astype(_MXU)
            yield
            pre = proj_v[:, pw:]
            uv, cdf = _gelu(pre)
            u = uv[:, :sw]
            vhat, rstd = _layernorm(uv[:, sw:])
            vn = (vhat * sg_ref[...]).astype(_MXU)
            chunks = [slice(n * GROUP, (n + 1) * GROUP) for n in range(pt // GROUP)]
            side_by_side = lambda a, cols: jnp.concatenate([a[c, cols] for c in chunks], axis=1)
            for hh in range(nh):
                wm = jnp.where(mask, sw_ref[hh], 0.0).astype(_MXU)
                cols = slice(hh * GROUP, (hh + 1) * GROUP)
                vs = side_by_side(vn, cols)
                z = _nn(wm, vs)
                dy = side_by_side(dmix, slice(pw + hh * GROUP, pw + (hh + 1) * GROUP))
                dz = dy * side_by_side(u, cols)
                dzm = dz.astype(_MXU)
                dvs = _tn(wm, dzm)
                gsw_ref[hh] += jnp.where(mask, _nt(dzm, vs), 0.0)
                gb = jnp.zeros((GROUP, GROUP), F32)
                for n, c in enumerate(chunks):
                    gb = gb + dz[:, c]
                    duv[at + n * GROUP:at + (n + 1) * GROUP, cols] = dy[:, c] * (z[:, c] + sbt_ref[hh])
                    duv[at + n * GROUP:at + (n + 1) * GROUP, sw + hh * GROUP:sw + (hh + 1) * GROUP] = dvs[:, c]
                gsbt_ref[hh] += gb
            yield
            dvn = duv[rows, sw:]
            gsg_ref[...] += jnp.sum(dvn * vhat, axis=0, keepdims=True)
            dxh = dvn * sg_ref[...]
            dv = rstd * (dxh - jnp.mean(dxh, axis=-1, keepdims=True) - vhat * jnp.mean(dxh * vhat, axis=-1, keepdims=True))
            gp = _gelu_grad(pre, cdf)
            dproj_ref[rows, pw:pw + sw] = (duv[rows, :sw] * gp[:, :sw]).astype(_MXU)
            dproj_ref[rows, pw + sw:] = (dv * gp[:, sw:]).astype(_MXU)
            dxn = _nt(dproj_ref[rows, :], win_ref[...])
            yield
            xhat, r = _rms(h_ref[rows, :])
            dx, gg = _rms_bwd(dxn, xhat, r, g_ref[...])
            gg_ref[...] += gg
            dh_ref[rows, :] = dh1v + dx

        _in_turns([part(at) for at in reversed(range(0, tb, pt))])

    rev = lambda w: pl.BlockSpec((tb, w), lambda i: (nb - 1 - i, 0))
    halo = pl.BlockSpec((HALO, pw), lambda i: (jnp.maximum((nb - 1 - i) * (tb // HALO) - 1, 0), 0))
    small = [(1, d), (ng, GROUP, GROUP), (1, pw), (1, sw), (nh, GROUP, GROUP), (nh, GROUP, GROUP)]
    return _call_hosting(
        body, hosted, name="mixer_bwd", steps=nb,
        in_specs=[rev(d), rev(d), rev(proj.shape[1]), halo, VM, VM, VM, VM, VM, VM, VM, VM],
        out_specs=[rev(d), rev(proj.shape[1])] + [_const_spec(s) for s in small],
        out_shape=[jax.ShapeDtypeStruct((t, d), F32), jax.ShapeDtypeStruct(proj.shape, _MXU)]
        + [jax.ShapeDtypeStruct(s, F32) for s in small],
        scratch_shapes=[pltpu.VMEM((HALO, pw), F32), pltpu.VMEM((tb, 2 * sw), F32)],
        operands=(dh1, h, proj, proj, g, w_in, pool_w, pool_scale, sgu_g, sgu_w, sgu_bt, w_out))


def _kv_fwd(mem, gm, wk, wv):
    n, d = mem.shape

    def body(mem_ref, gm_ref, wk_ref, wv_ref, k_ref, v_ref, memn_ref):
        xhat, _ = _rms(mem_ref[...])
        memn = (xhat * gm_ref[...]).astype(_MXU)
        memn_ref[...] = memn
        k_ref[...] = _nn(memn, wk_ref[...]).astype(_MXU)
        v_ref[...] = _nn(memn, wv_ref[...]).astype(_MXU)

    return pl.pallas_call(
        body, name="kv_fwd", in_specs=[VM] * 4, out_specs=[VM] * 3,
        out_shape=[jax.ShapeDtypeStruct((n, d), _MXU)] * 3, compiler_params=_params(),
    )(mem, gm, wk, wv)


def _kv_bwd(dk, dv, mem, wk, wv):
    n, d = mem.shape

    def body(dk_ref, dv_ref, mem_ref, wk_ref, wv_ref, ggm_ref):
        dmemn = _nt(dk_ref[...].astype(_MXU), wk_ref[...]) + _nt(dv_ref[...].astype(_MXU), wv_ref[...])
        xhat, _ = _rms(mem_ref[...])
        ggm_ref[...] = jnp.sum(dmemn * xhat, axis=0, keepdims=True)

    return pl.pallas_call(
        body, name="kv_bwd", in_specs=[VM] * 5, out_specs=VM,
        out_shape=jax.ShapeDtypeStruct((1, d), F32), compiler_params=_params(),
    )(dk, dv, mem, wk, wv)


def _softmax(s):
    e = jnp.exp(s - jnp.max(s, axis=-1, keepdims=True))
    return e / jnp.sum(e, axis=-1, keepdims=True)


def _one_ahead(n, issue):
    nxt = issue(0)
    for a in range(n):
        cur = nxt
        if a + 1 < n:
            nxt = issue(a + 1)
        yield a, cur


def _xattn_fwd(h, g, wq, k, v, wo, hosted=None):
    t, d = h.shape
    hd = d // N_XHEADS
    scale = hd ** -0.5
    tb = _token_block(t, 512)

    def body(h_ref, g_ref, wq_ref, k_ref, v_ref, wo_ref, h2_ref, q_ref, o_ref, xn_ref):
        x = h_ref[...]
        xhat, _ = _rms(x)
        xn = (xhat * g_ref[...]).astype(_MXU)
        xn_ref[...] = xn
        qm = _nn(xn, wq_ref[...]).astype(_MXU)
        q_ref[...] = qm
        heads = [slice(a * hd, (a + 1) * hd) for a in range(N_XHEADS)]
        for a, s in _one_ahead(N_XHEADS, lambda a: _nt(qm[:, heads[a]], k_ref[:, heads[a]]) * scale):
            o_ref[:, heads[a]] = _nn(_softmax(s).astype(_MXU), v_ref[:, heads[a]]).astype(_MXU)
        h2_ref[...] = x + _nn(o_ref[...], wo_ref[...])

    blk = pl.BlockSpec((tb, d), lambda i: (i, 0))
    return _call_hosting(
        body, hosted, name="xattn_fwd", steps=t // tb,
        in_specs=[blk, VM, VM, VM, VM, VM], out_specs=[blk] * 4,
        out_shape=[jax.ShapeDtypeStruct((t, d), F32)] + [jax.ShapeDtypeStruct((t, d), _MXU)] * 3,
        scratch_shapes=[], operands=(h, g, wq, k, v, wo), middle_at=LATE_MIDDLE)


def _xattn_bwd(dh2, h, q, g, wq, k, v, wo, hosted=None):
    t, d = h.shape
    n = k.shape[0]
    hd = d // N_XHEADS
    scale = hd ** -0.5
    tb = _token_block(t, 512)
    pt = tb // 2 if tb % (2 * GROUP) == 0 else tb

    def body(dh2_ref, h_ref, q_ref, g_ref, wq_ref, k_ref, v_ref, wo_ref, dh_ref, dq_ref, dk_ref, dv_ref, gg_ref):
        @pl.when(pl.program_id(0) == 0)
        def _():
            for r in (dk_ref, dv_ref, gg_ref):
                r[...] = jnp.zeros(r.shape, F32)

        heads = [slice(a * hd, (a + 1) * hd) for a in range(N_XHEADS)]

        def part(at):
            rows = slice(at, at + pt)
            dh2v = dh2_ref[rows, :]
            dom = _nt(dh2v.astype(_MXU), wo_ref[...]).astype(_MXU)
            yield
            issue = lambda a: (_nt(q_ref[rows, heads[a]], k_ref[:, heads[a]]) * scale, _nt(dom[:, heads[a]], v_ref[:, heads[a]]))
            for a, (s, dpr) in _one_ahead(N_XHEADS, issue):
                sl = heads[a]
                pr = _softmax(s)
                dv_ref[:, sl] += _tn(pr.astype(_MXU), dom[:, sl])
                ds = (pr * (dpr - jnp.sum(dpr * pr, axis=-1, keepdims=True)) * scale).astype(_MXU)
                dq_ref[rows, sl] = _nn(ds, k_ref[:, sl]).astype(_MXU)
                dk_ref[:, sl] += _tn(ds, q_ref[rows, sl])
                yield
            dxn = _nt(dq_ref[rows, :], wq_ref[...])
            yield
            xhat, r = _rms(h_ref[rows, :])
            dx, gg = _rms_bwd(dxn, xhat, r, g_ref[...])
            gg_ref[...] += gg
            dh_ref[rows, :] = dh2v + dx

        _in_turns([part(at) for at in range(0, tb, pt)])

    blk = pl.BlockSpec((tb, d), lambda i: (i, 0))
    return _call_hosting(
        body, hosted, name="xattn_bwd", steps=t // tb,
        in_specs=[blk, blk, blk, VM, VM, VM, VM, VM],
        out_specs=[blk, blk, _const_spec((n, d)), _const_spec((n, d)), _const_spec((1, d))],
        out_shape=[jax.ShapeDtypeStruct((t, d), F32), jax.ShapeDtypeStruct((t, d), _MXU),
                   jax.ShapeDtypeStruct((n, d), F32), jax.ShapeDtypeStruct((n, d), F32), jax.ShapeDtypeStruct((1, d), F32)],
        scratch_shapes=[], operands=(dh2, h, q, g, wq, k, v, wo))


def _ffn_fwd(h, g, w_up, conv_w, conv_b, w_down, hosted=None, head=None):
    t, d = h.shape
    f = w_down.shape[0]
    ft = FF_TILE
    tb = _token_block(t, 256)

    def body(h_ref, g_ref, wup_ref, cw_ref, cb_ref, wdown_ref, *rest):
        if head is None:
            h3_ref, hh_ref, hc_ref, ext, carry, act_sc = rest
        else:
            gf_ref, tgt_ref, h3_ref, hh_ref, hc_ref, loss_ref, ggf_ref, ext, carry, act_sc = rest

        @pl.when(pl.program_id(0) == 0)
        def _():
            carry[...] = jnp.zeros(carry.shape, F32)
            if head is not None:
                loss_ref[...] = jnp.zeros(loss_ref.shape, F32)
                ggf_ref[...] = jnp.zeros(ggf_ref.shape, F32)

        x = h_ref[...]
        xhat, _ = _rms(x)
        xn = (xhat * g_ref[...]).astype(_MXU)
        acc = jnp.zeros((tb, d), F32)
        up = lambda j: [_nn(xn, wup_ref[:, off:off + ft]) for off in (j * ft, f + j * ft)]
        up_next = up(0)
        for j in range(f // ft):
            hc = []
            up_cur = up_next
            if j + 1 < f // ft:
                up_next = up(j + 1)
            for part, off in enumerate((j * ft, f + j * ft)):
                cols = slice(off, off + ft)
                cur = up_cur[part]
                hh_ref[:, cols] = cur.astype(_MXU)
                ext[part, 0:8, :] = carry[:, cols]
                ext[part, 8:8 + tb, :] = cur
                carry[:, cols] = cur[tb - 8:tb, :]
                hc.append(cb_ref[:, cols] + cw_ref[0:1, cols] * ext[part, 6:6 + tb, :]
                          + cw_ref[1:2, cols] * ext[part, 7:7 + tb, :] + cw_ref[2:3, cols] * cur)
                hc_ref[:, cols] = hc[part].astype(_MXU)
            at = j % DOWN_TILES
            act_sc[:, at * ft:(at + 1) * ft] = (hc[0] * jax.nn.sigmoid(hc[0]) * hc[1]).astype(_MXU)
            if at + 1 == DOWN_TILES or j + 1 == f // ft:
                acc = acc + _nn(act_sc[:, 0:(at + 1) * ft], wdown_ref[(j - at) * ft:(j + 1) * ft, :])
        if head is None:
            h3_ref[...] = x + acc
        else:
            yhat, r = _rms(x + acc)
            err = yhat * gf_ref[...] - tgt_ref[...]
            loss_ref[...] += 0.5 * jnp.sum(jnp.sum(err * err, axis=-1, keepdims=True), axis=0, keepdims=True) / d
            dx, gg = _rms_bwd(err / d, yhat, r, gf_ref[...])
            ggf_ref[...] += gg
            h3_ref[...] = dx

    blk = lambda w: pl.BlockSpec((tb, w), lambda i: (i, 0))
    in_specs = [blk(d), VM, VM, VM, VM, VM]
    out_specs = [blk(d), blk(2 * f), blk(2 * f)]
    out_shape = [jax.ShapeDtypeStruct((t, d), F32), jax.ShapeDtypeStruct((t, 2 * f), _MXU), jax.ShapeDtypeStruct((t, 2 * f), _MXU)]
    operands = (h, g, w_up, conv_w, conv_b, w_down)
    if head is not None:
        in_specs += [VM, blk(d)]
        out_specs += [_const_spec((1, 1)), _const_spec((1, d))]
        out_shape += [jax.ShapeDtypeStruct((1, 1), F32), jax.ShapeDtypeStruct((1, d), F32)]
        operands += tuple(head)
    return _call_hosting(
        body, hosted, name="ffn_fwd", steps=t // tb, in_specs=in_specs, out_specs=out_specs, out_shape=out_shape,
        scratch_shapes=[pltpu.VMEM((2, 8 + tb, ft), F32), pltpu.VMEM((8, 2 * f), F32), pltpu.VMEM((tb, DOWN_TILES * ft), _MXU)],
        operands=operands)


def _ffn_bwd(dh3, h, hh, hc, g, w_up, conv_w, w_down, hosted=None):
    t, d = h.shape
    f = w_down.shape[0]
    ft = FF_TILE
    tb = _token_block(t, 256)
    nb = t // tb

    def body(dh3_ref, h_ref, hh_ref, hc_ref, g_ref, wup_ref, cw_ref, wdown_ref,
             dh_ref, dhh_ref, act_ref, xn_ref, gcw_ref, gcb_ref, gg_ref, dcarry):
        @pl.when(pl.program_id(0) == 0)
        def _():
            for r in (gcw_ref, gcb_ref, gg_ref, dcarry):
                r[...] = jnp.zeros(r.shape, F32)

        dh3v = dh3_ref[...]
        dhm = dh3v.astype(_MXU)
        dxn = jnp.zeros((tb, d), F32)
        dact_next = _nt(dhm, wdown_ref[0:ft, :])
        for j in range(f // ft):
            dact = dact_next
            if j + 1 < f // ft:
                dact_next = _nt(dhm, wdown_ref[(j + 1) * ft:(j + 2) * ft, :])
            gate = hc_ref[:, j * ft:(j + 1) * ft].astype(F32)
            val = hc_ref[:, f + j * ft:f + (j + 1) * ft].astype(F32)
            sg = jax.nn.sigmoid(gate)
            silu = gate * sg
            act_ref[:, j * ft:(j + 1) * ft] = (silu * val).astype(_MXU)
            dhc = (dact * val * sg * (1.0 + gate * (1.0 - sg)), dact * silu)
            for part, off in enumerate((j * ft, f + j * ft)):
                cols = slice(off, off + ft)
                dc = dhc[part]
                c0 = hh_ref[:, cols].astype(F32)
                after = dcarry[:, cols]
                ahead1 = _shift_rows(dc, -1, after)
                ahead2 = _shift_rows(dc, -2, after)
                dcarry[:, cols] = dc[0:8, :]
                gcb_ref[:, cols] += jnp.sum(dc, axis=0, keepdims=True)
                gcw_ref[0:1, cols] += jnp.sum(ahead2 * c0, axis=0, keepdims=True)
                gcw_ref[1:2, cols] += jnp.sum(ahead1 * c0, axis=0, keepdims=True)
                gcw_ref[2:3, cols] += jnp.sum(dc * c0, axis=0, keepdims=True)
                dhh = (cw_ref[2:3, cols] * dc + cw_ref[1:2, cols] * ahead1 + cw_ref[0:1, cols] * ahead2).astype(_MXU)
                dhh_ref[:, cols] = dhh
                dxn = dxn + _nt(dhh, wup_ref[:, cols])
        xhat, r = _rms(h_ref[...])
        xn_ref[...] = (xhat * g_ref[...]).astype(_MXU)
        dx, gg = _rms_bwd(dxn, xhat, r, g_ref[...])
        gg_ref[...] += gg
        dh_ref[...] = dh3v + dx

    rev = lambda w: pl.BlockSpec((tb, w), lambda i: (nb - 1 - i, 0))
    return _call_hosting(
        body, hosted, name="ffn_bwd", steps=nb,
        in_specs=[rev(d), rev(d), rev(2 * f), rev(2 * f), VM, VM, VM, VM],
        out_specs=[rev(d), rev(2 * f), rev(f), rev(d), _const_spec((3, 2 * f)), _const_spec((1, 2 * f)), _const_spec((1, d))],
        out_shape=[jax.ShapeDtypeStruct((t, d), F32), jax.ShapeDtypeStruct((t, 2 * f), _MXU), jax.ShapeDtypeStruct((t, f), _MXU),
                   jax.ShapeDtypeStruct((t, d), _MXU),
                   jax.ShapeDtypeStruct((3, 2 * f), F32), jax.ShapeDtypeStruct((1, 2 * f), F32), jax.ShapeDtypeStruct((1, d), F32)],
        scratch_shapes=[pltpu.VMEM((8, 2 * f), F32)],
        operands=(dh3, h, hh, hc, g, w_up, conv_w, w_down))


def _largest_tile(n, cap, mult=128):
    best = None
    for c in range(mult, min(n, cap) + 1, mult):
        if n % c == 0:
            best = c
    return best if best is not None else n


def _grad_matmul(a, b, name, layer, n_layers, into=None, hosted=None):
    t, m = a.shape
    n = b.shape[1]
    tm, tn, tk = _largest_tile(m, 1408), _largest_tile(n, 1536), _largest_tile(t, 1024)
    nk = t // tk

    def body(a_ref, b_ref, *rest):
        o_ref = rest[-1]

        @pl.when(pl.program_id(2) == 0)
        def _():
            o_ref[...] = jnp.zeros(o_ref.shape, F32)

        o_ref[...] += _tn(a_ref[...].astype(_MXU), b_ref[...].astype(_MXU))

    in_specs = [pl.BlockSpec((tk, tm), lambda i, j, k: (k, i)), pl.BlockSpec((tk, tn), lambda i, j, k: (k, j))]
    operands = (a, b)
    aliases = {}
    if into is not None:
        in_specs.append(pl.BlockSpec(memory_space=pl.ANY))
        operands = (a, b, into)
        aliases = {2: 0}
    (out,), got = _call_hosting(
        body, hosted, name=name, steps=(m // tm, n // tn, nk), in_specs=in_specs,
        out_specs=[pl.BlockSpec((None, tm, tn), lambda i, j, k: (layer, i, j))],
        out_shape=[jax.ShapeDtypeStruct((n_layers, m, n), F32)], scratch_shapes=[], operands=operands, aliases=aliases)
    return out, got


def _adamw_math(w, g, m, v):
    m = ADAM_B1 * m + (1.0 - ADAM_B1) * g
    v = ADAM_B2 * v + (1.0 - ADAM_B2) * (g * g)
    m_hat = m / (1.0 - ADAM_B1 ** ADAM_STEP)
    v_hat = v / (1.0 - ADAM_B2 ** ADAM_STEP)
    return -ADAM_LR * (m_hat / (jnp.sqrt(v_hat) + ADAM_EPS) + ADAM_WD * w), m, v


def _row_block(rows, cols, max_bytes=1 << 20, mult=16):
    best = None
    for r in range(mult, rows + 1, mult):
        if rows % r == 0 and r * cols * 4 <= max_bytes:
            best = r
    return best if best is not None else rows


def _adamw_big(ws, gs, ms, vs, name):
    n = len(ws)
    shape = ws[0].shape
    cols = shape[-1]
    flat = lambda a: a.reshape(-1, cols)
    rows = flat(ws[0]).shape[0]
    rb = _row_block(rows, cols, (2 << 20) // n)

    def body(*refs):
        for a in range(n):
            w_ref, g_ref, m_ref, v_ref = (refs[s * n + a] for s in range(4))
            go_ref, d_ref, nm_ref, nv_ref = (refs[(4 + s) * n + a] for s in range(4))
            g = g_ref[...]
            go_ref[...] = g
            d_ref[...], nm_ref[...], nv_ref[...] = _adamw_math(w_ref[...], g, m_ref[...], v_ref[...])

    blk = pl.BlockSpec((rb, cols), lambda i: (i, 0))
    outs = pl.pallas_call(
        body, name=name, grid=(rows // rb,), in_specs=[blk] * (4 * n), out_specs=[blk] * (4 * n),
        out_shape=[jax.ShapeDtypeStruct((rows, cols), F32)] * (4 * n), compiler_params=_params(("parallel",)),
    )(*[flat(a) for group in (ws, gs, ms, vs) for a in group])
    return [[outs[s * n + a].reshape(shape) for s in range(4)] for a in range(n)]


def _adamw_small(ws, gs, ms, vs):
    n = len(ws)

    def body(*refs):
        for a in range(n):
            w_ref, g_ref, m_ref, v_ref = (refs[s * n + a] for s in range(4))
            d_ref, nm_ref, nv_ref = (refs[(4 + s) * n + a] for s in range(3))
            d_ref[...], nm_ref[...], nv_ref[...] = _adamw_math(w_ref[...], g_ref[...], m_ref[...], v_ref[...])

    outs = pl.pallas_call(
        body, name="adamw_small", in_specs=[VM] * (4 * n), out_specs=[VM] * (3 * n),
        out_shape=[jax.ShapeDtypeStruct(w.shape, F32) for w in ws] * 3, compiler_params=_params(),
    )(*ws, *gs, *ms, *vs)
    return outs[:n], outs[n:2 * n], outs[2 * n:]


def _place():
    x, y, c = lax.axis_index("x"), lax.axis_index("y"), lax.axis_index("c")
    chips = [(1 - x, y), (x, 1 - y), (1 - x, 1 - y)]
    return x, y, c, chips


def _rows(start, size, mult=16):
    return pl.ds(pl.multiple_of(start, mult), size)


def _full_window(ref, axis, chip, half=None):
    r, c = ref.shape
    if axis == 0:
        rs = r // 4
        if half is None:
            return ref.at[_rows(chip * rs, rs), :]
        return ref.at[_rows(chip * rs + half * (rs // 2), rs // 2), :]
    cs = c // 4
    if half is None:
        return ref.at[:, _rows(chip * cs, cs, 128)]
    return ref.at[_rows(half * (r // 2), r // 2), _rows(chip * cs, cs, 128)]


def _remote(src, dst, send_sem, recv_sem, to):
    return pltpu.make_async_remote_copy(src_ref=src, dst_ref=dst, send_sem=send_sem, recv_sem=recv_sem,
                                        device_id=to, device_id_type=MESH)


def _scalars(*vals):
    return jnp.stack([jnp.asarray(v, jnp.int32) for v in vals])


def _cast_place(shards, axis, chip, name):
    n = len(shards)
    nl, rs, cs = shards[0].shape
    full = (rs * 4, cs) if axis == 0 else (rs, cs * 4)
    rb = _row_block(rs, cs, (4 << 20) // (n * nl))
    nrb = rs // rb

    def body(chip_ref, *refs):
        for a in range(n):
            for l in range(nl):
                refs[n + a * nl + l][...] = refs[a][l].astype(_PAY)

    if axis == 0:
        out_map = lambda i, chip_ref: (chip_ref[0] * nrb + i, 0)
    else:
        out_map = lambda i, chip_ref: (i, chip_ref[0])
    outs = pl.pallas_call(
        body, name=name,
        grid_spec=pltpu.PrefetchScalarGridSpec(
            num_scalar_prefetch=1, grid=(nrb,),
            in_specs=[pl.BlockSpec((nl, rb, cs), lambda i, chip_ref: (0, i, 0))] * n,
            out_specs=[pl.BlockSpec((rb, cs), out_map)] * (n * nl)),
        out_shape=[jax.ShapeDtypeStruct(full, _PAY)] * (n * nl), compiler_params=_params(("parallel",)),
    )(_scalars(chip), *shards)
    return [[outs[a * nl + l] for l in range(nl)] for a in range(n)]


def _hosted_allgather(placed, axes):
    n = len(placed)

    def each(outs, half_of):
        x, y, c, chips = _place()
        for i in range(n):
            for k, chip in enumerate(chips):
                yield i * 3 + k, (*chip, c), (x, y, 1 - c), _full_window(outs[i], axes[i], 2 * x + y, c), \
                    _full_window(outs[i], axes[i], 2 * chip[0] + chip[1], half_of(c))

    def start(_, outs, sems):
        send, recv, _, _ = sems
        for s, peer, _, mine, _ in each(outs, lambda c: c):
            _remote(mine, mine, send.at[s], recv.at[s], peer).start()

    def middle(_, outs, sems):
        send, recv, fsend, frecv = sems
        for s, _, sibling, _, got in each(outs, lambda c: c):
            _remote(got, got, send.at[s], recv.at[s], sibling).wait_recv()
            _remote(got, got, fsend.at[s], frecv.at[s], sibling).start()

    def finish(_, outs, sems):
        send, recv, fsend, frecv = sems
        for s, _, sibling, _, got in each(outs, lambda c: 1 - c):
            _remote(got, got, fsend.at[s], frecv.at[s], sibling).wait_recv()
        for s, peer, sibling, mine, got in each(outs, lambda c: c):
            _remote(mine, mine, send.at[s], recv.at[s], peer).wait_send()
            _remote(got, got, fsend.at[s], frecv.at[s], sibling).wait_send()

    return _Hosted(tuple(placed), True, (), (pltpu.SemaphoreType.DMA((n * 3,)),) * 4, (start, middle, finish))


def _allgather_conv(conv_shard):
    nl, taps, cs = conv_shard.shape

    def body(in_ref, out_ref, send, recv, local):
        x, y, c, chips = _place()
        mine = out_ref.at[:, :, _rows((2 * x + y) * cs, cs, 128)]
        own = pltpu.make_async_copy(in_ref, mine, local)
        own.start()
        sends = [_remote(in_ref, mine, send.at[k], recv.at[k], (*chip, c)) for k, chip in enumerate(chips)]
        for cp in sends:
            cp.start()
        for k, chip in enumerate(chips):
            got = out_ref.at[:, :, _rows((2 * chip[0] + chip[1]) * cs, cs, 128)]
            _remote(got, got, send.at[k], recv.at[k], (*chip, c)).wait_recv()
        for cp in sends:
            cp.wait_send()
        own.wait()

    return pl.pallas_call(
        body, name="allgather_conv", in_specs=[HB], out_specs=HB, out_shape=jax.ShapeDtypeStruct((nl, taps, cs * 4), conv_shard.dtype),
        scratch_shapes=[pltpu.SemaphoreType.DMA((3,)), pltpu.SemaphoreType.DMA((3,)), pltpu.SemaphoreType.DMA],
        compiler_params=pltpu.CompilerParams(has_side_effects=True),
    )(conv_shard)


def _hosted_exchange(grads, axes, layer):
    na = len(grads)
    views = [g.reshape(g.shape[0], 4, 2, g.shape[1] // 8, g.shape[2]) if ax == 0 else g for g, ax in zip(grads, axes)]

    def region(ref, axis, half):
        if axis == 0:
            return ref.at[layer, :, half]
        r = ref.shape[1]
        return ref.at[layer, _rows(half * (r // 2), r // 2), :]

    def copies(ins, land, sems):
        send, recv = sems
        x, y, c, _ = _place()
        return [_remote(region(ins[a], axes[a], 1 - c), land[a], send.at[a], recv.at[a], (x, y, 1 - c)) for a in range(na)]

    def start(ins, land, sems):
        for cp in copies(ins, land, sems):
            cp.start()

    def finish(ins, land, sems):
        for cp in copies(ins, land, sems):
            cp.wait()

    shapes = [(4, g.shape[1] // 8, g.shape[2]) if ax == 0 else (g.shape[1] // 2, g.shape[2]) for g, ax in zip(grads, axes)]
    return _Hosted(tuple(views), False, tuple(jax.ShapeDtypeStruct(s, F32) for s in shapes),
                   (pltpu.SemaphoreType.DMA((na,)),) * 2, (start, None, finish))


def _add_cast(mines, theirs, core, base, name):
    n = len(mines)
    na, nb, cols = theirs[0].shape
    rb = _row_block(nb, cols, (4 << 20) // n)

    def body(core_ref, *refs):
        for a in range(n):
            refs[2 * n + a][...] = (refs[a][...] + refs[n + a][...]).astype(_PAY)

    blk = pl.BlockSpec((None, rb, cols), lambda i, k, core_ref: (i, k, 0))
    return pl.pallas_call(
        body, name=name,
        grid_spec=pltpu.PrefetchScalarGridSpec(
            num_scalar_prefetch=1, grid=(na, nb // rb),
            in_specs=[pl.BlockSpec((None, None, rb, cols), lambda i, k, core_ref: (base + i, core_ref[0], k, 0))] * n + [blk] * n,
            out_specs=[blk] * n),
        out_shape=[jax.ShapeDtypeStruct((na, nb, cols), _PAY)] * n, compiler_params=_params(("parallel", "parallel")),
    )(_scalars(core), *mines, *theirs)


def _piece(ref, axis, chip):
    if axis == 0:
        return ref.at[chip]
    cs = ref.shape[1] // 4
    return ref.at[:, _rows(chip * cs, cs, 128)]


def _hosted_scatter(sums, axes):
    na = len(sums)

    def piece_shape(a):
        if axes[a] == 0:
            return (sums[a].shape[1], sums[a].shape[2])
        return (sums[a].shape[0], sums[a].shape[1] // 4)

    def copies(ins, slots, sems):
        send, recv = sems
        _, _, c, chips = _place()
        return [_remote(_piece(ins[a], axes[a], 2 * chip[0] + chip[1]), slots[a].at[k], send.at[a * 3 + k], recv.at[a * 3 + k], (*chip, c))
                for a in range(na) for k, chip in enumerate(chips)]

    def start(ins, slots, sems):
        for cp in copies(ins, slots, sems):
            cp.start()

    def finish(ins, slots, sems):
        for cp in copies(ins, slots, sems):
            cp.wait()

    return _Hosted(tuple(sums), False, tuple(jax.ShapeDtypeStruct((3,) + piece_shape(a), sums[a].dtype) for a in range(na)),
                   (pltpu.SemaphoreType.DMA((na * 3,)),) * 2, (start, None, finish))


def _sum_slots(sums, slots, axis, chip, core, layer, n_layers, name, into=None):
    _, hr, cs = slots.shape
    rb = _row_block(hr, cs, 4 << 20)

    def body(at_ref, own_ref, s_ref, *rest):
        rest[-1][...] = ((own_ref[...].astype(F32) + s_ref[0].astype(F32)) + s_ref[1].astype(F32)) + s_ref[2].astype(F32)

    if axis == 0:
        own = pl.BlockSpec((None, rb, cs), lambda k, at_ref: (at_ref[0], k, 0))
    else:
        own = pl.BlockSpec((rb, cs), lambda k, at_ref: (k, at_ref[0]))
    in_specs = [own, pl.BlockSpec((3, rb, cs), lambda k, at_ref: (0, k, 0))]
    operands = (sums, slots)
    aliases = {}
    if into is not None:
        in_specs.append(pl.BlockSpec(memory_space=pl.ANY))
        operands = (sums, slots, into)
        aliases = {3: 0}
    return pl.pallas_call(
        body, name=name,
        grid_spec=pltpu.PrefetchScalarGridSpec(
            num_scalar_prefetch=1, grid=(hr // rb,), in_specs=in_specs,
            out_specs=pl.BlockSpec((None, None, rb, cs), lambda k, at_ref: (layer, at_ref[1], k, 0))),
        out_shape=jax.ShapeDtypeStruct((n_layers, 2, hr, cs), F32), input_output_aliases=aliases,
        compiler_params=_params(("parallel",)),
    )(_scalars(chip, core), *operands)


def _hosted_assemble(shards):
    na = len(shards)

    def copies(_, outs, sems):
        send, recv = sems
        x, y, c, _ = _place()
        halves = [outs[a].at[:, _rows(c * (outs[a].shape[1] // 2), outs[a].shape[1] // 2), :] for a in range(na)]
        return [_remote(mine, mine, send.at[a], recv.at[a], (x, y, 1 - c)) for a, mine in enumerate(halves)]

    def start(ins, outs, sems):
        for cp in copies(ins, outs, sems):
            cp.start()

    def finish(ins, outs, sems):
        for cp in copies(ins, outs, sems):
            cp.wait()

    return _Hosted(tuple(shards), True, (), (pltpu.SemaphoreType.DMA((na,)),) * 2, (start, None, finish))


def _allreduce_small(buf, hosted):
    rows, w = buf.shape
    half = rows // 2
    shapes = [_hosted_results(hs) for hs in hosted]
    flat = lambda lists: [x for xs in lists for x in xs]
    aliases, in_at, out_at = {}, 1, 1
    for hs, sh in zip(hosted, shapes):
        if hs.aliased:
            aliases.update({in_at + i: out_at + i for i in range(len(hs.operands))})
        in_at += len(hs.operands)
        out_at += len(sh)

    def body(buf_ref, *refs):
        at = [0]

        def take(n):
            at[0] += n
            return refs[at[0] - n:at[0]]

        h_in = [take(len(hs.operands)) for hs in hosted]
        (out_ref,) = take(1)
        h_out = [take(len(sh)) for sh in shapes]
        land, slots, red, sems_send, sems_recv = take(5)
        h_sems = [take(len(hs.sems)) for hs in hosted]
        for hs, a, b, s in zip(hosted, h_in, h_out, h_sems):
            hs.stages[0](a, b, s)
        x, y, c, chips = _place()
        me = 2 * x + y
        sibling = (x, y, 1 - c)
        first = _remote(buf_ref, land, sems_send.at[0], sems_recv.at[0], sibling)
        first.start()
        first.wait()
        mine = pl.ds(pl.multiple_of(c * half, 8), half)
        slots[me] = buf_ref[mine, :] + land[mine, :]
        sends = []
        for k, chip in enumerate(chips):
            cp = _remote(slots.at[me], slots.at[me], sems_send.at[1 + k], sems_recv.at[1 + k], (*chip, c))
            cp.start()
            sends.append(cp)
        for k, chip in enumerate(chips):
            got = slots.at[2 * chip[0] + chip[1]]
            _remote(got, got, sems_send.at[1 + k], sems_recv.at[1 + k], sibling).wait_recv()
        red[...] = ((slots[0] + slots[1]) + slots[2]) + slots[3]
        out_ref[mine, :] = red[...]
        last = _remote(red, out_ref.at[mine, :], sems_send.at[4], sems_recv.at[4], sibling)
        last.start()
        theirs = out_ref.at[pl.ds(pl.multiple_of((1 - c) * half, 8), half), :]
        _remote(red, theirs, sems_send.at[4], sems_recv.at[4], sibling).wait_recv()
        for cp in sends:
            cp.wait_send()
        last.wait_send()
        for hs, a, b, s in zip(hosted, h_in, h_out, h_sems):
            hs.stages[2](a, b, s)

    outs = pl.pallas_call(
        body, name="allreduce_small", in_specs=[VM] + [HB] * (in_at - 1), out_specs=[VM] + [HB] * (out_at - 1),
        out_shape=[jax.ShapeDtypeStruct((rows, w), F32)] + flat(shapes), input_output_aliases=aliases,
        scratch_shapes=[pltpu.VMEM((rows, w), F32), pltpu.VMEM((4, half, w), F32), pltpu.VMEM((half, w), F32),
                        pltpu.SemaphoreType.DMA((5,)), pltpu.SemaphoreType.DMA((5,))] + flat(hs.sems for hs in hosted),
        compiler_params=pltpu.CompilerParams(has_side_effects=True, vmem_limit_bytes=VMEM_LIMIT),
    )(buf, *flat(hs.operands for hs in hosted))
    results, at = [], 1
    for sh in shapes:
        results.append(outs[at:at + len(sh)])
        at += len(sh)
    return outs[0], results


BIG = ("w_in", "w_out", "wq", "wk", "wv", "wo", "w_up", "w_down")
MIXER, ATTN, MLP = ("w_in", "w_out"), ("wq", "wk", "wv", "wo"), ("w_up", "w_down")
BIG_AXIS = {"w_in": 1, "w_out": 0, "wq": 0, "wk": 0, "wv": 0, "wo": 0, "w_up": 1, "w_down": 0}
SMALL = ("norm_mix_g", "pool_w", "pool_scale", "sgu_g", "sgu_w", "sgu_b", "norm_xattn_g", "mem_norm_g", "norm_ffn_g",
         "conv_w", "conv_b", "final_norm_g")
ORDER = ("norm_mix_g", "w_in", "pool_w", "pool_scale", "sgu_g", "sgu_w", "sgu_b", "w_out", "norm_xattn_g", "mem_norm_g",
         "wq", "wk", "wv", "wo", "norm_ffn_g", "w_up", "conv_w", "conv_b", "w_down", "final_norm_g")
PACK_WIDTH = 512


def kernel(x, mem, norm_mix_g, w_in, pool_w, pool_scale, sgu_g, sgu_w, sgu_b, w_out, norm_xattn_g, mem_norm_g, wq, wk, wv, wo, norm_ffn_g, w_up, conv_w, conv_b, w_down, final_norm_g, loss_target, m_norm_mix_g, m_w_in, m_pool_w, m_pool_scale, m_sgu_g, m_sgu_w, m_sgu_b, m_w_out, m_norm_xattn_g, m_mem_norm_g, m_wq, m_wk, m_wv, m_wo, m_norm_ffn_g, m_w_up, m_conv_w, m_conv_b, m_w_down, m_final_norm_g, v_norm_mix_g, v_w_in, v_pool_w, v_pool_scale, v_sgu_g, v_sgu_w, v_sgu_b, v_w_out, v_norm_xattn_g, v_mem_norm_g, v_wq, v_wk, v_wv, v_wo, v_norm_ffn_g, v_w_up, v_conv_w, v_conv_b, v_w_down, v_final_norm_g):
    given = dict(locals())
    w = {n: given[n] for n in ORDER}
    mom = {n: given["m_" + n] for n in ORDER}
    var = {n: given["v_" + n] for n in ORDER}
    nl = w_in.shape[0]
    xs, mems, tgt = x[0], mem[0], loss_target[0]
    chip = 2 * lax.axis_index("x") + lax.axis_index("y")
    core = lax.axis_index("c")

    axes_of = lambda names: [BIG_AXIS[n] for n in names]
    alike = {}
    for n in BIG:
        alike.setdefault((w[n].shape, BIG_AXIS[n]), []).append(n)
    placed = [{} for _ in range(nl)]
    for (_, axis), names in alike.items():
        for n, per_layer in zip(names, _cast_place([w[n] for n in names], axis, chip, "place_" + names[0])):
            for l in range(nl):
                placed[l][n] = per_layer[l]
    conv_full = _allgather_conv(conv_w)

    def gather(names, l):
        return _hosted_allgather([placed[l][n] for n in names], axes_of(names))

    full = [dict(zip(MIXER, _run_hosted(gather(MIXER, 0), "allgather_weights")))]

    row = lambda a, l: a[l][None, :]
    saved = []
    h = xs
    for l in range(nl):
        fw = full[l]
        sbt = jnp.broadcast_to(sgu_b[l][:, :, None], sgu_w[l].shape)
        (h1, proj, xn1, mix), got = _mixer_fwd(h, row(norm_mix_g, l), fw["w_in"], pool_w[l], row(pool_scale, l), row(sgu_g, l), sgu_w[l], sbt, fw["w_out"],
                                               [gather(ATTN, 0), gather(("w_down",), 0)] if l == 0 else None)
        if l == 0:
            fw.update(zip(ATTN, got[0]))
            fw["w_down"] = got[1][0]
        k, v, memn = _kv_fwd(mems, row(mem_norm_g, l), fw["wk"], fw["wv"])
        (h2, q, o, xn2), got = _xattn_fwd(h1, row(norm_xattn_g, l), fw["wq"], k, v, fw["wo"], [gather(("w_up",), 0)] if l == 0 else None)
        if l == 0:
            fw["w_up"] = got[0][0]
        outs, got = _ffn_fwd(h2, row(norm_ffn_g, l), fw["w_up"], conv_full[l], row(conv_b, l), fw["w_down"],
                             [gather(BIG, l + 1)] if l + 1 < nl else None, None if l + 1 < nl else (final_norm_g[None, :], tgt))
        h3, hh, hc = outs[:3]
        if l + 1 < nl:
            full.append(dict(zip(BIG, got[0])))
        else:
            dh, loss_part, g_final = h3, outs[3], outs[4]
        saved.append(dict(h=h, h1=h1, h2=h2, proj=proj, xn1=xn1, mix=mix, k=k, v=v, memn=memn, q=q, o=o, xn2=xn2, hh=hh, hc=hc, sbt=sbt))
        h = h3


    big_grads = {}
    small_grads = [None] * nl

    def weight_grad(n, a, b, l, hosted=None):
        big_grads[n], got = _grad_matmul(a, b, "grad_" + n, l, nl, big_grads.get(n), hosted)
        return got

    sums, slots = {}, {}

    def exchange(names, l):
        return _hosted_exchange([big_grads[n] for n in names], axes_of(names), l)

    def scatter(names, l):
        return _hosted_scatter([sums[n, l] for n in names], axes_of(names))

    def add_casts(names, theirs, l):
        theirs = dict(zip(names, theirs))
        for group in alike.values():
            group = [n for n in group if n in theirs]
            if not group:
                continue
            gl, gr, gc = big_grads[group[0]].shape
            if BIG_AXIS[group[0]] == 0:
                outs = _add_cast([big_grads[n].reshape(gl * 4, 2, gr // 8, gc) for n in group], [theirs[n] for n in group],
                                 core, l * 4, "grad_chip_sum_" + group[0])
            else:
                outs = [o[0] for o in _add_cast([big_grads[n].reshape(gl, 2, gr // 2, gc) for n in group], [theirs[n][None] for n in group],
                                                core, l, "grad_chip_sum_" + group[0])]
            for n, o in zip(group, outs):
                sums[n, l] = o

    def keep_slots(names, got, l):
        for n, sl in zip(names, got):
            slots[n, l] = sl

    shard_grads = {}

    def shard_halves(names):
        out = []
        for n in names:
            buf = None
            for l in range(nl):
                buf = _sum_slots(sums[n, l], slots[n, l], BIG_AXIS[n], chip, core, l, nl, "grad_sum_" + n, buf)
            out.append(buf.reshape(nl, 2 * buf.shape[2], buf.shape[3]))
        return out

    for l in reversed(range(nl)):
        fw, s = full[l], saved[l]
        above = l + 1 < nl
        dh3 = dh
        (dh2, dhh, act, xn3, g_cw, g_cb, g_nf), got = _ffn_bwd(dh3, s["h2"], s["hh"], s["hc"], row(norm_ffn_g, l), fw["w_up"], conv_full[l], fw["w_down"],
                                                         [exchange(MIXER, l + 1), scatter(ATTN, l + 1)] if above else None)
        if above:
            add_casts(MIXER, got[0], l + 1)
            keep_slots(ATTN, got[1], l + 1)
        weight_grad("w_up", xn3, dhh, l)
        weight_grad("w_down", act, dh3, l)
        (dh1, dq, dk, dv, g_nx), got = _xattn_bwd(dh2, s["h1"], s["q"], row(norm_xattn_g, l), fw["wq"], s["k"], s["v"], fw["wo"],
                                                  [exchange(MLP, l), scatter(MIXER, l + 1) if above else None])
        add_casts(MLP, got[0], l)
        if above:
            keep_slots(MIXER, got[1], l + 1)
        weight_grad("wq", s["xn2"], dq, l)
        weight_grad("wo", s["o"], dh2, l)
        weight_grad("wk", s["memn"], dk, l)
        weight_grad("wv", s["memn"], dv, l)
        g_mn = _kv_bwd(dk, dv, mems, fw["wk"], fw["wv"])
        (dh0, dproj, g_nm, g_pw, g_ps, g_sg, g_sw, g_sbt), got = _mixer_bwd(dh1, s["h"], s["proj"], row(norm_mix_g, l), fw["w_in"], pool_w[l], row(pool_scale, l), row(sgu_g, l), sgu_w[l], s["sbt"], fw["w_out"],
                                                                           [scatter(MLP, l), exchange(ATTN, l)])
        keep_slots(MLP, got[0], l)
        add_casts(ATTN, got[1], l)
        half = len(ATTN) // 2
        got = weight_grad("w_in", s["xn1"], dproj, l, [scatter(ATTN[:half], l), _hosted_assemble(shard_halves(MLP))] if l == 0 else None)
        if l == 0:
            keep_slots(ATTN[:half], got[0], l)
            shard_grads.update(zip(MLP, got[1]))
        got = weight_grad("w_out", s["mix"], dh1, l, [scatter(ATTN[half:], l)] if l == 0 else None)
        if l == 0:
            keep_slots(ATTN[half:], got[0], l)
        small_grads[l] = dict(norm_mix_g=g_nm, pool_w=g_pw, pool_scale=g_ps, sgu_g=g_sg, sgu_w=g_sw, sgu_b=jnp.sum(g_sbt, axis=-1),
                              norm_xattn_g=g_nx, mem_norm_g=g_mn, norm_ffn_g=g_nf, conv_w=g_cw, conv_b=g_cb)
        dh = dh0
    grad_x = dh[None]

    add_casts(MIXER, _run_hosted(exchange(MIXER, 0), "grad_sibling_exchange"), 0)

    layered = [n for n in SMALL if n != "final_norm_g"]
    parts = [small_grads[l][n].reshape(-1, PACK_WIDTH) for n in layered for l in range(nl)]
    parts.append(g_final.reshape(-1, PACK_WIDTH))
    parts.append(jnp.pad(loss_part, ((0, 0), (0, PACK_WIDTH - 1))))
    used = sum(p.shape[0] for p in parts)
    total = -(-used // 16) * 16
    packed, got = _allreduce_small(jnp.concatenate(parts + [jnp.zeros((total - used, PACK_WIDTH), F32)], axis=0),
                                   [scatter(MIXER, 0), _hosted_assemble(shard_halves(ATTN))])
    keep_slots(MIXER, got[0], 0)
    shard_grads.update(zip(ATTN, got[1]))
    shard_grads.update(zip(MIXER, _run_hosted(_hosted_assemble(shard_halves(MIXER)), "grad_sibling_assemble")))

    delta, new_m, new_v, grads = {}, {}, {}, {}
    for names in alike.values():
        outs = _adamw_big([w[n] for n in names], [shard_grads[n] for n in names], [mom[n] for n in names], [var[n] for n in names], "adamw_" + names[0])
        for n, out in zip(names, outs):
            grads[n], delta[n], new_m[n], new_v[n] = out
    at = 0
    for n in layered:
        per_layer = []
        for l in range(nl):
            shape = small_grads[l][n].shape
            nrow = small_grads[l][n].size // PACK_WIDTH
            per_layer.append(packed[at:at + nrow].reshape(shape))
            at += nrow
        g = jnp.stack(per_layer)
        if n == "conv_w":
            cs = conv_w.shape[2]
            g = lax.dynamic_slice_in_dim(g, chip * cs, cs, axis=2)
        grads[n] = g.reshape(w[n].shape)
    grads["final_norm_g"] = packed[at:at + g_final.size // PACK_WIDTH].reshape(final_norm_g.shape)
    at += g_final.size // PACK_WIDTH
    loss = packed[at, 0]

    two_d = lambda a: a.reshape(-1, a.shape[-1])
    ds, nms, nvs = _adamw_small([two_d(w[n]) for n in SMALL], [two_d(grads[n]) for n in SMALL],
                                [two_d(mom[n]) for n in SMALL], [two_d(var[n]) for n in SMALL])
    for n, d_, m_, v_ in zip(SMALL, ds, nms, nvs):
        delta[n], new_m[n], new_v[n] = d_.reshape(w[n].shape), m_.reshape(w[n].shape), v_.reshape(w[n].shape)

    return (loss, grad_x, *[grads[n] for n in ORDER], *[delta[n] for n in ORDER], *[new_m[n] for n in ORDER], *[new_v[n] for n in ORDER])
```

```python
import math
from typing import NamedTuple

import jax
import jax.numpy as jnp
from jax import lax
from jax.experimental import pallas as pl
from jax.experimental.pallas import tpu as pltpu

F32 = jnp.float32
_MXU = jnp.bfloat16
_PAY = jnp.bfloat16
EPS = 1e-6
WINDOWS = (2, 4, 8, 16)
GROUP = 128
N_XHEADS = 4
HALO = 16
FF_TILE = 256
DOWN_TILES = 6
LATE_MIDDLE = 0.875
VMEM_LIMIT = 60 * 1024 * 1024
MESH = pl.DeviceIdType.MESH

ADAM_LR, ADAM_B1, ADAM_B2, ADAM_EPS, ADAM_WD, ADAM_STEP = 0.001, 0.9, 0.999, 1e-08, 0.01, 10

VM = pl.BlockSpec(memory_space=pltpu.VMEM)
HB = pl.BlockSpec(memory_space=pltpu.HBM)


def _nn(a, b):
    return jnp.dot(a, b, preferred_element_type=F32)


def _nt(a, b):
    return lax.dot_general(a, b, (((1,), (1,)), ((), ())), preferred_element_type=F32)


def _tn(a, b):
    return lax.dot_general(a, b, (((0,), (0,)), ((), ())), preferred_element_type=F32)


def _rms(x):
    r = lax.rsqrt(jnp.mean(x * x, axis=-1, keepdims=True) + EPS)
    return x * r, r


def _rms_bwd(dxn, xhat, r, g):
    dxh = dxn * g
    dx = r * (dxh - xhat * jnp.mean(dxh * xhat, axis=-1, keepdims=True))
    return dx, jnp.sum(dxn * xhat, axis=0, keepdims=True)


def _gelu(x):
    cdf = 0.5 * (1.0 + lax.erf(x * (2.0 ** -0.5)))
    return x * cdf, cdf


def _gelu_grad(x, cdf):
    return cdf + x * jnp.exp(-0.5 * x * x) * ((2.0 * math.pi) ** -0.5)


def _params(sem=None):
    return pltpu.CompilerParams(dimension_semantics=sem, vmem_limit_bytes=VMEM_LIMIT)


def _token_block(t, want):
    return want if t % want == 0 and t > want else GROUP


def _const_spec(shape):
    n = len(shape)
    return pl.BlockSpec(shape, lambda i: (0,) * n)


def _tril():
    return lax.broadcasted_iota(jnp.int32, (GROUP, GROUP), 0) >= lax.broadcasted_iota(jnp.int32, (GROUP, GROUP), 1)


def _shift_rows(x, k, edge):
    tb = x.shape[0]
    r8 = lax.broadcasted_iota(jnp.int32, (8, 1), 0)
    rolled = pltpu.roll(x, k % tb, 0)
    if k > 0:
        top = jnp.where(r8 < k, pltpu.roll(edge, k, 0), rolled[0:8, :])
        return jnp.concatenate([top, rolled[8:, :]], axis=0)
    bottom = jnp.where(r8 >= 8 + k, pltpu.roll(edge, 8 + k, 0), rolled[tb - 8:, :])
    return jnp.concatenate([rolled[:tb - 8, :], bottom], axis=0)


def _in_turns(parts):
    parts = list(parts)
    while parts:
        for p in list(parts):
            try:
                next(p)
            except StopIteration:
                parts.remove(p)


class _Hosted(NamedTuple):
    operands: tuple
    aliased: bool
    out_shapes: tuple
    sems: tuple
    stages: tuple


def _hosted_results(hosted):
    if hosted.aliased:
        return [jax.ShapeDtypeStruct(o.shape, o.dtype) for o in hosted.operands]
    return list(hosted.out_shapes)


def _call_hosting(main_body, hosted, *, name, steps, in_specs, out_specs, out_shape, scratch_shapes, operands, aliases=None, middle_at=0.75):
    grid = steps if isinstance(steps, tuple) else (steps,)
    semantics = ("arbitrary",) * len(grid)
    hosted = [hs for hs in (hosted or ()) if hs is not None]
    if not hosted:
        outs = pl.pallas_call(main_body, name=name, grid=grid, in_specs=in_specs, out_specs=out_specs, out_shape=out_shape,
                              scratch_shapes=scratch_shapes, input_output_aliases=aliases or {}, compiler_params=_params(semantics))(*operands)
        return outs, ()
    n_in, n_out, n_sc = len(in_specs), len(out_specs), len(scratch_shapes)
    shapes = [_hosted_results(hs) for hs in hosted]
    aliases, in_at, out_at = dict(aliases or {}), n_in, n_out
    for hs, sh in zip(hosted, shapes):
        if hs.aliased:
            aliases.update({in_at + i: out_at + i for i in range(len(hs.operands))})
        in_at += len(hs.operands)
        out_at += len(sh)

    def body(*refs):
        at = [0]

        def take(n):
            at[0] += n
            return refs[at[0] - n:at[0]]

        ins = take(n_in)
        h_in = [take(len(hs.operands)) for hs in hosted]
        outs = take(n_out)
        h_out = [take(len(sh)) for sh in shapes]
        scratch = take(n_sc)
        h_sems = [take(len(hs.sems)) for hs in hosted]
        ids = [pl.program_id(a) for a in range(len(grid))]

        def at_step(where):
            lead, rest = where
            ok = ids[0] == lead
            for a in range(1, len(grid)):
                ok = jnp.logical_and(ok, ids[a] == (grid[a] - 1 if rest else 0))
            return ok

        def run(stage):
            for hs, a, b, c in zip(hosted, h_in, h_out, h_sems):
                if hs.stages[stage] is not None:
                    hs.stages[stage](a, b, c)

        @pl.when(at_step((0, 0)))
        def _():
            run(0)

        if any(hs.stages[1] is not None for hs in hosted):
            @pl.when(at_step((min(int(middle_at * grid[0]), grid[0] - 1), 0)))
            def _():
                run(1)

        main_body(*ins, *outs, *scratch)

        @pl.when(at_step((grid[0] - 1, -1)))
        def _():
            run(2)

    flat = lambda lists: [x for xs in lists for x in xs]
    outs = pl.pallas_call(
        body, name=name, grid=grid, in_specs=list(in_specs) + [HB] * (in_at - n_in), out_specs=list(out_specs) + [HB] * (out_at - n_out),
        out_shape=list(out_shape) + flat(shapes), scratch_shapes=list(scratch_shapes) + flat(hs.sems for hs in hosted),
        input_output_aliases=aliases, compiler_params=_params(semantics),
    )(*operands, *flat(hs.operands for hs in hosted))
    results, at = [], n_out
    for sh in shapes:
        results.append(outs[at:at + len(sh)])
        at += len(sh)
    return outs[:n_out], results


def _run_hosted(hosted, name):
    nh = len(hosted.operands)
    h_shapes = _hosted_results(hosted)

    def body(*refs):
        h_in, h_out, h_sems = refs[:nh], refs[nh:nh + len(h_shapes)], refs[nh + len(h_shapes):]
        for stage in hosted.stages:
            if stage is not None:
                stage(h_in, h_out, h_sems)

    return pl.pallas_call(
        body, name=name, in_specs=[HB] * nh, out_specs=[HB] * len(h_shapes), out_shape=h_shapes, scratch_shapes=list(hosted.sems),
        input_output_aliases={i: i for i in range(nh)} if hosted.aliased else {},
        compiler_params=pltpu.CompilerParams(has_side_effects=True),
    )(*hosted.operands)


def _window_sums(e, win, back):
    n = e.shape[0]
    k = 1
    while k < win:
        e = e + pltpu.roll(e, k if back else n - k, 0)
        k *= 2
    return e


def _pool_diff(prev, p, t0, gi, win):
    sl = slice(gi * GROUP, (gi + 1) * GROUP)
    tb = p.shape[0]
    s = _window_sums(jnp.concatenate([prev[:, sl], p[:, sl]], axis=0), win, True)[HALO:, :]
    tglob = t0 + lax.broadcasted_iota(jnp.int32, (tb, 1), 0)
    cnt = jnp.minimum(tglob + 1, win).astype(F32)
    return s / cnt - p[:, sl], cnt


def _layernorm(v):
    xc = v - jnp.mean(v, axis=-1, keepdims=True)
    rstd = lax.rsqrt(jnp.mean(xc * xc, axis=-1, keepdims=True) + EPS)
    return xc * rstd, rstd


def _mixer_fwd(h, g, w_in, pool_w, pool_scale, sgu_g, sgu_w, sgu_bt, w_out, hosted=None):
    t, d = h.shape
    pw = pool_w.shape[0] * GROUP
    sw = sgu_w.shape[0] * GROUP
    tb = _token_block(t, 512)

    def body(h_ref, g_ref, win_ref, pw_ref, ps_ref, sg_ref, sw_ref, sbt_ref, wout_ref, h1_ref, proj_ref, xn_ref, mix_ref, pext):
        i = pl.program_id(0)

        @pl.when(i == 0)
        def _():
            pext[...] = jnp.zeros((HALO, pw), F32)

        x = h_ref[...]
        xhat, _ = _rms(x)
        xn = (xhat * g_ref[...]).astype(_MXU)
        xn_ref[...] = xn
        proj = _nn(xn, win_ref[...])
        proj_ref[...] = proj
        p = proj[:, :pw]
        prev = pext[...]
        for gi, win in enumerate(WINDOWS):
            sl = slice(gi * GROUP, (gi + 1) * GROUP)
            dg, _ = _pool_diff(prev, p, i * tb, gi, win)
            e = _nn(dg.astype(_MXU), pw_ref[gi].astype(_MXU))
            mix_ref[:, sl] = (e * ps_ref[:, sl]).astype(_MXU)
        pext[...] = p[tb - HALO:tb, :]
        uv, _ = _gelu(proj[:, pw:])
        u = uv[:, :sw]
        vhat, _ = _layernorm(uv[:, sw:])
        vn = (vhat * sg_ref[...]).astype(_MXU)
        mask = _tril()
        chunks = [slice(n * GROUP, (n + 1) * GROUP) for n in range(tb // GROUP)]
        for hh in range(sw // GROUP):
            wm = jnp.where(mask, sw_ref[hh], 0.0).astype(_MXU)
            cols = slice(hh * GROUP, (hh + 1) * GROUP)
            z = _nn(wm, jnp.concatenate([vn[rows, cols] for rows in chunks], axis=1))
            for n, rows in enumerate(chunks):
                mix_ref[rows, pw + hh * GROUP:pw + (hh + 1) * GROUP] = (u[rows, cols] * (z[:, chunks[n]] + sbt_ref[hh])).astype(_MXU)
        h1_ref[...] = x + _nn(mix_ref[...], wout_ref[...])

    blk = lambda w: pl.BlockSpec((tb, w), lambda i: (i, 0))
    return _call_hosting(
        body, hosted, name="mixer_fwd", steps=t // tb,
        in_specs=[blk(d), VM, VM, VM, VM, VM, VM, VM, VM],
        out_specs=[blk(d), blk(w_in.shape[1]), blk(d), blk(d)],
        out_shape=[jax.ShapeDtypeStruct((t, d), F32), jax.ShapeDtypeStruct((t, w_in.shape[1]), F32),
                   jax.ShapeDtypeStruct((t, d), _MXU), jax.ShapeDtypeStruct((t, d), _MXU)],
        scratch_shapes=[pltpu.VMEM((HALO, pw), F32)],
        operands=(h, g, w_in, pool_w, pool_scale, sgu_g, sgu_w, sgu_bt, w_out), middle_at=LATE_MIDDLE)


def _mixer_bwd(dh1, h, proj, g, w_in, pool_w, pool_scale, sgu_g, sgu_w, sgu_bt, w_out, hosted=None):
    t, d = h.shape
    ng, nh = pool_w.shape[0], sgu_w.shape[0]
    pw, sw = ng * GROUP, nh * GROUP
    tb = _token_block(t, 512)
    nb = t // tb
    n_parts = 2 if tb % (2 * GROUP) == 0 else 1
    pt = tb // n_parts

    def body(dh1_ref, h_ref, proj_ref, halo_ref, g_ref, win_ref, pw_ref, ps_ref, sg_ref, sw_ref, sbt_ref, wout_ref,
             dh_ref, dproj_ref, gg_ref, gpw_ref, gps_ref, gsg_ref, gsw_ref, gsbt_ref, dext, duv):
        i = pl.program_id(0)
        blk = nb - 1 - i

        @pl.when(i == 0)
        def _():
            for r in (gg_ref, gpw_ref, gps_ref, gsg_ref, gsw_ref, gsbt_ref, dext):
                r[...] = jnp.zeros(r.shape, F32)

        mask = _tril()

        def part(at):
            rows = slice(at, at + pt)
            dh1v = dh1_ref[rows, :]
            dmix = _nt(dh1v.astype(_MXU), wout_ref[...])
            yield
            proj_v = proj_ref[rows, :]
            p = proj_v[:, :pw]
            prev = jnp.where(blk == 0, 0.0, halo_ref[...]) if at == 0 else proj_ref[at - HALO:at, 0:pw]
            for gi, win in enumerate(WINDOWS):
                sl = slice(gi * GROUP, (gi + 1) * GROUP)
                dg, cnt = _pool_diff(prev, p, blk * tb + at, gi, win)
                dgm = dg.astype(_MXU)
                pwm = pw_ref[gi].astype(_MXU)
                e = _nn(dgm, pwm)
                dy = dmix[:, sl]
                gps_ref[:, sl] += jnp.sum(dy * e, axis=0, keepdims=True)
                de = (dy * ps_ref[:, sl]).astype(_MXU)
                gpw_ref[gi] += _tn(dgm, de)
                dd = _nt(de, pwm)
                ddc = dd / cnt
                acc = _window_sums(jnp.concatenate([ddc, dext[:, sl]], axis=0), win, False)[:pt, :]
                dext[:, sl] = ddc[0:HALO, :]
                dproj_ref[rows, sl] = (acc - dd).astype(_MXU)
            yield
            pre = proj_v[:, pw:]
            uv, cdf = _gelu(pre)
            u = uv[:, :sw]
            vhat, rstd = _layernorm(uv[:, sw:])
            vn = (vhat * sg_ref[...]).astype(_MXU)
            chunks = [slice(n * GROUP, (n + 1) * GROUP) for n in range(pt // GROUP)]
            side_by_side = lambda a, cols: jnp.concatenate([a[c, cols] for c in chunks], axis=1)
            for hh in range(nh):
                wm = jnp.where(mask, sw_ref[hh], 0.0).astype(_MXU)
                cols = slice(hh * GROUP, (hh + 1) * GROUP)
                vs = side_by_side(vn, cols)
                z = _nn(wm, vs)
                dy = side_by_side(dmix, slice(pw + hh * GROUP, pw + (hh + 1) * GROUP))
                dz = dy * side_by_side(u, cols)
                dzm = dz.astype(_MXU)
                dvs = _tn(wm, dzm)
                gsw_ref[hh] += jnp.where(mask, _nt(dzm, vs), 0.0)
                gb = jnp.zeros((GROUP, GROUP), F32)
                for n, c in enumerate(chunks):
                    gb = gb + dz[:, c]
                    duv[at + n * GROUP:at + (n + 1) * GROUP, cols] = dy[:, c] * (z[:, c] + sbt_ref[hh])
                    duv[at + n * GROUP:at + (n + 1) * GROUP, sw + hh * GROUP:sw + (hh + 1) * GROUP] = dvs[:, c]
                gsbt_ref[hh] += gb
            yield
            dvn = duv[rows, sw:]
            gsg_ref[...] += jnp.sum(dvn * vhat, axis=0, keepdims=True)
            dxh = dvn * sg_ref[...]
            dv = rstd * (dxh - jnp.mean(dxh, axis=-1, keepdims=True) - vhat * jnp.mean(dxh * vhat, axis=-1, keepdims=True))
            gp = _gelu_grad(pre, cdf)
            dproj_ref[rows, pw:pw + sw] = (duv[rows, :sw] * gp[:, :sw]).astype(_MXU)
            dproj_ref[rows, pw + sw:] = (dv * gp[:, sw:]).astype(_MXU)
            dxn = _nt(dproj_ref[rows, :], win_ref[...])
            yield
            xhat, r = _rms(h_ref[rows, :])
            dx, gg = _rms_bwd(dxn, xhat, r, g_ref[...])
            gg_ref[...] += gg
            dh_ref[rows, :] = dh1v + dx

        _in_turns([part(at) for at in reversed(range(0, tb, pt))])

    rev = lambda w: pl.BlockSpec((tb, w), lambda i: (nb - 1 - i, 0))
    halo = pl.BlockSpec((HALO, pw), lambda i: (jnp.maximum((nb - 1 - i) * (tb // HALO) - 1, 0), 0))
    small = [(1, d), (ng, GROUP, GROUP), (1, pw), (1, sw), (nh, GROUP, GROUP), (nh, GROUP, GROUP)]
    return _call_hosting(
        body, hosted, name="mixer_bwd", steps=nb,
        in_specs=[rev(d), rev(d), rev(proj.shape[1]), halo, VM, VM, VM, VM, VM, VM, VM, VM],
        out_specs=[rev(d), rev(proj.shape[1])] + [_const_spec(s) for s in small],
        out_shape=[jax.ShapeDtypeStruct((t, d), F32), jax.ShapeDtypeStruct(proj.shape, _MXU)]
        + [jax.ShapeDtypeStruct(s, F32) for s in small],
        scratch_shapes=[pltpu.VMEM((HALO, pw), F32), pltpu.VMEM((tb, 2 * sw), F32)],
        operands=(dh1, h, proj, proj, g, w_in, pool_w, pool_scale, sgu_g, sgu_w, sgu_bt, w_out))


def _kv_fwd(mem, gm, wk, wv):
    n, d = mem.shape

    def body(mem_ref, gm_ref, wk_ref, wv_ref, k_ref, v_ref, memn_ref):
        xhat, _ = _rms(mem_ref[...])
        memn = (xhat * gm_ref[...]).astype(_MXU)
        memn_ref[...] = memn
        k_ref[...] = _nn(memn, wk_ref[...]).astype(_MXU)
        v_ref[...] = _nn(memn, wv_ref[...]).astype(_MXU)

    return pl.pallas_call(
        body, name="kv_fwd", in_specs=[VM] * 4, out_specs=[VM] * 3,
        out_shape=[jax.ShapeDtypeStruct((n, d), _MXU)] * 3, compiler_params=_params(),
    )(mem, gm, wk, wv)


def _kv_bwd(dk, dv, mem, wk, wv):
    n, d = mem.shape

    def body(dk_ref, dv_ref, mem_ref, wk_ref, wv_ref, ggm_ref):
        dmemn = _nt(dk_ref[...].astype(_MXU), wk_ref[...]) + _nt(dv_ref[...].astype(_MXU), wv_ref[...])
        xhat, _ = _rms(mem_ref[...])
        ggm_ref[...] = jnp.sum(dmemn * xhat, axis=0, keepdims=True)

    return pl.pallas_call(
        body, name="kv_bwd", in_specs=[VM] * 5, out_specs=VM,
        out_shape=jax.ShapeDtypeStruct((1, d), F32), compiler_params=_params(),
    )(dk, dv, mem, wk, wv)


def _softmax(s):
    e = jnp.exp(s - jnp.max(s, axis=-1, keepdims=True))
    return e / jnp.sum(e, axis=-1, keepdims=True)


def _one_ahead(n, issue):
    nxt = issue(0)
    for a in range(n):
        cur = nxt
        if a + 1 < n:
            nxt = issue(a + 1)
        yield a, cur


def _xattn_fwd(h, g, wq, k, v, wo, hosted=None):
    t, d = h.shape
    hd = d // N_XHEADS
    scale = hd ** -0.5
    tb = _token_block(t, 512)

    def body(h_ref, g_ref, wq_ref, k_ref, v_ref, wo_ref, h2_ref, q_ref, o_ref, xn_ref):
        x = h_ref[...]
        xhat, _ = _rms(x)
        xn = (xhat * g_ref[...]).astype(_MXU)
        xn_ref[...] = xn
        qm = _nn(xn, wq_ref[...]).astype(_MXU)
        q_ref[...] = qm
        heads = [slice(a * hd, (a + 1) * hd) for a in range(N_XHEADS)]
        for a, s in _one_ahead(N_XHEADS, lambda a: _nt(qm[:, heads[a]], k_ref[:, heads[a]]) * scale):
            o_ref[:, heads[a]] = _nn(_softmax(s).astype(_MXU), v_ref[:, heads[a]]).astype(_MXU)
        h2_ref[...] = x + _nn(o_ref[...], wo_ref[...])

    blk = pl.BlockSpec((tb, d), lambda i: (i, 0))
    return _call_hosting(
        body, hosted, name="xattn_fwd", steps=t // tb,
        in_specs=[blk, VM, VM, VM, VM, VM], out_specs=[blk] * 4,
        out_shape=[jax.ShapeDtypeStruct((t, d), F32)] + [jax.ShapeDtypeStruct((t, d), _MXU)] * 3,
        scratch_shapes=[], operands=(h, g, wq, k, v, wo), middle_at=LATE_MIDDLE)


def _xattn_bwd(dh2, h, q, g, wq, k, v, wo, hosted=None):
    t, d = h.shape
    n = k.shape[0]
    hd = d // N_XHEADS
    scale = hd ** -0.5
    tb = _token_block(t, 512)
    pt = tb // 2 if tb % (2 * GROUP) == 0 else tb

    def body(dh2_ref, h_ref, q_ref, g_ref, wq_ref, k_ref, v_ref, wo_ref, dh_ref, dq_ref, dk_ref, dv_ref, gg_ref):
        @pl.when(pl.program_id(0) == 0)
        def _():
            for r in (dk_ref, dv_ref, gg_ref):
                r[...] = jnp.zeros(r.shape, F32)

        heads = [slice(a * hd, (a + 1) * hd) for a in range(N_XHEADS)]

        def part(at):
            rows = slice(at, at + pt)
            dh2v = dh2_ref[rows, :]
            dom = _nt(dh2v.astype(_MXU), wo_ref[...]).astype(_MXU)
            yield
            issue = lambda a: (_nt(q_ref[rows, heads[a]], k_ref[:, heads[a]]) * scale, _nt(dom[:, heads[a]], v_ref[:, heads[a]]))
            for a, (s, dpr) in _one_ahead(N_XHEADS, issue):
                sl = heads[a]
                pr = _softmax(s)
                dv_ref[:, sl] += _tn(pr.astype(_MXU), dom[:, sl])
                ds = (pr * (dpr - jnp.sum(dpr * pr, axis=-1, keepdims=True)) * scale).astype(_MXU)
                dq_ref[rows, sl] = _nn(ds, k_ref[:, sl]).astype(_MXU)
                dk_ref[:, sl] += _tn(ds, q_ref[rows, sl])
                yield
            dxn = _nt(dq_ref[rows, :], wq_ref[...])
            yield
            xhat, r = _rms(h_ref[rows, :])
            dx, gg = _rms_bwd(dxn, xhat, r, g_ref[...])
            gg_ref[...] += gg
            dh_ref[rows, :] = dh2v + dx

        _in_turns([part(at) for at in range(0, tb, pt)])

    blk = pl.BlockSpec((tb, d), lambda i: (i, 0))
    return _call_hosting(
        body, hosted, name="xattn_bwd", steps=t // tb,
        in_specs=[blk, blk, blk, VM, VM, VM, VM, VM],
        out_specs=[blk, blk, _const_spec((n, d)), _const_spec((n, d)), _const_spec((1, d))],
        out_shape=[jax.ShapeDtypeStruct((t, d), F32), jax.ShapeDtypeStruct((t, d), _MXU),
                   jax.ShapeDtypeStruct((n, d), F32), jax.ShapeDtypeStruct((n, d), F32), jax.ShapeDtypeStruct((1, d), F32)],
        scratch_shapes=[], operands=(dh2, h, q, g, wq, k, v, wo))


def _ffn_fwd(h, g, w_up, conv_w, conv_b, w_down, hosted=None, head=None):
    t, d = h.shape
    f = w_down.shape[0]
    ft = FF_TILE
    tb = _token_block(t, 256)

    def body(h_ref, g_ref, wup_ref, cw_ref, cb_ref, wdown_ref, *rest):
        if head is None:
            h3_ref, hh_ref, hc_ref, ext, carry, act_sc = rest
        else:
            gf_ref, tgt_ref, h3_ref, hh_ref, hc_ref, loss_ref, ggf_ref, ext, carry, act_sc = rest

        @pl.when(pl.program_id(0) == 0)
        def _():
            carry[...] = jnp.zeros(carry.shape, F32)
            if head is not None:
                loss_ref[...] = jnp.zeros(loss_ref.shape, F32)
                ggf_ref[...] = jnp.zeros(ggf_ref.shape, F32)

        x = h_ref[...]
        xhat, _ = _rms(x)
        xn = (xhat * g_ref[...]).astype(_MXU)
        acc = jnp.zeros((tb, d), F32)
        up = lambda j: [_nn(xn, wup_ref[:, off:off + ft]) for off in (j * ft, f + j * ft)]
        up_next = up(0)
        for j in range(f // ft):
            hc = []
            up_cur = up_next
            if j + 1 < f // ft:
                up_next = up(j + 1)
            for part, off in enumerate((j * ft, f + j * ft)):
                cols = slice(off, off + ft)
                cur = up_cur[part]
                hh_ref[:, cols] = cur.astype(_MXU)
                ext[part, 0:8, :] = carry[:, cols]
                ext[part, 8:8 + tb, :] = cur
                carry[:, cols] = cur[tb - 8:tb, :]
                hc.append(cb_ref[:, cols] + cw_ref[0:1, cols] * ext[part, 6:6 + tb, :]
                          + cw_ref[1:2, cols] * ext[part, 7:7 + tb, :] + cw_ref[2:3, cols] * cur)
                hc_ref[:, cols] = hc[part].astype(_MXU)
            at = j % DOWN_TILES
            act_sc[:, at * ft:(at + 1) * ft] = (hc[0] * jax.nn.sigmoid(hc[0]) * hc[1]).astype(_MXU)
            if at + 1 == DOWN_TILES or j + 1 == f // ft:
                acc = acc + _nn(act_sc[:, 0:(at + 1) * ft], wdown_ref[(j - at) * ft:(j + 1) * ft, :])
        if head is None:
            h3_ref[...] = x + acc
        else:
            yhat, r = _rms(x + acc)
            err = yhat * gf_ref[...] - tgt_ref[...]
            loss_ref[...] += 0.5 * jnp.sum(jnp.sum(err * err, axis=-1, keepdims=True), axis=0, keepdims=True) / d
            dx, gg = _rms_bwd(err / d, yhat, r, gf_ref[...])
            ggf_ref[...] += gg
            h3_ref[...] = dx

    blk = lambda w: pl.BlockSpec((tb, w), lambda i: (i, 0))
    in_specs = [blk(d), VM, VM, VM, VM, VM]
    out_specs = [blk(d), blk(2 * f), blk(2 * f)]
    out_shape = [jax.ShapeDtypeStruct((t, d), F32), jax.ShapeDtypeStruct((t, 2 * f), _MXU), jax.ShapeDtypeStruct((t, 2 * f), _MXU)]
    operands = (h, g, w_up, conv_w, conv_b, w_down)
    if head is not None:
        in_specs += [VM, blk(d)]
        out_specs += [_const_spec((1, 1)), _const_spec((1, d))]
        out_shape += [jax.ShapeDtypeStruct((1, 1), F32), jax.ShapeDtypeStruct((1, d), F32)]
        operands += tuple(head)
    return _call_hosting(
        body, hosted, name="ffn_fwd", steps=t // tb, in_specs=in_specs, out_specs=out_specs, out_shape=out_shape,
        scratch_shapes=[pltpu.VMEM((2, 8 + tb, ft), F32), pltpu.VMEM((8, 2 * f), F32), pltpu.VMEM((tb, DOWN_TILES * ft), _MXU)],
        operands=operands)


def _ffn_bwd(dh3, h, hh, hc, g, w_up, conv_w, w_down, hosted=None):
    t, d = h.shape
    f = w_down.shape[0]
    ft = FF_TILE
    tb = _token_block(t, 256)
    nb = t // tb

    def body(dh3_ref, h_ref, hh_ref, hc_ref, g_ref, wup_ref, cw_ref, wdown_ref,
             dh_ref, dhh_ref, act_ref, xn_ref, gcw_ref, gcb_ref, gg_ref, dcarry):
        @pl.when(pl.program_id(0) == 0)
        def _():
            for r in (gcw_ref, gcb_ref, gg_ref, dcarry):
                r[...] = jnp.zeros(r.shape, F32)

        dh3v = dh3_ref[...]
        dhm = dh3v.astype(_MXU)
        dxn = jnp.zeros((tb, d), F32)
        dact_next = _nt(dhm, wdown_ref[0:ft, :])
        for j in range(f // ft):
            dact = dact_next
            if j + 1 < f // ft:
                dact_next = _nt(dhm, wdown_ref[(j + 1) * ft:(j + 2) * ft, :])
            gate = hc_ref[:, j * ft:(j + 1) * ft].astype(F32)
            val = hc_ref[:, f + j * ft:f + (j + 1) * ft].astype(F32)
            sg = jax.nn.sigmoid(gate)
            silu = gate * sg
            act_ref[:, j * ft:(j + 1) * ft] = (silu * val).astype(_MXU)
            dhc = (dact * val * sg * (1.0 + gate * (1.0 - sg)), dact * silu)
            for part, off in enumerate((j * ft, f + j * ft)):
                cols = slice(off, off + ft)
                dc = dhc[part]
                c0 = hh_ref[:, cols].astype(F32)
                after = dcarry[:, cols]
                ahead1 = _shift_rows(dc, -1, after)
                ahead2 = _shift_rows(dc, -2, after)
                dcarry[:, cols] = dc[0:8, :]
                gcb_ref[:, cols] += jnp.sum(dc, axis=0, keepdims=True)
                gcw_ref[0:1, cols] += jnp.sum(ahead2 * c0, axis=0, keepdims=True)
                gcw_ref[1:2, cols] += jnp.sum(ahead1 * c0, axis=0, keepdims=True)
                gcw_ref[2:3, cols] += jnp.sum(dc * c0, axis=0, keepdims=True)
                dhh = (cw_ref[2:3, cols] * dc + cw_ref[1:2, cols] * ahead1 + cw_ref[0:1, cols] * ahead2).astype(_MXU)
                dhh_ref[:, cols] = dhh
                dxn = dxn + _nt(dhh, wup_ref[:, cols])
        xhat, r = _rms(h_ref[...])
        xn_ref[...] = (xhat * g_ref[...]).astype(_MXU)
        dx, gg = _rms_bwd(dxn, xhat, r, g_ref[...])
        gg_ref[...] += gg
        dh_ref[...] = dh3v + dx

    rev = lambda w: pl.BlockSpec((tb, w), lambda i: (nb - 1 - i, 0))
    return _call_hosting(
        body, hosted, name="ffn_bwd", steps=nb,
        in_specs=[rev(d), rev(d), rev(2 * f), rev(2 * f), VM, VM, VM, VM],
        out_specs=[rev(d), rev(2 * f), rev(f), rev(d), _const_spec((3, 2 * f)), _const_spec((1, 2 * f)), _const_spec((1, d))],
        out_shape=[jax.ShapeDtypeStruct((t, d), F32), jax.ShapeDtypeStruct((t, 2 * f), _MXU), jax.ShapeDtypeStruct((t, f), _MXU),
                   jax.ShapeDtypeStruct((t, d), _MXU),
                   jax.ShapeDtypeStruct((3, 2 * f), F32), jax.ShapeDtypeStruct((1, 2 * f), F32), jax.ShapeDtypeStruct((1, d), F32)],
        scratch_shapes=[pltpu.VMEM((8, 2 * f), F32)],
        operands=(dh3, h, hh, hc, g, w_up, conv_w, w_down))


def _largest_tile(n, cap, mult=128):
    best = None
    for c in range(mult, min(n, cap) + 1, mult):
        if n % c == 0:
            best = c
    return best if best is not None else n


def _grad_matmul(a, b, name, layer, n_layers, into=None, hosted=None):
    t, m = a.shape
    n = b.shape[1]
    tm, tn, tk = _largest_tile(m, 1408), _largest_tile(n, 1536), _largest_tile(t, 2048)
    nk = t // tk

    def body(a_ref, b_ref, *rest):
        o_ref = rest[-1]

        @pl.when(pl.program_id(2) == 0)
        def _():
            o_ref[...] = jnp.zeros(o_ref.shape, F32)

        o_ref[...] += _tn(a_ref[...].astype(_MXU), b_ref[...].astype(_MXU))

    in_specs = [pl.BlockSpec((tk, tm), lambda i, j, k: (k, i)), pl.BlockSpec((tk, tn), lambda i, j, k: (k, j))]
    operands = (a, b)
    aliases = {}
    if into is not None:
        in_specs.append(pl.BlockSpec(memory_space=pl.ANY))
        operands = (a, b, into)
        aliases = {2: 0}
    (out,), got = _call_hosting(
        body, hosted, name=name, steps=(m // tm, n // tn, nk), in_specs=in_specs,
        out_specs=[pl.BlockSpec((None, tm, tn), lambda i, j, k: (layer, i, j))],
        out_shape=[jax.ShapeDtypeStruct((n_layers, m, n), F32)], scratch_shapes=[], operands=operands, aliases=aliases)
    return out, got


def _adamw_math(w, g, m, v):
    m = ADAM_B1 * m + (1.0 - ADAM_B1) * g
    v = ADAM_B2 * v + (1.0 - ADAM_B2) * (g * g)
    m_hat = m / (1.0 - ADAM_B1 ** ADAM_STEP)
    v_hat = v / (1.0 - ADAM_B2 ** ADAM_STEP)
    return -ADAM_LR * (m_hat / (jnp.sqrt(v_hat) + ADAM_EPS) + ADAM_WD * w), m, v


def _row_block(rows, cols, max_bytes=1 << 20, mult=16):
    best = None
    for r in range(mult, rows + 1, mult):
        if rows % r == 0 and r * cols * 4 <= max_bytes:
            best = r
    return best if best is not None else rows


def _adamw_big(ws, gs, ms, vs, name):
    n = len(ws)
    shape = ws[0].shape
    cols = shape[-1]
    flat = lambda a: a.reshape(-1, cols)
    rows = flat(ws[0]).shape[0]
    rb = _row_block(rows, cols, (2 << 20) // n)

    def body(*refs):
        for a in range(n):
            w_ref, g_ref, m_ref, v_ref = (refs[s * n + a] for s in range(4))
            go_ref, d_ref, nm_ref, nv_ref = (refs[(4 + s) * n + a] for s in range(4))
            g = g_ref[...]
            go_ref[...] = g
            d_ref[...], nm_ref[...], nv_ref[...] = _adamw_math(w_ref[...], g, m_ref[...], v_ref[...])

    blk = pl.BlockSpec((rb, cols), lambda i: (i, 0))
    outs = pl.pallas_call(
        body, name=name, grid=(rows // rb,), in_specs=[blk] * (4 * n), out_specs=[blk] * (4 * n),
        out_shape=[jax.ShapeDtypeStruct((rows, cols), F32)] * (4 * n), compiler_params=_params(("parallel",)),
    )(*[flat(a) for group in (ws, gs, ms, vs) for a in group])
    return [[outs[s * n + a].reshape(shape) for s in range(4)] for a in range(n)]


def _adamw_small(ws, gs, ms, vs):
    n = len(ws)

    def body(*refs):
        for a in range(n):
            w_ref, g_ref, m_ref, v_ref = (refs[s * n + a] for s in range(4))
            d_ref, nm_ref, nv_ref = (refs[(4 + s) * n + a] for s in range(3))
            d_ref[...], nm_ref[...], nv_ref[...] = _adamw_math(w_ref[...], g_ref[...], m_ref[...], v_ref[...])

    outs = pl.pallas_call(
        body, name="adamw_small", in_specs=[VM] * (4 * n), out_specs=[VM] * (3 * n),
        out_shape=[jax.ShapeDtypeStruct(w.shape, F32) for w in ws] * 3, compiler_params=_params(),
    )(*ws, *gs, *ms, *vs)
    return outs[:n], outs[n:2 * n], outs[2 * n:]


def _place():
    x, y, c = lax.axis_index("x"), lax.axis_index("y"), lax.axis_index("c")
    chips = [(1 - x, y), (x, 1 - y), (1 - x, 1 - y)]
    return x, y, c, chips


def _rows(start, size, mult=16):
    return pl.ds(pl.multiple_of(start, mult), size)


def _full_window(ref, axis, chip, half=None):
    r, c = ref.shape
    if axis == 0:
        rs = r // 4
        if half is None:
            return ref.at[_rows(chip * rs, rs), :]
        return ref.at[_rows(chip * rs + half * (rs // 2), rs // 2), :]
    cs = c // 4
    if half is None:
        return ref.at[:, _rows(chip * cs, cs, 128)]
    return ref.at[_rows(half * (r // 2), r // 2), _rows(chip * cs, cs, 128)]


def _remote(src, dst, send_sem, recv_sem, to):
    return pltpu.make_async_remote_copy(src_ref=src, dst_ref=dst, send_sem=send_sem, recv_sem=recv_sem,
                                        device_id=to, device_id_type=MESH)


def _scalars(*vals):
    return jnp.stack([jnp.asarray(v, jnp.int32) for v in vals])


def _cast_place(shards, axis, chip, name):
    n = len(shards)
    nl, rs, cs = shards[0].shape
    full = (rs * 4, cs) if axis == 0 else (rs, cs * 4)
    rb = _row_block(rs, cs, (4 << 20) // (n * nl))
    nrb = rs // rb

    def body(chip_ref, *refs):
        for a in range(n):
            for l in range(nl):
                refs[n + a * nl + l][...] = refs[a][l].astype(_PAY)

    if axis == 0:
        out_map = lambda i, chip_ref: (chip_ref[0] * nrb + i, 0)
    else:
        out_map = lambda i, chip_ref: (i, chip_ref[0])
    outs = pl.pallas_call(
        body, name=name,
        grid_spec=pltpu.PrefetchScalarGridSpec(
            num_scalar_prefetch=1, grid=(nrb,),
            in_specs=[pl.BlockSpec((nl, rb, cs), lambda i, chip_ref: (0, i, 0))] * n,
            out_specs=[pl.BlockSpec((rb, cs), out_map)] * (n * nl)),
        out_shape=[jax.ShapeDtypeStruct(full, _PAY)] * (n * nl), compiler_params=_params(("parallel",)),
    )(_scalars(chip), *shards)
    return [[outs[a * nl + l] for l in range(nl)] for a in range(n)]


def _hosted_allgather(placed, axes):
    n = len(placed)

    def each(outs, half_of):
        x, y, c, chips = _place()
        for i in range(n):
            for k, chip in enumerate(chips):
                yield i * 3 + k, (*chip, c), (x, y, 1 - c), _full_window(outs[i], axes[i], 2 * x + y, c), \
                    _full_window(outs[i], axes[i], 2 * chip[0] + chip[1], half_of(c))

    def start(_, outs, sems):
        send, recv, _, _ = sems
        for s, peer, _, mine, _ in each(outs, lambda c: c):
            _remote(mine, mine, send.at[s], recv.at[s], peer).start()

    def middle(_, outs, sems):
        send, recv, fsend, frecv = sems
        for s, _, sibling, _, got in each(outs, lambda c: c):
            _remote(got, got, send.at[s], recv.at[s], sibling).wait_recv()
            _remote(got, got, fsend.at[s], frecv.at[s], sibling).start()

    def finish(_, outs, sems):
        send, recv, fsend, frecv = sems
        for s, _, sibling, _, got in each(outs, lambda c: 1 - c):
            _remote(got, got, fsend.at[s], frecv.at[s], sibling).wait_recv()
        for s, peer, sibling, mine, got in each(outs, lambda c: c):
            _remote(mine, mine, send.at[s], recv.at[s], peer).wait_send()
            _remote(got, got, fsend.at[s], frecv.at[s], sibling).wait_send()

    return _Hosted(tuple(placed), True, (), (pltpu.SemaphoreType.DMA((n * 3,)),) * 4, (start, middle, finish))


def _allgather_conv(conv_shard):
    nl, taps, cs = conv_shard.shape

    def body(in_ref, out_ref, send, recv, local):
        x, y, c, chips = _place()
        mine = out_ref.at[:, :, _rows((2 * x + y) * cs, cs, 128)]
        own = pltpu.make_async_copy(in_ref, mine, local)
        own.start()
        sends = [_remote(in_ref, mine, send.at[k], recv.at[k], (*chip, c)) for k, chip in enumerate(chips)]
        for cp in sends:
            cp.start()
        for k, chip in enumerate(chips):
            got = out_ref.at[:, :, _rows((2 * chip[0] + chip[1]) * cs, cs, 128)]
            _remote(got, got, send.at[k], recv.at[k], (*chip, c)).wait_recv()
        for cp in sends:
            cp.wait_send()
        own.wait()

    return pl.pallas_call(
        body, name="allgather_conv", in_specs=[HB], out_specs=HB, out_shape=jax.ShapeDtypeStruct((nl, taps, cs * 4), conv_shard.dtype),
        scratch_shapes=[pltpu.SemaphoreType.DMA((3,)), pltpu.SemaphoreType.DMA((3,)), pltpu.SemaphoreType.DMA],
        compiler_params=pltpu.CompilerParams(has_side_effects=True),
    )(conv_shard)


def _hosted_exchange(grads, axes, layer):
    na = len(grads)
    views = [g.reshape(g.shape[0], 4, 2, g.shape[1] // 8, g.shape[2]) if ax == 0 else g for g, ax in zip(grads, axes)]

    def region(ref, axis, half):
        if axis == 0:
            return ref.at[layer, :, half]
        r = ref.shape[1]
        return ref.at[layer, _rows(half * (r // 2), r // 2), :]

    def copies(ins, land, sems):
        send, recv = sems
        x, y, c, _ = _place()
        return [_remote(region(ins[a], axes[a], 1 - c), land[a], send.at[a], recv.at[a], (x, y, 1 - c)) for a in range(na)]

    def start(ins, land, sems):
        for cp in copies(ins, land, sems):
            cp.start()

    def finish(ins, land, sems):
        for cp in copies(ins, land, sems):
            cp.wait()

    shapes = [(4, g.shape[1] // 8, g.shape[2]) if ax == 0 else (g.shape[1] // 2, g.shape[2]) for g, ax in zip(grads, axes)]
    return _Hosted(tuple(views), False, tuple(jax.ShapeDtypeStruct(s, F32) for s in shapes),
                   (pltpu.SemaphoreType.DMA((na,)),) * 2, (start, None, finish))


def _add_cast(mines, theirs, core, base, name):
    n = len(mines)
    na, nb, cols = theirs[0].shape
    rb = _row_block(nb, cols, (4 << 20) // n)

    def body(core_ref, *refs):
        for a in range(n):
            refs[2 * n + a][...] = (refs[a][...] + refs[n + a][...]).astype(_PAY)

    blk = pl.BlockSpec((None, rb, cols), lambda i, k, core_ref: (i, k, 0))
    return pl.pallas_call(
        body, name=name,
        grid_spec=pltpu.PrefetchScalarGridSpec(
            num_scalar_prefetch=1, grid=(na, nb // rb),
            in_specs=[pl.BlockSpec((None, None, rb, cols), lambda i, k, core_ref: (base + i, core_ref[0], k, 0))] * n + [blk] * n,
            out_specs=[blk] * n),
        out_shape=[jax.ShapeDtypeStruct((na, nb, cols), _PAY)] * n, compiler_params=_params(("parallel", "parallel")),
    )(_scalars(core), *mines, *theirs)


def _piece(ref, axis, chip):
    if axis == 0:
        return ref.at[chip]
    cs = ref.shape[1] // 4
    return ref.at[:, _rows(chip * cs, cs, 128)]


def _hosted_scatter(sums, axes):
    na = len(sums)

    def piece_shape(a):
        if axes[a] == 0:
            return (sums[a].shape[1], sums[a].shape[2])
        return (sums[a].shape[0], sums[a].shape[1] // 4)

    def copies(ins, slots, sems):
        send, recv = sems
        _, _, c, chips = _place()
        return [_remote(_piece(ins[a], axes[a], 2 * chip[0] + chip[1]), slots[a].at[k], send.at[a * 3 + k], recv.at[a * 3 + k], (*chip, c))
                for a in range(na) for k, chip in enumerate(chips)]

    def start(ins, slots, sems):
        for cp in copies(ins, slots, sems):
            cp.start()

    def finish(ins, slots, sems):
        for cp in copies(ins, slots, sems):
            cp.wait()

    return _Hosted(tuple(sums), False, tuple(jax.ShapeDtypeStruct((3,) + piece_shape(a), sums[a].dtype) for a in range(na)),
                   (pltpu.SemaphoreType.DMA((na * 3,)),) * 2, (start, None, finish))


def _sum_slots(sums, slots, axis, chip, core, layer, n_layers, name, into=None):
    _, hr, cs = slots.shape
    rb = _row_block(hr, cs, 4 << 20)

    def body(at_ref, own_ref, s_ref, *rest):
        rest[-1][...] = ((own_ref[...].astype(F32) + s_ref[0].astype(F32)) + s_ref[1].astype(F32)) + s_ref[2].astype(F32)

    if axis == 0:
        own = pl.BlockSpec((None, rb, cs), lambda k, at_ref: (at_ref[0], k, 0))
    else:
        own = pl.BlockSpec((rb, cs), lambda k, at_ref: (k, at_ref[0]))
    in_specs = [own, pl.BlockSpec((3, rb, cs), lambda k, at_ref: (0, k, 0))]
    operands = (sums, slots)
    aliases = {}
    if into is not None:
        in_specs.append(pl.BlockSpec(memory_space=pl.ANY))
        operands = (sums, slots, into)
        aliases = {3: 0}
    return pl.pallas_call(
        body, name=name,
        grid_spec=pltpu.PrefetchScalarGridSpec(
            num_scalar_prefetch=1, grid=(hr // rb,), in_specs=in_specs,
            out_specs=pl.BlockSpec((None, None, rb, cs), lambda k, at_ref: (layer, at_ref[1], k, 0))),
        out_shape=jax.ShapeDtypeStruct((n_layers, 2, hr, cs), F32), input_output_aliases=aliases,
        compiler_params=_params(("parallel",)),
    )(_scalars(chip, core), *operands)


def _hosted_assemble(shards):
    na = len(shards)

    def copies(_, outs, sems):
        send, recv = sems
        x, y, c, _ = _place()
        halves = [outs[a].at[:, _rows(c * (outs[a].shape[1] // 2), outs[a].shape[1] // 2), :] for a in range(na)]
        return [_remote(mine, mine, send.at[a], recv.at[a], (x, y, 1 - c)) for a, mine in enumerate(halves)]

    def start(ins, outs, sems):
        for cp in copies(ins, outs, sems):
            cp.start()

    def finish(ins, outs, sems):
        for cp in copies(ins, outs, sems):
            cp.wait()

    return _Hosted(tuple(shards), True, (), (pltpu.SemaphoreType.DMA((na,)),) * 2, (start, None, finish))


def _allreduce_small(buf, hosted):
    rows, w = buf.shape
    half = rows // 2
    shapes = [_hosted_results(hs) for hs in hosted]
    flat = lambda lists: [x for xs in lists for x in xs]
    aliases, in_at, out_at = {}, 1, 1
    for hs, sh in zip(hosted, shapes):
        if hs.aliased:
            aliases.update({in_at + i: out_at + i for i in range(len(hs.operands))})
        in_at += len(hs.operands)
        out_at += len(sh)

    def body(buf_ref, *refs):
        at = [0]

        def take(n):
            at[0] += n
            return refs[at[0] - n:at[0]]

        h_in = [take(len(hs.operands)) for hs in hosted]
        (out_ref,) = take(1)
        h_out = [take(len(sh)) for sh in shapes]
        land, slots, red, sems_send, sems_recv = take(5)
        h_sems = [take(len(hs.sems)) for hs in hosted]
        for hs, a, b, s in zip(hosted, h_in, h_out, h_sems):
            hs.stages[0](a, b, s)
        x, y, c, chips = _place()
        me = 2 * x + y
        sibling = (x, y, 1 - c)
        first = _remote(buf_ref, land, sems_send.at[0], sems_recv.at[0], sibling)
        first.start()
        first.wait()
        mine = pl.ds(pl.multiple_of(c * half, 8), half)
        slots[me] = buf_ref[mine, :] + land[mine, :]
        sends = []
        for k, chip in enumerate(chips):
            cp = _remote(slots.at[me], slots.at[me], sems_send.at[1 + k], sems_recv.at[1 + k], (*chip, c))
            cp.start()
            sends.append(cp)
        for k, chip in enumerate(chips):
            got = slots.at[2 * chip[0] + chip[1]]
            _remote(got, got, sems_send.at[1 + k], sems_recv.at[1 + k], sibling).wait_recv()
        red[...] = ((slots[0] + slots[1]) + slots[2]) + slots[3]
        out_ref[mine, :] = red[...]
        last = _remote(red, out_ref.at[mine, :], sems_send.at[4], sems_recv.at[4], sibling)
        last.start()
        theirs = out_ref.at[pl.ds(pl.multiple_of((1 - c) * half, 8), half), :]
        _remote(red, theirs, sems_send.at[4], sems_recv.at[4], sibling).wait_recv()
        for cp in sends:
            cp.wait_send()
        last.wait_send()
        for hs, a, b, s in zip(hosted, h_in, h_out, h_sems):
            hs.stages[2](a, b, s)

    outs = pl.pallas_call(
        body, name="allreduce_small", in_specs=[VM] + [HB] * (in_at - 1), out_specs=[VM] + [HB] * (out_at - 1),
        out_shape=[jax.ShapeDtypeStruct((rows, w), F32)] + flat(shapes), input_output_aliases=aliases,
        scratch_shapes=[pltpu.VMEM((rows, w), F32), pltpu.VMEM((4, half, w), F32), pltpu.VMEM((half, w), F32),
                        pltpu.SemaphoreType.DMA((5,)), pltpu.SemaphoreType.DMA((5,))] + flat(hs.sems for hs in hosted),
        compiler_params=pltpu.CompilerParams(has_side_effects=True, vmem_limit_bytes=VMEM_LIMIT),
    )(buf, *flat(hs.operands for hs in hosted))
    results, at = [], 1
    for sh in shapes:
        results.append(outs[at:at + len(sh)])
        at += len(sh)
    return outs[0], results


BIG = ("w_in", "w_out", "wq", "wk", "wv", "wo", "w_up", "w_down")
MIXER, ATTN, MLP = ("w_in", "w_out"), ("wq", "wk", "wv", "wo"), ("w_up", "w_down")
BIG_AXIS = {"w_in": 1, "w_out": 0, "wq": 0, "wk": 0, "wv": 0, "wo": 0, "w_up": 1, "w_down": 0}
SMALL = ("norm_mix_g", "pool_w", "pool_scale", "sgu_g", "sgu_w", "sgu_b", "norm_xattn_g", "mem_norm_g", "norm_ffn_g",
         "conv_w", "conv_b", "final_norm_g")
ORDER = ("norm_mix_g", "w_in", "pool_w", "pool_scale", "sgu_g", "sgu_w", "sgu_b", "w_out", "norm_xattn_g", "mem_norm_g",
         "wq", "wk", "wv", "wo", "norm_ffn_g", "w_up", "conv_w", "conv_b", "w_down", "final_norm_g")
PACK_WIDTH = 512


def kernel(x, mem, norm_mix_g, w_in, pool_w, pool_scale, sgu_g, sgu_w, sgu_b, w_out, norm_xattn_g, mem_norm_g, wq, wk, wv, wo, norm_ffn_g, w_up, conv_w, conv_b, w_down, final_norm_g, loss_target, m_norm_mix_g, m_w_in, m_pool_w, m_pool_scale, m_sgu_g, m_sgu_w, m_sgu_b, m_w_out, m_norm_xattn_g, m_mem_norm_g, m_wq, m_wk, m_wv, m_wo, m_norm_ffn_g, m_w_up, m_conv_w, m_conv_b, m_w_down, m_final_norm_g, v_norm_mix_g, v_w_in, v_pool_w, v_pool_scale, v_sgu_g, v_sgu_w, v_sgu_b, v_w_out, v_norm_xattn_g, v_mem_norm_g, v_wq, v_wk, v_wv, v_wo, v_norm_ffn_g, v_w_up, v_conv_w, v_conv_b, v_w_down, v_final_norm_g):
    given = dict(locals())
    w = {n: given[n] for n in ORDER}
    mom = {n: given["m_" + n] for n in ORDER}
    var = {n: given["v_" + n] for n in ORDER}
    nl = w_in.shape[0]
    xs, mems, tgt = x[0], mem[0], loss_target[0]
    chip = 2 * lax.axis_index("x") + lax.axis_index("y")
    core = lax.axis_index("c")

    axes_of = lambda names: [BIG_AXIS[n] for n in names]
    alike = {}
    for n in BIG:
        alike.setdefault((w[n].shape, BIG_AXIS[n]), []).append(n)
    placed = [{} for _ in range(nl)]
    for (_, axis), names in alike.items():
        for n, per_layer in zip(names, _cast_place([w[n] for n in names], axis, chip, "place_" + names[0])):
            for l in range(nl):
                placed[l][n] = per_layer[l]
    conv_full = _allgather_conv(conv_w)

    def gather(names, l):
        return _hosted_allgather([placed[l][n] for n in names], axes_of(names))

    full = [dict(zip(MIXER, _run_hosted(gather(MIXER, 0), "allgather_weights")))]

    row = lambda a, l: a[l][None, :]
    saved = []
    h = xs
    for l in range(nl):
        fw = full[l]
        sbt = jnp.broadcast_to(sgu_b[l][:, :, None], sgu_w[l].shape)
        (h1, proj, xn1, mix), got = _mixer_fwd(h, row(norm_mix_g, l), fw["w_in"], pool_w[l], row(pool_scale, l), row(sgu_g, l), sgu_w[l], sbt, fw["w_out"],
                                               [gather(ATTN, 0), gather(("w_down",), 0)] if l == 0 else None)
        if l == 0:
            fw.update(zip(ATTN, got[0]))
            fw["w_down"] = got[1][0]
        k, v, memn = _kv_fwd(mems, row(mem_norm_g, l), fw["wk"], fw["wv"])
        (h2, q, o, xn2), got = _xattn_fwd(h1, row(norm_xattn_g, l), fw["wq"], k, v, fw["wo"], [gather(("w_up",), 0)] if l == 0 else None)
        if l == 0:
            fw["w_up"] = got[0][0]
        outs, got = _ffn_fwd(h2, row(norm_ffn_g, l), fw["w_up"], conv_full[l], row(conv_b, l), fw["w_down"],
                             [gather(BIG, l + 1)] if l + 1 < nl else None, None if l + 1 < nl else (final_norm_g[None, :], tgt))
        h3, hh, hc = outs[:3]
        if l + 1 < nl:
            full.append(dict(zip(BIG, got[0])))
        else:
            dh, loss_part, g_final = h3, outs[3], outs[4]
        saved.append(dict(h=h, h1=h1, h2=h2, proj=proj, xn1=xn1, mix=mix, k=k, v=v, memn=memn, q=q, o=o, xn2=xn2, hh=hh, hc=hc, sbt=sbt))
        h = h3


    big_grads = {}
    small_grads = [None] * nl

    def weight_grad(n, a, b, l, hosted=None):
        big_grads[n], got = _grad_matmul(a, b, "grad_" + n, l, nl, big_grads.get(n), hosted)
        return got

    sums, slots = {}, {}

    def exchange(names, l):
        return _hosted_exchange([big_grads[n] for n in names], axes_of(names), l)

    def scatter(names, l):
        return _hosted_scatter([sums[n, l] for n in names], axes_of(names))

    def add_casts(names, theirs, l):
        theirs = dict(zip(names, theirs))
        for group in alike.values():
            group = [n for n in group if n in theirs]
            if not group:
                continue
            gl, gr, gc = big_grads[group[0]].shape
            if BIG_AXIS[group[0]] == 0:
                outs = _add_cast([big_grads[n].reshape(gl * 4, 2, gr // 8, gc) for n in group], [theirs[n] for n in group],
                                 core, l * 4, "grad_chip_sum_" + group[0])
            else:
                outs = [o[0] for o in _add_cast([big_grads[n].reshape(gl, 2, gr // 2, gc) for n in group], [theirs[n][None] for n in group],
                                                core, l, "grad_chip_sum_" + group[0])]
            for n, o in zip(group, outs):
                sums[n, l] = o

    def keep_slots(names, got, l):
        for n, sl in zip(names, got):
            slots[n, l] = sl

    shard_grads = {}

    def shard_halves(names):
        out = []
        for n in names:
            buf = None
            for l in range(nl):
                buf = _sum_slots(sums[n, l], slots[n, l], BIG_AXIS[n], chip, core, l, nl, "grad_sum_" + n, buf)
            out.append(buf.reshape(nl, 2 * buf.shape[2], buf.shape[3]))
        return out

    for l in reversed(range(nl)):
        fw, s = full[l], saved[l]
        above = l + 1 < nl
        dh3 = dh
        (dh2, dhh, act, xn3, g_cw, g_cb, g_nf), got = _ffn_bwd(dh3, s["h2"], s["hh"], s["hc"], row(norm_ffn_g, l), fw["w_up"], conv_full[l], fw["w_down"],
                                                         [exchange(MIXER, l + 1), scatter(ATTN, l + 1)] if above else None)
        if above:
            add_casts(MIXER, got[0], l + 1)
            keep_slots(ATTN, got[1], l + 1)
        weight_grad("w_up", xn3, dhh, l)
        weight_grad("w_down", act, dh3, l)
        (dh1, dq, dk, dv, g_nx), got = _xattn_bwd(dh2, s["h1"], s["q"], row(norm_xattn_g, l), fw["wq"], s["k"], s["v"], fw["wo"],
                                                  [exchange(MLP, l), scatter(MIXER, l + 1) if above else None])
        add_casts(MLP, got[0], l)
        if above:
            keep_slots(MIXER, got[1], l + 1)
        weight_grad("wq", s["xn2"], dq, l)
        weight_grad("wo", s["o"], dh2, l)
        weight_grad("wk", s["memn"], dk, l)
        weight_grad("wv", s["memn"], dv, l)
        g_mn = _kv_bwd(dk, dv, mems, fw["wk"], fw["wv"])
        (dh0, dproj, g_nm, g_pw, g_ps, g_sg, g_sw, g_sbt), got = _mixer_bwd(dh1, s["h"], s["proj"], row(norm_mix_g, l), fw["w_in"], pool_w[l], row(pool_scale, l), row(sgu_g, l), sgu_w[l], s["sbt"], fw["w_out"],
                                                                           [scatter(MLP, l), exchange(ATTN, l)])
        keep_slots(MLP, got[0], l)
        add_casts(ATTN, got[1], l)
        half = len(ATTN) // 2
        got = weight_grad("w_in", s["xn1"], dproj, l, [scatter(ATTN[:half], l), _hosted_assemble(shard_halves(MLP))] if l == 0 else None)
        if l == 0:
            keep_slots(ATTN[:half], got[0], l)
            shard_grads.update(zip(MLP, got[1]))
        got = weight_grad("w_out", s["mix"], dh1, l, [scatter(ATTN[half:], l)] if l == 0 else None)
        if l == 0:
            keep_slots(ATTN[half:], got[0], l)
        small_grads[l] = dict(norm_mix_g=g_nm, pool_w=g_pw, pool_scale=g_ps, sgu_g=g_sg, sgu_w=g_sw, sgu_b=jnp.sum(g_sbt, axis=-1),
                              norm_xattn_g=g_nx, mem_norm_g=g_mn, norm_ffn_g=g_nf, conv_w=g_cw, conv_b=g_cb)
        dh = dh0
    grad_x = dh[None]

    add_casts(MIXER, _run_hosted(exchange(MIXER, 0), "grad_sibling_exchange"), 0)

    layered = [n for n in SMALL if n != "final_norm_g"]
    parts = [small_grads[l][n].reshape(-1, PACK_WIDTH) for n in layered for l in range(nl)]
    parts.append(g_final.reshape(-1, PACK_WIDTH))
    parts.append(jnp.pad(loss_part, ((0, 0), (0, PACK_WIDTH - 1))))
    used = sum(p.shape[0] for p in parts)
    total = -(-used // 16) * 16
    packed, got = _allreduce_small(jnp.concatenate(parts + [jnp.zeros((total - used, PACK_WIDTH), F32)], axis=0),
                                   [scatter(MIXER, 0), _hosted_assemble(shard_halves(ATTN))])
    keep_slots(MIXER, got[0], 0)
    shard_grads.update(zip(ATTN, got[1]))
    shard_grads.update(zip(MIXER, _run_hosted(_hosted_assemble(shard_halves(MIXER)), "grad_sibling_assemble")))

    delta, new_m, new_v, grads = {}, {}, {}, {}
    for names in alike.values():
        outs = _adamw_big([w[n] for n in names], [shard_grads[n] for n in names], [mom[n] for n in names], [var[n] for n in names], "adamw_" + names[0])
        for n, out in zip(names, outs):
            grads[n], delta[n], new_m[n], new_v[n] = out
    at = 0
    for n in layered:
        per_layer = []
        for l in range(nl):
            shape = small_grads[l][n].shape
            nrow = small_grads[l][n].size // PACK_WIDTH
            per_layer.append(packed[at:at + nrow].reshape(shape))
            at += nrow
        g = jnp.stack(per_layer)
        if n == "conv_w":
            cs = conv_w.shape[2]
            g = lax.dynamic_slice_in_dim(g, chip * cs, cs, axis=2)
        grads[n] = g.reshape(w[n].shape)
    grads["final_norm_g"] = packed[at:at + g_final.size // PACK_WIDTH].reshape(final_norm_g.shape)
    at += g_final.size // PACK_WIDTH
    loss = packed[at, 0]

    two_d = lambda a: a.reshape(-1, a.shape[-1])
    ds, nms, nvs = _adamw_small([two_d(w[n]) for n in SMALL], [two_d(grads[n]) for n in SMALL],
                                [two_d(mom[n]) for n in SMALL], [two_d(var[n]) for n in SMALL])
    for n, d_, m_, v_ in zip(SMALL, ds, nms, nvs):
        delta[n], new_m[n], new_v[n] = d_.reshape(w[n].shape), m_.reshape(w[n].shape), v_.reshape(w[n].shape)

    return (loss, grad_x, *[grads[n] for n in ORDER], *[delta[n] for n in ORDER], *[new_m[n] for n in ORDER], *[new_v[n] for n in ORDER])
```

```python
import math
from typing import NamedTuple

import jax
import jax.numpy as jnp
from jax import lax
from jax.experimental import pallas as pl
from jax.experimental.pallas import tpu as pltpu

F32 = jnp.float32
_MXU = jnp.bfloat16
_PAY = jnp.bfloat16
EPS = 1e-6
WINDOWS = (2, 4, 8, 16)
GROUP = 128
N_XHEADS = 4
HALO = 16
FF_TILE = 256
DOWN_TILES = 6
LATE_MIDDLE = 0.875
VMEM_LIMIT = 60 * 1024 * 1024
MESH = pl.DeviceIdType.MESH

ADAM_LR, ADAM_B1, ADAM_B2, ADAM_EPS, ADAM_WD, ADAM_STEP = 0.001, 0.9, 0.999, 1e-08, 0.01, 10

VM = pl.BlockSpec(memory_space=pltpu.VMEM)
HB = pl.BlockSpec(memory_space=pltpu.HBM)


def _nn(a, b):
    return jnp.dot(a, b, preferred_element_type=F32)


def _nt(a, b):
    return lax.dot_general(a, b, (((1,), (1,)), ((), ())), preferred_element_type=F32)


def _tn(a, b):
    return lax.dot_general(a, b, (((0,), (0,)), ((), ())), preferred_element_type=F32)


def _rms(x):
    r = lax.rsqrt(jnp.mean(x * x, axis=-1, keepdims=True) + EPS)
    return x * r, r


def _rms_bwd(dxn, xhat, r, g):
    dxh = dxn * g
    dx = r * (dxh - xhat * jnp.mean(dxh * xhat, axis=-1, keepdims=True))
    return dx, jnp.sum(dxn * xhat, axis=0, keepdims=True)


def _gelu(x):
    cdf = 0.5 * (1.0 + lax.erf(x * (2.0 ** -0.5)))
    return x * cdf, cdf


def _gelu_grad(x, cdf):
    return cdf + x * jnp.exp(-0.5 * x * x) * ((2.0 * math.pi) ** -0.5)


def _params(sem=None):
    return pltpu.CompilerParams(dimension_semantics=sem, vmem_limit_bytes=VMEM_LIMIT)


def _token_block(t, want):
    return want if t % want == 0 and t > want else GROUP


def _const_spec(shape):
    n = len(shape)
    return pl.BlockSpec(shape, lambda i: (0,) * n)


def _tril():
    return lax.broadcasted_iota(jnp.int32, (GROUP, GROUP), 0) >= lax.broadcasted_iota(jnp.int32, (GROUP, GROUP), 1)


def _shift_rows(x, k, edge):
    tb = x.shape[0]
    r8 = lax.broadcasted_iota(jnp.int32, (8, 1), 0)
    rolled = pltpu.roll(x, k % tb, 0)
    if k > 0:
        top = jnp.where(r8 < k, pltpu.roll(edge, k, 0), rolled[0:8, :])
        return jnp.concatenate([top, rolled[8:, :]], axis=0)
    bottom = jnp.where(r8 >= 8 + k, pltpu.roll(edge, 8 + k, 0), rolled[tb - 8:, :])
    return jnp.concatenate([rolled[:tb - 8, :], bottom], axis=0)


def _in_turns(parts):
    parts = list(parts)
    while parts:
        for p in list(parts):
            try:
                next(p)
            except StopIteration:
                parts.remove(p)


class _Hosted(NamedTuple):
    operands: tuple
    aliased: bool
    out_shapes: tuple
    sems: tuple
    stages: tuple


def _hosted_results(hosted):
    if hosted.aliased:
        return [jax.ShapeDtypeStruct(o.shape, o.dtype) for o in hosted.operands]
    return list(hosted.out_shapes)


def _call_hosting(main_body, hosted, *, name, steps, in_specs, out_specs, out_shape, scratch_shapes, operands, aliases=None, middle_at=0.75):
    grid = steps if isinstance(steps, tuple) else (steps,)
    semantics = ("arbitrary",) * len(grid)
    hosted = [hs for hs in (hosted or ()) if hs is not None]
    if not hosted:
        outs = pl.pallas_call(main_body, name=name, grid=grid, in_specs=in_specs, out_specs=out_specs, out_shape=out_shape,
                              scratch_shapes=scratch_shapes, input_output_aliases=aliases or {}, compiler_params=_params(semantics))(*operands)
        return outs, ()
    n_in, n_out, n_sc = len(in_specs), len(out_specs), len(scratch_shapes)
    shapes = [_hosted_results(hs) for hs in hosted]
    aliases, in_at, out_at = dict(aliases or {}), n_in, n_out
    for hs, sh in zip(hosted, shapes):
        if hs.aliased:
            aliases.update({in_at + i: out_at + i for i in range(len(hs.operands))})
        in_at += len(hs.operands)
        out_at += len(sh)

    def body(*refs):
        at = [0]

        def take(n):
            at[0] += n
            return refs[at[0] - n:at[0]]

        ins = take(n_in)
        h_in = [take(len(hs.operands)) for hs in hosted]
        outs = take(n_out)
        h_out = [take(len(sh)) for sh in shapes]
        scratch = take(n_sc)
        h_sems = [take(len(hs.sems)) for hs in hosted]
        ids = [pl.program_id(a) for a in range(len(grid))]

        def at_step(where):
            lead, rest = where
            ok = ids[0] == lead
            for a in range(1, len(grid)):
                ok = jnp.logical_and(ok, ids[a] == (grid[a] - 1 if rest else 0))
            return ok

        def run(stage):
            for hs, a, b, c in zip(hosted, h_in, h_out, h_sems):
                if hs.stages[stage] is not None:
                    hs.stages[stage](a, b, c)

        @pl.when(at_step((0, 0)))
        def _():
            run(0)

        if any(hs.stages[1] is not None for hs in hosted):
            @pl.when(at_step((min(int(middle_at * grid[0]), grid[0] - 1), 0)))
            def _():
                run(1)

        main_body(*ins, *outs, *scratch)

        @pl.when(at_step((grid[0] - 1, -1)))
        def _():
            run(2)

    flat = lambda lists: [x for xs in lists for x in xs]
    outs = pl.pallas_call(
        body, name=name, grid=grid, in_specs=list(in_specs) + [HB] * (in_at - n_in), out_specs=list(out_specs) + [HB] * (out_at - n_out),
        out_shape=list(out_shape) + flat(shapes), scratch_shapes=list(scratch_shapes) + flat(hs.sems for hs in hosted),
        input_output_aliases=aliases, compiler_params=_params(semantics),
    )(*operands, *flat(hs.operands for hs in hosted))
    results, at = [], n_out
    for sh in shapes:
        results.append(outs[at:at + len(sh)])
        at += len(sh)
    return outs[:n_out], results


def _run_hosted(hosted, name):
    nh = len(hosted.operands)
    h_shapes = _hosted_results(hosted)

    def body(*refs):
        h_in, h_out, h_sems = refs[:nh], refs[nh:nh + len(h_shapes)], refs[nh + len(h_shapes):]
        for stage in hosted.stages:
            if stage is not None:
                stage(h_in, h_out, h_sems)

    return pl.pallas_call(
        body, name=name, in_specs=[HB] * nh, out_specs=[HB] * len(h_shapes), out_shape=h_shapes, scratch_shapes=list(hosted.sems),
        input_output_aliases={i: i for i in range(nh)} if hosted.aliased else {},
        compiler_params=pltpu.CompilerParams(has_side_effects=True),
    )(*hosted.operands)


def _window_sums(e, win, back):
    n = e.shape[0]
    k = 1
    while k < win:
        e = e + pltpu.roll(e, k if back else n - k, 0)
        k *= 2
    return e


def _pool_diff(prev, p, t0, gi, win):
    sl = slice(gi * GROUP, (gi + 1) * GROUP)
    tb = p.shape[0]
    s = _window_sums(jnp.concatenate([prev[:, sl], p[:, sl]], axis=0), win, True)[HALO:, :]
    tglob = t0 + lax.broadcasted_iota(jnp.int32, (tb, 1), 0)
    cnt = jnp.minimum(tglob + 1, win).astype(F32)
    return s / cnt - p[:, sl], cnt


def _layernorm(v):
    xc = v - jnp.mean(v, axis=-1, keepdims=True)
    rstd = lax.rsqrt(jnp.mean(xc * xc, axis=-1, keepdims=True) + EPS)
    return xc * rstd, rstd


def _mixer_fwd(h, g, w_in, pool_w, pool_scale, sgu_g, sgu_w, sgu_bt, w_out, hosted=None):
    t, d = h.shape
    pw = pool_w.shape[0] * GROUP
    sw = sgu_w.shape[0] * GROUP
    tb = _token_block(t, 512)

    def body(h_ref, g_ref, win_ref, pw_ref, ps_ref, sg_ref, sw_ref, sbt_ref, wout_ref, h1_ref, proj_ref, xn_ref, mix_ref, pext):
        i = pl.program_id(0)

        @pl.when(i == 0)
        def _():
            pext[...] = jnp.zeros((HALO, pw), F32)

        x = h_ref[...]
        xhat, _ = _rms(x)
        xn = (xhat * g_ref[...]).astype(_MXU)
        xn_ref[...] = xn
        proj = _nn(xn, win_ref[...])
        proj_ref[...] = proj
        p = proj[:, :pw]
        prev = pext[...]
        for gi, win in enumerate(WINDOWS):
            sl = slice(gi * GROUP, (gi + 1) * GROUP)
            dg, _ = _pool_diff(prev, p, i * tb, gi, win)
            e = _nn(dg.astype(_MXU), pw_ref[gi].astype(_MXU))
            mix_ref[:, sl] = (e * ps_ref[:, sl]).astype(_MXU)
        pext[...] = p[tb - HALO:tb, :]
        uv, _ = _gelu(proj[:, pw:])
        u = uv[:, :sw]
        vhat, _ = _layernorm(uv[:, sw:])
        vn = (vhat * sg_ref[...]).astype(_MXU)
        mask = _tril()
        chunks = [slice(n * GROUP, (n + 1) * GROUP) for n in range(tb // GROUP)]
        for hh in range(sw // GROUP):
            wm = jnp.where(mask, sw_ref[hh], 0.0).astype(_MXU)
            cols = slice(hh * GROUP, (hh + 1) * GROUP)
            z = _nn(wm, jnp.concatenate([vn[rows, cols] for rows in chunks], axis=1))
            for n, rows in enumerate(chunks):
                mix_ref[rows, pw + hh * GROUP:pw + (hh + 1) * GROUP] = (u[rows, cols] * (z[:, chunks[n]] + sbt_ref[hh])).astype(_MXU)
        h1_ref[...] = x + _nn(mix_ref[...], wout_ref[...])

    blk = lambda w: pl.BlockSpec((tb, w), lambda i: (i, 0))
    return _call_hosting(
        body, hosted, name="mixer_fwd", steps=t // tb,
        in_specs=[blk(d), VM, VM, VM, VM, VM, VM, VM, VM],
        out_specs=[blk(d), blk(w_in.shape[1]), blk(d), blk(d)],
        out_shape=[jax.ShapeDtypeStruct((t, d), F32), jax.ShapeDtypeStruct((t, w_in.shape[1]), F32),
                   jax.ShapeDtypeStruct((t, d), _MXU), jax.ShapeDtypeStruct((t, d), _MXU)],
        scratch_shapes=[pltpu.VMEM((HALO, pw), F32)],
        operands=(h, g, w_in, pool_w, pool_scale, sgu_g, sgu_w, sgu_bt, w_out), middle_at=LATE_MIDDLE)


def _mixer_bwd(dh1, h, proj, g, w_in, pool_w, pool_scale, sgu_g, sgu_w, sgu_bt, w_out, hosted=None):
    t, d = h.shape
    ng, nh = pool_w.shape[0], sgu_w.shape[0]
    pw, sw = ng * GROUP, nh * GROUP
    tb = _token_block(t, 512)
    nb = t // tb
    n_parts = 2 if tb % (2 * GROUP) == 0 else 1
    pt = tb // n_parts

    def body(dh1_ref, h_ref, proj_ref, halo_ref, g_ref, win_ref, pw_ref, ps_ref, sg_ref, sw_ref, sbt_ref, wout_ref,
             dh_ref, dproj_ref, gg_ref, gpw_ref, gps_ref, gsg_ref, gsw_ref, gsbt_ref, dext, duv):
        i = pl.program_id(0)
        blk = nb - 1 - i

        @pl.when(i == 0)
        def _():
            for r in (gg_ref, gpw_ref, gps_ref, gsg_ref, gsw_ref, gsbt_ref, dext):
                r[...] = jnp.zeros(r.shape, F32)

        mask = _tril()

        def part(at):
            rows = slice(at, at + pt)
            dh1v = dh1_ref[rows, :]
            dmix = _nt(dh1v.astype(_MXU), wout_ref[...])
            yield
            proj_v = proj_ref[rows, :]
            p = proj_v[:, :pw]
            prev = jnp.where(blk == 0, 0.0, halo_ref[...]) if at == 0 else proj_ref[at - HALO:at, 0:pw]
            for gi, win in enumerate(WINDOWS):
                sl = slice(gi * GROUP, (gi + 1) * GROUP)
                dg, cnt = _pool_diff(prev, p, blk * tb + at, gi, win)
                dgm = dg.astype(_MXU)
                pwm = pw_ref[gi].astype(_MXU)
                e = _nn(dgm, pwm)
                dy = dmix[:, sl]
                gps_ref[:, sl] += jnp.sum(dy * e, axis=0, keepdims=True)
                de = (dy * ps_ref[:, sl]).astype(_MXU)
                gpw_ref[gi] += _tn(dgm, de)
                dd = _nt(de, pwm)
                ddc = dd / cnt
                acc = _window_sums(jnp.concatenate([ddc, dext[:, sl]], axis=0), win, False)[:pt, :]
                dext[:, sl] = ddc[0:HALO, :]
                dproj_ref[rows, sl] = (acc - dd).astype(_MXU)
            yield
            pre = proj_v[:, pw:]
            uv, cdf = _gelu(pre)
            u = uv[:, :sw]
            vhat, rstd = _layernorm(uv[:, sw:])
            vn = (vhat * sg_ref[...]).astype(_MXU)
            chunks = [slice(n * GROUP, (n + 1) * GROUP) for n in range(pt // GROUP)]
            side_by_side = lambda a, cols: jnp.concatenate([a[c, cols] for c in chunks], axis=1)
            for hh in range(nh):
                wm = jnp.where(mask, sw_ref[hh], 0.0).astype(_MXU)
                cols = slice(hh * GROUP, (hh + 1) * GROUP)
                vs = side_by_side(vn, cols)
                z = _nn(wm, vs)
                dy = side_by_side(dmix, slice(pw + hh * GROUP, pw + (hh + 1) * GROUP))
                dz = dy * side_by_side(u, cols)
                dzm = dz.astype(_MXU)
                dvs = _tn(wm, dzm)
                gsw_ref[hh] += jnp.where(mask, _nt(dzm, vs), 0.0)
                gb = jnp.zeros((GROUP, GROUP), F32)
                for n, c in enumerate(chunks):
                    gb = gb + dz[:, c]
                    duv[at + n * GROUP:at + (n + 1) * GROUP, cols] = dy[:, c] * (z[:, c] + sbt_ref[hh])
                    duv[at + n * GROUP:at + (n + 1) * GROUP, sw + hh * GROUP:sw + (hh + 1) * GROUP] = dvs[:, c]
                gsbt_ref[hh] += gb
            yield
            dvn = duv[rows, sw:]
            gsg_ref[...] += jnp.sum(dvn * vhat, axis=0, keepdims=True)
            dxh = dvn * sg_ref[...]
            dv = rstd * (dxh - jnp.mean(dxh, axis=-1, keepdims=True) - vhat * jnp.mean(dxh * vhat, axis=-1, keepdims=True))
            gp = _gelu_grad(pre, cdf)
            dproj_ref[rows, pw:pw + sw] = (duv[rows, :sw] * gp[:, :sw]).astype(_MXU)
            dproj_ref[rows, pw + sw:] = (dv * gp[:, sw:]).astype(_MXU)
            dxn = _nt(dproj_ref[rows, :], win_ref[...])
            yield
            xhat, r = _rms(h_ref[rows, :])
            dx, gg = _rms_bwd(dxn, xhat, r, g_ref[...])
            gg_ref[...] += gg
            dh_ref[rows, :] = dh1v + dx

        _in_turns([part(at) for at in reversed(range(0, tb, pt))])

    rev = lambda w: pl.BlockSpec((tb, w), lambda i: (nb - 1 - i, 0))
    halo = pl.BlockSpec((HALO, pw), lambda i: (jnp.maximum((nb - 1 - i) * (tb // HALO) - 1, 0), 0))
    small = [(1, d), (ng, GROUP, GROUP), (1, pw), (1, sw), (nh, GROUP, GROUP), (nh, GROUP, GROUP)]
    return _call_hosting(
        body, hosted, name="mixer_bwd", steps=nb,
        in_specs=[rev(d), rev(d), rev(proj.shape[1]), halo, VM, VM, VM, VM, VM, VM, VM, VM],
        out_specs=[rev(d), rev(proj.shape[1])] + [_const_spec(s) for s in small],
        out_shape=[jax.ShapeDtypeStruct((t, d), F32), jax.ShapeDtypeStruct(proj.shape, _MXU)]
        + [jax.ShapeDtypeStruct(s, F32) for s in small],
        scratch_shapes=[pltpu.VMEM((HALO, pw), F32), pltpu.VMEM((tb, 2 * sw), F32)],
        operands=(dh1, h, proj, proj, g, w_in, pool_w, pool_scale, sgu_g, sgu_w, sgu_bt, w_out))


def _kv_fwd(mem, gm, wk, wv):
    n, d = mem.shape

    def body(mem_ref, gm_ref, wk_ref, wv_ref, k_ref, v_ref, memn_ref):
        xhat, _ = _rms(mem_ref[...])
        memn = (xhat * gm_ref[...]).astype(_MXU)
        memn_ref[...] = memn
        k_ref[...] = _nn(memn, wk_ref[...]).astype(_MXU)
        v_ref[...] = _nn(memn, wv_ref[...]).astype(_MXU)

    return pl.pallas_call(
        body, name="kv_fwd", in_specs=[VM] * 4, out_specs=[VM] * 3,
        out_shape=[jax.ShapeDtypeStruct((n, d), _MXU)] * 3, compiler_params=_params(),
    )(mem, gm, wk, wv)


def _kv_bwd(dk, dv, mem, wk, wv):
    n, d = mem.shape

    def body(dk_ref, dv_ref, mem_ref, wk_ref, wv_ref, ggm_ref):
        dmemn = _nt(dk_ref[...].astype(_MXU), wk_ref[...]) + _nt(dv_ref[...].astype(_MXU), wv_ref[...])
        xhat, _ = _rms(mem_ref[...])
        ggm_ref[...] = jnp.sum(dmemn * xhat, axis=0, keepdims=True)

    return pl.pallas_call(
        body, name="kv_bwd", in_specs=[VM] * 5, out_specs=VM,
        out_shape=jax.ShapeDtypeStruct((1, d), F32), compiler_params=_params(),
    )(dk, dv, mem, wk, wv)


def _softmax(s):
    e = jnp.exp(s - jnp.max(s, axis=-1, keepdims=True))
    return e / jnp.sum(e, axis=-1, keepdims=True)


def _one_ahead(n, issue):
    nxt = issue(0)
    for a in range(n):
        cur = nxt
        if a + 1 < n:
            nxt = issue(a + 1)
        yield a, cur


def _xattn_fwd(h, g, wq, k, v, wo, hosted=None):
    t, d = h.shape
    hd = d // N_XHEADS
    scale = hd ** -0.5
    tb = _token_block(t, 512)

    def body(h_ref, g_ref, wq_ref, k_ref, v_ref, wo_ref, h2_ref, q_ref, o_ref, xn_ref):
        x = h_ref[...]
        xhat, _ = _rms(x)
        xn = (xhat * g_ref[...]).astype(_MXU)
        xn_ref[...] = xn
        qm = _nn(xn, wq_ref[...]).astype(_MXU)
        q_ref[...] = qm
        heads = [slice(a * hd, (a + 1) * hd) for a in range(N_XHEADS)]
        for a, s in _one_ahead(N_XHEADS, lambda a: _nt(qm[:, heads[a]], k_ref[:, heads[a]]) * scale):
            o_ref[:, heads[a]] = _nn(_softmax(s).astype(_MXU), v_ref[:, heads[a]]).astype(_MXU)
        h2_ref[...] = x + _nn(o_ref[...], wo_ref[...])

    blk = pl.BlockSpec((tb, d), lambda i: (i, 0))
    return _call_hosting(
        body, hosted, name="xattn_fwd", steps=t // tb,
        in_specs=[blk, VM, VM, VM, VM, VM], out_specs=[blk] * 4,
        out_shape=[jax.ShapeDtypeStruct((t, d), F32)] + [jax.ShapeDtypeStruct((t, d), _MXU)] * 3,
        scratch_shapes=[], operands=(h, g, wq, k, v, wo), middle_at=LATE_MIDDLE)


def _xattn_bwd(dh2, h, q, g, wq, k, v, wo, hosted=None):
    t, d = h.shape
    n = k.shape[0]
    hd = d // N_XHEADS
    scale = hd ** -0.5
    tb = _token_block(t, 512)
    pt = tb // 2 if tb % (2 * GROUP) == 0 else tb

    def body(dh2_ref, h_ref, q_ref, g_ref, wq_ref, k_ref, v_ref, wo_ref, dh_ref, dq_ref, dk_ref, dv_ref, gg_ref):
        @pl.when(pl.program_id(0) == 0)
        def _():
            for r in (dk_ref, dv_ref, gg_ref):
                r[...] = jnp.zeros(r.shape, F32)

        heads = [slice(a * hd, (a + 1) * hd) for a in range(N_XHEADS)]

        def part(at):
            rows = slice(at, at + pt)
            dh2v = dh2_ref[rows, :]
            dom = _nt(dh2v.astype(_MXU), wo_ref[...]).astype(_MXU)
            yield
            issue = lambda a: (_nt(q_ref[rows, heads[a]], k_ref[:, heads[a]]) * scale, _nt(dom[:, heads[a]], v_ref[:, heads[a]]))
            for a, (s, dpr) in _one_ahead(N_XHEADS, issue):
                sl = heads[a]
                pr = _softmax(s)
                dv_ref[:, sl] += _tn(pr.astype(_MXU), dom[:, sl])
                ds = (pr * (dpr - jnp.sum(dpr * pr, axis=-1, keepdims=True)) * scale).astype(_MXU)
                dq_ref[rows, sl] = _nn(ds, k_ref[:, sl]).astype(_MXU)
                dk_ref[:, sl] += _tn(ds, q_ref[rows, sl])
                yield
            dxn = _nt(dq_ref[rows, :], wq_ref[...])
            yield
            xhat, r = _rms(h_ref[rows, :])
            dx, gg = _rms_bwd(dxn, xhat, r, g_ref[...])
            gg_ref[...] += gg
            dh_ref[rows, :] = dh2v + dx

        _in_turns([part(at) for at in range(0, tb, pt)])

    blk = pl.BlockSpec((tb, d), lambda i: (i, 0))
    return _call_hosting(
        body, hosted, name="xattn_bwd", steps=t // tb,
        in_specs=[blk, blk, blk, VM, VM, VM, VM, VM],
        out_specs=[blk, blk, _const_spec((n, d)), _const_spec((n, d)), _const_spec((1, d))],
        out_shape=[jax.ShapeDtypeStruct((t, d), F32), jax.ShapeDtypeStruct((t, d), _MXU),
                   jax.ShapeDtypeStruct((n, d), F32), jax.ShapeDtypeStruct((n, d), F32), jax.ShapeDtypeStruct((1, d), F32)],
        scratch_shapes=[], operands=(dh2, h, q, g, wq, k, v, wo))


def _ffn_fwd(h, g, w_up, conv_w, conv_b, w_down, hosted=None, head=None):
    t, d = h.shape
    f = w_down.shape[0]
    ft = FF_TILE
    tb = _token_block(t, 256)

    def body(h_ref, g_ref, wup_ref, cw_ref, cb_ref, wdown_ref, *rest):
        if head is None:
            h3_ref, hh_ref, hc_ref, ext, carry, act_sc = rest
        else:
            gf_ref, tgt_ref, h3_ref, hh_ref, hc_ref, loss_ref, ggf_ref, ext, carry, act_sc = rest

        @pl.when(pl.program_id(0) == 0)
        def _():
            carry[...] = jnp.zeros(carry.shape, F32)
            if head is not None:
                loss_ref[...] = jnp.zeros(loss_ref.shape, F32)
                ggf_ref[...] = jnp.zeros(ggf_ref.shape, F32)

        x = h_ref[...]
        xhat, _ = _rms(x)
        xn = (xhat * g_ref[...]).astype(_MXU)
        acc = jnp.zeros((tb, d), F32)
        up = lambda j: [_nn(xn, wup_ref[:, off:off + ft]) for off in (j * ft, f + j * ft)]
        up_next = up(0)
        for j in range(f // ft):
            hc = []
            up_cur = up_next
            if j + 1 < f // ft:
                up_next = up(j + 1)
            for part, off in enumerate((j * ft, f + j * ft)):
                cols = slice(off, off + ft)
                cur = up_cur[part]
                hh_ref[:, cols] = cur.astype(_MXU)
                ext[part, 0:8, :] = carry[:, cols]
                ext[part, 8:8 + tb, :] = cur
                carry[:, cols] = cur[tb - 8:tb, :]
                hc.append(cb_ref[:, cols] + cw_ref[0:1, cols] * ext[part, 6:6 + tb, :]
                          + cw_ref[1:2, cols] * ext[part, 7:7 + tb, :] + cw_ref[2:3, cols] * cur)
                hc_ref[:, cols] = hc[part].astype(_MXU)
            at = j % DOWN_TILES
            act_sc[:, at * ft:(at + 1) * ft] = (hc[0] * jax.nn.sigmoid(hc[0]) * hc[1]).astype(_MXU)
            if at + 1 == DOWN_TILES or j + 1 == f // ft:
                acc = acc + _nn(act_sc[:, 0:(at + 1) * ft], wdown_ref[(j - at) * ft:(j + 1) * ft, :])
        if head is None:
            h3_ref[...] = x + acc
        else:
            yhat, r = _rms(x + acc)
            err = yhat * gf_ref[...] - tgt_ref[...]
            loss_ref[...] += 0.5 * jnp.sum(jnp.sum(err * err, axis=-1, keepdims=True), axis=0, keepdims=True) / d
            dx, gg = _rms_bwd(err / d, yhat, r, gf_ref[...])
            ggf_ref[...] += gg
            h3_ref[...] = dx

    blk = lambda w: pl.BlockSpec((tb, w), lambda i: (i, 0))
    in_specs = [blk(d), VM, VM, VM, VM, VM]
    out_specs = [blk(d), blk(2 * f), blk(2 * f)]
    out_shape = [jax.ShapeDtypeStruct((t, d), F32), jax.ShapeDtypeStruct((t, 2 * f), _MXU), jax.ShapeDtypeStruct((t, 2 * f), _MXU)]
    operands = (h, g, w_up, conv_w, conv_b, w_down)
    if head is not None:
        in_specs += [VM, blk(d)]
        out_specs += [_const_spec((1, 1)), _const_spec((1, d))]
        out_shape += [jax.ShapeDtypeStruct((1, 1), F32), jax.ShapeDtypeStruct((1, d), F32)]
        operands += tuple(head)
    return _call_hosting(
        body, hosted, name="ffn_fwd", steps=t // tb, in_specs=in_specs, out_specs=out_specs, out_shape=out_shape,
        scratch_shapes=[pltpu.VMEM((2, 8 + tb, ft), F32), pltpu.VMEM((8, 2 * f), F32), pltpu.VMEM((tb, DOWN_TILES * ft), _MXU)],
        operands=operands)


def _ffn_bwd(dh3, h, hh, hc, g, w_up, conv_w, w_down, hosted=None):
    t, d = h.shape
    f = w_down.shape[0]
    ft = FF_TILE
    tb = _token_block(t, 256)
    nb = t // tb

    def body(dh3_ref, h_ref, hh_ref, hc_ref, g_ref, wup_ref, cw_ref, wdown_ref,
             dh_ref, dhh_ref, act_ref, xn_ref, gcw_ref, gcb_ref, gg_ref, dcarry):
        @pl.when(pl.program_id(0) == 0)
        def _():
            for r in (gcw_ref, gcb_ref, gg_ref, dcarry):
                r[...] = jnp.zeros(r.shape, F32)

        dh3v = dh3_ref[...]
        dhm = dh3v.astype(_MXU)
        dxn = jnp.zeros((tb, d), F32)
        dact_next = _nt(dhm, wdown_ref[0:ft, :])
        for j in range(f // ft):
            dact = dact_next
            if j + 1 < f // ft:
                dact_next = _nt(dhm, wdown_ref[(j + 1) * ft:(j + 2) * ft, :])
            gate = hc_ref[:, j * ft:(j + 1) * ft].astype(F32)
            val = hc_ref[:, f + j * ft:f + (j + 1) * ft].astype(F32)
            sg = jax.nn.sigmoid(gate)
            silu = gate * sg
            act_ref[:, j * ft:(j + 1) * ft] = (silu * val).astype(_MXU)
            dhc = (dact * val * sg * (1.0 + gate * (1.0 - sg)), dact * silu)
            for part, off in enumerate((j * ft, f + j * ft)):
                cols = slice(off, off + ft)
                dc = dhc[part]
                c0 = hh_ref[:, cols].astype(F32)
                after = dcarry[:, cols]
                ahead1 = _shift_rows(dc, -1, after)
                ahead2 = _shift_rows(dc, -2, after)
                dcarry[:, cols] = dc[0:8, :]
                gcb_ref[:, cols] += jnp.sum(dc, axis=0, keepdims=True)
                gcw_ref[0:1, cols] += jnp.sum(ahead2 * c0, axis=0, keepdims=True)
                gcw_ref[1:2, cols] += jnp.sum(ahead1 * c0, axis=0, keepdims=True)
                gcw_ref[2:3, cols] += jnp.sum(dc * c0, axis=0, keepdims=True)
                dhh = (cw_ref[2:3, cols] * dc + cw_ref[1:2, cols] * ahead1 + cw_ref[0:1, cols] * ahead2).astype(_MXU)
                dhh_ref[:, cols] = dhh
                dxn = dxn + _nt(dhh, wup_ref[:, cols])
        xhat, r = _rms(h_ref[...])
        xn_ref[...] = (xhat * g_ref[...]).astype(_MXU)
        dx, gg = _rms_bwd(dxn, xhat, r, g_ref[...])
        gg_ref[...] += gg
        dh_ref[...] = dh3v + dx

    rev = lambda w: pl.BlockSpec((tb, w), lambda i: (nb - 1 - i, 0))
    return _call_hosting(
        body, hosted, name="ffn_bwd", steps=nb,
        in_specs=[rev(d), rev(d), rev(2 * f), rev(2 * f), VM, VM, VM, VM],
        out_specs=[rev(d), rev(2 * f), rev(f), rev(d), _const_spec((3, 2 * f)), _const_spec((1, 2 * f)), _const_spec((1, d))],
        out_shape=[jax.ShapeDtypeStruct((t, d), F32), jax.ShapeDtypeStruct((t, 2 * f), _MXU), jax.ShapeDtypeStruct((t, f), _MXU),
                   jax.ShapeDtypeStruct((t, d), _MXU),
                   jax.ShapeDtypeStruct((3, 2 * f), F32), jax.ShapeDtypeStruct((1, 2 * f), F32), jax.ShapeDtypeStruct((1, d), F32)],
        scratch_shapes=[pltpu.VMEM((8, 2 * f), F32)],
        operands=(dh3, h, hh, hc, g, w_up, conv_w, w_down))


def _largest_tile(n, cap, mult=128):
    best = None
    for c in range(mult, min(n, cap) + 1, mult):
        if n % c == 0:
            best = c
    return best if best is not None else n


def _grad_matmul(a, b, name, layer, n_layers, into=None, hosted=None):
    t, m = a.shape
    n = b.shape[1]
    tm, tn, tk = _largest_tile(m, 1408), _largest_tile(n, 1536), _largest_tile(t, 2048)
    nk = t // tk

    def body(a_ref, b_ref, *rest):
        o_ref = rest[-1]

        @pl.when(pl.program_id(2) == 0)
        def _():
            o_ref[...] = jnp.zeros(o_ref.shape, F32)

        o_ref[...] += _tn(a_ref[...].astype(_MXU), b_ref[...].astype(_MXU))

    in_specs = [pl.BlockSpec((tk, tm), lambda i, j, k: (k, i)), pl.BlockSpec((tk, tn), lambda i, j, k: (k, j))]
    operands = (a, b)
    aliases = {}
    if into is not None:
        in_specs.append(pl.BlockSpec(memory_space=pl.ANY))
        operands = (a, b, into)
        aliases = {2: 0}
    (out,), got = _call_hosting(
        body, hosted, name=name, steps=(m // tm, n // tn, nk), in_specs=in_specs,
        out_specs=[pl.BlockSpec((None, tm, tn), lambda i, j, k: (layer, i, j))],
        out_shape=[jax.ShapeDtypeStruct((n_layers, m, n), F32)], scratch_shapes=[], operands=operands, aliases=aliases)
    return out, got


def _adamw_math(w, g, m, v):
    m = ADAM_B1 * m + (1.0 - ADAM_B1) * g
    v = ADAM_B2 * v + (1.0 - ADAM_B2) * (g * g)
    m_hat = m / (1.0 - ADAM_B1 ** ADAM_STEP)
    v_hat = v / (1.0 - ADAM_B2 ** ADAM_STEP)
    return -ADAM_LR * (m_hat / (jnp.sqrt(v_hat) + ADAM_EPS) + ADAM_WD * w), m, v


def _row_block(rows, cols, max_bytes=1 << 20, mult=16):
    best = None
    for r in range(mult, rows + 1, mult):
        if rows % r == 0 and r * cols * 4 <= max_bytes:
            best = r
    return best if best is not None else rows


def _adamw_big(ws, gs, ms, vs, name):
    n = len(ws)
    shape = ws[0].shape
    cols = shape[-1]
    flat = lambda a: a.reshape(-1, cols)
    rows = flat(ws[0]).shape[0]
    rb = _row_block(rows, cols, (3 << 20) // n)

    def body(*refs):
        for a in range(n):
            w_ref, g_ref, m_ref, v_ref = (refs[s * n + a] for s in range(4))
            go_ref, d_ref, nm_ref, nv_ref = (refs[(4 + s) * n + a] for s in range(4))
            g = g_ref[...]
            go_ref[...] = g
            d_ref[...], nm_ref[...], nv_ref[...] = _adamw_math(w_ref[...], g, m_ref[...], v_ref[...])

    blk = pl.BlockSpec((rb, cols), lambda i: (i, 0))
    outs = pl.pallas_call(
        body, name=name, grid=(rows // rb,), in_specs=[blk] * (4 * n), out_specs=[blk] * (4 * n),
        out_shape=[jax.ShapeDtypeStruct((rows, cols), F32)] * (4 * n), compiler_params=_params(("parallel",)),
    )(*[flat(a) for group in (ws, gs, ms, vs) for a in group])
    return [[outs[s * n + a].reshape(shape) for s in range(4)] for a in range(n)]


def _adamw_small(ws, gs, ms, vs):
    n = len(ws)

    def body(*refs):
        for a in range(n):
            w_ref, g_ref, m_ref, v_ref = (refs[s * n + a] for s in range(4))
            d_ref, nm_ref, nv_ref = (refs[(4 + s) * n + a] for s in range(3))
            d_ref[...], nm_ref[...], nv_ref[...] = _adamw_math(w_ref[...], g_ref[...], m_ref[...], v_ref[...])

    outs = pl.pallas_call(
        body, name="adamw_small", in_specs=[VM] * (4 * n), out_specs=[VM] * (3 * n),
        out_shape=[jax.ShapeDtypeStruct(w.shape, F32) for w in ws] * 3, compiler_params=_params(),
    )(*ws, *gs, *ms, *vs)
    return outs[:n], outs[n:2 * n], outs[2 * n:]


def _place():
    x, y, c = lax.axis_index("x"), lax.axis_index("y"), lax.axis_index("c")
    chips = [(1 - x, y), (x, 1 - y), (1 - x, 1 - y)]
    return x, y, c, chips


def _rows(start, size, mult=16):
    return pl.ds(pl.multiple_of(start, mult), size)


def _full_window(ref, axis, chip, half=None):
    r, c = ref.shape
    if axis == 0:
        rs = r // 4
        if half is None:
            return ref.at[_rows(chip * rs, rs), :]
        return ref.at[_rows(chip * rs + half * (rs // 2), rs // 2), :]
    cs = c // 4
    if half is None:
        return ref.at[:, _rows(chip * cs, cs, 128)]
    return ref.at[_rows(half * (r // 2), r // 2), _rows(chip * cs, cs, 128)]


def _remote(src, dst, send_sem, recv_sem, to):
    return pltpu.make_async_remote_copy(src_ref=src, dst_ref=dst, send_sem=send_sem, recv_sem=recv_sem,
                                        device_id=to, device_id_type=MESH)


def _scalars(*vals):
    return jnp.stack([jnp.asarray(v, jnp.int32) for v in vals])


def _cast_place(shards, axis, chip, name):
    n = len(shards)
    nl, rs, cs = shards[0].shape
    full = (rs * 4, cs) if axis == 0 else (rs, cs * 4)
    rb = _row_block(rs, cs, (4 << 20) // (n * nl))
    nrb = rs // rb

    def body(chip_ref, *refs):
        for a in range(n):
            for l in range(nl):
                refs[n + a * nl + l][...] = refs[a][l].astype(_PAY)

    if axis == 0:
        out_map = lambda i, chip_ref: (chip_ref[0] * nrb + i, 0)
    else:
        out_map = lambda i, chip_ref: (i, chip_ref[0])
    outs = pl.pallas_call(
        body, name=name,
        grid_spec=pltpu.PrefetchScalarGridSpec(
            num_scalar_prefetch=1, grid=(nrb,),
            in_specs=[pl.BlockSpec((nl, rb, cs), lambda i, chip_ref: (0, i, 0))] * n,
            out_specs=[pl.BlockSpec((rb, cs), out_map)] * (n * nl)),
        out_shape=[jax.ShapeDtypeStruct(full, _PAY)] * (n * nl), compiler_params=_params(("parallel",)),
    )(_scalars(chip), *shards)
    return [[outs[a * nl + l] for l in range(nl)] for a in range(n)]


def _hosted_allgather(placed, axes):
    n = len(placed)

    def each(outs, half_of):
        x, y, c, chips = _place()
        for i in range(n):
            for k, chip in enumerate(chips):
                yield i * 3 + k, (*chip, c), (x, y, 1 - c), _full_window(outs[i], axes[i], 2 * x + y, c), \
                    _full_window(outs[i], axes[i], 2 * chip[0] + chip[1], half_of(c))

    def start(_, outs, sems):
        send, recv, _, _ = sems
        for s, peer, _, mine, _ in each(outs, lambda c: c):
            _remote(mine, mine, send.at[s], recv.at[s], peer).start()

    def middle(_, outs, sems):
        send, recv, fsend, frecv = sems
        for s, _, sibling, _, got in each(outs, lambda c: c):
            _remote(got, got, send.at[s], recv.at[s], sibling).wait_recv()
            _remote(got, got, fsend.at[s], frecv.at[s], sibling).start()

    def finish(_, outs, sems):
        send, recv, fsend, frecv = sems
        for s, _, sibling, _, got in each(outs, lambda c: 1 - c):
            _remote(got, got, fsend.at[s], frecv.at[s], sibling).wait_recv()
        for s, peer, sibling, mine, got in each(outs, lambda c: c):
            _remote(mine, mine, send.at[s], recv.at[s], peer).wait_send()
            _remote(got, got, fsend.at[s], frecv.at[s], sibling).wait_send()

    return _Hosted(tuple(placed), True, (), (pltpu.SemaphoreType.DMA((n * 3,)),) * 4, (start, middle, finish))


def _allgather_conv(conv_shard):
    nl, taps, cs = conv_shard.shape

    def body(in_ref, out_ref, send, recv, local):
        x, y, c, chips = _place()
        mine = out_ref.at[:, :, _rows((2 * x + y) * cs, cs, 128)]
        own = pltpu.make_async_copy(in_ref, mine, local)
        own.start()
        sends = [_remote(in_ref, mine, send.at[k], recv.at[k], (*chip, c)) for k, chip in enumerate(chips)]
        for cp in sends:
            cp.start()
        for k, chip in enumerate(chips):
            got = out_ref.at[:, :, _rows((2 * chip[0] + chip[1]) * cs, cs, 128)]
            _remote(got, got, send.at[k], recv.at[k], (*chip, c)).wait_recv()
        for cp in sends:
            cp.wait_send()
        own.wait()

    return pl.pallas_call(
        body, name="allgather_conv", in_specs=[HB], out_specs=HB, out_shape=jax.ShapeDtypeStruct((nl, taps, cs * 4), conv_shard.dtype),
        scratch_shapes=[pltpu.SemaphoreType.DMA((3,)), pltpu.SemaphoreType.DMA((3,)), pltpu.SemaphoreType.DMA],
        compiler_params=pltpu.CompilerParams(has_side_effects=True),
    )(conv_shard)


def _hosted_exchange(grads, axes, layer):
    na = len(grads)
    views = [g.reshape(g.shape[0], 4, 2, g.shape[1] // 8, g.shape[2]) if ax == 0 else g for g, ax in zip(grads, axes)]

    def region(ref, axis, half):
        if axis == 0:
            return ref.at[layer, :, half]
        r = ref.shape[1]
        return ref.at[layer, _rows(half * (r // 2), r // 2), :]

    def copies(ins, land, sems):
        send, recv = sems
        x, y, c, _ = _place()
        return [_remote(region(ins[a], axes[a], 1 - c), land[a], send.at[a], recv.at[a], (x, y, 1 - c)) for a in range(na)]

    def start(ins, land, sems):
        for cp in copies(ins, land, sems):
            cp.start()

    def finish(ins, land, sems):
        for cp in copies(ins, land, sems):
            cp.wait()

    shapes = [(4, g.shape[1] // 8, g.shape[2]) if ax == 0 else (g.shape[1] // 2, g.shape[2]) for g, ax in zip(grads, axes)]
    return _Hosted(tuple(views), False, tuple(jax.ShapeDtypeStruct(s, F32) for s in shapes),
                   (pltpu.SemaphoreType.DMA((na,)),) * 2, (start, None, finish))


def _add_cast(mines, theirs, core, base, name):
    n = len(mines)
    na, nb, cols = theirs[0].shape
    rb = _row_block(nb, cols, (4 << 20) // n)

    def body(core_ref, *refs):
        for a in range(n):
            refs[2 * n + a][...] = (refs[a][...] + refs[n + a][...]).astype(_PAY)

    blk = pl.BlockSpec((None, rb, cols), lambda i, k, core_ref: (i, k, 0))
    return pl.pallas_call(
        body, name=name,
        grid_spec=pltpu.PrefetchScalarGridSpec(
            num_scalar_prefetch=1, grid=(na, nb // rb),
            in_specs=[pl.BlockSpec((None, None, rb, cols), lambda i, k, core_ref: (base + i, core_ref[0], k, 0))] * n + [blk] * n,
            out_specs=[blk] * n),
        out_shape=[jax.ShapeDtypeStruct((na, nb, cols), _PAY)] * n, compiler_params=_params(("parallel", "parallel")),
    )(_scalars(core), *mines, *theirs)


def _piece(ref, axis, chip):
    if axis == 0:
        return ref.at[chip]
    cs = ref.shape[1] // 4
    return ref.at[:, _rows(chip * cs, cs, 128)]


def _hosted_scatter(sums, axes):
    na = len(sums)

    def piece_shape(a):
        if axes[a] == 0:
            return (sums[a].shape[1], sums[a].shape[2])
        return (sums[a].shape[0], sums[a].shape[1] // 4)

    def copies(ins, slots, sems):
        send, recv = sems
        _, _, c, chips = _place()
        return [_remote(_piece(ins[a], axes[a], 2 * chip[0] + chip[1]), slots[a].at[k], send.at[a * 3 + k], recv.at[a * 3 + k], (*chip, c))
                for a in range(na) for k, chip in enumerate(chips)]

    def start(ins, slots, sems):
        for cp in copies(ins, slots, sems):
            cp.start()

    def finish(ins, slots, sems):
        for cp in copies(ins, slots, sems):
            cp.wait()

    return _Hosted(tuple(sums), False, tuple(jax.ShapeDtypeStruct((3,) + piece_shape(a), sums[a].dtype) for a in range(na)),
                   (pltpu.SemaphoreType.DMA((na * 3,)),) * 2, (start, None, finish))


def _sum_slots(sums, slots, axis, chip, core, layer, n_layers, name, into=None):
    _, hr, cs = slots.shape
    rb = _row_block(hr, cs, 4 << 20)

    def body(at_ref, own_ref, s_ref, *rest):
        rest[-1][...] = ((own_ref[...].astype(F32) + s_ref[0].astype(F32)) + s_ref[1].astype(F32)) + s_ref[2].astype(F32)

    if axis == 0:
        own = pl.BlockSpec((None, rb, cs), lambda k, at_ref: (at_ref[0], k, 0))
    else:
        own = pl.BlockSpec((rb, cs), lambda k, at_ref: (k, at_ref[0]))
    in_specs = [own, pl.BlockSpec((3, rb, cs), lambda k, at_ref: (0, k, 0))]
    operands = (sums, slots)
    aliases = {}
    if into is not None:
        in_specs.append(pl.BlockSpec(memory_space=pl.ANY))
        operands = (sums, slots, into)
        aliases = {3: 0}
    return pl.pallas_call(
        body, name=name,
        grid_spec=pltpu.PrefetchScalarGridSpec(
            num_scalar_prefetch=1, grid=(hr // rb,), in_specs=in_specs,
            out_specs=pl.BlockSpec((None, None, rb, cs), lambda k, at_ref: (layer, at_ref[1], k, 0))),
        out_shape=jax.ShapeDtypeStruct((n_layers, 2, hr, cs), F32), input_output_aliases=aliases,
        compiler_params=_params(("parallel",)),
    )(_scalars(chip, core), *operands)


def _hosted_assemble(shards):
    na = len(shards)

    def copies(_, outs, sems):
        send, recv = sems
        x, y, c, _ = _place()
        halves = [outs[a].at[:, _rows(c * (outs[a].shape[1] // 2), outs[a].shape[1] // 2), :] for a in range(na)]
        return [_remote(mine, mine, send.at[a], recv.at[a], (x, y, 1 - c)) for a, mine in enumerate(halves)]

    def start(ins, outs, sems):
        for cp in copies(ins, outs, sems):
            cp.start()

    def finish(ins, outs, sems):
        for cp in copies(ins, outs, sems):
            cp.wait()

    return _Hosted(tuple(shards), True, (), (pltpu.SemaphoreType.DMA((na,)),) * 2, (start, None, finish))


def _allreduce_small(buf, hosted):
    rows, w = buf.shape
    half = rows // 2
    shapes = [_hosted_results(hs) for hs in hosted]
    flat = lambda lists: [x for xs in lists for x in xs]
    aliases, in_at, out_at = {}, 1, 1
    for hs, sh in zip(hosted, shapes):
        if hs.aliased:
            aliases.update({in_at + i: out_at + i for i in range(len(hs.operands))})
        in_at += len(hs.operands)
        out_at += len(sh)

    def body(buf_ref, *refs):
        at = [0]

        def take(n):
            at[0] += n
            return refs[at[0] - n:at[0]]

        h_in = [take(len(hs.operands)) for hs in hosted]
        (out_ref,) = take(1)
        h_out = [take(len(sh)) for sh in shapes]
        land, slots, red, sems_send, sems_recv = take(5)
        h_sems = [take(len(hs.sems)) for hs in hosted]
        for hs, a, b, s in zip(hosted, h_in, h_out, h_sems):
            hs.stages[0](a, b, s)
        x, y, c, chips = _place()
        me = 2 * x + y
        sibling = (x, y, 1 - c)
        first = _remote(buf_ref, land, sems_send.at[0], sems_recv.at[0], sibling)
        first.start()
        first.wait()
        mine = pl.ds(pl.multiple_of(c * half, 8), half)
        slots[me] = buf_ref[mine, :] + land[mine, :]
        sends = []
        for k, chip in enumerate(chips):
            cp = _remote(slots.at[me], slots.at[me], sems_send.at[1 + k], sems_recv.at[1 + k], (*chip, c))
            cp.start()
            sends.append(cp)
        for k, chip in enumerate(chips):
            got = slots.at[2 * chip[0] + chip[1]]
            _remote(got, got, sems_send.at[1 + k], sems_recv.at[1 + k], sibling).wait_recv()
        red[...] = ((slots[0] + slots[1]) + slots[2]) + slots[3]
        out_ref[mine, :] = red[...]
        last = _remote(red, out_ref.at[mine, :], sems_send.at[4], sems_recv.at[4], sibling)
        last.start()
        theirs = out_ref.at[pl.ds(pl.multiple_of((1 - c) * half, 8), half), :]
        _remote(red, theirs, sems_send.at[4], sems_recv.at[4], sibling).wait_recv()
        for cp in sends:
            cp.wait_send()
        last.wait_send()
        for hs, a, b, s in zip(hosted, h_in, h_out, h_sems):
            hs.stages[2](a, b, s)

    outs = pl.pallas_call(
        body, name="allreduce_small", in_specs=[VM] + [HB] * (in_at - 1), out_specs=[VM] + [HB] * (out_at - 1),
        out_shape=[jax.ShapeDtypeStruct((rows, w), F32)] + flat(shapes), input_output_aliases=aliases,
        scratch_shapes=[pltpu.VMEM((rows, w), F32), pltpu.VMEM((4, half, w), F32), pltpu.VMEM((half, w), F32),
                        pltpu.SemaphoreType.DMA((5,)), pltpu.SemaphoreType.DMA((5,))] + flat(hs.sems for hs in hosted),
        compiler_params=pltpu.CompilerParams(has_side_effects=True, vmem_limit_bytes=VMEM_LIMIT),
    )(buf, *flat(hs.operands for hs in hosted))
    results, at = [], 1
    for sh in shapes:
        results.append(outs[at:at + len(sh)])
        at += len(sh)
    return outs[0], results


BIG = ("w_in", "w_out", "wq", "wk", "wv", "wo", "w_up", "w_down")
MIXER, ATTN, MLP = ("w_in", "w_out"), ("wq", "wk", "wv", "wo"), ("w_up", "w_down")
BIG_AXIS = {"w_in": 1, "w_out": 0, "wq": 0, "wk": 0, "wv": 0, "wo": 0, "w_up": 1, "w_down": 0}
SMALL = ("norm_mix_g", "pool_w", "pool_scale", "sgu_g", "sgu_w", "sgu_b", "norm_xattn_g", "mem_norm_g", "norm_ffn_g",
         "conv_w", "conv_b", "final_norm_g")
ORDER = ("norm_mix_g", "w_in", "pool_w", "pool_scale", "sgu_g", "sgu_w", "sgu_b", "w_out", "norm_xattn_g", "mem_norm_g",
         "wq", "wk", "wv", "wo", "norm_ffn_g", "w_up", "conv_w", "conv_b", "w_down", "final_norm_g")
PACK_WIDTH = 512


def kernel(x, mem, norm_mix_g, w_in, pool_w, pool_scale, sgu_g, sgu_w, sgu_b, w_out, norm_xattn_g, mem_norm_g, wq, wk, wv, wo, norm_ffn_g, w_up, conv_w, conv_b, w_down, final_norm_g, loss_target, m_norm_mix_g, m_w_in, m_pool_w, m_pool_scale, m_sgu_g, m_sgu_w, m_sgu_b, m_w_out, m_norm_xattn_g, m_mem_norm_g, m_wq, m_wk, m_wv, m_wo, m_norm_ffn_g, m_w_up, m_conv_w, m_conv_b, m_w_down, m_final_norm_g, v_norm_mix_g, v_w_in, v_pool_w, v_pool_scale, v_sgu_g, v_sgu_w, v_sgu_b, v_w_out, v_norm_xattn_g, v_mem_norm_g, v_wq, v_wk, v_wv, v_wo, v_norm_ffn_g, v_w_up, v_conv_w, v_conv_b, v_w_down, v_final_norm_g):
    given = dict(locals())
    w = {n: given[n] for n in ORDER}
    mom = {n: given["m_" + n] for n in ORDER}
    var = {n: given["v_" + n] for n in ORDER}
    nl = w_in.shape[0]
    xs, mems, tgt = x[0], mem[0], loss_target[0]
    chip = 2 * lax.axis_index("x") + lax.axis_index("y")
    core = lax.axis_index("c")

    axes_of = lambda names: [BIG_AXIS[n] for n in names]
    alike = {}
    for n in BIG:
        alike.setdefault((w[n].shape, BIG_AXIS[n]), []).append(n)
    placed = [{} for _ in range(nl)]
    for (_, axis), names in alike.items():
        for n, per_layer in zip(names, _cast_place([w[n] for n in names], axis, chip, "place_" + names[0])):
            for l in range(nl):
                placed[l][n] = per_layer[l]
    conv_full = _allgather_conv(conv_w)

    def gather(names, l):
        return _hosted_allgather([placed[l][n] for n in names], axes_of(names))

    full = [dict(zip(MIXER, _run_hosted(gather(MIXER, 0), "allgather_weights")))]

    row = lambda a, l: a[l][None, :]
    saved = []
    h = xs
    for l in range(nl):
        fw = full[l]
        sbt = jnp.broadcast_to(sgu_b[l][:, :, None], sgu_w[l].shape)
        (h1, proj, xn1, mix), got = _mixer_fwd(h, row(norm_mix_g, l), fw["w_in"], pool_w[l], row(pool_scale, l), row(sgu_g, l), sgu_w[l], sbt, fw["w_out"],
                                               [gather(ATTN, 0), gather(("w_down",), 0)] if l == 0 else None)
        if l == 0:
            fw.update(zip(ATTN, got[0]))
            fw["w_down"] = got[1][0]
        k, v, memn = _kv_fwd(mems, row(mem_norm_g, l), fw["wk"], fw["wv"])
        (h2, q, o, xn2), got = _xattn_fwd(h1, row(norm_xattn_g, l), fw["wq"], k, v, fw["wo"], [gather(("w_up",), 0)] if l == 0 else None)
        if l == 0:
            fw["w_up"] = got[0][0]
        outs, got = _ffn_fwd(h2, row(norm_ffn_g, l), fw["w_up"], conv_full[l], row(conv_b, l), fw["w_down"],
                             [gather(BIG, l + 1)] if l + 1 < nl else None, None if l + 1 < nl else (final_norm_g[None, :], tgt))
        h3, hh, hc = outs[:3]
        if l + 1 < nl:
            full.append(dict(zip(BIG, got[0])))
        else:
            dh, loss_part, g_final = h3, outs[3], outs[4]
        saved.append(dict(h=h, h1=h1, h2=h2, proj=proj, xn1=xn1, mix=mix, k=k, v=v, memn=memn, q=q, o=o, xn2=xn2, hh=hh, hc=hc, sbt=sbt))
        h = h3


    big_grads = {}
    small_grads = [None] * nl

    def weight_grad(n, a, b, l, hosted=None):
        big_grads[n], got = _grad_matmul(a, b, "grad_" + n, l, nl, big_grads.get(n), hosted)
        return got

    sums, slots = {}, {}

    def exchange(names, l):
        return _hosted_exchange([big_grads[n] for n in names], axes_of(names), l)

    def scatter(names, l):
        return _hosted_scatter([sums[n, l] for n in names], axes_of(names))

    def add_casts(names, theirs, l):
        theirs = dict(zip(names, theirs))
        for group in alike.values():
            group = [n for n in group if n in theirs]
            if not group:
                continue
            gl, gr, gc = big_grads[group[0]].shape
            if BIG_AXIS[group[0]] == 0:
                outs = _add_cast([big_grads[n].reshape(gl * 4, 2, gr // 8, gc) for n in group], [theirs[n] for n in group],
                                 core, l * 4, "grad_chip_sum_" + group[0])
            else:
                outs = [o[0] for o in _add_cast([big_grads[n].reshape(gl, 2, gr // 2, gc) for n in group], [theirs[n][None] for n in group],
                                                core, l, "grad_chip_sum_" + group[0])]
            for n, o in zip(group, outs):
                sums[n, l] = o

    def keep_slots(names, got, l):
        for n, sl in zip(names, got):
            slots[n, l] = sl

    shard_grads = {}

    def shard_halves(names):
        out = []
        for n in names:
            buf = None
            for l in range(nl):
                buf = _sum_slots(sums[n, l], slots[n, l], BIG_AXIS[n], chip, core, l, nl, "grad_sum_" + n, buf)
            out.append(buf.reshape(nl, 2 * buf.shape[2], buf.shape[3]))
        return out

    for l in reversed(range(nl)):
        fw, s = full[l], saved[l]
        above = l + 1 < nl
        dh3 = dh
        (dh2, dhh, act, xn3, g_cw, g_cb, g_nf), got = _ffn_bwd(dh3, s["h2"], s["hh"], s["hc"], row(norm_ffn_g, l), fw["w_up"], conv_full[l], fw["w_down"],
                                                         [exchange(MIXER, l + 1), scatter(ATTN, l + 1)] if above else None)
        if above:
            add_casts(MIXER, got[0], l + 1)
            keep_slots(ATTN, got[1], l + 1)
        weight_grad("w_up", xn3, dhh, l)
        weight_grad("w_down", act, dh3, l)
        (dh1, dq, dk, dv, g_nx), got = _xattn_bwd(dh2, s["h1"], s["q"], row(norm_xattn_g, l), fw["wq"], s["k"], s["v"], fw["wo"],
                                                  [exchange(MLP, l), scatter(MIXER, l + 1) if above else None])
        add_casts(MLP, got[0], l)
        if above:
            keep_slots(MIXER, got[1], l + 1)
        weight_grad("wq", s["xn2"], dq, l)
        weight_grad("wo", s["o"], dh2, l)
        weight_grad("wk", s["memn"], dk, l)
        weight_grad("wv", s["memn"], dv, l)
        g_mn = _kv_bwd(dk, dv, mems, fw["wk"], fw["wv"])
        (dh0, dproj, g_nm, g_pw, g_ps, g_sg, g_sw, g_sbt), got = _mixer_bwd(dh1, s["h"], s["proj"], row(norm_mix_g, l), fw["w_in"], pool_w[l], row(pool_scale, l), row(sgu_g, l), sgu_w[l], s["sbt"], fw["w_out"],
                                                                           [scatter(MLP, l), exchange(ATTN, l)])
        keep_slots(MLP, got[0], l)
        add_casts(ATTN, got[1], l)
        half = len(ATTN) // 2
        got = weight_grad("w_in", s["xn1"], dproj, l, [scatter(ATTN[:half], l), _hosted_assemble(shard_halves(MLP))] if l == 0 else None)
        if l == 0:
            keep_slots(ATTN[:half], got[0], l)
            shard_grads.update(zip(MLP, got[1]))
        got = weight_grad("w_out", s["mix"], dh1, l, [scatter(ATTN[half:], l)] if l == 0 else None)
        if l == 0:
            keep_slots(ATTN[half:], got[0], l)
        small_grads[l] = dict(norm_mix_g=g_nm, pool_w=g_pw, pool_scale=g_ps, sgu_g=g_sg, sgu_w=g_sw, sgu_b=jnp.sum(g_sbt, axis=-1),
                              norm_xattn_g=g_nx, mem_norm_g=g_mn, norm_ffn_g=g_nf, conv_w=g_cw, conv_b=g_cb)
        dh = dh0
    grad_x = dh[None]

    add_casts(MIXER, _run_hosted(exchange(MIXER, 0), "grad_sibling_exchange"), 0)

    layered = [n for n in SMALL if n != "final_norm_g"]
    parts = [small_grads[l][n].reshape(-1, PACK_WIDTH) for n in layered for l in range(nl)]
    parts.append(g_final.reshape(-1, PACK_WIDTH))
    parts.append(jnp.pad(loss_part, ((0, 0), (0, PACK_WIDTH - 1))))
    used = sum(p.shape[0] for p in parts)
    total = -(-used // 16) * 16
    packed, got = _allreduce_small(jnp.concatenate(parts + [jnp.zeros((total - used, PACK_WIDTH), F32)], axis=0),
                                   [scatter(MIXER, 0), _hosted_assemble(shard_halves(ATTN))])
    keep_slots(MIXER, got[0], 0)
    shard_grads.update(zip(ATTN, got[1]))
    shard_grads.update(zip(MIXER, _run_hosted(_hosted_assemble(shard_halves(MIXER)), "grad_sibling_assemble")))

    delta, new_m, new_v, grads = {}, {}, {}, {}
    for names in alike.values():
        outs = _adamw_big([w[n] for n in names], [shard_grads[n] for n in names], [mom[n] for n in names], [var[n] for n in names], "adamw_" + names[0])
        for n, out in zip(names, outs):
            grads[n], delta[n], new_m[n], new_v[n] = out
    at = 0
    for n in layered:
        per_layer = []
        for l in range(nl):
            shape = small_grads[l][n].shape
            nrow = small_grads[l][n].size // PACK_WIDTH
            per_layer.append(packed[at:at + nrow].reshape(shape))
            at += nrow
        g = jnp.stack(per_layer)
        if n == "conv_w":
            cs = conv_w.shape[2]
            g = lax.dynamic_slice_in_dim(g, chip * cs, cs, axis=2)
        grads[n] = g.reshape(w[n].shape)
    grads["final_norm_g"] = packed[at:at + g_final.size // PACK_WIDTH].reshape(final_norm_g.shape)
    at += g_final.size // PACK_WIDTH
    loss = packed[at, 0]

    two_d = lambda a: a.reshape(-1, a.shape[-1])
    ds, nms, nvs = _adamw_small([two_d(w[n]) for n in SMALL], [two_d(grads[n]) for n in SMALL],
                                [two_d(mom[n]) for n in SMALL], [two_d(var[n]) for n in SMALL])
    for n, d_, m_, v_ in zip(SMALL, ds, nms, nvs):
        delta[n], new_m[n], new_v[n] = d_.reshape(w[n].shape), m_.reshape(w[n].shape), v_.reshape(w[n].shape)

    return (loss, grad_x, *[grads[n] for n in ORDER], *[delta[n] for n in ORDER], *[new_m[n] for n in ORDER], *[new_v[n] for n in ORDER])
```
